```python
import math
import jax, jax.numpy as jnp
from jax import lax
import numpy as np

D_MODEL = 1024
BATCH = 8
SEQ = 2048
DEPTH = 2
DEC_BATCH = 128
DEC_SEQ = 8
PAST_LEN = 8192
PAGE_SIZE = 128

D_MIX = D_MODEL
SSD_HEADS = 8
SSD_HEAD_DIM = 64
SSD_WIDTH = SSD_HEADS * SSD_HEAD_DIM
SSD_GROUPS = 2
SSD_STATE = 64
SSD_CONV = 4
SSD_CONV_DIM = SSD_WIDTH + 2 * SSD_GROUPS * SSD_STATE
SSD_CHUNK = 128
GLA_HEADS = 4
GLA_DK = 32
GLA_DV = 64
GLA_WIDTH = GLA_HEADS * GLA_DV
GLA_RANK = 16
GLA_GATE_NORM = 16.0
GLA_CHUNK = 64
SWA_HEADS = 4
SWA_KV_HEADS = 2
SWA_HEAD_DIM = 64
SWA_WIDTH = SWA_HEADS * SWA_HEAD_DIM
SWA_REP = SWA_HEADS // SWA_KV_HEADS
WINDOW = 128
SWA_BUF = min(WINDOW, PAST_LEN)
REL_BUCKETS = 32
REL_MAX_DIST = 128
PLE_DIM = 256
EPS = 1e-6

IN_SIZES = (SSD_WIDTH, SSD_CONV_DIM, SSD_HEADS,
            GLA_HEADS * GLA_DK, GLA_HEADS * GLA_DK, GLA_WIDTH, GLA_WIDTH, GLA_RANK,
            SWA_WIDTH, SWA_KV_HEADS * SWA_HEAD_DIM, SWA_KV_HEADS * SWA_HEAD_DIM, SWA_WIDTH)
D_IN = sum(IN_SIZES)

kernel_name = 'hybrid_ssd_gla_swa_step'


def _split_points():
    return np.cumsum(IN_SIZES)[:-1].tolist()


def rmsnorm(x, w):
    xf = x.astype(jnp.float32)
    y = xf * lax.rsqrt(jnp.mean(xf * xf, axis=-1, keepdims=True) + EPS)
    return (y * w.astype(jnp.float32)).astype(x.dtype)


def _chunk_len(L, c):
    return c if L % c == 0 else L


def causal_conv(xbc, buf, w, b):
    L = xbc.shape[1]
    xp = jnp.concatenate([buf.astype(xbc.dtype), xbc], axis=1)
    out = sum(xp[:, k:k + L] * w[k] for k in range(SSD_CONV)) + b
    return jax.nn.silu(out), xp[:, -(SSD_CONV - 1):]


def ssd_scan(x, dt, a, bm, cm, s0):
    Bsz, L = x.shape[:2]
    Q = _chunk_len(L, SSD_CHUNK)
    nc = L // Q
    G, R, P, N = SSD_GROUPS, SSD_HEADS // SSD_GROUPS, SSD_HEAD_DIM, SSD_STATE
    x = x.reshape(Bsz, nc, Q, G, R, P)
    dt = dt.reshape(Bsz, nc, Q, G, R)
    bm = bm.reshape(Bsz, nc, Q, G, N)
    cm = cm.reshape(Bsz, nc, Q, G, N)
    acum = jnp.cumsum(dt * a.reshape(G, R), axis=2)
    causal = jnp.tril(jnp.ones((Q, Q), dtype=bool))
    seg = acum[:, :, :, None] - acum[:, :, None, :]
    decay = jnp.exp(jnp.where(causal[:, :, None, None], seg, -jnp.inf))
    cb = jnp.einsum('bctgn,bcsgn->bctsg', cm, bm)
    xdt = x * dt[..., None]
    y_intra = jnp.einsum('bctsgr,bcsgrp->bctgrp', cb[..., None] * decay, xdt)
    tail = jnp.exp(acum[:, :, -1:] - acum)
    st = jnp.einsum('bcsgn,bcsgrp->bcgrpn', bm, xdt * tail[..., None])
    chunk_decay = jnp.exp(acum[:, :, -1])

    def step(s, inp):
        dec, stc = inp
        return dec[..., None, None] * s + stc, s

    s_final, s_prev = lax.scan(step, s0.reshape(Bsz, G, R, P, N),
                               (jnp.moveaxis(chunk_decay, 1, 0), jnp.moveaxis(st, 1, 0)))
    s_prev = jnp.moveaxis(s_prev, 0, 1)
    y_inter = jnp.einsum('bctgn,bcgrpn->bctgrp', cm, s_prev) * jnp.exp(acum)[..., None]
    y = (y_intra + y_inter).reshape(Bsz, L, SSD_HEADS, P)
    return y, s_final.reshape(Bsz, SSD_HEADS, P, N)


def gla_scan(q, k, v, g, s0):
    Bsz, L, H, Dk = q.shape
    Dv = v.shape[-1]
    Q = _chunk_len(L, GLA_CHUNK)
    nc = L // Q
    q = q.reshape(Bsz, nc, Q, H, Dk)
    k = k.reshape(Bsz, nc, Q, H, Dk)
    v = v.reshape(Bsz, nc, Q, H, Dv)
    b = jnp.cumsum(g.reshape(Bsz, nc, Q, H, Dk), axis=2)
    qe = q * jnp.exp(b)
    ke = k * jnp.exp(-b)
    causal = jnp.tril(jnp.ones((Q, Q), dtype=bool))
    att = jnp.where(causal, jnp.einsum('bcthd,bcshd->bchts', qe, ke), 0.0)
    o_intra = jnp.einsum('bchts,bcshv->bcthv', att, v)
    kd = k * jnp.exp(b[:, :, -1:] - b)
    U = jnp.einsum('bcshd,bcshv->bchdv', kd, v)
    dec = jnp.exp(b[:, :, -1])

    def step(s, inp):
        d, u = inp
        return d[..., None] * s + u, s

    s_final, s_prev = lax.scan(step, s0, (jnp.moveaxis(dec, 1, 0), jnp.moveaxis(U, 1, 0)))
    s_prev = jnp.moveaxis(s_prev, 0, 1)
    o_inter = jnp.einsum('bcthd,bchdv->bcthv', qe, s_prev)
    return (o_intra + o_inter).reshape(Bsz, L, H, Dv), s_final


def rel_bucket(dist):
    n = jnp.maximum(dist, 0)
    exact = REL_BUCKETS // 2
    nf = jnp.maximum(n, 1).astype(jnp.float32)
    large = exact + (jnp.log(nf / exact) / math.log(REL_MAX_DIST / exact)
                     * (REL_BUCKETS - exact)).astype(jnp.int32)
    large = jnp.minimum(large, REL_BUCKETS - 1)
    return jnp.where(n < exact, n, large)


def swa_attend(q, k, v, dist, valid, rel_bias, sinks):
    G, R = SWA_KV_HEADS, SWA_REP
    Tq, Tk = dist.shape
    logits = jnp.einsum('bnqgrd,bnkgd->bngrqk', q, k).astype(jnp.float32) * (SWA_HEAD_DIM ** -0.5)
    bias = rel_bias.astype(jnp.float32)[rel_bucket(dist)]
    bias = jnp.moveaxis(bias, -1, 0).reshape(G, R, Tq, Tk)
    mask = valid[None, :, None, None] & (dist >= 0) & (dist < WINDOW)
    logits = jnp.where(mask, logits + bias, -jnp.inf)
    sink = jnp.broadcast_to(sinks.astype(jnp.float32).reshape(G, R, 1, 1), logits.shape[:-1] + (1,))
    probs = jax.nn.softmax(jnp.concatenate([logits, sink], axis=-1), axis=-1)[..., :-1]
    return jnp.einsum('bngrqk,bnkgd->bnqgrd', probs.astype(v.dtype), v)


def swa_prompt(q, k, v, rel_bias, sinks):
    Bsz, L = q.shape[:2]
    nb = L // WINDOW
    qb = q.reshape(Bsz, nb, WINDOW, SWA_KV_HEADS, SWA_REP, SWA_HEAD_DIM)
    kb = k.reshape(Bsz, nb, WINDOW, SWA_KV_HEADS, SWA_HEAD_DIM)
    vb = v.reshape(Bsz, nb, WINDOW, SWA_KV_HEADS, SWA_HEAD_DIM)
    pad = ((0, 0), (1, 0), (0, 0), (0, 0), (0, 0))
    k_ext = jnp.concatenate([jnp.pad(kb[:, :-1], pad), kb], axis=2)
    v_ext = jnp.concatenate([jnp.pad(vb[:, :-1], pad), vb], axis=2)
    dist = WINDOW + jnp.arange(WINDOW)[:, None] - jnp.arange(2 * WINDOW)[None, :]
    valid = (jnp.arange(nb)[:, None, None] > 0) | (jnp.arange(2 * WINDOW)[None, None, :] >= WINDOW)
    o = swa_attend(qb, k_ext, v_ext, dist, valid, rel_bias, sinks)
    return o.reshape(Bsz, L, SWA_WIDTH)


def swa_sample(q, k, v, k_buf, v_buf, rel_bias, sinks):
    Bsz, L = q.shape[:2]
    kk = jnp.concatenate([k_buf.astype(k.dtype), k], axis=1)
    vv = jnp.concatenate([v_buf.astype(v.dtype), v], axis=1)
    Tk = kk.shape[1]
    dist = SWA_BUF + jnp.arange(L)[:, None] - jnp.arange(Tk)[None, :]
    valid = jnp.ones((1, 1, Tk), dtype=bool)
    qb = q.reshape(Bsz, 1, L, SWA_KV_HEADS, SWA_REP, SWA_HEAD_DIM)
    o = swa_attend(qb, kk[:, None], vv[:, None], dist, valid, rel_bias, sinks)
    return o.reshape(Bsz, L, SWA_WIDTH), kk[:, -SWA_BUF:], vv[:, -SWA_BUF:]


def layer(h, p, conv_buf, ssm0, gla0, k_buf, v_buf, rel_bias, norm_w, w_in, conv_w, conv_b,
          dt_bias, a_log, d_skip, ssd_norm_w, gla_w_gk, gla_b_gk, gla_norm_w, q_norm_w,
          k_norm_w, sinks, w_out, w_pe, w_pg):
    f32 = jnp.float32
    Bsz, L, _ = h.shape
    u = rmsnorm(h, norm_w)
    proj = u @ w_in
    (z, xbc, dt, gq, gk, gv, gg, glr, sq, sk, sv, sg) = jnp.split(proj, _split_points(), axis=-1)

    xbc_c, conv_new = causal_conv(xbc, conv_buf, conv_w, conv_b)
    xs, bm, cm = jnp.split(xbc_c, [SSD_WIDTH, SSD_WIDTH + SSD_GROUPS * SSD_STATE], axis=-1)
    xs = xs.reshape(Bsz, L, SSD_HEADS, SSD_HEAD_DIM).astype(f32)
    dtv = jax.nn.softplus(dt.astype(f32) + dt_bias.astype(f32))
    a = -jnp.exp(a_log.astype(f32))
    y, ssm_new = ssd_scan(xs, dtv, a,
                          bm.reshape(Bsz, L, SSD_GROUPS, SSD_STATE).astype(f32),
                          cm.reshape(Bsz, L, SSD_GROUPS, SSD_STATE).astype(f32),
                          ssm0.astype(f32))
    y = y + d_skip.astype(f32)[:, None] * xs
    y = (y.reshape(Bsz, L, SSD_WIDTH) * jax.nn.silu(z.astype(f32)))
    y = y.reshape(Bsz, L, SSD_GROUPS, SSD_WIDTH // SSD_GROUPS)
    y_ssd = rmsnorm(y, ssd_norm_w.reshape(SSD_GROUPS, -1)).reshape(Bsz, L, SSD_WIDTH)

    q = gq.reshape(Bsz, L, GLA_HEADS, GLA_DK).astype(f32) * (GLA_DK ** -0.5)
    k = gk.reshape(Bsz, L, GLA_HEADS, GLA_DK).astype(f32)
    v = gv.reshape(Bsz, L, GLA_HEADS, GLA_DV).astype(f32)
    glog = jax.nn.log_sigmoid((glr @ gla_w_gk + gla_b_gk).astype(f32)) / GLA_GATE_NORM
    o, gla_new = gla_scan(q, k, v, glog.reshape(Bsz, L, GLA_HEADS, GLA_DK), gla0.astype(f32))
    y_gla = rmsnorm(o, gla_norm_w).reshape(Bsz, L, GLA_WIDTH) * jax.nn.silu(gg.astype(f32))

    qa = rmsnorm(sq.reshape(Bsz, L, SWA_HEADS, SWA_HEAD_DIM), q_norm_w)
    ka = rmsnorm(sk.reshape(Bsz, L, SWA_KV_HEADS, SWA_HEAD_DIM), k_norm_w)
    va = sv.reshape(Bsz, L, SWA_KV_HEADS, SWA_HEAD_DIM)
    if k_buf is None:
        oa = swa_prompt(qa, ka, va, rel_bias, sinks)
        k_new, v_new = ka[:, -SWA_BUF:], va[:, -SWA_BUF:]
    else:
        oa, k_new, v_new = swa_sample(qa, ka, va, k_buf, v_buf, rel_bias, sinks)
    y_swa = oa.astype(f32) * jax.nn.silu(sg.astype(f32))

    mix = jnp.concatenate([y_ssd.astype(f32), y_gla, y_swa], axis=-1).astype(h.dtype) @ w_out
    h = h + mix
    h = h + jax.nn.sigmoid(h @ w_pg) * (p.astype(h.dtype) @ w_pe)
    dt_ = h.dtype
    return (h, ssm_new.astype(dt_), conv_new.astype(dt_), gla_new.astype(dt_),
            k_new.astype(dt_), v_new.astype(dt_))


def setup_inputs(seed: int = 0) -> dict:
    key = jax.random.key(seed)
    ks = jax.random.split(key, 32)
    f32 = jnp.float32

    def nrm(k, shape, scale):
        return jax.random.normal(k, shape, f32) * scale

    dt_init = jnp.exp(jax.random.uniform(ks[10], (DEPTH, SSD_HEADS), f32,
                                         math.log(1e-3), math.log(1e-1)))
    return dict(
        x_prompt=nrm(ks[0], (BATCH, SEQ, D_MODEL), 1.0),
        x_sample=nrm(ks[1], (DEC_BATCH, DEC_SEQ, D_MODEL), 1.0),
        state_ssm=nrm(ks[2], (DEPTH, DEC_BATCH, SSD_HEADS, SSD_HEAD_DIM, SSD_STATE), 0.1),
        state_conv=nrm(ks[3], (DEPTH, DEC_BATCH, SSD_CONV - 1, SSD_CONV_DIM), 1.0),
        state_gla=nrm(ks[4], (DEPTH, DEC_BATCH, GLA_HEADS, GLA_DK, GLA_DV), 0.3),
        cache_swa_k=nrm(ks[5], (DEPTH, DEC_BATCH, SWA_BUF, SWA_KV_HEADS, SWA_HEAD_DIM), 1.0),
        cache_swa_v=nrm(ks[6], (DEPTH, DEC_BATCH, SWA_BUF, SWA_KV_HEADS, SWA_HEAD_DIM), 1.0),
        p_prompt=nrm(ks[7], (DEPTH, BATCH, SEQ, PLE_DIM), 1.0),
        p_sample=nrm(ks[8], (DEPTH, DEC_BATCH, DEC_SEQ, PLE_DIM), 1.0),
        rel_bias=nrm(ks[9], (REL_BUCKETS, SWA_HEADS), 0.5),
        norm_w=1.0 + nrm(ks[11], (DEPTH, D_MODEL), 0.01),
        w_in=nrm(ks[12], (DEPTH, D_MODEL, D_IN), D_MODEL ** -0.5),
        conv_w=nrm(ks[13], (DEPTH, SSD_CONV, SSD_CONV_DIM), SSD_CONV ** -0.5),
        conv_b=nrm(ks[14], (DEPTH, SSD_CONV_DIM), 0.01),
        dt_bias=dt_init + jnp.log(-jnp.expm1(-dt_init)),
        a_log=jnp.log(jax.random.uniform(ks[15], (DEPTH, SSD_HEADS), f32, 1.0, 16.0)),
        d_skip=1.0 + nrm(ks[16], (DEPTH, SSD_HEADS), 0.01),
        ssd_norm_w=1.0 + nrm(ks[17], (DEPTH, SSD_WIDTH), 0.01),
        gla_w_gk=nrm(ks[18], (DEPTH, GLA_RANK, GLA_HEADS * GLA_DK), GLA_RANK ** -0.5),
        gla_b_gk=nrm(ks[19], (DEPTH, GLA_HEADS * GLA_DK), 0.01),
        gla_norm_w=1.0 + nrm(ks[20], (DEPTH, GLA_DV), 0.01),
        q_norm_w=1.0 + nrm(ks[21], (DEPTH, SWA_HEAD_DIM), 0.01),
        k_norm_w=1.0 + nrm(ks[22], (DEPTH, SWA_HEAD_DIM), 0.01),
        attn_sinks=nrm(ks[23], (DEPTH, SWA_HEADS), 0.5),
        w_out=nrm(ks[24], (DEPTH, D_MIX, D_MODEL), 0.5 * D_MIX ** -0.5),
        w_pe=nrm(ks[25], (DEPTH, PLE_DIM, D_MODEL), 0.5 * PLE_DIM ** -0.5),
        w_pg=nrm(ks[26], (DEPTH, D_MODEL, D_MODEL), D_MODEL ** -0.5),
    )


def reference(x_prompt, x_sample, state_ssm, state_conv, state_gla, cache_swa_k, cache_swa_v,
              p_prompt, p_sample, rel_bias, norm_w, w_in, conv_w, conv_b, dt_bias, a_log,
              d_skip, ssd_norm_w, gla_w_gk, gla_b_gk, gla_norm_w, q_norm_w, k_norm_w,
              attn_sinks, w_out, w_pe, w_pg):
    hp, hs = x_prompt, x_sample
    bp = x_prompt.shape[0]
    dtp = x_prompt.dtype
    sp_ssm, sp_conv, sp_gla, sp_k, sp_v = [], [], [], [], []
    ss_ssm, ss_conv, ss_gla, ss_k, ss_v = [], [], [], [], []
    for i in range(DEPTH):
        wts = (rel_bias, norm_w[i], w_in[i], conv_w[i], conv_b[i], dt_bias[i], a_log[i], d_skip[i],
               ssd_norm_w[i], gla_w_gk[i], gla_b_gk[i], gla_norm_w[i], q_norm_w[i], k_norm_w[i],
               attn_sinks[i], w_out[i], w_pe[i], w_pg[i])
        conv0 = jnp.zeros((bp, SSD_CONV - 1, SSD_CONV_DIM), dtp)
        ssm0 = jnp.zeros((bp, SSD_HEADS, SSD_HEAD_DIM, SSD_STATE), dtp)
        gla0 = jnp.zeros((bp, GLA_HEADS, GLA_DK, GLA_DV), dtp)
        hp, a1, a2, a3, a4, a5 = layer(hp, p_prompt[i], conv0, ssm0, gla0, None, None, *wts)
        sp_ssm.append(a1); sp_conv.append(a2); sp_gla.append(a3); sp_k.append(a4); sp_v.append(a5)
        hs, b1, b2, b3, b4, b5 = layer(hs, p_sample[i], state_conv[i], state_ssm[i], state_gla[i],
                                       cache_swa_k[i], cache_swa_v[i], *wts)
        ss_ssm.append(b1); ss_conv.append(b2); ss_gla.append(b3); ss_k.append(b4); ss_v.append(b5)
    return (hp, hs,
            jnp.stack(sp_ssm), jnp.stack(sp_conv), jnp.stack(sp_gla), jnp.stack(sp_k), jnp.stack(sp_v),
            jnp.stack(ss_ssm), jnp.stack(ss_conv), jnp.stack(ss_gla), jnp.stack(ss_k), jnp.stack(ss_v))
```

```python
import functools
import math

import numpy as np
import jax
import jax.numpy as jnp
from jax import lax
from jax.experimental import pallas as pl
from jax.experimental.pallas import tpu as pltpu

D_MODEL = 1024
DEPTH = 2
SSD_HEADS = 8
SSD_HEAD_DIM = 64
SSD_WIDTH = SSD_HEADS * SSD_HEAD_DIM
SSD_GROUPS = 2
SSD_STATE = 64
SSD_CONV = 4
SSD_CONV_DIM = SSD_WIDTH + 2 * SSD_GROUPS * SSD_STATE
SSD_CHUNK = 128
GLA_HEADS = 4
GLA_DK = 32
GLA_DV = 64
GLA_WIDTH = GLA_HEADS * GLA_DV
GLA_RANK = 16
GLA_GATE_NORM = 16.0
GLA_CHUNK = 64
SWA_HEADS = 4
SWA_KV_HEADS = 2
SWA_HEAD_DIM = 64
SWA_WIDTH = SWA_HEADS * SWA_HEAD_DIM
WINDOW = 128
REL_BUCKETS = 32
REL_MAX_DIST = 128
PLE_DIM = 256
EPS = 1e-6

LANES = 128
SUBLANES = 8
HALF = LANES // 2
BLK = 128
VMEM_LIMIT_BYTES = 56 * 1024 * 1024

XBC_W = SSD_CONV_DIM
P_Z = 0
P_GQ = P_Z + SSD_WIDTH
P_GK = P_GQ + LANES
P_GV = P_GK + LANES
P_GG = P_GV + GLA_WIDTH
P_SQ = P_GG + GLA_WIDTH
P_SK = P_SQ + SWA_WIDTH
P_SV = P_SK + LANES
P_SG = P_SV + LANES
P_DTLR = P_SG + SWA_WIDTH
PROJ_W = P_DTLR + LANES
LR_LANE0 = SSD_HEADS

F32 = jnp.float32
BF16 = jnp.bfloat16
NEG_INF = float("-inf")
HIGHEST = lax.Precision.HIGHEST
NT_DIMS = (((1,), (1,)), ((), ()))
TN_DIMS = (((0,), (0,)), ((), ()))


def _iota(shape, dim):
    return lax.broadcasted_iota(jnp.int32, shape, dim)


def _div(x, d):
    return x >> (d.bit_length() - 1)


def _mod(x, d):
    return x & (d - 1)


def _softplus(x):
    return jnp.maximum(x, 0.0) + jnp.log1p(jnp.exp(-jnp.abs(x)))


def _log_sigmoid(x):
    return jnp.minimum(x, 0.0) - jnp.log1p(jnp.exp(-jnp.abs(x)))


def _silu(x):
    return x * jax.nn.sigmoid(x)


def _dot(a, b):
    return jnp.dot(a.astype(BF16), b.astype(BF16), preferred_element_type=F32)


def _dot_nt(a, b):
    return lax.dot_general(a.astype(BF16), b.astype(BF16), NT_DIMS, preferred_element_type=F32)


def _dot_tn(a, b):
    return lax.dot_general(a.astype(BF16), b.astype(BF16), TN_DIMS, preferred_element_type=F32)


def _dot_exact(a, b):
    return jnp.dot(a, b, precision=HIGHEST, preferred_element_type=F32)


def _expand_heads(x, n_heads):
    rows = x.shape[0]
    lo = _iota((rows, LANES), 1) < HALF
    tiles = []
    for j in range(n_heads // 2):
        a = jnp.broadcast_to(x[:, 2 * j:2 * j + 1], (rows, LANES))
        b = jnp.broadcast_to(x[:, 2 * j + 1:2 * j + 2], (rows, LANES))
        tiles.append(jnp.where(lo, a, b))
    return jnp.concatenate(tiles, axis=1)


def _head_rms_scale(x):
    rows, width = x.shape
    lo = _iota((rows, LANES), 1) < HALF
    outs = []
    for j in range(width // LANES):
        t = x[:, j * LANES:(j + 1) * LANES]
        sq = t * t
        s_lo = jnp.sum(jnp.where(lo, sq, 0.0), axis=-1, keepdims=True)
        s_hi = jnp.sum(jnp.where(lo, 0.0, sq), axis=-1, keepdims=True)
        outs.append(lax.rsqrt(jnp.where(lo, s_lo, s_hi) * (1.0 / HALF) + EPS))
    return outs[0] if len(outs) == 1 else jnp.concatenate(outs, axis=1)


def _group_rmsnorm(y, w):
    gw = SSD_WIDTH // SSD_GROUPS
    outs = []
    for g in range(SSD_GROUPS):
        t = y[:, g * gw:(g + 1) * gw]
        ms = jnp.sum(t * t, axis=-1, keepdims=True) * (1.0 / gw)
        outs.append(t * lax.rsqrt(ms + EPS))
    return jnp.concatenate(outs, axis=1) * w


def _rel_bucket_np(dist):
    n = np.maximum(dist, 0)
    exact = REL_BUCKETS // 2
    nf = np.maximum(n, 1).astype(np.float64)
    large = exact + (np.log(nf / exact) / math.log(REL_MAX_DIST / exact) * (REL_BUCKETS - exact)).astype(np.int32)
    large = np.minimum(large, REL_BUCKETS - 1)
    return np.where(n < exact, n, large).astype(np.int32)


def _bucket_table(dist):
    return np.where((dist >= 0) & (dist < WINDOW), _rel_bucket_np(dist), -1).astype(np.int32)


def _ssd_intra(xbc_c, dtlr, dtb, a_row, pair_mask, tri, total_of):
    xs = xbc_c[:, :SSD_WIDTH]
    bm = xbc_c[:, SSD_WIDTH:SSD_WIDTH + LANES]
    cm = xbc_c[:, SSD_WIDTH + LANES:]
    lane = _iota((BLK, LANES), 1)
    lo = lane < HALF
    dtv = _softplus(dtlr + dtb)
    adt = dtv * a_row
    acum = _dot_exact(tri, adt)
    acum_t = acum.T
    eacum = jnp.exp(acum)
    tail = jnp.exp(total_of(adt, acum) - acum)
    dtv_e = _expand_heads(dtv, SSD_HEADS)
    eacum_e = _expand_heads(eacum, SSD_HEADS)
    tail_e = _expand_heads(tail, SSD_HEADS)
    xdt = xs * dtv_e
    xw = xdt * tail_e
    cb = [_dot_nt(jnp.where(lo, cm, 0.0), bm), _dot_nt(jnp.where(lo, 0.0, cm), bm)]
    y_pairs = []
    for j in range(SSD_HEADS // 2):
        g = (2 * j) // (SSD_HEADS // SSD_GROUPS)
        ms = []
        for k in range(2):
            h = 2 * j + k
            seg = acum[:, h:h + 1] - acum_t[h:h + 1, :]
            dec = jnp.where(pair_mask, jnp.exp(seg), 0.0)
            ms.append((cb[g] * dec).astype(BF16))
        xp = xdt[:, j * LANES:(j + 1) * LANES]
        rhs = jnp.concatenate([jnp.where(lo, xp, 0.0), jnp.where(lo, 0.0, xp)], axis=0)
        y_pairs.append(_dot(jnp.concatenate(ms, axis=1), rhs))
    y_intra = jnp.concatenate(y_pairs, axis=1)
    return y_intra, xs, bm, cm, xw, eacum, eacum_e


def _gla_intra(gq, gk, gv, glog, tri, total_of, att_masks):
    bcs = _dot_exact(tri, glog)
    eb = jnp.exp(bcs)
    qe = gq * (GLA_DK ** -0.5) * eb
    ke = gk * jnp.exp(-bcs)
    btot = total_of(glog, bcs)
    kd = gk * jnp.exp(btot - bcs)
    lane_k = _iota((GLA_CHUNK, LANES), 1)
    lane_v = _iota((GLA_CHUNK, GLA_WIDTH), 1)
    outs = []
    for c2 in range(BLK // GLA_CHUNK):
        rs = slice(c2 * GLA_CHUNK, (c2 + 1) * GLA_CHUNK)
        ke_c = ke[rs]
        v_c = gv[rs]
        kbd = jnp.concatenate(
            [jnp.where(_div(lane_k, GLA_DK) == h, ke_c, 0.0) for h in range(GLA_HEADS)], axis=0)
        att = _dot_nt(qe[rs], kbd)
        att = jnp.where(att_masks[c2], att, 0.0)
        vbd = jnp.concatenate(
            [jnp.where(_div(lane_v, GLA_DV) == h, v_c, 0.0) for h in range(GLA_HEADS)], axis=0)
        outs.append(_dot(att, vbd))
    return jnp.concatenate(outs, axis=0), qe, kd, jnp.exp(btot)


def _build_bias(bucket, rel_ref, n_heads_in_lanes=None):
    accs = [jnp.full(bucket.shape, NEG_INF, F32) for _ in range(SWA_HEADS)]
    for b in range(REL_BUCKETS):
        hit = bucket == b
        for h in range(SWA_HEADS):
            accs[h] = jnp.where(hit, rel_ref[b * SWA_HEADS + h], accs[h])
    return accs


def _epilogue(h, mix, p, wout_ref, wpg_ref, wpe_ref):
    h1 = h + jnp.dot(mix, wout_ref[...], preferred_element_type=F32)
    gate = jax.nn.sigmoid(jnp.dot(h1.astype(BF16), wpg_ref[...], preferred_element_type=F32))
    pe = jnp.dot(p.astype(BF16), wpe_ref[...], preferred_element_type=F32)
    return h1 + gate * pe


def _project(h, nw_ref, win_ref, xbc_s, proj_s, rows):
    ms = jnp.mean(h * h, axis=-1, keepdims=True)
    u = (h * lax.rsqrt(ms + EPS) * nw_ref[...]).astype(BF16)
    xbc_s[SUBLANES:SUBLANES + rows, :] = jnp.dot(u, win_ref[:, :XBC_W], preferred_element_type=F32)
    proj_s[...] = jnp.dot(u, win_ref[:, XBC_W:], preferred_element_type=F32)


def _prompt_kernel(t_rows, n_chunks,
                   h_ref, p_ref, bucket_ref, rel_ref, sink_ref, nw_ref, win_ref, cw_ref, cb_ref, dtb_ref,
                   alog_ref, dsk_ref, snw_ref, wgk_ref, bgk_ref, gnw_ref, qnw_ref, knw_ref, wout_ref,
                   wpe_ref, wpg_ref,
                   y_ref, ssm_ref, conv_ref, gla_ref, ko_ref, vo_ref,
                   proj_s, xbc_s, st_s, s2_s, kext_s, vext_s, mix_s, bias_s):
    b_idx = pl.program_id(0)
    c_idx = pl.program_id(1)

    @pl.when((b_idx == 0) & (c_idx == 0))
    def _():
        accs = _build_bias(bucket_ref[...], rel_ref)
        for hh in range(SWA_HEADS):
            bias_s[hh] = accs[hh]

    @pl.when(c_idx == 0)
    def _():
        xbc_s[0:SUBLANES, :] = jnp.zeros((SUBLANES, XBC_W), F32)
        st_s[...] = jnp.zeros(st_s.shape, F32)
        s2_s[...] = jnp.zeros(s2_s.shape, F32)
        kext_s[0:BLK, :] = jnp.zeros((BLK, LANES), F32)
        vext_s[0:BLK, :] = jnp.zeros((BLK, LANES), F32)

    _project(h_ref[0], nw_ref, win_ref, xbc_s, proj_s, t_rows)

    row = _iota((BLK, BLK), 0)
    col = _iota((BLK, BLK), 1)
    causal = row >= col
    tri_ssd = jnp.where(causal, 1.0, 0.0)
    tri_gla = jnp.where(causal & (_div(row, GLA_CHUNK) == _div(col, GLA_CHUNK)), 1.0, 0.0)
    lo = col < HALF
    lane_row = _iota((1, LANES), 1)
    a_row = jnp.where(lane_row < SSD_HEADS, -jnp.exp(alog_ref[...]), 0.0)
    st_mask = (_iota((BLK, SSD_WIDTH), 0) < SSD_STATE) == (_iota((BLK, SSD_WIDTH), 1) < SSD_WIDTH // SSD_GROUPS)
    bd_mask = _div(_iota((LANES, GLA_WIDTH), 0), GLA_DK) == _div(_iota((LANES, GLA_WIDTH), 1), GLA_DV)
    att_t = _iota((GLA_CHUNK, GLA_WIDTH), 0)
    att_s = _mod(_iota((GLA_CHUNK, GLA_WIDTH), 1), GLA_CHUNK)
    att_mask = att_s <= att_t
    col2 = _iota((1, 2 * BLK), 1)
    lo2 = _iota((2 * BLK, LANES), 1) < HALF

    def ssd_total(adt, acum):
        return jnp.broadcast_to(acum[BLK - 1:BLK, :], acum.shape)

    def gla_total(glog, bcs):
        return jnp.concatenate(
            [jnp.broadcast_to(bcs[(c2 + 1) * GLA_CHUNK - 1:(c2 + 1) * GLA_CHUNK, :], (GLA_CHUNK, LANES))
             for c2 in range(BLK // GLA_CHUNK)], axis=0)

    def block(i, carry):
        r0 = pl.multiple_of(i * BLK, BLK)
        rows = pl.ds(r0, BLK)
        cw = cw_ref[...]
        xwin = xbc_s[pl.ds(r0, BLK + SUBLANES), :]
        acc = xwin[SUBLANES - 3:SUBLANES - 3 + BLK, :] * cw[0:1, :]
        for k in range(1, SSD_CONV):
            acc = acc + xwin[SUBLANES - 3 + k:SUBLANES - 3 + k + BLK, :] * cw[k:k + 1, :]
        xbc_c = _silu(acc + cb_ref[...])
        dtlr = proj_s[rows, P_DTLR:P_DTLR + LANES]
        y_intra, xs, bm, cm, xw, eacum, eacum_e = _ssd_intra(
            xbc_c, dtlr, dtb_ref[...], a_row, causal, tri_ssd, ssd_total)
        st = st_s[...]
        y = y_intra + _dot(cm, st) * eacum_e + dsk_ref[...] * xs
        st_s[...] = st * eacum_e[BLK - 1:BLK, :] + jnp.where(st_mask, _dot(bm.T, xw), 0.0)
        y = y * _silu(proj_s[rows, P_Z:P_Z + SSD_WIDTH])
        mix_s[rows, 0:SSD_WIDTH] = _group_rmsnorm(y, snw_ref[...]).astype(BF16)
        gk = proj_s[rows, P_GK:P_GK + LANES]
        gv = proj_s[rows, P_GV:P_GV + GLA_WIDTH]
        glog = _log_sigmoid(_dot(dtlr, wgk_ref[...]) + bgk_ref[...]) * (1.0 / GLA_GATE_NORM)
        o_intra, qe, kd, ebt = _gla_intra(
            proj_s[rows, P_GQ:P_GQ + LANES], gk, gv, glog, tri_gla, gla_total, [att_mask, att_mask])
        kd_t = kd.T
        ebt_t = ebt.T
        s2 = s2_s[...]
        o_parts = []
        for c2 in range(BLK // GLA_CHUNK):
            rs = slice(c2 * GLA_CHUNK, (c2 + 1) * GLA_CHUNK)
            o_parts.append(o_intra[rs] + _dot(qe[rs], s2))
            u2 = _dot(jnp.where(_div(col, GLA_CHUNK) == c2, kd_t, 0.0), gv)
            last = (c2 + 1) * GLA_CHUNK - 1
            s2 = s2 * ebt_t[:, last:last + 1] + jnp.where(bd_mask, u2, 0.0)
        s2_s[...] = s2
        o = jnp.concatenate(o_parts, axis=0)
        y_gla = o * _head_rms_scale(o) * gnw_ref[...] * _silu(proj_s[rows, P_GG:P_GG + GLA_WIDTH])
        mix_s[rows, SSD_WIDTH:SSD_WIDTH + GLA_WIDTH] = y_gla.astype(BF16)
        sq = proj_s[rows, P_SQ:P_SQ + SWA_WIDTH]
        qn = sq * _head_rms_scale(sq) * qnw_ref[...]
        sk = proj_s[rows, P_SK:P_SK + LANES]
        kn = sk * _head_rms_scale(sk) * knw_ref[...]
        vn = proj_s[rows, P_SV:P_SV + LANES]
        kext_s[BLK:2 * BLK, :] = kn
        vext_s[BLK:2 * BLK, :] = vn
        kext = kext_s[...]
        vext = vext_s[...]
        qa = qn[:, :LANES]
        qb = qn[:, LANES:]
        qs = jnp.concatenate([jnp.where(lo, qa, 0.0), jnp.where(lo, qb, 0.0),
                              jnp.where(lo, 0.0, qa), jnp.where(lo, 0.0, qb)], axis=0)
        logits = _dot_nt(qs, kext)
        first = (c_idx == 0) & (i == 0)
        pen = jnp.where(col2 < BLK, jnp.where(first, NEG_INF, 0.0), 0.0)
        es = []
        invs = []
        for hh in range(SWA_HEADS):
            sink = sink_ref[hh]
            l = logits[hh * BLK:(hh + 1) * BLK] * (SWA_HEAD_DIM ** -0.5) + bias_s[hh] + pen
            m = jnp.maximum(jnp.max(l, axis=-1, keepdims=True), sink)
            e = jnp.exp(l - m)
            den = jnp.sum(e, axis=-1, keepdims=True) + jnp.exp(sink - m)
            es.append(e.astype(BF16))
            invs.append(1.0 / den)
        v_stack = jnp.concatenate([jnp.where(lo2, vext, 0.0), jnp.where(lo2, 0.0, vext)], axis=0)
        tile_a = _dot(jnp.concatenate([es[0], es[2]], axis=1), v_stack) * jnp.where(lo, invs[0], invs[2])
        tile_b = _dot(jnp.concatenate([es[1], es[3]], axis=1), v_stack) * jnp.where(lo, invs[1], invs[3])
        oa = jnp.concatenate([tile_a, tile_b], axis=1)
        y_swa = oa * _silu(proj_s[rows, P_SG:P_SG + SWA_WIDTH])
        mix_s[rows, SSD_WIDTH + GLA_WIDTH:] = y_swa.astype(BF16)
        kext_s[0:BLK, :] = kn
        vext_s[0:BLK, :] = vn
        return carry

    lax.fori_loop(0, t_rows // BLK, block, 0)

    y_ref[0] = _epilogue(h_ref[0], mix_s[...], p_ref[0], wout_ref, wpg_ref, wpe_ref)
    xbc_s[0:SUBLANES, :] = xbc_s[t_rows:t_rows + SUBLANES, :]

    @pl.when(c_idx == n_chunks - 1)
    def _():
        st = st_s[...]
        stc = st[:SSD_STATE] + st[SSD_STATE:]
        ssm_ref[0] = jnp.concatenate([stc, stc], axis=0).T[:, :SSD_STATE]
        conv_ref[0] = xbc_s[pl.ds(t_rows + SUBLANES - 3, 3), :]
        s2 = s2_s[...]
        w = s2[:, :LANES] + s2[:, LANES:]
        gla_ref[0] = w[:, :GLA_DV] + w[:, GLA_DV:]
        ko_ref[0] = kext_s[BLK:2 * BLK, :]
        vo_ref[0] = vext_s[BLK:2 * BLK, :]


def _sample_kernel(seq, n_seq,
                   h_ref, p_ref, ssm_ref, cst_ref, gst_ref, kc_ref, vc_ref, bucket_ref, rel_ref, sink_ref,
                   nw_ref, win_ref, cw_ref, cb_ref, dtb_ref, alog_ref, dsk_ref, snw_ref, wgk_ref, bgk_ref,
                   gnw_ref, qnw_ref, knw_ref, wout_ref, wpe_ref, wpg_ref,
                   y_ref, ssm_o, xbc_o, gla_o, ko_ref, vo_ref,
                   proj_s, xbc_s, cs_s, cm_s, bm_s, xw_s, ead_s, yint_s, qe_s, kd_s, ebt_s, oint_s,
                   qn_s, kn_s, vn_s, oswa_s, kk_s, vv_s, bias_s):
    rows = n_seq * seq
    n_keys = WINDOW + seq

    @pl.when(pl.program_id(0) == 0)
    def _():
        bucket = bucket_ref[...]
        accs = _build_bias(bucket, rel_ref)
        head_of_lane = _div(_iota(bucket.shape, 1), seq)
        out = accs[SWA_HEADS - 1]
        for hh in range(SWA_HEADS - 2, -1, -1):
            out = jnp.where(head_of_lane == hh, accs[hh], out)
        bias_s[...] = out

    xbc_s[0:SUBLANES, :] = jnp.zeros((SUBLANES, XBC_W), F32)
    _project(h_ref[...], nw_ref, win_ref, xbc_s, proj_s, rows)
    cs_s[0:rows, :] = cst_ref[...]
    cs_s[rows:rows + SUBLANES, :] = jnp.zeros((SUBLANES, XBC_W), F32)
    xbc_o[...] = xbc_s[SUBLANES:SUBLANES + rows, :]

    row = _iota((BLK, BLK), 0)
    col = _iota((BLK, BLK), 1)
    same_seq = _div(row, seq) == _div(col, seq)
    pair_mask = same_seq & (row >= col)
    tri = jnp.where(pair_mask, 1.0, 0.0)
    ones_seq = jnp.where(same_seq, 1.0, 0.0)
    lo = col < HALF
    lane_row = _iota((1, LANES), 1)
    a_row = jnp.where(lane_row < SSD_HEADS, -jnp.exp(alog_ref[...]), 0.0)
    bd_mask = _div(_iota((LANES, GLA_WIDTH), 0), GLA_DK) == _div(_iota((LANES, GLA_WIDTH), 1), GLA_DV)
    eye = row == col

    def total_of(x, _cum):
        return _dot_exact(ones_seq, x)

    cw = cw_ref[...]
    t_of_row = _mod(_iota((rows, XBC_W), 0), seq)
    taps = []
    for j in range(SSD_CONV - 1, 0, -1):
        cur = xbc_s[pl.ds(SUBLANES - j, rows), :]
        old = cs_s[pl.ds(SUBLANES - j, rows), :]
        taps.append(jnp.where(t_of_row >= j, cur, old))
    taps.append(xbc_s[pl.ds(SUBLANES, rows), :])
    acc = taps[0] * cw[0:1, :]
    for k in range(1, SSD_CONV):
        acc = acc + taps[k] * cw[k:k + 1, :]
    xbc_c = _silu(acc + cb_ref[...])
    dtlr = proj_s[:, P_DTLR:P_DTLR + LANES]
    y_intra, xs, bm, cm, xw, eacum, eacum_e = _ssd_intra(
        xbc_c, dtlr, dtb_ref[...], a_row, pair_mask, tri, total_of)
    cm_s[...] = cm
    bm_s[...] = bm
    xw_s[...] = xw
    ead_s[...] = eacum

    gk = proj_s[:, P_GK:P_GK + LANES]
    gv = proj_s[:, P_GV:P_GV + GLA_WIDTH]
    glog = _log_sigmoid(_dot(dtlr, wgk_ref[...]) + bgk_ref[...]) * (1.0 / GLA_GATE_NORM)
    att_masks = []
    for c2 in range(BLK // GLA_CHUNK):
        t_loc = _iota((GLA_CHUNK, GLA_WIDTH), 0)
        s_loc = _mod(_iota((GLA_CHUNK, GLA_WIDTH), 1), GLA_CHUNK)
        att_masks.append((_div(t_loc, seq) == _div(s_loc, seq)) & (s_loc <= t_loc))
    o_intra, qe, kd, ebt = _gla_intra(
        proj_s[:, P_GQ:P_GQ + LANES], gk, gv, glog, tri, total_of, att_masks)
    qe_s[...] = qe
    kd_s[...] = kd
    ebt_s[...] = ebt

    sq = proj_s[:, P_SQ:P_SQ + SWA_WIDTH]
    qn_s[...] = sq * _head_rms_scale(sq) * qnw_ref[...]
    sk = proj_s[:, P_SK:P_SK + LANES]
    kn_s[...] = sk * _head_rms_scale(sk) * knw_ref[...]
    vn_s[...] = proj_s[:, P_SV:P_SV + LANES]

    lane32 = _div(_iota((1, SWA_HEADS * seq), 1), seq)
    sink_row = jnp.full((1, SWA_HEADS * seq), sink_ref[SWA_HEADS - 1], F32)
    for hh in range(SWA_HEADS - 2, -1, -1):
        sink_row = jnp.where(lane32 == hh, sink_ref[hh], sink_row)
    lo8 = _iota((seq, LANES), 1) < HALF
    heads_per_group = SSD_HEADS // SSD_GROUPS
    gw = SSD_WIDTH // SSD_GROUPS

    def per_seq(b, carry):
        r0 = pl.multiple_of(b * seq, seq)
        rb = pl.ds(r0, seq)
        s_prev = ssm_ref[b]
        cmb = cm_s[rb, :]
        bmb = bm_s[rb, :]
        xwb = xw_s[rb, :]
        yint_s[rb, :] = jnp.concatenate(
            [_dot_nt(cmb[:, g * SSD_STATE:(g + 1) * SSD_STATE], s_prev[g * gw:(g + 1) * gw, :])
             for g in range(SSD_GROUPS)], axis=1)
        upd = [_dot_tn(xwb[:, g * gw:(g + 1) * gw], bmb[:, g * SSD_STATE:(g + 1) * SSD_STATE])
               for g in range(SSD_GROUPS)]
        drow = ead_s[rb, :][seq - 1:seq, :]
        for hh in range(SSD_HEADS):
            g, r = divmod(hh, heads_per_group)
            hs = slice(hh * SSD_HEAD_DIM, (hh + 1) * SSD_HEAD_DIM)
            ssm_o[b, hs, :] = (s_prev[hs, :] * drow[:, hh:hh + 1]
                               + upd[g][r * SSD_HEAD_DIM:(r + 1) * SSD_HEAD_DIM, :])
        g_prev = gst_ref[b]
        g2 = jnp.concatenate([g_prev, g_prev], axis=1)
        s_bd = jnp.where(bd_mask, jnp.concatenate([g2, g2], axis=1), 0.0)
        oint_s[rb, :] = _dot(qe_s[rb, :], s_bd)
        u2 = jnp.where(bd_mask, _dot_tn(kd_s[rb, :], proj_s[rb, P_GV:P_GV + GLA_WIDTH]), 0.0)
        w = u2[:, :LANES] + u2[:, LANES:]
        erow = jnp.broadcast_to(ebt_s[rb, :][seq - 1:seq, :], (LANES, LANES))
        ecol = jnp.sum(jnp.where(eye, erow, 0.0), axis=-1, keepdims=True)
        gla_o[b] = g_prev * ecol + (w[:, :GLA_DV] + w[:, GLA_DV:])
        knb = kn_s[rb, :]
        vnb = vn_s[rb, :]
        kk_s[0:WINDOW, :] = kc_ref[b]
        kk_s[WINDOW:n_keys, :] = knb
        vv_s[0:WINDOW, :] = vc_ref[b]
        vv_s[WINDOW:n_keys, :] = vnb
        qa = qn_s[rb, 0:LANES]
        qb = qn_s[rb, LANES:2 * LANES]
        qs = jnp.concatenate([jnp.where(lo8, qa, 0.0), jnp.where(lo8, qb, 0.0),
                              jnp.where(lo8, 0.0, qa), jnp.where(lo8, 0.0, qb)], axis=0)
        l = _dot_nt(kk_s[...], qs) * (SWA_HEAD_DIM ** -0.5) + bias_s[...]
        m = jnp.maximum(jnp.max(l, axis=0, keepdims=True), sink_row)
        e = jnp.exp(l - m)
        den = jnp.sum(e, axis=0, keepdims=True) + jnp.exp(sink_row - m)
        o = _dot_tn(e / den, vv_s[...])
        tile_a = jnp.where(lo8, o[0:seq], o[2 * seq:3 * seq])
        tile_b = jnp.where(lo8, o[seq:2 * seq], o[3 * seq:4 * seq])
        oswa_s[rb, :] = jnp.concatenate([tile_a, tile_b], axis=1)
        ko_ref[b, 0:WINDOW - seq, :] = kc_ref[b, seq:WINDOW, :]
        ko_ref[b, WINDOW - seq:WINDOW, :] = knb
        vo_ref[b, 0:WINDOW - seq, :] = vc_ref[b, seq:WINDOW, :]
        vo_ref[b, WINDOW - seq:WINDOW, :] = vnb
        return carry

    lax.fori_loop(0, n_seq, per_seq, 0)

    y = y_intra + yint_s[...] * eacum_e + dsk_ref[...] * xs
    y = y * _silu(proj_s[:, P_Z:P_Z + SSD_WIDTH])
    y_ssd = _group_rmsnorm(y, snw_ref[...])
    o = o_intra + oint_s[...]
    y_gla = o * _head_rms_scale(o) * gnw_ref[...] * _silu(proj_s[:, P_GG:P_GG + GLA_WIDTH])
    y_swa = oswa_s[...] * _silu(proj_s[:, P_SG:P_SG + SWA_WIDTH])
    mix = jnp.concatenate([y_ssd, y_gla, y_swa], axis=1).astype(BF16)
    y_ref[...] = _epilogue(h_ref[...], mix, p_ref[...], wout_ref, wpg_ref, wpe_ref)


def _const_spec(shape):
    nd = len(shape)
    return pl.BlockSpec(shape, lambda *_: (0,) * nd)


def _resident_spec(shape):
    nd = len(shape)
    return pl.BlockSpec(shape, lambda *_: (0,) * nd, pipeline_mode=pl.Buffered(1))


def _smem_spec():
    return pl.BlockSpec(memory_space=pltpu.SMEM)


def _layer_weights(i, w):
    ops = [w["norm_w"][i], w["w_in"][i], w["conv_w"][i], w["conv_b"][i], w["dt_bias"][i], w["a_log"][i],
           w["d_skip"][i], w["ssd_norm_w"][i], w["gla_w_gk"][i], w["gla_b_gk"][i], w["gla_norm_w"][i],
           w["q_norm_w"][i], w["k_norm_w"][i], w["w_out"][i], w["w_pe"][i], w["w_pg"][i]]
    return ops, [_resident_spec(o.shape) if o.dtype == BF16 else _const_spec(o.shape) for o in ops]


def _prompt_tile_rows(seq_len):
    for cand in (2 * BLK, BLK):
        if seq_len % cand == 0:
            return cand
    raise ValueError("prompt length must be a multiple of 128")


def _prompt_layer(h, p, bucket, rel, sinks, wops, wspecs):
    bsz, seq_len, _ = h.shape
    t_rows = _prompt_tile_rows(seq_len)
    n_chunks = seq_len // t_rows
    kern = functools.partial(_prompt_kernel, t_rows, n_chunks)
    row_spec = lambda w: pl.BlockSpec((1, t_rows, w), lambda b, c: (b, c, 0))
    per_seq = lambda s: pl.BlockSpec((1,) + s, lambda b, c: (b,) + (0,) * len(s))
    out_shape = (
        jax.ShapeDtypeStruct((bsz, seq_len, D_MODEL), F32),
        jax.ShapeDtypeStruct((bsz, SSD_WIDTH, SSD_STATE), F32),
        jax.ShapeDtypeStruct((bsz, SSD_CONV - 1, SSD_CONV_DIM), F32),
        jax.ShapeDtypeStruct((bsz, GLA_HEADS * GLA_DK, GLA_DV), F32),
        jax.ShapeDtypeStruct((bsz, WINDOW, LANES), F32),
        jax.ShapeDtypeStruct((bsz, WINDOW, LANES), F32),
    )
    return pl.pallas_call(
        kern,
        grid=(bsz, n_chunks),
        in_specs=[row_spec(D_MODEL), row_spec(PLE_DIM), _const_spec(bucket.shape), _smem_spec(), _smem_spec()]
        + wspecs,
        out_specs=(row_spec(D_MODEL), per_seq((SSD_WIDTH, SSD_STATE)), per_seq((SSD_CONV - 1, SSD_CONV_DIM)),
                   per_seq((GLA_HEADS * GLA_DK, GLA_DV)), per_seq((WINDOW, LANES)), per_seq((WINDOW, LANES))),
        out_shape=out_shape,
        scratch_shapes=[
            pltpu.VMEM((t_rows, PROJ_W), F32),
            pltpu.VMEM((t_rows + SUBLANES, XBC_W), F32),
            pltpu.VMEM((BLK, SSD_WIDTH), F32),
            pltpu.VMEM((LANES, GLA_WIDTH), F32),
            pltpu.VMEM((2 * BLK, LANES), F32),
            pltpu.VMEM((2 * BLK, LANES), F32),
            pltpu.VMEM((t_rows, D_MODEL), BF16),
            pltpu.VMEM((SWA_HEADS, BLK, 2 * BLK), F32),
        ],
        compiler_params=pltpu.CompilerParams(
            dimension_semantics=("arbitrary", "arbitrary"), vmem_limit_bytes=VMEM_LIMIT_BYTES),
        name="prompt_layer",
    )(h, p, bucket, rel, sinks, *wops)


def _sample_layer(h, p, ssm, conv_pad, gla, kc, vc, bucket, rel, sinks, wops, wspecs, seq):
    rows_total = h.shape[0]
    n_seq = BLK // seq
    rows = n_seq * seq
    bsz = rows_total // seq
    kern = functools.partial(_sample_kernel, seq, n_seq)
    row_spec = lambda w: pl.BlockSpec((rows, w), lambda s: (s, 0))
    seq_spec = lambda a, c: pl.BlockSpec((n_seq, a, c), lambda s: (s, 0, 0))
    out_shape = (
        jax.ShapeDtypeStruct((rows_total, D_MODEL), F32),
        jax.ShapeDtypeStruct((bsz, SSD_WIDTH, SSD_STATE), F32),
        jax.ShapeDtypeStruct((rows_total, SSD_CONV_DIM), F32),
        jax.ShapeDtypeStruct((bsz, GLA_HEADS * GLA_DK, GLA_DV), F32),
        jax.ShapeDtypeStruct((bsz, WINDOW, LANES), F32),
        jax.ShapeDtypeStruct((bsz, WINDOW, LANES), F32),
    )
    n_keys = WINDOW + seq
    f32_scratch = lambda r, c: pltpu.VMEM((r, c), F32)
    return pl.pallas_call(
        kern,
        grid=(rows_total // rows,),
        in_specs=[row_spec(D_MODEL), row_spec(PLE_DIM), seq_spec(SSD_WIDTH, SSD_STATE), row_spec(SSD_CONV_DIM),
                  seq_spec(GLA_HEADS * GLA_DK, GLA_DV), seq_spec(WINDOW, LANES), seq_spec(WINDOW, LANES),
                  _const_spec(bucket.shape), _smem_spec(), _smem_spec()] + wspecs,
        out_specs=(row_spec(D_MODEL), seq_spec(SSD_WIDTH, SSD_STATE), row_spec(SSD_CONV_DIM),
                   seq_spec(GLA_HEADS * GLA_DK, GLA_DV), seq_spec(WINDOW, LANES), seq_spec(WINDOW, LANES)),
        out_shape=out_shape,
        scratch_shapes=[
            f32_scratch(rows, PROJ_W),
            f32_scratch(rows + SUBLANES, XBC_W),
            f32_scratch(rows + SUBLANES, XBC_W),
            f32_scratch(rows, LANES), f32_scratch(rows, LANES), f32_scratch(rows, SSD_WIDTH),
            f32_scratch(rows, LANES), f32_scratch(rows, SSD_WIDTH),
            f32_scratch(rows, LANES), f32_scratch(rows, LANES), f32_scratch(rows, LANES),
            f32_scratch(rows, GLA_WIDTH),
            f32_scratch(rows, SWA_WIDTH), f32_scratch(rows, LANES), f32_scratch(rows, LANES),
            f32_scratch(rows, SWA_WIDTH),
            f32_scratch(n_keys, LANES), f32_scratch(n_keys, LANES),
            f32_scratch(n_keys, SWA_HEADS * seq),
        ],
        compiler_params=pltpu.CompilerParams(
            dimension_semantics=("arbitrary",), vmem_limit_bytes=VMEM_LIMIT_BYTES),
        name="sample_layer",
    )(h, p, ssm, conv_pad, gla, kc, vc, bucket, rel, sinks, *wops)


def _prepare_weights(norm_w, w_in, conv_w, conv_b, dt_bias, a_log, d_skip, ssd_norm_w, gla_w_gk, gla_b_gk,
                     gla_norm_w, q_norm_w, k_norm_w, w_out, w_pe, w_pg):
    sizes = (SSD_WIDTH, SSD_CONV_DIM, SSD_HEADS, GLA_HEADS * GLA_DK, GLA_HEADS * GLA_DK, GLA_WIDTH, GLA_WIDTH,
             GLA_RANK, SWA_WIDTH, SWA_KV_HEADS * SWA_HEAD_DIM, SWA_KV_HEADS * SWA_HEAD_DIM, SWA_WIDTH)
    offs = np.concatenate([[0], np.cumsum(sizes)])
    z, xbc, dt, gq, gk, gv, gg, glr, sq, sk, sv, sg = [w_in[:, :, offs[k]:offs[k + 1]] for k in range(len(sizes))]
    head_order = np.array([0, 2, 1, 3])
    swa_perm = (head_order[:, None] * SWA_HEAD_DIM + np.arange(SWA_HEAD_DIM)[None, :]).reshape(-1)
    pad_w = LANES - SSD_HEADS - GLA_RANK
    pad = jnp.zeros(w_in.shape[:2] + (pad_w,), w_in.dtype)
    w_in_p = jnp.concatenate([xbc, z, gq, gk, gv, gg, sq[:, :, swa_perm], sk, sv, sg[:, :, swa_perm], dt, glr, pad],
                             axis=-1).astype(BF16)
    out_rows = np.concatenate([np.arange(SSD_WIDTH + GLA_WIDTH), SSD_WIDTH + GLA_WIDTH + swa_perm])
    lane_pad = lambda x: jnp.pad(x, ((0, 0), (0, LANES - x.shape[-1])))[:, None, :]
    wgk_p = jnp.pad(gla_w_gk, ((0, 0), (LR_LANE0, LANES - LR_LANE0 - GLA_RANK), (0, 0))).astype(BF16)
    return dict(
        norm_w=norm_w[:, None, :], w_in=w_in_p, conv_w=conv_w, conv_b=conv_b[:, None, :],
        dt_bias=lane_pad(dt_bias), a_log=lane_pad(a_log),
        d_skip=jnp.repeat(d_skip, SSD_HEAD_DIM, axis=-1)[:, None, :], ssd_norm_w=ssd_norm_w[:, None, :],
        gla_w_gk=wgk_p, gla_b_gk=gla_b_gk[:, None, :],
        gla_norm_w=jnp.tile(gla_norm_w, (1, GLA_HEADS))[:, None, :],
        q_norm_w=jnp.tile(q_norm_w, (1, SWA_HEADS))[:, None, :],
        k_norm_w=jnp.tile(k_norm_w, (1, SWA_KV_HEADS))[:, None, :],
        w_out=w_out[:, out_rows, :].astype(BF16), w_pe=w_pe.astype(BF16), w_pg=w_pg.astype(BF16))


def kernel(x_prompt, x_sample, state_ssm, state_conv, state_gla, cache_swa_k, cache_swa_v, p_prompt, p_sample, rel_bias, norm_w, w_in, conv_w, conv_b, dt_bias, a_log, d_skip, ssd_norm_w, gla_w_gk, gla_b_gk, gla_norm_w, q_norm_w, k_norm_w, attn_sinks, w_out, w_pe, w_pg):
    depth = w_in.shape[0]
    bp, seq_p, _ = x_prompt.shape
    bs, seq_s, _ = x_sample.shape
    assert seq_s == SUBLANES and BLK % seq_s == 0 and (bs * seq_s) % BLK == 0
    assert cache_swa_k.shape[2] == WINDOW
    w = _prepare_weights(norm_w, w_in, conv_w, conv_b, dt_bias, a_log, d_skip, ssd_norm_w, gla_w_gk, gla_b_gk,
                         gla_norm_w, q_norm_w, k_norm_w, w_out, w_pe, w_pg)
    rel_flat = rel_bias.reshape(-1)
    dist_p = WINDOW + np.arange(BLK)[:, None] - np.arange(2 * BLK)[None, :]
    bucket_p = jnp.asarray(_bucket_table(dist_p))
    dist_s = WINDOW + np.arange(seq_s)[None, :] - np.arange(WINDOW + seq_s)[:, None]
    bucket_s = jnp.asarray(np.tile(_bucket_table(dist_s), (1, SWA_HEADS)))

    hp = x_prompt
    hs = x_sample.reshape(bs * seq_s, D_MODEL)
    outs_p = [[] for _ in range(5)]
    outs_s = [[] for _ in range(5)]
    for i in range(depth):
        wops, wspecs = _layer_weights(i, w)
        hp, ssm, conv, gla, ko, vo = _prompt_layer(hp, p_prompt[i], bucket_p, rel_flat, attn_sinks[i], wops, wspecs)
        for lst, val in zip(outs_p, (ssm.reshape(bp, SSD_HEADS, SSD_HEAD_DIM, SSD_STATE), conv,
                                     gla.reshape(bp, GLA_HEADS, GLA_DK, GLA_DV),
                                     ko.reshape(bp, WINDOW, SWA_KV_HEADS, SWA_HEAD_DIM),
                                     vo.reshape(bp, WINDOW, SWA_KV_HEADS, SWA_HEAD_DIM))):
            lst.append(val)
        conv_pad = jnp.pad(state_conv[i], ((0, 0), (seq_s - (SSD_CONV - 1), 0), (0, 0)))
        hs, ssm, xbc_raw, gla, ko, vo = _sample_layer(
            hs, p_sample[i].reshape(bs * seq_s, PLE_DIM),
            state_ssm[i].reshape(bs, SSD_WIDTH, SSD_STATE), conv_pad.reshape(bs * seq_s, SSD_CONV_DIM),
            state_gla[i].reshape(bs, GLA_HEADS * GLA_DK, GLA_DV),
            cache_swa_k[i].reshape(bs, WINDOW, LANES), cache_swa_v[i].reshape(bs, WINDOW, LANES),
            bucket_s, rel_flat, attn_sinks[i], wops, wspecs, seq_s)
        conv = xbc_raw.reshape(bs, seq_s, SSD_CONV_DIM)[:, seq_s - (SSD_CONV - 1):, :]
        for lst, val in zip(outs_s, (ssm.reshape(bs, SSD_HEADS, SSD_HEAD_DIM, SSD_STATE), conv,
                                     gla.reshape(bs, GLA_HEADS, GLA_DK, GLA_DV),
                                     ko.reshape(bs, WINDOW, SWA_KV_HEADS, SWA_HEAD_DIM),
                                     vo.reshape(bs, WINDOW, SWA_KV_HEADS, SWA_HEAD_DIM))):
            lst.append(val)
    return ((hp, hs.reshape(bs, seq_s, D_MODEL))
            + tuple(jnp.stack(v) for v in outs_p) + tuple(jnp.stack(v) for v in outs_s))
```

```python
import functools
import math

import numpy as np
import jax
import jax.numpy as jnp
from jax import lax
from jax.experimental import pallas as pl
from jax.experimental.pallas import tpu as pltpu

D_MODEL = 1024
DEPTH = 2
SSD_HEADS = 8
SSD_HEAD_DIM = 64
SSD_WIDTH = SSD_HEADS * SSD_HEAD_DIM
SSD_GROUPS = 2
SSD_STATE = 64
SSD_CONV = 4
SSD_CONV_DIM = SSD_WIDTH + 2 * SSD_GROUPS * SSD_STATE
SSD_CHUNK = 128
GLA_HEADS = 4
GLA_DK = 32
GLA_DV = 64
GLA_WIDTH = GLA_HEADS * GLA_DV
GLA_RANK = 16
GLA_GATE_NORM = 16.0
GLA_CHUNK = 64
SWA_HEADS = 4
SWA_KV_HEADS = 2
SWA_HEAD_DIM = 64
SWA_WIDTH = SWA_HEADS * SWA_HEAD_DIM
WINDOW = 128
REL_BUCKETS = 32
REL_MAX_DIST = 128
PLE_DIM = 256
EPS = 1e-6

LANES = 128
SUBLANES = 8
HALF = LANES // 2
BLK = 128
VMEM_LIMIT_BYTES = 56 * 1024 * 1024

XBC_W = SSD_CONV_DIM
P_Z = 0
P_GQ = P_Z + SSD_WIDTH
P_GK = P_GQ + LANES
P_GV = P_GK + LANES
P_GG = P_GV + GLA_WIDTH
P_SQ = P_GG + GLA_WIDTH
P_SK = P_SQ + SWA_WIDTH
P_SV = P_SK + LANES
P_SG = P_SV + LANES
P_DTLR = P_SG + SWA_WIDTH
PROJ_W = P_DTLR + LANES
LR_LANE0 = SSD_HEADS

F32 = jnp.float32
BF16 = jnp.bfloat16
NEG_INF = float("-inf")
N_PROMPT_INPUTS = 21
NT_DIMS = (((1,), (1,)), ((), ()))
TN_DIMS = (((0,), (0,)), ((), ()))


def _iota(shape, dim):
    return lax.broadcasted_iota(jnp.int32, shape, dim)


def _div(x, d):
    return x >> (d.bit_length() - 1)


def _mod(x, d):
    return x & (d - 1)


def _softplus(x):
    e = jnp.exp(-jnp.abs(x))
    u = 1.0 + e
    d = u - 1.0
    log1p_e = jnp.where(d == 0.0, e, jnp.log(u) * (e / jnp.where(d == 0.0, 1.0, d)))
    return jnp.maximum(x, 0.0) + log1p_e


def _log_sigmoid(x):
    return jnp.minimum(x, 0.0) - jnp.log(1.0 + jnp.exp(-jnp.abs(x)))


def _silu(x):
    return x * jax.nn.sigmoid(x)


def _dot(a, b):
    return jnp.dot(a.astype(BF16), b.astype(BF16), preferred_element_type=F32)


def _dot_nt(a, b):
    return lax.dot_general(a.astype(BF16), b.astype(BF16), NT_DIMS, preferred_element_type=F32)


def _dot_tn(a, b):
    return lax.dot_general(a.astype(BF16), b.astype(BF16), TN_DIMS, preferred_element_type=F32)


def _dot_exact(sel, x):
    x1 = x.astype(BF16)
    r1 = x - x1.astype(F32)
    x2 = r1.astype(BF16)
    x3 = (r1 - x2.astype(F32)).astype(BF16)
    dot = functools.partial(jnp.dot, sel, preferred_element_type=F32)
    return dot(x1) + dot(x2) + dot(x3)


def _expand_heads(x, n_heads):
    rows = x.shape[0]
    lo = _iota((rows, LANES), 1) < HALF
    tiles = []
    for j in range(n_heads // 2):
        a = jnp.broadcast_to(x[:, 2 * j:2 * j + 1], (rows, LANES))
        b = jnp.broadcast_to(x[:, 2 * j + 1:2 * j + 2], (rows, LANES))
        tiles.append(jnp.where(lo, a, b))
    return jnp.concatenate(tiles, axis=1)


def _head_rms_scale(x):
    rows, width = x.shape
    lo = _iota((rows, LANES), 1) < HALF
    outs = []
    for j in range(width // LANES):
        t = x[:, j * LANES:(j + 1) * LANES]
        sq = t * t
        s_lo = jnp.sum(jnp.where(lo, sq, 0.0), axis=-1, keepdims=True)
        s_hi = jnp.sum(jnp.where(lo, 0.0, sq), axis=-1, keepdims=True)
        outs.append(lax.rsqrt(jnp.where(lo, s_lo, s_hi) * (1.0 / HALF) + EPS))
    return outs[0] if len(outs) == 1 else jnp.concatenate(outs, axis=1)


def _group_rmsnorm(y, w):
    gw = SSD_WIDTH // SSD_GROUPS
    outs = []
    for g in range(SSD_GROUPS):
        t = y[:, g * gw:(g + 1) * gw]
        ms = jnp.sum(t * t, axis=-1, keepdims=True) * (1.0 / gw)
        outs.append(t * lax.rsqrt(ms + EPS))
    return jnp.concatenate(outs, axis=1) * w


def _rel_bucket_np(dist):
    n = np.maximum(dist, 0)
    exact = REL_BUCKETS // 2
    nf = np.maximum(n, 1).astype(np.float64)
    large = exact + (np.log(nf / exact) / math.log(REL_MAX_DIST / exact) * (REL_BUCKETS - exact)).astype(np.int32)
    large = np.minimum(large, REL_BUCKETS - 1)
    return np.where(n < exact, n, large).astype(np.int32)


def _bucket_table(dist):
    return np.where((dist >= 0) & (dist < WINDOW), _rel_bucket_np(dist), -1).astype(np.int32)


def _ssd_intra(xbc_c, dtlr, dtb, a_row, pair_mask, tri, total_of):
    xs = xbc_c[:, :SSD_WIDTH]
    bm = xbc_c[:, SSD_WIDTH:SSD_WIDTH + LANES]
    cm = xbc_c[:, SSD_WIDTH + LANES:]
    lane = _iota((BLK, LANES), 1)
    lo = lane < HALF
    dtv = _softplus(dtlr + dtb)
    adt = dtv * a_row
    acum = _dot_exact(tri, adt)
    acum_t = acum.T
    eacum = jnp.exp(acum)
    tail = jnp.exp(total_of(adt, acum) - acum)
    dtv_e = _expand_heads(dtv, SSD_HEADS)
    eacum_e = _expand_heads(eacum, SSD_HEADS)
    tail_e = _expand_heads(tail, SSD_HEADS)
    xdt = xs * dtv_e
    xw = xdt * tail_e
    cb = [_dot_nt(jnp.where(lo, cm, 0.0), bm), _dot_nt(jnp.where(lo, 0.0, cm), bm)]
    y_pairs = []
    for j in range(SSD_HEADS // 2):
        g = (2 * j) // (SSD_HEADS // SSD_GROUPS)
        ms = []
        for k in range(2):
            h = 2 * j + k
            seg = acum[:, h:h + 1] - acum_t[h:h + 1, :]
            dec = jnp.where(pair_mask, jnp.exp(seg), 0.0)
            ms.append((cb[g] * dec).astype(BF16))
        xp = xdt[:, j * LANES:(j + 1) * LANES]
        rhs = jnp.concatenate([jnp.where(lo, xp, 0.0), jnp.where(lo, 0.0, xp)], axis=0)
        y_pairs.append(_dot(jnp.concatenate(ms, axis=1), rhs))
    y_intra = jnp.concatenate(y_pairs, axis=1)
    return y_intra, xs, bm, cm, xw, eacum, eacum_e


def _gla_intra(gq, gk, gv, glog, tri, total_of, att_masks):
    bcs = _dot_exact(tri, glog)
    eb = jnp.exp(bcs)
    qe = gq * (GLA_DK ** -0.5) * eb
    ke = gk * jnp.exp(-bcs)
    btot = total_of(glog, bcs)
    kd = gk * jnp.exp(btot - bcs)
    lane_k = _iota((GLA_CHUNK, LANES), 1)
    lane_v = _iota((GLA_CHUNK, GLA_WIDTH), 1)
    outs = []
    for c2 in range(BLK // GLA_CHUNK):
        rs = slice(c2 * GLA_CHUNK, (c2 + 1) * GLA_CHUNK)
        ke_c = ke[rs]
        v_c = gv[rs]
        kbd = jnp.concatenate(
            [jnp.where(_div(lane_k, GLA_DK) == h, ke_c, 0.0) for h in range(GLA_HEADS)], axis=0)
        att = _dot_nt(qe[rs], kbd)
        att = jnp.where(att_masks[c2], att, 0.0)
        vbd = jnp.concatenate(
            [jnp.where(_div(lane_v, GLA_DV) == h, v_c, 0.0) for h in range(GLA_HEADS)], axis=0)
        outs.append(_dot(att, vbd))
    return jnp.concatenate(outs, axis=0), qe, kd, jnp.exp(btot)


def _build_bias(bucket, rel_ref, n_heads_in_lanes=None):
    accs = [jnp.full(bucket.shape, NEG_INF, F32) for _ in range(SWA_HEADS)]
    for b in range(REL_BUCKETS):
        hit = bucket == b
        for h in range(SWA_HEADS):
            accs[h] = jnp.where(hit, rel_ref[b * SWA_HEADS + h], accs[h])
    return accs


def _epilogue(h, mix, p, wout_ref, wpg_ref, wpe_ref):
    h1 = h + jnp.dot(mix, wout_ref[...], preferred_element_type=F32)
    gate = jax.nn.sigmoid(jnp.dot(h1.astype(BF16), wpg_ref[...], preferred_element_type=F32))
    pe = jnp.dot(p.astype(BF16), wpe_ref[...], preferred_element_type=F32)
    return h1 + gate * pe


def _project(h, nw_ref, win_ref, xbc_s, proj_s, rows):
    ms = jnp.mean(h * h, axis=-1, keepdims=True)
    u = (h * lax.rsqrt(ms + EPS) * nw_ref[...]).astype(BF16)
    xbc_s[SUBLANES:SUBLANES + rows, :] = jnp.dot(u, win_ref[:, :XBC_W], preferred_element_type=F32)
    proj_s[...] = jnp.dot(u, win_ref[:, XBC_W:], preferred_element_type=F32)


def _prompt_kernel(t_rows, n_chunks, n_aliased, *refs):
    (h_ref, p_ref, bucket_ref, rel_ref, sink_ref, nw_ref, win_ref, cw_ref, cb_ref, dtb_ref,
     alog_ref, dsk_ref, snw_ref, wgk_ref, bgk_ref, gnw_ref, qnw_ref, knw_ref, wout_ref,
     wpe_ref, wpg_ref) = refs[:N_PROMPT_INPUTS]
    (y_ref, ssm_ref, conv_ref, gla_ref, ko_ref, vo_ref,
     proj_s, xbc_s, st_s, s2_s, kext_s, vext_s, mix_s, bias_s) = refs[N_PROMPT_INPUTS + n_aliased:]
    b_idx = pl.program_id(0)
    c_idx = pl.program_id(1)

    @pl.when((b_idx == 0) & (c_idx == 0))
    def _():
        accs = _build_bias(bucket_ref[...], rel_ref)
        own_block = _iota((BLK, 2 * BLK), 1) >= BLK
        for hh in range(SWA_HEADS):
            bias_s[hh] = accs[hh]
            bias_s[SWA_HEADS + hh] = jnp.where(own_block, accs[hh], NEG_INF)

    @pl.when(c_idx == 0)
    def _():
        xbc_s[0:SUBLANES, :] = jnp.zeros((SUBLANES, XBC_W), F32)
        st_s[...] = jnp.zeros(st_s.shape, F32)
        s2_s[...] = jnp.zeros(s2_s.shape, F32)
        kext_s[0:BLK, :] = jnp.zeros((BLK, LANES), F32)
        vext_s[0:BLK, :] = jnp.zeros((BLK, LANES), F32)

    _project(h_ref[0], nw_ref, win_ref, xbc_s, proj_s, t_rows)

    row = _iota((BLK, BLK), 0)
    col = _iota((BLK, BLK), 1)
    causal = row >= col
    tri_ssd = jnp.where(causal, 1.0, 0.0).astype(BF16)
    tri_gla = jnp.where(causal & (_div(row, GLA_CHUNK) == _div(col, GLA_CHUNK)), 1.0, 0.0).astype(BF16)
    lo = col < HALF
    lane_row = _iota((1, LANES), 1)
    a_row = jnp.where(lane_row < SSD_HEADS, -jnp.exp(alog_ref[...]), 0.0)
    st_mask = (_iota((BLK, SSD_WIDTH), 0) < SSD_STATE) == (_iota((BLK, SSD_WIDTH), 1) < SSD_WIDTH // SSD_GROUPS)
    bd_mask = _div(_iota((LANES, GLA_WIDTH), 0), GLA_DK) == _div(_iota((LANES, GLA_WIDTH), 1), GLA_DV)
    att_t = _iota((GLA_CHUNK, GLA_WIDTH), 0)
    att_s = _mod(_iota((GLA_CHUNK, GLA_WIDTH), 1), GLA_CHUNK)
    att_mask = att_s <= att_t
    lo2 = _iota((2 * BLK, LANES), 1) < HALF

    def ssd_total(adt, acum):
        return jnp.broadcast_to(acum[BLK - 1:BLK, :], acum.shape)

    def gla_total(glog, bcs):
        return jnp.concatenate(
            [jnp.broadcast_to(bcs[(c2 + 1) * GLA_CHUNK - 1:(c2 + 1) * GLA_CHUNK, :], (GLA_CHUNK, LANES))
             for c2 in range(BLK // GLA_CHUNK)], axis=0)

    def block(i, carry):
        r0 = pl.multiple_of(i * BLK, BLK)
        rows = pl.ds(r0, BLK)
        cw = cw_ref[...]
        xwin = xbc_s[pl.ds(r0, BLK + SUBLANES), :]
        acc = xwin[SUBLANES - 3:SUBLANES - 3 + BLK, :] * cw[0:1, :]
        for k in range(1, SSD_CONV):
            acc = acc + xwin[SUBLANES - 3 + k:SUBLANES - 3 + k + BLK, :] * cw[k:k + 1, :]
        xbc_c = _silu(acc + cb_ref[...])
        dtlr = proj_s[rows, P_DTLR:P_DTLR + LANES]
        y_intra, xs, bm, cm, xw, eacum, eacum_e = _ssd_intra(
            xbc_c, dtlr, dtb_ref[...], a_row, causal, tri_ssd, ssd_total)
        st = st_s[...]
        y = y_intra + _dot(cm, st) * eacum_e + dsk_ref[...] * xs
        st_s[...] = st * eacum_e[BLK - 1:BLK, :] + jnp.where(st_mask, _dot(bm.T, xw), 0.0)
        y = y * _silu(proj_s[rows, P_Z:P_Z + SSD_WIDTH])
        mix_s[rows, 0:SSD_WIDTH] = _group_rmsnorm(y, snw_ref[...]).astype(BF16)
        gk = proj_s[rows, P_GK:P_GK + LANES]
        gv = proj_s[rows, P_GV:P_GV + GLA_WIDTH]
        glog = _log_sigmoid(_dot(dtlr, wgk_ref[...]) + bgk_ref[...]) * (1.0 / GLA_GATE_NORM)
        o_intra, qe, kd, ebt = _gla_intra(
            proj_s[rows, P_GQ:P_GQ + LANES], gk, gv, glog, tri_gla, gla_total, [att_mask, att_mask])
        kd_t = kd.T
        ebt_t = ebt.T
        s2 = s2_s[...]
        o_parts = []
        for c2 in range(BLK // GLA_CHUNK):
            rs = slice(c2 * GLA_CHUNK, (c2 + 1) * GLA_CHUNK)
            o_parts.append(o_intra[rs] + _dot(qe[rs], s2))
            u2 = _dot(jnp.where(_div(col, GLA_CHUNK) == c2, kd_t, 0.0), gv)
            last = (c2 + 1) * GLA_CHUNK - 1
            s2 = s2 * ebt_t[:, last:last + 1] + jnp.where(bd_mask, u2, 0.0)
        s2_s[...] = s2
        o = jnp.concatenate(o_parts, axis=0)
        y_gla = o * _head_rms_scale(o) * gnw_ref[...] * _silu(proj_s[rows, P_GG:P_GG + GLA_WIDTH])
        mix_s[rows, SSD_WIDTH:SSD_WIDTH + GLA_WIDTH] = y_gla.astype(BF16)
        sq = proj_s[rows, P_SQ:P_SQ + SWA_WIDTH]
        qn = sq * _head_rms_scale(sq) * qnw_ref[...] * (SWA_HEAD_DIM ** -0.5)
        sk = proj_s[rows, P_SK:P_SK + LANES]
        kn = sk * _head_rms_scale(sk) * knw_ref[...]
        vn = proj_s[rows, P_SV:P_SV + LANES]
        kext_s[BLK:2 * BLK, :] = kn
        vext_s[BLK:2 * BLK, :] = vn
        kext = kext_s[...]
        vext = vext_s[...]
        qa = qn[:, :LANES]
        qb = qn[:, LANES:]
        qs = jnp.concatenate([jnp.where(lo, qa, 0.0), jnp.where(lo, qb, 0.0),
                              jnp.where(lo, 0.0, qa), jnp.where(lo, 0.0, qb)], axis=0)
        logits = _dot_nt(qs, kext)
        bias_row0 = jnp.where((c_idx == 0) & (i == 0), SWA_HEADS, 0)
        es = []
        invs = []
        for hh in range(SWA_HEADS):
            sink = sink_ref[hh]
            l = logits[hh * BLK:(hh + 1) * BLK] + bias_s[bias_row0 + hh]
            m = jnp.maximum(jnp.max(l, axis=-1, keepdims=True), sink)
            e = jnp.exp(l - m)
            den = jnp.sum(e, axis=-1, keepdims=True) + jnp.exp(sink - m)
            es.append(e.astype(BF16))
            invs.append(1.0 / den)
        v_stack = jnp.concatenate([jnp.where(lo2, vext, 0.0), jnp.where(lo2, 0.0, vext)], axis=0)
        tile_a = _dot(jnp.concatenate([es[0], es[2]], axis=1), v_stack) * jnp.where(lo, invs[0], invs[2])
        tile_b = _dot(jnp.concatenate([es[1], es[3]], axis=1), v_stack) * jnp.where(lo, invs[1], invs[3])
        oa = jnp.concatenate([tile_a, tile_b], axis=1)
        y_swa = oa * _silu(proj_s[rows, P_SG:P_SG + SWA_WIDTH])
        mix_s[rows, SSD_WIDTH + GLA_WIDTH:] = y_swa.astype(BF16)
        kext_s[0:BLK, :] = kn
        vext_s[0:BLK, :] = vn
        return carry

    lax.fori_loop(0, t_rows // BLK, block, 0)

    y_ref[0] = _epilogue(h_ref[0], mix_s[...], p_ref[0, 0], wout_ref, wpg_ref, wpe_ref)
    xbc_s[0:SUBLANES, :] = xbc_s[t_rows:t_rows + SUBLANES, :]

    @pl.when(c_idx == n_chunks - 1)
    def _():
        st = st_s[...]
        stc = st[:SSD_STATE] + st[SSD_STATE:]
        ssm_ref[0, 0] = jnp.concatenate([stc, stc], axis=0).T[:, :SSD_STATE]
        conv_ref[0, 0] = xbc_s[pl.ds(t_rows + SUBLANES - 3, 3), :]
        s2 = s2_s[...]
        w = s2[:, :LANES] + s2[:, LANES:]
        gla_ref[0, 0] = w[:, :GLA_DV] + w[:, GLA_DV:]
        ko_ref[0, 0] = kext_s[BLK:2 * BLK, :].T
        vo_ref[0, 0] = vext_s[BLK:2 * BLK, :].T


def _sample_kernel(seq, n_seq,
                   h_ref, p_ref, ssm_ref, cst_ref, gst_ref, kc_ref, vc_ref, bucket_ref, rel_ref, sink_ref,
                   nw_ref, win_ref, cw_ref, cb_ref, dtb_ref, alog_ref, dsk_ref, snw_ref, wgk_ref, bgk_ref,
                   gnw_ref, qnw_ref, knw_ref, wout_ref, wpe_ref, wpg_ref,
                   y_ref, ssm_o, xbc_o, gla_o, ko_ref, vo_ref,
                   proj_s, xbc_s, cs_s, cm_s, bm_s, xw_s, ead_s, yint_s, qe_s, kd_s, ebt_s, oint_s,
                   qn_s, kn_s, vn_s, oswa_s, kk_s, vv_s, bias_s):
    rows = n_seq * seq
    n_keys = WINDOW + seq

    @pl.when(pl.program_id(0) == 0)
    def _():
        bucket = bucket_ref[...]
        accs = _build_bias(bucket, rel_ref)
        head_of_lane = _div(_iota(bucket.shape, 1), seq)
        out = accs[SWA_HEADS - 1]
        for hh in range(SWA_HEADS - 2, -1, -1):
            out = jnp.where(head_of_lane == hh, accs[hh], out)
        bias_s[...] = out

    xbc_s[0:SUBLANES, :] = jnp.zeros((SUBLANES, XBC_W), F32)
    _project(h_ref[...], nw_ref, win_ref, xbc_s, proj_s, rows)
    cs_s[0:rows, :] = cst_ref[...]
    cs_s[rows:rows + SUBLANES, :] = jnp.zeros((SUBLANES, XBC_W), F32)
    xbc_o[...] = xbc_s[SUBLANES:SUBLANES + rows, :]

    row = _iota((BLK, BLK), 0)
    col = _iota((BLK, BLK), 1)
    same_seq = _div(row, seq) == _div(col, seq)
    pair_mask = same_seq & (row >= col)
    tri = jnp.where(pair_mask, 1.0, 0.0).astype(BF16)
    ones_seq = jnp.where(same_seq, 1.0, 0.0).astype(BF16)
    lo = col < HALF
    lane_row = _iota((1, LANES), 1)
    a_row = jnp.where(lane_row < SSD_HEADS, -jnp.exp(alog_ref[...]), 0.0)
    bd_mask = _div(_iota((LANES, GLA_WIDTH), 0), GLA_DK) == _div(_iota((LANES, GLA_WIDTH), 1), GLA_DV)
    eye = row == col

    def total_of(x, _cum):
        return _dot_exact(ones_seq, x)

    cw = cw_ref[...]
    t_of_row = _mod(_iota((rows, XBC_W), 0), seq)
    taps = []
    for j in range(SSD_CONV - 1, 0, -1):
        cur = xbc_s[pl.ds(SUBLANES - j, rows), :]
        old = cs_s[pl.ds(SUBLANES - j, rows), :]
        taps.append(jnp.where(t_of_row >= j, cur, old))
    taps.append(xbc_s[pl.ds(SUBLANES, rows), :])
    acc = taps[0] * cw[0:1, :]
    for k in range(1, SSD_CONV):
        acc = acc + taps[k] * cw[k:k + 1, :]
    xbc_c = _silu(acc + cb_ref[...])
    dtlr = proj_s[:, P_DTLR:P_DTLR + LANES]
    y_intra, xs, bm, cm, xw, eacum, eacum_e = _ssd_intra(
        xbc_c, dtlr, dtb_ref[...], a_row, pair_mask, tri, total_of)
    cm_s[...] = cm
    bm_s[...] = bm
    xw_s[...] = xw
    ead_s[...] = eacum

    gk = proj_s[:, P_GK:P_GK + LANES]
    gv = proj_s[:, P_GV:P_GV + GLA_WIDTH]
    glog = _log_sigmoid(_dot(dtlr, wgk_ref[...]) + bgk_ref[...]) * (1.0 / GLA_GATE_NORM)
    att_masks = []
    for c2 in range(BLK // GLA_CHUNK):
        t_loc = _iota((GLA_CHUNK, GLA_WIDTH), 0)
        s_loc = _mod(_iota((GLA_CHUNK, GLA_WIDTH), 1), GLA_CHUNK)
        att_masks.append((_div(t_loc, seq) == _div(s_loc, seq)) & (s_loc <= t_loc))
    o_intra, qe, kd, ebt = _gla_intra(
        proj_s[:, P_GQ:P_GQ + LANES], gk, gv, glog, tri, total_of, att_masks)
    qe_s[...] = qe
    kd_s[...] = kd
    ebt_s[...] = ebt

    sq = proj_s[:, P_SQ:P_SQ + SWA_WIDTH]
    qn_s[...] = sq * _head_rms_scale(sq) * qnw_ref[...]
    sk = proj_s[:, P_SK:P_SK + LANES]
    kn_s[...] = sk * _head_rms_scale(sk) * knw_ref[...]
    vn_s[...] = proj_s[:, P_SV:P_SV + LANES]

    lane32 = _div(_iota((1, SWA_HEADS * seq), 1), seq)
    sink_row = jnp.full((1, SWA_HEADS * seq), sink_ref[SWA_HEADS - 1], F32)
    for hh in range(SWA_HEADS - 2, -1, -1):
        sink_row = jnp.where(lane32 == hh, sink_ref[hh], sink_row)
    lo8 = _iota((seq, LANES), 1) < HALF
    heads_per_group = SSD_HEADS // SSD_GROUPS
    gw = SSD_WIDTH // SSD_GROUPS

    def per_seq(b, carry):
        r0 = pl.multiple_of(b * seq, seq)
        rb = pl.ds(r0, seq)
        s_prev = ssm_ref[b]
        cmb = cm_s[rb, :]
        bmb = bm_s[rb, :]
        xwb = xw_s[rb, :]
        yint_s[rb, :] = jnp.concatenate(
            [_dot_nt(cmb[:, g * SSD_STATE:(g + 1) * SSD_STATE], s_prev[g * gw:(g + 1) * gw, :])
             for g in range(SSD_GROUPS)], axis=1)
        upd = [_dot_tn(xwb[:, g * gw:(g + 1) * gw], bmb[:, g * SSD_STATE:(g + 1) * SSD_STATE])
               for g in range(SSD_GROUPS)]
        drow = ead_s[rb, :][seq - 1:seq, :]
        for hh in range(SSD_HEADS):
            g, r = divmod(hh, heads_per_group)
            hs = slice(hh * SSD_HEAD_DIM, (hh + 1) * SSD_HEAD_DIM)
            ssm_o[b, hs, :] = (s_prev[hs, :] * drow[:, hh:hh + 1]
                               + upd[g][r * SSD_HEAD_DIM:(r + 1) * SSD_HEAD_DIM, :])
        g_prev = gst_ref[b]
        g2 = jnp.concatenate([g_prev, g_prev], axis=1)
        s_bd = jnp.where(bd_mask, jnp.concatenate([g2, g2], axis=1), 0.0)
        oint_s[rb, :] = _dot(qe_s[rb, :], s_bd)
        u2 = jnp.where(bd_mask, _dot_tn(kd_s[rb, :], proj_s[rb, P_GV:P_GV + GLA_WIDTH]), 0.0)
        w = u2[:, :LANES] + u2[:, LANES:]
        erow = jnp.broadcast_to(ebt_s[rb, :][seq - 1:seq, :], (LANES, LANES))
        ecol = jnp.sum(jnp.where(eye, erow, 0.0), axis=-1, keepdims=True)
        gla_o[b] = g_prev * ecol + (w[:, :GLA_DV] + w[:, GLA_DV:])
        knb = kn_s[rb, :]
        vnb = vn_s[rb, :]
        kk_s[0:WINDOW, :] = kc_ref[b]
        kk_s[WINDOW:n_keys, :] = knb
        vv_s[0:WINDOW, :] = vc_ref[b]
        vv_s[WINDOW:n_keys, :] = vnb
        qa = qn_s[rb, 0:LANES]
        qb = qn_s[rb, LANES:2 * LANES]
        qs = jnp.concatenate([jnp.where(lo8, qa, 0.0), jnp.where(lo8, qb, 0.0),
                              jnp.where(lo8, 0.0, qa), jnp.where(lo8, 0.0, qb)], axis=0)
        l = _dot_nt(kk_s[...], qs) * (SWA_HEAD_DIM ** -0.5) + bias_s[...]
        m = jnp.maximum(jnp.max(l, axis=0, keepdims=True), sink_row)
        e = jnp.exp(l - m)
        den = jnp.sum(e, axis=0, keepdims=True) + jnp.exp(sink_row - m)
        o = _dot_tn(e / den, vv_s[...])
        tile_a = jnp.where(lo8, o[0:seq], o[2 * seq:3 * seq])
        tile_b = jnp.where(lo8, o[seq:2 * seq], o[3 * seq:4 * seq])
        oswa_s[rb, :] = jnp.concatenate([tile_a, tile_b], axis=1)
        ko_ref[b, 0:WINDOW - seq, :] = kc_ref[b, seq:WINDOW, :]
        ko_ref[b, WINDOW - seq:WINDOW, :] = knb
        vo_ref[b, 0:WINDOW - seq, :] = vc_ref[b, seq:WINDOW, :]
        vo_ref[b, WINDOW - seq:WINDOW, :] = vnb
        return carry

    lax.fori_loop(0, n_seq, per_seq, 0)

    y = y_intra + yint_s[...] * eacum_e + dsk_ref[...] * xs
    y = y * _silu(proj_s[:, P_Z:P_Z + SSD_WIDTH])
    y_ssd = _group_rmsnorm(y, snw_ref[...])
    o = o_intra + oint_s[...]
    y_gla = o * _head_rms_scale(o) * gnw_ref[...] * _silu(proj_s[:, P_GG:P_GG + GLA_WIDTH])
    y_swa = oswa_s[...] * _silu(proj_s[:, P_SG:P_SG + SWA_WIDTH])
    mix = jnp.concatenate([y_ssd, y_gla, y_swa], axis=1).astype(BF16)
    y_ref[...] = _epilogue(h_ref[...], mix, p_ref[...], wout_ref, wpg_ref, wpe_ref)


def _const_spec(shape):
    nd = len(shape)
    return pl.BlockSpec(shape, lambda *_: (0,) * nd)


def _resident_spec(shape):
    nd = len(shape)
    return pl.BlockSpec(shape, lambda *_: (0,) * nd, pipeline_mode=pl.Buffered(1))


def _smem_spec():
    return pl.BlockSpec(memory_space=pltpu.SMEM)


def _layer_weights(i, w):
    ops = [w["norm_w"][i], w["w_in"][i], w["conv_w"][i], w["conv_b"][i], w["dt_bias"][i], w["a_log"][i],
           w["d_skip"][i], w["ssd_norm_w"][i], w["gla_w_gk"][i], w["gla_b_gk"][i], w["gla_norm_w"][i],
           w["q_norm_w"][i], w["k_norm_w"][i], w["w_out"][i], w["w_pe"][i], w["w_pg"][i]]
    return ops, [_resident_spec(o.shape) if o.dtype == BF16 else _const_spec(o.shape) for o in ops]


def _prompt_tile_rows(seq_len):
    for cand in (2 * BLK, BLK):
        if seq_len % cand == 0:
            return cand
    raise ValueError("prompt length must be a multiple of 128")


def _prompt_layer(layer, depth, h, p_all, prev_states, bucket, rel, sinks, wops, wspecs):
    bsz, seq_len, _ = h.shape
    t_rows = _prompt_tile_rows(seq_len)
    n_chunks = seq_len // t_rows
    kern = functools.partial(_prompt_kernel, t_rows, n_chunks, len(prev_states))
    row_spec = lambda w: pl.BlockSpec((1, t_rows, w), lambda b, c: (b, c, 0))
    p_spec = pl.BlockSpec((1, 1, t_rows, PLE_DIM), lambda b, c: (layer, b, c, 0))
    per_seq = lambda s: pl.BlockSpec((1, 1) + s, lambda b, c: (layer, b) + (0,) * len(s))
    state_shapes = ((SSD_WIDTH, SSD_STATE), (SSD_CONV - 1, SSD_CONV_DIM), (GLA_HEADS * GLA_DK, GLA_DV),
                    (LANES, WINDOW), (LANES, WINDOW))
    out_shape = (jax.ShapeDtypeStruct((bsz, seq_len, D_MODEL), F32),) + tuple(
        jax.ShapeDtypeStruct((depth, bsz) + s, F32) for s in state_shapes)
    return pl.pallas_call(
        kern,
        grid=(bsz, n_chunks),
        in_specs=[row_spec(D_MODEL), p_spec, _const_spec(bucket.shape), _smem_spec(), _smem_spec()]
        + wspecs + [pl.BlockSpec(memory_space=pl.ANY)] * len(prev_states),
        out_specs=(row_spec(D_MODEL),) + tuple(per_seq(s) for s in state_shapes),
        out_shape=out_shape,
        input_output_aliases={N_PROMPT_INPUTS + k: 1 + k for k in range(len(prev_states))},
        scratch_shapes=[
            pltpu.VMEM((t_rows, PROJ_W), F32),
            pltpu.VMEM((t_rows + SUBLANES, XBC_W), F32),
            pltpu.VMEM((BLK, SSD_WIDTH), F32),
            pltpu.VMEM((LANES, GLA_WIDTH), F32),
            pltpu.VMEM((2 * BLK, LANES), F32),
            pltpu.VMEM((2 * BLK, LANES), F32),
            pltpu.VMEM((t_rows, D_MODEL), BF16),
            pltpu.VMEM((2 * SWA_HEADS, BLK, 2 * BLK), F32),
        ],
        compiler_params=pltpu.CompilerParams(
            dimension_semantics=("arbitrary", "arbitrary"), vmem_limit_bytes=VMEM_LIMIT_BYTES),
        name="prompt_layer",
    )(h, p_all, bucket, rel, sinks, *wops, *prev_states)


def _sample_layer(h, p, ssm, conv_pad, gla, kc, vc, bucket, rel, sinks, wops, wspecs, seq):
    rows_total = h.shape[0]
    n_seq = BLK // seq
    rows = n_seq * seq
    bsz = rows_total // seq
    kern = functools.partial(_sample_kernel, seq, n_seq)
    row_spec = lambda w: pl.BlockSpec((rows, w), lambda s: (s, 0))
    seq_spec = lambda a, c: pl.BlockSpec((n_seq, a, c), lambda s: (s, 0, 0))
    out_shape = (
        jax.ShapeDtypeStruct((rows_total, D_MODEL), F32),
        jax.ShapeDtypeStruct((bsz, SSD_WIDTH, SSD_STATE), F32),
        jax.ShapeDtypeStruct((rows_total, SSD_CONV_DIM), F32),
        jax.ShapeDtypeStruct((bsz, GLA_HEADS * GLA_DK, GLA_DV), F32),
        jax.ShapeDtypeStruct((bsz, WINDOW, LANES), F32),
        jax.ShapeDtypeStruct((bsz, WINDOW, LANES), F32),
    )
    n_keys = WINDOW + seq
    f32_scratch = lambda r, c: pltpu.VMEM((r, c), F32)
    return pl.pallas_call(
        kern,
        grid=(rows_total // rows,),
        in_specs=[row_spec(D_MODEL), row_spec(PLE_DIM), seq_spec(SSD_WIDTH, SSD_STATE), row_spec(SSD_CONV_DIM),
                  seq_spec(GLA_HEADS * GLA_DK, GLA_DV), seq_spec(WINDOW, LANES), seq_spec(WINDOW, LANES),
                  _const_spec(bucket.shape), _smem_spec(), _smem_spec()] + wspecs,
        out_specs=(row_spec(D_MODEL), seq_spec(SSD_WIDTH, SSD_STATE), row_spec(SSD_CONV_DIM),
                   seq_spec(GLA_HEADS * GLA_DK, GLA_DV), seq_spec(WINDOW, LANES), seq_spec(WINDOW, LANES)),
        out_shape=out_shape,
        scratch_shapes=[
            f32_scratch(rows, PROJ_W),
            f32_scratch(rows + SUBLANES, XBC_W),
            f32_scratch(rows + SUBLANES, XBC_W),
            f32_scratch(rows, LANES), f32_scratch(rows, LANES), f32_scratch(rows, SSD_WIDTH),
            f32_scratch(rows, LANES), f32_scratch(rows, SSD_WIDTH),
            f32_scratch(rows, LANES), f32_scratch(rows, LANES), f32_scratch(rows, LANES),
            f32_scratch(rows, GLA_WIDTH),
            f32_scratch(rows, SWA_WIDTH), f32_scratch(rows, LANES), f32_scratch(rows, LANES),
            f32_scratch(rows, SWA_WIDTH),
            f32_scratch(n_keys, LANES), f32_scratch(n_keys, LANES),
            f32_scratch(n_keys, SWA_HEADS * seq),
        ],
        compiler_params=pltpu.CompilerParams(
            dimension_semantics=("arbitrary",), vmem_limit_bytes=VMEM_LIMIT_BYTES),
        name="sample_layer",
    )(h, p, ssm, conv_pad, gla, kc, vc, bucket, rel, sinks, *wops)


SWA_HEAD_ORDER = (0, 2, 1, 3)


def _win_tile_runs():
    sizes = (SSD_WIDTH, SSD_CONV_DIM, SSD_HEADS, GLA_HEADS * GLA_DK, GLA_HEADS * GLA_DK, GLA_WIDTH, GLA_WIDTH,
             GLA_RANK, SWA_WIDTH, SWA_KV_HEADS * SWA_HEAD_DIM, SWA_KV_HEADS * SWA_HEAD_DIM, SWA_WIDTH)
    offs = np.concatenate([[0], np.cumsum(sizes)])
    seg = lambda k: np.arange(offs[k], offs[k + 1])
    z, xbc, dt, gq, gk, gv, gg, glr, sq, sk, sv, sg = [seg(k) for k in range(len(sizes))]
    heads = lambda a: np.concatenate([a[h * SWA_HEAD_DIM:(h + 1) * SWA_HEAD_DIM] for h in SWA_HEAD_ORDER])
    pad = np.full(LANES - SSD_HEADS - GLA_RANK, -1)
    src = np.concatenate([xbc, z, gq, gk, gv, gg, heads(sq), sk, sv, heads(sg), dt, glr, pad])
    assert src.size == XBC_W + PROJ_W
    tiles = []
    for j in range(src.size // LANES):
        idx = src[j * LANES:(j + 1) * LANES]
        cuts = [0] + [k for k in range(1, LANES) if (idx[k] != idx[k - 1] + 1 and not (idx[k] == -1 == idx[k - 1]))]
        runs = [(int(idx[a]), b - a) for a, b in zip(cuts, cuts[1:] + [LANES])]
        assert all(n % SUBLANES == 0 and (s < 0 or s % SUBLANES == 0) for s, n in runs)
        tiles.append(runs)
    return tiles


def _win_prep_kernel(tile_runs, wt_ref, out_ref):
    for j, runs in enumerate(tile_runs):
        parts = [jnp.zeros((n, D_MODEL), F32) if s < 0 else wt_ref[0, s:s + n, :] for s, n in runs]
        tile = parts[0] if len(parts) == 1 else jnp.concatenate(parts, axis=0)
        out_ref[0, :, j * LANES:(j + 1) * LANES] = tile.T.astype(BF16)


def _prepare_w_in(w_in):
    depth, d_model, d_in = w_in.shape
    w_t = jnp.swapaxes(w_in, 1, 2)
    return pl.pallas_call(
        functools.partial(_win_prep_kernel, _win_tile_runs()),
        grid=(depth,),
        in_specs=[pl.BlockSpec((1, d_in, d_model), lambda l: (l, 0, 0))],
        out_specs=pl.BlockSpec((1, d_model, XBC_W + PROJ_W), lambda l: (l, 0, 0)),
        out_shape=jax.ShapeDtypeStruct((depth, d_model, XBC_W + PROJ_W), BF16),
        compiler_params=pltpu.CompilerParams(
            dimension_semantics=("arbitrary",), vmem_limit_bytes=VMEM_LIMIT_BYTES),
        name="w_in_prep",
    )(w_t)


def _prepare_weights(norm_w, w_in, conv_w, conv_b, dt_bias, a_log, d_skip, ssd_norm_w, gla_w_gk, gla_b_gk,
                     gla_norm_w, q_norm_w, k_norm_w, w_out, w_pe, w_pg):
    mix_w = SSD_WIDTH + GLA_WIDTH
    w_out_p = jnp.concatenate(
        [w_out[:, :mix_w, :]] + [w_out[:, mix_w + h * SWA_HEAD_DIM:mix_w + (h + 1) * SWA_HEAD_DIM, :]
                                 for h in SWA_HEAD_ORDER], axis=1).astype(BF16)
    lane_pad = lambda x: jnp.pad(x, ((0, 0), (0, LANES - x.shape[-1])))[:, None, :]
    wgk_p = jnp.pad(gla_w_gk, ((0, 0), (LR_LANE0, LANES - LR_LANE0 - GLA_RANK), (0, 0))).astype(BF16)
    return dict(
        norm_w=norm_w[:, None, :], w_in=_prepare_w_in(w_in), conv_w=conv_w, conv_b=conv_b[:, None, :],
        dt_bias=lane_pad(dt_bias), a_log=lane_pad(a_log),
        d_skip=jnp.repeat(d_skip, SSD_HEAD_DIM, axis=-1)[:, None, :], ssd_norm_w=ssd_norm_w[:, None, :],
        gla_w_gk=wgk_p, gla_b_gk=gla_b_gk[:, None, :],
        gla_norm_w=jnp.tile(gla_norm_w, (1, GLA_HEADS))[:, None, :],
        q_norm_w=jnp.tile(q_norm_w, (1, SWA_HEADS))[:, None, :],
        k_norm_w=jnp.tile(k_norm_w, (1, SWA_KV_HEADS))[:, None, :],
        w_out=w_out_p, w_pe=w_pe.astype(BF16), w_pg=w_pg.astype(BF16))


def kernel(x_prompt, x_sample, state_ssm, state_conv, state_gla, cache_swa_k, cache_swa_v, p_prompt, p_sample, rel_bias, norm_w, w_in, conv_w, conv_b, dt_bias, a_log, d_skip, ssd_norm_w, gla_w_gk, gla_b_gk, gla_norm_w, q_norm_w, k_norm_w, attn_sinks, w_out, w_pe, w_pg):
    depth = w_in.shape[0]
    bp, seq_p, _ = x_prompt.shape
    bs, seq_s, _ = x_sample.shape
    assert seq_s == SUBLANES and BLK % seq_s == 0 and (bs * seq_s) % BLK == 0
    assert cache_swa_k.shape[2] == WINDOW
    w = _prepare_weights(norm_w, w_in, conv_w, conv_b, dt_bias, a_log, d_skip, ssd_norm_w, gla_w_gk, gla_b_gk,
                         gla_norm_w, q_norm_w, k_norm_w, w_out, w_pe, w_pg)
    rel_flat = rel_bias.reshape(-1)
    dist_p = WINDOW + np.arange(BLK)[:, None] - np.arange(2 * BLK)[None, :]
    bucket_p = jnp.asarray(_bucket_table(dist_p))
    dist_s = WINDOW + np.arange(seq_s)[None, :] - np.arange(WINDOW + seq_s)[:, None]
    bucket_s = jnp.asarray(np.tile(_bucket_table(dist_s), (1, SWA_HEADS)))

    hp = x_prompt
    hs = x_sample.reshape(bs * seq_s, D_MODEL)
    states_p = ()
    outs_s = [[] for _ in range(5)]
    for i in range(depth):
        wops, wspecs = _layer_weights(i, w)
        hp, *states_p = _prompt_layer(i, depth, hp, p_prompt, tuple(states_p), bucket_p, rel_flat, attn_sinks[i],
                                      wops, wspecs)
        conv_pad = jnp.pad(state_conv[i], ((0, 0), (seq_s - (SSD_CONV - 1), 0), (0, 0)))
        hs, ssm, xbc_raw, gla, ko, vo = _sample_layer(
            hs, p_sample[i].reshape(bs * seq_s, PLE_DIM),
            state_ssm[i].reshape(bs, SSD_WIDTH, SSD_STATE), conv_pad.reshape(bs * seq_s, SSD_CONV_DIM),
            state_gla[i].reshape(bs, GLA_HEADS * GLA_DK, GLA_DV),
            cache_swa_k[i].reshape(bs, WINDOW, LANES), cache_swa_v[i].reshape(bs, WINDOW, LANES),
            bucket_s, rel_flat, attn_sinks[i], wops, wspecs, seq_s)
        conv = xbc_raw.reshape(bs, seq_s, SSD_CONV_DIM)[:, seq_s - (SSD_CONV - 1):, :]
        for lst, val in zip(outs_s, (ssm.reshape(bs, SSD_HEADS, SSD_HEAD_DIM, SSD_STATE), conv,
                                     gla.reshape(bs, GLA_HEADS, GLA_DK, GLA_DV),
                                     ko.reshape(bs, WINDOW, SWA_KV_HEADS, SWA_HEAD_DIM),
                                     vo.reshape(bs, WINDOW, SWA_KV_HEADS, SWA_HEAD_DIM))):
            lst.append(val)
    ssm_p, conv_p, gla_p, kt_p, vt_p = states_p
    unpack_kv = lambda a: jnp.transpose(a.reshape(depth, bp, SWA_KV_HEADS, SWA_HEAD_DIM, WINDOW), (0, 1, 4, 2, 3))
    outs_p = (ssm_p.reshape(depth, bp, SSD_HEADS, SSD_HEAD_DIM, SSD_STATE), conv_p,
              gla_p.reshape(depth, bp, GLA_HEADS, GLA_DK, GLA_DV), unpack_kv(kt_p), unpack_kv(vt_p))
    return (hp, hs.reshape(bs, seq_s, D_MODEL)) + outs_p + tuple(jnp.stack(v) for v in outs_s)
```

```python
import functools
import math

import numpy as np
import jax
import jax.numpy as jnp
from jax import lax
from jax.experimental import pallas as pl
from jax.experimental.pallas import tpu as pltpu

D_MODEL = 1024
DEPTH = 2
SSD_HEADS = 8
SSD_HEAD_DIM = 64
SSD_WIDTH = SSD_HEADS * SSD_HEAD_DIM
SSD_GROUPS = 2
SSD_STATE = 64
SSD_CONV = 4
SSD_CONV_DIM = SSD_WIDTH + 2 * SSD_GROUPS * SSD_STATE
SSD_CHUNK = 128
GLA_HEADS = 4
GLA_DK = 32
GLA_DV = 64
GLA_WIDTH = GLA_HEADS * GLA_DV
GLA_RANK = 16
GLA_GATE_NORM = 16.0
GLA_CHUNK = 64
SWA_HEADS = 4
SWA_KV_HEADS = 2
SWA_HEAD_DIM = 64
SWA_WIDTH = SWA_HEADS * SWA_HEAD_DIM
WINDOW = 128
REL_BUCKETS = 32
REL_MAX_DIST = 128
PLE_DIM = 256
EPS = 1e-6

LANES = 128
SUBLANES = 8
HALF = LANES // 2
BLK = 128
VMEM_LIMIT_BYTES = 56 * 1024 * 1024

XBC_W = SSD_CONV_DIM
P_Z = 0
P_GQ = P_Z + SSD_WIDTH
P_GK = P_GQ + LANES
P_GV = P_GK + LANES
P_GG = P_GV + GLA_WIDTH
P_SQ = P_GG + GLA_WIDTH
P_SK = P_SQ + SWA_WIDTH
P_SV = P_SK + LANES
P_SG = P_SV + LANES
P_DTLR = P_SG + SWA_WIDTH
PROJ_W = P_DTLR + LANES
LR_LANE0 = SSD_HEADS

F32 = jnp.float32
BF16 = jnp.bfloat16
NEG_INF = float("-inf")
N_PROMPT_INPUTS = 22
PROMPT_CHUNK_ROWS = 2 * BLK
TICKS_PER_ITEM = 2
NT_DIMS = (((1,), (1,)), ((), ()))
TN_DIMS = (((0,), (0,)), ((), ()))


def _iota(shape, dim):
    return lax.broadcasted_iota(jnp.int32, shape, dim)


def _div(x, d):
    return x >> (d.bit_length() - 1)


def _mod(x, d):
    return x & (d - 1)


def _softplus(x):
    e = jnp.exp(-jnp.abs(x))
    u = 1.0 + e
    d = u - 1.0
    log1p_e = jnp.where(d == 0.0, e, jnp.log(u) * (e / jnp.where(d == 0.0, 1.0, d)))
    return jnp.maximum(x, 0.0) + log1p_e


def _log_sigmoid(x):
    return jnp.minimum(x, 0.0) - jnp.log(1.0 + jnp.exp(-jnp.abs(x)))


def _silu(x):
    return x * jax.nn.sigmoid(x)


def _dot(a, b):
    return jnp.dot(a.astype(BF16), b.astype(BF16), preferred_element_type=F32)


def _dot_nt(a, b):
    return lax.dot_general(a.astype(BF16), b.astype(BF16), NT_DIMS, preferred_element_type=F32)


def _dot_tn(a, b):
    return lax.dot_general(a.astype(BF16), b.astype(BF16), TN_DIMS, preferred_element_type=F32)


def _dot_exact(sel, x):
    x1 = x.astype(BF16)
    r1 = x - x1.astype(F32)
    x2 = r1.astype(BF16)
    x3 = (r1 - x2.astype(F32)).astype(BF16)
    dot = functools.partial(jnp.dot, sel, preferred_element_type=F32)
    return dot(x1) + dot(x2) + dot(x3)


def _expand_heads(x, n_heads):
    rows = x.shape[0]
    lo = _iota((rows, LANES), 1) < HALF
    tiles = []
    for j in range(n_heads // 2):
        a = jnp.broadcast_to(x[:, 2 * j:2 * j + 1], (rows, LANES))
        b = jnp.broadcast_to(x[:, 2 * j + 1:2 * j + 2], (rows, LANES))
        tiles.append(jnp.where(lo, a, b))
    return jnp.concatenate(tiles, axis=1)


def _head_rms_scale(x):
    rows, width = x.shape
    lo = _iota((rows, LANES), 1) < HALF
    outs = []
    for j in range(width // LANES):
        t = x[:, j * LANES:(j + 1) * LANES]
        sq = t * t
        s_lo = jnp.sum(jnp.where(lo, sq, 0.0), axis=-1, keepdims=True)
        s_hi = jnp.sum(jnp.where(lo, 0.0, sq), axis=-1, keepdims=True)
        outs.append(lax.rsqrt(jnp.where(lo, s_lo, s_hi) * (1.0 / HALF) + EPS))
    return outs[0] if len(outs) == 1 else jnp.concatenate(outs, axis=1)


def _group_rmsnorm(y, w):
    gw = SSD_WIDTH // SSD_GROUPS
    outs = []
    for g in range(SSD_GROUPS):
        t = y[:, g * gw:(g + 1) * gw]
        ms = jnp.sum(t * t, axis=-1, keepdims=True) * (1.0 / gw)
        outs.append(t * lax.rsqrt(ms + EPS))
    return jnp.concatenate(outs, axis=1) * w


def _rel_bucket_np(dist):
    n = np.maximum(dist, 0)
    exact = REL_BUCKETS // 2
    nf = np.maximum(n, 1).astype(np.float64)
    large = exact + (np.log(nf / exact) / math.log(REL_MAX_DIST / exact) * (REL_BUCKETS - exact)).astype(np.int32)
    large = np.minimum(large, REL_BUCKETS - 1)
    return np.where(n < exact, n, large).astype(np.int32)


def _bucket_table(dist):
    return np.where((dist >= 0) & (dist < WINDOW), _rel_bucket_np(dist), -1).astype(np.int32)


def _no_tick():
    pass


def _ssd_intra(xbc_c, dtlr, dtb, a_row, pair_mask, tri, total_of, tick=_no_tick):
    xs = xbc_c[:, :SSD_WIDTH]
    bm = xbc_c[:, SSD_WIDTH:SSD_WIDTH + LANES]
    cm = xbc_c[:, SSD_WIDTH + LANES:]
    lane = _iota((BLK, LANES), 1)
    lo = lane < HALF
    dtv = _softplus(dtlr + dtb)
    adt = dtv * a_row
    acum = _dot_exact(tri, adt)
    tick()
    acum_t = acum.T
    eacum = jnp.exp(acum)
    tail = jnp.exp(total_of(adt, acum) - acum)
    dtv_e = _expand_heads(dtv, SSD_HEADS)
    eacum_e = _expand_heads(eacum, SSD_HEADS)
    tail_e = _expand_heads(tail, SSD_HEADS)
    tick()
    xdt = xs * dtv_e
    xw = xdt * tail_e
    cb = [_dot_nt(jnp.where(lo, cm, 0.0), bm), _dot_nt(jnp.where(lo, 0.0, cm), bm)]
    y_pairs = []
    for j in range(SSD_HEADS // 2):
        tick()
        g = (2 * j) // (SSD_HEADS // SSD_GROUPS)
        ms = []
        for k in range(2):
            h = 2 * j + k
            seg = acum[:, h:h + 1] - acum_t[h:h + 1, :]
            dec = jnp.where(pair_mask, jnp.exp(seg), 0.0)
            ms.append((cb[g] * dec).astype(BF16))
        xp = xdt[:, j * LANES:(j + 1) * LANES]
        rhs = jnp.concatenate([jnp.where(lo, xp, 0.0), jnp.where(lo, 0.0, xp)], axis=0)
        y_pairs.append(_dot(jnp.concatenate(ms, axis=1), rhs))
    y_intra = jnp.concatenate(y_pairs, axis=1)
    return y_intra, xs, bm, cm, xw, eacum, eacum_e


def _gla_intra(gq, gk, gv, glog, tri, total_of, att_masks, tick=_no_tick):
    bcs = _dot_exact(tri, glog)
    tick()
    eb = jnp.exp(bcs)
    qe = gq * (GLA_DK ** -0.5) * eb
    ke = gk * jnp.exp(-bcs)
    btot = total_of(glog, bcs)
    kd = gk * jnp.exp(btot - bcs)
    lane_k = _iota((GLA_CHUNK, LANES), 1)
    lane_v = _iota((GLA_CHUNK, GLA_WIDTH), 1)
    outs = []
    for c2 in range(BLK // GLA_CHUNK):
        tick()
        rs = slice(c2 * GLA_CHUNK, (c2 + 1) * GLA_CHUNK)
        ke_c = ke[rs]
        v_c = gv[rs]
        kbd = jnp.concatenate(
            [jnp.where(_div(lane_k, GLA_DK) == h, ke_c, 0.0) for h in range(GLA_HEADS)], axis=0)
        att = _dot_nt(qe[rs], kbd)
        att = jnp.where(att_masks[c2], att, 0.0)
        vbd = jnp.concatenate(
            [jnp.where(_div(lane_v, GLA_DV) == h, v_c, 0.0) for h in range(GLA_HEADS)], axis=0)
        outs.append(_dot(att, vbd))
    return jnp.concatenate(outs, axis=0), qe, kd, jnp.exp(btot)


def _build_bias(bucket, rel_ref, n_heads_in_lanes=None):
    accs = [jnp.full(bucket.shape, NEG_INF, F32) for _ in range(SWA_HEADS)]
    for b in range(REL_BUCKETS):
        hit = bucket == b
        for h in range(SWA_HEADS):
            accs[h] = jnp.where(hit, rel_ref[b * SWA_HEADS + h], accs[h])
    return accs


def _epilogue(h, mix, p, wout_ref, wpg_ref, wpe_ref):
    h1 = h + jnp.dot(mix, wout_ref[...], preferred_element_type=F32)
    gate = jax.nn.sigmoid(jnp.dot(h1.astype(BF16), wpg_ref[...], preferred_element_type=F32))
    pe = jnp.dot(p.astype(BF16), wpe_ref[...], preferred_element_type=F32)
    return h1 + gate * pe


def _project(h, nw_ref, win_ref, xbc_s, proj_s, rows):
    ms = jnp.mean(h * h, axis=-1, keepdims=True)
    u = (h * lax.rsqrt(ms + EPS) * nw_ref[...]).astype(BF16)
    xbc_s[SUBLANES:SUBLANES + rows, :] = jnp.dot(u, win_ref[:, :XBC_W], preferred_element_type=F32)
    proj_s[...] = jnp.dot(u, win_ref[:, XBC_W:], preferred_element_type=F32)


def _prompt_kernel(chunks_per_seq, n_aliased, *refs):
    (ha_ref, hc_ref, p_ref, bucket_ref, rel_ref, sink_ref, nw_ref, win_ref, cw_ref, cb_ref, dtb_ref,
     alog_ref, dsk_ref, snw_ref, wgk_ref, bgk_ref, gnw_ref, qnw_ref, knw_ref, wout_ref,
     wpe_ref, wpg_ref) = refs[:N_PROMPT_INPUTS]
    (y_ref, ssm_ref, conv_ref, gla_ref, ko_ref, vo_ref,
     proj_e, proj_o, xbc_e, xbc_o, mix_e, mix_o, u_s, h1_s, hist_s, st_s, s2_s, kext_s, vext_s,
     bias_s) = refs[N_PROMPT_INPUTS + n_aliased:]
    k_idx = pl.program_id(0)

    @pl.when(k_idx == 0)
    def _():
        accs = _build_bias(bucket_ref[...], rel_ref)
        own_block = _iota((BLK, 2 * BLK), 1) >= BLK
        for hh in range(SWA_HEADS):
            bias_s[hh] = accs[hh]
            bias_s[SWA_HEADS + hh] = jnp.where(own_block, accs[hh], NEG_INF)
        for ref in (proj_o, xbc_o, mix_e, mix_o, hist_s, st_s, s2_s, kext_s, vext_s):
            ref[...] = jnp.zeros(ref.shape, ref.dtype)

    row = _iota((BLK, BLK), 0)
    col = _iota((BLK, BLK), 1)
    causal = row >= col
    tri_ssd = jnp.where(causal, 1.0, 0.0).astype(BF16)
    tri_gla = jnp.where(causal & (_div(row, GLA_CHUNK) == _div(col, GLA_CHUNK)), 1.0, 0.0).astype(BF16)
    lo = col < HALF
    lane_row = _iota((1, LANES), 1)
    a_row = jnp.where(lane_row < SSD_HEADS, -jnp.exp(alog_ref[...]), 0.0)
    st_mask = (_iota((BLK, SSD_WIDTH), 0) < SSD_STATE) == (_iota((BLK, SSD_WIDTH), 1) < SSD_WIDTH // SSD_GROUPS)
    bd_mask = _div(_iota((LANES, GLA_WIDTH), 0), GLA_DK) == _div(_iota((LANES, GLA_WIDTH), 1), GLA_DV)
    att_t = _iota((GLA_CHUNK, GLA_WIDTH), 0)
    att_s = _mod(_iota((GLA_CHUNK, GLA_WIDTH), 1), GLA_CHUNK)
    att_mask = att_s <= att_t
    lo2 = _iota((2 * BLK, LANES), 1) < HALF

    def ssd_total(adt, acum):
        return jnp.broadcast_to(acum[BLK - 1:BLK, :], acum.shape)

    def gla_total(glog, bcs):
        return jnp.concatenate(
            [jnp.broadcast_to(bcs[(c2 + 1) * GLA_CHUNK - 1:(c2 + 1) * GLA_CHUNK, :], (GLA_CHUNK, LANES))
             for c2 in range(BLK // GLA_CHUNK)], axis=0)

    def block(blk, proj_s, xbc_s, mix_s, starts_sequence, tick):
        rows = slice(blk * BLK, (blk + 1) * BLK)
        cw = cw_ref[...]
        tick()
        if blk == 0:
            xwin = jnp.concatenate([hist_s[...], xbc_s[0:BLK, :]], axis=0)
        else:
            xwin = xbc_s[blk * BLK - SUBLANES:(blk + 1) * BLK, :]
        acc = xwin[SUBLANES - 3:SUBLANES - 3 + BLK, :] * cw[0:1, :]
        for k in range(1, SSD_CONV):
            acc = acc + xwin[SUBLANES - 3 + k:SUBLANES - 3 + k + BLK, :] * cw[k:k + 1, :]
        xbc_c = _silu(acc + cb_ref[...])
        tick()
        dtlr = proj_s[rows, P_DTLR:P_DTLR + LANES]
        y_intra, xs, bm, cm, xw, eacum, eacum_e = _ssd_intra(
            xbc_c, dtlr, dtb_ref[...], a_row, causal, tri_ssd, ssd_total, tick)
        tick()
        st = st_s[...]
        y = y_intra + _dot(cm, st) * eacum_e + dsk_ref[...] * xs
        st_s[...] = st * eacum_e[BLK - 1:BLK, :] + jnp.where(st_mask, _dot(bm.T, xw), 0.0)
        tick()
        y = y * _silu(proj_s[rows, P_Z:P_Z + SSD_WIDTH])
        mix_s[rows, 0:SSD_WIDTH] = _group_rmsnorm(y, snw_ref[...]).astype(BF16)
        tick()
        gk = proj_s[rows, P_GK:P_GK + LANES]
        gv = proj_s[rows, P_GV:P_GV + GLA_WIDTH]
        glog = _log_sigmoid(_dot(dtlr, wgk_ref[...]) + bgk_ref[...]) * (1.0 / GLA_GATE_NORM)
        o_intra, qe, kd, ebt = _gla_intra(
            proj_s[rows, P_GQ:P_GQ + LANES], gk, gv, glog, tri_gla, gla_total, [att_mask, att_mask], tick)
        kd_t = kd.T
        ebt_t = ebt.T
        s2 = s2_s[...]
        o_parts = []
        for c2 in range(BLK // GLA_CHUNK):
            tick()
            rs = slice(c2 * GLA_CHUNK, (c2 + 1) * GLA_CHUNK)
            o_parts.append(o_intra[rs] + _dot(qe[rs], s2))
            u2 = _dot(jnp.where(_div(col, GLA_CHUNK) == c2, kd_t, 0.0), gv)
            last = (c2 + 1) * GLA_CHUNK - 1
            s2 = s2 * ebt_t[:, last:last + 1] + jnp.where(bd_mask, u2, 0.0)
        s2_s[...] = s2
        o = jnp.concatenate(o_parts, axis=0)
        y_gla = o * _head_rms_scale(o) * gnw_ref[...] * _silu(proj_s[rows, P_GG:P_GG + GLA_WIDTH])
        mix_s[rows, SSD_WIDTH:SSD_WIDTH + GLA_WIDTH] = y_gla.astype(BF16)
        tick()
        sq = proj_s[rows, P_SQ:P_SQ + SWA_WIDTH]
        qn = sq * _head_rms_scale(sq) * qnw_ref[...] * (SWA_HEAD_DIM ** -0.5)
        sk = proj_s[rows, P_SK:P_SK + LANES]
        kn = sk * _head_rms_scale(sk) * knw_ref[...]
        vn = proj_s[rows, P_SV:P_SV + LANES]
        kext_s[BLK:2 * BLK, :] = kn
        vext_s[BLK:2 * BLK, :] = vn
        kext = kext_s[...]
        vext = vext_s[...]
        qa = qn[:, :LANES]
        qb = qn[:, LANES:]
        qs = jnp.concatenate([jnp.where(lo, qa, 0.0), jnp.where(lo, qb, 0.0),
                              jnp.where(lo, 0.0, qa), jnp.where(lo, 0.0, qb)], axis=0)
        logits = _dot_nt(qs, kext)
        tick()
        if blk == 0 and starts_sequence is not False:
            bias_row0 = jnp.where(starts_sequence, SWA_HEADS, 0)
        else:
            bias_row0 = 0
        es = []
        invs = []
        for hh in range(SWA_HEADS):
            tick()
            sink = sink_ref[hh]
            l = logits[hh * BLK:(hh + 1) * BLK] + bias_s[bias_row0 + hh]
            m = jnp.maximum(jnp.max(l, axis=-1, keepdims=True), sink)
            e = jnp.exp(l - m)
            den = jnp.sum(e, axis=-1, keepdims=True) + jnp.exp(sink - m)
            es.append(e.astype(BF16))
            invs.append(1.0 / den)
        v_stack = jnp.concatenate([jnp.where(lo2, vext, 0.0), jnp.where(lo2, 0.0, vext)], axis=0)
        tile_a = _dot(jnp.concatenate([es[0], es[2]], axis=1), v_stack) * jnp.where(lo, invs[0], invs[2])
        tile_b = _dot(jnp.concatenate([es[1], es[3]], axis=1), v_stack) * jnp.where(lo, invs[1], invs[3])
        oa = jnp.concatenate([tile_a, tile_b], axis=1)
        y_swa = oa * _silu(proj_s[rows, P_SG:P_SG + SWA_WIDTH])
        mix_s[rows, SSD_WIDTH + GLA_WIDTH:] = y_swa.astype(BF16)
        kext_s[0:BLK, :] = kn
        vext_s[0:BLK, :] = vn

    chunk = proj_e.shape[0]

    def project_items(rows, proj_s, xbc_s):
        def norm():
            h = ha_ref[rows, :]
            ms = jnp.mean(h * h, axis=-1, keepdims=True)
            u_s[...] = (h * lax.rsqrt(ms + EPS) * nw_ref[...]).astype(BF16)

        def cols(dst, lo_c, hi_c, w_off):
            def item():
                dst[:, lo_c:hi_c] = jnp.dot(u_s[...], win_ref[:, w_off + lo_c:w_off + hi_c],
                                            preferred_element_type=F32)
            return item

        step = 2 * LANES
        items = [norm]
        items += [cols(xbc_s, c, min(c + step, XBC_W), 0) for c in range(0, XBC_W, step)]
        items += [cols(proj_s, c, min(c + step, PROJ_W), XBC_W) for c in range(0, PROJ_W, step)]
        return items

    def epilogue_items(rows, mix_s):
        half_w = 2 * LANES

        def residual(c):
            def item():
                h1_s[:, c:c + half_w] = hc_ref[rows, c:c + half_w] + jnp.dot(
                    mix_s[...], wout_ref[:, c:c + half_w], preferred_element_type=F32)
            return item

        def gated(c):
            def item():
                gate = jax.nn.sigmoid(jnp.dot(h1_s[...].astype(BF16), wpg_ref[:, c:c + half_w],
                                              preferred_element_type=F32))
                pe = jnp.dot(p_ref[0, rows, :].astype(BF16), wpe_ref[:, c:c + half_w],
                             preferred_element_type=F32)
                y_ref[rows, c:c + half_w] = h1_s[:, c:c + half_w] + gate * pe
            return item

        col0 = range(0, D_MODEL, half_w)
        return [residual(c) for c in col0] + [gated(c) for c in col0]

    def merge(first, second):
        out = list(first)
        for j, item in enumerate(second):
            out.insert(((j + 1) * len(first)) // len(second) + j, item)
        return out

    def mixer(proj_s, xbc_s, mix_s, starts_sequence, items):
        if starts_sequence is not False:
            keep = jnp.where(starts_sequence, 0.0, 1.0)
            for ref in (hist_s, st_s, s2_s):
                ref[...] = ref[...] * keep
            kext_s[0:BLK, :] = kext_s[0:BLK, :] * keep
            vext_s[0:BLK, :] = vext_s[0:BLK, :] * keep
        queue = list(items)
        calls = [0]

        def tick():
            calls[0] += 1
            if queue and calls[0] % TICKS_PER_ITEM == 0:
                queue.pop(0)()

        for blk in range(chunk // BLK):
            block(blk, proj_s, xbc_s, mix_s, starts_sequence, tick)
        while queue:
            queue.pop(0)()
        hist_s[...] = xbc_s[chunk - SUBLANES:chunk, :]

    def write_states():
        st = st_s[...]
        stc = st[:SSD_STATE] + st[SSD_STATE:]
        ssm_ref[0, 0] = jnp.concatenate([stc, stc], axis=0).T[:, :SSD_STATE]
        conv_ref[0, 0] = hist_s[SUBLANES - (SSD_CONV - 1):SUBLANES, :]
        s2 = s2_s[...]
        w = s2[:, :LANES] + s2[:, LANES:]
        gla_ref[0, 0] = w[:, :GLA_DV] + w[:, GLA_DV:]
        ko_ref[0, 0] = kext_s[0:BLK, :].T
        vo_ref[0, 0] = vext_s[0:BLK, :].T

    even = slice(0, chunk)
    odd = slice(chunk, 2 * chunk)
    mixer(proj_o, xbc_o, mix_o, False, merge(project_items(even, proj_e, xbc_e), epilogue_items(even, mix_e)))
    write_states()
    mixer(proj_e, xbc_e, mix_e, _mod(2 * k_idx, chunks_per_seq) == 0,
          merge(project_items(odd, proj_o, xbc_o), epilogue_items(odd, mix_o)))


def _sample_kernel(seq, n_seq,
                   h_ref, p_ref, ssm_ref, cst_ref, gst_ref, kc_ref, vc_ref, bucket_ref, rel_ref, sink_ref,
                   nw_ref, win_ref, cw_ref, cb_ref, dtb_ref, alog_ref, dsk_ref, snw_ref, wgk_ref, bgk_ref,
                   gnw_ref, qnw_ref, knw_ref, wout_ref, wpe_ref, wpg_ref,
                   y_ref, ssm_o, xbc_o, gla_o, ko_ref, vo_ref,
                   proj_s, xbc_s, cs_s, cm_s, bm_s, xw_s, ead_s, yint_s, qe_s, kd_s, ebt_s, oint_s,
                   qn_s, kn_s, vn_s, oswa_s, kk_s, vv_s, bias_s):
    rows = n_seq * seq
    n_keys = WINDOW + seq

    @pl.when(pl.program_id(0) == 0)
    def _():
        bucket = bucket_ref[...]
        accs = _build_bias(bucket, rel_ref)
        head_of_lane = _div(_iota(bucket.shape, 1), seq)
        out = accs[SWA_HEADS - 1]
        for hh in range(SWA_HEADS - 2, -1, -1):
            out = jnp.where(head_of_lane == hh, accs[hh], out)
        bias_s[...] = out

    xbc_s[0:SUBLANES, :] = jnp.zeros((SUBLANES, XBC_W), F32)
    _project(h_ref[...], nw_ref, win_ref, xbc_s, proj_s, rows)
    cs_s[0:rows, :] = cst_ref[...]
    cs_s[rows:rows + SUBLANES, :] = jnp.zeros((SUBLANES, XBC_W), F32)
    xbc_o[...] = xbc_s[SUBLANES:SUBLANES + rows, :]

    row = _iota((BLK, BLK), 0)
    col = _iota((BLK, BLK), 1)
    same_seq = _div(row, seq) == _div(col, seq)
    pair_mask = same_seq & (row >= col)
    tri = jnp.where(pair_mask, 1.0, 0.0).astype(BF16)
    ones_seq = jnp.where(same_seq, 1.0, 0.0).astype(BF16)
    lo = col < HALF
    lane_row = _iota((1, LANES), 1)
    a_row = jnp.where(lane_row < SSD_HEADS, -jnp.exp(alog_ref[...]), 0.0)
    bd_mask = _div(_iota((LANES, GLA_WIDTH), 0), GLA_DK) == _div(_iota((LANES, GLA_WIDTH), 1), GLA_DV)
    eye = row == col

    def total_of(x, _cum):
        return _dot_exact(ones_seq, x)

    cw = cw_ref[...]
    t_of_row = _mod(_iota((rows, XBC_W), 0), seq)
    taps = []
    for j in range(SSD_CONV - 1, 0, -1):
        cur = xbc_s[pl.ds(SUBLANES - j, rows), :]
        old = cs_s[pl.ds(SUBLANES - j, rows), :]
        taps.append(jnp.where(t_of_row >= j, cur, old))
    taps.append(xbc_s[pl.ds(SUBLANES, rows), :])
    acc = taps[0] * cw[0:1, :]
    for k in range(1, SSD_CONV):
        acc = acc + taps[k] * cw[k:k + 1, :]
    xbc_c = _silu(acc + cb_ref[...])
    dtlr = proj_s[:, P_DTLR:P_DTLR + LANES]
    y_intra, xs, bm, cm, xw, eacum, eacum_e = _ssd_intra(
        xbc_c, dtlr, dtb_ref[...], a_row, pair_mask, tri, total_of)
    cm_s[...] = cm
    bm_s[...] = bm
    xw_s[...] = xw
    ead_s[...] = eacum

    gk = proj_s[:, P_GK:P_GK + LANES]
    gv = proj_s[:, P_GV:P_GV + GLA_WIDTH]
    glog = _log_sigmoid(_dot(dtlr, wgk_ref[...]) + bgk_ref[...]) * (1.0 / GLA_GATE_NORM)
    att_masks = []
    for c2 in range(BLK // GLA_CHUNK):
        t_loc = _iota((GLA_CHUNK, GLA_WIDTH), 0)
        s_loc = _mod(_iota((GLA_CHUNK, GLA_WIDTH), 1), GLA_CHUNK)
        att_masks.append((_div(t_loc, seq) == _div(s_loc, seq)) & (s_loc <= t_loc))
    o_intra, qe, kd, ebt = _gla_intra(
        proj_s[:, P_GQ:P_GQ + LANES], gk, gv, glog, tri, total_of, att_masks)
    qe_s[...] = qe
    kd_s[...] = kd
    ebt_s[...] = ebt

    sq = proj_s[:, P_SQ:P_SQ + SWA_WIDTH]
    qn_s[...] = sq * _head_rms_scale(sq) * qnw_ref[...]
    sk = proj_s[:, P_SK:P_SK + LANES]
    kn_s[...] = sk * _head_rms_scale(sk) * knw_ref[...]
    vn_s[...] = proj_s[:, P_SV:P_SV + LANES]

    lane32 = _div(_iota((1, SWA_HEADS * seq), 1), seq)
    sink_row = jnp.full((1, SWA_HEADS * seq), sink_ref[SWA_HEADS - 1], F32)
    for hh in range(SWA_HEADS - 2, -1, -1):
        sink_row = jnp.where(lane32 == hh, sink_ref[hh], sink_row)
    lo8 = _iota((seq, LANES), 1) < HALF
    heads_per_group = SSD_HEADS // SSD_GROUPS
    gw = SSD_WIDTH // SSD_GROUPS

    def per_seq(b, carry):
        r0 = pl.multiple_of(b * seq, seq)
        rb = pl.ds(r0, seq)
        s_prev = ssm_ref[b]
        cmb = cm_s[rb, :]
        bmb = bm_s[rb, :]
        xwb = xw_s[rb, :]
        yint_s[rb, :] = jnp.concatenate(
            [_dot_nt(cmb[:, g * SSD_STATE:(g + 1) * SSD_STATE], s_prev[g * gw:(g + 1) * gw, :])
             for g in range(SSD_GROUPS)], axis=1)
        upd = [_dot_tn(xwb[:, g * gw:(g + 1) * gw], bmb[:, g * SSD_STATE:(g + 1) * SSD_STATE])
               for g in range(SSD_GROUPS)]
        drow = ead_s[rb, :][seq - 1:seq, :]
        for hh in range(SSD_HEADS):
            g, r = divmod(hh, heads_per_group)
            hs = slice(hh * SSD_HEAD_DIM, (hh + 1) * SSD_HEAD_DIM)
            ssm_o[b, hs, :] = (s_prev[hs, :] * drow[:, hh:hh + 1]
                               + upd[g][r * SSD_HEAD_DIM:(r + 1) * SSD_HEAD_DIM, :])
        g_prev = gst_ref[b]
        g2 = jnp.concatenate([g_prev, g_prev], axis=1)
        s_bd = jnp.where(bd_mask, jnp.concatenate([g2, g2], axis=1), 0.0)
        oint_s[rb, :] = _dot(qe_s[rb, :], s_bd)
        u2 = jnp.where(bd_mask, _dot_tn(kd_s[rb, :], proj_s[rb, P_GV:P_GV + GLA_WIDTH]), 0.0)
        w = u2[:, :LANES] + u2[:, LANES:]
        erow = jnp.broadcast_to(ebt_s[rb, :][seq - 1:seq, :], (LANES, LANES))
        ecol = jnp.sum(jnp.where(eye, erow, 0.0), axis=-1, keepdims=True)
        gla_o[b] = g_prev * ecol + (w[:, :GLA_DV] + w[:, GLA_DV:])
        knb = kn_s[rb, :]
        vnb = vn_s[rb, :]
        kk_s[0:WINDOW, :] = kc_ref[b]
        kk_s[WINDOW:n_keys, :] = knb
        vv_s[0:WINDOW, :] = vc_ref[b]
        vv_s[WINDOW:n_keys, :] = vnb
        qa = qn_s[rb, 0:LANES]
        qb = qn_s[rb, LANES:2 * LANES]
        qs = jnp.concatenate([jnp.where(lo8, qa, 0.0), jnp.where(lo8, qb, 0.0),
                              jnp.where(lo8, 0.0, qa), jnp.where(lo8, 0.0, qb)], axis=0)
        l = _dot_nt(kk_s[...], qs) * (SWA_HEAD_DIM ** -0.5) + bias_s[...]
        m = jnp.maximum(jnp.max(l, axis=0, keepdims=True), sink_row)
        e = jnp.exp(l - m)
        den = jnp.sum(e, axis=0, keepdims=True) + jnp.exp(sink_row - m)
        o = _dot_tn(e / den, vv_s[...])
        tile_a = jnp.where(lo8, o[0:seq], o[2 * seq:3 * seq])
        tile_b = jnp.where(lo8, o[seq:2 * seq], o[3 * seq:4 * seq])
        oswa_s[rb, :] = jnp.concatenate([tile_a, tile_b], axis=1)
        ko_ref[b, 0:WINDOW - seq, :] = kc_ref[b, seq:WINDOW, :]
        ko_ref[b, WINDOW - seq:WINDOW, :] = knb
        vo_ref[b, 0:WINDOW - seq, :] = vc_ref[b, seq:WINDOW, :]
        vo_ref[b, WINDOW - seq:WINDOW, :] = vnb
        return carry

    lax.fori_loop(0, n_seq, per_seq, 0)

    y = y_intra + yint_s[...] * eacum_e + dsk_ref[...] * xs
    y = y * _silu(proj_s[:, P_Z:P_Z + SSD_WIDTH])
    y_ssd = _group_rmsnorm(y, snw_ref[...])
    o = o_intra + oint_s[...]
    y_gla = o * _head_rms_scale(o) * gnw_ref[...] * _silu(proj_s[:, P_GG:P_GG + GLA_WIDTH])
    y_swa = oswa_s[...] * _silu(proj_s[:, P_SG:P_SG + SWA_WIDTH])
    mix = jnp.concatenate([y_ssd, y_gla, y_swa], axis=1).astype(BF16)
    y_ref[...] = _epilogue(h_ref[...], mix, p_ref[...], wout_ref, wpg_ref, wpe_ref)


def _const_spec(shape):
    nd = len(shape)
    return pl.BlockSpec(shape, lambda *_: (0,) * nd)


def _resident_spec(shape):
    nd = len(shape)
    return pl.BlockSpec(shape, lambda *_: (0,) * nd, pipeline_mode=pl.Buffered(1))


def _smem_spec():
    return pl.BlockSpec(memory_space=pltpu.SMEM)


def _layer_weights(i, w):
    ops = [w["norm_w"][i], w["w_in"][i], w["conv_w"][i], w["conv_b"][i], w["dt_bias"][i], w["a_log"][i],
           w["d_skip"][i], w["ssd_norm_w"][i], w["gla_w_gk"][i], w["gla_b_gk"][i], w["gla_norm_w"][i],
           w["q_norm_w"][i], w["k_norm_w"][i], w["w_out"][i], w["w_pe"][i], w["w_pg"][i]]
    return ops, [_resident_spec(o.shape) if o.dtype == BF16 else _const_spec(o.shape) for o in ops]


def _prompt_layer(layer, depth, bsz, h, p_all, prev_states, bucket, rel, sinks, wops, wspecs):
    rows_total, _ = h.shape
    seq_len = rows_total // bsz
    chunk = PROMPT_CHUNK_ROWS
    pair = 2 * chunk
    chunks_per_seq = seq_len // chunk
    assert seq_len % pair == 0 and chunks_per_seq & (chunks_per_seq - 1) == 0
    n_pairs = rows_total // pair
    kern = functools.partial(_prompt_kernel, chunks_per_seq, len(prev_states))
    proj_rows = pl.BlockSpec((pair, D_MODEL), lambda k: (jnp.minimum(k, n_pairs - 1), 0))
    out_rows = pl.BlockSpec((pair, D_MODEL), lambda k: (jnp.maximum(k - 1, 0), 0))
    p_spec = pl.BlockSpec((1, pair, PLE_DIM), lambda k: (layer, jnp.maximum(k - 1, 0), 0))
    per_seq = lambda s: pl.BlockSpec(
        (1, 1) + s, lambda k: (layer, jnp.maximum(2 * k - 1, 0) // chunks_per_seq) + (0,) * len(s))
    state_shapes = ((SSD_WIDTH, SSD_STATE), (SSD_CONV - 1, SSD_CONV_DIM), (GLA_HEADS * GLA_DK, GLA_DV),
                    (LANES, WINDOW), (LANES, WINDOW))
    out_shape = (jax.ShapeDtypeStruct((rows_total, D_MODEL), F32),) + tuple(
        jax.ShapeDtypeStruct((depth, bsz) + s, F32) for s in state_shapes)
    return pl.pallas_call(
        kern,
        grid=(n_pairs + 1,),
        in_specs=[proj_rows, out_rows, p_spec, _const_spec(bucket.shape), _smem_spec(), _smem_spec()]
        + wspecs + [pl.BlockSpec(memory_space=pl.ANY)] * len(prev_states),
        out_specs=(out_rows,) + tuple(per_seq(s) for s in state_shapes),
        out_shape=out_shape,
        input_output_aliases={N_PROMPT_INPUTS + k: 1 + k for k in range(len(prev_states))},
        scratch_shapes=[
            pltpu.VMEM((chunk, PROJ_W), F32), pltpu.VMEM((chunk, PROJ_W), F32),
            pltpu.VMEM((chunk, XBC_W), F32), pltpu.VMEM((chunk, XBC_W), F32),
            pltpu.VMEM((chunk, D_MODEL), BF16), pltpu.VMEM((chunk, D_MODEL), BF16),
            pltpu.VMEM((chunk, D_MODEL), BF16),
            pltpu.VMEM((chunk, D_MODEL), F32),
            pltpu.VMEM((SUBLANES, XBC_W), F32),
            pltpu.VMEM((BLK, SSD_WIDTH), F32),
            pltpu.VMEM((LANES, GLA_WIDTH), F32),
            pltpu.VMEM((2 * BLK, LANES), F32),
            pltpu.VMEM((2 * BLK, LANES), F32),
            pltpu.VMEM((2 * SWA_HEADS, BLK, 2 * BLK), F32),
        ],
        compiler_params=pltpu.CompilerParams(
            dimension_semantics=("arbitrary",), vmem_limit_bytes=VMEM_LIMIT_BYTES),
        name="prompt_layer",
    )(h, h, p_all, bucket, rel, sinks, *wops, *prev_states)


def _sample_layer(h, p, ssm, conv_pad, gla, kc, vc, bucket, rel, sinks, wops, wspecs, seq):
    rows_total = h.shape[0]
    n_seq = BLK // seq
    rows = n_seq * seq
    bsz = rows_total // seq
    kern = functools.partial(_sample_kernel, seq, n_seq)
    row_spec = lambda w: pl.BlockSpec((rows, w), lambda s: (s, 0))
    seq_spec = lambda a, c: pl.BlockSpec((n_seq, a, c), lambda s: (s, 0, 0))
    out_shape = (
        jax.ShapeDtypeStruct((rows_total, D_MODEL), F32),
        jax.ShapeDtypeStruct((bsz, SSD_WIDTH, SSD_STATE), F32),
        jax.ShapeDtypeStruct((rows_total, SSD_CONV_DIM), F32),
        jax.ShapeDtypeStruct((bsz, GLA_HEADS * GLA_DK, GLA_DV), F32),
        jax.ShapeDtypeStruct((bsz, WINDOW, LANES), F32),
        jax.ShapeDtypeStruct((bsz, WINDOW, LANES), F32),
    )
    n_keys = WINDOW + seq
    f32_scratch = lambda r, c: pltpu.VMEM((r, c), F32)
    return pl.pallas_call(
        kern,
        grid=(rows_total // rows,),
        in_specs=[row_spec(D_MODEL), row_spec(PLE_DIM), seq_spec(SSD_WIDTH, SSD_STATE), row_spec(SSD_CONV_DIM),
                  seq_spec(GLA_HEADS * GLA_DK, GLA_DV), seq_spec(WINDOW, LANES), seq_spec(WINDOW, LANES),
                  _const_spec(bucket.shape), _smem_spec(), _smem_spec()] + wspecs,
        out_specs=(row_spec(D_MODEL), seq_spec(SSD_WIDTH, SSD_STATE), row_spec(SSD_CONV_DIM),
                   seq_spec(GLA_HEADS * GLA_DK, GLA_DV), seq_spec(WINDOW, LANES), seq_spec(WINDOW, LANES)),
        out_shape=out_shape,
        scratch_shapes=[
            f32_scratch(rows, PROJ_W),
            f32_scratch(rows + SUBLANES, XBC_W),
            f32_scratch(rows + SUBLANES, XBC_W),
            f32_scratch(rows, LANES), f32_scratch(rows, LANES), f32_scratch(rows, SSD_WIDTH),
            f32_scratch(rows, LANES), f32_scratch(rows, SSD_WIDTH),
            f32_scratch(rows, LANES), f32_scratch(rows, LANES), f32_scratch(rows, LANES),
            f32_scratch(rows, GLA_WIDTH),
            f32_scratch(rows, SWA_WIDTH), f32_scratch(rows, LANES), f32_scratch(rows, LANES),
            f32_scratch(rows, SWA_WIDTH),
            f32_scratch(n_keys, LANES), f32_scratch(n_keys, LANES),
            f32_scratch(n_keys, SWA_HEADS * seq),
        ],
        compiler_params=pltpu.CompilerParams(
            dimension_semantics=("arbitrary",), vmem_limit_bytes=VMEM_LIMIT_BYTES),
        name="sample_layer",
    )(h, p, ssm, conv_pad, gla, kc, vc, bucket, rel, sinks, *wops)


SWA_HEAD_ORDER = (0, 2, 1, 3)


def _win_tile_runs():
    sizes = (SSD_WIDTH, SSD_CONV_DIM, SSD_HEADS, GLA_HEADS * GLA_DK, GLA_HEADS * GLA_DK, GLA_WIDTH, GLA_WIDTH,
             GLA_RANK, SWA_WIDTH, SWA_KV_HEADS * SWA_HEAD_DIM, SWA_KV_HEADS * SWA_HEAD_DIM, SWA_WIDTH)
    offs = np.concatenate([[0], np.cumsum(sizes)])
    seg = lambda k: np.arange(offs[k], offs[k + 1])
    z, xbc, dt, gq, gk, gv, gg, glr, sq, sk, sv, sg = [seg(k) for k in range(len(sizes))]
    heads = lambda a: np.concatenate([a[h * SWA_HEAD_DIM:(h + 1) * SWA_HEAD_DIM] for h in SWA_HEAD_ORDER])
    pad = np.full(LANES - SSD_HEADS - GLA_RANK, -1)
    src = np.concatenate([xbc, z, gq, gk, gv, gg, heads(sq), sk, sv, heads(sg), dt, glr, pad])
    assert src.size == XBC_W + PROJ_W
    tiles = []
    for j in range(src.size // LANES):
        idx = src[j * LANES:(j + 1) * LANES]
        cuts = [0] + [k for k in range(1, LANES) if (idx[k] != idx[k - 1] + 1 and not (idx[k] == -1 == idx[k - 1]))]
        runs = [(int(idx[a]), b - a) for a, b in zip(cuts, cuts[1:] + [LANES])]
        assert all(n % SUBLANES == 0 and (s < 0 or s % SUBLANES == 0) for s, n in runs)
        tiles.append(runs)
    return tiles


def _win_prep_kernel(tile_runs, wt_ref, out_ref):
    for j, runs in enumerate(tile_runs):
        parts = [jnp.zeros((n, D_MODEL), F32) if s < 0 else wt_ref[0, s:s + n, :] for s, n in runs]
        tile = parts[0] if len(parts) == 1 else jnp.concatenate(parts, axis=0)
        out_ref[0, :, j * LANES:(j + 1) * LANES] = tile.T.astype(BF16)


def _prepare_w_in(w_in):
    depth, d_model, d_in = w_in.shape
    w_t = jnp.swapaxes(w_in, 1, 2)
    return pl.pallas_call(
        functools.partial(_win_prep_kernel, _win_tile_runs()),
        grid=(depth,),
        in_specs=[pl.BlockSpec((1, d_in, d_model), lambda l: (l, 0, 0))],
        out_specs=pl.BlockSpec((1, d_model, XBC_W + PROJ_W), lambda l: (l, 0, 0)),
        out_shape=jax.ShapeDtypeStruct((depth, d_model, XBC_W + PROJ_W), BF16),
        compiler_params=pltpu.CompilerParams(
            dimension_semantics=("arbitrary",), vmem_limit_bytes=VMEM_LIMIT_BYTES),
        name="w_in_prep",
    )(w_t)


def _prepare_weights(norm_w, w_in, conv_w, conv_b, dt_bias, a_log, d_skip, ssd_norm_w, gla_w_gk, gla_b_gk,
                     gla_norm_w, q_norm_w, k_norm_w, w_out, w_pe, w_pg):
    mix_w = SSD_WIDTH + GLA_WIDTH
    w_out_p = jnp.concatenate(
        [w_out[:, :mix_w, :]] + [w_out[:, mix_w + h * SWA_HEAD_DIM:mix_w + (h + 1) * SWA_HEAD_DIM, :]
                                 for h in SWA_HEAD_ORDER], axis=1).astype(BF16)
    lane_pad = lambda x: jnp.pad(x, ((0, 0), (0, LANES - x.shape[-1])))[:, None, :]
    wgk_p = jnp.pad(gla_w_gk, ((0, 0), (LR_LANE0, LANES - LR_LANE0 - GLA_RANK), (0, 0))).astype(BF16)
    return dict(
        norm_w=norm_w[:, None, :], w_in=_prepare_w_in(w_in), conv_w=conv_w, conv_b=conv_b[:, None, :],
        dt_bias=lane_pad(dt_bias), a_log=lane_pad(a_log),
        d_skip=jnp.repeat(d_skip, SSD_HEAD_DIM, axis=-1)[:, None, :], ssd_norm_w=ssd_norm_w[:, None, :],
        gla_w_gk=wgk_p, gla_b_gk=gla_b_gk[:, None, :],
        gla_norm_w=jnp.tile(gla_norm_w, (1, GLA_HEADS))[:, None, :],
        q_norm_w=jnp.tile(q_norm_w, (1, SWA_HEADS))[:, None, :],
        k_norm_w=jnp.tile(k_norm_w, (1, SWA_KV_HEADS))[:, None, :],
        w_out=w_out_p, w_pe=w_pe.astype(BF16), w_pg=w_pg.astype(BF16))


def kernel(x_prompt, x_sample, state_ssm, state_conv, state_gla, cache_swa_k, cache_swa_v, p_prompt, p_sample, rel_bias, norm_w, w_in, conv_w, conv_b, dt_bias, a_log, d_skip, ssd_norm_w, gla_w_gk, gla_b_gk, gla_norm_w, q_norm_w, k_norm_w, attn_sinks, w_out, w_pe, w_pg):
    depth = w_in.shape[0]
    bp, seq_p, _ = x_prompt.shape
    bs, seq_s, _ = x_sample.shape
    assert seq_s == SUBLANES and BLK % seq_s == 0 and (bs * seq_s) % BLK == 0
    assert cache_swa_k.shape[2] == WINDOW
    w = _prepare_weights(norm_w, w_in, conv_w, conv_b, dt_bias, a_log, d_skip, ssd_norm_w, gla_w_gk, gla_b_gk,
                         gla_norm_w, q_norm_w, k_norm_w, w_out, w_pe, w_pg)
    rel_flat = rel_bias.reshape(-1)
    dist_p = WINDOW + np.arange(BLK)[:, None] - np.arange(2 * BLK)[None, :]
    bucket_p = jnp.asarray(_bucket_table(dist_p))
    dist_s = WINDOW + np.arange(seq_s)[None, :] - np.arange(WINDOW + seq_s)[:, None]
    bucket_s = jnp.asarray(np.tile(_bucket_table(dist_s), (1, SWA_HEADS)))

    hp = x_prompt.reshape(bp * seq_p, D_MODEL)
    p_prompt_rows = p_prompt.reshape(depth, bp * seq_p, PLE_DIM)
    hs = x_sample.reshape(bs * seq_s, D_MODEL)
    states_p = ()
    outs_s = [[] for _ in range(5)]
    for i in range(depth):
        wops, wspecs = _layer_weights(i, w)
        hp, *states_p = _prompt_layer(i, depth, bp, hp, p_prompt_rows, tuple(states_p), bucket_p, rel_flat,
                                      attn_sinks[i], wops, wspecs)
        conv_pad = jnp.pad(state_conv[i], ((0, 0), (seq_s - (SSD_CONV - 1), 0), (0, 0)))
        hs, ssm, xbc_raw, gla, ko, vo = _sample_layer(
            hs, p_sample[i].reshape(bs * seq_s, PLE_DIM),
            state_ssm[i].reshape(bs, SSD_WIDTH, SSD_STATE), conv_pad.reshape(bs * seq_s, SSD_CONV_DIM),
            state_gla[i].reshape(bs, GLA_HEADS * GLA_DK, GLA_DV),
            cache_swa_k[i].reshape(bs, WINDOW, LANES), cache_swa_v[i].reshape(bs, WINDOW, LANES),
            bucket_s, rel_flat, attn_sinks[i], wops, wspecs, seq_s)
        conv = xbc_raw.reshape(bs, seq_s, SSD_CONV_DIM)[:, seq_s - (SSD_CONV - 1):, :]
        for lst, val in zip(outs_s, (ssm.reshape(bs, SSD_HEADS, SSD_HEAD_DIM, SSD_STATE), conv,
                                     gla.reshape(bs, GLA_HEADS, GLA_DK, GLA_DV),
                                     ko.reshape(bs, WINDOW, SWA_KV_HEADS, SWA_HEAD_DIM),
                                     vo.reshape(bs, WINDOW, SWA_KV_HEADS, SWA_HEAD_DIM))):
            lst.append(val)
    ssm_p, conv_p, gla_p, kt_p, vt_p = states_p
    unpack_kv = lambda a: jnp.transpose(a.reshape(depth, bp, SWA_KV_HEADS, SWA_HEAD_DIM, WINDOW), (0, 1, 4, 2, 3))
    outs_p = (ssm_p.reshape(depth, bp, SSD_HEADS, SSD_HEAD_DIM, SSD_STATE), conv_p,
              gla_p.reshape(depth, bp, GLA_HEADS, GLA_DK, GLA_DV), unpack_kv(kt_p), unpack_kv(vt_p))
    return ((hp.reshape(bp, seq_p, D_MODEL), hs.reshape(bs, seq_s, D_MODEL)) + outs_p
            + tuple(jnp.stack(v) for v in outs_s))
```

```python
import functools
import math

import numpy as np
import jax
import jax.numpy as jnp
from jax import lax
from jax.experimental import pallas as pl
from jax.experimental.pallas import tpu as pltpu

D_MODEL = 1024
DEPTH = 2
SSD_HEADS = 8
SSD_HEAD_DIM = 64
SSD_WIDTH = SSD_HEADS * SSD_HEAD_DIM
SSD_GROUPS = 2
SSD_STATE = 64
SSD_CONV = 4
SSD_CONV_DIM = SSD_WIDTH + 2 * SSD_GROUPS * SSD_STATE
SSD_CHUNK = 128
GLA_HEADS = 4
GLA_DK = 32
GLA_DV = 64
GLA_WIDTH = GLA_HEADS * GLA_DV
GLA_RANK = 16
GLA_GATE_NORM = 16.0
GLA_CHUNK = 64
SWA_HEADS = 4
SWA_KV_HEADS = 2
SWA_HEAD_DIM = 64
SWA_WIDTH = SWA_HEADS * SWA_HEAD_DIM
WINDOW = 128
REL_BUCKETS = 32
REL_MAX_DIST = 128
PLE_DIM = 256
EPS = 1e-6

LANES = 128
SUBLANES = 8
HALF = LANES // 2
BLK = 128
VMEM_LIMIT_BYTES = 56 * 1024 * 1024

XBC_W = SSD_CONV_DIM
P_Z = 0
P_GQ = P_Z + SSD_WIDTH
P_GK = P_GQ + LANES
P_GV = P_GK + LANES
P_GG = P_GV + GLA_WIDTH
P_SQ = P_GG + GLA_WIDTH
P_SK = P_SQ + SWA_WIDTH
P_SV = P_SK + LANES
P_SG = P_SV + LANES
P_DTLR = P_SG + SWA_WIDTH
PROJ_W = P_DTLR + LANES
LR_LANE0 = SSD_HEADS

F32 = jnp.float32
BF16 = jnp.bfloat16
NEG_INF = float("-inf")
N_PROMPT_INPUTS = 22
PROMPT_CHUNK_ROWS = 2 * BLK
TICKS_PER_ITEM = 2
NT_DIMS = (((1,), (1,)), ((), ()))
TN_DIMS = (((0,), (0,)), ((), ()))


def _iota(shape, dim):
    return lax.broadcasted_iota(jnp.int32, shape, dim)


def _div(x, d):
    return x >> (d.bit_length() - 1)


def _mod(x, d):
    return x & (d - 1)


def _softplus(x):
    e = jnp.exp(-jnp.abs(x))
    u = 1.0 + e
    d = u - 1.0
    log1p_e = jnp.where(d == 0.0, e, jnp.log(u) * (e / jnp.where(d == 0.0, 1.0, d)))
    return jnp.maximum(x, 0.0) + log1p_e


def _log_sigmoid(x):
    return jnp.minimum(x, 0.0) - jnp.log(1.0 + jnp.exp(-jnp.abs(x)))


def _silu(x):
    return x * jax.nn.sigmoid(x)


def _dot(a, b):
    return jnp.dot(a.astype(BF16), b.astype(BF16), preferred_element_type=F32)


def _dot_nt(a, b):
    return lax.dot_general(a.astype(BF16), b.astype(BF16), NT_DIMS, preferred_element_type=F32)


def _dot_tn(a, b):
    return lax.dot_general(a.astype(BF16), b.astype(BF16), TN_DIMS, preferred_element_type=F32)


def _dot_exact(sel, x):
    x1 = x.astype(BF16)
    r1 = x - x1.astype(F32)
    x2 = r1.astype(BF16)
    x3 = (r1 - x2.astype(F32)).astype(BF16)
    dot = functools.partial(jnp.dot, sel, preferred_element_type=F32)
    return dot(x1) + dot(x2) + dot(x3)


def _expand_heads(x, n_heads):
    rows = x.shape[0]
    lo = _iota((rows, LANES), 1) < HALF
    tiles = []
    for j in range(n_heads // 2):
        a = jnp.broadcast_to(x[:, 2 * j:2 * j + 1], (rows, LANES))
        b = jnp.broadcast_to(x[:, 2 * j + 1:2 * j + 2], (rows, LANES))
        tiles.append(jnp.where(lo, a, b))
    return jnp.concatenate(tiles, axis=1)


def _head_rms_scale(x):
    rows, width = x.shape
    lo = _iota((rows, LANES), 1) < HALF
    outs = []
    for j in range(width // LANES):
        t = x[:, j * LANES:(j + 1) * LANES]
        sq = t * t
        s_lo = jnp.sum(jnp.where(lo, sq, 0.0), axis=-1, keepdims=True)
        s_hi = jnp.sum(jnp.where(lo, 0.0, sq), axis=-1, keepdims=True)
        outs.append(lax.rsqrt(jnp.where(lo, s_lo, s_hi) * (1.0 / HALF) + EPS))
    return outs[0] if len(outs) == 1 else jnp.concatenate(outs, axis=1)


def _group_rmsnorm(y, w):
    gw = SSD_WIDTH // SSD_GROUPS
    outs = []
    for g in range(SSD_GROUPS):
        t = y[:, g * gw:(g + 1) * gw]
        ms = jnp.sum(t * t, axis=-1, keepdims=True) * (1.0 / gw)
        outs.append(t * lax.rsqrt(ms + EPS))
    return jnp.concatenate(outs, axis=1) * w


def _rel_bucket_np(dist):
    n = np.maximum(dist, 0)
    exact = REL_BUCKETS // 2
    nf = np.maximum(n, 1).astype(np.float64)
    large = exact + (np.log(nf / exact) / math.log(REL_MAX_DIST / exact) * (REL_BUCKETS - exact)).astype(np.int32)
    large = np.minimum(large, REL_BUCKETS - 1)
    return np.where(n < exact, n, large).astype(np.int32)


def _bucket_table(dist):
    return np.where((dist >= 0) & (dist < WINDOW), _rel_bucket_np(dist), -1).astype(np.int32)


def _no_tick():
    pass


def _ssd_intra(xbc_c, dtlr, dtb, a_row, pair_mask, tri, total_of, tick=_no_tick):
    xs = xbc_c[:, :SSD_WIDTH]
    bm = xbc_c[:, SSD_WIDTH:SSD_WIDTH + LANES]
    cm = xbc_c[:, SSD_WIDTH + LANES:]
    lane = _iota((BLK, LANES), 1)
    lo = lane < HALF
    dtv = _softplus(dtlr + dtb)
    adt = dtv * a_row
    acum = _dot_exact(tri, adt)
    tick()
    acum_t = acum.T
    eacum = jnp.exp(acum)
    tail = jnp.exp(total_of(adt, acum) - acum)
    dtv_e = _expand_heads(dtv, SSD_HEADS)
    eacum_e = _expand_heads(eacum, SSD_HEADS)
    tail_e = _expand_heads(tail, SSD_HEADS)
    tick()
    xdt = xs * dtv_e
    xw = xdt * tail_e
    cb = [_dot_nt(jnp.where(lo, cm, 0.0), bm), _dot_nt(jnp.where(lo, 0.0, cm), bm)]
    y_pairs = []
    for j in range(SSD_HEADS // 2):
        tick()
        g = (2 * j) // (SSD_HEADS // SSD_GROUPS)
        ms = []
        for k in range(2):
            h = 2 * j + k
            seg = acum[:, h:h + 1] - acum_t[h:h + 1, :]
            dec = jnp.where(pair_mask, jnp.exp(seg), 0.0)
            ms.append((cb[g] * dec).astype(BF16))
        xp = xdt[:, j * LANES:(j + 1) * LANES]
        rhs = jnp.concatenate([jnp.where(lo, xp, 0.0), jnp.where(lo, 0.0, xp)], axis=0)
        y_pairs.append(_dot(jnp.concatenate(ms, axis=1), rhs))
    y_intra = jnp.concatenate(y_pairs, axis=1)
    return y_intra, xs, bm, cm, xw, eacum, eacum_e


def _gla_intra(gq, gk, gv, glog, tri, total_of, att_masks, tick=_no_tick):
    bcs = _dot_exact(tri, glog)
    tick()
    eb = jnp.exp(bcs)
    qe = gq * (GLA_DK ** -0.5) * eb
    ke = gk * jnp.exp(-bcs)
    btot = total_of(glog, bcs)
    kd = gk * jnp.exp(btot - bcs)
    lane_k = _iota((GLA_CHUNK, LANES), 1)
    lane_v = _iota((GLA_CHUNK, GLA_WIDTH), 1)
    outs = []
    for c2 in range(BLK // GLA_CHUNK):
        tick()
        rs = slice(c2 * GLA_CHUNK, (c2 + 1) * GLA_CHUNK)
        ke_c = ke[rs]
        v_c = gv[rs]
        kbd = jnp.concatenate(
            [jnp.where(_div(lane_k, GLA_DK) == h, ke_c, 0.0) for h in range(GLA_HEADS)], axis=0)
        att = _dot_nt(qe[rs], kbd)
        att = jnp.where(att_masks[c2], att, 0.0)
        vbd = jnp.concatenate(
            [jnp.where(_div(lane_v, GLA_DV) == h, v_c, 0.0) for h in range(GLA_HEADS)], axis=0)
        outs.append(_dot(att, vbd))
    return jnp.concatenate(outs, axis=0), qe, kd, jnp.exp(btot)


def _build_bias(bucket, rel_ref, n_heads_in_lanes=None):
    accs = [jnp.full(bucket.shape, NEG_INF, F32) for _ in range(SWA_HEADS)]
    for b in range(REL_BUCKETS):
        hit = bucket == b
        for h in range(SWA_HEADS):
            accs[h] = jnp.where(hit, rel_ref[b * SWA_HEADS + h], accs[h])
    return accs


def _epilogue(h, mix, p, wout_ref, wpg_ref, wpe_ref):
    h1 = h + jnp.dot(mix, wout_ref[...], preferred_element_type=F32)
    gate = jax.nn.sigmoid(jnp.dot(h1.astype(BF16), wpg_ref[...], preferred_element_type=F32))
    pe = jnp.dot(p.astype(BF16), wpe_ref[...], preferred_element_type=F32)
    return h1 + gate * pe


def _project(h, nw_ref, win_ref, xbc_s, proj_s, rows):
    ms = jnp.mean(h * h, axis=-1, keepdims=True)
    u = (h * lax.rsqrt(ms + EPS) * nw_ref[...]).astype(BF16)
    xbc_s[SUBLANES:SUBLANES + rows, :] = jnp.dot(u, win_ref[:, :XBC_W], preferred_element_type=F32)
    proj_s[...] = jnp.dot(u, win_ref[:, XBC_W:], preferred_element_type=F32)


def _prompt_kernel(chunks_per_seq, n_aliased, *refs):
    (ha_ref, hc_ref, p_ref, bucket_ref, rel_ref, sink_ref, nw_ref, win_ref, cw_ref, cb_ref, dtb_ref,
     alog_ref, dsk_ref, snw_ref, wgk_ref, bgk_ref, gnw_ref, qnw_ref, knw_ref, wout_ref,
     wpe_ref, wpg_ref) = refs[:N_PROMPT_INPUTS]
    (y_ref, ssm_ref, conv_ref, gla_ref, ko_ref, vo_ref,
     proj_e, proj_o, xbc_e, xbc_o, mix_e, mix_o, u_s, h1_s, hist_s, st_s, s2_s, kext_s, vext_s,
     bias_s) = refs[N_PROMPT_INPUTS + n_aliased:]
    k_idx = pl.program_id(0)

    @pl.when(k_idx == 0)
    def _():
        accs = _build_bias(bucket_ref[...], rel_ref)
        own_block = _iota((BLK, 2 * BLK), 1) >= BLK
        for hh in range(SWA_HEADS):
            bias_s[hh] = accs[hh]
            bias_s[SWA_HEADS + hh] = jnp.where(own_block, accs[hh], NEG_INF)
        for ref in (proj_o, xbc_o, mix_e, mix_o, hist_s, st_s, s2_s, kext_s, vext_s):
            ref[...] = jnp.zeros(ref.shape, ref.dtype)

    row = _iota((BLK, BLK), 0)
    col = _iota((BLK, BLK), 1)
    causal = row >= col
    tri_ssd = jnp.where(causal, 1.0, 0.0).astype(BF16)
    tri_gla = jnp.where(causal & (_div(row, GLA_CHUNK) == _div(col, GLA_CHUNK)), 1.0, 0.0).astype(BF16)
    lo = col < HALF
    lane_row = _iota((1, LANES), 1)
    a_row = jnp.where(lane_row < SSD_HEADS, -jnp.exp(alog_ref[...]), 0.0)
    st_mask = (_iota((BLK, SSD_WIDTH), 0) < SSD_STATE) == (_iota((BLK, SSD_WIDTH), 1) < SSD_WIDTH // SSD_GROUPS)
    bd_mask = _div(_iota((LANES, GLA_WIDTH), 0), GLA_DK) == _div(_iota((LANES, GLA_WIDTH), 1), GLA_DV)
    att_t = _iota((GLA_CHUNK, GLA_WIDTH), 0)
    att_s = _mod(_iota((GLA_CHUNK, GLA_WIDTH), 1), GLA_CHUNK)
    att_mask = att_s <= att_t
    lo2 = _iota((2 * BLK, LANES), 1) < HALF

    def ssd_total(adt, acum):
        return jnp.broadcast_to(acum[BLK - 1:BLK, :], acum.shape)

    def gla_total(glog, bcs):
        return jnp.concatenate(
            [jnp.broadcast_to(bcs[(c2 + 1) * GLA_CHUNK - 1:(c2 + 1) * GLA_CHUNK, :], (GLA_CHUNK, LANES))
             for c2 in range(BLK // GLA_CHUNK)], axis=0)

    def block(blk, proj_s, xbc_s, mix_s, starts_sequence, tick):
        rows = slice(blk * BLK, (blk + 1) * BLK)
        cw = cw_ref[...]
        tick()
        if blk == 0:
            xwin = jnp.concatenate([hist_s[...], xbc_s[0:BLK, :]], axis=0)
        else:
            xwin = xbc_s[blk * BLK - SUBLANES:(blk + 1) * BLK, :]
        acc = xwin[SUBLANES - 3:SUBLANES - 3 + BLK, :] * cw[0:1, :]
        for k in range(1, SSD_CONV):
            acc = acc + xwin[SUBLANES - 3 + k:SUBLANES - 3 + k + BLK, :] * cw[k:k + 1, :]
        xbc_c = _silu(acc + cb_ref[...])
        tick()
        dtlr = proj_s[rows, P_DTLR:P_DTLR + LANES]
        y_intra, xs, bm, cm, xw, eacum, eacum_e = _ssd_intra(
            xbc_c, dtlr, dtb_ref[...], a_row, causal, tri_ssd, ssd_total, tick)
        tick()
        st = st_s[...]
        y = y_intra + _dot(cm, st) * eacum_e + dsk_ref[...] * xs
        st_s[...] = st * eacum_e[BLK - 1:BLK, :] + jnp.where(st_mask, _dot(bm.T, xw), 0.0)
        tick()
        y = y * _silu(proj_s[rows, P_Z:P_Z + SSD_WIDTH])
        mix_s[rows, 0:SSD_WIDTH] = _group_rmsnorm(y, snw_ref[...]).astype(BF16)
        tick()
        gk = proj_s[rows, P_GK:P_GK + LANES]
        gv = proj_s[rows, P_GV:P_GV + GLA_WIDTH]
        glog = _log_sigmoid(_dot(dtlr, wgk_ref[...]) + bgk_ref[...]) * (1.0 / GLA_GATE_NORM)
        o_intra, qe, kd, ebt = _gla_intra(
            proj_s[rows, P_GQ:P_GQ + LANES], gk, gv, glog, tri_gla, gla_total, [att_mask, att_mask], tick)
        kd_t = kd.T
        ebt_t = ebt.T
        s2 = s2_s[...]
        o_parts = []
        for c2 in range(BLK // GLA_CHUNK):
            tick()
            rs = slice(c2 * GLA_CHUNK, (c2 + 1) * GLA_CHUNK)
            o_parts.append(o_intra[rs] + _dot(qe[rs], s2))
            u2 = _dot(jnp.where(_div(col, GLA_CHUNK) == c2, kd_t, 0.0), gv)
            last = (c2 + 1) * GLA_CHUNK - 1
            s2 = s2 * ebt_t[:, last:last + 1] + jnp.where(bd_mask, u2, 0.0)
        s2_s[...] = s2
        o = jnp.concatenate(o_parts, axis=0)
        y_gla = o * _head_rms_scale(o) * gnw_ref[...] * _silu(proj_s[rows, P_GG:P_GG + GLA_WIDTH])
        mix_s[rows, SSD_WIDTH:SSD_WIDTH + GLA_WIDTH] = y_gla.astype(BF16)
        tick()
        sq = proj_s[rows, P_SQ:P_SQ + SWA_WIDTH]
        qn = sq * _head_rms_scale(sq) * qnw_ref[...] * (SWA_HEAD_DIM ** -0.5)
        sk = proj_s[rows, P_SK:P_SK + LANES]
        kn = sk * _head_rms_scale(sk) * knw_ref[...]
        vn = proj_s[rows, P_SV:P_SV + LANES]
        kext_s[BLK:2 * BLK, :] = kn
        vext_s[BLK:2 * BLK, :] = vn
        kext = kext_s[...]
        vext = vext_s[...]
        qa = qn[:, :LANES]
        qb = qn[:, LANES:]
        qs = jnp.concatenate([jnp.where(lo, qa, 0.0), jnp.where(lo, qb, 0.0),
                              jnp.where(lo, 0.0, qa), jnp.where(lo, 0.0, qb)], axis=0)
        logits = _dot_nt(qs, kext)
        tick()
        if blk == 0 and starts_sequence is not False:
            bias_row0 = jnp.where(starts_sequence, SWA_HEADS, 0)
        else:
            bias_row0 = 0
        es = []
        invs = []
        for hh in range(SWA_HEADS):
            tick()
            sink = sink_ref[hh]
            l = logits[hh * BLK:(hh + 1) * BLK] + bias_s[bias_row0 + hh]
            m = jnp.maximum(jnp.max(l, axis=-1, keepdims=True), sink)
            e = jnp.exp(l - m)
            den = jnp.sum(e, axis=-1, keepdims=True) + jnp.exp(sink - m)
            es.append(e.astype(BF16))
            invs.append(1.0 / den)
        v_stack = jnp.concatenate([jnp.where(lo2, vext, 0.0), jnp.where(lo2, 0.0, vext)], axis=0)
        tile_a = _dot(jnp.concatenate([es[0], es[2]], axis=1), v_stack) * jnp.where(lo, invs[0], invs[2])
        tile_b = _dot(jnp.concatenate([es[1], es[3]], axis=1), v_stack) * jnp.where(lo, invs[1], invs[3])
        oa = jnp.concatenate([tile_a, tile_b], axis=1)
        y_swa = oa * _silu(proj_s[rows, P_SG:P_SG + SWA_WIDTH])
        mix_s[rows, SSD_WIDTH + GLA_WIDTH:] = y_swa.astype(BF16)
        kext_s[0:BLK, :] = kn
        vext_s[0:BLK, :] = vn

    chunk = proj_e.shape[0]

    def project_items(rows, proj_s, xbc_s):
        def norm():
            h = ha_ref[rows, :]
            ms = jnp.mean(h * h, axis=-1, keepdims=True)
            u_s[...] = (h * lax.rsqrt(ms + EPS) * nw_ref[...]).astype(BF16)

        def cols(dst, lo_c, hi_c, w_off):
            def item():
                dst[:, lo_c:hi_c] = jnp.dot(u_s[...], win_ref[:, w_off + lo_c:w_off + hi_c],
                                            preferred_element_type=F32)
            return item

        step = 2 * LANES
        items = [norm]
        items += [cols(xbc_s, c, min(c + step, XBC_W), 0) for c in range(0, XBC_W, step)]
        items += [cols(proj_s, c, min(c + step, PROJ_W), XBC_W) for c in range(0, PROJ_W, step)]
        return items

    def epilogue_items(rows, mix_s):
        half_w = 2 * LANES

        def residual(c):
            def item():
                h1_s[:, c:c + half_w] = hc_ref[rows, c:c + half_w] + jnp.dot(
                    mix_s[...], wout_ref[:, c:c + half_w], preferred_element_type=F32)
            return item

        def gated(c):
            def item():
                gate = jax.nn.sigmoid(jnp.dot(h1_s[...].astype(BF16), wpg_ref[:, c:c + half_w],
                                              preferred_element_type=F32))
                pe = jnp.dot(p_ref[0, rows, :].astype(BF16), wpe_ref[:, c:c + half_w],
                             preferred_element_type=F32)
                y_ref[rows, c:c + half_w] = h1_s[:, c:c + half_w] + gate * pe
            return item

        col0 = range(0, D_MODEL, half_w)
        return [residual(c) for c in col0] + [gated(c) for c in col0]

    def merge(first, second):
        out = list(first)
        for j, item in enumerate(second):
            out.insert(((j + 1) * len(first)) // len(second) + j, item)
        return out

    def mixer(proj_s, xbc_s, mix_s, starts_sequence, items):
        if starts_sequence is not False:
            keep = jnp.where(starts_sequence, 0.0, 1.0)
            for ref in (hist_s, st_s, s2_s):
                ref[...] = ref[...] * keep
            kext_s[0:BLK, :] = kext_s[0:BLK, :] * keep
            vext_s[0:BLK, :] = vext_s[0:BLK, :] * keep
        queue = list(items)
        calls = [0]

        def tick():
            calls[0] += 1
            if queue and calls[0] % TICKS_PER_ITEM == 0:
                queue.pop(0)()

        for blk in range(chunk // BLK):
            block(blk, proj_s, xbc_s, mix_s, starts_sequence, tick)
        while queue:
            queue.pop(0)()
        hist_s[...] = xbc_s[chunk - SUBLANES:chunk, :]

    def write_states():
        st = st_s[...]
        stc = st[:SSD_STATE] + st[SSD_STATE:]
        ssm_ref[0, 0] = jnp.concatenate([stc, stc], axis=0).T[:, :SSD_STATE]
        conv_ref[0, 0] = hist_s[SUBLANES - (SSD_CONV - 1):SUBLANES, :]
        s2 = s2_s[...]
        w = s2[:, :LANES] + s2[:, LANES:]
        gla_ref[0, 0] = w[:, :GLA_DV] + w[:, GLA_DV:]
        ko_ref[0, 0] = kext_s[0:BLK, :].T
        vo_ref[0, 0] = vext_s[0:BLK, :].T

    even = slice(0, chunk)
    odd = slice(chunk, 2 * chunk)
    mixer(proj_o, xbc_o, mix_o, False, merge(project_items(even, proj_e, xbc_e), epilogue_items(even, mix_e)))
    write_states()
    mixer(proj_e, xbc_e, mix_e, _mod(2 * k_idx, chunks_per_seq) == 0,
          merge(project_items(odd, proj_o, xbc_o), epilogue_items(odd, mix_o)))


def _sample_kernel(seq, n_seq,
                   h_ref, p_ref, ssm_ref, cst_ref, gst_ref, kc_ref, vc_ref, bucket_ref, rel_ref, sink_ref,
                   nw_ref, win_ref, cw_ref, cb_ref, dtb_ref, alog_ref, dsk_ref, snw_ref, wgk_ref, bgk_ref,
                   gnw_ref, qnw_ref, knw_ref, wout_ref, wpe_ref, wpg_ref,
                   y_ref, ssm_o, xbc_o, gla_o, ko_ref, vo_ref,
                   proj_s, xbc_s, cs_s, cm_s, bm_s, xw_s, ead_s, yint_s, qe_s, kd_s, ebt_s, oint_s,
                   qn_s, kn_s, vn_s, oswa_s, kk_s, vv_s, bias_s):
    rows = n_seq * seq
    n_keys = WINDOW + seq

    @pl.when(pl.program_id(0) == 0)
    def _():
        bucket = bucket_ref[...]
        accs = _build_bias(bucket, rel_ref)
        head_of_lane = _div(_iota(bucket.shape, 1), seq)
        out = accs[SWA_HEADS - 1]
        for hh in range(SWA_HEADS - 2, -1, -1):
            out = jnp.where(head_of_lane == hh, accs[hh], out)
        bias_s[...] = out

    xbc_s[0:SUBLANES, :] = jnp.zeros((SUBLANES, XBC_W), F32)
    _project(h_ref[...], nw_ref, win_ref, xbc_s, proj_s, rows)
    cs_s[0:rows, :] = cst_ref[...]
    cs_s[rows:rows + SUBLANES, :] = jnp.zeros((SUBLANES, XBC_W), F32)
    xbc_o[...] = xbc_s[SUBLANES:SUBLANES + rows, :]

    row = _iota((BLK, BLK), 0)
    col = _iota((BLK, BLK), 1)
    same_seq = _div(row, seq) == _div(col, seq)
    pair_mask = same_seq & (row >= col)
    tri = jnp.where(pair_mask, 1.0, 0.0).astype(BF16)
    ones_seq = jnp.where(same_seq, 1.0, 0.0).astype(BF16)
    lo = col < HALF
    lane_row = _iota((1, LANES), 1)
    a_row = jnp.where(lane_row < SSD_HEADS, -jnp.exp(alog_ref[...]), 0.0)
    bd_mask = _div(_iota((LANES, GLA_WIDTH), 0), GLA_DK) == _div(_iota((LANES, GLA_WIDTH), 1), GLA_DV)
    eye = row == col

    def total_of(x, _cum):
        return _dot_exact(ones_seq, x)

    cw = cw_ref[...]
    t_of_row = _mod(_iota((rows, XBC_W), 0), seq)
    taps = []
    for j in range(SSD_CONV - 1, 0, -1):
        cur = xbc_s[pl.ds(SUBLANES - j, rows), :]
        old = cs_s[pl.ds(SUBLANES - j, rows), :]
        taps.append(jnp.where(t_of_row >= j, cur, old))
    taps.append(xbc_s[pl.ds(SUBLANES, rows), :])
    acc = taps[0] * cw[0:1, :]
    for k in range(1, SSD_CONV):
        acc = acc + taps[k] * cw[k:k + 1, :]
    xbc_c = _silu(acc + cb_ref[...])
    dtlr = proj_s[:, P_DTLR:P_DTLR + LANES]
    y_intra, xs, bm, cm, xw, eacum, eacum_e = _ssd_intra(
        xbc_c, dtlr, dtb_ref[...], a_row, pair_mask, tri, total_of)
    cm_s[...] = cm
    bm_s[...] = bm
    xw_s[...] = xw
    ead_s[...] = eacum

    gk = proj_s[:, P_GK:P_GK + LANES]
    gv = proj_s[:, P_GV:P_GV + GLA_WIDTH]
    glog = _log_sigmoid(_dot(dtlr, wgk_ref[...]) + bgk_ref[...]) * (1.0 / GLA_GATE_NORM)
    att_masks = []
    for c2 in range(BLK // GLA_CHUNK):
        t_loc = _iota((GLA_CHUNK, GLA_WIDTH), 0)
        s_loc = _mod(_iota((GLA_CHUNK, GLA_WIDTH), 1), GLA_CHUNK)
        att_masks.append((_div(t_loc, seq) == _div(s_loc, seq)) & (s_loc <= t_loc))
    o_intra, qe, kd, ebt = _gla_intra(
        proj_s[:, P_GQ:P_GQ + LANES], gk, gv, glog, tri, total_of, att_masks)
    qe_s[...] = qe
    kd_s[...] = kd
    ebt_s[...] = ebt

    sq = proj_s[:, P_SQ:P_SQ + SWA_WIDTH]
    qn_s[...] = sq * _head_rms_scale(sq) * qnw_ref[...]
    sk = proj_s[:, P_SK:P_SK + LANES]
    kn_s[...] = sk * _head_rms_scale(sk) * knw_ref[...]
    vn_s[...] = proj_s[:, P_SV:P_SV + LANES]

    lane32 = _div(_iota((1, SWA_HEADS * seq), 1), seq)
    sink_row = jnp.full((1, SWA_HEADS * seq), sink_ref[SWA_HEADS - 1], F32)
    for hh in range(SWA_HEADS - 2, -1, -1):
        sink_row = jnp.where(lane32 == hh, sink_ref[hh], sink_row)
    lo8 = _iota((seq, LANES), 1) < HALF
    heads_per_group = SSD_HEADS // SSD_GROUPS
    gw = SSD_WIDTH // SSD_GROUPS

    def per_seq(b, carry):
        r0 = pl.multiple_of(b * seq, seq)
        rb = pl.ds(r0, seq)
        s_prev = ssm_ref[b]
        cmb = cm_s[rb, :]
        bmb = bm_s[rb, :]
        xwb = xw_s[rb, :]
        yint_s[rb, :] = jnp.concatenate(
            [_dot_nt(cmb[:, g * SSD_STATE:(g + 1) * SSD_STATE], s_prev[g * gw:(g + 1) * gw, :])
             for g in range(SSD_GROUPS)], axis=1)
        upd = [_dot_tn(xwb[:, g * gw:(g + 1) * gw], bmb[:, g * SSD_STATE:(g + 1) * SSD_STATE])
               for g in range(SSD_GROUPS)]
        drow = ead_s[rb, :][seq - 1:seq, :]
        for hh in range(SSD_HEADS):
            g, r = divmod(hh, heads_per_group)
            hs = slice(hh * SSD_HEAD_DIM, (hh + 1) * SSD_HEAD_DIM)
            ssm_o[b, hs, :] = (s_prev[hs, :] * drow[:, hh:hh + 1]
                               + upd[g][r * SSD_HEAD_DIM:(r + 1) * SSD_HEAD_DIM, :])
        g_prev = gst_ref[b]
        g2 = jnp.concatenate([g_prev, g_prev], axis=1)
        s_bd = jnp.where(bd_mask, jnp.concatenate([g2, g2], axis=1), 0.0)
        oint_s[rb, :] = _dot(qe_s[rb, :], s_bd)
        u2 = jnp.where(bd_mask, _dot_tn(kd_s[rb, :], proj_s[rb, P_GV:P_GV + GLA_WIDTH]), 0.0)
        w = u2[:, :LANES] + u2[:, LANES:]
        erow = jnp.broadcast_to(ebt_s[rb, :][seq - 1:seq, :], (LANES, LANES))
        ecol = jnp.sum(jnp.where(eye, erow, 0.0), axis=-1, keepdims=True)
        gla_o[b] = g_prev * ecol + (w[:, :GLA_DV] + w[:, GLA_DV:])
        knb = kn_s[rb, :]
        vnb = vn_s[rb, :]
        kk_s[0:WINDOW, :] = kc_ref[b]
        kk_s[WINDOW:n_keys, :] = knb
        vv_s[0:WINDOW, :] = vc_ref[b]
        vv_s[WINDOW:n_keys, :] = vnb
        qa = qn_s[rb, 0:LANES]
        qb = qn_s[rb, LANES:2 * LANES]
        qs = jnp.concatenate([jnp.where(lo8, qa, 0.0), jnp.where(lo8, qb, 0.0),
                              jnp.where(lo8, 0.0, qa), jnp.where(lo8, 0.0, qb)], axis=0)
        l = _dot_nt(kk_s[...], qs) * (SWA_HEAD_DIM ** -0.5) + bias_s[...]
        m = jnp.maximum(jnp.max(l, axis=0, keepdims=True), sink_row)
        e = jnp.exp(l - m)
        den = jnp.sum(e, axis=0, keepdims=True) + jnp.exp(sink_row - m)
        o = _dot_tn(e / den, vv_s[...])
        tile_a = jnp.where(lo8, o[0:seq], o[2 * seq:3 * seq])
        tile_b = jnp.where(lo8, o[seq:2 * seq], o[3 * seq:4 * seq])
        oswa_s[rb, :] = jnp.concatenate([tile_a, tile_b], axis=1)
        ko_ref[b, 0:WINDOW - seq, :] = kc_ref[b, seq:WINDOW, :]
        ko_ref[b, WINDOW - seq:WINDOW, :] = knb
        vo_ref[b, 0:WINDOW - seq, :] = vc_ref[b, seq:WINDOW, :]
        vo_ref[b, WINDOW - seq:WINDOW, :] = vnb
        return carry

    lax.fori_loop(0, n_seq, per_seq, 0)

    y = y_intra + yint_s[...] * eacum_e + dsk_ref[...] * xs
    y = y * _silu(proj_s[:, P_Z:P_Z + SSD_WIDTH])
    y_ssd = _group_rmsnorm(y, snw_ref[...])
    o = o_intra + oint_s[...]
    y_gla = o * _head_rms_scale(o) * gnw_ref[...] * _silu(proj_s[:, P_GG:P_GG + GLA_WIDTH])
    y_swa = oswa_s[...] * _silu(proj_s[:, P_SG:P_SG + SWA_WIDTH])
    mix = jnp.concatenate([y_ssd, y_gla, y_swa], axis=1).astype(BF16)
    y_ref[...] = _epilogue(h_ref[...], mix, p_ref[...], wout_ref, wpg_ref, wpe_ref)


N_FRONT_INPUTS = 12


def _sample_front_kernel(seq, n_aliased, *refs):
    (h_ref, cst_ref, nw_ref, win_ref, cw_ref, cb_ref, dtb_ref, alog_ref, wgk_ref, bgk_ref,
     qnw_ref, knw_ref) = refs[:N_FRONT_INPUTS]
    (xs_ref, gates_ref, xt_ref, bt_ref, ct_ref, at_ref, qt_ref, kt_ref, egt_ref, vt_ref,
     qn_ref, kn_ref, vn_ref, conv_ref, u_s, xbc_s, proj_s) = refs[N_FRONT_INPUTS + n_aliased:]
    n_seq = BLK
    for t in range(seq):
        rows = slice(t * n_seq, (t + 1) * n_seq)
        ht = h_ref[rows, :]
        ms = jnp.mean(ht * ht, axis=-1, keepdims=True)
        u_s[rows, :] = (ht * lax.rsqrt(ms + EPS) * nw_ref[...]).astype(BF16)
    xbc_s[...] = jnp.dot(u_s[...], win_ref[:, :XBC_W], preferred_element_type=F32)
    proj_s[...] = jnp.dot(u_s[...], win_ref[:, XBC_W:], preferred_element_type=F32)
    cw = cw_ref[...]
    a_row = jnp.where(_iota((1, LANES), 1) < SSD_HEADS, -jnp.exp(alog_ref[...]), 0.0)
    for t in range(seq):
        rows = slice(t * n_seq, (t + 1) * n_seq)

        def raw_xbc(back):
            if t >= back:
                return xbc_s[(t - back) * n_seq:(t - back + 1) * n_seq, :]
            return cst_ref[0, SSD_CONV - 1 + t - back]

        acc = raw_xbc(SSD_CONV - 1) * cw[0:1, :]
        for k in range(1, SSD_CONV):
            acc = acc + raw_xbc(SSD_CONV - 1 - k) * cw[k:k + 1, :]
        xbc_c = _silu(acc + cb_ref[...])
        xs = xbc_c[:, :SSD_WIDTH]
        dtlr = proj_s[rows, P_DTLR:P_DTLR + LANES]
        dtv = _softplus(dtlr + dtb_ref[...])
        xs_ref[rows, :] = xs
        xt_ref[t] = (xs * _expand_heads(dtv, SSD_HEADS)).T
        bt_ref[t] = xbc_c[:, SSD_WIDTH:SSD_WIDTH + LANES].T
        ct_ref[t] = xbc_c[:, SSD_WIDTH + LANES:].T
        at_ref[t] = jnp.exp(dtv * a_row).T[:SSD_HEADS, :]
        glog = _log_sigmoid(_dot(dtlr, wgk_ref[...]) + bgk_ref[...]) * (1.0 / GLA_GATE_NORM)
        qt_ref[t] = (proj_s[rows, P_GQ:P_GQ + LANES] * (GLA_DK ** -0.5)).T
        kt_ref[t] = proj_s[rows, P_GK:P_GK + LANES].T
        egt_ref[t] = jnp.exp(glog).T
        vt_ref[t] = proj_s[rows, P_GV:P_GV + GLA_WIDTH].T
        sq = proj_s[rows, P_SQ:P_SQ + SWA_WIDTH]
        qn_ref[rows, :] = sq * _head_rms_scale(sq) * qnw_ref[...] * (SWA_HEAD_DIM ** -0.5)
        sk = proj_s[rows, P_SK:P_SK + LANES]
        kn_ref[rows, :] = sk * _head_rms_scale(sk) * knw_ref[...]
        vn_ref[rows, :] = proj_s[rows, P_SV:P_SV + LANES]
        gates_ref[rows, :] = jnp.concatenate(
            [_silu(proj_s[rows, P_Z:P_Z + SSD_WIDTH]), _silu(proj_s[rows, P_GG:P_GG + GLA_WIDTH]),
             _silu(proj_s[rows, P_SG:P_SG + SWA_WIDTH])], axis=1)
        if t >= seq - (SSD_CONV - 1):
            conv_ref[0, t - (seq - (SSD_CONV - 1))] = xbc_s[rows, :]


N_STATE_INPUTS = 19
SEQ_PER_STEP = 16


def _sample_state_kernel(seq, n_aliased, *refs):
    (xt_ref, bt_ref, ct_ref, at_ref, qt_ref, kt_ref, egt_ref, vt_ref, qn_ref, kn_ref, vn_ref,
     ssm_ref, gla_ref, kc_ref, vc_ref, bucket_c_ref, bucket_n_ref, rel_ref, sink_ref) = refs[:N_STATE_INPUTS]
    (ssm_o, gla_o, ko_ref, vo_ref, yt_ref, ot_ref, oswa_ref,
     qa_s, qb_s, krow_s, vrow_s, oa_s, ob_s, biasc_s, biasn_s) = refs[N_STATE_INPUTS + n_aliased:]
    j = pl.program_id(0)
    n_seq = LANES
    head_of_row = _div(_iota((SWA_HEADS * seq, LANES), 0), seq)

    def by_head(values):
        out = values[SWA_HEADS - 1]
        for hh in range(SWA_HEADS - 2, -1, -1):
            out = jnp.where(head_of_row == hh, values[hh], out)
        return out

    @pl.when(j == 0)
    def _():
        biasc_s[...] = by_head(_build_bias(bucket_c_ref[...], rel_ref))
        biasn_s[...] = by_head(_build_bias(bucket_n_ref[...], rel_ref))

    sub = _iota((SUBLANES, LANES), 0)
    a_rows = [jnp.sum(jnp.where(sub == j, at_ref[t], 0.0), axis=0, keepdims=True) for t in range(seq)]

    def ssd_body(p8, carry):
        r8 = pl.multiple_of(p8 * SUBLANES, SUBLANES)
        x_tiles = [xt_ref[t, pl.ds(r8, SUBLANES), :] for t in range(seq)]
        y_rows = [[] for _ in range(seq)]
        for pp in range(SUBLANES):
            r64 = pl.multiple_of((p8 * SUBLANES + pp) * SSD_STATE, SSD_STATE)
            slab = ssm_ref[0, 0, pl.ds(r64, SSD_STATE), :]
            for t in range(seq):
                slab = slab * a_rows[t] + x_tiles[t][pp:pp + 1, :] * bt_ref[t]
                y_rows[t].append(jnp.sum(ct_ref[t] * slab, axis=0, keepdims=True))
            ssm_o[0, 0, pl.ds(r64, SSD_STATE), :] = slab
        for t in range(seq):
            yt_ref[t, pl.ds(r8, SUBLANES), :] = jnp.concatenate(y_rows[t], axis=0)
        return carry

    lax.fori_loop(0, SSD_HEAD_DIM // SUBLANES, ssd_body, 0)

    @pl.when(j < GLA_HEADS)
    def _():
        for t in range(seq):
            ot_ref[t] = jnp.zeros((GLA_DV, LANES), F32)

        def gla_body(d8, carry):
            r8 = pl.multiple_of(d8 * SUBLANES, SUBLANES)
            q_tiles = [qt_ref[t, pl.ds(r8, SUBLANES), :] for t in range(seq)]
            k_tiles = [kt_ref[t, pl.ds(r8, SUBLANES), :] for t in range(seq)]
            g_tiles = [egt_ref[t, pl.ds(r8, SUBLANES), :] for t in range(seq)]
            for dd in range(SUBLANES):
                r64 = pl.multiple_of((d8 * SUBLANES + dd) * GLA_DV, GLA_DV)
                slab = gla_ref[0, 0, pl.ds(r64, GLA_DV), :]
                for t in range(seq):
                    slab = slab * g_tiles[t][dd:dd + 1, :] + k_tiles[t][dd:dd + 1, :] * vt_ref[t]
                    ot_ref[t] = ot_ref[t] + q_tiles[t][dd:dd + 1, :] * slab
                gla_o[0, 0, pl.ds(r64, GLA_DV), :] = slab
            return carry

        lax.fori_loop(0, GLA_DK // SUBLANES, gla_body, 0)

    base = pl.multiple_of(j * SEQ_PER_STEP, SEQ_PER_STEP)
    for t in range(seq):
        src = pl.ds(t * n_seq + base, SEQ_PER_STEP)
        dst = pl.ds(t, SEQ_PER_STEP, stride=seq)
        qa_s[dst, :] = qn_ref[src, 0:LANES]
        qb_s[dst, :] = qn_ref[src, LANES:2 * LANES]
        krow_s[dst, :] = kn_ref[src, :]
        vrow_s[dst, :] = vn_ref[src, :]
    kn_t = krow_s[...].T
    vn_t = vrow_s[...].T
    keep_old = _iota((LANES, WINDOW), 1) < WINDOW - seq
    lo8 = _iota((seq, LANES), 1) < HALF
    sink_col = by_head([jnp.full((SWA_HEADS * seq, LANES), sink_ref[hh], F32) for hh in range(SWA_HEADS)])[:, 0:1]

    def swa_body(bl, carry):
        r8 = pl.multiple_of(bl * seq, seq)
        kn_b = krow_s[pl.ds(r8, seq), :]
        vn_b = vrow_s[pl.ds(r8, seq), :]
        qa = qa_s[pl.ds(r8, seq), :]
        qb = qb_s[pl.ds(r8, seq), :]
        qs =jnp.concatenate([jnp.where(lo8, qa, 0.0), jnp.where(lo8, qb, 0.0),
                              jnp.where(lo8, 0.0, qa), jnp.where(lo8, 0.0, qb)], axis=0)
        k_old = kc_ref[0, bl]
        v_old = vc_ref[0, bl]
        lc = _dot(qs, k_old) + biasc_s[...]
        ln = _dot_nt(qs, kn_b) + biasn_s[:, 0:seq]
        m = jnp.maximum(jnp.maximum(jnp.max(lc, axis=-1, keepdims=True), jnp.max(ln, axis=-1, keepdims=True)),
                        sink_col)
        ec = jnp.exp(lc - m)
        en = jnp.exp(ln - m)
        den = jnp.sum(ec, axis=-1, keepdims=True) + jnp.sum(en, axis=-1, keepdims=True) + jnp.exp(sink_col - m)
        o = (_dot_nt(ec, v_old) + _dot(en, vn_b)) * (1.0 / den)
        oa_s[pl.ds(r8, seq), :] = jnp.where(lo8, o[0:seq], o[2 * seq:3 * seq])
        ob_s[pl.ds(r8, seq), :] = jnp.where(lo8, o[seq:2 * seq], o[3 * seq:4 * seq])
        ko_ref[0, bl] = jnp.where(keep_old, pltpu.roll(k_old, WINDOW - seq, axis=1),
                                  pltpu.roll(kn_t, WINDOW - seq - r8, axis=1))
        vo_ref[0, bl] = jnp.where(keep_old, pltpu.roll(v_old, WINDOW - seq, axis=1),
                                  pltpu.roll(vn_t, WINDOW - seq - r8, axis=1))
        return carry

    lax.fori_loop(0, SEQ_PER_STEP, swa_body, 0)
    for t in range(seq):
        src = pl.ds(t, SEQ_PER_STEP, stride=seq)
        oswa_ref[t] = jnp.concatenate([oa_s[src, :], ob_s[src, :]], axis=1)


def _sample_back_kernel(seq, yt_ref, ot_ref, oswa_ref, xs_ref, gates_ref, h_ref, p_ref, dsk_ref, snw_ref, gnw_ref,
                        wout_ref, wpe_ref, wpg_ref, y_ref, mix_s):
    n_seq = BLK
    for t in range(seq):
        rows = slice(t * n_seq, (t + 1) * n_seq)
        y = (yt_ref[t].T + dsk_ref[...] * xs_ref[rows, :]) * gates_ref[rows, 0:SSD_WIDTH]
        mix_s[rows, 0:SSD_WIDTH] = _group_rmsnorm(y, snw_ref[...]).astype(BF16)
        o = ot_ref[t].T
        y_gla = o * _head_rms_scale(o) * gnw_ref[...] * gates_ref[rows, SSD_WIDTH:SSD_WIDTH + GLA_WIDTH]
        mix_s[rows, SSD_WIDTH:SSD_WIDTH + GLA_WIDTH] = y_gla.astype(BF16)
        mix_s[rows, SSD_WIDTH + GLA_WIDTH:] = (oswa_ref[t] * gates_ref[rows, SSD_WIDTH + GLA_WIDTH:]).astype(BF16)
    y_ref[...] = _epilogue(h_ref[...], mix_s[...], p_ref[0], wout_ref, wpg_ref, wpe_ref)


def _const_spec(shape):
    nd = len(shape)
    return pl.BlockSpec(shape, lambda *_: (0,) * nd)


def _resident_spec(shape):
    nd = len(shape)
    return pl.BlockSpec(shape, lambda *_: (0,) * nd, pipeline_mode=pl.Buffered(1))


def _smem_spec():
    return pl.BlockSpec(memory_space=pltpu.SMEM)


def _layer_weights(i, w):
    ops = [w["norm_w"][i], w["w_in"][i], w["conv_w"][i], w["conv_b"][i], w["dt_bias"][i], w["a_log"][i],
           w["d_skip"][i], w["ssd_norm_w"][i], w["gla_w_gk"][i], w["gla_b_gk"][i], w["gla_norm_w"][i],
           w["q_norm_w"][i], w["k_norm_w"][i], w["w_out"][i], w["w_pe"][i], w["w_pg"][i]]
    return ops, [_resident_spec(o.shape) if o.dtype == BF16 else _const_spec(o.shape) for o in ops]


def _prompt_layer(layer, depth, bsz, h, p_all, prev_states, bucket, rel, sinks, wops, wspecs):
    rows_total, _ = h.shape
    seq_len = rows_total // bsz
    chunk = PROMPT_CHUNK_ROWS
    pair = 2 * chunk
    chunks_per_seq = seq_len // chunk
    assert seq_len % pair == 0 and chunks_per_seq & (chunks_per_seq - 1) == 0
    n_pairs = rows_total // pair
    kern = functools.partial(_prompt_kernel, chunks_per_seq, len(prev_states))
    proj_rows = pl.BlockSpec((pair, D_MODEL), lambda k: (jnp.minimum(k, n_pairs - 1), 0))
    out_rows = pl.BlockSpec((pair, D_MODEL), lambda k: (jnp.maximum(k - 1, 0), 0))
    p_spec = pl.BlockSpec((1, pair, PLE_DIM), lambda k: (layer, jnp.maximum(k - 1, 0), 0))
    per_seq = lambda s: pl.BlockSpec(
        (1, 1) + s, lambda k: (layer, jnp.maximum(2 * k - 1, 0) // chunks_per_seq) + (0,) * len(s))
    state_shapes = ((SSD_WIDTH, SSD_STATE), (SSD_CONV - 1, SSD_CONV_DIM), (GLA_HEADS * GLA_DK, GLA_DV),
                    (LANES, WINDOW), (LANES, WINDOW))
    out_shape = (jax.ShapeDtypeStruct((rows_total, D_MODEL), F32),) + tuple(
        jax.ShapeDtypeStruct((depth, bsz) + s, F32) for s in state_shapes)
    return pl.pallas_call(
        kern,
        grid=(n_pairs + 1,),
        in_specs=[proj_rows, out_rows, p_spec, _const_spec(bucket.shape), _smem_spec(), _smem_spec()]
        + wspecs + [pl.BlockSpec(memory_space=pl.ANY)] * len(prev_states),
        out_specs=(out_rows,) + tuple(per_seq(s) for s in state_shapes),
        out_shape=out_shape,
        input_output_aliases={N_PROMPT_INPUTS + k: 1 + k for k in range(len(prev_states))},
        scratch_shapes=[
            pltpu.VMEM((chunk, PROJ_W), F32), pltpu.VMEM((chunk, PROJ_W), F32),
            pltpu.VMEM((chunk, XBC_W), F32), pltpu.VMEM((chunk, XBC_W), F32),
            pltpu.VMEM((chunk, D_MODEL), BF16), pltpu.VMEM((chunk, D_MODEL), BF16),
            pltpu.VMEM((chunk, D_MODEL), BF16),
            pltpu.VMEM((chunk, D_MODEL), F32),
            pltpu.VMEM((SUBLANES, XBC_W), F32),
            pltpu.VMEM((BLK, SSD_WIDTH), F32),
            pltpu.VMEM((LANES, GLA_WIDTH), F32),
            pltpu.VMEM((2 * BLK, LANES), F32),
            pltpu.VMEM((2 * BLK, LANES), F32),
            pltpu.VMEM((2 * SWA_HEADS, BLK, 2 * BLK), F32),
        ],
        compiler_params=pltpu.CompilerParams(
            dimension_semantics=("arbitrary",), vmem_limit_bytes=VMEM_LIMIT_BYTES),
        name="prompt_layer",
    )(h, h, p_all, bucket, rel, sinks, *wops, *prev_states)


def _sample_layer(h, p, ssm, conv_pad, gla, kc, vc, bucket, rel, sinks, wops, wspecs, seq):
    rows_total = h.shape[0]
    n_seq = BLK // seq
    rows = n_seq * seq
    bsz = rows_total // seq
    kern = functools.partial(_sample_kernel, seq, n_seq)
    row_spec = lambda w: pl.BlockSpec((rows, w), lambda s: (s, 0))
    seq_spec = lambda a, c: pl.BlockSpec((n_seq, a, c), lambda s: (s, 0, 0))
    out_shape = (
        jax.ShapeDtypeStruct((rows_total, D_MODEL), F32),
        jax.ShapeDtypeStruct((bsz, SSD_WIDTH, SSD_STATE), F32),
        jax.ShapeDtypeStruct((rows_total, SSD_CONV_DIM), F32),
        jax.ShapeDtypeStruct((bsz, GLA_HEADS * GLA_DK, GLA_DV), F32),
        jax.ShapeDtypeStruct((bsz, WINDOW, LANES), F32),
        jax.ShapeDtypeStruct((bsz, WINDOW, LANES), F32),
    )
    n_keys = WINDOW + seq
    f32_scratch = lambda r, c: pltpu.VMEM((r, c), F32)
    return pl.pallas_call(
        kern,
        grid=(rows_total // rows,),
        in_specs=[row_spec(D_MODEL), row_spec(PLE_DIM), seq_spec(SSD_WIDTH, SSD_STATE), row_spec(SSD_CONV_DIM),
                  seq_spec(GLA_HEADS * GLA_DK, GLA_DV), seq_spec(WINDOW, LANES), seq_spec(WINDOW, LANES),
                  _const_spec(bucket.shape), _smem_spec(), _smem_spec()] + wspecs,
        out_specs=(row_spec(D_MODEL), seq_spec(SSD_WIDTH, SSD_STATE), row_spec(SSD_CONV_DIM),
                   seq_spec(GLA_HEADS * GLA_DK, GLA_DV), seq_spec(WINDOW, LANES), seq_spec(WINDOW, LANES)),
        out_shape=out_shape,
        scratch_shapes=[
            f32_scratch(rows, PROJ_W),
            f32_scratch(rows + SUBLANES, XBC_W),
            f32_scratch(rows + SUBLANES, XBC_W),
            f32_scratch(rows, LANES), f32_scratch(rows, LANES), f32_scratch(rows, SSD_WIDTH),
            f32_scratch(rows, LANES), f32_scratch(rows, SSD_WIDTH),
            f32_scratch(rows, LANES), f32_scratch(rows, LANES), f32_scratch(rows, LANES),
            f32_scratch(rows, GLA_WIDTH),
            f32_scratch(rows, SWA_WIDTH), f32_scratch(rows, LANES), f32_scratch(rows, LANES),
            f32_scratch(rows, SWA_WIDTH),
            f32_scratch(n_keys, LANES), f32_scratch(n_keys, LANES),
            f32_scratch(n_keys, SWA_HEADS * seq),
        ],
        compiler_params=pltpu.CompilerParams(
            dimension_semantics=("arbitrary",), vmem_limit_bytes=VMEM_LIMIT_BYTES),
        name="sample_layer",
    )(h, p, ssm, conv_pad, gla, kc, vc, bucket, rel, sinks, *wops)


def _whole(shape, layer=None):
    if layer is None:
        return pl.BlockSpec(shape, lambda *_: (0,) * len(shape), pipeline_mode=pl.Buffered(1))
    return pl.BlockSpec((1,) + shape[1:], lambda *_: (layer,) + (0,) * (len(shape) - 1),
                        pipeline_mode=pl.Buffered(1))


def _sample_layer_native(layer, depth, seq, h, p_all, conv_in, ssm_in, gla_in, kc_in, vc_in, prev_states,
                         buckets, rel, sinks, wops):
    (nw, win, cw, cb, dtb, alog, dsk, snw, wgk, bgk, gnw, qnw, knw, wout, wpe, wpg) = wops
    rows = h.shape[0]
    n_seq = rows // seq
    assert n_seq == LANES and n_seq % SEQ_PER_STEP == 0 and SSD_HEADS * SEQ_PER_STEP == n_seq
    prev_conv, prev_rest = (prev_states[:1], prev_states[1:]) if prev_states else ((), ())
    f32 = lambda *s: jax.ShapeDtypeStruct(s, F32)
    cparams = lambda sem: pltpu.CompilerParams(dimension_semantics=sem, vmem_limit_bytes=VMEM_LIMIT_BYTES)

    front_in = [h, conv_in, nw, win, cw, cb, dtb, alog, wgk, bgk, qnw, knw]
    front_specs = [_whole(h.shape), _whole(conv_in.shape, layer)] + [_whole(a.shape) for a in front_in[2:]]
    front_out = (f32(rows, SSD_WIDTH), f32(rows, D_MODEL),
                 f32(seq, SSD_WIDTH, n_seq), f32(seq, LANES, n_seq), f32(seq, LANES, n_seq),
                 f32(seq, SSD_HEADS, n_seq), f32(seq, LANES, n_seq), f32(seq, LANES, n_seq), f32(seq, LANES, n_seq),
                 f32(seq, GLA_WIDTH, n_seq), f32(rows, SWA_WIDTH), f32(rows, LANES), f32(rows, LANES),
                 f32(*conv_in.shape))
    (xs, gates, xt, bt, ct, at, qt, kt, egt, vt, qn, kn, vn, conv_o) = pl.pallas_call(
        functools.partial(_sample_front_kernel, seq, len(prev_conv)),
        grid=(1,),
        in_specs=front_specs + [pl.BlockSpec(memory_space=pl.ANY)] * len(prev_conv),
        out_specs=tuple(_whole(o.shape) for o in front_out[:-1]) + (_whole(conv_in.shape, layer),),
        out_shape=front_out,
        input_output_aliases={len(front_in) + k: len(front_out) - 1 + k for k in range(len(prev_conv))},
        scratch_shapes=[pltpu.VMEM((rows, D_MODEL), BF16), pltpu.VMEM((rows, XBC_W), F32),
                        pltpu.VMEM((rows, PROJ_W), F32)],
        compiler_params=cparams(("arbitrary",)),
        name="sample_front",
    )(*front_in, *prev_conv)

    n_steps = SSD_HEADS
    per_group = SSD_HEADS // SSD_GROUPS
    gla_head = lambda j: jnp.minimum(j, GLA_HEADS - 1)
    blk3 = lambda n, f: pl.BlockSpec((seq, n, n_seq), lambda j: (0, f(j), 0))
    state_in = [xt, bt, ct, at, qt, kt, egt, vt, qn, kn, vn, ssm_in, gla_in, kc_in, vc_in, buckets[0], buckets[1],
                rel, sinks]
    ssm_spec = pl.BlockSpec((1, 1) + ssm_in.shape[2:], lambda j: (layer, j, 0, 0))
    gla_spec = pl.BlockSpec((1, 1) + gla_in.shape[2:], lambda j: (layer, gla_head(j), 0, 0))
    kv_spec = pl.BlockSpec((1, SEQ_PER_STEP) + kc_in.shape[2:], lambda j: (layer, j, 0, 0))
    state_specs = [blk3(SSD_HEAD_DIM, lambda j: j), blk3(SSD_STATE, lambda j: j // per_group),
                   blk3(SSD_STATE, lambda j: j // per_group), _const_spec(at.shape),
                   blk3(GLA_DK, gla_head), blk3(GLA_DK, gla_head), blk3(GLA_DK, gla_head), blk3(GLA_DV, gla_head),
                   _const_spec(qn.shape), _const_spec(kn.shape), _const_spec(vn.shape),
                   ssm_spec, gla_spec, kv_spec, kv_spec,
                   _const_spec(buckets[0].shape), _const_spec(buckets[1].shape), _smem_spec(), _smem_spec()]
    state_out = (f32(*ssm_in.shape), f32(*gla_in.shape), f32(*kc_in.shape), f32(*vc_in.shape),
                 f32(seq, SSD_WIDTH, n_seq), f32(seq, GLA_WIDTH, n_seq), f32(seq, n_seq, SWA_WIDTH))
    ssm_o, gla_o, ko, vo, yt, ot, oswa = pl.pallas_call(
        functools.partial(_sample_state_kernel, seq, len(prev_rest)),
        grid=(n_steps,),
        in_specs=state_specs + [pl.BlockSpec(memory_space=pl.ANY)] * len(prev_rest),
        out_specs=(ssm_spec, gla_spec, kv_spec, kv_spec, blk3(SSD_HEAD_DIM, lambda j: j), blk3(GLA_DV, gla_head),
                   pl.BlockSpec((seq, SEQ_PER_STEP, SWA_WIDTH), lambda j: (0, j, 0))),
        out_shape=state_out,
        input_output_aliases={len(state_in) + k: k for k in range(len(prev_rest))},
        scratch_shapes=[pltpu.VMEM((SEQ_PER_STEP * seq, LANES), F32)] * 6
        + [pltpu.VMEM((SWA_HEADS * seq, LANES), F32)] * 2,
        compiler_params=cparams(("arbitrary",)),
        name="sample_state",
    )(*state_in, *prev_rest)

    back_in = [yt, ot, oswa, xs, gates, h, p_all, dsk, snw, gnw, wout, wpe, wpg]
    back_specs = [_whole(a.shape) for a in back_in[:6]] + [_whole(p_all.shape, layer)] + [
        _whole(a.shape) for a in back_in[7:]]
    y = pl.pallas_call(
        functools.partial(_sample_back_kernel, seq),
        grid=(1,),
        in_specs=back_specs,
        out_specs=_whole((rows, D_MODEL)),
        out_shape=f32(rows, D_MODEL),
        scratch_shapes=[pltpu.VMEM((rows, D_MODEL), BF16)],
        compiler_params=cparams(("arbitrary",)),
        name="sample_back",
    )(*back_in)
    return y, (conv_o, ssm_o, gla_o, ko, vo)


SWA_HEAD_ORDER = (0, 2, 1, 3)


def _win_tile_runs():
    sizes = (SSD_WIDTH, SSD_CONV_DIM, SSD_HEADS, GLA_HEADS * GLA_DK, GLA_HEADS * GLA_DK, GLA_WIDTH, GLA_WIDTH,
             GLA_RANK, SWA_WIDTH, SWA_KV_HEADS * SWA_HEAD_DIM, SWA_KV_HEADS * SWA_HEAD_DIM, SWA_WIDTH)
    offs = np.concatenate([[0], np.cumsum(sizes)])
    seg = lambda k: np.arange(offs[k], offs[k + 1])
    z, xbc, dt, gq, gk, gv, gg, glr, sq, sk, sv, sg = [seg(k) for k in range(len(sizes))]
    heads = lambda a: np.concatenate([a[h * SWA_HEAD_DIM:(h + 1) * SWA_HEAD_DIM] for h in SWA_HEAD_ORDER])
    pad = np.full(LANES - SSD_HEADS - GLA_RANK, -1)
    src = np.concatenate([xbc, z, gq, gk, gv, gg, heads(sq), sk, sv, heads(sg), dt, glr, pad])
    assert src.size == XBC_W + PROJ_W
    tiles = []
    for j in range(src.size // LANES):
        idx = src[j * LANES:(j + 1) * LANES]
        cuts = [0] + [k for k in range(1, LANES) if (idx[k] != idx[k - 1] + 1 and not (idx[k] == -1 == idx[k - 1]))]
        runs = [(int(idx[a]), b - a) for a, b in zip(cuts, cuts[1:] + [LANES])]
        assert all(n % SUBLANES == 0 and (s < 0 or s % SUBLANES == 0) for s, n in runs)
        tiles.append(runs)
    return tiles


def _win_prep_kernel(tile_runs, wt_ref, out_ref):
    for j, runs in enumerate(tile_runs):
        parts = [jnp.zeros((n, D_MODEL), F32) if s < 0 else wt_ref[0, s:s + n, :] for s, n in runs]
        tile = parts[0] if len(parts) == 1 else jnp.concatenate(parts, axis=0)
        out_ref[0, :, j * LANES:(j + 1) * LANES] = tile.T.astype(BF16)


def _prepare_w_in(w_in):
    depth, d_model, d_in = w_in.shape
    w_t = jnp.swapaxes(w_in, 1, 2)
    return pl.pallas_call(
        functools.partial(_win_prep_kernel, _win_tile_runs()),
        grid=(depth,),
        in_specs=[pl.BlockSpec((1, d_in, d_model), lambda l: (l, 0, 0))],
        out_specs=pl.BlockSpec((1, d_model, XBC_W + PROJ_W), lambda l: (l, 0, 0)),
        out_shape=jax.ShapeDtypeStruct((depth, d_model, XBC_W + PROJ_W), BF16),
        compiler_params=pltpu.CompilerParams(
            dimension_semantics=("arbitrary",), vmem_limit_bytes=VMEM_LIMIT_BYTES),
        name="w_in_prep",
    )(w_t)


def _prepare_weights(norm_w, w_in, conv_w, conv_b, dt_bias, a_log, d_skip, ssd_norm_w, gla_w_gk, gla_b_gk,
                     gla_norm_w, q_norm_w, k_norm_w, w_out, w_pe, w_pg):
    mix_w = SSD_WIDTH + GLA_WIDTH
    w_out_p = jnp.concatenate(
        [w_out[:, :mix_w, :]] + [w_out[:, mix_w + h * SWA_HEAD_DIM:mix_w + (h + 1) * SWA_HEAD_DIM, :]
                                 for h in SWA_HEAD_ORDER], axis=1).astype(BF16)
    lane_pad = lambda x: jnp.pad(x, ((0, 0), (0, LANES - x.shape[-1])))[:, None, :]
    wgk_p = jnp.pad(gla_w_gk, ((0, 0), (LR_LANE0, LANES - LR_LANE0 - GLA_RANK), (0, 0))).astype(BF16)
    return dict(
        norm_w=norm_w[:, None, :], w_in=_prepare_w_in(w_in), conv_w=conv_w, conv_b=conv_b[:, None, :],
        dt_bias=lane_pad(dt_bias), a_log=lane_pad(a_log),
        d_skip=jnp.repeat(d_skip, SSD_HEAD_DIM, axis=-1)[:, None, :], ssd_norm_w=ssd_norm_w[:, None, :],
        gla_w_gk=wgk_p, gla_b_gk=gla_b_gk[:, None, :],
        gla_norm_w=jnp.tile(gla_norm_w, (1, GLA_HEADS))[:, None, :],
        q_norm_w=jnp.tile(q_norm_w, (1, SWA_HEADS))[:, None, :],
        k_norm_w=jnp.tile(k_norm_w, (1, SWA_KV_HEADS))[:, None, :],
        w_out=w_out_p, w_pe=w_pe.astype(BF16), w_pg=w_pg.astype(BF16))


def kernel(x_prompt, x_sample, state_ssm, state_conv, state_gla, cache_swa_k, cache_swa_v, p_prompt, p_sample, rel_bias, norm_w, w_in, conv_w, conv_b, dt_bias, a_log, d_skip, ssd_norm_w, gla_w_gk, gla_b_gk, gla_norm_w, q_norm_w, k_norm_w, attn_sinks, w_out, w_pe, w_pg):
    depth = w_in.shape[0]
    bp, seq_p, _ = x_prompt.shape
    bs, seq_s, _ = x_sample.shape
    assert seq_s == SUBLANES and BLK % seq_s == 0 and (bs * seq_s) % BLK == 0
    assert cache_swa_k.shape[2] == WINDOW
    w = _prepare_weights(norm_w, w_in, conv_w, conv_b, dt_bias, a_log, d_skip, ssd_norm_w, gla_w_gk, gla_b_gk,
                         gla_norm_w, q_norm_w, k_norm_w, w_out, w_pe, w_pg)
    rel_flat = rel_bias.reshape(-1)
    dist_p = WINDOW + np.arange(BLK)[:, None] - np.arange(2 * BLK)[None, :]
    bucket_p = jnp.asarray(_bucket_table(dist_p))
    t_of_row = np.tile(np.arange(seq_s), SWA_HEADS)[:, None]
    bucket_c = jnp.asarray(_bucket_table(WINDOW + t_of_row - np.arange(WINDOW)[None, :]))
    dist_n = np.where(np.arange(LANES)[None, :] < seq_s, t_of_row - np.arange(LANES)[None, :], -1)
    bucket_n = jnp.asarray(_bucket_table(dist_n))

    ssm_in = jnp.transpose(state_ssm, (0, 2, 3, 4, 1)).reshape(depth, SSD_HEADS, SSD_HEAD_DIM * SSD_STATE, bs)
    gla_in = jnp.transpose(state_gla, (0, 2, 3, 4, 1)).reshape(depth, GLA_HEADS, GLA_DK * GLA_DV, bs)
    kv_in = lambda a: jnp.transpose(a, (0, 1, 3, 4, 2)).reshape(depth, bs, SWA_KV_HEADS * SWA_HEAD_DIM, WINDOW)
    kc_in, vc_in = kv_in(cache_swa_k), kv_in(cache_swa_v)
    conv_in = jnp.transpose(state_conv, (0, 2, 1, 3))

    hp = x_prompt.reshape(bp * seq_p, D_MODEL)
    p_prompt_rows = p_prompt.reshape(depth, bp * seq_p, PLE_DIM)
    hs = jnp.transpose(x_sample, (1, 0, 2)).reshape(seq_s * bs, D_MODEL)
    p_sample_rows = jnp.transpose(p_sample, (0, 2, 1, 3)).reshape(depth, seq_s * bs, PLE_DIM)
    states_p = ()
    states_s = ()
    for i in range(depth):
        wops, wspecs = _layer_weights(i, w)
        hp, *states_p = _prompt_layer(i, depth, bp, hp, p_prompt_rows, tuple(states_p), bucket_p, rel_flat,
                                      attn_sinks[i], wops, wspecs)
        hs, states_s = _sample_layer_native(i, depth, seq_s, hs, p_sample_rows, conv_in, ssm_in, gla_in, kc_in,
                                            vc_in, states_s, (bucket_c, bucket_n), rel_flat, attn_sinks[i], wops)
    ssm_p, conv_p, gla_p, kt_p, vt_p = states_p
    conv_s, ssm_s, gla_s, kt_s, vt_s = states_s
    unpack_kv = lambda a: jnp.transpose(
        a.reshape(a.shape[:2] + (SWA_KV_HEADS, SWA_HEAD_DIM, WINDOW)), (0, 1, 4, 2, 3))
    outs_p = (ssm_p.reshape(depth, bp, SSD_HEADS, SSD_HEAD_DIM, SSD_STATE), conv_p,
              gla_p.reshape(depth, bp, GLA_HEADS, GLA_DK, GLA_DV), unpack_kv(kt_p), unpack_kv(vt_p))
    seq_last = lambda a, dims: jnp.transpose(a.reshape(a.shape[:2] + dims + (bs,)), (0, 4, 1, 2, 3))
    outs_s = (seq_last(ssm_s, (SSD_HEAD_DIM, SSD_STATE)), jnp.transpose(conv_s, (0, 2, 1, 3)),
              seq_last(gla_s, (GLA_DK, GLA_DV)), unpack_kv(kt_s), unpack_kv(vt_s))
    y_sample = jnp.transpose(hs.reshape(seq_s, bs, D_MODEL), (1, 0, 2))
    return (hp.reshape(bp, seq_p, D_MODEL), y_sample) + outs_p + outs_s
```

```python
import functools
import math

import numpy as np
import jax
import jax.numpy as jnp
from jax import lax
from jax.experimental import pallas as pl
from jax.experimental.pallas import tpu as pltpu

D_MODEL = 1024
DEPTH = 2
SSD_HEADS = 8
SSD_HEAD_DIM = 64
SSD_WIDTH = SSD_HEADS * SSD_HEAD_DIM
SSD_GROUPS = 2
SSD_STATE = 64
SSD_CONV = 4
SSD_CONV_DIM = SSD_WIDTH + 2 * SSD_GROUPS * SSD_STATE
SSD_CHUNK = 128
GLA_HEADS = 4
GLA_DK = 32
GLA_DV = 64
GLA_WIDTH = GLA_HEADS * GLA_DV
GLA_RANK = 16
GLA_GATE_NORM = 16.0
GLA_CHUNK = 64
SWA_HEADS = 4
SWA_KV_HEADS = 2
SWA_HEAD_DIM = 64
SWA_WIDTH = SWA_HEADS * SWA_HEAD_DIM
WINDOW = 128
REL_BUCKETS = 32
REL_MAX_DIST = 128
PLE_DIM = 256
EPS = 1e-6

LANES = 128
SUBLANES = 8
HALF = LANES // 2
BLK = 128
VMEM_LIMIT_BYTES = 56 * 1024 * 1024

XBC_W = SSD_CONV_DIM
P_Z = 0
P_GQ = P_Z + SSD_WIDTH
P_GK = P_GQ + LANES
P_GV = P_GK + LANES
P_GG = P_GV + GLA_WIDTH
P_SQ = P_GG + GLA_WIDTH
P_SK = P_SQ + SWA_WIDTH
P_SV = P_SK + LANES
P_SG = P_SV + LANES
P_DTLR = P_SG + SWA_WIDTH
PROJ_W = P_DTLR + LANES
LR_LANE0 = SSD_HEADS

F32 = jnp.float32
BF16 = jnp.bfloat16
NEG_INF = float("-inf")
N_PROMPT_INPUTS = 22
PROMPT_CHUNK_ROWS = 2 * BLK
TICKS_PER_ITEM = 2
NT_DIMS = (((1,), (1,)), ((), ()))
TN_DIMS = (((0,), (0,)), ((), ()))


def _iota(shape, dim):
    return lax.broadcasted_iota(jnp.int32, shape, dim)


def _div(x, d):
    return x >> (d.bit_length() - 1)


def _mod(x, d):
    return x & (d - 1)


def _softplus(x):
    e = jnp.exp(-jnp.abs(x))
    u = 1.0 + e
    d = u - 1.0
    log1p_e = jnp.where(d == 0.0, e, jnp.log(u) * (e / jnp.where(d == 0.0, 1.0, d)))
    return jnp.maximum(x, 0.0) + log1p_e


def _log_sigmoid(x):
    return jnp.minimum(x, 0.0) - jnp.log(1.0 + jnp.exp(-jnp.abs(x)))


def _silu(x):
    return x * jax.nn.sigmoid(x)


def _dot(a, b):
    return jnp.dot(a.astype(BF16), b.astype(BF16), preferred_element_type=F32)


def _dot_nt(a, b):
    return lax.dot_general(a.astype(BF16), b.astype(BF16), NT_DIMS, preferred_element_type=F32)


def _dot_tn(a, b):
    return lax.dot_general(a.astype(BF16), b.astype(BF16), TN_DIMS, preferred_element_type=F32)


def _dot_exact(sel, x):
    x1 = x.astype(BF16)
    r1 = x - x1.astype(F32)
    x2 = r1.astype(BF16)
    x3 = (r1 - x2.astype(F32)).astype(BF16)
    dot = functools.partial(jnp.dot, sel, preferred_element_type=F32)
    return dot(x1) + dot(x2) + dot(x3)


def _expand_heads(x, n_heads):
    rows = x.shape[0]
    lo = _iota((rows, LANES), 1) < HALF
    tiles = []
    for j in range(n_heads // 2):
        a = jnp.broadcast_to(x[:, 2 * j:2 * j + 1], (rows, LANES))
        b = jnp.broadcast_to(x[:, 2 * j + 1:2 * j + 2], (rows, LANES))
        tiles.append(jnp.where(lo, a, b))
    return jnp.concatenate(tiles, axis=1)


def _head_rms_scale(x):
    rows, width = x.shape
    lo = _iota((rows, LANES), 1) < HALF
    outs = []
    for j in range(width // LANES):
        t = x[:, j * LANES:(j + 1) * LANES]
        sq = t * t
        s_lo = jnp.sum(jnp.where(lo, sq, 0.0), axis=-1, keepdims=True)
        s_hi = jnp.sum(jnp.where(lo, 0.0, sq), axis=-1, keepdims=True)
        outs.append(lax.rsqrt(jnp.where(lo, s_lo, s_hi) * (1.0 / HALF) + EPS))
    return outs[0] if len(outs) == 1 else jnp.concatenate(outs, axis=1)


def _group_rmsnorm(y, w):
    gw = SSD_WIDTH // SSD_GROUPS
    outs = []
    for g in range(SSD_GROUPS):
        t = y[:, g * gw:(g + 1) * gw]
        ms = jnp.sum(t * t, axis=-1, keepdims=True) * (1.0 / gw)
        outs.append(t * lax.rsqrt(ms + EPS))
    return jnp.concatenate(outs, axis=1) * w


def _rel_bucket_np(dist):
    n = np.maximum(dist, 0)
    exact = REL_BUCKETS // 2
    nf = np.maximum(n, 1).astype(np.float64)
    large = exact + (np.log(nf / exact) / math.log(REL_MAX_DIST / exact) * (REL_BUCKETS - exact)).astype(np.int32)
    large = np.minimum(large, REL_BUCKETS - 1)
    return np.where(n < exact, n, large).astype(np.int32)


def _bucket_table(dist):
    return np.where((dist >= 0) & (dist < WINDOW), _rel_bucket_np(dist), -1).astype(np.int32)


def _no_tick():
    pass


def _ssd_intra(xbc_c, dtlr, dtb, a_row, pair_mask, tri, total_of, tick=_no_tick):
    xs = xbc_c[:, :SSD_WIDTH]
    bm = xbc_c[:, SSD_WIDTH:SSD_WIDTH + LANES]
    cm = xbc_c[:, SSD_WIDTH + LANES:]
    lane = _iota((BLK, LANES), 1)
    lo = lane < HALF
    dtv = _softplus(dtlr + dtb)
    adt = dtv * a_row
    acum = _dot_exact(tri, adt)
    tick()
    acum_t = acum.T
    eacum = jnp.exp(acum)
    tail = jnp.exp(total_of(adt, acum) - acum)
    dtv_e = _expand_heads(dtv, SSD_HEADS)
    eacum_e = _expand_heads(eacum, SSD_HEADS)
    tail_e = _expand_heads(tail, SSD_HEADS)
    tick()
    xdt = xs * dtv_e
    xw = xdt * tail_e
    cb = [_dot_nt(jnp.where(lo, cm, 0.0), bm), _dot_nt(jnp.where(lo, 0.0, cm), bm)]
    y_pairs = []
    for j in range(SSD_HEADS // 2):
        tick()
        g = (2 * j) // (SSD_HEADS // SSD_GROUPS)
        ms = []
        for k in range(2):
            h = 2 * j + k
            seg = acum[:, h:h + 1] - acum_t[h:h + 1, :]
            dec = jnp.where(pair_mask, jnp.exp(seg), 0.0)
            ms.append((cb[g] * dec).astype(BF16))
        xp = xdt[:, j * LANES:(j + 1) * LANES]
        rhs = jnp.concatenate([jnp.where(lo, xp, 0.0), jnp.where(lo, 0.0, xp)], axis=0)
        y_pairs.append(_dot(jnp.concatenate(ms, axis=1), rhs))
    y_intra = jnp.concatenate(y_pairs, axis=1)
    return y_intra, xs, bm, cm, xw, eacum, eacum_e


def _gla_intra(gq, gk, gv, glog, tri, total_of, att_masks, tick=_no_tick):
    bcs = _dot_exact(tri, glog)
    tick()
    eb = jnp.exp(bcs)
    qe = gq * (GLA_DK ** -0.5) * eb
    ke = gk * jnp.exp(-bcs)
    btot = total_of(glog, bcs)
    kd = gk * jnp.exp(btot - bcs)
    lane_k = _iota((GLA_CHUNK, LANES), 1)
    lane_v = _iota((GLA_CHUNK, GLA_WIDTH), 1)
    outs = []
    for c2 in range(BLK // GLA_CHUNK):
        tick()
        rs = slice(c2 * GLA_CHUNK, (c2 + 1) * GLA_CHUNK)
        ke_c = ke[rs]
        v_c = gv[rs]
        kbd = jnp.concatenate(
            [jnp.where(_div(lane_k, GLA_DK) == h, ke_c, 0.0) for h in range(GLA_HEADS)], axis=0)
        att = _dot_nt(qe[rs], kbd)
        att = jnp.where(att_masks[c2], att, 0.0)
        vbd = jnp.concatenate(
            [jnp.where(_div(lane_v, GLA_DV) == h, v_c, 0.0) for h in range(GLA_HEADS)], axis=0)
        outs.append(_dot(att, vbd))
    return jnp.concatenate(outs, axis=0), qe, kd, jnp.exp(btot)


def _build_bias(bucket, rel_ref, n_heads_in_lanes=None):
    accs = [jnp.full(bucket.shape, NEG_INF, F32) for _ in range(SWA_HEADS)]
    for b in range(REL_BUCKETS):
        hit = bucket == b
        for h in range(SWA_HEADS):
            accs[h] = jnp.where(hit, rel_ref[b * SWA_HEADS + h], accs[h])
    return accs


def _epilogue(h, mix, p, wout_ref, wpg_ref, wpe_ref):
    h1 = h + jnp.dot(mix, wout_ref[...], preferred_element_type=F32)
    gate = jax.nn.sigmoid(jnp.dot(h1.astype(BF16), wpg_ref[...], preferred_element_type=F32))
    pe = jnp.dot(p.astype(BF16), wpe_ref[...], preferred_element_type=F32)
    return h1 + gate * pe


def _project(h, nw_ref, win_ref, xbc_s, proj_s, rows):
    ms = jnp.mean(h * h, axis=-1, keepdims=True)
    u = (h * lax.rsqrt(ms + EPS) * nw_ref[...]).astype(BF16)
    xbc_s[SUBLANES:SUBLANES + rows, :] = jnp.dot(u, win_ref[:, :XBC_W], preferred_element_type=F32)
    proj_s[...] = jnp.dot(u, win_ref[:, XBC_W:], preferred_element_type=F32)


def _prompt_kernel(chunks_per_seq, n_aliased, *refs):
    (ha_ref, hc_ref, p_ref, bucket_ref, rel_ref, sink_ref, nw_ref, win_ref, cw_ref, cb_ref, dtb_ref,
     alog_ref, dsk_ref, snw_ref, wgk_ref, bgk_ref, gnw_ref, qnw_ref, knw_ref, wout_ref,
     wpe_ref, wpg_ref) = refs[:N_PROMPT_INPUTS]
    (y_ref, ssm_ref, conv_ref, gla_ref, ko_ref, vo_ref,
     proj_e, proj_o, xbc_e, xbc_o, mix_e, mix_o, u_s, h1_s, hist_s, st_s, s2_s, kext_s, vext_s,
     bias_s) = refs[N_PROMPT_INPUTS + n_aliased:]
    k_idx = pl.program_id(0)

    @pl.when(k_idx == 0)
    def _():
        accs = _build_bias(bucket_ref[...], rel_ref)
        own_block = _iota((BLK, 2 * BLK), 1) >= BLK
        for hh in range(SWA_HEADS):
            bias_s[hh] = accs[hh]
            bias_s[SWA_HEADS + hh] = jnp.where(own_block, accs[hh], NEG_INF)
        for ref in (proj_o, xbc_o, mix_e, mix_o, hist_s, st_s, s2_s, kext_s, vext_s):
            ref[...] = jnp.zeros(ref.shape, ref.dtype)

    row = _iota((BLK, BLK), 0)
    col = _iota((BLK, BLK), 1)
    causal = row >= col
    tri_ssd = jnp.where(causal, 1.0, 0.0).astype(BF16)
    tri_gla = jnp.where(causal & (_div(row, GLA_CHUNK) == _div(col, GLA_CHUNK)), 1.0, 0.0).astype(BF16)
    lo = col < HALF
    lane_row = _iota((1, LANES), 1)
    a_row = jnp.where(lane_row < SSD_HEADS, -jnp.exp(alog_ref[...]), 0.0)
    st_mask = (_iota((BLK, SSD_WIDTH), 0) < SSD_STATE) == (_iota((BLK, SSD_WIDTH), 1) < SSD_WIDTH // SSD_GROUPS)
    bd_mask = _div(_iota((LANES, GLA_WIDTH), 0), GLA_DK) == _div(_iota((LANES, GLA_WIDTH), 1), GLA_DV)
    att_t = _iota((GLA_CHUNK, GLA_WIDTH), 0)
    att_s = _mod(_iota((GLA_CHUNK, GLA_WIDTH), 1), GLA_CHUNK)
    att_mask = att_s <= att_t
    lo2 = _iota((2 * BLK, LANES), 1) < HALF

    def ssd_total(adt, acum):
        return jnp.broadcast_to(acum[BLK - 1:BLK, :], acum.shape)

    def gla_total(glog, bcs):
        return jnp.concatenate(
            [jnp.broadcast_to(bcs[(c2 + 1) * GLA_CHUNK - 1:(c2 + 1) * GLA_CHUNK, :], (GLA_CHUNK, LANES))
             for c2 in range(BLK // GLA_CHUNK)], axis=0)

    def block(blk, proj_s, xbc_s, mix_s, starts_sequence, tick):
        rows = slice(blk * BLK, (blk + 1) * BLK)
        cw = cw_ref[...]
        tick()
        if blk == 0:
            xwin = jnp.concatenate([hist_s[...], xbc_s[0:BLK, :]], axis=0)
        else:
            xwin = xbc_s[blk * BLK - SUBLANES:(blk + 1) * BLK, :]
        acc = xwin[SUBLANES - 3:SUBLANES - 3 + BLK, :] * cw[0:1, :]
        for k in range(1, SSD_CONV):
            acc = acc + xwin[SUBLANES - 3 + k:SUBLANES - 3 + k + BLK, :] * cw[k:k + 1, :]
        xbc_c = _silu(acc + cb_ref[...])
        tick()
        dtlr = proj_s[rows, P_DTLR:P_DTLR + LANES]
        y_intra, xs, bm, cm, xw, eacum, eacum_e = _ssd_intra(
            xbc_c, dtlr, dtb_ref[...], a_row, causal, tri_ssd, ssd_total, tick)
        tick()
        st = st_s[...]
        y = y_intra + _dot(cm, st) * eacum_e + dsk_ref[...] * xs
        st_s[...] = st * eacum_e[BLK - 1:BLK, :] + jnp.where(st_mask, _dot(bm.T, xw), 0.0)
        tick()
        y = y * _silu(proj_s[rows, P_Z:P_Z + SSD_WIDTH])
        mix_s[rows, 0:SSD_WIDTH] = _group_rmsnorm(y, snw_ref[...]).astype(BF16)
        tick()
        gk = proj_s[rows, P_GK:P_GK + LANES]
        gv = proj_s[rows, P_GV:P_GV + GLA_WIDTH]
        glog = _log_sigmoid(_dot(dtlr, wgk_ref[...]) + bgk_ref[...]) * (1.0 / GLA_GATE_NORM)
        o_intra, qe, kd, ebt = _gla_intra(
            proj_s[rows, P_GQ:P_GQ + LANES], gk, gv, glog, tri_gla, gla_total, [att_mask, att_mask], tick)
        kd_t = kd.T
        ebt_t = ebt.T
        s2 = s2_s[...]
        o_parts = []
        for c2 in range(BLK // GLA_CHUNK):
            tick()
            rs = slice(c2 * GLA_CHUNK, (c2 + 1) * GLA_CHUNK)
            o_parts.append(o_intra[rs] + _dot(qe[rs], s2))
            u2 = _dot(jnp.where(_div(col, GLA_CHUNK) == c2, kd_t, 0.0), gv)
            last = (c2 + 1) * GLA_CHUNK - 1
            s2 = s2 * ebt_t[:, last:last + 1] + jnp.where(bd_mask, u2, 0.0)
        s2_s[...] = s2
        o = jnp.concatenate(o_parts, axis=0)
        y_gla = o * _head_rms_scale(o) * gnw_ref[...] * _silu(proj_s[rows, P_GG:P_GG + GLA_WIDTH])
        mix_s[rows, SSD_WIDTH:SSD_WIDTH + GLA_WIDTH] = y_gla.astype(BF16)
        tick()
        sq = proj_s[rows, P_SQ:P_SQ + SWA_WIDTH]
        qn = sq * _head_rms_scale(sq) * qnw_ref[...] * (SWA_HEAD_DIM ** -0.5)
        sk = proj_s[rows, P_SK:P_SK + LANES]
        kn = sk * _head_rms_scale(sk) * knw_ref[...]
        vn = proj_s[rows, P_SV:P_SV + LANES]
        kext_s[BLK:2 * BLK, :] = kn
        vext_s[BLK:2 * BLK, :] = vn
        kext = kext_s[...]
        vext = vext_s[...]
        qa = qn[:, :LANES]
        qb = qn[:, LANES:]
        qs = jnp.concatenate([jnp.where(lo, qa, 0.0), jnp.where(lo, qb, 0.0),
                              jnp.where(lo, 0.0, qa), jnp.where(lo, 0.0, qb)], axis=0)
        logits = _dot_nt(qs, kext)
        tick()
        if blk == 0 and starts_sequence is not False:
            bias_row0 = jnp.where(starts_sequence, SWA_HEADS, 0)
        else:
            bias_row0 = 0
        es = []
        invs = []
        for hh in range(SWA_HEADS):
            tick()
            sink = sink_ref[hh]
            l = logits[hh * BLK:(hh + 1) * BLK] + bias_s[bias_row0 + hh]
            m = jnp.maximum(jnp.max(l, axis=-1, keepdims=True), sink)
            e = jnp.exp(l - m)
            den = jnp.sum(e, axis=-1, keepdims=True) + jnp.exp(sink - m)
            es.append(e.astype(BF16))
            invs.append(1.0 / den)
        v_stack = jnp.concatenate([jnp.where(lo2, vext, 0.0), jnp.where(lo2, 0.0, vext)], axis=0)
        tile_a = _dot(jnp.concatenate([es[0], es[2]], axis=1), v_stack) * jnp.where(lo, invs[0], invs[2])
        tile_b = _dot(jnp.concatenate([es[1], es[3]], axis=1), v_stack) * jnp.where(lo, invs[1], invs[3])
        oa = jnp.concatenate([tile_a, tile_b], axis=1)
        y_swa = oa * _silu(proj_s[rows, P_SG:P_SG + SWA_WIDTH])
        mix_s[rows, SSD_WIDTH + GLA_WIDTH:] = y_swa.astype(BF16)
        kext_s[0:BLK, :] = kn
        vext_s[0:BLK, :] = vn

    chunk = proj_e.shape[0]

    def project_items(rows, proj_s, xbc_s):
        def norm():
            h = ha_ref[rows, :]
            ms = jnp.mean(h * h, axis=-1, keepdims=True)
            u_s[...] = (h * lax.rsqrt(ms + EPS) * nw_ref[...]).astype(BF16)

        def cols(dst, lo_c, hi_c, w_off):
            def item():
                dst[:, lo_c:hi_c] = jnp.dot(u_s[...], win_ref[:, w_off + lo_c:w_off + hi_c],
                                            preferred_element_type=F32)
            return item

        step = 2 * LANES
        items = [norm]
        items += [cols(xbc_s, c, min(c + step, XBC_W), 0) for c in range(0, XBC_W, step)]
        items += [cols(proj_s, c, min(c + step, PROJ_W), XBC_W) for c in range(0, PROJ_W, step)]
        return items

    def epilogue_items(rows, mix_s):
        half_w = 2 * LANES

        def residual(c):
            def item():
                h1_s[:, c:c + half_w] = hc_ref[rows, c:c + half_w] + jnp.dot(
                    mix_s[...], wout_ref[:, c:c + half_w], preferred_element_type=F32)
            return item

        def gated(c):
            def item():
                gate = jax.nn.sigmoid(jnp.dot(h1_s[...].astype(BF16), wpg_ref[:, c:c + half_w],
                                              preferred_element_type=F32))
                pe = jnp.dot(p_ref[0, rows, :].astype(BF16), wpe_ref[:, c:c + half_w],
                             preferred_element_type=F32)
                y_ref[rows, c:c + half_w] = h1_s[:, c:c + half_w] + gate * pe
            return item

        col0 = range(0, D_MODEL, half_w)
        return [residual(c) for c in col0] + [gated(c) for c in col0]

    def merge(first, second):
        out = list(first)
        for j, item in enumerate(second):
            out.insert(((j + 1) * len(first)) // len(second) + j, item)
        return out

    def mixer(proj_s, xbc_s, mix_s, starts_sequence, items):
        if starts_sequence is not False:
            keep = jnp.where(starts_sequence, 0.0, 1.0)
            for ref in (hist_s, st_s, s2_s):
                ref[...] = ref[...] * keep
            kext_s[0:BLK, :] = kext_s[0:BLK, :] * keep
            vext_s[0:BLK, :] = vext_s[0:BLK, :] * keep
        queue = list(items)
        calls = [0]

        def tick():
            calls[0] += 1
            if queue and calls[0] % TICKS_PER_ITEM == 0:
                queue.pop(0)()

        for blk in range(chunk // BLK):
            block(blk, proj_s, xbc_s, mix_s, starts_sequence, tick)
        while queue:
            queue.pop(0)()
        hist_s[...] = xbc_s[chunk - SUBLANES:chunk, :]

    def write_states():
        st = st_s[...]
        stc = st[:SSD_STATE] + st[SSD_STATE:]
        ssm_ref[0, 0] = jnp.concatenate([stc, stc], axis=0).T[:, :SSD_STATE]
        conv_ref[0, 0] = hist_s[SUBLANES - (SSD_CONV - 1):SUBLANES, :]
        s2 = s2_s[...]
        w = s2[:, :LANES] + s2[:, LANES:]
        gla_ref[0, 0] = w[:, :GLA_DV] + w[:, GLA_DV:]
        ko_ref[0, 0] = kext_s[0:BLK, :].T
        vo_ref[0, 0] = vext_s[0:BLK, :].T

    even = slice(0, chunk)
    odd = slice(chunk, 2 * chunk)
    mixer(proj_o, xbc_o, mix_o, False, merge(project_items(even, proj_e, xbc_e), epilogue_items(even, mix_e)))
    write_states()
    @pl.when(k_idx >= 0)
    def _():
        mixer(proj_e, xbc_e, mix_e, _mod(2 * k_idx, chunks_per_seq) == 0,
              merge(project_items(odd, proj_o, xbc_o), epilogue_items(odd, mix_o)))


def _sample_kernel(seq, n_seq,
                   h_ref, p_ref, ssm_ref, cst_ref, gst_ref, kc_ref, vc_ref, bucket_ref, rel_ref, sink_ref,
                   nw_ref, win_ref, cw_ref, cb_ref, dtb_ref, alog_ref, dsk_ref, snw_ref, wgk_ref, bgk_ref,
                   gnw_ref, qnw_ref, knw_ref, wout_ref, wpe_ref, wpg_ref,
                   y_ref, ssm_o, xbc_o, gla_o, ko_ref, vo_ref,
                   proj_s, xbc_s, cs_s, cm_s, bm_s, xw_s, ead_s, yint_s, qe_s, kd_s, ebt_s, oint_s,
                   qn_s, kn_s, vn_s, oswa_s, kk_s, vv_s, bias_s):
    rows = n_seq * seq
    n_keys = WINDOW + seq

    @pl.when(pl.program_id(0) == 0)
    def _():
        bucket = bucket_ref[...]
        accs = _build_bias(bucket, rel_ref)
        head_of_lane = _div(_iota(bucket.shape, 1), seq)
        out = accs[SWA_HEADS - 1]
        for hh in range(SWA_HEADS - 2, -1, -1):
            out = jnp.where(head_of_lane == hh, accs[hh], out)
        bias_s[...] = out

    xbc_s[0:SUBLANES, :] = jnp.zeros((SUBLANES, XBC_W), F32)
    _project(h_ref[...], nw_ref, win_ref, xbc_s, proj_s, rows)
    cs_s[0:rows, :] = cst_ref[...]
    cs_s[rows:rows + SUBLANES, :] = jnp.zeros((SUBLANES, XBC_W), F32)
    xbc_o[...] = xbc_s[SUBLANES:SUBLANES + rows, :]

    row = _iota((BLK, BLK), 0)
    col = _iota((BLK, BLK), 1)
    same_seq = _div(row, seq) == _div(col, seq)
    pair_mask = same_seq & (row >= col)
    tri = jnp.where(pair_mask, 1.0, 0.0).astype(BF16)
    ones_seq = jnp.where(same_seq, 1.0, 0.0).astype(BF16)
    lo = col < HALF
    lane_row = _iota((1, LANES), 1)
    a_row = jnp.where(lane_row < SSD_HEADS, -jnp.exp(alog_ref[...]), 0.0)
    bd_mask = _div(_iota((LANES, GLA_WIDTH), 0), GLA_DK) == _div(_iota((LANES, GLA_WIDTH), 1), GLA_DV)
    eye = row == col

    def total_of(x, _cum):
        return _dot_exact(ones_seq, x)

    cw = cw_ref[...]
    t_of_row = _mod(_iota((rows, XBC_W), 0), seq)
    taps = []
    for j in range(SSD_CONV - 1, 0, -1):
        cur = xbc_s[pl.ds(SUBLANES - j, rows), :]
        old = cs_s[pl.ds(SUBLANES - j, rows), :]
        taps.append(jnp.where(t_of_row >= j, cur, old))
    taps.append(xbc_s[pl.ds(SUBLANES, rows), :])
    acc = taps[0] * cw[0:1, :]
    for k in range(1, SSD_CONV):
        acc = acc + taps[k] * cw[k:k + 1, :]
    xbc_c = _silu(acc + cb_ref[...])
    dtlr = proj_s[:, P_DTLR:P_DTLR + LANES]
    y_intra, xs, bm, cm, xw, eacum, eacum_e = _ssd_intra(
        xbc_c, dtlr, dtb_ref[...], a_row, pair_mask, tri, total_of)
    cm_s[...] = cm
    bm_s[...] = bm
    xw_s[...] = xw
    ead_s[...] = eacum

    gk = proj_s[:, P_GK:P_GK + LANES]
    gv = proj_s[:, P_GV:P_GV + GLA_WIDTH]
    glog = _log_sigmoid(_dot(dtlr, wgk_ref[...]) + bgk_ref[...]) * (1.0 / GLA_GATE_NORM)
    att_masks = []
    for c2 in range(BLK // GLA_CHUNK):
        t_loc = _iota((GLA_CHUNK, GLA_WIDTH), 0)
        s_loc = _mod(_iota((GLA_CHUNK, GLA_WIDTH), 1), GLA_CHUNK)
        att_masks.append((_div(t_loc, seq) == _div(s_loc, seq)) & (s_loc <= t_loc))
    o_intra, qe, kd, ebt = _gla_intra(
        proj_s[:, P_GQ:P_GQ + LANES], gk, gv, glog, tri, total_of, att_masks)
    qe_s[...] = qe
    kd_s[...] = kd
    ebt_s[...] = ebt

    sq = proj_s[:, P_SQ:P_SQ + SWA_WIDTH]
    qn_s[...] = sq * _head_rms_scale(sq) * qnw_ref[...]
    sk = proj_s[:, P_SK:P_SK + LANES]
    kn_s[...] = sk * _head_rms_scale(sk) * knw_ref[...]
    vn_s[...] = proj_s[:, P_SV:P_SV + LANES]

    lane32 = _div(_iota((1, SWA_HEADS * seq), 1), seq)
    sink_row = jnp.full((1, SWA_HEADS * seq), sink_ref[SWA_HEADS - 1], F32)
    for hh in range(SWA_HEADS - 2, -1, -1):
        sink_row = jnp.where(lane32 == hh, sink_ref[hh], sink_row)
    lo8 = _iota((seq, LANES), 1) < HALF
    heads_per_group = SSD_HEADS // SSD_GROUPS
    gw = SSD_WIDTH // SSD_GROUPS

    def per_seq(b, carry):
        r0 = pl.multiple_of(b * seq, seq)
        rb = pl.ds(r0, seq)
        s_prev = ssm_ref[b]
        cmb = cm_s[rb, :]
        bmb = bm_s[rb, :]
        xwb = xw_s[rb, :]
        yint_s[rb, :] = jnp.concatenate(
            [_dot_nt(cmb[:, g * SSD_STATE:(g + 1) * SSD_STATE], s_prev[g * gw:(g + 1) * gw, :])
             for g in range(SSD_GROUPS)], axis=1)
        upd = [_dot_tn(xwb[:, g * gw:(g + 1) * gw], bmb[:, g * SSD_STATE:(g + 1) * SSD_STATE])
               for g in range(SSD_GROUPS)]
        drow = ead_s[rb, :][seq - 1:seq, :]
        for hh in range(SSD_HEADS):
            g, r = divmod(hh, heads_per_group)
            hs = slice(hh * SSD_HEAD_DIM, (hh + 1) * SSD_HEAD_DIM)
            ssm_o[b, hs, :] = (s_prev[hs, :] * drow[:, hh:hh + 1]
                               + upd[g][r * SSD_HEAD_DIM:(r + 1) * SSD_HEAD_DIM, :])
        g_prev = gst_ref[b]
        g2 = jnp.concatenate([g_prev, g_prev], axis=1)
        s_bd = jnp.where(bd_mask, jnp.concatenate([g2, g2], axis=1), 0.0)
        oint_s[rb, :] = _dot(qe_s[rb, :], s_bd)
        u2 = jnp.where(bd_mask, _dot_tn(kd_s[rb, :], proj_s[rb, P_GV:P_GV + GLA_WIDTH]), 0.0)
        w = u2[:, :LANES] + u2[:, LANES:]
        erow = jnp.broadcast_to(ebt_s[rb, :][seq - 1:seq, :], (LANES, LANES))
        ecol = jnp.sum(jnp.where(eye, erow, 0.0), axis=-1, keepdims=True)
        gla_o[b] = g_prev * ecol + (w[:, :GLA_DV] + w[:, GLA_DV:])
        knb = kn_s[rb, :]
        vnb = vn_s[rb, :]
        kk_s[0:WINDOW, :] = kc_ref[b]
        kk_s[WINDOW:n_keys, :] = knb
        vv_s[0:WINDOW, :] = vc_ref[b]
        vv_s[WINDOW:n_keys, :] = vnb
        qa = qn_s[rb, 0:LANES]
        qb = qn_s[rb, LANES:2 * LANES]
        qs = jnp.concatenate([jnp.where(lo8, qa, 0.0), jnp.where(lo8, qb, 0.0),
                              jnp.where(lo8, 0.0, qa), jnp.where(lo8, 0.0, qb)], axis=0)
        l = _dot_nt(kk_s[...], qs) * (SWA_HEAD_DIM ** -0.5) + bias_s[...]
        m = jnp.maximum(jnp.max(l, axis=0, keepdims=True), sink_row)
        e = jnp.exp(l - m)
        den = jnp.sum(e, axis=0, keepdims=True) + jnp.exp(sink_row - m)
        o = _dot_tn(e / den, vv_s[...])
        tile_a = jnp.where(lo8, o[0:seq], o[2 * seq:3 * seq])
        tile_b = jnp.where(lo8, o[seq:2 * seq], o[3 * seq:4 * seq])
        oswa_s[rb, :] = jnp.concatenate([tile_a, tile_b], axis=1)
        ko_ref[b, 0:WINDOW - seq, :] = kc_ref[b, seq:WINDOW, :]
        ko_ref[b, WINDOW - seq:WINDOW, :] = knb
        vo_ref[b, 0:WINDOW - seq, :] = vc_ref[b, seq:WINDOW, :]
        vo_ref[b, WINDOW - seq:WINDOW, :] = vnb
        return carry

    lax.fori_loop(0, n_seq, per_seq, 0)

    y = y_intra + yint_s[...] * eacum_e + dsk_ref[...] * xs
    y = y * _silu(proj_s[:, P_Z:P_Z + SSD_WIDTH])
    y_ssd = _group_rmsnorm(y, snw_ref[...])
    o = o_intra + oint_s[...]
    y_gla = o * _head_rms_scale(o) * gnw_ref[...] * _silu(proj_s[:, P_GG:P_GG + GLA_WIDTH])
    y_swa = oswa_s[...] * _silu(proj_s[:, P_SG:P_SG + SWA_WIDTH])
    mix = jnp.concatenate([y_ssd, y_gla, y_swa], axis=1).astype(BF16)
    y_ref[...] = _epilogue(h_ref[...], mix, p_ref[...], wout_ref, wpg_ref, wpe_ref)


N_FRONT_INPUTS = 12


def _sample_front_kernel(seq, n_aliased, *refs):
    (h_ref, cst_ref, nw_ref, win_ref, cw_ref, cb_ref, dtb_ref, alog_ref, wgk_ref, bgk_ref,
     qnw_ref, knw_ref) = refs[:N_FRONT_INPUTS]
    (xs_ref, gates_ref, xt_ref, bt_ref, ct_ref, at_ref, qt_ref, kt_ref, egt_ref, vt_ref,
     qn_ref, kn_ref, vn_ref, conv_ref, u_s, xbc_s, proj_s) = refs[N_FRONT_INPUTS + n_aliased:]
    n_seq = BLK
    for t in range(seq):
        rows = slice(t * n_seq, (t + 1) * n_seq)
        ht = h_ref[rows, :]
        ms = jnp.mean(ht * ht, axis=-1, keepdims=True)
        u_s[rows, :] = (ht * lax.rsqrt(ms + EPS) * nw_ref[...]).astype(BF16)
    xbc_s[...] = jnp.dot(u_s[...], win_ref[:, :XBC_W], preferred_element_type=F32)
    proj_s[...] = jnp.dot(u_s[...], win_ref[:, XBC_W:], preferred_element_type=F32)
    cw = cw_ref[...]
    a_row = jnp.where(_iota((1, LANES), 1) < SSD_HEADS, -jnp.exp(alog_ref[...]), 0.0)
    for t in range(seq):
        rows = slice(t * n_seq, (t + 1) * n_seq)

        def raw_xbc(back):
            if t >= back:
                return xbc_s[(t - back) * n_seq:(t - back + 1) * n_seq, :]
            return cst_ref[0, SSD_CONV - 1 + t - back]

        acc = raw_xbc(SSD_CONV - 1) * cw[0:1, :]
        for k in range(1, SSD_CONV):
            acc = acc + raw_xbc(SSD_CONV - 1 - k) * cw[k:k + 1, :]
        xbc_c = _silu(acc + cb_ref[...])
        xs = xbc_c[:, :SSD_WIDTH]
        dtlr = proj_s[rows, P_DTLR:P_DTLR + LANES]
        dtv = _softplus(dtlr + dtb_ref[...])
        xs_ref[rows, :] = xs
        xt_ref[t] = (xs * _expand_heads(dtv, SSD_HEADS)).T
        bt_ref[t] = xbc_c[:, SSD_WIDTH:SSD_WIDTH + LANES].T
        ct_ref[t] = xbc_c[:, SSD_WIDTH + LANES:].T
        at_ref[t] = jnp.exp(dtv * a_row).T[:SSD_HEADS, :]
        glog = _log_sigmoid(_dot(dtlr, wgk_ref[...]) + bgk_ref[...]) * (1.0 / GLA_GATE_NORM)
        qt_ref[t] = (proj_s[rows, P_GQ:P_GQ + LANES] * (GLA_DK ** -0.5)).T
        kt_ref[t] = proj_s[rows, P_GK:P_GK + LANES].T
        egt_ref[t] = jnp.exp(glog).T
        vt_ref[t] = proj_s[rows, P_GV:P_GV + GLA_WIDTH].T
        sq = proj_s[rows, P_SQ:P_SQ + SWA_WIDTH]
        qn_ref[rows, :] = sq * _head_rms_scale(sq) * qnw_ref[...] * (SWA_HEAD_DIM ** -0.5)
        sk = proj_s[rows, P_SK:P_SK + LANES]
        kn_ref[rows, :] = sk * _head_rms_scale(sk) * knw_ref[...]
        vn_ref[rows, :] = proj_s[rows, P_SV:P_SV + LANES]
        gates_ref[rows, :] = jnp.concatenate(
            [_silu(proj_s[rows, P_Z:P_Z + SSD_WIDTH]), _silu(proj_s[rows, P_GG:P_GG + GLA_WIDTH]),
             _silu(proj_s[rows, P_SG:P_SG + SWA_WIDTH])], axis=1)
        if t >= seq - (SSD_CONV - 1):
            conv_ref[0, t - (seq - (SSD_CONV - 1))] = xbc_s[rows, :]


N_STATE_INPUTS = 19
SEQ_PER_STEP = 16
SWA_UNROLL = 4


def _sample_state_kernel(seq, n_aliased, *refs):
    (xt_ref, bt_ref, ct_ref, at_ref, qt_ref, kt_ref, egt_ref, vt_ref, qn_ref, kn_ref, vn_ref,
     ssm_ref, gla_ref, kc_ref, vc_ref, bucket_c_ref, bucket_n_ref, rel_ref, sink_ref) = refs[:N_STATE_INPUTS]
    (ssm_o, gla_o, ko_ref, vo_ref, yt_ref, ot_ref, oswa_ref,
     qa_s, qb_s, krow_s, vrow_s, oa_s, ob_s, biasc_s, biasn_s) = refs[N_STATE_INPUTS + n_aliased:]
    j = pl.program_id(0)
    n_seq = LANES
    head_of_row = _div(_iota((SWA_HEADS * seq, LANES), 0), seq)

    def by_head(values):
        out = values[SWA_HEADS - 1]
        for hh in range(SWA_HEADS - 2, -1, -1):
            out = jnp.where(head_of_row == hh, values[hh], out)
        return out

    @pl.when(j == 0)
    def _():
        biasc_s[...] = by_head(_build_bias(bucket_c_ref[...], rel_ref))
        biasn_s[...] = by_head(_build_bias(bucket_n_ref[...], rel_ref))

    sub = _iota((SUBLANES, LANES), 0)
    a_rows = [jnp.sum(jnp.where(sub == j, at_ref[t], 0.0), axis=0, keepdims=True) for t in range(seq)]

    def ssd_body(p8, carry):
        r8 = pl.multiple_of(p8 * SUBLANES, SUBLANES)
        x_tiles = [xt_ref[t, pl.ds(r8, SUBLANES), :] for t in range(seq)]
        y_rows = [[] for _ in range(seq)]
        for pp in range(SUBLANES):
            r64 = pl.multiple_of((p8 * SUBLANES + pp) * SSD_STATE, SSD_STATE)
            slab = ssm_ref[0, 0, pl.ds(r64, SSD_STATE), :]
            for t in range(seq):
                slab = slab * a_rows[t] + x_tiles[t][pp:pp + 1, :] * bt_ref[t]
                y_rows[t].append(jnp.sum(ct_ref[t] * slab, axis=0, keepdims=True))
            ssm_o[0, 0, pl.ds(r64, SSD_STATE), :] = slab
        for t in range(seq):
            yt_ref[t, pl.ds(r8, SUBLANES), :] = jnp.concatenate(y_rows[t], axis=0)
        return carry

    lax.fori_loop(0, SSD_HEAD_DIM // SUBLANES, ssd_body, 0)

    @pl.when(j < GLA_HEADS)
    def _():
        for t in range(seq):
            ot_ref[t] = jnp.zeros((GLA_DV, LANES), F32)

        def gla_body(d8, carry):
            r8 = pl.multiple_of(d8 * SUBLANES, SUBLANES)
            q_tiles = [qt_ref[t, pl.ds(r8, SUBLANES), :] for t in range(seq)]
            k_tiles = [kt_ref[t, pl.ds(r8, SUBLANES), :] for t in range(seq)]
            g_tiles = [egt_ref[t, pl.ds(r8, SUBLANES), :] for t in range(seq)]
            for dd in range(SUBLANES):
                r64 = pl.multiple_of((d8 * SUBLANES + dd) * GLA_DV, GLA_DV)
                slab = gla_ref[0, 0, pl.ds(r64, GLA_DV), :]
                for t in range(seq):
                    slab = slab * g_tiles[t][dd:dd + 1, :] + k_tiles[t][dd:dd + 1, :] * vt_ref[t]
                    ot_ref[t] = ot_ref[t] + q_tiles[t][dd:dd + 1, :] * slab
                gla_o[0, 0, pl.ds(r64, GLA_DV), :] = slab
            return carry

        lax.fori_loop(0, GLA_DK // SUBLANES, gla_body, 0)

    base = pl.multiple_of(j * SEQ_PER_STEP, SEQ_PER_STEP)
    for t in range(seq):
        src = pl.ds(t * n_seq + base, SEQ_PER_STEP)
        dst = pl.ds(t, SEQ_PER_STEP, stride=seq)
        qa_s[dst, :] = qn_ref[src, 0:LANES]
        qb_s[dst, :] = qn_ref[src, LANES:2 * LANES]
        krow_s[dst, :] = kn_ref[src, :]
        vrow_s[dst, :] = vn_ref[src, :]
    kn_t = krow_s[...].T
    vn_t = vrow_s[...].T
    keep_old = _iota((LANES, WINDOW), 1) < WINDOW - seq
    lo8 = _iota((seq, LANES), 1) < HALF
    sink_col = by_head([jnp.full((SWA_HEADS * seq, LANES), sink_ref[hh], F32) for hh in range(SWA_HEADS)])[:, 0:1]

    def swa_body(bl, carry):
        r8 = pl.multiple_of(bl * seq, seq)
        kn_b = krow_s[pl.ds(r8, seq), :]
        vn_b = vrow_s[pl.ds(r8, seq), :]
        qa = qa_s[pl.ds(r8, seq), :]
        qb = qb_s[pl.ds(r8, seq), :]
        qs =jnp.concatenate([jnp.where(lo8, qa, 0.0), jnp.where(lo8, qb, 0.0),
                              jnp.where(lo8, 0.0, qa), jnp.where(lo8, 0.0, qb)], axis=0)
        k_old = kc_ref[0, bl]
        v_old = vc_ref[0, bl]
        lc = _dot(qs, k_old) + biasc_s[...]
        ln = _dot_nt(qs, kn_b) + biasn_s[:, 0:seq]
        m = jnp.maximum(jnp.maximum(jnp.max(lc, axis=-1, keepdims=True), jnp.max(ln, axis=-1, keepdims=True)),
                        sink_col)
        ec = jnp.exp(lc - m)
        en = jnp.exp(ln - m)
        den = jnp.sum(ec, axis=-1, keepdims=True) + jnp.sum(en, axis=-1, keepdims=True) + jnp.exp(sink_col - m)
        o = (_dot_nt(ec, v_old) + _dot(en, vn_b)) * (1.0 / den)
        oa_s[pl.ds(r8, seq), :] = jnp.where(lo8, o[0:seq], o[2 * seq:3 * seq])
        ob_s[pl.ds(r8, seq), :] = jnp.where(lo8, o[seq:2 * seq], o[3 * seq:4 * seq])
        ko_ref[0, bl] = jnp.where(keep_old, pltpu.roll(k_old, WINDOW - seq, axis=1),
                                  pltpu.roll(kn_t, WINDOW - seq - r8, axis=1))
        vo_ref[0, bl] = jnp.where(keep_old, pltpu.roll(v_old, WINDOW - seq, axis=1),
                                  pltpu.roll(vn_t, WINDOW - seq - r8, axis=1))
        return carry

    lax.fori_loop(0, SEQ_PER_STEP, swa_body, 0, unroll=SWA_UNROLL)
    for t in range(seq):
        src = pl.ds(t, SEQ_PER_STEP, stride=seq)
        oswa_ref[t] = jnp.concatenate([oa_s[src, :], ob_s[src, :]], axis=1)


def _sample_back_kernel(seq, yt_ref, ot_ref, oswa_ref, xs_ref, gates_ref, h_ref, p_ref, dsk_ref, snw_ref, gnw_ref,
                        wout_ref, wpe_ref, wpg_ref, y_ref, mix_s):
    n_seq = BLK
    for t in range(seq):
        rows = slice(t * n_seq, (t + 1) * n_seq)
        y = (yt_ref[t].T + dsk_ref[...] * xs_ref[rows, :]) * gates_ref[rows, 0:SSD_WIDTH]
        mix_s[rows, 0:SSD_WIDTH] = _group_rmsnorm(y, snw_ref[...]).astype(BF16)
        o = ot_ref[t].T
        y_gla = o * _head_rms_scale(o) * gnw_ref[...] * gates_ref[rows, SSD_WIDTH:SSD_WIDTH + GLA_WIDTH]
        mix_s[rows, SSD_WIDTH:SSD_WIDTH + GLA_WIDTH] = y_gla.astype(BF16)
        mix_s[rows, SSD_WIDTH + GLA_WIDTH:] = (oswa_ref[t] * gates_ref[rows, SSD_WIDTH + GLA_WIDTH:]).astype(BF16)
    y_ref[...] = _epilogue(h_ref[...], mix_s[...], p_ref[0], wout_ref, wpg_ref, wpe_ref)


def _const_spec(shape):
    nd = len(shape)
    return pl.BlockSpec(shape, lambda *_: (0,) * nd)


def _resident_spec(shape):
    nd = len(shape)
    return pl.BlockSpec(shape, lambda *_: (0,) * nd, pipeline_mode=pl.Buffered(1))


def _smem_spec():
    return pl.BlockSpec(memory_space=pltpu.SMEM)


def _layer_weights(i, w):
    ops = [w["norm_w"][i], w["w_in"][i], w["conv_w"][i], w["conv_b"][i], w["dt_bias"][i], w["a_log"][i],
           w["d_skip"][i], w["ssd_norm_w"][i], w["gla_w_gk"][i], w["gla_b_gk"][i], w["gla_norm_w"][i],
           w["q_norm_w"][i], w["k_norm_w"][i], w["w_out"][i], w["w_pe"][i], w["w_pg"][i]]
    return ops, [_resident_spec(o.shape) if o.dtype == BF16 else _const_spec(o.shape) for o in ops]


def _prompt_layer(layer, depth, bsz, h, p_all, prev_states, bucket, rel, sinks, wops, wspecs):
    rows_total, _ = h.shape
    seq_len = rows_total // bsz
    chunk = PROMPT_CHUNK_ROWS
    pair = 2 * chunk
    chunks_per_seq = seq_len // chunk
    assert seq_len % pair == 0 and chunks_per_seq & (chunks_per_seq - 1) == 0
    n_pairs = rows_total // pair
    kern = functools.partial(_prompt_kernel, chunks_per_seq, len(prev_states))
    proj_rows = pl.BlockSpec((pair, D_MODEL), lambda k: (jnp.minimum(k, n_pairs - 1), 0))
    out_rows = pl.BlockSpec((pair, D_MODEL), lambda k: (jnp.maximum(k - 1, 0), 0))
    p_spec = pl.BlockSpec((1, pair, PLE_DIM), lambda k: (layer, jnp.maximum(k - 1, 0), 0))
    per_seq = lambda s: pl.BlockSpec(
        (1, 1) + s, lambda k: (layer, jnp.maximum(2 * k - 1, 0) // chunks_per_seq) + (0,) * len(s))
    state_shapes = ((SSD_WIDTH, SSD_STATE), (SSD_CONV - 1, SSD_CONV_DIM), (GLA_HEADS * GLA_DK, GLA_DV),
                    (LANES, WINDOW), (LANES, WINDOW))
    out_shape = (jax.ShapeDtypeStruct((rows_total, D_MODEL), F32),) + tuple(
        jax.ShapeDtypeStruct((depth, bsz) + s, F32) for s in state_shapes)
    return pl.pallas_call(
        kern,
        grid=(n_pairs + 1,),
        in_specs=[proj_rows, out_rows, p_spec, _const_spec(bucket.shape), _smem_spec(), _smem_spec()]
        + wspecs + [pl.BlockSpec(memory_space=pl.ANY)] * len(prev_states),
        out_specs=(out_rows,) + tuple(per_seq(s) for s in state_shapes),
        out_shape=out_shape,
        input_output_aliases={N_PROMPT_INPUTS + k: 1 + k for k in range(len(prev_states))},
        scratch_shapes=[
            pltpu.VMEM((chunk, PROJ_W), F32), pltpu.VMEM((chunk, PROJ_W), F32),
            pltpu.VMEM((chunk, XBC_W), F32), pltpu.VMEM((chunk, XBC_W), F32),
            pltpu.VMEM((chunk, D_MODEL), BF16), pltpu.VMEM((chunk, D_MODEL), BF16),
            pltpu.VMEM((chunk, D_MODEL), BF16),
            pltpu.VMEM((chunk, D_MODEL), F32),
            pltpu.VMEM((SUBLANES, XBC_W), F32),
            pltpu.VMEM((BLK, SSD_WIDTH), F32),
            pltpu.VMEM((LANES, GLA_WIDTH), F32),
            pltpu.VMEM((2 * BLK, LANES), F32),
            pltpu.VMEM((2 * BLK, LANES), F32),
            pltpu.VMEM((2 * SWA_HEADS, BLK, 2 * BLK), F32),
        ],
        compiler_params=pltpu.CompilerParams(
            dimension_semantics=("arbitrary",), vmem_limit_bytes=VMEM_LIMIT_BYTES),
        name="prompt_layer",
    )(h, h, p_all, bucket, rel, sinks, *wops, *prev_states)


def _sample_layer(h, p, ssm, conv_pad, gla, kc, vc, bucket, rel, sinks, wops, wspecs, seq):
    rows_total = h.shape[0]
    n_seq = BLK // seq
    rows = n_seq * seq
    bsz = rows_total // seq
    kern = functools.partial(_sample_kernel, seq, n_seq)
    row_spec = lambda w: pl.BlockSpec((rows, w), lambda s: (s, 0))
    seq_spec = lambda a, c: pl.BlockSpec((n_seq, a, c), lambda s: (s, 0, 0))
    out_shape = (
        jax.ShapeDtypeStruct((rows_total, D_MODEL), F32),
        jax.ShapeDtypeStruct((bsz, SSD_WIDTH, SSD_STATE), F32),
        jax.ShapeDtypeStruct((rows_total, SSD_CONV_DIM), F32),
        jax.ShapeDtypeStruct((bsz, GLA_HEADS * GLA_DK, GLA_DV), F32),
        jax.ShapeDtypeStruct((bsz, WINDOW, LANES), F32),
        jax.ShapeDtypeStruct((bsz, WINDOW, LANES), F32),
    )
    n_keys = WINDOW + seq
    f32_scratch = lambda r, c: pltpu.VMEM((r, c), F32)
    return pl.pallas_call(
        kern,
        grid=(rows_total // rows,),
        in_specs=[row_spec(D_MODEL), row_spec(PLE_DIM), seq_spec(SSD_WIDTH, SSD_STATE), row_spec(SSD_CONV_DIM),
                  seq_spec(GLA_HEADS * GLA_DK, GLA_DV), seq_spec(WINDOW, LANES), seq_spec(WINDOW, LANES),
                  _const_spec(bucket.shape), _smem_spec(), _smem_spec()] + wspecs,
        out_specs=(row_spec(D_MODEL), seq_spec(SSD_WIDTH, SSD_STATE), row_spec(SSD_CONV_DIM),
                   seq_spec(GLA_HEADS * GLA_DK, GLA_DV), seq_spec(WINDOW, LANES), seq_spec(WINDOW, LANES)),
        out_shape=out_shape,
        scratch_shapes=[
            f32_scratch(rows, PROJ_W),
            f32_scratch(rows + SUBLANES, XBC_W),
            f32_scratch(rows + SUBLANES, XBC_W),
            f32_scratch(rows, LANES), f32_scratch(rows, LANES), f32_scratch(rows, SSD_WIDTH),
            f32_scratch(rows, LANES), f32_scratch(rows, SSD_WIDTH),
            f32_scratch(rows, LANES), f32_scratch(rows, LANES), f32_scratch(rows, LANES),
            f32_scratch(rows, GLA_WIDTH),
            f32_scratch(rows, SWA_WIDTH), f32_scratch(rows, LANES), f32_scratch(rows, LANES),
            f32_scratch(rows, SWA_WIDTH),
            f32_scratch(n_keys, LANES), f32_scratch(n_keys, LANES),
            f32_scratch(n_keys, SWA_HEADS * seq),
        ],
        compiler_params=pltpu.CompilerParams(
            dimension_semantics=("arbitrary",), vmem_limit_bytes=VMEM_LIMIT_BYTES),
        name="sample_layer",
    )(h, p, ssm, conv_pad, gla, kc, vc, bucket, rel, sinks, *wops)


def _whole(shape, layer=None):
    if layer is None:
        return pl.BlockSpec(shape, lambda *_: (0,) * len(shape), pipeline_mode=pl.Buffered(1))
    return pl.BlockSpec((1,) + shape[1:], lambda *_: (layer,) + (0,) * (len(shape) - 1),
                        pipeline_mode=pl.Buffered(1))


def _sample_layer_native(layer, depth, seq, h, p_all, conv_in, ssm_in, gla_in, kc_in, vc_in, prev_states,
                         buckets, rel, sinks, wops):
    (nw, win, cw, cb, dtb, alog, dsk, snw, wgk, bgk, gnw, qnw, knw, wout, wpe, wpg) = wops
    rows = h.shape[0]
    n_seq = rows // seq
    assert n_seq == LANES and n_seq % SEQ_PER_STEP == 0 and SSD_HEADS * SEQ_PER_STEP == n_seq
    prev_conv, prev_rest = (prev_states[:1], prev_states[1:]) if prev_states else ((), ())
    f32 = lambda *s: jax.ShapeDtypeStruct(s, F32)
    cparams = lambda sem: pltpu.CompilerParams(dimension_semantics=sem, vmem_limit_bytes=VMEM_LIMIT_BYTES)

    front_in = [h, conv_in, nw, win, cw, cb, dtb, alog, wgk, bgk, qnw, knw]
    front_specs = [_whole(h.shape), _whole(conv_in.shape, layer)] + [_whole(a.shape) for a in front_in[2:]]
    front_out = (f32(rows, SSD_WIDTH), f32(rows, D_MODEL),
                 f32(seq, SSD_WIDTH, n_seq), f32(seq, LANES, n_seq), f32(seq, LANES, n_seq),
                 f32(seq, SSD_HEADS, n_seq), f32(seq, LANES, n_seq), f32(seq, LANES, n_seq), f32(seq, LANES, n_seq),
                 f32(seq, GLA_WIDTH, n_seq), f32(rows, SWA_WIDTH), f32(rows, LANES), f32(rows, LANES),
                 f32(*conv_in.shape))
    (xs, gates, xt, bt, ct, at, qt, kt, egt, vt, qn, kn, vn, conv_o) = pl.pallas_call(
        functools.partial(_sample_front_kernel, seq, len(prev_conv)),
        grid=(1,),
        in_specs=front_specs + [pl.BlockSpec(memory_space=pl.ANY)] * len(prev_conv),
        out_specs=tuple(_whole(o.shape) for o in front_out[:-1]) + (_whole(conv_in.shape, layer),),
        out_shape=front_out,
        input_output_aliases={len(front_in) + k: len(front_out) - 1 + k for k in range(len(prev_conv))},
        scratch_shapes=[pltpu.VMEM((rows, D_MODEL), BF16), pltpu.VMEM((rows, XBC_W), F32),
                        pltpu.VMEM((rows, PROJ_W), F32)],
        compiler_params=cparams(("arbitrary",)),
        name="sample_front",
    )(*front_in, *prev_conv)

    n_steps = SSD_HEADS
    per_group = SSD_HEADS // SSD_GROUPS
    gla_head = lambda j: jnp.minimum(j, GLA_HEADS - 1)
    blk3 = lambda n, f: pl.BlockSpec((seq, n, n_seq), lambda j: (0, f(j), 0))
    state_in = [xt, bt, ct, at, qt, kt, egt, vt, qn, kn, vn, ssm_in, gla_in, kc_in, vc_in, buckets[0], buckets[1],
                rel, sinks]
    ssm_spec = pl.BlockSpec((1, 1) + ssm_in.shape[2:], lambda j: (layer, j, 0, 0))
    gla_spec = pl.BlockSpec((1, 1) + gla_in.shape[2:], lambda j: (layer, gla_head(j), 0, 0))
    kv_spec = pl.BlockSpec((1, SEQ_PER_STEP) + kc_in.shape[2:], lambda j: (layer, j, 0, 0))
    state_specs = [blk3(SSD_HEAD_DIM, lambda j: j), blk3(SSD_STATE, lambda j: j // per_group),
                   blk3(SSD_STATE, lambda j: j // per_group), _const_spec(at.shape),
                   blk3(GLA_DK, gla_head), blk3(GLA_DK, gla_head), blk3(GLA_DK, gla_head), blk3(GLA_DV, gla_head),
                   _const_spec(qn.shape), _const_spec(kn.shape), _const_spec(vn.shape),
                   ssm_spec, gla_spec, kv_spec, kv_spec,
                   _const_spec(buckets[0].shape), _const_spec(buckets[1].shape), _smem_spec(), _smem_spec()]
    state_out = (f32(*ssm_in.shape), f32(*gla_in.shape), f32(*kc_in.shape), f32(*vc_in.shape),
                 f32(seq, SSD_WIDTH, n_seq), f32(seq, GLA_WIDTH, n_seq), f32(seq, n_seq, SWA_WIDTH))
    ssm_o, gla_o, ko, vo, yt, ot, oswa = pl.pallas_call(
        functools.partial(_sample_state_kernel, seq, len(prev_rest)),
        grid=(n_steps,),
        in_specs=state_specs + [pl.BlockSpec(memory_space=pl.ANY)] * len(prev_rest),
        out_specs=(ssm_spec, gla_spec, kv_spec, kv_spec, blk3(SSD_HEAD_DIM, lambda j: j), blk3(GLA_DV, gla_head),
                   pl.BlockSpec((seq, SEQ_PER_STEP, SWA_WIDTH), lambda j: (0, j, 0))),
        out_shape=state_out,
        input_output_aliases={len(state_in) + k: k for k in range(len(prev_rest))},
        scratch_shapes=[pltpu.VMEM((SEQ_PER_STEP * seq, LANES), F32)] * 6
        + [pltpu.VMEM((SWA_HEADS * seq, LANES), F32)] * 2,
        compiler_params=cparams(("arbitrary",)),
        name="sample_state",
    )(*state_in, *prev_rest)

    back_in = [yt, ot, oswa, xs, gates, h, p_all, dsk, snw, gnw, wout, wpe, wpg]
    back_specs = [_whole(a.shape) for a in back_in[:6]] + [_whole(p_all.shape, layer)] + [
        _whole(a.shape) for a in back_in[7:]]
    y = pl.pallas_call(
        functools.partial(_sample_back_kernel, seq),
        grid=(1,),
        in_specs=back_specs,
        out_specs=_whole((rows, D_MODEL)),
        out_shape=f32(rows, D_MODEL),
        scratch_shapes=[pltpu.VMEM((rows, D_MODEL), BF16)],
        compiler_params=cparams(("arbitrary",)),
        name="sample_back",
    )(*back_in)
    return y, (conv_o, ssm_o, gla_o, ko, vo)


SWA_HEAD_ORDER = (0, 2, 1, 3)


def _win_tile_runs():
    sizes = (SSD_WIDTH, SSD_CONV_DIM, SSD_HEADS, GLA_HEADS * GLA_DK, GLA_HEADS * GLA_DK, GLA_WIDTH, GLA_WIDTH,
             GLA_RANK, SWA_WIDTH, SWA_KV_HEADS * SWA_HEAD_DIM, SWA_KV_HEADS * SWA_HEAD_DIM, SWA_WIDTH)
    offs = np.concatenate([[0], np.cumsum(sizes)])
    seg = lambda k: np.arange(offs[k], offs[k + 1])
    z, xbc, dt, gq, gk, gv, gg, glr, sq, sk, sv, sg = [seg(k) for k in range(len(sizes))]
    heads = lambda a: np.concatenate([a[h * SWA_HEAD_DIM:(h + 1) * SWA_HEAD_DIM] for h in SWA_HEAD_ORDER])
    pad = np.full(LANES - SSD_HEADS - GLA_RANK, -1)
    src = np.concatenate([xbc, z, gq, gk, gv, gg, heads(sq), sk, sv, heads(sg), dt, glr, pad])
    assert src.size == XBC_W + PROJ_W
    tiles = []
    for j in range(src.size // LANES):
        idx = src[j * LANES:(j + 1) * LANES]
        cuts = [0] + [k for k in range(1, LANES) if (idx[k] != idx[k - 1] + 1 and not (idx[k] == -1 == idx[k - 1]))]
        runs = [(int(idx[a]), b - a) for a, b in zip(cuts, cuts[1:] + [LANES])]
        assert all(n % SUBLANES == 0 and (s < 0 or s % SUBLANES == 0) for s, n in runs)
        tiles.append(runs)
    return tiles


def _win_prep_kernel(tile_runs, wt_ref, out_ref):
    for j, runs in enumerate(tile_runs):
        parts = [jnp.zeros((n, D_MODEL), F32) if s < 0 else wt_ref[0, s:s + n, :] for s, n in runs]
        tile = parts[0] if len(parts) == 1 else jnp.concatenate(parts, axis=0)
        out_ref[0, :, j * LANES:(j + 1) * LANES] = tile.T.astype(BF16)


def _prepare_w_in(w_in):
    depth, d_model, d_in = w_in.shape
    w_t = jnp.swapaxes(w_in, 1, 2)
    return pl.pallas_call(
        functools.partial(_win_prep_kernel, _win_tile_runs()),
        grid=(depth,),
        in_specs=[pl.BlockSpec((1, d_in, d_model), lambda l: (l, 0, 0))],
        out_specs=pl.BlockSpec((1, d_model, XBC_W + PROJ_W), lambda l: (l, 0, 0)),
        out_shape=jax.ShapeDtypeStruct((depth, d_model, XBC_W + PROJ_W), BF16),
        compiler_params=pltpu.CompilerParams(
            dimension_semantics=("arbitrary",), vmem_limit_bytes=VMEM_LIMIT_BYTES),
        name="w_in_prep",
    )(w_t)


def _prepare_weights(norm_w, w_in, conv_w, conv_b, dt_bias, a_log, d_skip, ssd_norm_w, gla_w_gk, gla_b_gk,
                     gla_norm_w, q_norm_w, k_norm_w, w_out, w_pe, w_pg):
    mix_w = SSD_WIDTH + GLA_WIDTH
    w_out_p = jnp.concatenate(
        [w_out[:, :mix_w, :]] + [w_out[:, mix_w + h * SWA_HEAD_DIM:mix_w + (h + 1) * SWA_HEAD_DIM, :]
                                 for h in SWA_HEAD_ORDER], axis=1).astype(BF16)
    lane_pad = lambda x: jnp.pad(x, ((0, 0), (0, LANES - x.shape[-1])))[:, None, :]
    wgk_p = jnp.pad(gla_w_gk, ((0, 0), (LR_LANE0, LANES - LR_LANE0 - GLA_RANK), (0, 0))).astype(BF16)
    return dict(
        norm_w=norm_w[:, None, :], w_in=_prepare_w_in(w_in), conv_w=conv_w, conv_b=conv_b[:, None, :],
        dt_bias=lane_pad(dt_bias), a_log=lane_pad(a_log),
        d_skip=jnp.repeat(d_skip, SSD_HEAD_DIM, axis=-1)[:, None, :], ssd_norm_w=ssd_norm_w[:, None, :],
        gla_w_gk=wgk_p, gla_b_gk=gla_b_gk[:, None, :],
        gla_norm_w=jnp.tile(gla_norm_w, (1, GLA_HEADS))[:, None, :],
        q_norm_w=jnp.tile(q_norm_w, (1, SWA_HEADS))[:, None, :],
        k_norm_w=jnp.tile(k_norm_w, (1, SWA_KV_HEADS))[:, None, :],
        w_out=w_out_p, w_pe=w_pe.astype(BF16), w_pg=w_pg.astype(BF16))


def kernel(x_prompt, x_sample, state_ssm, state_conv, state_gla, cache_swa_k, cache_swa_v, p_prompt, p_sample, rel_bias, norm_w, w_in, conv_w, conv_b, dt_bias, a_log, d_skip, ssd_norm_w, gla_w_gk, gla_b_gk, gla_norm_w, q_norm_w, k_norm_w, attn_sinks, w_out, w_pe, w_pg):
    depth = w_in.shape[0]
    bp, seq_p, _ = x_prompt.shape
    bs, seq_s, _ = x_sample.shape
    assert seq_s == SUBLANES and BLK % seq_s == 0 and (bs * seq_s) % BLK == 0
    assert cache_swa_k.shape[2] == WINDOW
    w = _prepare_weights(norm_w, w_in, conv_w, conv_b, dt_bias, a_log, d_skip, ssd_norm_w, gla_w_gk, gla_b_gk,
                         gla_norm_w, q_norm_w, k_norm_w, w_out, w_pe, w_pg)
    rel_flat = rel_bias.reshape(-1)
    dist_p = WINDOW + np.arange(BLK)[:, None] - np.arange(2 * BLK)[None, :]
    bucket_p = jnp.asarray(_bucket_table(dist_p))
    t_of_row = np.tile(np.arange(seq_s), SWA_HEADS)[:, None]
    bucket_c = jnp.asarray(_bucket_table(WINDOW + t_of_row - np.arange(WINDOW)[None, :]))
    dist_n = np.where(np.arange(LANES)[None, :] < seq_s, t_of_row - np.arange(LANES)[None, :], -1)
    bucket_n = jnp.asarray(_bucket_table(dist_n))

    ssm_in = jnp.transpose(state_ssm, (0, 2, 3, 4, 1)).reshape(depth, SSD_HEADS, SSD_HEAD_DIM * SSD_STATE, bs)
    gla_in = jnp.transpose(state_gla, (0, 2, 3, 4, 1)).reshape(depth, GLA_HEADS, GLA_DK * GLA_DV, bs)
    kv_in = lambda a: jnp.transpose(a, (0, 1, 3, 4, 2)).reshape(depth, bs, SWA_KV_HEADS * SWA_HEAD_DIM, WINDOW)
    kc_in, vc_in = kv_in(cache_swa_k), kv_in(cache_swa_v)
    conv_in = jnp.transpose(state_conv, (0, 2, 1, 3))

    hp = x_prompt.reshape(bp * seq_p, D_MODEL)
    p_prompt_rows = p_prompt.reshape(depth, bp * seq_p, PLE_DIM)
    hs = jnp.transpose(x_sample, (1, 0, 2)).reshape(seq_s * bs, D_MODEL)
    p_sample_rows = jnp.transpose(p_sample, (0, 2, 1, 3)).reshape(depth, seq_s * bs, PLE_DIM)
    states_p = ()
    states_s = ()
    for i in range(depth):
        wops, wspecs = _layer_weights(i, w)
        hp, *states_p = _prompt_layer(i, depth, bp, hp, p_prompt_rows, tuple(states_p), bucket_p, rel_flat,
                                      attn_sinks[i], wops, wspecs)
        hs, states_s = _sample_layer_native(i, depth, seq_s, hs, p_sample_rows, conv_in, ssm_in, gla_in, kc_in,
                                            vc_in, states_s, (bucket_c, bucket_n), rel_flat, attn_sinks[i], wops)
    ssm_p, conv_p, gla_p, kt_p, vt_p = states_p
    conv_s, ssm_s, gla_s, kt_s, vt_s = states_s
    unpack_kv = lambda a: jnp.transpose(
        a.reshape(a.shape[:2] + (SWA_KV_HEADS, SWA_HEAD_DIM, WINDOW)), (0, 1, 4, 2, 3))
    outs_p = (ssm_p.reshape(depth, bp, SSD_HEADS, SSD_HEAD_DIM, SSD_STATE), conv_p,
              gla_p.reshape(depth, bp, GLA_HEADS, GLA_DK, GLA_DV), unpack_kv(kt_p), unpack_kv(vt_p))
    seq_last = lambda a, dims: jnp.transpose(a.reshape(a.shape[:2] + dims + (bs,)), (0, 4, 1, 2, 3))
    outs_s = (seq_last(ssm_s, (SSD_HEAD_DIM, SSD_STATE)), jnp.transpose(conv_s, (0, 2, 1, 3)),
              seq_last(gla_s, (GLA_DK, GLA_DV)), unpack_kv(kt_s), unpack_kv(vt_s))
    y_sample = jnp.transpose(hs.reshape(seq_s, bs, D_MODEL), (1, 0, 2))
    return (hp.reshape(bp, seq_p, D_MODEL), y_sample) + outs_p + outs_s
```

```python
import functools
import math

import numpy as np
import jax
import jax.numpy as jnp
from jax import lax
from jax.experimental import pallas as pl
from jax.experimental.pallas import tpu as pltpu

D_MODEL = 1024
DEPTH = 2
SSD_HEADS = 8
SSD_HEAD_DIM = 64
SSD_WIDTH = SSD_HEADS * SSD_HEAD_DIM
SSD_GROUPS = 2
SSD_STATE = 64
SSD_CONV = 4
SSD_CONV_DIM = SSD_WIDTH + 2 * SSD_GROUPS * SSD_STATE
SSD_CHUNK = 128
GLA_HEADS = 4
GLA_DK = 32
GLA_DV = 64
GLA_WIDTH = GLA_HEADS * GLA_DV
GLA_RANK = 16
GLA_GATE_NORM = 16.0
GLA_CHUNK = 64
SWA_HEADS = 4
SWA_KV_HEADS = 2
SWA_HEAD_DIM = 64
SWA_WIDTH = SWA_HEADS * SWA_HEAD_DIM
WINDOW = 128
REL_BUCKETS = 32
REL_MAX_DIST = 128
PLE_DIM = 256
EPS = 1e-6

LANES = 128
SUBLANES = 8
HALF = LANES // 2
BLK = 128
VMEM_LIMIT_BYTES = 56 * 1024 * 1024

XBC_W = SSD_CONV_DIM
P_Z = 0
P_GQ = P_Z + SSD_WIDTH
P_GK = P_GQ + LANES
P_GV = P_GK + LANES
P_GG = P_GV + GLA_WIDTH
P_SQ = P_GG + GLA_WIDTH
P_SK = P_SQ + SWA_WIDTH
P_SV = P_SK + LANES
P_SG = P_SV + LANES
P_DTLR = P_SG + SWA_WIDTH
PROJ_W = P_DTLR + LANES
LR_LANE0 = SSD_HEADS

F32 = jnp.float32
BF16 = jnp.bfloat16
NEG_INF = float("-inf")
N_PROMPT_INPUTS = 22
PROMPT_CHUNK_ROWS = 2 * BLK
TICKS_PER_ITEM = 2
NT_DIMS = (((1,), (1,)), ((), ()))
TN_DIMS = (((0,), (0,)), ((), ()))


def _iota(shape, dim):
    return lax.broadcasted_iota(jnp.int32, shape, dim)


def _div(x, d):
    return x >> (d.bit_length() - 1)


def _mod(x, d):
    return x & (d - 1)


def _softplus(x):
    e = jnp.exp(-jnp.abs(x))
    u = 1.0 + e
    d = u - 1.0
    log1p_e = jnp.where(d == 0.0, e, jnp.log(u) * (e / jnp.where(d == 0.0, 1.0, d)))
    return jnp.maximum(x, 0.0) + log1p_e


def _log_sigmoid(x):
    return jnp.minimum(x, 0.0) - jnp.log(1.0 + jnp.exp(-jnp.abs(x)))


def _silu(x):
    return x * jax.nn.sigmoid(x)


def _dot(a, b):
    return jnp.dot(a.astype(BF16), b.astype(BF16), preferred_element_type=F32)


def _dot_nt(a, b):
    return lax.dot_general(a.astype(BF16), b.astype(BF16), NT_DIMS, preferred_element_type=F32)


def _dot_tn(a, b):
    return lax.dot_general(a.astype(BF16), b.astype(BF16), TN_DIMS, preferred_element_type=F32)


def _dot_exact(sel, x):
    x1 = x.astype(BF16)
    r1 = x - x1.astype(F32)
    x2 = r1.astype(BF16)
    x3 = (r1 - x2.astype(F32)).astype(BF16)
    dot = functools.partial(jnp.dot, sel, preferred_element_type=F32)
    return dot(x1) + dot(x2) + dot(x3)


def _expand_heads(x, n_heads):
    rows = x.shape[0]
    lo = _iota((rows, LANES), 1) < HALF
    tiles = []
    for j in range(n_heads // 2):
        a = jnp.broadcast_to(x[:, 2 * j:2 * j + 1], (rows, LANES))
        b = jnp.broadcast_to(x[:, 2 * j + 1:2 * j + 2], (rows, LANES))
        tiles.append(jnp.where(lo, a, b))
    return jnp.concatenate(tiles, axis=1)


def _head_rms_scale(x):
    rows, width = x.shape
    lo = _iota((rows, LANES), 1) < HALF
    outs = []
    for j in range(width // LANES):
        t = x[:, j * LANES:(j + 1) * LANES]
        sq = t * t
        s_lo = jnp.sum(jnp.where(lo, sq, 0.0), axis=-1, keepdims=True)
        s_hi = jnp.sum(jnp.where(lo, 0.0, sq), axis=-1, keepdims=True)
        outs.append(lax.rsqrt(jnp.where(lo, s_lo, s_hi) * (1.0 / HALF) + EPS))
    return outs[0] if len(outs) == 1 else jnp.concatenate(outs, axis=1)


def _group_rmsnorm(y, w):
    gw = SSD_WIDTH // SSD_GROUPS
    outs = []
    for g in range(SSD_GROUPS):
        t = y[:, g * gw:(g + 1) * gw]
        ms = jnp.sum(t * t, axis=-1, keepdims=True) * (1.0 / gw)
        outs.append(t * lax.rsqrt(ms + EPS))
    return jnp.concatenate(outs, axis=1) * w


def _rel_bucket_np(dist):
    n = np.maximum(dist, 0)
    exact = REL_BUCKETS // 2
    nf = np.maximum(n, 1).astype(np.float64)
    large = exact + (np.log(nf / exact) / math.log(REL_MAX_DIST / exact) * (REL_BUCKETS - exact)).astype(np.int32)
    large = np.minimum(large, REL_BUCKETS - 1)
    return np.where(n < exact, n, large).astype(np.int32)


def _bucket_table(dist):
    return np.where((dist >= 0) & (dist < WINDOW), _rel_bucket_np(dist), -1).astype(np.int32)


def _no_tick():
    pass


def _ssd_intra(xbc_c, dtlr, dtb, a_row, pair_mask, tri, total_of, tick=_no_tick):
    xs = xbc_c[:, :SSD_WIDTH]
    bm = xbc_c[:, SSD_WIDTH:SSD_WIDTH + LANES]
    cm = xbc_c[:, SSD_WIDTH + LANES:]
    lane = _iota((BLK, LANES), 1)
    lo = lane < HALF
    dtv = _softplus(dtlr + dtb)
    adt = dtv * a_row
    acum = _dot_exact(tri, adt)
    tick()
    acum_t = acum.T
    eacum = jnp.exp(acum)
    tail = jnp.exp(total_of(adt, acum) - acum)
    dtv_e = _expand_heads(dtv, SSD_HEADS)
    eacum_e = _expand_heads(eacum, SSD_HEADS)
    tail_e = _expand_heads(tail, SSD_HEADS)
    tick()
    xdt = xs * dtv_e
    xw = xdt * tail_e
    cb = [_dot_nt(jnp.where(lo, cm, 0.0), bm), _dot_nt(jnp.where(lo, 0.0, cm), bm)]
    y_pairs = []
    for j in range(SSD_HEADS // 2):
        tick()
        g = (2 * j) // (SSD_HEADS // SSD_GROUPS)
        ms = []
        for k in range(2):
            h = 2 * j + k
            seg = acum[:, h:h + 1] - acum_t[h:h + 1, :]
            dec = jnp.where(pair_mask, jnp.exp(seg), 0.0)
            ms.append((cb[g] * dec).astype(BF16))
        xp = xdt[:, j * LANES:(j + 1) * LANES]
        rhs = jnp.concatenate([jnp.where(lo, xp, 0.0), jnp.where(lo, 0.0, xp)], axis=0)
        y_pairs.append(_dot(jnp.concatenate(ms, axis=1), rhs))
    y_intra = jnp.concatenate(y_pairs, axis=1)
    return y_intra, xs, bm, cm, xw, eacum, eacum_e


def _gla_intra(gq, gk, gv, glog, tri, total_of, att_masks, tick=_no_tick):
    bcs = _dot_exact(tri, glog)
    tick()
    eb = jnp.exp(bcs)
    qe = gq * (GLA_DK ** -0.5) * eb
    ke = gk * jnp.exp(-bcs)
    btot = total_of(glog, bcs)
    kd = gk * jnp.exp(btot - bcs)
    lane_k = _iota((GLA_CHUNK, LANES), 1)
    lane_v = _iota((GLA_CHUNK, GLA_WIDTH), 1)
    outs = []
    for c2 in range(BLK // GLA_CHUNK):
        tick()
        rs = slice(c2 * GLA_CHUNK, (c2 + 1) * GLA_CHUNK)
        ke_c = ke[rs]
        v_c = gv[rs]
        kbd = jnp.concatenate(
            [jnp.where(_div(lane_k, GLA_DK) == h, ke_c, 0.0) for h in range(GLA_HEADS)], axis=0)
        att = _dot_nt(qe[rs], kbd)
        att = jnp.where(att_masks[c2], att, 0.0)
        vbd = jnp.concatenate(
            [jnp.where(_div(lane_v, GLA_DV) == h, v_c, 0.0) for h in range(GLA_HEADS)], axis=0)
        outs.append(_dot(att, vbd))
    return jnp.concatenate(outs, axis=0), qe, kd, jnp.exp(btot)


def _build_bias(bucket, rel_ref, n_heads_in_lanes=None):
    accs = [jnp.full(bucket.shape, NEG_INF, F32) for _ in range(SWA_HEADS)]
    for b in range(REL_BUCKETS):
        hit = bucket == b
        for h in range(SWA_HEADS):
            accs[h] = jnp.where(hit, rel_ref[b * SWA_HEADS + h], accs[h])
    return accs


def _epilogue(h, mix, p, wout_ref, wpg_ref, wpe_ref):
    h1 = h + jnp.dot(mix, wout_ref[...], preferred_element_type=F32)
    gate = jax.nn.sigmoid(jnp.dot(h1.astype(BF16), wpg_ref[...], preferred_element_type=F32))
    pe = jnp.dot(p.astype(BF16), wpe_ref[...], preferred_element_type=F32)
    return h1 + gate * pe


def _project(h, nw_ref, win_ref, xbc_s, proj_s, rows):
    ms = jnp.mean(h * h, axis=-1, keepdims=True)
    u = (h * lax.rsqrt(ms + EPS) * nw_ref[...]).astype(BF16)
    xbc_s[SUBLANES:SUBLANES + rows, :] = jnp.dot(u, win_ref[:, :XBC_W], preferred_element_type=F32)
    proj_s[...] = jnp.dot(u, win_ref[:, XBC_W:], preferred_element_type=F32)


def _prompt_kernel(chunks_per_seq, n_aliased, *refs):
    (ha_ref, hc_ref, p_ref, bucket_ref, rel_ref, sink_ref, nw_ref, win_ref, cw_ref, cb_ref, dtb_ref,
     alog_ref, dsk_ref, snw_ref, wgk_ref, bgk_ref, gnw_ref, qnw_ref, knw_ref, wout_ref,
     wpe_ref, wpg_ref) = refs[:N_PROMPT_INPUTS]
    (y_ref, ssm_ref, conv_ref, gla_ref, ko_ref, vo_ref,
     proj_e, proj_o, xbc_e, xbc_o, mix_e, mix_o, u_s, h1_s, hist_s, st_s, s2_s, kext_s, vext_s,
     bias_s) = refs[N_PROMPT_INPUTS + n_aliased:]
    k_idx = pl.program_id(0)

    @pl.when(k_idx == 0)
    def _():
        accs = _build_bias(bucket_ref[...], rel_ref)
        own_block = _iota((BLK, 2 * BLK), 1) >= BLK
        for hh in range(SWA_HEADS):
            bias_s[hh] = accs[hh]
            bias_s[SWA_HEADS + hh] = jnp.where(own_block, accs[hh], NEG_INF)
        for ref in (proj_o, xbc_o, mix_e, mix_o, hist_s, st_s, s2_s, kext_s, vext_s):
            ref[...] = jnp.zeros(ref.shape, ref.dtype)

    row = _iota((BLK, BLK), 0)
    col = _iota((BLK, BLK), 1)
    causal = row >= col
    tri_ssd = jnp.where(causal, 1.0, 0.0).astype(BF16)
    tri_gla = jnp.where(causal & (_div(row, GLA_CHUNK) == _div(col, GLA_CHUNK)), 1.0, 0.0).astype(BF16)
    lo = col < HALF
    lane_row = _iota((1, LANES), 1)
    a_row = jnp.where(lane_row < SSD_HEADS, -jnp.exp(alog_ref[...]), 0.0)
    st_mask = (_iota((BLK, SSD_WIDTH), 0) < SSD_STATE) == (_iota((BLK, SSD_WIDTH), 1) < SSD_WIDTH // SSD_GROUPS)
    bd_mask = _div(_iota((LANES, GLA_WIDTH), 0), GLA_DK) == _div(_iota((LANES, GLA_WIDTH), 1), GLA_DV)
    att_t = _iota((GLA_CHUNK, GLA_WIDTH), 0)
    att_s = _mod(_iota((GLA_CHUNK, GLA_WIDTH), 1), GLA_CHUNK)
    att_mask = att_s <= att_t
    lo2 = _iota((2 * BLK, LANES), 1) < HALF

    def ssd_total(adt, acum):
        return jnp.broadcast_to(acum[BLK - 1:BLK, :], acum.shape)

    def gla_total(glog, bcs):
        return jnp.concatenate(
            [jnp.broadcast_to(bcs[(c2 + 1) * GLA_CHUNK - 1:(c2 + 1) * GLA_CHUNK, :], (GLA_CHUNK, LANES))
             for c2 in range(BLK // GLA_CHUNK)], axis=0)

    def block(blk, proj_s, xbc_s, mix_s, starts_sequence, tick):
        rows = slice(blk * BLK, (blk + 1) * BLK)
        cw = cw_ref[...]
        tick()
        if blk == 0:
            xwin = jnp.concatenate([hist_s[...], xbc_s[0:BLK, :]], axis=0)
        else:
            xwin = xbc_s[blk * BLK - SUBLANES:(blk + 1) * BLK, :]
        acc = xwin[SUBLANES - 3:SUBLANES - 3 + BLK, :] * cw[0:1, :]
        for k in range(1, SSD_CONV):
            acc = acc + xwin[SUBLANES - 3 + k:SUBLANES - 3 + k + BLK, :] * cw[k:k + 1, :]
        xbc_c = _silu(acc + cb_ref[...])
        tick()
        dtlr = proj_s[rows, P_DTLR:P_DTLR + LANES]
        y_intra, xs, bm, cm, xw, eacum, eacum_e = _ssd_intra(
            xbc_c, dtlr, dtb_ref[...], a_row, causal, tri_ssd, ssd_total, tick)
        tick()
        st = st_s[...]
        y = y_intra + _dot(cm, st) * eacum_e + dsk_ref[...] * xs
        st_s[...] = st * eacum_e[BLK - 1:BLK, :] + jnp.where(st_mask, _dot(bm.T, xw), 0.0)
        tick()
        y = y * _silu(proj_s[rows, P_Z:P_Z + SSD_WIDTH])
        mix_s[rows, 0:SSD_WIDTH] = _group_rmsnorm(y, snw_ref[...]).astype(BF16)
        tick()
        gk = proj_s[rows, P_GK:P_GK + LANES]
        gv = proj_s[rows, P_GV:P_GV + GLA_WIDTH]
        glog = _log_sigmoid(_dot(dtlr, wgk_ref[...]) + bgk_ref[...]) * (1.0 / GLA_GATE_NORM)
        o_intra, qe, kd, ebt = _gla_intra(
            proj_s[rows, P_GQ:P_GQ + LANES], gk, gv, glog, tri_gla, gla_total, [att_mask, att_mask], tick)
        kd_t = kd.T
        ebt_t = ebt.T
        s2 = s2_s[...]
        o_parts = []
        for c2 in range(BLK // GLA_CHUNK):
            tick()
            rs = slice(c2 * GLA_CHUNK, (c2 + 1) * GLA_CHUNK)
            o_parts.append(o_intra[rs] + _dot(qe[rs], s2))
            u2 = _dot(jnp.where(_div(col, GLA_CHUNK) == c2, kd_t, 0.0), gv)
            last = (c2 + 1) * GLA_CHUNK - 1
            s2 = s2 * ebt_t[:, last:last + 1] + jnp.where(bd_mask, u2, 0.0)
        s2_s[...] = s2
        o = jnp.concatenate(o_parts, axis=0)
        y_gla = o * _head_rms_scale(o) * gnw_ref[...] * _silu(proj_s[rows, P_GG:P_GG + GLA_WIDTH])
        mix_s[rows, SSD_WIDTH:SSD_WIDTH + GLA_WIDTH] = y_gla.astype(BF16)
        tick()
        sq = proj_s[rows, P_SQ:P_SQ + SWA_WIDTH]
        qn = sq * _head_rms_scale(sq) * qnw_ref[...] * (SWA_HEAD_DIM ** -0.5)
        sk = proj_s[rows, P_SK:P_SK + LANES]
        kn = sk * _head_rms_scale(sk) * knw_ref[...]
        vn = proj_s[rows, P_SV:P_SV + LANES]
        kext_s[BLK:2 * BLK, :] = kn
        vext_s[BLK:2 * BLK, :] = vn
        kext = kext_s[...]
        vext = vext_s[...]
        qa = qn[:, :LANES]
        qb = qn[:, LANES:]
        qs = jnp.concatenate([jnp.where(lo, qa, 0.0), jnp.where(lo, qb, 0.0),
                              jnp.where(lo, 0.0, qa), jnp.where(lo, 0.0, qb)], axis=0)
        logits = _dot_nt(qs, kext)
        tick()
        if blk == 0 and starts_sequence is not False:
            bias_row0 = jnp.where(starts_sequence, SWA_HEADS, 0)
        else:
            bias_row0 = 0
        es = []
        invs = []
        for hh in range(SWA_HEADS):
            tick()
            sink = sink_ref[hh]
            l = logits[hh * BLK:(hh + 1) * BLK] + bias_s[bias_row0 + hh]
            m = jnp.maximum(jnp.max(l, axis=-1, keepdims=True), sink)
            e = jnp.exp(l - m)
            den = jnp.sum(e, axis=-1, keepdims=True) + jnp.exp(sink - m)
            es.append(e.astype(BF16))
            invs.append(1.0 / den)
        v_stack = jnp.concatenate([jnp.where(lo2, vext, 0.0), jnp.where(lo2, 0.0, vext)], axis=0)
        tile_a = _dot(jnp.concatenate([es[0], es[2]], axis=1), v_stack) * jnp.where(lo, invs[0], invs[2])
        tile_b = _dot(jnp.concatenate([es[1], es[3]], axis=1), v_stack) * jnp.where(lo, invs[1], invs[3])
        oa = jnp.concatenate([tile_a, tile_b], axis=1)
        y_swa = oa * _silu(proj_s[rows, P_SG:P_SG + SWA_WIDTH])
        mix_s[rows, SSD_WIDTH + GLA_WIDTH:] = y_swa.astype(BF16)
        kext_s[0:BLK, :] = kn
        vext_s[0:BLK, :] = vn

    chunk = proj_e.shape[0]

    def project_items(rows, proj_s, xbc_s):
        def norm():
            h = ha_ref[rows, :]
            ms = jnp.mean(h * h, axis=-1, keepdims=True)
            u_s[...] = (h * lax.rsqrt(ms + EPS) * nw_ref[...]).astype(BF16)

        def cols(dst, lo_c, hi_c, w_off):
            def item():
                dst[:, lo_c:hi_c] = jnp.dot(u_s[...], win_ref[:, w_off + lo_c:w_off + hi_c],
                                            preferred_element_type=F32)
            return item

        step = 2 * LANES
        items = [norm]
        items += [cols(xbc_s, c, min(c + step, XBC_W), 0) for c in range(0, XBC_W, step)]
        items += [cols(proj_s, c, min(c + step, PROJ_W), XBC_W) for c in range(0, PROJ_W, step)]
        return items

    def epilogue_items(rows, mix_s):
        half_w = 2 * LANES

        def residual(c):
            def item():
                h1_s[:, c:c + half_w] = hc_ref[rows, c:c + half_w] + jnp.dot(
                    mix_s[...], wout_ref[:, c:c + half_w], preferred_element_type=F32)
            return item

        def gated(c):
            def item():
                gate = jax.nn.sigmoid(jnp.dot(h1_s[...].astype(BF16), wpg_ref[:, c:c + half_w],
                                              preferred_element_type=F32))
                pe = jnp.dot(p_ref[0, rows, :].astype(BF16), wpe_ref[:, c:c + half_w],
                             preferred_element_type=F32)
                y_ref[rows, c:c + half_w] = h1_s[:, c:c + half_w] + gate * pe
            return item

        col0 = range(0, D_MODEL, half_w)
        return [residual(c) for c in col0] + [gated(c) for c in col0]

    def merge(first, second):
        out = list(first)
        for j, item in enumerate(second):
            out.insert(((j + 1) * len(first)) // len(second) + j, item)
        return out

    def mixer(proj_s, xbc_s, mix_s, starts_sequence, items):
        if starts_sequence is not False:
            keep = jnp.where(starts_sequence, 0.0, 1.0)
            for ref in (hist_s, st_s, s2_s):
                ref[...] = ref[...] * keep
            kext_s[0:BLK, :] = kext_s[0:BLK, :] * keep
            vext_s[0:BLK, :] = vext_s[0:BLK, :] * keep
        queue = list(items)
        calls = [0]

        def tick():
            calls[0] += 1
            if queue and calls[0] % TICKS_PER_ITEM == 0:
                queue.pop(0)()

        for blk in range(chunk // BLK):
            block(blk, proj_s, xbc_s, mix_s, starts_sequence, tick)
        while queue:
            queue.pop(0)()
        hist_s[...] = xbc_s[chunk - SUBLANES:chunk, :]

    def write_states():
        st = st_s[...]
        stc = st[:SSD_STATE] + st[SSD_STATE:]
        ssm_ref[0, 0] = jnp.concatenate([stc, stc], axis=0).T[:, :SSD_STATE]
        conv_ref[0, 0] = hist_s[SUBLANES - (SSD_CONV - 1):SUBLANES, :]
        s2 = s2_s[...]
        w = s2[:, :LANES] + s2[:, LANES:]
        gla_ref[0, 0] = w[:, :GLA_DV] + w[:, GLA_DV:]
        ko_ref[0, 0] = kext_s[0:BLK, :].T
        vo_ref[0, 0] = vext_s[0:BLK, :].T

    even = slice(0, chunk)
    odd = slice(chunk, 2 * chunk)
    mixer(proj_o, xbc_o, mix_o, False, merge(project_items(even, proj_e, xbc_e), epilogue_items(even, mix_e)))
    write_states()
    mixer(proj_e, xbc_e, mix_e, _mod(2 * k_idx, chunks_per_seq) == 0,
          merge(project_items(odd, proj_o, xbc_o), epilogue_items(odd, mix_o)))


def _sample_kernel(seq, n_seq,
                   h_ref, p_ref, ssm_ref, cst_ref, gst_ref, kc_ref, vc_ref, bucket_ref, rel_ref, sink_ref,
                   nw_ref, win_ref, cw_ref, cb_ref, dtb_ref, alog_ref, dsk_ref, snw_ref, wgk_ref, bgk_ref,
                   gnw_ref, qnw_ref, knw_ref, wout_ref, wpe_ref, wpg_ref,
                   y_ref, ssm_o, xbc_o, gla_o, ko_ref, vo_ref,
                   proj_s, xbc_s, cs_s, cm_s, bm_s, xw_s, ead_s, yint_s, qe_s, kd_s, ebt_s, oint_s,
                   qn_s, kn_s, vn_s, oswa_s, kk_s, vv_s, bias_s):
    rows = n_seq * seq
    n_keys = WINDOW + seq

    @pl.when(pl.program_id(0) == 0)
    def _():
        bucket = bucket_ref[...]
        accs = _build_bias(bucket, rel_ref)
        head_of_lane = _div(_iota(bucket.shape, 1), seq)
        out = accs[SWA_HEADS - 1]
        for hh in range(SWA_HEADS - 2, -1, -1):
            out = jnp.where(head_of_lane == hh, accs[hh], out)
        bias_s[...] = out

    xbc_s[0:SUBLANES, :] = jnp.zeros((SUBLANES, XBC_W), F32)
    _project(h_ref[...], nw_ref, win_ref, xbc_s, proj_s, rows)
    cs_s[0:rows, :] = cst_ref[...]
    cs_s[rows:rows + SUBLANES, :] = jnp.zeros((SUBLANES, XBC_W), F32)
    xbc_o[...] = xbc_s[SUBLANES:SUBLANES + rows, :]

    row = _iota((BLK, BLK), 0)
    col = _iota((BLK, BLK), 1)
    same_seq = _div(row, seq) == _div(col, seq)
    pair_mask = same_seq & (row >= col)
    tri = jnp.where(pair_mask, 1.0, 0.0).astype(BF16)
    ones_seq = jnp.where(same_seq, 1.0, 0.0).astype(BF16)
    lo = col < HALF
    lane_row = _iota((1, LANES), 1)
    a_row = jnp.where(lane_row < SSD_HEADS, -jnp.exp(alog_ref[...]), 0.0)
    bd_mask = _div(_iota((LANES, GLA_WIDTH), 0), GLA_DK) == _div(_iota((LANES, GLA_WIDTH), 1), GLA_DV)
    eye = row == col

    def total_of(x, _cum):
        return _dot_exact(ones_seq, x)

    cw = cw_ref[...]
    t_of_row = _mod(_iota((rows, XBC_W), 0), seq)
    taps = []
    for j in range(SSD_CONV - 1, 0, -1):
        cur = xbc_s[pl.ds(SUBLANES - j, rows), :]
        old = cs_s[pl.ds(SUBLANES - j, rows), :]
        taps.append(jnp.where(t_of_row >= j, cur, old))
    taps.append(xbc_s[pl.ds(SUBLANES, rows), :])
    acc = taps[0] * cw[0:1, :]
    for k in range(1, SSD_CONV):
        acc = acc + taps[k] * cw[k:k + 1, :]
    xbc_c = _silu(acc + cb_ref[...])
    dtlr = proj_s[:, P_DTLR:P_DTLR + LANES]
    y_intra, xs, bm, cm, xw, eacum, eacum_e = _ssd_intra(
        xbc_c, dtlr, dtb_ref[...], a_row, pair_mask, tri, total_of)
    cm_s[...] = cm
    bm_s[...] = bm
    xw_s[...] = xw
    ead_s[...] = eacum

    gk = proj_s[:, P_GK:P_GK + LANES]
    gv = proj_s[:, P_GV:P_GV + GLA_WIDTH]
    glog = _log_sigmoid(_dot(dtlr, wgk_ref[...]) + bgk_ref[...]) * (1.0 / GLA_GATE_NORM)
    att_masks = []
    for c2 in range(BLK // GLA_CHUNK):
        t_loc = _iota((GLA_CHUNK, GLA_WIDTH), 0)
        s_loc = _mod(_iota((GLA_CHUNK, GLA_WIDTH), 1), GLA_CHUNK)
        att_masks.append((_div(t_loc, seq) == _div(s_loc, seq)) & (s_loc <= t_loc))
    o_intra, qe, kd, ebt = _gla_intra(
        proj_s[:, P_GQ:P_GQ + LANES], gk, gv, glog, tri, total_of, att_masks)
    qe_s[...] = qe
    kd_s[...] = kd
    ebt_s[...] = ebt

    sq = proj_s[:, P_SQ:P_SQ + SWA_WIDTH]
    qn_s[...] = sq * _head_rms_scale(sq) * qnw_ref[...]
    sk = proj_s[:, P_SK:P_SK + LANES]
    kn_s[...] = sk * _head_rms_scale(sk) * knw_ref[...]
    vn_s[...] = proj_s[:, P_SV:P_SV + LANES]

    lane32 = _div(_iota((1, SWA_HEADS * seq), 1), seq)
    sink_row = jnp.full((1, SWA_HEADS * seq), sink_ref[SWA_HEADS - 1], F32)
    for hh in range(SWA_HEADS - 2, -1, -1):
        sink_row = jnp.where(lane32 == hh, sink_ref[hh], sink_row)
    lo8 = _iota((seq, LANES), 1) < HALF
    heads_per_group = SSD_HEADS // SSD_GROUPS
    gw = SSD_WIDTH // SSD_GROUPS

    def per_seq(b, carry):
        r0 = pl.multiple_of(b * seq, seq)
        rb = pl.ds(r0, seq)
        s_prev = ssm_ref[b]
        cmb = cm_s[rb, :]
        bmb = bm_s[rb, :]
        xwb = xw_s[rb, :]
        yint_s[rb, :] = jnp.concatenate(
            [_dot_nt(cmb[:, g * SSD_STATE:(g + 1) * SSD_STATE], s_prev[g * gw:(g + 1) * gw, :])
             for g in range(SSD_GROUPS)], axis=1)
        upd = [_dot_tn(xwb[:, g * gw:(g + 1) * gw], bmb[:, g * SSD_STATE:(g + 1) * SSD_STATE])
               for g in range(SSD_GROUPS)]
        drow = ead_s[rb, :][seq - 1:seq, :]
        for hh in range(SSD_HEADS):
            g, r = divmod(hh, heads_per_group)
            hs = slice(hh * SSD_HEAD_DIM, (hh + 1) * SSD_HEAD_DIM)
            ssm_o[b, hs, :] = (s_prev[hs, :] * drow[:, hh:hh + 1]
                               + upd[g][r * SSD_HEAD_DIM:(r + 1) * SSD_HEAD_DIM, :])
        g_prev = gst_ref[b]
        g2 = jnp.concatenate([g_prev, g_prev], axis=1)
        s_bd = jnp.where(bd_mask, jnp.concatenate([g2, g2], axis=1), 0.0)
        oint_s[rb, :] = _dot(qe_s[rb, :], s_bd)
        u2 = jnp.where(bd_mask, _dot_tn(kd_s[rb, :], proj_s[rb, P_GV:P_GV + GLA_WIDTH]), 0.0)
        w = u2[:, :LANES] + u2[:, LANES:]
        erow = jnp.broadcast_to(ebt_s[rb, :][seq - 1:seq, :], (LANES, LANES))
        ecol = jnp.sum(jnp.where(eye, erow, 0.0), axis=-1, keepdims=True)
        gla_o[b] = g_prev * ecol + (w[:, :GLA_DV] + w[:, GLA_DV:])
        knb = kn_s[rb, :]
        vnb = vn_s[rb, :]
        kk_s[0:WINDOW, :] = kc_ref[b]
        kk_s[WINDOW:n_keys, :] = knb
        vv_s[0:WINDOW, :] = vc_ref[b]
        vv_s[WINDOW:n_keys, :] = vnb
        qa = qn_s[rb, 0:LANES]
        qb = qn_s[rb, LANES:2 * LANES]
        qs = jnp.concatenate([jnp.where(lo8, qa, 0.0), jnp.where(lo8, qb, 0.0),
                              jnp.where(lo8, 0.0, qa), jnp.where(lo8, 0.0, qb)], axis=0)
        l = _dot_nt(kk_s[...], qs) * (SWA_HEAD_DIM ** -0.5) + bias_s[...]
        m = jnp.maximum(jnp.max(l, axis=0, keepdims=True), sink_row)
        e = jnp.exp(l - m)
        den = jnp.sum(e, axis=0, keepdims=True) + jnp.exp(sink_row - m)
        o = _dot_tn(e / den, vv_s[...])
        tile_a = jnp.where(lo8, o[0:seq], o[2 * seq:3 * seq])
        tile_b = jnp.where(lo8, o[seq:2 * seq], o[3 * seq:4 * seq])
        oswa_s[rb, :] = jnp.concatenate([tile_a, tile_b], axis=1)
        ko_ref[b, 0:WINDOW - seq, :] = kc_ref[b, seq:WINDOW, :]
        ko_ref[b, WINDOW - seq:WINDOW, :] = knb
        vo_ref[b, 0:WINDOW - seq, :] = vc_ref[b, seq:WINDOW, :]
        vo_ref[b, WINDOW - seq:WINDOW, :] = vnb
        return carry

    lax.fori_loop(0, n_seq, per_seq, 0)

    y = y_intra + yint_s[...] * eacum_e + dsk_ref[...] * xs
    y = y * _silu(proj_s[:, P_Z:P_Z + SSD_WIDTH])
    y_ssd = _group_rmsnorm(y, snw_ref[...])
    o = o_intra + oint_s[...]
    y_gla = o * _head_rms_scale(o) * gnw_ref[...] * _silu(proj_s[:, P_GG:P_GG + GLA_WIDTH])
    y_swa = oswa_s[...] * _silu(proj_s[:, P_SG:P_SG + SWA_WIDTH])
    mix = jnp.concatenate([y_ssd, y_gla, y_swa], axis=1).astype(BF16)
    y_ref[...] = _epilogue(h_ref[...], mix, p_ref[...], wout_ref, wpg_ref, wpe_ref)


N_FRONT_INPUTS = 12


def _sample_front_kernel(seq, n_aliased, *refs):
    (h_ref, cst_ref, nw_ref, win_ref, cw_ref, cb_ref, dtb_ref, alog_ref, wgk_ref, bgk_ref,
     qnw_ref, knw_ref) = refs[:N_FRONT_INPUTS]
    (xs_ref, gates_ref, xt_ref, bt_ref, ct_ref, at_ref, qt_ref, kt_ref, egt_ref, vt_ref,
     qn_ref, kn_ref, vn_ref, conv_ref, u_s, xbc_s, proj_s) = refs[N_FRONT_INPUTS + n_aliased:]
    n_seq = BLK
    for t in range(seq):
        rows = slice(t * n_seq, (t + 1) * n_seq)
        ht = h_ref[rows, :]
        ms = jnp.mean(ht * ht, axis=-1, keepdims=True)
        u_s[rows, :] = (ht * lax.rsqrt(ms + EPS) * nw_ref[...]).astype(BF16)
    xbc_s[...] = jnp.dot(u_s[...], win_ref[:, :XBC_W], preferred_element_type=F32)
    proj_s[...] = jnp.dot(u_s[...], win_ref[:, XBC_W:], preferred_element_type=F32)
    cw = cw_ref[...]
    a_row = jnp.where(_iota((1, LANES), 1) < SSD_HEADS, -jnp.exp(alog_ref[...]), 0.0)
    for t in range(seq):
        rows = slice(t * n_seq, (t + 1) * n_seq)

        def raw_xbc(back):
            if t >= back:
                return xbc_s[(t - back) * n_seq:(t - back + 1) * n_seq, :]
            return cst_ref[0, SSD_CONV - 1 + t - back]

        acc = raw_xbc(SSD_CONV - 1) * cw[0:1, :]
        for k in range(1, SSD_CONV):
            acc = acc + raw_xbc(SSD_CONV - 1 - k) * cw[k:k + 1, :]
        xbc_c = _silu(acc + cb_ref[...])
        xs = xbc_c[:, :SSD_WIDTH]
        dtlr = proj_s[rows, P_DTLR:P_DTLR + LANES]
        dtv = _softplus(dtlr + dtb_ref[...])
        xs_ref[rows, :] = xs
        xt_ref[t] = (xs * _expand_heads(dtv, SSD_HEADS)).T
        bt_ref[t] = xbc_c[:, SSD_WIDTH:SSD_WIDTH + LANES].T
        ct_ref[t] = xbc_c[:, SSD_WIDTH + LANES:].T
        at_ref[t] = jnp.exp(dtv * a_row).T[:SSD_HEADS, :]
        glog = _log_sigmoid(_dot(dtlr, wgk_ref[...]) + bgk_ref[...]) * (1.0 / GLA_GATE_NORM)
        qt_ref[t] = (proj_s[rows, P_GQ:P_GQ + LANES] * (GLA_DK ** -0.5)).T
        kt_ref[t] = proj_s[rows, P_GK:P_GK + LANES].T
        egt_ref[t] = jnp.exp(glog).T
        vt_ref[t] = proj_s[rows, P_GV:P_GV + GLA_WIDTH].T
        sq = proj_s[rows, P_SQ:P_SQ + SWA_WIDTH]
        qn_ref[rows, :] = sq * _head_rms_scale(sq) * qnw_ref[...] * (SWA_HEAD_DIM ** -0.5)
        sk = proj_s[rows, P_SK:P_SK + LANES]
        kn_ref[rows, :] = sk * _head_rms_scale(sk) * knw_ref[...]
        vn_ref[rows, :] = proj_s[rows, P_SV:P_SV + LANES]
        gates_ref[rows, :] = jnp.concatenate(
            [_silu(proj_s[rows, P_Z:P_Z + SSD_WIDTH]), _silu(proj_s[rows, P_GG:P_GG + GLA_WIDTH]),
             _silu(proj_s[rows, P_SG:P_SG + SWA_WIDTH])], axis=1)
        if t >= seq - (SSD_CONV - 1):
            conv_ref[0, t - (seq - (SSD_CONV - 1))] = xbc_s[rows, :]


N_STATE_INPUTS = 19
SEQ_PER_STEP = 16


def _sample_state_kernel(seq, n_aliased, *refs):
    (xt_ref, bt_ref, ct_ref, at_ref, qt_ref, kt_ref, egt_ref, vt_ref, qn_ref, kn_ref, vn_ref,
     ssm_ref, gla_ref, kc_ref, vc_ref, bucket_c_ref, bucket_n_ref, rel_ref, sink_ref) = refs[:N_STATE_INPUTS]
    (ssm_o, gla_o, ko_ref, vo_ref, yt_ref, ot_ref, oswa_ref,
     qa_s, qb_s, krow_s, vrow_s, oa_s, ob_s, biasc_s, biasn_s) = refs[N_STATE_INPUTS + n_aliased:]
    j = pl.program_id(0)
    n_seq = LANES
    head_of_row = _div(_iota((SWA_HEADS * seq, LANES), 0), seq)

    def by_head(values):
        out = values[SWA_HEADS - 1]
        for hh in range(SWA_HEADS - 2, -1, -1):
            out = jnp.where(head_of_row == hh, values[hh], out)
        return out

    @pl.when(j == 0)
    def _():
        biasc_s[...] = by_head(_build_bias(bucket_c_ref[...], rel_ref))
        biasn_s[...] = by_head(_build_bias(bucket_n_ref[...], rel_ref))

    sub = _iota((SUBLANES, LANES), 0)
    a_rows = [jnp.sum(jnp.where(sub == j, at_ref[t], 0.0), axis=0, keepdims=True) for t in range(seq)]

    def ssd_body(p8, carry):
        r8 = pl.multiple_of(p8 * SUBLANES, SUBLANES)
        x_tiles = [xt_ref[t, pl.ds(r8, SUBLANES), :] for t in range(seq)]
        y_rows = [[] for _ in range(seq)]
        for pp in range(SUBLANES):
            r64 = pl.multiple_of((p8 * SUBLANES + pp) * SSD_STATE, SSD_STATE)
            slab = ssm_ref[0, 0, pl.ds(r64, SSD_STATE), :]
            for t in range(seq):
                slab = slab * a_rows[t] + x_tiles[t][pp:pp + 1, :] * bt_ref[t]
                y_rows[t].append(jnp.sum(ct_ref[t] * slab, axis=0, keepdims=True))
            ssm_o[0, 0, pl.ds(r64, SSD_STATE), :] = slab
        for t in range(seq):
            yt_ref[t, pl.ds(r8, SUBLANES), :] = jnp.concatenate(y_rows[t], axis=0)
        return carry

    lax.fori_loop(0, SSD_HEAD_DIM // SUBLANES, ssd_body, 0)

    @pl.when(j < GLA_HEADS)
    def _():
        for t in range(seq):
            ot_ref[t] = jnp.zeros((GLA_DV, LANES), F32)

        def gla_body(d8, carry):
            r8 = pl.multiple_of(d8 * SUBLANES, SUBLANES)
            q_tiles = [qt_ref[t, pl.ds(r8, SUBLANES), :] for t in range(seq)]
            k_tiles = [kt_ref[t, pl.ds(r8, SUBLANES), :] for t in range(seq)]
            g_tiles = [egt_ref[t, pl.ds(r8, SUBLANES), :] for t in range(seq)]
            for dd in range(SUBLANES):
                r64 = pl.multiple_of((d8 * SUBLANES + dd) * GLA_DV, GLA_DV)
                slab = gla_ref[0, 0, pl.ds(r64, GLA_DV), :]
                for t in range(seq):
                    slab = slab * g_tiles[t][dd:dd + 1, :] + k_tiles[t][dd:dd + 1, :] * vt_ref[t]
                    ot_ref[t] = ot_ref[t] + q_tiles[t][dd:dd + 1, :] * slab
                gla_o[0, 0, pl.ds(r64, GLA_DV), :] = slab
            return carry

        lax.fori_loop(0, GLA_DK // SUBLANES, gla_body, 0)

    base = pl.multiple_of(j * SEQ_PER_STEP, SEQ_PER_STEP)
    for t in range(seq):
        src = pl.ds(t * n_seq + base, SEQ_PER_STEP)
        dst = pl.ds(t, SEQ_PER_STEP, stride=seq)
        qa_s[dst, :] = qn_ref[src, 0:LANES]
        qb_s[dst, :] = qn_ref[src, LANES:2 * LANES]
        krow_s[dst, :] = kn_ref[src, :]
        vrow_s[dst, :] = vn_ref[src, :]
    kn_t = krow_s[...].T
    vn_t = vrow_s[...].T
    keep_old = _iota((LANES, WINDOW), 1) < WINDOW - seq
    lo8 = _iota((seq, LANES), 1) < HALF
    sink_col = by_head([jnp.full((SWA_HEADS * seq, LANES), sink_ref[hh], F32) for hh in range(SWA_HEADS)])[:, 0:1]

    def swa_stages(bl):
        r8 = pl.multiple_of(bl * seq, seq)
        v = {}

        def logits():
            qa = qa_s[pl.ds(r8, seq), :]
            qb = qb_s[pl.ds(r8, seq), :]
            qs = jnp.concatenate([jnp.where(lo8, qa, 0.0), jnp.where(lo8, qb, 0.0),
                                  jnp.where(lo8, 0.0, qa), jnp.where(lo8, 0.0, qb)], axis=0)
            v["lc"] = _dot(qs, kc_ref[0, bl]) + biasc_s[...]
            v["ln"] = _dot_nt(qs, krow_s[pl.ds(r8, seq), :]) + biasn_s[:, 0:seq]

        def softmax():
            lc, ln = v["lc"], v["ln"]
            m = jnp.maximum(jnp.maximum(jnp.max(lc, axis=-1, keepdims=True), jnp.max(ln, axis=-1, keepdims=True)),
                            sink_col)
            v["ec"] = jnp.exp(lc - m)
            v["en"] = jnp.exp(ln - m)
            v["inv"] = 1.0 / (jnp.sum(v["ec"], axis=-1, keepdims=True) + jnp.sum(v["en"], axis=-1, keepdims=True)
                              + jnp.exp(sink_col - m))

        def values():
            o = (_dot_nt(v["ec"], vc_ref[0, bl]) + _dot(v["en"], vrow_s[pl.ds(r8, seq), :])) * v["inv"]
            oa_s[pl.ds(r8, seq), :] = jnp.where(lo8, o[0:seq], o[2 * seq:3 * seq])
            ob_s[pl.ds(r8, seq), :] = jnp.where(lo8, o[seq:2 * seq], o[3 * seq:4 * seq])

        def window():
            ko_ref[0, bl] = jnp.where(keep_old, pltpu.roll(kc_ref[0, bl], WINDOW - seq, axis=1),
                                      pltpu.roll(kn_t, WINDOW - seq - r8, axis=1))
            vo_ref[0, bl] = jnp.where(keep_old, pltpu.roll(vc_ref[0, bl], WINDOW - seq, axis=1),
                                      pltpu.roll(vn_t, WINDOW - seq - r8, axis=1))

        return [logits, softmax, values, window]

    all_stages = [swa_stages(bl) for bl in range(SEQ_PER_STEP)]
    for k in range(len(all_stages[0])):
        for stages in all_stages:
            stages[k]()
    for t in range(seq):
        src = pl.ds(t, SEQ_PER_STEP, stride=seq)
        oswa_ref[t] = jnp.concatenate([oa_s[src, :], ob_s[src, :]], axis=1)


def _sample_back_kernel(seq, yt_ref, ot_ref, oswa_ref, xs_ref, gates_ref, h_ref, p_ref, dsk_ref, snw_ref, gnw_ref,
                        wout_ref, wpe_ref, wpg_ref, y_ref, mix_s):
    n_seq = BLK
    for t in range(seq):
        rows = slice(t * n_seq, (t + 1) * n_seq)
        y = (yt_ref[t].T + dsk_ref[...] * xs_ref[rows, :]) * gates_ref[rows, 0:SSD_WIDTH]
        mix_s[rows, 0:SSD_WIDTH] = _group_rmsnorm(y, snw_ref[...]).astype(BF16)
        o = ot_ref[t].T
        y_gla = o * _head_rms_scale(o) * gnw_ref[...] * gates_ref[rows, SSD_WIDTH:SSD_WIDTH + GLA_WIDTH]
        mix_s[rows, SSD_WIDTH:SSD_WIDTH + GLA_WIDTH] = y_gla.astype(BF16)
        mix_s[rows, SSD_WIDTH + GLA_WIDTH:] = (oswa_ref[t] * gates_ref[rows, SSD_WIDTH + GLA_WIDTH:]).astype(BF16)
    y_ref[...] = _epilogue(h_ref[...], mix_s[...], p_ref[0], wout_ref, wpg_ref, wpe_ref)


def _const_spec(shape):
    nd = len(shape)
    return pl.BlockSpec(shape, lambda *_: (0,) * nd)


def _resident_spec(shape):
    nd = len(shape)
    return pl.BlockSpec(shape, lambda *_: (0,) * nd, pipeline_mode=pl.Buffered(1))


def _smem_spec():
    return pl.BlockSpec(memory_space=pltpu.SMEM)


def _layer_weights(i, w):
    ops = [w["norm_w"][i], w["w_in"][i], w["conv_w"][i], w["conv_b"][i], w["dt_bias"][i], w["a_log"][i],
           w["d_skip"][i], w["ssd_norm_w"][i], w["gla_w_gk"][i], w["gla_b_gk"][i], w["gla_norm_w"][i],
           w["q_norm_w"][i], w["k_norm_w"][i], w["w_out"][i], w["w_pe"][i], w["w_pg"][i]]
    return ops, [_resident_spec(o.shape) if o.dtype == BF16 else _const_spec(o.shape) for o in ops]


def _prompt_layer(layer, depth, bsz, h, p_all, prev_states, bucket, rel, sinks, wops, wspecs):
    rows_total, _ = h.shape
    seq_len = rows_total // bsz
    chunk = PROMPT_CHUNK_ROWS
    pair = 2 * chunk
    chunks_per_seq = seq_len // chunk
    assert seq_len % pair == 0 and chunks_per_seq & (chunks_per_seq - 1) == 0
    n_pairs = rows_total // pair
    kern = functools.partial(_prompt_kernel, chunks_per_seq, len(prev_states))
    proj_rows = pl.BlockSpec((pair, D_MODEL), lambda k: (jnp.minimum(k, n_pairs - 1), 0))
    out_rows = pl.BlockSpec((pair, D_MODEL), lambda k: (jnp.maximum(k - 1, 0), 0))
    p_spec = pl.BlockSpec((1, pair, PLE_DIM), lambda k: (layer, jnp.maximum(k - 1, 0), 0))
    per_seq = lambda s: pl.BlockSpec(
        (1, 1) + s, lambda k: (layer, jnp.maximum(2 * k - 1, 0) // chunks_per_seq) + (0,) * len(s))
    state_shapes = ((SSD_WIDTH, SSD_STATE), (SSD_CONV - 1, SSD_CONV_DIM), (GLA_HEADS * GLA_DK, GLA_DV),
                    (LANES, WINDOW), (LANES, WINDOW))
    out_shape = (jax.ShapeDtypeStruct((rows_total, D_MODEL), F32),) + tuple(
        jax.ShapeDtypeStruct((depth, bsz) + s, F32) for s in state_shapes)
    return pl.pallas_call(
        kern,
        grid=(n_pairs + 1,),
        in_specs=[proj_rows, out_rows, p_spec, _const_spec(bucket.shape), _smem_spec(), _smem_spec()]
        + wspecs + [pl.BlockSpec(memory_space=pl.ANY)] * len(prev_states),
        out_specs=(out_rows,) + tuple(per_seq(s) for s in state_shapes),
        out_shape=out_shape,
        input_output_aliases={N_PROMPT_INPUTS + k: 1 + k for k in range(len(prev_states))},
        scratch_shapes=[
            pltpu.VMEM((chunk, PROJ_W), F32), pltpu.VMEM((chunk, PROJ_W), F32),
            pltpu.VMEM((chunk, XBC_W), F32), pltpu.VMEM((chunk, XBC_W), F32),
            pltpu.VMEM((chunk, D_MODEL), BF16), pltpu.VMEM((chunk, D_MODEL), BF16),
            pltpu.VMEM((chunk, D_MODEL), BF16),
            pltpu.VMEM((chunk, D_MODEL), F32),
            pltpu.VMEM((SUBLANES, XBC_W), F32),
            pltpu.VMEM((BLK, SSD_WIDTH), F32),
            pltpu.VMEM((LANES, GLA_WIDTH), F32),
            pltpu.VMEM((2 * BLK, LANES), F32),
            pltpu.VMEM((2 * BLK, LANES), F32),
            pltpu.VMEM((2 * SWA_HEADS, BLK, 2 * BLK), F32),
        ],
        compiler_params=pltpu.CompilerParams(
            dimension_semantics=("arbitrary",), vmem_limit_bytes=VMEM_LIMIT_BYTES),
        name="prompt_layer",
    )(h, h, p_all, bucket, rel, sinks, *wops, *prev_states)


def _sample_layer(h, p, ssm, conv_pad, gla, kc, vc, bucket, rel, sinks, wops, wspecs, seq):
    rows_total = h.shape[0]
    n_seq = BLK // seq
    rows = n_seq * seq
    bsz = rows_total // seq
    kern = functools.partial(_sample_kernel, seq, n_seq)
    row_spec = lambda w: pl.BlockSpec((rows, w), lambda s: (s, 0))
    seq_spec = lambda a, c: pl.BlockSpec((n_seq, a, c), lambda s: (s, 0, 0))
    out_shape = (
        jax.ShapeDtypeStruct((rows_total, D_MODEL), F32),
        jax.ShapeDtypeStruct((bsz, SSD_WIDTH, SSD_STATE), F32),
        jax.ShapeDtypeStruct((rows_total, SSD_CONV_DIM), F32),
        jax.ShapeDtypeStruct((bsz, GLA_HEADS * GLA_DK, GLA_DV), F32),
        jax.ShapeDtypeStruct((bsz, WINDOW, LANES), F32),
        jax.ShapeDtypeStruct((bsz, WINDOW, LANES), F32),
    )
    n_keys = WINDOW + seq
    f32_scratch = lambda r, c: pltpu.VMEM((r, c), F32)
    return pl.pallas_call(
        kern,
        grid=(rows_total // rows,),
        in_specs=[row_spec(D_MODEL), row_spec(PLE_DIM), seq_spec(SSD_WIDTH, SSD_STATE), row_spec(SSD_CONV_DIM),
                  seq_spec(GLA_HEADS * GLA_DK, GLA_DV), seq_spec(WINDOW, LANES), seq_spec(WINDOW, LANES),
                  _const_spec(bucket.shape), _smem_spec(), _smem_spec()] + wspecs,
        out_specs=(row_spec(D_MODEL), seq_spec(SSD_WIDTH, SSD_STATE), row_spec(SSD_CONV_DIM),
                   seq_spec(GLA_HEADS * GLA_DK, GLA_DV), seq_spec(WINDOW, LANES), seq_spec(WINDOW, LANES)),
        out_shape=out_shape,
        scratch_shapes=[
            f32_scratch(rows, PROJ_W),
            f32_scratch(rows + SUBLANES, XBC_W),
            f32_scratch(rows + SUBLANES, XBC_W),
            f32_scratch(rows, LANES), f32_scratch(rows, LANES), f32_scratch(rows, SSD_WIDTH),
            f32_scratch(rows, LANES), f32_scratch(rows, SSD_WIDTH),
            f32_scratch(rows, LANES), f32_scratch(rows, LANES), f32_scratch(rows, LANES),
            f32_scratch(rows, GLA_WIDTH),
            f32_scratch(rows, SWA_WIDTH), f32_scratch(rows, LANES), f32_scratch(rows, LANES),
            f32_scratch(rows, SWA_WIDTH),
            f32_scratch(n_keys, LANES), f32_scratch(n_keys, LANES),
            f32_scratch(n_keys, SWA_HEADS * seq),
        ],
        compiler_params=pltpu.CompilerParams(
            dimension_semantics=("arbitrary",), vmem_limit_bytes=VMEM_LIMIT_BYTES),
        name="sample_layer",
    )(h, p, ssm, conv_pad, gla, kc, vc, bucket, rel, sinks, *wops)


def _whole(shape, layer=None):
    if layer is None:
        return pl.BlockSpec(shape, lambda *_: (0,) * len(shape), pipeline_mode=pl.Buffered(1))
    return pl.BlockSpec((1,) + shape[1:], lambda *_: (layer,) + (0,) * (len(shape) - 1),
                        pipeline_mode=pl.Buffered(1))


def _sample_layer_native(layer, depth, seq, h, p_all, conv_in, ssm_in, gla_in, kc_in, vc_in, prev_states,
                         buckets, rel, sinks, wops):
    (nw, win, cw, cb, dtb, alog, dsk, snw, wgk, bgk, gnw, qnw, knw, wout, wpe, wpg) = wops
    rows = h.shape[0]
    n_seq = rows // seq
    assert n_seq == LANES and n_seq % SEQ_PER_STEP == 0 and SSD_HEADS * SEQ_PER_STEP == n_seq
    prev_conv, prev_rest = (prev_states[:1], prev_states[1:]) if prev_states else ((), ())
    f32 = lambda *s: jax.ShapeDtypeStruct(s, F32)
    cparams = lambda sem: pltpu.CompilerParams(dimension_semantics=sem, vmem_limit_bytes=VMEM_LIMIT_BYTES)

    front_in = [h, conv_in, nw, win, cw, cb, dtb, alog, wgk, bgk, qnw, knw]
    front_specs = [_whole(h.shape), _whole(conv_in.shape, layer)] + [_whole(a.shape) for a in front_in[2:]]
    front_out = (f32(rows, SSD_WIDTH), f32(rows, D_MODEL),
                 f32(seq, SSD_WIDTH, n_seq), f32(seq, LANES, n_seq), f32(seq, LANES, n_seq),
                 f32(seq, SSD_HEADS, n_seq), f32(seq, LANES, n_seq), f32(seq, LANES, n_seq), f32(seq, LANES, n_seq),
                 f32(seq, GLA_WIDTH, n_seq), f32(rows, SWA_WIDTH), f32(rows, LANES), f32(rows, LANES),
                 f32(*conv_in.shape))
    (xs, gates, xt, bt, ct, at, qt, kt, egt, vt, qn, kn, vn, conv_o) = pl.pallas_call(
        functools.partial(_sample_front_kernel, seq, len(prev_conv)),
        grid=(1,),
        in_specs=front_specs + [pl.BlockSpec(memory_space=pl.ANY)] * len(prev_conv),
        out_specs=tuple(_whole(o.shape) for o in front_out[:-1]) + (_whole(conv_in.shape, layer),),
        out_shape=front_out,
        input_output_aliases={len(front_in) + k: len(front_out) - 1 + k for k in range(len(prev_conv))},
        scratch_shapes=[pltpu.VMEM((rows, D_MODEL), BF16), pltpu.VMEM((rows, XBC_W), F32),
                        pltpu.VMEM((rows, PROJ_W), F32)],
        compiler_params=cparams(("arbitrary",)),
        name="sample_front",
    )(*front_in, *prev_conv)

    n_steps = SSD_HEADS
    per_group = SSD_HEADS // SSD_GROUPS
    gla_head = lambda j: jnp.minimum(j, GLA_HEADS - 1)
    blk3 = lambda n, f: pl.BlockSpec((seq, n, n_seq), lambda j: (0, f(j), 0))
    state_in = [xt, bt, ct, at, qt, kt, egt, vt, qn, kn, vn, ssm_in, gla_in, kc_in, vc_in, buckets[0], buckets[1],
                rel, sinks]
    ssm_spec = pl.BlockSpec((1, 1) + ssm_in.shape[2:], lambda j: (layer, j, 0, 0))
    gla_spec = pl.BlockSpec((1, 1) + gla_in.shape[2:], lambda j: (layer, gla_head(j), 0, 0))
    kv_spec = pl.BlockSpec((1, SEQ_PER_STEP) + kc_in.shape[2:], lambda j: (layer, j, 0, 0))
    state_specs = [blk3(SSD_HEAD_DIM, lambda j: j), blk3(SSD_STATE, lambda j: j // per_group),
                   blk3(SSD_STATE, lambda j: j // per_group), _const_spec(at.shape),
                   blk3(GLA_DK, gla_head), blk3(GLA_DK, gla_head), blk3(GLA_DK, gla_head), blk3(GLA_DV, gla_head),
                   _const_spec(qn.shape), _const_spec(kn.shape), _const_spec(vn.shape),
                   ssm_spec, gla_spec, kv_spec, kv_spec,
                   _const_spec(buckets[0].shape), _const_spec(buckets[1].shape), _smem_spec(), _smem_spec()]
    state_out = (f32(*ssm_in.shape), f32(*gla_in.shape), f32(*kc_in.shape), f32(*vc_in.shape),
                 f32(seq, SSD_WIDTH, n_seq), f32(seq, GLA_WIDTH, n_seq), f32(seq, n_seq, SWA_WIDTH))
    ssm_o, gla_o, ko, vo, yt, ot, oswa = pl.pallas_call(
        functools.partial(_sample_state_kernel, seq, len(prev_rest)),
        grid=(n_steps,),
        in_specs=state_specs + [pl.BlockSpec(memory_space=pl.ANY)] * len(prev_rest),
        out_specs=(ssm_spec, gla_spec, kv_spec, kv_spec, blk3(SSD_HEAD_DIM, lambda j: j), blk3(GLA_DV, gla_head),
                   pl.BlockSpec((seq, SEQ_PER_STEP, SWA_WIDTH), lambda j: (0, j, 0))),
        out_shape=state_out,
        input_output_aliases={len(state_in) + k: k for k in range(len(prev_rest))},
        scratch_shapes=[pltpu.VMEM((SEQ_PER_STEP * seq, LANES), F32)] * 6
        + [pltpu.VMEM((SWA_HEADS * seq, LANES), F32)] * 2,
        compiler_params=cparams(("arbitrary",)),
        name="sample_state",
    )(*state_in, *prev_rest)

    back_in = [yt, ot, oswa, xs, gates, h, p_all, dsk, snw, gnw, wout, wpe, wpg]
    back_specs = [_whole(a.shape) for a in back_in[:6]] + [_whole(p_all.shape, layer)] + [
        _whole(a.shape) for a in back_in[7:]]
    y = pl.pallas_call(
        functools.partial(_sample_back_kernel, seq),
        grid=(1,),
        in_specs=back_specs,
        out_specs=_whole((rows, D_MODEL)),
        out_shape=f32(rows, D_MODEL),
        scratch_shapes=[pltpu.VMEM((rows, D_MODEL), BF16)],
        compiler_params=cparams(("arbitrary",)),
        name="sample_back",
    )(*back_in)
    return y, (conv_o, ssm_o, gla_o, ko, vo)


SWA_HEAD_ORDER = (0, 2, 1, 3)


def _win_tile_runs():
    sizes = (SSD_WIDTH, SSD_CONV_DIM, SSD_HEADS, GLA_HEADS * GLA_DK, GLA_HEADS * GLA_DK, GLA_WIDTH, GLA_WIDTH,
             GLA_RANK, SWA_WIDTH, SWA_KV_HEADS * SWA_HEAD_DIM, SWA_KV_HEADS * SWA_HEAD_DIM, SWA_WIDTH)
    offs = np.concatenate([[0], np.cumsum(sizes)])
    seg = lambda k: np.arange(offs[k], offs[k + 1])
    z, xbc, dt, gq, gk, gv, gg, glr, sq, sk, sv, sg = [seg(k) for k in range(len(sizes))]
    heads = lambda a: np.concatenate([a[h * SWA_HEAD_DIM:(h + 1) * SWA_HEAD_DIM] for h in SWA_HEAD_ORDER])
    pad = np.full(LANES - SSD_HEADS - GLA_RANK, -1)
    src = np.concatenate([xbc, z, gq, gk, gv, gg, heads(sq), sk, sv, heads(sg), dt, glr, pad])
    assert src.size == XBC_W + PROJ_W
    tiles = []
    for j in range(src.size // LANES):
        idx = src[j * LANES:(j + 1) * LANES]
        cuts = [0] + [k for k in range(1, LANES) if (idx[k] != idx[k - 1] + 1 and not (idx[k] == -1 == idx[k - 1]))]
        runs = [(int(idx[a]), b - a) for a, b in zip(cuts, cuts[1:] + [LANES])]
        assert all(n % SUBLANES == 0 and (s < 0 or s % SUBLANES == 0) for s, n in runs)
        tiles.append(runs)
    return tiles


def _win_prep_kernel(tile_runs, wt_ref, out_ref):
    for j, runs in enumerate(tile_runs):
        parts = [jnp.zeros((n, D_MODEL), F32) if s < 0 else wt_ref[0, s:s + n, :] for s, n in runs]
        tile = parts[0] if len(parts) == 1 else jnp.concatenate(parts, axis=0)
        out_ref[0, :, j * LANES:(j + 1) * LANES] = tile.T.astype(BF16)


def _prepare_w_in(w_in):
    depth, d_model, d_in = w_in.shape
    w_t = jnp.swapaxes(w_in, 1, 2)
    return pl.pallas_call(
        functools.partial(_win_prep_kernel, _win_tile_runs()),
        grid=(depth,),
        in_specs=[pl.BlockSpec((1, d_in, d_model), lambda l: (l, 0, 0))],
        out_specs=pl.BlockSpec((1, d_model, XBC_W + PROJ_W), lambda l: (l, 0, 0)),
        out_shape=jax.ShapeDtypeStruct((depth, d_model, XBC_W + PROJ_W), BF16),
        compiler_params=pltpu.CompilerParams(
            dimension_semantics=("arbitrary",), vmem_limit_bytes=VMEM_LIMIT_BYTES),
        name="w_in_prep",
    )(w_t)


def _prepare_weights(norm_w, w_in, conv_w, conv_b, dt_bias, a_log, d_skip, ssd_norm_w, gla_w_gk, gla_b_gk,
                     gla_norm_w, q_norm_w, k_norm_w, w_out, w_pe, w_pg):
    mix_w = SSD_WIDTH + GLA_WIDTH
    w_out_p = jnp.concatenate(
        [w_out[:, :mix_w, :]] + [w_out[:, mix_w + h * SWA_HEAD_DIM:mix_w + (h + 1) * SWA_HEAD_DIM, :]
                                 for h in SWA_HEAD_ORDER], axis=1).astype(BF16)
    lane_pad = lambda x: jnp.pad(x, ((0, 0), (0, LANES - x.shape[-1])))[:, None, :]
    wgk_p = jnp.pad(gla_w_gk, ((0, 0), (LR_LANE0, LANES - LR_LANE0 - GLA_RANK), (0, 0))).astype(BF16)
    return dict(
        norm_w=norm_w[:, None, :], w_in=_prepare_w_in(w_in), conv_w=conv_w, conv_b=conv_b[:, None, :],
        dt_bias=lane_pad(dt_bias), a_log=lane_pad(a_log),
        d_skip=jnp.repeat(d_skip, SSD_HEAD_DIM, axis=-1)[:, None, :], ssd_norm_w=ssd_norm_w[:, None, :],
        gla_w_gk=wgk_p, gla_b_gk=gla_b_gk[:, None, :],
        gla_norm_w=jnp.tile(gla_norm_w, (1, GLA_HEADS))[:, None, :],
        q_norm_w=jnp.tile(q_norm_w, (1, SWA_HEADS))[:, None, :],
        k_norm_w=jnp.tile(k_norm_w, (1, SWA_KV_HEADS))[:, None, :],
        w_out=w_out_p, w_pe=w_pe.astype(BF16), w_pg=w_pg.astype(BF16))


def kernel(x_prompt, x_sample, state_ssm, state_conv, state_gla, cache_swa_k, cache_swa_v, p_prompt, p_sample, rel_bias, norm_w, w_in, conv_w, conv_b, dt_bias, a_log, d_skip, ssd_norm_w, gla_w_gk, gla_b_gk, gla_norm_w, q_norm_w, k_norm_w, attn_sinks, w_out, w_pe, w_pg):
    depth = w_in.shape[0]
    bp, seq_p, _ = x_prompt.shape
    bs, seq_s, _ = x_sample.shape
    assert seq_s == SUBLANES and BLK % seq_s == 0 and (bs * seq_s) % BLK == 0
    assert cache_swa_k.shape[2] == WINDOW
    w = _prepare_weights(norm_w, w_in, conv_w, conv_b, dt_bias, a_log, d_skip, ssd_norm_w, gla_w_gk, gla_b_gk,
                         gla_norm_w, q_norm_w, k_norm_w, w_out, w_pe, w_pg)
    rel_flat = rel_bias.reshape(-1)
    dist_p = WINDOW + np.arange(BLK)[:, None] - np.arange(2 * BLK)[None, :]
    bucket_p = jnp.asarray(_bucket_table(dist_p))
    t_of_row = np.tile(np.arange(seq_s), SWA_HEADS)[:, None]
    bucket_c = jnp.asarray(_bucket_table(WINDOW + t_of_row - np.arange(WINDOW)[None, :]))
    dist_n = np.where(np.arange(LANES)[None, :] < seq_s, t_of_row - np.arange(LANES)[None, :], -1)
    bucket_n = jnp.asarray(_bucket_table(dist_n))

    ssm_in = jnp.transpose(state_ssm, (0, 2, 3, 4, 1)).reshape(depth, SSD_HEADS, SSD_HEAD_DIM * SSD_STATE, bs)
    gla_in = jnp.transpose(state_gla, (0, 2, 3, 4, 1)).reshape(depth, GLA_HEADS, GLA_DK * GLA_DV, bs)
    kv_in = lambda a: jnp.transpose(a, (0, 1, 3, 4, 2)).reshape(depth, bs, SWA_KV_HEADS * SWA_HEAD_DIM, WINDOW)
    kc_in, vc_in = kv_in(cache_swa_k), kv_in(cache_swa_v)
    conv_in = jnp.transpose(state_conv, (0, 2, 1, 3))

    hp = x_prompt.reshape(bp * seq_p, D_MODEL)
    p_prompt_rows = p_prompt.reshape(depth, bp * seq_p, PLE_DIM)
    hs = jnp.transpose(x_sample, (1, 0, 2)).reshape(seq_s * bs, D_MODEL)
    p_sample_rows = jnp.transpose(p_sample, (0, 2, 1, 3)).reshape(depth, seq_s * bs, PLE_DIM)
    states_p = ()
    states_s = ()
    for i in range(depth):
        wops, wspecs = _layer_weights(i, w)
        hp, *states_p = _prompt_layer(i, depth, bp, hp, p_prompt_rows, tuple(states_p), bucket_p, rel_flat,
                                      attn_sinks[i], wops, wspecs)
        hs, states_s = _sample_layer_native(i, depth, seq_s, hs, p_sample_rows, conv_in, ssm_in, gla_in, kc_in,
                                            vc_in, states_s, (bucket_c, bucket_n), rel_flat, attn_sinks[i], wops)
    ssm_p, conv_p, gla_p, kt_p, vt_p = states_p
    conv_s, ssm_s, gla_s, kt_s, vt_s = states_s
    unpack_kv = lambda a: jnp.transpose(
        a.reshape(a.shape[:2] + (SWA_KV_HEADS, SWA_HEAD_DIM, WINDOW)), (0, 1, 4, 2, 3))
    outs_p = (ssm_p.reshape(depth, bp, SSD_HEADS, SSD_HEAD_DIM, SSD_STATE), conv_p,
              gla_p.reshape(depth, bp, GLA_HEADS, GLA_DK, GLA_DV), unpack_kv(kt_p), unpack_kv(vt_p))
    seq_last = lambda a, dims: jnp.transpose(a.reshape(a.shape[:2] + dims + (bs,)), (0, 4, 1, 2, 3))
    outs_s = (seq_last(ssm_s, (SSD_HEAD_DIM, SSD_STATE)), jnp.transpose(conv_s, (0, 2, 1, 3)),
              seq_last(gla_s, (GLA_DK, GLA_DV)), unpack_kv(kt_s), unpack_kv(vt_s))
    y_sample = jnp.transpose(hs.reshape(seq_s, bs, D_MODEL), (1, 0, 2))
    return (hp.reshape(bp, seq_p, D_MODEL), y_sample) + outs_p + outs_s
```

```python
import functools
import math

import numpy as np
import jax
import jax.numpy as jnp
from jax import lax
from jax.experimental import pallas as pl
from jax.experimental.pallas import tpu as pltpu

D_MODEL = 1024
DEPTH = 2
SSD_HEADS = 8
SSD_HEAD_DIM = 64
SSD_WIDTH = SSD_HEADS * SSD_HEAD_DIM
SSD_GROUPS = 2
SSD_STATE = 64
SSD_CONV = 4
SSD_CONV_DIM = SSD_WIDTH + 2 * SSD_GROUPS * SSD_STATE
SSD_CHUNK = 128
GLA_HEADS = 4
GLA_DK = 32
GLA_DV = 64
GLA_WIDTH = GLA_HEADS * GLA_DV
GLA_RANK = 16
GLA_GATE_NORM = 16.0
GLA_CHUNK = 64
SWA_HEADS = 4
SWA_KV_HEADS = 2
SWA_HEAD_DIM = 64
SWA_WIDTH = SWA_HEADS * SWA_HEAD_DIM
WINDOW = 128
REL_BUCKETS = 32
REL_MAX_DIST = 128
PLE_DIM = 256
EPS = 1e-6

LANES = 128
SUBLANES = 8
HALF = LANES // 2
BLK = 128
VMEM_LIMIT_BYTES = 56 * 1024 * 1024

XBC_W = SSD_CONV_DIM
P_Z = 0
P_GQ = P_Z + SSD_WIDTH
P_GK = P_GQ + LANES
P_GV = P_GK + LANES
P_GG = P_GV + GLA_WIDTH
P_SQ = P_GG + GLA_WIDTH
P_SK = P_SQ + SWA_WIDTH
P_SV = P_SK + LANES
P_SG = P_SV + LANES
P_DTLR = P_SG + SWA_WIDTH
PROJ_W = P_DTLR + LANES
LR_LANE0 = SSD_HEADS

F32 = jnp.float32
BF16 = jnp.bfloat16
NEG_INF = float("-inf")
N_PROMPT_INPUTS = 22
PROMPT_CHUNK_ROWS = 2 * BLK
TICKS_PER_ITEM = 2
NT_DIMS = (((1,), (1,)), ((), ()))


def _iota(shape, dim):
    return lax.broadcasted_iota(jnp.int32, shape, dim)


def _div(x, d):
    return x >> (d.bit_length() - 1)


def _mod(x, d):
    return x & (d - 1)


def _softplus(x):
    e = jnp.exp(-jnp.abs(x))
    u = 1.0 + e
    d = u - 1.0
    log1p_e = jnp.where(d == 0.0, e, jnp.log(u) * (e / jnp.where(d == 0.0, 1.0, d)))
    return jnp.maximum(x, 0.0) + log1p_e


def _log_sigmoid(x):
    return jnp.minimum(x, 0.0) - jnp.log(1.0 + jnp.exp(-jnp.abs(x)))


def _silu(x):
    return x * jax.nn.sigmoid(x)


def _dot(a, b):
    return jnp.dot(a.astype(BF16), b.astype(BF16), preferred_element_type=F32)


def _dot_nt(a, b):
    return lax.dot_general(a.astype(BF16), b.astype(BF16), NT_DIMS, preferred_element_type=F32)


def _dot_exact(sel, x):
    x1 = x.astype(BF16)
    r1 = x - x1.astype(F32)
    x2 = r1.astype(BF16)
    x3 = (r1 - x2.astype(F32)).astype(BF16)
    dot = functools.partial(jnp.dot, sel, preferred_element_type=F32)
    return dot(x1) + dot(x2) + dot(x3)


def _expand_heads(x, n_heads):
    rows = x.shape[0]
    lo = _iota((rows, LANES), 1) < HALF
    tiles = []
    for j in range(n_heads // 2):
        a = jnp.broadcast_to(x[:, 2 * j:2 * j + 1], (rows, LANES))
        b = jnp.broadcast_to(x[:, 2 * j + 1:2 * j + 2], (rows, LANES))
        tiles.append(jnp.where(lo, a, b))
    return jnp.concatenate(tiles, axis=1)


def _head_rms_scale(x):
    rows, width = x.shape
    lo = _iota((rows, LANES), 1) < HALF
    outs = []
    for j in range(width // LANES):
        t = x[:, j * LANES:(j + 1) * LANES]
        sq = t * t
        s_lo = jnp.sum(jnp.where(lo, sq, 0.0), axis=-1, keepdims=True)
        s_hi = jnp.sum(jnp.where(lo, 0.0, sq), axis=-1, keepdims=True)
        outs.append(lax.rsqrt(jnp.where(lo, s_lo, s_hi) * (1.0 / HALF) + EPS))
    return outs[0] if len(outs) == 1 else jnp.concatenate(outs, axis=1)


def _group_rmsnorm(y, w):
    gw = SSD_WIDTH // SSD_GROUPS
    outs = []
    for g in range(SSD_GROUPS):
        t = y[:, g * gw:(g + 1) * gw]
        ms = jnp.sum(t * t, axis=-1, keepdims=True) * (1.0 / gw)
        outs.append(t * lax.rsqrt(ms + EPS))
    return jnp.concatenate(outs, axis=1) * w


def _rel_bucket_np(dist):
    n = np.maximum(dist, 0)
    exact = REL_BUCKETS // 2
    nf = np.maximum(n, 1).astype(np.float64)
    large = exact + (np.log(nf / exact) / math.log(REL_MAX_DIST / exact) * (REL_BUCKETS - exact)).astype(np.int32)
    large = np.minimum(large, REL_BUCKETS - 1)
    return np.where(n < exact, n, large).astype(np.int32)


def _bucket_table(dist):
    return np.where((dist >= 0) & (dist < WINDOW), _rel_bucket_np(dist), -1).astype(np.int32)


def _no_tick():
    pass


def _ssd_intra(xbc_c, dtlr, dtb, a_row, pair_mask, tri, total_of, tick=_no_tick):
    xs = xbc_c[:, :SSD_WIDTH]
    bm = xbc_c[:, SSD_WIDTH:SSD_WIDTH + LANES]
    cm = xbc_c[:, SSD_WIDTH + LANES:]
    lane = _iota((BLK, LANES), 1)
    lo = lane < HALF
    dtv = _softplus(dtlr + dtb)
    adt = dtv * a_row
    acum = _dot_exact(tri, adt)
    tick()
    acum_t = acum.T
    eacum = jnp.exp(acum)
    tail = jnp.exp(total_of(adt, acum) - acum)
    dtv_e = _expand_heads(dtv, SSD_HEADS)
    eacum_e = _expand_heads(eacum, SSD_HEADS)
    tail_e = _expand_heads(tail, SSD_HEADS)
    tick()
    xdt = xs * dtv_e
    xw = xdt * tail_e
    cb = [_dot_nt(jnp.where(lo, cm, 0.0), bm), _dot_nt(jnp.where(lo, 0.0, cm), bm)]
    y_pairs = []
    for j in range(SSD_HEADS // 2):
        tick()
        g = (2 * j) // (SSD_HEADS // SSD_GROUPS)
        ms = []
        for k in range(2):
            h = 2 * j + k
            seg = acum[:, h:h + 1] - acum_t[h:h + 1, :]
            dec = jnp.where(pair_mask, jnp.exp(seg), 0.0)
            ms.append((cb[g] * dec).astype(BF16))
        xp = xdt[:, j * LANES:(j + 1) * LANES]
        rhs = jnp.concatenate([jnp.where(lo, xp, 0.0), jnp.where(lo, 0.0, xp)], axis=0)
        y_pairs.append(_dot(jnp.concatenate(ms, axis=1), rhs))
    y_intra = jnp.concatenate(y_pairs, axis=1)
    return y_intra, xs, bm, cm, xw, eacum, eacum_e


def _gla_intra(gq, gk, gv, glog, tri, total_of, att_masks, tick=_no_tick):
    bcs = _dot_exact(tri, glog)
    tick()
    eb = jnp.exp(bcs)
    qe = gq * (GLA_DK ** -0.5) * eb
    ke = gk * jnp.exp(-bcs)
    btot = total_of(glog, bcs)
    kd = gk * jnp.exp(btot - bcs)
    lane_k = _iota((GLA_CHUNK, LANES), 1)
    lane_v = _iota((GLA_CHUNK, GLA_WIDTH), 1)
    outs = []
    for c2 in range(BLK // GLA_CHUNK):
        tick()
        rs = slice(c2 * GLA_CHUNK, (c2 + 1) * GLA_CHUNK)
        ke_c = ke[rs]
        v_c = gv[rs]
        kbd = jnp.concatenate(
            [jnp.where(_div(lane_k, GLA_DK) == h, ke_c, 0.0) for h in range(GLA_HEADS)], axis=0)
        att = _dot_nt(qe[rs], kbd)
        att = jnp.where(att_masks[c2], att, 0.0)
        vbd = jnp.concatenate(
            [jnp.where(_div(lane_v, GLA_DV) == h, v_c, 0.0) for h in range(GLA_HEADS)], axis=0)
        outs.append(_dot(att, vbd))
    return jnp.concatenate(outs, axis=0), qe, kd, jnp.exp(btot)


def _build_bias(bucket, rel_ref):
    accs = [jnp.full(bucket.shape, NEG_INF, F32) for _ in range(SWA_HEADS)]
    for b in range(REL_BUCKETS):
        hit = bucket == b
        for h in range(SWA_HEADS):
            accs[h] = jnp.where(hit, rel_ref[b * SWA_HEADS + h], accs[h])
    return accs


def _epilogue(h, mix, p, wout_ref, wpg_ref, wpe_ref):
    h1 = h + jnp.dot(mix, wout_ref[...], preferred_element_type=F32)
    gate = jax.nn.sigmoid(jnp.dot(h1.astype(BF16), wpg_ref[...], preferred_element_type=F32))
    pe = jnp.dot(p.astype(BF16), wpe_ref[...], preferred_element_type=F32)
    return h1 + gate * pe


def _prompt_kernel(chunks_per_seq, n_aliased, *refs):
    (ha_ref, hc_ref, p_ref, bucket_ref, rel_ref, sink_ref, nw_ref, win_ref, cw_ref, cb_ref, dtb_ref,
     alog_ref, dsk_ref, snw_ref, wgk_ref, bgk_ref, gnw_ref, qnw_ref, knw_ref, wout_ref,
     wpe_ref, wpg_ref) = refs[:N_PROMPT_INPUTS]
    (y_ref, ssm_ref, conv_ref, gla_ref, ko_ref, vo_ref,
     proj_e, proj_o, xbc_e, xbc_o, mix_e, mix_o, u_s, h1_s, h1b_s, hist_s, st_s, s2_s, kext_s, vext_s,
     bias_s) = refs[N_PROMPT_INPUTS + n_aliased:]
    k_idx = pl.program_id(0)

    @pl.when(k_idx == 0)
    def _():
        accs = _build_bias(bucket_ref[...], rel_ref)
        own_block = _iota((BLK, 2 * BLK), 1) >= BLK
        for hh in range(SWA_HEADS):
            bias_s[hh] = accs[hh]
            bias_s[SWA_HEADS + hh] = jnp.where(own_block, accs[hh], NEG_INF)
        for ref in (proj_o, xbc_o, mix_e, mix_o, hist_s, st_s, s2_s, kext_s, vext_s):
            ref[...] = jnp.zeros(ref.shape, ref.dtype)

    row = _iota((BLK, BLK), 0)
    col = _iota((BLK, BLK), 1)
    causal = row >= col
    tri_ssd = jnp.where(causal, 1.0, 0.0).astype(BF16)
    tri_gla = jnp.where(causal & (_div(row, GLA_CHUNK) == _div(col, GLA_CHUNK)), 1.0, 0.0).astype(BF16)
    lo = col < HALF
    lane_row = _iota((1, LANES), 1)
    a_row = jnp.where(lane_row < SSD_HEADS, -jnp.exp(alog_ref[...]), 0.0)
    st_mask = (_iota((BLK, SSD_WIDTH), 0) < SSD_STATE) == (_iota((BLK, SSD_WIDTH), 1) < SSD_WIDTH // SSD_GROUPS)
    bd_mask = _div(_iota((LANES, GLA_WIDTH), 0), GLA_DK) == _div(_iota((LANES, GLA_WIDTH), 1), GLA_DV)
    att_t = _iota((GLA_CHUNK, GLA_WIDTH), 0)
    att_s = _mod(_iota((GLA_CHUNK, GLA_WIDTH), 1), GLA_CHUNK)
    att_mask = att_s <= att_t
    lo2 = _iota((2 * BLK, LANES), 1) < HALF

    def ssd_total(adt, acum):
        return jnp.broadcast_to(acum[BLK - 1:BLK, :], acum.shape)

    def gla_total(glog, bcs):
        return jnp.concatenate(
            [jnp.broadcast_to(bcs[(c2 + 1) * GLA_CHUNK - 1:(c2 + 1) * GLA_CHUNK, :], (GLA_CHUNK, LANES))
             for c2 in range(BLK // GLA_CHUNK)], axis=0)

    def block(blk, proj_s, xbc_s, mix_s, starts_sequence, tick):
        rows = slice(blk * BLK, (blk + 1) * BLK)
        cw = cw_ref[...]
        tick()
        if blk == 0:
            xwin = jnp.concatenate([hist_s[...], xbc_s[0:BLK, :]], axis=0)
        else:
            xwin = xbc_s[blk * BLK - SUBLANES:(blk + 1) * BLK, :]
        acc = xwin[SUBLANES - 3:SUBLANES - 3 + BLK, :] * cw[0:1, :]
        for k in range(1, SSD_CONV):
            acc = acc + xwin[SUBLANES - 3 + k:SUBLANES - 3 + k + BLK, :] * cw[k:k + 1, :]
        xbc_c = _silu(acc + cb_ref[...])
        tick()
        dtlr = proj_s[rows, P_DTLR:P_DTLR + LANES]
        y_intra, xs, bm, cm, xw, eacum, eacum_e = _ssd_intra(
            xbc_c, dtlr, dtb_ref[...], a_row, causal, tri_ssd, ssd_total, tick)
        tick()
        st = st_s[...]
        y = y_intra + _dot(cm, st) * eacum_e + dsk_ref[...] * xs
        st_s[...] = st * eacum_e[BLK - 1:BLK, :] + jnp.where(st_mask, _dot(bm.T, xw), 0.0)
        tick()
        y = y * _silu(proj_s[rows, P_Z:P_Z + SSD_WIDTH])
        mix_s[rows, 0:SSD_WIDTH] = _group_rmsnorm(y, snw_ref[...]).astype(BF16)
        tick()
        gk = proj_s[rows, P_GK:P_GK + LANES]
        gv = proj_s[rows, P_GV:P_GV + GLA_WIDTH]
        glog = _log_sigmoid(_dot(dtlr, wgk_ref[...]) + bgk_ref[...]) * (1.0 / GLA_GATE_NORM)
        o_intra, qe, kd, ebt = _gla_intra(
            proj_s[rows, P_GQ:P_GQ + LANES], gk, gv, glog, tri_gla, gla_total, [att_mask, att_mask], tick)
        kd_t = kd.T
        ebt_t = ebt.T
        s2 = s2_s[...]
        o_parts = []
        for c2 in range(BLK // GLA_CHUNK):
            tick()
            rs = slice(c2 * GLA_CHUNK, (c2 + 1) * GLA_CHUNK)
            o_parts.append(o_intra[rs] + _dot(qe[rs], s2))
            u2 = _dot(jnp.where(_div(col, GLA_CHUNK) == c2, kd_t, 0.0), gv)
            last = (c2 + 1) * GLA_CHUNK - 1
            s2 = s2 * ebt_t[:, last:last + 1] + jnp.where(bd_mask, u2, 0.0)
        s2_s[...] = s2
        o = jnp.concatenate(o_parts, axis=0)
        y_gla = o * _head_rms_scale(o) * gnw_ref[...] * _silu(proj_s[rows, P_GG:P_GG + GLA_WIDTH])
        mix_s[rows, SSD_WIDTH:SSD_WIDTH + GLA_WIDTH] = y_gla.astype(BF16)
        tick()
        sq = proj_s[rows, P_SQ:P_SQ + SWA_WIDTH]
        qn = sq * _head_rms_scale(sq) * qnw_ref[...] * (SWA_HEAD_DIM ** -0.5)
        sk = proj_s[rows, P_SK:P_SK + LANES]
        kn = sk * _head_rms_scale(sk) * knw_ref[...]
        vn = proj_s[rows, P_SV:P_SV + LANES]
        kext_s[BLK:2 * BLK, :] = kn
        vext_s[BLK:2 * BLK, :] = vn
        kext = kext_s[...]
        vext = vext_s[...]
        qa = qn[:, :LANES]
        qb = qn[:, LANES:]
        qs = jnp.concatenate([jnp.where(lo, qa, 0.0), jnp.where(lo, qb, 0.0),
                              jnp.where(lo, 0.0, qa), jnp.where(lo, 0.0, qb)], axis=0)
        logits = _dot_nt(qs, kext)
        tick()
        if blk == 0 and starts_sequence is not False:
            bias_row0 = jnp.where(starts_sequence, SWA_HEADS, 0)
        else:
            bias_row0 = 0
        es = []
        invs = []
        for hh in range(SWA_HEADS):
            tick()
            sink = sink_ref[hh]
            l = logits[hh * BLK:(hh + 1) * BLK] + bias_s[bias_row0 + hh]
            m = jnp.maximum(jnp.max(l, axis=-1, keepdims=True), sink)
            e = jnp.exp(l - m)
            den = jnp.sum(e, axis=-1, keepdims=True) + jnp.exp(sink - m)
            es.append(e.astype(BF16))
            invs.append(1.0 / den)
        v_stack = jnp.concatenate([jnp.where(lo2, vext, 0.0), jnp.where(lo2, 0.0, vext)], axis=0)
        tile_a = _dot(jnp.concatenate([es[0], es[2]], axis=1), v_stack) * jnp.where(lo, invs[0], invs[2])
        tile_b = _dot(jnp.concatenate([es[1], es[3]], axis=1), v_stack) * jnp.where(lo, invs[1], invs[3])
        oa = jnp.concatenate([tile_a, tile_b], axis=1)
        y_swa = oa * _silu(proj_s[rows, P_SG:P_SG + SWA_WIDTH])
        mix_s[rows, SSD_WIDTH + GLA_WIDTH:] = y_swa.astype(BF16)
        kext_s[0:BLK, :] = kn
        vext_s[0:BLK, :] = vn

    chunk = proj_e.shape[0]

    def project_items(rows, proj_s, xbc_s):
        def norm():
            h = ha_ref[rows, :]
            ms = jnp.mean(h * h, axis=-1, keepdims=True)
            u_s[...] = (h * lax.rsqrt(ms + EPS) * nw_ref[...]).astype(BF16)

        def cols(dst, lo_c, hi_c, w_off):
            def item():
                dst[:, lo_c:hi_c] = jnp.dot(u_s[...], win_ref[:, w_off + lo_c:w_off + hi_c],
                                            preferred_element_type=F32)
            return item

        step = 2 * LANES
        items = [norm]
        items += [cols(xbc_s, c, min(c + step, XBC_W), 0) for c in range(0, XBC_W, step)]
        items += [cols(proj_s, c, min(c + step, PROJ_W), XBC_W) for c in range(0, PROJ_W, step)]
        return items

    def epilogue_items(rows, mix_s):
        half_w = 2 * LANES

        def residual(c):
            def item():
                h1 = hc_ref[rows, c:c + half_w] + jnp.dot(
                    mix_s[...], wout_ref[:, c:c + half_w], preferred_element_type=F32)
                h1_s[:, c:c + half_w] = h1
                h1b_s[:, c:c + half_w] = h1.astype(BF16)
            return item

        def gated(c):
            def item():
                gate = jax.nn.sigmoid(jnp.dot(h1b_s[...], wpg_ref[:, c:c + half_w], preferred_element_type=F32))
                pe = jnp.dot(p_ref[0, rows, :].astype(BF16), wpe_ref[:, c:c + half_w],
                             preferred_element_type=F32)
                y_ref[rows, c:c + half_w] = h1_s[:, c:c + half_w] + gate * pe
            return item

        col0 = range(0, D_MODEL, half_w)
        return [residual(c) for c in col0] + [gated(c) for c in col0]

    def merge(first, second):
        out = list(first)
        for j, item in enumerate(second):
            out.insert(((j + 1) * len(first)) // len(second) + j, item)
        return out

    def mixer(proj_s, xbc_s, mix_s, starts_sequence, items):
        if starts_sequence is not False:
            keep = jnp.where(starts_sequence, 0.0, 1.0)
            for ref in (hist_s, st_s, s2_s):
                ref[...] = ref[...] * keep
            kext_s[0:BLK, :] = kext_s[0:BLK, :] * keep
            vext_s[0:BLK, :] = vext_s[0:BLK, :] * keep
        queue = list(items)
        calls = [0]

        def tick():
            calls[0] += 1
            if queue and calls[0] % TICKS_PER_ITEM == 0:
                queue.pop(0)()

        for blk in range(chunk // BLK):
            block(blk, proj_s, xbc_s, mix_s, starts_sequence, tick)
        while queue:
            queue.pop(0)()
        hist_s[...] = xbc_s[chunk - SUBLANES:chunk, :]

    def write_states():
        st = st_s[...]
        stc = st[:SSD_STATE] + st[SSD_STATE:]
        ssm_ref[0, 0] = jnp.concatenate([stc, stc], axis=0).T[:, :SSD_STATE]
        conv_ref[0, 0] = hist_s[SUBLANES - (SSD_CONV - 1):SUBLANES, :]
        s2 = s2_s[...]
        w = s2[:, :LANES] + s2[:, LANES:]
        gla_ref[0, 0] = w[:, :GLA_DV] + w[:, GLA_DV:]
        ko_ref[0, 0] = kext_s[0:BLK, :].T
        vo_ref[0, 0] = vext_s[0:BLK, :].T

    even = slice(0, chunk)
    odd = slice(chunk, 2 * chunk)
    mixer(proj_o, xbc_o, mix_o, False, merge(project_items(even, proj_e, xbc_e), epilogue_items(even, mix_e)))
    write_states()
    mixer(proj_e, xbc_e, mix_e, _mod(2 * k_idx, chunks_per_seq) == 0,
          merge(project_items(odd, proj_o, xbc_o), epilogue_items(odd, mix_o)))


N_FRONT_INPUTS = 12
SAMPLE_POSITIONS_PER_STEP = 2


def _sample_front_kernel(seq, n_aliased, *refs):
    (h_ref, cst_ref, nw_ref, win_ref, cw_ref, cb_ref, dtb_ref, alog_ref, wgk_ref, bgk_ref,
     qnw_ref, knw_ref) = refs[:N_FRONT_INPUTS]
    (xs_ref, gates_ref, xt_ref, bt_ref, ct_ref, at_ref, qt_ref, kt_ref, egt_ref, vt_ref,
     qn_ref, kn_ref, vn_ref, conv_ref, u_s, xbc_s, proj_s) = refs[N_FRONT_INPUTS + n_aliased:]
    n_seq = BLK
    per_step = h_ref.shape[0] // n_seq
    step = pl.program_id(0)
    ht = h_ref[...]
    ms = jnp.mean(ht * ht, axis=-1, keepdims=True)
    u_s[...] = (ht * lax.rsqrt(ms + EPS) * nw_ref[...]).astype(BF16)
    xbc_s[pl.ds(pl.multiple_of(step * per_step * n_seq, n_seq), per_step * n_seq), :] = jnp.dot(
        u_s[...], win_ref[:, :XBC_W], preferred_element_type=F32)
    proj_s[...] = jnp.dot(u_s[...], win_ref[:, XBC_W:], preferred_element_type=F32)
    cw = cw_ref[...]
    a_row = jnp.where(_iota((1, LANES), 1) < SSD_HEADS, -jnp.exp(alog_ref[...]), 0.0)
    for i in range(per_step):
        t = step * per_step + i
        rows = slice(i * n_seq, (i + 1) * n_seq)

        def raw_xbc(back):
            cur = xbc_s[pl.ds(pl.multiple_of(jnp.maximum(t - back, 0) * n_seq, n_seq), n_seq), :]
            if back == 0:
                return cur
            old = cst_ref[0, jnp.clip(SSD_CONV - 1 + t - back, 0, SSD_CONV - 2)]
            return jnp.where(t >= back, cur, old)

        acc = raw_xbc(SSD_CONV - 1) * cw[0:1, :]
        for k in range(1, SSD_CONV):
            acc = acc + raw_xbc(SSD_CONV - 1 - k) * cw[k:k + 1, :]
        xbc_c = _silu(acc + cb_ref[...])
        xs = xbc_c[:, :SSD_WIDTH]
        dtlr = proj_s[rows, P_DTLR:P_DTLR + LANES]
        dtv = _softplus(dtlr + dtb_ref[...])
        xs_ref[rows, :] = xs
        xt_ref[i] = (xs * _expand_heads(dtv, SSD_HEADS)).T
        bt_ref[i] = xbc_c[:, SSD_WIDTH:SSD_WIDTH + LANES].T
        ct_ref[i] = xbc_c[:, SSD_WIDTH + LANES:].T
        at_ref[i] = jnp.exp(dtv * a_row).T[:SSD_HEADS, :]
        glog = _log_sigmoid(_dot(dtlr, wgk_ref[...]) + bgk_ref[...]) * (1.0 / GLA_GATE_NORM)
        qt_ref[i] = (proj_s[rows, P_GQ:P_GQ + LANES] * (GLA_DK ** -0.5)).T
        kt_ref[i] = proj_s[rows, P_GK:P_GK + LANES].T
        egt_ref[i] = jnp.exp(glog).T
        vt_ref[i] = proj_s[rows, P_GV:P_GV + GLA_WIDTH].T
        sq = proj_s[rows, P_SQ:P_SQ + SWA_WIDTH]
        qn_ref[rows, :] = sq * _head_rms_scale(sq) * qnw_ref[...] * (SWA_HEAD_DIM ** -0.5)
        sk = proj_s[rows, P_SK:P_SK + LANES]
        kn_ref[rows, :] = sk * _head_rms_scale(sk) * knw_ref[...]
        vn_ref[rows, :] = proj_s[rows, P_SV:P_SV + LANES]
        gates_ref[rows, :] = jnp.concatenate(
            [_silu(proj_s[rows, P_Z:P_Z + SSD_WIDTH]), _silu(proj_s[rows, P_GG:P_GG + GLA_WIDTH]),
             _silu(proj_s[rows, P_SG:P_SG + SWA_WIDTH])], axis=1)
        first_kept = seq - (SSD_CONV - 1)

        @pl.when(t >= first_kept)
        def _():
            conv_ref[0, jnp.maximum(t - first_kept, 0)] = raw_xbc(0)


N_STATE_INPUTS = 19
SEQ_PER_STEP = 16


def _sample_state_kernel(seq, n_aliased, *refs):
    (xt_ref, bt_ref, ct_ref, at_ref, qt_ref, kt_ref, egt_ref, vt_ref, qn_ref, kn_ref, vn_ref,
     ssm_ref, gla_ref, kc_ref, vc_ref, bucket_c_ref, bucket_n_ref, rel_ref, sink_ref) = refs[:N_STATE_INPUTS]
    (ssm_o, gla_o, ko_ref, vo_ref, yt_ref, ot_ref, oswa_ref,
     qa_s, qb_s, krow_s, vrow_s, oa_s, ob_s, biasc_s, biasn_s) = refs[N_STATE_INPUTS + n_aliased:]
    j = pl.program_id(0)
    n_seq = LANES
    head_of_row = _div(_iota((SWA_HEADS * seq, LANES), 0), seq)

    def by_head(values):
        out = values[SWA_HEADS - 1]
        for hh in range(SWA_HEADS - 2, -1, -1):
            out = jnp.where(head_of_row == hh, values[hh], out)
        return out

    @pl.when(j == 0)
    def _():
        biasc_s[...] = by_head(_build_bias(bucket_c_ref[...], rel_ref))
        biasn_s[...] = by_head(_build_bias(bucket_n_ref[...], rel_ref))

    sub = _iota((SUBLANES, LANES), 0)
    a_rows = [jnp.sum(jnp.where(sub == j, at_ref[t], 0.0), axis=0, keepdims=True) for t in range(seq)]

    def ssd_body(p8, carry):
        r8 = pl.multiple_of(p8 * SUBLANES, SUBLANES)
        x_tiles = [xt_ref[t, pl.ds(r8, SUBLANES), :] for t in range(seq)]
        y_rows = [[] for _ in range(seq)]
        for pp in range(SUBLANES):
            r64 = pl.multiple_of((p8 * SUBLANES + pp) * SSD_STATE, SSD_STATE)
            slab = ssm_ref[0, 0, pl.ds(r64, SSD_STATE), :]
            for t in range(seq):
                slab = slab * a_rows[t] + x_tiles[t][pp:pp + 1, :] * bt_ref[t]
                y_rows[t].append(jnp.sum(ct_ref[t] * slab, axis=0, keepdims=True))
            ssm_o[0, 0, pl.ds(r64, SSD_STATE), :] = slab
        for t in range(seq):
            yt_ref[t, pl.ds(r8, SUBLANES), :] = jnp.concatenate(y_rows[t], axis=0)
        return carry

    lax.fori_loop(0, SSD_HEAD_DIM // SUBLANES, ssd_body, 0)

    @pl.when(j < GLA_HEADS)
    def _():
        for t in range(seq):
            ot_ref[t] = jnp.zeros((GLA_DV, LANES), F32)

        def gla_body(d8, carry):
            r8 = pl.multiple_of(d8 * SUBLANES, SUBLANES)
            q_tiles = [qt_ref[t, pl.ds(r8, SUBLANES), :] for t in range(seq)]
            k_tiles = [kt_ref[t, pl.ds(r8, SUBLANES), :] for t in range(seq)]
            g_tiles = [egt_ref[t, pl.ds(r8, SUBLANES), :] for t in range(seq)]
            for dd in range(SUBLANES):
                r64 = pl.multiple_of((d8 * SUBLANES + dd) * GLA_DV, GLA_DV)
                slab = gla_ref[0, 0, pl.ds(r64, GLA_DV), :]
                for t in range(seq):
                    slab = slab * g_tiles[t][dd:dd + 1, :] + k_tiles[t][dd:dd + 1, :] * vt_ref[t]
                    ot_ref[t] = ot_ref[t] + q_tiles[t][dd:dd + 1, :] * slab
                gla_o[0, 0, pl.ds(r64, GLA_DV), :] = slab
            return carry

        lax.fori_loop(0, GLA_DK // SUBLANES, gla_body, 0)

    base = pl.multiple_of(j * SEQ_PER_STEP, SEQ_PER_STEP)
    for t in range(seq):
        src = pl.ds(t * n_seq + base, SEQ_PER_STEP)
        dst = pl.ds(t, SEQ_PER_STEP, stride=seq)
        qa_s[dst, :] = qn_ref[src, 0:LANES]
        qb_s[dst, :] = qn_ref[src, LANES:2 * LANES]
        krow_s[dst, :] = kn_ref[src, :]
        vrow_s[dst, :] = vn_ref[src, :]
    kn_t = krow_s[...].T
    vn_t = vrow_s[...].T
    keep_old = _iota((LANES, WINDOW), 1) < WINDOW - seq
    lo8 = _iota((seq, LANES), 1) < HALF
    sink_col = by_head([jnp.full((SWA_HEADS * seq, LANES), sink_ref[hh], F32) for hh in range(SWA_HEADS)])[:, 0:1]

    def swa_stages(bl):
        r8 = pl.multiple_of(bl * seq, seq)
        v = {}

        def logits():
            qa = qa_s[pl.ds(r8, seq), :]
            qb = qb_s[pl.ds(r8, seq), :]
            qs = jnp.concatenate([jnp.where(lo8, qa, 0.0), jnp.where(lo8, qb, 0.0),
                                  jnp.where(lo8, 0.0, qa), jnp.where(lo8, 0.0, qb)], axis=0)
            v["lc"] = _dot(qs, kc_ref[0, bl]) + biasc_s[...]
            v["ln"] = _dot_nt(qs, krow_s[pl.ds(r8, seq), :]) + biasn_s[:, 0:seq]

        def softmax():
            lc, ln = v["lc"], v["ln"]
            m = jnp.maximum(jnp.maximum(jnp.max(lc, axis=-1, keepdims=True), jnp.max(ln, axis=-1, keepdims=True)),
                            sink_col)
            v["ec"] = jnp.exp(lc - m)
            v["en"] = jnp.exp(ln - m)
            v["inv"] = 1.0 / (jnp.sum(v["ec"], axis=-1, keepdims=True) + jnp.sum(v["en"], axis=-1, keepdims=True)
                              + jnp.exp(sink_col - m))

        def values():
            o = (_dot_nt(v["ec"], vc_ref[0, bl]) + _dot(v["en"], vrow_s[pl.ds(r8, seq), :])) * v["inv"]
            oa_s[pl.ds(r8, seq), :] = jnp.where(lo8, o[0:seq], o[2 * seq:3 * seq])
            ob_s[pl.ds(r8, seq), :] = jnp.where(lo8, o[seq:2 * seq], o[3 * seq:4 * seq])

        def window():
            ko_ref[0, bl] = jnp.where(keep_old, pltpu.roll(kc_ref[0, bl], WINDOW - seq, axis=1),
                                      pltpu.roll(kn_t, WINDOW - seq - r8, axis=1))
            vo_ref[0, bl] = jnp.where(keep_old, pltpu.roll(vc_ref[0, bl], WINDOW - seq, axis=1),
                                      pltpu.roll(vn_t, WINDOW - seq - r8, axis=1))

        return [logits, softmax, values, window]

    all_stages = [swa_stages(bl) for bl in range(SEQ_PER_STEP)]
    for k in range(len(all_stages[0])):
        for stages in all_stages:
            stages[k]()
    for t in range(seq):
        src = pl.ds(t, SEQ_PER_STEP, stride=seq)
        oswa_ref[t] = jnp.concatenate([oa_s[src, :], ob_s[src, :]], axis=1)


def _sample_back_kernel(yt_ref, ot_ref, oswa_ref, xs_ref, gates_ref, h_ref, p_ref, dsk_ref, snw_ref, gnw_ref,
                        wout_ref, wpe_ref, wpg_ref, y_ref, mix_s):
    n_seq = BLK
    for t in range(yt_ref.shape[0]):
        rows = slice(t * n_seq, (t + 1) * n_seq)
        y = (yt_ref[t].T + dsk_ref[...] * xs_ref[rows, :]) * gates_ref[rows, 0:SSD_WIDTH]
        mix_s[rows, 0:SSD_WIDTH] = _group_rmsnorm(y, snw_ref[...]).astype(BF16)
        o = ot_ref[t].T
        y_gla = o * _head_rms_scale(o) * gnw_ref[...] * gates_ref[rows, SSD_WIDTH:SSD_WIDTH + GLA_WIDTH]
        mix_s[rows, SSD_WIDTH:SSD_WIDTH + GLA_WIDTH] = y_gla.astype(BF16)
        mix_s[rows, SSD_WIDTH + GLA_WIDTH:] = (oswa_ref[t] * gates_ref[rows, SSD_WIDTH + GLA_WIDTH:]).astype(BF16)
    y_ref[...] = _epilogue(h_ref[...], mix_s[...], p_ref[0], wout_ref, wpg_ref, wpe_ref)


def _const_spec(shape):
    nd = len(shape)
    return pl.BlockSpec(shape, lambda *_: (0,) * nd)


def _resident_spec(shape):
    nd = len(shape)
    return pl.BlockSpec(shape, lambda *_: (0,) * nd, pipeline_mode=pl.Buffered(1))


def _smem_spec():
    return pl.BlockSpec(memory_space=pltpu.SMEM)


def _layer_weights(i, w):
    ops = [w["norm_w"][i], w["w_in"][i], w["conv_w"][i], w["conv_b"][i], w["dt_bias"][i], w["a_log"][i],
           w["d_skip"][i], w["ssd_norm_w"][i], w["gla_w_gk"][i], w["gla_b_gk"][i], w["gla_norm_w"][i],
           w["q_norm_w"][i], w["k_norm_w"][i], w["w_out"][i], w["w_pe"][i], w["w_pg"][i]]
    return ops, [_resident_spec(o.shape) if o.dtype == BF16 else _const_spec(o.shape) for o in ops]


def _prompt_layer(layer, depth, bsz, h, p_all, prev_states, bucket, rel, sinks, wops, wspecs):
    rows_total, _ = h.shape
    seq_len = rows_total // bsz
    chunk = PROMPT_CHUNK_ROWS
    pair = 2 * chunk
    chunks_per_seq = seq_len // chunk
    assert seq_len % pair == 0 and chunks_per_seq & (chunks_per_seq - 1) == 0
    n_pairs = rows_total // pair
    kern = functools.partial(_prompt_kernel, chunks_per_seq, len(prev_states))
    proj_rows = pl.BlockSpec((pair, D_MODEL), lambda k: (jnp.minimum(k, n_pairs - 1), 0))
    out_rows = pl.BlockSpec((pair, D_MODEL), lambda k: (jnp.maximum(k - 1, 0), 0))
    p_spec = pl.BlockSpec((1, pair, PLE_DIM), lambda k: (layer, jnp.maximum(k - 1, 0), 0))
    per_seq = lambda s: pl.BlockSpec(
        (1, 1) + s, lambda k: (layer, jnp.maximum(2 * k - 1, 0) // chunks_per_seq) + (0,) * len(s))
    state_shapes = ((SSD_WIDTH, SSD_STATE), (SSD_CONV - 1, SSD_CONV_DIM), (GLA_HEADS * GLA_DK, GLA_DV),
                    (LANES, WINDOW), (LANES, WINDOW))
    out_shape = (jax.ShapeDtypeStruct((rows_total, D_MODEL), F32),) + tuple(
        jax.ShapeDtypeStruct((depth, bsz) + s, F32) for s in state_shapes)
    return pl.pallas_call(
        kern,
        grid=(n_pairs + 1,),
        in_specs=[proj_rows, out_rows, p_spec, _const_spec(bucket.shape), _smem_spec(), _smem_spec()]
        + wspecs + [pl.BlockSpec(memory_space=pl.ANY)] * len(prev_states),
        out_specs=(out_rows,) + tuple(per_seq(s) for s in state_shapes),
        out_shape=out_shape,
        input_output_aliases={N_PROMPT_INPUTS + k: 1 + k for k in range(len(prev_states))},
        scratch_shapes=[
            pltpu.VMEM((chunk, PROJ_W), F32), pltpu.VMEM((chunk, PROJ_W), F32),
            pltpu.VMEM((chunk, XBC_W), F32), pltpu.VMEM((chunk, XBC_W), F32),
            pltpu.VMEM((chunk, D_MODEL), BF16), pltpu.VMEM((chunk, D_MODEL), BF16),
            pltpu.VMEM((chunk, D_MODEL), BF16),
            pltpu.VMEM((chunk, D_MODEL), F32),
            pltpu.VMEM((chunk, D_MODEL), BF16),
            pltpu.VMEM((SUBLANES, XBC_W), F32),
            pltpu.VMEM((BLK, SSD_WIDTH), F32),
            pltpu.VMEM((LANES, GLA_WIDTH), F32),
            pltpu.VMEM((2 * BLK, LANES), F32),
            pltpu.VMEM((2 * BLK, LANES), F32),
            pltpu.VMEM((2 * SWA_HEADS, BLK, 2 * BLK), F32),
        ],
        compiler_params=pltpu.CompilerParams(
            dimension_semantics=("arbitrary",), vmem_limit_bytes=VMEM_LIMIT_BYTES),
        name="prompt_layer",
    )(h, h, p_all, bucket, rel, sinks, *wops, *prev_states)


def _whole(shape, layer=None):
    if layer is None:
        return pl.BlockSpec(shape, lambda *_: (0,) * len(shape), pipeline_mode=pl.Buffered(1))
    return pl.BlockSpec((1,) + shape[1:], lambda *_: (layer,) + (0,) * (len(shape) - 1),
                        pipeline_mode=pl.Buffered(1))


def _sample_layer_native(layer, depth, seq, h, p_all, conv_in, ssm_in, gla_in, kc_in, vc_in, prev_states,
                         buckets, rel, sinks, wops):
    (nw, win, cw, cb, dtb, alog, dsk, snw, wgk, bgk, gnw, qnw, knw, wout, wpe, wpg) = wops
    rows = h.shape[0]
    n_seq = rows // seq
    assert n_seq == LANES and n_seq % SEQ_PER_STEP == 0 and SSD_HEADS * SEQ_PER_STEP == n_seq
    prev_conv, prev_rest = (prev_states[:1], prev_states[1:]) if prev_states else ((), ())
    f32 = lambda *s: jax.ShapeDtypeStruct(s, F32)
    cparams = lambda sem: pltpu.CompilerParams(dimension_semantics=sem, vmem_limit_bytes=VMEM_LIMIT_BYTES)

    per_step = SAMPLE_POSITIONS_PER_STEP
    step_rows = per_step * n_seq
    assert seq % per_step == 0
    row_blk = lambda w: pl.BlockSpec((step_rows, w), lambda s: (s, 0))
    pos_blk = lambda n: pl.BlockSpec((per_step, n, n_seq), lambda s: (s, 0, 0))
    front_in = [h, conv_in, nw, win, cw, cb, dtb, alog, wgk, bgk, qnw, knw]
    front_specs = [row_blk(D_MODEL), _whole(conv_in.shape, layer)] + [_whole(a.shape) for a in front_in[2:]]
    front_out = (f32(rows, SSD_WIDTH), f32(rows, D_MODEL),
                 f32(seq, SSD_WIDTH, n_seq), f32(seq, LANES, n_seq), f32(seq, LANES, n_seq),
                 f32(seq, SSD_HEADS, n_seq), f32(seq, LANES, n_seq), f32(seq, LANES, n_seq), f32(seq, LANES, n_seq),
                 f32(seq, GLA_WIDTH, n_seq), f32(rows, SWA_WIDTH), f32(rows, LANES), f32(rows, LANES),
                 f32(*conv_in.shape))
    front_out_specs = (row_blk(SSD_WIDTH), row_blk(D_MODEL), pos_blk(SSD_WIDTH), pos_blk(LANES), pos_blk(LANES),
                       pos_blk(SSD_HEADS), pos_blk(LANES), pos_blk(LANES), pos_blk(LANES), pos_blk(GLA_WIDTH),
                       row_blk(SWA_WIDTH), row_blk(LANES), row_blk(LANES), _whole(conv_in.shape, layer))
    (xs, gates, xt, bt, ct, at, qt, kt, egt, vt, qn, kn, vn, conv_o) = pl.pallas_call(
        functools.partial(_sample_front_kernel, seq, len(prev_conv)),
        grid=(seq // per_step,),
        in_specs=front_specs + [pl.BlockSpec(memory_space=pl.ANY)] * len(prev_conv),
        out_specs=front_out_specs,
        out_shape=front_out,
        input_output_aliases={len(front_in) + k: len(front_out) - 1 + k for k in range(len(prev_conv))},
        scratch_shapes=[pltpu.VMEM((step_rows, D_MODEL), BF16), pltpu.VMEM((rows, XBC_W), F32),
                        pltpu.VMEM((step_rows, PROJ_W), F32)],
        compiler_params=cparams(("arbitrary",)),
        name="sample_front",
    )(*front_in, *prev_conv)

    n_steps = SSD_HEADS
    per_group = SSD_HEADS // SSD_GROUPS
    gla_head = lambda j: jnp.minimum(j, GLA_HEADS - 1)
    blk3 = lambda n, f: pl.BlockSpec((seq, n, n_seq), lambda j: (0, f(j), 0))
    state_in = [xt, bt, ct, at, qt, kt, egt, vt, qn, kn, vn, ssm_in, gla_in, kc_in, vc_in, buckets[0], buckets[1],
                rel, sinks]
    ssm_spec = pl.BlockSpec((1, 1) + ssm_in.shape[2:], lambda j: (layer, j, 0, 0))
    gla_spec = pl.BlockSpec((1, 1) + gla_in.shape[2:], lambda j: (layer, gla_head(j), 0, 0))
    kv_spec = pl.BlockSpec((1, SEQ_PER_STEP) + kc_in.shape[2:], lambda j: (layer, j, 0, 0))
    state_specs = [blk3(SSD_HEAD_DIM, lambda j: j), blk3(SSD_STATE, lambda j: j // per_group),
                   blk3(SSD_STATE, lambda j: j // per_group), _const_spec(at.shape),
                   blk3(GLA_DK, gla_head), blk3(GLA_DK, gla_head), blk3(GLA_DK, gla_head), blk3(GLA_DV, gla_head),
                   _const_spec(qn.shape), _const_spec(kn.shape), _const_spec(vn.shape),
                   ssm_spec, gla_spec, kv_spec, kv_spec,
                   _const_spec(buckets[0].shape), _const_spec(buckets[1].shape), _smem_spec(), _smem_spec()]
    state_out = (f32(*ssm_in.shape), f32(*gla_in.shape), f32(*kc_in.shape), f32(*vc_in.shape),
                 f32(seq, SSD_WIDTH, n_seq), f32(seq, GLA_WIDTH, n_seq), f32(seq, n_seq, SWA_WIDTH))
    ssm_o, gla_o, ko, vo, yt, ot, oswa = pl.pallas_call(
        functools.partial(_sample_state_kernel, seq, len(prev_rest)),
        grid=(n_steps,),
        in_specs=state_specs + [pl.BlockSpec(memory_space=pl.ANY)] * len(prev_rest),
        out_specs=(ssm_spec, gla_spec, kv_spec, kv_spec, blk3(SSD_HEAD_DIM, lambda j: j), blk3(GLA_DV, gla_head),
                   pl.BlockSpec((seq, SEQ_PER_STEP, SWA_WIDTH), lambda j: (0, j, 0))),
        out_shape=state_out,
        input_output_aliases={len(state_in) + k: k for k in range(len(prev_rest))},
        scratch_shapes=[pltpu.VMEM((SEQ_PER_STEP * seq, LANES), F32)] * 6
        + [pltpu.VMEM((SWA_HEADS * seq, LANES), F32)] * 2,
        compiler_params=cparams(("arbitrary",)),
        name="sample_state",
    )(*state_in, *prev_rest)

    back_in = [yt, ot, oswa, xs, gates, h, p_all, dsk, snw, gnw, wout, wpe, wpg]
    back_specs = [pos_blk(SSD_WIDTH), pos_blk(GLA_WIDTH),
                  pl.BlockSpec((per_step, n_seq, SWA_WIDTH), lambda s: (s, 0, 0)),
                  row_blk(SSD_WIDTH), row_blk(D_MODEL), row_blk(D_MODEL),
                  pl.BlockSpec((1, step_rows, PLE_DIM), lambda s: (layer, s, 0))] + [
        _whole(a.shape) for a in back_in[7:]]
    y = pl.pallas_call(
        _sample_back_kernel,
        grid=(seq // per_step,),
        in_specs=back_specs,
        out_specs=row_blk(D_MODEL),
        out_shape=f32(rows, D_MODEL),
        scratch_shapes=[pltpu.VMEM((step_rows, D_MODEL), BF16)],
        compiler_params=cparams(("arbitrary",)),
        name="sample_back",
    )(*back_in)
    return y, (conv_o, ssm_o, gla_o, ko, vo)


SWA_HEAD_ORDER = (0, 2, 1, 3)


def _win_tile_runs():
    sizes = (SSD_WIDTH, SSD_CONV_DIM, SSD_HEADS, GLA_HEADS * GLA_DK, GLA_HEADS * GLA_DK, GLA_WIDTH, GLA_WIDTH,
             GLA_RANK, SWA_WIDTH, SWA_KV_HEADS * SWA_HEAD_DIM, SWA_KV_HEADS * SWA_HEAD_DIM, SWA_WIDTH)
    offs = np.concatenate([[0], np.cumsum(sizes)])
    seg = lambda k: np.arange(offs[k], offs[k + 1])
    z, xbc, dt, gq, gk, gv, gg, glr, sq, sk, sv, sg = [seg(k) for k in range(len(sizes))]
    heads = lambda a: np.concatenate([a[h * SWA_HEAD_DIM:(h + 1) * SWA_HEAD_DIM] for h in SWA_HEAD_ORDER])
    pad = np.full(LANES - SSD_HEADS - GLA_RANK, -1)
    src = np.concatenate([xbc, z, gq, gk, gv, gg, heads(sq), sk, sv, heads(sg), dt, glr, pad])
    assert src.size == XBC_W + PROJ_W
    tiles = []
    for j in range(src.size // LANES):
        idx = src[j * LANES:(j + 1) * LANES]
        cuts = [0] + [k for k in range(1, LANES) if (idx[k] != idx[k - 1] + 1 and not (idx[k] == -1 == idx[k - 1]))]
        runs = [(int(idx[a]), b - a) for a, b in zip(cuts, cuts[1:] + [LANES])]
        assert all(n % SUBLANES == 0 and (s < 0 or s % SUBLANES == 0) for s, n in runs)
        tiles.append(runs)
    return tiles


def _win_prep_kernel(tile_runs, wt_ref, out_ref):
    for j, runs in enumerate(tile_runs):
        parts = [jnp.zeros((n, D_MODEL), F32) if s < 0 else wt_ref[0, s:s + n, :] for s, n in runs]
        tile = parts[0] if len(parts) == 1 else jnp.concatenate(parts, axis=0)
        out_ref[0, :, j * LANES:(j + 1) * LANES] = tile.T.astype(BF16)


def _prepare_w_in(w_in):
    depth, d_model, d_in = w_in.shape
    w_t = jnp.swapaxes(w_in, 1, 2)
    return pl.pallas_call(
        functools.partial(_win_prep_kernel, _win_tile_runs()),
        grid=(depth,),
        in_specs=[pl.BlockSpec((1, d_in, d_model), lambda l: (l, 0, 0))],
        out_specs=pl.BlockSpec((1, d_model, XBC_W + PROJ_W), lambda l: (l, 0, 0)),
        out_shape=jax.ShapeDtypeStruct((depth, d_model, XBC_W + PROJ_W), BF16),
        compiler_params=pltpu.CompilerParams(
            dimension_semantics=("arbitrary",), vmem_limit_bytes=VMEM_LIMIT_BYTES),
        name="w_in_prep",
    )(w_t)


def _prepare_weights(norm_w, w_in, conv_w, conv_b, dt_bias, a_log, d_skip, ssd_norm_w, gla_w_gk, gla_b_gk,
                     gla_norm_w, q_norm_w, k_norm_w, w_out, w_pe, w_pg):
    mix_w = SSD_WIDTH + GLA_WIDTH
    w_out_p = jnp.concatenate(
        [w_out[:, :mix_w, :]] + [w_out[:, mix_w + h * SWA_HEAD_DIM:mix_w + (h + 1) * SWA_HEAD_DIM, :]
                                 for h in SWA_HEAD_ORDER], axis=1).astype(BF16)
    lane_pad = lambda x: jnp.pad(x, ((0, 0), (0, LANES - x.shape[-1])))[:, None, :]
    wgk_p = jnp.pad(gla_w_gk, ((0, 0), (LR_LANE0, LANES - LR_LANE0 - GLA_RANK), (0, 0))).astype(BF16)
    return dict(
        norm_w=norm_w[:, None, :], w_in=_prepare_w_in(w_in), conv_w=conv_w, conv_b=conv_b[:, None, :],
        dt_bias=lane_pad(dt_bias), a_log=lane_pad(a_log),
        d_skip=jnp.repeat(d_skip, SSD_HEAD_DIM, axis=-1)[:, None, :], ssd_norm_w=ssd_norm_w[:, None, :],
        gla_w_gk=wgk_p, gla_b_gk=gla_b_gk[:, None, :],
        gla_norm_w=jnp.tile(gla_norm_w, (1, GLA_HEADS))[:, None, :],
        q_norm_w=jnp.tile(q_norm_w, (1, SWA_HEADS))[:, None, :],
        k_norm_w=jnp.tile(k_norm_w, (1, SWA_KV_HEADS))[:, None, :],
        w_out=w_out_p, w_pe=w_pe.astype(BF16), w_pg=w_pg.astype(BF16))


def kernel(x_prompt, x_sample, state_ssm, state_conv, state_gla, cache_swa_k, cache_swa_v, p_prompt, p_sample, rel_bias, norm_w, w_in, conv_w, conv_b, dt_bias, a_log, d_skip, ssd_norm_w, gla_w_gk, gla_b_gk, gla_norm_w, q_norm_w, k_norm_w, attn_sinks, w_out, w_pe, w_pg):
    depth = w_in.shape[0]
    bp, seq_p, _ = x_prompt.shape
    bs, seq_s, _ = x_sample.shape
    assert seq_s == SUBLANES and BLK % seq_s == 0 and (bs * seq_s) % BLK == 0
    assert cache_swa_k.shape[2] == WINDOW
    w = _prepare_weights(norm_w, w_in, conv_w, conv_b, dt_bias, a_log, d_skip, ssd_norm_w, gla_w_gk, gla_b_gk,
                         gla_norm_w, q_norm_w, k_norm_w, w_out, w_pe, w_pg)
    rel_flat = rel_bias.reshape(-1)
    dist_p = WINDOW + np.arange(BLK)[:, None] - np.arange(2 * BLK)[None, :]
    bucket_p = jnp.asarray(_bucket_table(dist_p))
    t_of_row = np.tile(np.arange(seq_s), SWA_HEADS)[:, None]
    bucket_c = jnp.asarray(_bucket_table(WINDOW + t_of_row - np.arange(WINDOW)[None, :]))
    dist_n = np.where(np.arange(LANES)[None, :] < seq_s, t_of_row - np.arange(LANES)[None, :], -1)
    bucket_n = jnp.asarray(_bucket_table(dist_n))

    ssm_in = jnp.transpose(state_ssm, (0, 2, 3, 4, 1)).reshape(depth, SSD_HEADS, SSD_HEAD_DIM * SSD_STATE, bs)
    gla_in = jnp.transpose(state_gla, (0, 2, 3, 4, 1)).reshape(depth, GLA_HEADS, GLA_DK * GLA_DV, bs)
    kv_in = lambda a: jnp.transpose(a, (0, 1, 3, 4, 2)).reshape(depth, bs, SWA_KV_HEADS * SWA_HEAD_DIM, WINDOW)
    kc_in, vc_in = kv_in(cache_swa_k), kv_in(cache_swa_v)
    conv_in = jnp.transpose(state_conv, (0, 2, 1, 3))

    hp = x_prompt.reshape(bp * seq_p, D_MODEL)
    p_prompt_rows = p_prompt.reshape(depth, bp * seq_p, PLE_DIM)
    hs = jnp.transpose(x_sample, (1, 0, 2)).reshape(seq_s * bs, D_MODEL)
    p_sample_rows = jnp.transpose(p_sample, (0, 2, 1, 3)).reshape(depth, seq_s * bs, PLE_DIM)
    states_p = ()
    states_s = ()
    for i in range(depth):
        wops, wspecs = _layer_weights(i, w)
        hp, *states_p = _prompt_layer(i, depth, bp, hp, p_prompt_rows, tuple(states_p), bucket_p, rel_flat,
                                      attn_sinks[i], wops, wspecs)
        hs, states_s = _sample_layer_native(i, depth, seq_s, hs, p_sample_rows, conv_in, ssm_in, gla_in, kc_in,
                                            vc_in, states_s, (bucket_c, bucket_n), rel_flat, attn_sinks[i], wops)
    ssm_p, conv_p, gla_p, kt_p, vt_p = states_p
    conv_s, ssm_s, gla_s, kt_s, vt_s = states_s
    unpack_kv = lambda a: jnp.transpose(
        a.reshape(a.shape[:2] + (SWA_KV_HEADS, SWA_HEAD_DIM, WINDOW)), (0, 1, 4, 2, 3))
    outs_p = (ssm_p.reshape(depth, bp, SSD_HEADS, SSD_HEAD_DIM, SSD_STATE), conv_p,
              gla_p.reshape(depth, bp, GLA_HEADS, GLA_DK, GLA_DV), unpack_kv(kt_p), unpack_kv(vt_p))
    seq_last = lambda a, dims: jnp.transpose(a.reshape(a.shape[:2] + dims + (bs,)), (0, 4, 1, 2, 3))
    outs_s = (seq_last(ssm_s, (SSD_HEAD_DIM, SSD_STATE)), jnp.transpose(conv_s, (0, 2, 1, 3)),
              seq_last(gla_s, (GLA_DK, GLA_DV)), unpack_kv(kt_s), unpack_kv(vt_s))
    y_sample = jnp.transpose(hs.reshape(seq_s, bs, D_MODEL), (1, 0, 2))
    return (hp.reshape(bp, seq_p, D_MODEL), y_sample) + outs_p + outs_s
```

```python
import functools
import math

import numpy as np
import jax
import jax.numpy as jnp
from jax import lax
from jax.experimental import pallas as pl
from jax.experimental.pallas import tpu as pltpu

D_MODEL = 1024
DEPTH = 2
SSD_HEADS = 8
SSD_HEAD_DIM = 64
SSD_WIDTH = SSD_HEADS * SSD_HEAD_DIM
SSD_GROUPS = 2
SSD_STATE = 64
SSD_CONV = 4
SSD_CONV_DIM = SSD_WIDTH + 2 * SSD_GROUPS * SSD_STATE
SSD_CHUNK = 128
GLA_HEADS = 4
GLA_DK = 32
GLA_DV = 64
GLA_WIDTH = GLA_HEADS * GLA_DV
GLA_RANK = 16
GLA_GATE_NORM = 16.0
GLA_CHUNK = 64
SWA_HEADS = 4
SWA_KV_HEADS = 2
SWA_HEAD_DIM = 64
SWA_WIDTH = SWA_HEADS * SWA_HEAD_DIM
WINDOW = 128
REL_BUCKETS = 32
REL_MAX_DIST = 128
PLE_DIM = 256
EPS = 1e-6

LANES = 128
SUBLANES = 8
HALF = LANES // 2
BLK = 128
VMEM_LIMIT_BYTES = 56 * 1024 * 1024

XBC_W = SSD_CONV_DIM
P_Z = 0
P_GQ = P_Z + SSD_WIDTH
P_GK = P_GQ + LANES
P_GV = P_GK + LANES
P_GG = P_GV + GLA_WIDTH
P_SQ = P_GG + GLA_WIDTH
P_SK = P_SQ + SWA_WIDTH
P_SV = P_SK + LANES
P_SG = P_SV + LANES
P_DTLR = P_SG + SWA_WIDTH
PROJ_W = P_DTLR + LANES
LR_LANE0 = SSD_HEADS

F32 = jnp.float32
BF16 = jnp.bfloat16
NEG_INF = float("-inf")
N_PROMPT_INPUTS = 22
PROMPT_CHUNK_ROWS = 2 * BLK
TICKS_PER_ITEM = 2
NT_DIMS = (((1,), (1,)), ((), ()))


def _iota(shape, dim):
    return lax.broadcasted_iota(jnp.int32, shape, dim)


def _div(x, d):
    return x >> (d.bit_length() - 1)


def _mod(x, d):
    return x & (d - 1)


def _softplus(x):
    e = jnp.exp(-jnp.abs(x))
    u = 1.0 + e
    d = u - 1.0
    log1p_e = jnp.where(d == 0.0, e, jnp.log(u) * (e / jnp.where(d == 0.0, 1.0, d)))
    return jnp.maximum(x, 0.0) + log1p_e


def _log_sigmoid(x):
    return jnp.minimum(x, 0.0) - jnp.log(1.0 + jnp.exp(-jnp.abs(x)))


def _silu(x):
    return x * jax.nn.sigmoid(x)


def _dot(a, b):
    return jnp.dot(a.astype(BF16), b.astype(BF16), preferred_element_type=F32)


def _dot_nt(a, b):
    return lax.dot_general(a.astype(BF16), b.astype(BF16), NT_DIMS, preferred_element_type=F32)


def _dot_exact(sel, x):
    x1 = x.astype(BF16)
    r1 = x - x1.astype(F32)
    x2 = r1.astype(BF16)
    x3 = (r1 - x2.astype(F32)).astype(BF16)
    dot = functools.partial(jnp.dot, sel, preferred_element_type=F32)
    return dot(x1) + dot(x2) + dot(x3)


def _expand_heads(x, n_heads):
    rows = x.shape[0]
    lo = _iota((rows, LANES), 1) < HALF
    tiles = []
    for j in range(n_heads // 2):
        a = jnp.broadcast_to(x[:, 2 * j:2 * j + 1], (rows, LANES))
        b = jnp.broadcast_to(x[:, 2 * j + 1:2 * j + 2], (rows, LANES))
        tiles.append(jnp.where(lo, a, b))
    return jnp.concatenate(tiles, axis=1)


def _head_rms_scale(x):
    rows, width = x.shape
    lo = _iota((rows, LANES), 1) < HALF
    outs = []
    for j in range(width // LANES):
        t = x[:, j * LANES:(j + 1) * LANES]
        sq = t * t
        s_lo = jnp.sum(jnp.where(lo, sq, 0.0), axis=-1, keepdims=True)
        s_hi = jnp.sum(jnp.where(lo, 0.0, sq), axis=-1, keepdims=True)
        outs.append(lax.rsqrt(jnp.where(lo, s_lo, s_hi) * (1.0 / HALF) + EPS))
    return outs[0] if len(outs) == 1 else jnp.concatenate(outs, axis=1)


def _group_rmsnorm(y, w):
    gw = SSD_WIDTH // SSD_GROUPS
    outs = []
    for g in range(SSD_GROUPS):
        t = y[:, g * gw:(g + 1) * gw]
        ms = jnp.sum(t * t, axis=-1, keepdims=True) * (1.0 / gw)
        outs.append(t * lax.rsqrt(ms + EPS))
    return jnp.concatenate(outs, axis=1) * w


def _rel_bucket_np(dist):
    n = np.maximum(dist, 0)
    exact = REL_BUCKETS // 2
    nf = np.maximum(n, 1).astype(np.float64)
    large = exact + (np.log(nf / exact) / math.log(REL_MAX_DIST / exact) * (REL_BUCKETS - exact)).astype(np.int32)
    large = np.minimum(large, REL_BUCKETS - 1)
    return np.where(n < exact, n, large).astype(np.int32)


def _bucket_table(dist):
    return np.where((dist >= 0) & (dist < WINDOW), _rel_bucket_np(dist), -1).astype(np.int32)


def _no_tick():
    pass


def _ssd_intra(xbc_c, dtlr, dtb, a_row, pair_mask, tri, total_of, tick=_no_tick):
    xs = xbc_c[:, :SSD_WIDTH]
    bm = xbc_c[:, SSD_WIDTH:SSD_WIDTH + LANES]
    cm = xbc_c[:, SSD_WIDTH + LANES:]
    lane = _iota((BLK, LANES), 1)
    lo = lane < HALF
    dtv = _softplus(dtlr + dtb)
    adt = dtv * a_row
    acum = _dot_exact(tri, adt)
    tick()
    acum_t = acum.T
    eacum = jnp.exp(acum)
    tail = jnp.exp(total_of(adt, acum) - acum)
    dtv_e = _expand_heads(dtv, SSD_HEADS)
    eacum_e = _expand_heads(eacum, SSD_HEADS)
    tail_e = _expand_heads(tail, SSD_HEADS)
    tick()
    xdt = xs * dtv_e
    xw = xdt * tail_e
    cb = [_dot_nt(jnp.where(lo, cm, 0.0), bm), _dot_nt(jnp.where(lo, 0.0, cm), bm)]
    y_pairs = []
    for j in range(SSD_HEADS // 2):
        tick()
        g = (2 * j) // (SSD_HEADS // SSD_GROUPS)
        ms = []
        for k in range(2):
            h = 2 * j + k
            seg = acum[:, h:h + 1] - acum_t[h:h + 1, :]
            dec = jnp.where(pair_mask, jnp.exp(seg), 0.0)
            ms.append((cb[g] * dec).astype(BF16))
        xp = xdt[:, j * LANES:(j + 1) * LANES]
        rhs = jnp.concatenate([jnp.where(lo, xp, 0.0), jnp.where(lo, 0.0, xp)], axis=0)
        y_pairs.append(_dot(jnp.concatenate(ms, axis=1), rhs))
    y_intra = jnp.concatenate(y_pairs, axis=1)
    return y_intra, xs, bm, cm, xw, eacum, eacum_e


def _gla_intra(gq, gk, gv, glog, tri, total_of, att_masks, tick=_no_tick):
    bcs = _dot_exact(tri, glog)
    tick()
    eb = jnp.exp(bcs)
    qe = gq * (GLA_DK ** -0.5) * eb
    ke = gk * jnp.exp(-bcs)
    btot = total_of(glog, bcs)
    kd = gk * jnp.exp(btot - bcs)
    lane_k = _iota((GLA_CHUNK, LANES), 1)
    lane_v = _iota((GLA_CHUNK, GLA_WIDTH), 1)
    outs = []
    for c2 in range(BLK // GLA_CHUNK):
        tick()
        rs = slice(c2 * GLA_CHUNK, (c2 + 1) * GLA_CHUNK)
        ke_c = ke[rs]
        v_c = gv[rs]
        kbd = jnp.concatenate(
            [jnp.where(_div(lane_k, GLA_DK) == h, ke_c, 0.0) for h in range(GLA_HEADS)], axis=0)
        att = _dot_nt(qe[rs], kbd)
        att = jnp.where(att_masks[c2], att, 0.0)
        vbd = jnp.concatenate(
            [jnp.where(_div(lane_v, GLA_DV) == h, v_c, 0.0) for h in range(GLA_HEADS)], axis=0)
        outs.append(_dot(att, vbd))
    return jnp.concatenate(outs, axis=0), qe, kd, jnp.exp(btot)


def _build_bias(bucket, rel_ref):
    accs = [jnp.full(bucket.shape, NEG_INF, F32) for _ in range(SWA_HEADS)]
    for b in range(REL_BUCKETS):
        hit = bucket == b
        for h in range(SWA_HEADS):
            accs[h] = jnp.where(hit, rel_ref[b * SWA_HEADS + h], accs[h])
    return accs


def _epilogue(h, mix, p, wout_ref, wpg_ref, wpe_ref):
    h1 = h + jnp.dot(mix, wout_ref[...], preferred_element_type=F32)
    gate = jax.nn.sigmoid(jnp.dot(h1.astype(BF16), wpg_ref[...], preferred_element_type=F32))
    pe = jnp.dot(p.astype(BF16), wpe_ref[...], preferred_element_type=F32)
    return h1 + gate * pe


def _prompt_kernel(chunks_per_seq, n_aliased, *refs):
    (ha_ref, hc_ref, p_ref, bucket_ref, rel_ref, sink_ref, nw_ref, win_ref, cw_ref, cb_ref, dtb_ref,
     alog_ref, dsk_ref, snw_ref, wgk_ref, bgk_ref, gnw_ref, qnw_ref, knw_ref, wout_ref,
     wpe_ref, wpg_ref) = refs[:N_PROMPT_INPUTS]
    (y_ref, ssm_ref, conv_ref, gla_ref, ko_ref, vo_ref,
     proj_e, proj_o, xbc_e, xbc_o, mix_e, mix_o, u_s, h1_s, h1b_s, hist_s, st_s, s2_s, kext_s, vext_s,
     bias_s, hist_snap, st_snap, s2_snap, k_snap, v_snap) = refs[N_PROMPT_INPUTS + n_aliased:]
    k_idx = pl.program_id(0)

    @pl.when(k_idx == 0)
    def _():
        accs = _build_bias(bucket_ref[...], rel_ref)
        own_block = _iota((BLK, 2 * BLK), 1) >= BLK
        for hh in range(SWA_HEADS):
            bias_s[hh] = accs[hh]
            bias_s[SWA_HEADS + hh] = jnp.where(own_block, accs[hh], NEG_INF)
        for ref in (proj_o, xbc_o, mix_e, mix_o, hist_s, st_s, s2_s, kext_s, vext_s):
            ref[...] = jnp.zeros(ref.shape, ref.dtype)

    row = _iota((BLK, BLK), 0)
    col = _iota((BLK, BLK), 1)
    causal = row >= col
    tri_ssd = jnp.where(causal, 1.0, 0.0).astype(BF16)
    tri_gla = jnp.where(causal & (_div(row, GLA_CHUNK) == _div(col, GLA_CHUNK)), 1.0, 0.0).astype(BF16)
    lo = col < HALF
    lane_row = _iota((1, LANES), 1)
    a_row = jnp.where(lane_row < SSD_HEADS, -jnp.exp(alog_ref[...]), 0.0)
    st_mask = (_iota((BLK, SSD_WIDTH), 0) < SSD_STATE) == (_iota((BLK, SSD_WIDTH), 1) < SSD_WIDTH // SSD_GROUPS)
    bd_mask = _div(_iota((LANES, GLA_WIDTH), 0), GLA_DK) == _div(_iota((LANES, GLA_WIDTH), 1), GLA_DV)
    att_t = _iota((GLA_CHUNK, GLA_WIDTH), 0)
    att_s = _mod(_iota((GLA_CHUNK, GLA_WIDTH), 1), GLA_CHUNK)
    att_mask = att_s <= att_t
    lo2 = _iota((2 * BLK, LANES), 1) < HALF

    def ssd_total(adt, acum):
        return jnp.broadcast_to(acum[BLK - 1:BLK, :], acum.shape)

    def gla_total(glog, bcs):
        return jnp.concatenate(
            [jnp.broadcast_to(bcs[(c2 + 1) * GLA_CHUNK - 1:(c2 + 1) * GLA_CHUNK, :], (GLA_CHUNK, LANES))
             for c2 in range(BLK // GLA_CHUNK)], axis=0)

    def block(blk, proj_s, xbc_s, mix_s, starts_sequence, tick):
        rows = slice(blk * BLK, (blk + 1) * BLK)
        cw = cw_ref[...]
        tick()
        if blk == 0:
            xwin = jnp.concatenate([hist_s[...], xbc_s[0:BLK, :]], axis=0)
        else:
            xwin = xbc_s[blk * BLK - SUBLANES:(blk + 1) * BLK, :]
        acc = xwin[SUBLANES - 3:SUBLANES - 3 + BLK, :] * cw[0:1, :]
        for k in range(1, SSD_CONV):
            acc = acc + xwin[SUBLANES - 3 + k:SUBLANES - 3 + k + BLK, :] * cw[k:k + 1, :]
        xbc_c = _silu(acc + cb_ref[...])
        tick()
        dtlr = proj_s[rows, P_DTLR:P_DTLR + LANES]
        y_intra, xs, bm, cm, xw, eacum, eacum_e = _ssd_intra(
            xbc_c, dtlr, dtb_ref[...], a_row, causal, tri_ssd, ssd_total, tick)
        tick()
        st = st_s[...]
        y = y_intra + _dot(cm, st) * eacum_e + dsk_ref[...] * xs
        st_s[...] = st * eacum_e[BLK - 1:BLK, :] + jnp.where(st_mask, _dot(bm.T, xw), 0.0)
        tick()
        y = y * _silu(proj_s[rows, P_Z:P_Z + SSD_WIDTH])
        mix_s[rows, 0:SSD_WIDTH] = _group_rmsnorm(y, snw_ref[...]).astype(BF16)
        tick()
        gk = proj_s[rows, P_GK:P_GK + LANES]
        gv = proj_s[rows, P_GV:P_GV + GLA_WIDTH]
        glog = _log_sigmoid(_dot(dtlr, wgk_ref[...]) + bgk_ref[...]) * (1.0 / GLA_GATE_NORM)
        o_intra, qe, kd, ebt = _gla_intra(
            proj_s[rows, P_GQ:P_GQ + LANES], gk, gv, glog, tri_gla, gla_total, [att_mask, att_mask], tick)
        kd_t = kd.T
        ebt_t = ebt.T
        s2 = s2_s[...]
        o_parts = []
        for c2 in range(BLK // GLA_CHUNK):
            tick()
            rs = slice(c2 * GLA_CHUNK, (c2 + 1) * GLA_CHUNK)
            o_parts.append(o_intra[rs] + _dot(qe[rs], s2))
            u2 = _dot(jnp.where(_div(col, GLA_CHUNK) == c2, kd_t, 0.0), gv)
            last = (c2 + 1) * GLA_CHUNK - 1
            s2 = s2 * ebt_t[:, last:last + 1] + jnp.where(bd_mask, u2, 0.0)
        s2_s[...] = s2
        o = jnp.concatenate(o_parts, axis=0)
        y_gla = o * _head_rms_scale(o) * gnw_ref[...] * _silu(proj_s[rows, P_GG:P_GG + GLA_WIDTH])
        mix_s[rows, SSD_WIDTH:SSD_WIDTH + GLA_WIDTH] = y_gla.astype(BF16)
        tick()
        sq = proj_s[rows, P_SQ:P_SQ + SWA_WIDTH]
        qn = sq * _head_rms_scale(sq) * qnw_ref[...] * (SWA_HEAD_DIM ** -0.5)
        sk = proj_s[rows, P_SK:P_SK + LANES]
        kn = sk * _head_rms_scale(sk) * knw_ref[...]
        vn = proj_s[rows, P_SV:P_SV + LANES]
        kext_s[BLK:2 * BLK, :] = kn
        vext_s[BLK:2 * BLK, :] = vn
        kext = kext_s[...]
        vext = vext_s[...]
        qa = qn[:, :LANES]
        qb = qn[:, LANES:]
        qs = jnp.concatenate([jnp.where(lo, qa, 0.0), jnp.where(lo, qb, 0.0),
                              jnp.where(lo, 0.0, qa), jnp.where(lo, 0.0, qb)], axis=0)
        logits = _dot_nt(qs, kext)
        tick()
        if blk == 0 and starts_sequence is not False:
            bias_row0 = jnp.where(starts_sequence, SWA_HEADS, 0)
        else:
            bias_row0 = 0
        es = []
        invs = []
        for hh in range(SWA_HEADS):
            tick()
            sink = sink_ref[hh]
            l = logits[hh * BLK:(hh + 1) * BLK] + bias_s[bias_row0 + hh]
            m = jnp.maximum(jnp.max(l, axis=-1, keepdims=True), sink)
            e = jnp.exp(l - m)
            den = jnp.sum(e, axis=-1, keepdims=True) + jnp.exp(sink - m)
            es.append(e.astype(BF16))
            invs.append(1.0 / den)
        v_stack = jnp.concatenate([jnp.where(lo2, vext, 0.0), jnp.where(lo2, 0.0, vext)], axis=0)
        tile_a = _dot(jnp.concatenate([es[0], es[2]], axis=1), v_stack) * jnp.where(lo, invs[0], invs[2])
        tile_b = _dot(jnp.concatenate([es[1], es[3]], axis=1), v_stack) * jnp.where(lo, invs[1], invs[3])
        oa = jnp.concatenate([tile_a, tile_b], axis=1)
        y_swa = oa * _silu(proj_s[rows, P_SG:P_SG + SWA_WIDTH])
        mix_s[rows, SSD_WIDTH + GLA_WIDTH:] = y_swa.astype(BF16)
        kext_s[0:BLK, :] = kn
        vext_s[0:BLK, :] = vn

    chunk = proj_e.shape[0]

    def project_items(rows, proj_s, xbc_s):
        def norm():
            h = ha_ref[rows, :]
            ms = jnp.mean(h * h, axis=-1, keepdims=True)
            u_s[...] = (h * lax.rsqrt(ms + EPS) * nw_ref[...]).astype(BF16)

        def cols(dst, lo_c, hi_c, w_off):
            def item():
                dst[:, lo_c:hi_c] = jnp.dot(u_s[...], win_ref[:, w_off + lo_c:w_off + hi_c],
                                            preferred_element_type=F32)
            return item

        step = 2 * LANES
        items = [norm]
        items += [cols(xbc_s, c, min(c + step, XBC_W), 0) for c in range(0, XBC_W, step)]
        items += [cols(proj_s, c, min(c + step, PROJ_W), XBC_W) for c in range(0, PROJ_W, step)]
        return items

    def epilogue_items(rows, mix_s):
        half_w = 2 * LANES

        def residual(c):
            def item():
                h1 = hc_ref[rows, c:c + half_w] + jnp.dot(
                    mix_s[...], wout_ref[:, c:c + half_w], preferred_element_type=F32)
                h1_s[:, c:c + half_w] = h1
                h1b_s[:, c:c + half_w] = h1.astype(BF16)
            return item

        def gated(c):
            def item():
                gate = jax.nn.sigmoid(jnp.dot(h1b_s[...], wpg_ref[:, c:c + half_w], preferred_element_type=F32))
                pe = jnp.dot(p_ref[0, rows, :].astype(BF16), wpe_ref[:, c:c + half_w],
                             preferred_element_type=F32)
                y_ref[rows, c:c + half_w] = h1_s[:, c:c + half_w] + gate * pe
            return item

        col0 = range(0, D_MODEL, half_w)
        return [residual(c) for c in col0] + [gated(c) for c in col0]

    def merge(first, second):
        out = list(first)
        for j, item in enumerate(second):
            out.insert(((j + 1) * len(first)) // len(second) + j, item)
        return out

    def mixer(proj_s, xbc_s, mix_s, starts_sequence, items):
        if starts_sequence is not False:
            keep = jnp.where(starts_sequence, 0.0, 1.0)
            for ref in (hist_s, st_s, s2_s):
                ref[...] = ref[...] * keep
            kext_s[0:BLK, :] = kext_s[0:BLK, :] * keep
            vext_s[0:BLK, :] = vext_s[0:BLK, :] * keep
        queue = list(items)
        calls = [0]

        def tick():
            calls[0] += 1
            if queue and calls[0] % TICKS_PER_ITEM == 0:
                queue.pop(0)()

        for blk in range(chunk // BLK):
            block(blk, proj_s, xbc_s, mix_s, starts_sequence, tick)
        while queue:
            queue.pop(0)()
        hist_s[...] = xbc_s[chunk - SUBLANES:chunk, :]

    def snapshot_states():
        st_snap[...] = st_s[...]
        s2_snap[...] = s2_s[...]
        hist_snap[...] = hist_s[...]
        k_snap[...] = kext_s[0:BLK, :]
        v_snap[...] = vext_s[0:BLK, :]

    def write_states():
        st = st_snap[...]
        stc = st[:SSD_STATE] + st[SSD_STATE:]
        ssm_ref[0, 0] = jnp.concatenate([stc, stc], axis=0).T[:, :SSD_STATE]
        conv_ref[0, 0] = hist_snap[SUBLANES - (SSD_CONV - 1):SUBLANES, :]
        s2 = s2_snap[...]
        w = s2[:, :LANES] + s2[:, LANES:]
        gla_ref[0, 0] = w[:, :GLA_DV] + w[:, GLA_DV:]
        ko_ref[0, 0] = k_snap[...].T
        vo_ref[0, 0] = v_snap[...].T

    even = slice(0, chunk)
    odd = slice(chunk, 2 * chunk)
    mixer(proj_o, xbc_o, mix_o, False, merge(project_items(even, proj_e, xbc_e), epilogue_items(even, mix_e)))
    snapshot_states()
    mixer(proj_e, xbc_e, mix_e, _mod(2 * k_idx, chunks_per_seq) == 0,
          merge(project_items(odd, proj_o, xbc_o), epilogue_items(odd, mix_o)))

    @pl.when((k_idx >= 1) & (_mod(2 * k_idx - 1, chunks_per_seq) == chunks_per_seq - 1))
    def _():
        write_states()


N_FRONT_INPUTS = 12
SAMPLE_POSITIONS_PER_STEP = 2


def _sample_front_kernel(seq, n_aliased, *refs):
    (h_ref, cst_ref, nw_ref, win_ref, cw_ref, cb_ref, dtb_ref, alog_ref, wgk_ref, bgk_ref,
     qnw_ref, knw_ref) = refs[:N_FRONT_INPUTS]
    (xs_ref, gates_ref, xt_ref, bt_ref, ct_ref, at_ref, qt_ref, kt_ref, egt_ref, vt_ref,
     qn_ref, kn_ref, vn_ref, conv_ref, u_s, xbc_s, proj_s) = refs[N_FRONT_INPUTS + n_aliased:]
    n_seq = BLK
    per_step = h_ref.shape[0] // n_seq
    step = pl.program_id(0)
    ht = h_ref[...]
    ms = jnp.mean(ht * ht, axis=-1, keepdims=True)
    u_s[...] = (ht * lax.rsqrt(ms + EPS) * nw_ref[...]).astype(BF16)
    xbc_s[pl.ds(pl.multiple_of(step * per_step * n_seq, n_seq), per_step * n_seq), :] = jnp.dot(
        u_s[...], win_ref[:, :XBC_W], preferred_element_type=F32)
    proj_s[...] = jnp.dot(u_s[...], win_ref[:, XBC_W:], preferred_element_type=F32)
    cw = cw_ref[...]
    a_row = jnp.where(_iota((1, LANES), 1) < SSD_HEADS, -jnp.exp(alog_ref[...]), 0.0)
    for i in range(per_step):
        t = step * per_step + i
        rows = slice(i * n_seq, (i + 1) * n_seq)

        def raw_xbc(back):
            cur = xbc_s[pl.ds(pl.multiple_of(jnp.maximum(t - back, 0) * n_seq, n_seq), n_seq), :]
            if back == 0:
                return cur
            old = cst_ref[0, jnp.clip(SSD_CONV - 1 + t - back, 0, SSD_CONV - 2)]
            return jnp.where(t >= back, cur, old)

        acc = raw_xbc(SSD_CONV - 1) * cw[0:1, :]
        for k in range(1, SSD_CONV):
            acc = acc + raw_xbc(SSD_CONV - 1 - k) * cw[k:k + 1, :]
        xbc_c = _silu(acc + cb_ref[...])
        xs = xbc_c[:, :SSD_WIDTH]
        dtlr = proj_s[rows, P_DTLR:P_DTLR + LANES]
        dtv = _softplus(dtlr + dtb_ref[...])
        xs_ref[rows, :] = xs
        xt_ref[i] = (xs * _expand_heads(dtv, SSD_HEADS)).T
        bt_ref[i] = xbc_c[:, SSD_WIDTH:SSD_WIDTH + LANES].T
        ct_ref[i] = xbc_c[:, SSD_WIDTH + LANES:].T
        at_ref[i] = jnp.exp(dtv * a_row).T[:SSD_HEADS, :]
        glog = _log_sigmoid(_dot(dtlr, wgk_ref[...]) + bgk_ref[...]) * (1.0 / GLA_GATE_NORM)
        qt_ref[i] = (proj_s[rows, P_GQ:P_GQ + LANES] * (GLA_DK ** -0.5)).T
        kt_ref[i] = proj_s[rows, P_GK:P_GK + LANES].T
        egt_ref[i] = jnp.exp(glog).T
        vt_ref[i] = proj_s[rows, P_GV:P_GV + GLA_WIDTH].T
        sq = proj_s[rows, P_SQ:P_SQ + SWA_WIDTH]
        qn_ref[rows, :] = sq * _head_rms_scale(sq) * qnw_ref[...] * (SWA_HEAD_DIM ** -0.5)
        sk = proj_s[rows, P_SK:P_SK + LANES]
        kn_ref[rows, :] = sk * _head_rms_scale(sk) * knw_ref[...]
        vn_ref[rows, :] = proj_s[rows, P_SV:P_SV + LANES]
        gates_ref[rows, :] = jnp.concatenate(
            [_silu(proj_s[rows, P_Z:P_Z + SSD_WIDTH]), _silu(proj_s[rows, P_GG:P_GG + GLA_WIDTH]),
             _silu(proj_s[rows, P_SG:P_SG + SWA_WIDTH])], axis=1)
        first_kept = seq - (SSD_CONV - 1)

        @pl.when(t >= first_kept)
        def _():
            conv_ref[0, jnp.maximum(t - first_kept, 0)] = raw_xbc(0)


N_STATE_INPUTS = 19
SEQ_PER_STEP = 16


def _sample_state_kernel(seq, n_aliased, *refs):
    (xt_ref, bt_ref, ct_ref, at_ref, qt_ref, kt_ref, egt_ref, vt_ref, qn_ref, kn_ref, vn_ref,
     ssm_ref, gla_ref, kc_ref, vc_ref, bucket_c_ref, bucket_n_ref, rel_ref, sink_ref) = refs[:N_STATE_INPUTS]
    (ssm_o, gla_o, ko_ref, vo_ref, yt_ref, ot_ref, oswa_ref,
     qa_s, qb_s, krow_s, vrow_s, oa_s, ob_s, biasc_s, biasn_s) = refs[N_STATE_INPUTS + n_aliased:]
    j = pl.program_id(0)
    n_seq = LANES
    head_of_row = _div(_iota((SWA_HEADS * seq, LANES), 0), seq)

    def by_head(values):
        out = values[SWA_HEADS - 1]
        for hh in range(SWA_HEADS - 2, -1, -1):
            out = jnp.where(head_of_row == hh, values[hh], out)
        return out

    @pl.when(j == 0)
    def _():
        biasc_s[...] = by_head(_build_bias(bucket_c_ref[...], rel_ref))
        biasn_s[...] = by_head(_build_bias(bucket_n_ref[...], rel_ref))

    sub = _iota((SUBLANES, LANES), 0)
    a_rows = [jnp.sum(jnp.where(sub == j, at_ref[t], 0.0), axis=0, keepdims=True) for t in range(seq)]

    def ssd_body(p8, carry):
        r8 = pl.multiple_of(p8 * SUBLANES, SUBLANES)
        x_tiles = [xt_ref[t, pl.ds(r8, SUBLANES), :] for t in range(seq)]
        y_rows = [[] for _ in range(seq)]
        for pp in range(SUBLANES):
            r64 = pl.multiple_of((p8 * SUBLANES + pp) * SSD_STATE, SSD_STATE)
            slab = ssm_ref[0, 0, pl.ds(r64, SSD_STATE), :]
            for t in range(seq):
                slab = slab * a_rows[t] + x_tiles[t][pp:pp + 1, :] * bt_ref[t]
                y_rows[t].append(jnp.sum(ct_ref[t] * slab, axis=0, keepdims=True))
            ssm_o[0, 0, pl.ds(r64, SSD_STATE), :] = slab
        for t in range(seq):
            yt_ref[t, pl.ds(r8, SUBLANES), :] = jnp.concatenate(y_rows[t], axis=0)
        return carry

    lax.fori_loop(0, SSD_HEAD_DIM // SUBLANES, ssd_body, 0)

    @pl.when(j < GLA_HEADS)
    def _():
        for t in range(seq):
            ot_ref[t] = jnp.zeros((GLA_DV, LANES), F32)

        def gla_body(d8, carry):
            r8 = pl.multiple_of(d8 * SUBLANES, SUBLANES)
            q_tiles = [qt_ref[t, pl.ds(r8, SUBLANES), :] for t in range(seq)]
            k_tiles = [kt_ref[t, pl.ds(r8, SUBLANES), :] for t in range(seq)]
            g_tiles = [egt_ref[t, pl.ds(r8, SUBLANES), :] for t in range(seq)]
            for dd in range(SUBLANES):
                r64 = pl.multiple_of((d8 * SUBLANES + dd) * GLA_DV, GLA_DV)
                slab = gla_ref[0, 0, pl.ds(r64, GLA_DV), :]
                for t in range(seq):
                    slab = slab * g_tiles[t][dd:dd + 1, :] + k_tiles[t][dd:dd + 1, :] * vt_ref[t]
                    ot_ref[t] = ot_ref[t] + q_tiles[t][dd:dd + 1, :] * slab
                gla_o[0, 0, pl.ds(r64, GLA_DV), :] = slab
            return carry

        lax.fori_loop(0, GLA_DK // SUBLANES, gla_body, 0)

    base = pl.multiple_of(j * SEQ_PER_STEP, SEQ_PER_STEP)
    for t in range(seq):
        src = pl.ds(t * n_seq + base, SEQ_PER_STEP)
        dst = pl.ds(t, SEQ_PER_STEP, stride=seq)
        qa_s[dst, :] = qn_ref[src, 0:LANES]
        qb_s[dst, :] = qn_ref[src, LANES:2 * LANES]
        krow_s[dst, :] = kn_ref[src, :]
        vrow_s[dst, :] = vn_ref[src, :]
    kn_t = krow_s[...].T
    vn_t = vrow_s[...].T
    keep_old = _iota((LANES, WINDOW), 1) < WINDOW - seq
    lo8 = _iota((seq, LANES), 1) < HALF
    sink_col = by_head([jnp.full((SWA_HEADS * seq, LANES), sink_ref[hh], F32) for hh in range(SWA_HEADS)])[:, 0:1]

    def swa_stages(bl):
        r8 = pl.multiple_of(bl * seq, seq)
        v = {}

        def logits():
            qa = qa_s[pl.ds(r8, seq), :]
            qb = qb_s[pl.ds(r8, seq), :]
            qs = jnp.concatenate([jnp.where(lo8, qa, 0.0), jnp.where(lo8, qb, 0.0),
                                  jnp.where(lo8, 0.0, qa), jnp.where(lo8, 0.0, qb)], axis=0)
            v["lc"] = _dot(qs, kc_ref[0, bl]) + biasc_s[...]
            v["ln"] = _dot_nt(qs, krow_s[pl.ds(r8, seq), :]) + biasn_s[:, 0:seq]

        def softmax():
            lc, ln = v["lc"], v["ln"]
            m = jnp.maximum(jnp.maximum(jnp.max(lc, axis=-1, keepdims=True), jnp.max(ln, axis=-1, keepdims=True)),
                            sink_col)
            v["ec"] = jnp.exp(lc - m)
            v["en"] = jnp.exp(ln - m)
            v["inv"] = 1.0 / (jnp.sum(v["ec"], axis=-1, keepdims=True) + jnp.sum(v["en"], axis=-1, keepdims=True)
                              + jnp.exp(sink_col - m))

        def values():
            o = (_dot_nt(v["ec"], vc_ref[0, bl]) + _dot(v["en"], vrow_s[pl.ds(r8, seq), :])) * v["inv"]
            oa_s[pl.ds(r8, seq), :] = jnp.where(lo8, o[0:seq], o[2 * seq:3 * seq])
            ob_s[pl.ds(r8, seq), :] = jnp.where(lo8, o[seq:2 * seq], o[3 * seq:4 * seq])

        def window():
            ko_ref[0, bl] = jnp.where(keep_old, pltpu.roll(kc_ref[0, bl], WINDOW - seq, axis=1),
                                      pltpu.roll(kn_t, WINDOW - seq - r8, axis=1))
            vo_ref[0, bl] = jnp.where(keep_old, pltpu.roll(vc_ref[0, bl], WINDOW - seq, axis=1),
                                      pltpu.roll(vn_t, WINDOW - seq - r8, axis=1))

        return [logits, softmax, values, window]

    all_stages = [swa_stages(bl) for bl in range(SEQ_PER_STEP)]
    for k in range(len(all_stages[0])):
        for stages in all_stages:
            stages[k]()
    for t in range(seq):
        src = pl.ds(t, SEQ_PER_STEP, stride=seq)
        oswa_ref[t] = jnp.concatenate([oa_s[src, :], ob_s[src, :]], axis=1)


def _sample_back_kernel(yt_ref, ot_ref, oswa_ref, xs_ref, gates_ref, h_ref, p_ref, dsk_ref, snw_ref, gnw_ref,
                        wout_ref, wpe_ref, wpg_ref, y_ref, mix_s):
    n_seq = BLK
    for t in range(yt_ref.shape[0]):
        rows = slice(t * n_seq, (t + 1) * n_seq)
        y = (yt_ref[t].T + dsk_ref[...] * xs_ref[rows, :]) * gates_ref[rows, 0:SSD_WIDTH]
        mix_s[rows, 0:SSD_WIDTH] = _group_rmsnorm(y, snw_ref[...]).astype(BF16)
        o = ot_ref[t].T
        y_gla = o * _head_rms_scale(o) * gnw_ref[...] * gates_ref[rows, SSD_WIDTH:SSD_WIDTH + GLA_WIDTH]
        mix_s[rows, SSD_WIDTH:SSD_WIDTH + GLA_WIDTH] = y_gla.astype(BF16)
        mix_s[rows, SSD_WIDTH + GLA_WIDTH:] = (oswa_ref[t] * gates_ref[rows, SSD_WIDTH + GLA_WIDTH:]).astype(BF16)
    y_ref[...] = _epilogue(h_ref[...], mix_s[...], p_ref[0], wout_ref, wpg_ref, wpe_ref)


def _const_spec(shape):
    nd = len(shape)
    return pl.BlockSpec(shape, lambda *_: (0,) * nd)


def _smem_spec():
    return pl.BlockSpec(memory_space=pltpu.SMEM)


def _layer_spec(arr, layer):
    return pl.BlockSpec((None,) + arr.shape[1:], lambda *_: (layer, 0, 0), pipeline_mode=pl.Buffered(1))


def _layer_weights(layer, w):
    ops = [w[name] for name in ("norm_w", "w_in", "conv_w", "conv_b", "dt_bias", "a_log", "d_skip", "ssd_norm_w",
                                "gla_w_gk", "gla_b_gk", "gla_norm_w", "q_norm_w", "k_norm_w", "w_out", "w_pe",
                                "w_pg")]
    return ops, [_layer_spec(o, layer) for o in ops]


def _prompt_layer(layer, depth, bsz, h, p_all, prev_states, bucket, rel, sinks, wops, wspecs):
    rows_total, _ = h.shape
    seq_len = rows_total // bsz
    chunk = PROMPT_CHUNK_ROWS
    pair = 2 * chunk
    chunks_per_seq = seq_len // chunk
    assert seq_len % pair == 0 and chunks_per_seq & (chunks_per_seq - 1) == 0
    n_pairs = rows_total // pair
    kern = functools.partial(_prompt_kernel, chunks_per_seq, len(prev_states))
    proj_rows = pl.BlockSpec((pair, D_MODEL), lambda k: (jnp.minimum(k, n_pairs - 1), 0))
    out_rows = pl.BlockSpec((pair, D_MODEL), lambda k: (jnp.maximum(k - 1, 0), 0))
    p_spec = pl.BlockSpec((1, pair, PLE_DIM), lambda k: (layer, jnp.maximum(k - 1, 0), 0))
    per_seq = lambda s: pl.BlockSpec(
        (1, 1) + s, lambda k: (layer, jnp.maximum(2 * k - 1, 0) // chunks_per_seq) + (0,) * len(s))
    state_shapes = ((SSD_WIDTH, SSD_STATE), (SSD_CONV - 1, SSD_CONV_DIM), (GLA_HEADS * GLA_DK, GLA_DV),
                    (LANES, WINDOW), (LANES, WINDOW))
    out_shape = (jax.ShapeDtypeStruct((rows_total, D_MODEL), F32),) + tuple(
        jax.ShapeDtypeStruct((depth, bsz) + s, F32) for s in state_shapes)
    return pl.pallas_call(
        kern,
        grid=(n_pairs + 1,),
        in_specs=[proj_rows, out_rows, p_spec, _const_spec(bucket.shape), _smem_spec(), _smem_spec()]
        + wspecs + [pl.BlockSpec(memory_space=pl.ANY)] * len(prev_states),
        out_specs=(out_rows,) + tuple(per_seq(s) for s in state_shapes),
        out_shape=out_shape,
        input_output_aliases={N_PROMPT_INPUTS + k: 1 + k for k in range(len(prev_states))},
        scratch_shapes=[
            pltpu.VMEM((chunk, PROJ_W), F32), pltpu.VMEM((chunk, PROJ_W), F32),
            pltpu.VMEM((chunk, XBC_W), F32), pltpu.VMEM((chunk, XBC_W), F32),
            pltpu.VMEM((chunk, D_MODEL), BF16), pltpu.VMEM((chunk, D_MODEL), BF16),
            pltpu.VMEM((chunk, D_MODEL), BF16),
            pltpu.VMEM((chunk, D_MODEL), F32),
            pltpu.VMEM((chunk, D_MODEL), BF16),
            pltpu.VMEM((SUBLANES, XBC_W), F32),
            pltpu.VMEM((BLK, SSD_WIDTH), F32),
            pltpu.VMEM((LANES, GLA_WIDTH), F32),
            pltpu.VMEM((2 * BLK, LANES), F32),
            pltpu.VMEM((2 * BLK, LANES), F32),
            pltpu.VMEM((2 * SWA_HEADS, BLK, 2 * BLK), F32),
            pltpu.VMEM((SUBLANES, XBC_W), F32), pltpu.VMEM((BLK, SSD_WIDTH), F32),
            pltpu.VMEM((LANES, GLA_WIDTH), F32), pltpu.VMEM((BLK, LANES), F32),
            pltpu.VMEM((BLK, LANES), F32),
        ],
        compiler_params=pltpu.CompilerParams(
            dimension_semantics=("arbitrary",), vmem_limit_bytes=VMEM_LIMIT_BYTES),
        name="prompt_layer",
    )(h, h, p_all, bucket, rel, sinks, *wops, *prev_states)


def _whole(shape, layer=None):
    if layer is None:
        return pl.BlockSpec(shape, lambda *_: (0,) * len(shape), pipeline_mode=pl.Buffered(1))
    return pl.BlockSpec((1,) + shape[1:], lambda *_: (layer,) + (0,) * (len(shape) - 1),
                        pipeline_mode=pl.Buffered(1))


def _sample_layer_native(layer, depth, seq, h, p_all, conv_in, ssm_in, gla_in, kc_in, vc_in, prev_states,
                         buckets, rel, sinks, wops):
    (nw, win, cw, cb, dtb, alog, dsk, snw, wgk, bgk, gnw, qnw, knw, wout, wpe, wpg) = wops
    rows = h.shape[0]
    n_seq = rows // seq
    assert n_seq == LANES and n_seq % SEQ_PER_STEP == 0 and SSD_HEADS * SEQ_PER_STEP == n_seq
    prev_conv, prev_rest = (prev_states[:1], prev_states[1:]) if prev_states else ((), ())
    f32 = lambda *s: jax.ShapeDtypeStruct(s, F32)
    cparams = lambda sem: pltpu.CompilerParams(dimension_semantics=sem, vmem_limit_bytes=VMEM_LIMIT_BYTES)

    per_step = SAMPLE_POSITIONS_PER_STEP
    step_rows = per_step * n_seq
    assert seq % per_step == 0
    row_blk = lambda w: pl.BlockSpec((step_rows, w), lambda s: (s, 0))
    pos_blk = lambda n: pl.BlockSpec((per_step, n, n_seq), lambda s: (s, 0, 0))
    front_in = [h, conv_in, nw, win, cw, cb, dtb, alog, wgk, bgk, qnw, knw]
    front_specs = [row_blk(D_MODEL), _whole(conv_in.shape, layer)] + [_layer_spec(a, layer) for a in front_in[2:]]
    front_out = (f32(rows, SSD_WIDTH), f32(rows, D_MODEL),
                 f32(seq, SSD_WIDTH, n_seq), f32(seq, LANES, n_seq), f32(seq, LANES, n_seq),
                 f32(seq, SSD_HEADS, n_seq), f32(seq, LANES, n_seq), f32(seq, LANES, n_seq), f32(seq, LANES, n_seq),
                 f32(seq, GLA_WIDTH, n_seq), f32(rows, SWA_WIDTH), f32(rows, LANES), f32(rows, LANES),
                 f32(*conv_in.shape))
    front_out_specs = (row_blk(SSD_WIDTH), row_blk(D_MODEL), pos_blk(SSD_WIDTH), pos_blk(LANES), pos_blk(LANES),
                       pos_blk(SSD_HEADS), pos_blk(LANES), pos_blk(LANES), pos_blk(LANES), pos_blk(GLA_WIDTH),
                       row_blk(SWA_WIDTH), row_blk(LANES), row_blk(LANES), _whole(conv_in.shape, layer))
    (xs, gates, xt, bt, ct, at, qt, kt, egt, vt, qn, kn, vn, conv_o) = pl.pallas_call(
        functools.partial(_sample_front_kernel, seq, len(prev_conv)),
        grid=(seq // per_step,),
        in_specs=front_specs + [pl.BlockSpec(memory_space=pl.ANY)] * len(prev_conv),
        out_specs=front_out_specs,
        out_shape=front_out,
        input_output_aliases={len(front_in) + k: len(front_out) - 1 + k for k in range(len(prev_conv))},
        scratch_shapes=[pltpu.VMEM((step_rows, D_MODEL), BF16), pltpu.VMEM((rows, XBC_W), F32),
                        pltpu.VMEM((step_rows, PROJ_W), F32)],
        compiler_params=cparams(("arbitrary",)),
        name="sample_front",
    )(*front_in, *prev_conv)

    n_steps = SSD_HEADS
    per_group = SSD_HEADS // SSD_GROUPS
    gla_head = lambda j: jnp.minimum(j, GLA_HEADS - 1)
    blk3 = lambda n, f: pl.BlockSpec((seq, n, n_seq), lambda j: (0, f(j), 0))
    state_in = [xt, bt, ct, at, qt, kt, egt, vt, qn, kn, vn, ssm_in, gla_in, kc_in, vc_in, buckets[0], buckets[1],
                rel, sinks]
    ssm_spec = pl.BlockSpec((1, 1) + ssm_in.shape[2:], lambda j: (layer, j, 0, 0))
    gla_spec = pl.BlockSpec((1, 1) + gla_in.shape[2:], lambda j: (layer, gla_head(j), 0, 0))
    kv_spec = pl.BlockSpec((1, SEQ_PER_STEP) + kc_in.shape[2:], lambda j: (layer, j, 0, 0))
    state_specs = [blk3(SSD_HEAD_DIM, lambda j: j), blk3(SSD_STATE, lambda j: j // per_group),
                   blk3(SSD_STATE, lambda j: j // per_group), _const_spec(at.shape),
                   blk3(GLA_DK, gla_head), blk3(GLA_DK, gla_head), blk3(GLA_DK, gla_head), blk3(GLA_DV, gla_head),
                   _const_spec(qn.shape), _const_spec(kn.shape), _const_spec(vn.shape),
                   ssm_spec, gla_spec, kv_spec, kv_spec,
                   _const_spec(buckets[0].shape), _const_spec(buckets[1].shape), _smem_spec(), _smem_spec()]
    state_out = (f32(*ssm_in.shape), f32(*gla_in.shape), f32(*kc_in.shape), f32(*vc_in.shape),
                 f32(seq, SSD_WIDTH, n_seq), f32(seq, GLA_WIDTH, n_seq), f32(seq, n_seq, SWA_WIDTH))
    ssm_o, gla_o, ko, vo, yt, ot, oswa = pl.pallas_call(
        functools.partial(_sample_state_kernel, seq, len(prev_rest)),
        grid=(n_steps,),
        in_specs=state_specs + [pl.BlockSpec(memory_space=pl.ANY)] * len(prev_rest),
        out_specs=(ssm_spec, gla_spec, kv_spec, kv_spec, blk3(SSD_HEAD_DIM, lambda j: j), blk3(GLA_DV, gla_head),
                   pl.BlockSpec((seq, SEQ_PER_STEP, SWA_WIDTH), lambda j: (0, j, 0))),
        out_shape=state_out,
        input_output_aliases={len(state_in) + k: k for k in range(len(prev_rest))},
        scratch_shapes=[pltpu.VMEM((SEQ_PER_STEP * seq, LANES), F32)] * 6
        + [pltpu.VMEM((SWA_HEADS * seq, LANES), F32)] * 2,
        compiler_params=cparams(("arbitrary",)),
        name="sample_state",
    )(*state_in, *prev_rest)

    back_in = [yt, ot, oswa, xs, gates, h, p_all, dsk, snw, gnw, wout, wpe, wpg]
    back_specs = [pos_blk(SSD_WIDTH), pos_blk(GLA_WIDTH),
                  pl.BlockSpec((per_step, n_seq, SWA_WIDTH), lambda s: (s, 0, 0)),
                  row_blk(SSD_WIDTH), row_blk(D_MODEL), row_blk(D_MODEL),
                  pl.BlockSpec((1, step_rows, PLE_DIM), lambda s: (layer, s, 0))] + [
        _layer_spec(a, layer) for a in back_in[7:]]
    y = pl.pallas_call(
        _sample_back_kernel,
        grid=(seq // per_step,),
        in_specs=back_specs,
        out_specs=row_blk(D_MODEL),
        out_shape=f32(rows, D_MODEL),
        scratch_shapes=[pltpu.VMEM((step_rows, D_MODEL), BF16)],
        compiler_params=cparams(("arbitrary",)),
        name="sample_back",
    )(*back_in)
    return y, (conv_o, ssm_o, gla_o, ko, vo)


SWA_HEAD_ORDER = (0, 2, 1, 3)


def _win_tile_runs():
    sizes = (SSD_WIDTH, SSD_CONV_DIM, SSD_HEADS, GLA_HEADS * GLA_DK, GLA_HEADS * GLA_DK, GLA_WIDTH, GLA_WIDTH,
             GLA_RANK, SWA_WIDTH, SWA_KV_HEADS * SWA_HEAD_DIM, SWA_KV_HEADS * SWA_HEAD_DIM, SWA_WIDTH)
    offs = np.concatenate([[0], np.cumsum(sizes)])
    seg = lambda k: np.arange(offs[k], offs[k + 1])
    z, xbc, dt, gq, gk, gv, gg, glr, sq, sk, sv, sg = [seg(k) for k in range(len(sizes))]
    heads = lambda a: np.concatenate([a[h * SWA_HEAD_DIM:(h + 1) * SWA_HEAD_DIM] for h in SWA_HEAD_ORDER])
    pad = np.full(LANES - SSD_HEADS - GLA_RANK, -1)
    src = np.concatenate([xbc, z, gq, gk, gv, gg, heads(sq), sk, sv, heads(sg), dt, glr, pad])
    assert src.size == XBC_W + PROJ_W
    tiles = []
    for j in range(src.size // LANES):
        idx = src[j * LANES:(j + 1) * LANES]
        cuts = [0] + [k for k in range(1, LANES) if (idx[k] != idx[k - 1] + 1 and not (idx[k] == -1 == idx[k - 1]))]
        runs = [(int(idx[a]), b - a) for a, b in zip(cuts, cuts[1:] + [LANES])]
        assert all(n % SUBLANES == 0 and (s < 0 or s % SUBLANES == 0) for s, n in runs)
        tiles.append(runs)
    return tiles


def _win_prep_kernel(tile_runs, wt_ref, out_ref):
    for j, runs in enumerate(tile_runs):
        parts = [jnp.zeros((n, D_MODEL), F32) if s < 0 else wt_ref[0, s:s + n, :] for s, n in runs]
        tile = parts[0] if len(parts) == 1 else jnp.concatenate(parts, axis=0)
        out_ref[0, :, j * LANES:(j + 1) * LANES] = tile.T.astype(BF16)


def _prepare_w_in(w_in):
    depth, d_model, d_in = w_in.shape
    w_t = jnp.swapaxes(w_in, 1, 2)
    return pl.pallas_call(
        functools.partial(_win_prep_kernel, _win_tile_runs()),
        grid=(depth,),
        in_specs=[pl.BlockSpec((1, d_in, d_model), lambda l: (l, 0, 0))],
        out_specs=pl.BlockSpec((1, d_model, XBC_W + PROJ_W), lambda l: (l, 0, 0)),
        out_shape=jax.ShapeDtypeStruct((depth, d_model, XBC_W + PROJ_W), BF16),
        compiler_params=pltpu.CompilerParams(
            dimension_semantics=("arbitrary",), vmem_limit_bytes=VMEM_LIMIT_BYTES),
        name="w_in_prep",
    )(w_t)


def _prepare_weights(norm_w, w_in, conv_w, conv_b, dt_bias, a_log, d_skip, ssd_norm_w, gla_w_gk, gla_b_gk,
                     gla_norm_w, q_norm_w, k_norm_w, w_out, w_pe, w_pg):
    mix_w = SSD_WIDTH + GLA_WIDTH
    w_out_p = jnp.concatenate(
        [w_out[:, :mix_w, :]] + [w_out[:, mix_w + h * SWA_HEAD_DIM:mix_w + (h + 1) * SWA_HEAD_DIM, :]
                                 for h in SWA_HEAD_ORDER], axis=1).astype(BF16)
    lane_pad = lambda x: jnp.pad(x, ((0, 0), (0, LANES - x.shape[-1])))[:, None, :]
    wgk_p = jnp.pad(gla_w_gk, ((0, 0), (LR_LANE0, LANES - LR_LANE0 - GLA_RANK), (0, 0))).astype(BF16)
    return dict(
        norm_w=norm_w[:, None, :], w_in=_prepare_w_in(w_in), conv_w=conv_w, conv_b=conv_b[:, None, :],
        dt_bias=lane_pad(dt_bias), a_log=lane_pad(a_log),
        d_skip=jnp.repeat(d_skip, SSD_HEAD_DIM, axis=-1)[:, None, :], ssd_norm_w=ssd_norm_w[:, None, :],
        gla_w_gk=wgk_p, gla_b_gk=gla_b_gk[:, None, :],
        gla_norm_w=jnp.tile(gla_norm_w, (1, GLA_HEADS))[:, None, :],
        q_norm_w=jnp.tile(q_norm_w, (1, SWA_HEADS))[:, None, :],
        k_norm_w=jnp.tile(k_norm_w, (1, SWA_KV_HEADS))[:, None, :],
        w_out=w_out_p, w_pe=w_pe.astype(BF16), w_pg=w_pg.astype(BF16))


def kernel(x_prompt, x_sample, state_ssm, state_conv, state_gla, cache_swa_k, cache_swa_v, p_prompt, p_sample, rel_bias, norm_w, w_in, conv_w, conv_b, dt_bias, a_log, d_skip, ssd_norm_w, gla_w_gk, gla_b_gk, gla_norm_w, q_norm_w, k_norm_w, attn_sinks, w_out, w_pe, w_pg):
    depth = w_in.shape[0]
    bp, seq_p, _ = x_prompt.shape
    bs, seq_s, _ = x_sample.shape
    assert seq_s == SUBLANES and BLK % seq_s == 0 and (bs * seq_s) % BLK == 0
    assert cache_swa_k.shape[2] == WINDOW
    w = _prepare_weights(norm_w, w_in, conv_w, conv_b, dt_bias, a_log, d_skip, ssd_norm_w, gla_w_gk, gla_b_gk,
                         gla_norm_w, q_norm_w, k_norm_w, w_out, w_pe, w_pg)
    rel_flat = rel_bias.reshape(-1)
    dist_p = WINDOW + np.arange(BLK)[:, None] - np.arange(2 * BLK)[None, :]
    bucket_p = jnp.asarray(_bucket_table(dist_p))
    t_of_row = np.tile(np.arange(seq_s), SWA_HEADS)[:, None]
    bucket_c = jnp.asarray(_bucket_table(WINDOW + t_of_row - np.arange(WINDOW)[None, :]))
    dist_n = np.where(np.arange(LANES)[None, :] < seq_s, t_of_row - np.arange(LANES)[None, :], -1)
    bucket_n = jnp.asarray(_bucket_table(dist_n))

    ssm_in = jnp.transpose(state_ssm, (0, 2, 3, 4, 1)).reshape(depth, SSD_HEADS, SSD_HEAD_DIM * SSD_STATE, bs)
    gla_in = jnp.transpose(state_gla, (0, 2, 3, 4, 1)).reshape(depth, GLA_HEADS, GLA_DK * GLA_DV, bs)
    kv_in = lambda a: jnp.transpose(a, (0, 1, 3, 4, 2)).reshape(depth, bs, SWA_KV_HEADS * SWA_HEAD_DIM, WINDOW)
    kc_in, vc_in = kv_in(cache_swa_k), kv_in(cache_swa_v)
    conv_in = jnp.transpose(state_conv, (0, 2, 1, 3))

    hp = x_prompt.reshape(bp * seq_p, D_MODEL)
    p_prompt_rows = p_prompt.reshape(depth, bp * seq_p, PLE_DIM)
    hs = jnp.transpose(x_sample, (1, 0, 2)).reshape(seq_s * bs, D_MODEL)
    p_sample_rows = jnp.transpose(p_sample, (0, 2, 1, 3)).reshape(depth, seq_s * bs, PLE_DIM)
    states_p = ()
    states_s = ()
    for i in range(depth):
        wops, wspecs = _layer_weights(i, w)
        hp, *states_p = _prompt_layer(i, depth, bp, hp, p_prompt_rows, tuple(states_p), bucket_p, rel_flat,
                                      attn_sinks[i], wops, wspecs)
        hs, states_s = _sample_layer_native(i, depth, seq_s, hs, p_sample_rows, conv_in, ssm_in, gla_in, kc_in,
                                            vc_in, states_s, (bucket_c, bucket_n), rel_flat, attn_sinks[i], wops)
    ssm_p, conv_p, gla_p, kt_p, vt_p = states_p
    conv_s, ssm_s, gla_s, kt_s, vt_s = states_s
    unpack_kv = lambda a: jnp.transpose(
        a.reshape(a.shape[:2] + (SWA_KV_HEADS, SWA_HEAD_DIM, WINDOW)), (0, 1, 4, 2, 3))
    outs_p = (ssm_p.reshape(depth, bp, SSD_HEADS, SSD_HEAD_DIM, SSD_STATE), conv_p,
              gla_p.reshape(depth, bp, GLA_HEADS, GLA_DK, GLA_DV), unpack_kv(kt_p), unpack_kv(vt_p))
    seq_last = lambda a, dims: jnp.transpose(a.reshape(a.shape[:2] + dims + (bs,)), (0, 4, 1, 2, 3))
    outs_s = (seq_last(ssm_s, (SSD_HEAD_DIM, SSD_STATE)), jnp.transpose(conv_s, (0, 2, 1, 3)),
              seq_last(gla_s, (GLA_DK, GLA_DV)), unpack_kv(kt_s), unpack_kv(vt_s))
    y_sample = jnp.transpose(hs.reshape(seq_s, bs, D_MODEL), (1, 0, 2))
    return (hp.reshape(bp, seq_p, D_MODEL), y_sample) + outs_p + outs_s
```

```python
import functools
import math

import numpy as np
import jax
import jax.numpy as jnp
from jax import lax
from jax.experimental import pallas as pl
from jax.experimental.pallas import tpu as pltpu

D_MODEL = 1024
DEPTH = 2
SSD_HEADS = 8
SSD_HEAD_DIM = 64
SSD_WIDTH = SSD_HEADS * SSD_HEAD_DIM
SSD_GROUPS = 2
SSD_STATE = 64
SSD_CONV = 4
SSD_CONV_DIM = SSD_WIDTH + 2 * SSD_GROUPS * SSD_STATE
SSD_CHUNK = 128
GLA_HEADS = 4
GLA_DK = 32
GLA_DV = 64
GLA_WIDTH = GLA_HEADS * GLA_DV
GLA_RANK = 16
GLA_GATE_NORM = 16.0
GLA_CHUNK = 64
SWA_HEADS = 4
SWA_KV_HEADS = 2
SWA_HEAD_DIM = 64
SWA_WIDTH = SWA_HEADS * SWA_HEAD_DIM
WINDOW = 128
REL_BUCKETS = 32
REL_MAX_DIST = 128
PLE_DIM = 256
EPS = 1e-6

LANES = 128
SUBLANES = 8
HALF = LANES // 2
BLK = 128
VMEM_LIMIT_BYTES = 56 * 1024 * 1024

XBC_W = SSD_CONV_DIM
P_Z = 0
P_GQ = P_Z + SSD_WIDTH
P_GK = P_GQ + LANES
P_GV = P_GK + LANES
P_GG = P_GV + GLA_WIDTH
P_SQ = P_GG + GLA_WIDTH
P_SK = P_SQ + SWA_WIDTH
P_SV = P_SK + LANES
P_SG = P_SV + LANES
P_DTLR = P_SG + SWA_WIDTH
PROJ_W = P_DTLR + LANES
LR_LANE0 = SSD_HEADS

F32 = jnp.float32
BF16 = jnp.bfloat16
NEG_INF = float("-inf")
N_PROMPT_INPUTS = 22
PROMPT_CHUNK_ROWS = 2 * BLK
TICKS_PER_ITEM = 2
NT_DIMS = (((1,), (1,)), ((), ()))


def _iota(shape, dim):
    return lax.broadcasted_iota(jnp.int32, shape, dim)


def _div(x, d):
    return x >> (d.bit_length() - 1)


def _mod(x, d):
    return x & (d - 1)


def _softplus(x):
    e = jnp.exp(-jnp.abs(x))
    u = 1.0 + e
    d = u - 1.0
    log1p_e = jnp.where(d == 0.0, e, jnp.log(u) * (e / jnp.where(d == 0.0, 1.0, d)))
    return jnp.maximum(x, 0.0) + log1p_e


def _log_sigmoid(x):
    return jnp.minimum(x, 0.0) - jnp.log(1.0 + jnp.exp(-jnp.abs(x)))


def _silu(x):
    return x * jax.nn.sigmoid(x)


def _dot(a, b):
    return jnp.dot(a.astype(BF16), b.astype(BF16), preferred_element_type=F32)


def _dot_nt(a, b):
    return lax.dot_general(a.astype(BF16), b.astype(BF16), NT_DIMS, preferred_element_type=F32)


def _dot_exact(sel, x):
    x1 = x.astype(BF16)
    r1 = x - x1.astype(F32)
    x2 = r1.astype(BF16)
    x3 = (r1 - x2.astype(F32)).astype(BF16)
    dot = functools.partial(jnp.dot, sel, preferred_element_type=F32)
    return dot(x1) + dot(x2) + dot(x3)


def _expand_heads(x, n_heads):
    rows = x.shape[0]
    lo = _iota((rows, LANES), 1) < HALF
    tiles = []
    for j in range(n_heads // 2):
        a = jnp.broadcast_to(x[:, 2 * j:2 * j + 1], (rows, LANES))
        b = jnp.broadcast_to(x[:, 2 * j + 1:2 * j + 2], (rows, LANES))
        tiles.append(jnp.where(lo, a, b))
    return jnp.concatenate(tiles, axis=1)


def _head_rms_scale(x):
    rows, width = x.shape
    lo = _iota((rows, LANES), 1) < HALF
    outs = []
    for j in range(width // LANES):
        t = x[:, j * LANES:(j + 1) * LANES]
        sq = t * t
        s_lo = jnp.sum(jnp.where(lo, sq, 0.0), axis=-1, keepdims=True)
        s_hi = jnp.sum(jnp.where(lo, 0.0, sq), axis=-1, keepdims=True)
        outs.append(lax.rsqrt(jnp.where(lo, s_lo, s_hi) * (1.0 / HALF) + EPS))
    return outs[0] if len(outs) == 1 else jnp.concatenate(outs, axis=1)


def _group_rmsnorm(y, w):
    gw = SSD_WIDTH // SSD_GROUPS
    outs = []
    for g in range(SSD_GROUPS):
        t = y[:, g * gw:(g + 1) * gw]
        ms = jnp.sum(t * t, axis=-1, keepdims=True) * (1.0 / gw)
        outs.append(t * lax.rsqrt(ms + EPS))
    return jnp.concatenate(outs, axis=1) * w


def _rel_bucket_np(dist):
    n = np.maximum(dist, 0)
    exact = REL_BUCKETS // 2
    nf = np.maximum(n, 1).astype(np.float64)
    large = exact + (np.log(nf / exact) / math.log(REL_MAX_DIST / exact) * (REL_BUCKETS - exact)).astype(np.int32)
    large = np.minimum(large, REL_BUCKETS - 1)
    return np.where(n < exact, n, large).astype(np.int32)


def _bucket_table(dist):
    return np.where((dist >= 0) & (dist < WINDOW), _rel_bucket_np(dist), -1).astype(np.int32)


def _no_tick():
    pass


def _ssd_intra(xbc_c, dtv, acum, pair_mask, tick=_no_tick):
    xs = xbc_c[:, :SSD_WIDTH]
    bm = xbc_c[:, SSD_WIDTH:SSD_WIDTH + LANES]
    cm = xbc_c[:, SSD_WIDTH + LANES:]
    lane = _iota((BLK, LANES), 1)
    lo = lane < HALF
    acum_t = acum.T
    eacum = jnp.exp(acum)
    tail = jnp.exp(acum[BLK - 1:BLK, :] - acum)
    dtv_e = _expand_heads(dtv, SSD_HEADS)
    eacum_e = _expand_heads(eacum, SSD_HEADS)
    tail_e = _expand_heads(tail, SSD_HEADS)
    tick()
    xdt = xs * dtv_e
    xw = xdt * tail_e
    cb = [_dot_nt(jnp.where(lo, cm, 0.0), bm), _dot_nt(jnp.where(lo, 0.0, cm), bm)]
    y_pairs = []
    for j in range(SSD_HEADS // 2):
        tick()
        g = (2 * j) // (SSD_HEADS // SSD_GROUPS)
        ms = []
        for k in range(2):
            h = 2 * j + k
            seg = acum[:, h:h + 1] - acum_t[h:h + 1, :]
            dec = jnp.where(pair_mask, jnp.exp(seg), 0.0)
            ms.append((cb[g] * dec).astype(BF16))
        xp = xdt[:, j * LANES:(j + 1) * LANES]
        rhs = jnp.concatenate([jnp.where(lo, xp, 0.0), jnp.where(lo, 0.0, xp)], axis=0)
        y_pairs.append(_dot(jnp.concatenate(ms, axis=1), rhs))
    y_intra = jnp.concatenate(y_pairs, axis=1)
    return y_intra, xs, bm, cm, xw, eacum, eacum_e


def _gla_intra(gq, gk, gv, bcs, att_mask, tick=_no_tick):
    eb = jnp.exp(bcs)
    qe = gq * (GLA_DK ** -0.5) * eb
    ke = gk * jnp.exp(-bcs)
    btot = jnp.concatenate(
        [jnp.broadcast_to(bcs[(c2 + 1) * GLA_CHUNK - 1:(c2 + 1) * GLA_CHUNK, :], (GLA_CHUNK, LANES))
         for c2 in range(BLK // GLA_CHUNK)], axis=0)
    kd = gk * jnp.exp(btot - bcs)
    lane_k = _iota((GLA_CHUNK, LANES), 1)
    lane_v = _iota((GLA_CHUNK, GLA_WIDTH), 1)
    outs = []
    for c2 in range(BLK // GLA_CHUNK):
        tick()
        rs = slice(c2 * GLA_CHUNK, (c2 + 1) * GLA_CHUNK)
        ke_c = ke[rs]
        v_c = gv[rs]
        kbd = jnp.concatenate(
            [jnp.where(_div(lane_k, GLA_DK) == h, ke_c, 0.0) for h in range(GLA_HEADS)], axis=0)
        att = _dot_nt(qe[rs], kbd)
        att = jnp.where(att_mask, att, 0.0)
        vbd = jnp.concatenate(
            [jnp.where(_div(lane_v, GLA_DV) == h, v_c, 0.0) for h in range(GLA_HEADS)], axis=0)
        outs.append(_dot(att, vbd))
    return jnp.concatenate(outs, axis=0), qe, kd, jnp.exp(btot)


def _build_bias(bucket, rel_ref):
    accs = [jnp.full(bucket.shape, NEG_INF, F32) for _ in range(SWA_HEADS)]
    for b in range(REL_BUCKETS):
        hit = bucket == b
        for h in range(SWA_HEADS):
            accs[h] = jnp.where(hit, rel_ref[b * SWA_HEADS + h], accs[h])
    return accs


def _epilogue(h, mix, p, wout_ref, wpg_ref, wpe_ref):
    h1 = h + jnp.dot(mix, wout_ref[...], preferred_element_type=F32)
    gate = jax.nn.sigmoid(jnp.dot(h1.astype(BF16), wpg_ref[...], preferred_element_type=F32))
    pe = jnp.dot(p.astype(BF16), wpe_ref[...], preferred_element_type=F32)
    return h1 + gate * pe


def _prompt_kernel(chunks_per_seq, n_aliased, *refs):
    (ha_ref, hc_ref, p_ref, bucket_ref, rel_ref, sink_ref, nw_ref, win_ref, cw_ref, cb_ref, dtb_ref,
     alog_ref, dsk_ref, snw_ref, wgk_ref, bgk_ref, gnw_ref, qnw_ref, knw_ref, wout_ref,
     wpe_ref, wpg_ref) = refs[:N_PROMPT_INPUTS]
    (y_ref, ssm_ref, conv_ref, gla_ref, ko_ref, vo_ref,
     proj_e, proj_o, xbc_e, xbc_o, mix_e, mix_o, u_s, h1_s, h1b_s, hist_s, st_s, s2_s, kext_s, vext_s,
     bias_s, hist_snap, st_snap, s2_snap, k_snap, v_snap) = refs[N_PROMPT_INPUTS + n_aliased:]
    k_idx = pl.program_id(0)

    @pl.when(k_idx == 0)
    def _():
        accs = _build_bias(bucket_ref[...], rel_ref)
        own_block = _iota((BLK, 2 * BLK), 1) >= BLK
        for hh in range(SWA_HEADS):
            bias_s[hh] = accs[hh]
            bias_s[SWA_HEADS + hh] = jnp.where(own_block, accs[hh], NEG_INF)
        for ref in (proj_o, xbc_o, mix_e, mix_o, hist_s, st_s, s2_s, kext_s, vext_s):
            ref[...] = jnp.zeros(ref.shape, ref.dtype)

    row = _iota((BLK, BLK), 0)
    col = _iota((BLK, BLK), 1)
    causal = row >= col
    tri = jnp.where(causal, 1.0, 0.0).astype(BF16)
    lo = col < HALF
    lane_row = _iota((1, LANES), 1)
    a_row = jnp.where(lane_row < SSD_HEADS, -jnp.exp(alog_ref[...]), 0.0)
    bd_mask = _div(_iota((LANES, GLA_WIDTH), 0), GLA_DK) == _div(_iota((LANES, GLA_WIDTH), 1), GLA_DV)
    att_t = _iota((GLA_CHUNK, GLA_WIDTH), 0)
    att_s = _mod(_iota((GLA_CHUNK, GLA_WIDTH), 1), GLA_CHUNK)
    att_mask = att_s <= att_t
    lo2 = _iota((2 * BLK, LANES), 1) < HALF

    group_w = SSD_WIDTH // SSD_GROUPS

    def block(blk, proj_s, xbc_s, mix_s, starts_sequence, tick):
        rows = slice(blk * BLK, (blk + 1) * BLK)
        cw = cw_ref[...]
        tick()
        if blk == 0:
            xwin = jnp.concatenate([hist_s[...], xbc_s[0:BLK, :]], axis=0)
        else:
            xwin = xbc_s[blk * BLK - SUBLANES:(blk + 1) * BLK, :]
        acc = xwin[SUBLANES - 3:SUBLANES - 3 + BLK, :] * cw[0:1, :]
        for k in range(1, SSD_CONV):
            acc = acc + xwin[SUBLANES - 3 + k:SUBLANES - 3 + k + BLK, :] * cw[k:k + 1, :]
        xbc_c = _silu(acc + cb_ref[...])
        tick()
        dtlr = proj_s[rows, P_DTLR:P_DTLR + LANES]
        dtv = _softplus(dtlr + dtb_ref[...])
        glog = _log_sigmoid(_dot(dtlr, wgk_ref[...]) + bgk_ref[...]) * (1.0 / GLA_GATE_NORM)
        sums = _dot_exact(tri, jnp.concatenate([dtv * a_row, glog], axis=1))
        acum = sums[:, :LANES]
        gsum = sums[:, LANES:]
        bcs = gsum
        for c2 in range(1, BLK // GLA_CHUNK):
            before = gsum[c2 * GLA_CHUNK - 1:c2 * GLA_CHUNK, :]
            bcs = jnp.where(_div(row, GLA_CHUNK) == c2, gsum - before, bcs)
        tick()
        y_intra, xs, bm, cm, xw, eacum, eacum_e = _ssd_intra(xbc_c, dtv, acum, causal, tick)
        tick()
        st = st_s[...]
        y = y_intra + _dot(cm, st) * eacum_e + dsk_ref[...] * xs
        bm_t = bm.T
        for g in range(SSD_GROUPS):
            gr = slice(g * SSD_STATE, (g + 1) * SSD_STATE)
            gc = slice(g * group_w, (g + 1) * group_w)
            st_s[gr, gc] = st[gr, gc] * eacum_e[BLK - 1:BLK, gc] + _dot(bm_t[gr, :], xw[:, gc])
        tick()
        y = y * _silu(proj_s[rows, P_Z:P_Z + SSD_WIDTH])
        mix_s[rows, 0:SSD_WIDTH] = _group_rmsnorm(y, snw_ref[...]).astype(BF16)
        tick()
        gk = proj_s[rows, P_GK:P_GK + LANES]
        gv = proj_s[rows, P_GV:P_GV + GLA_WIDTH]
        o_intra, qe, kd, ebt = _gla_intra(proj_s[rows, P_GQ:P_GQ + LANES], gk, gv, bcs, att_mask, tick)
        kd_t = kd.T
        ebt_t = ebt.T
        s2 = s2_s[...]
        o_parts = []
        for c2 in range(BLK // GLA_CHUNK):
            tick()
            rs = slice(c2 * GLA_CHUNK, (c2 + 1) * GLA_CHUNK)
            o_parts.append(o_intra[rs] + _dot(qe[rs], s2))
            u2 = _dot(jnp.where(_div(col, GLA_CHUNK) == c2, kd_t, 0.0), gv)
            last = (c2 + 1) * GLA_CHUNK - 1
            s2 = s2 * ebt_t[:, last:last + 1] + jnp.where(bd_mask, u2, 0.0)
        s2_s[...] = s2
        o = jnp.concatenate(o_parts, axis=0)
        y_gla = o * _head_rms_scale(o) * gnw_ref[...] * _silu(proj_s[rows, P_GG:P_GG + GLA_WIDTH])
        mix_s[rows, SSD_WIDTH:SSD_WIDTH + GLA_WIDTH] = y_gla.astype(BF16)
        tick()
        sq = proj_s[rows, P_SQ:P_SQ + SWA_WIDTH]
        qn = sq * _head_rms_scale(sq) * qnw_ref[...] * (SWA_HEAD_DIM ** -0.5)
        sk = proj_s[rows, P_SK:P_SK + LANES]
        kn = sk * _head_rms_scale(sk) * knw_ref[...]
        vn = proj_s[rows, P_SV:P_SV + LANES]
        kext_s[BLK:2 * BLK, :] = kn
        vext_s[BLK:2 * BLK, :] = vn
        kext = kext_s[...]
        vext = vext_s[...]
        qa = qn[:, :LANES]
        qb = qn[:, LANES:]
        qs = jnp.concatenate([jnp.where(lo, qa, 0.0), jnp.where(lo, qb, 0.0),
                              jnp.where(lo, 0.0, qa), jnp.where(lo, 0.0, qb)], axis=0)
        logits = _dot_nt(qs, kext)
        tick()
        if blk == 0 and starts_sequence is not False:
            bias_row0 = jnp.where(starts_sequence, SWA_HEADS, 0)
        else:
            bias_row0 = 0
        es = []
        invs = []
        for hh in range(SWA_HEADS):
            tick()
            sink = sink_ref[hh]
            l = logits[hh * BLK:(hh + 1) * BLK] + bias_s[bias_row0 + hh]
            m = jnp.maximum(jnp.max(l, axis=-1, keepdims=True), sink)
            e = jnp.exp(l - m)
            den = jnp.sum(e, axis=-1, keepdims=True) + jnp.exp(sink - m)
            es.append(e.astype(BF16))
            invs.append(1.0 / den)
        v_stack = jnp.concatenate([jnp.where(lo2, vext, 0.0), jnp.where(lo2, 0.0, vext)], axis=0)
        tile_a = _dot(jnp.concatenate([es[0], es[2]], axis=1), v_stack) * jnp.where(lo, invs[0], invs[2])
        tile_b = _dot(jnp.concatenate([es[1], es[3]], axis=1), v_stack) * jnp.where(lo, invs[1], invs[3])
        oa = jnp.concatenate([tile_a, tile_b], axis=1)
        y_swa = oa * _silu(proj_s[rows, P_SG:P_SG + SWA_WIDTH])
        mix_s[rows, SSD_WIDTH + GLA_WIDTH:] = y_swa.astype(BF16)
        kext_s[0:BLK, :] = kn
        vext_s[0:BLK, :] = vn

    chunk = proj_e.shape[0]

    def project_items(rows, proj_s, xbc_s):
        def norm():
            h = ha_ref[rows, :]
            ms = jnp.mean(h * h, axis=-1, keepdims=True)
            u_s[...] = (h * lax.rsqrt(ms + EPS) * nw_ref[...]).astype(BF16)

        def cols(dst, lo_c, hi_c, w_off):
            def item():
                dst[:, lo_c:hi_c] = jnp.dot(u_s[...], win_ref[:, w_off + lo_c:w_off + hi_c],
                                            preferred_element_type=F32)
            return item

        step = 2 * LANES
        items = [norm]
        items += [cols(xbc_s, c, min(c + step, XBC_W), 0) for c in range(0, XBC_W, step)]
        items += [cols(proj_s, c, min(c + step, PROJ_W), XBC_W) for c in range(0, PROJ_W, step)]
        return items

    def epilogue_items(rows, mix_s):
        half_w = 2 * LANES

        def residual(c):
            def item():
                h1 = hc_ref[rows, c:c + half_w] + jnp.dot(
                    mix_s[...], wout_ref[:, c:c + half_w], preferred_element_type=F32)
                h1_s[:, c:c + half_w] = h1
                h1b_s[:, c:c + half_w] = h1.astype(BF16)
            return item

        def gated(c):
            def item():
                gate = jax.nn.sigmoid(jnp.dot(h1b_s[...], wpg_ref[:, c:c + half_w], preferred_element_type=F32))
                pe = jnp.dot(p_ref[0, rows, :].astype(BF16), wpe_ref[:, c:c + half_w],
                             preferred_element_type=F32)
                y_ref[rows, c:c + half_w] = h1_s[:, c:c + half_w] + gate * pe
            return item

        col0 = range(0, D_MODEL, half_w)
        return [residual(c) for c in col0] + [gated(c) for c in col0]

    def merge(first, second):
        out = list(first)
        for j, item in enumerate(second):
            out.insert(((j + 1) * len(first)) // len(second) + j, item)
        return out

    def mixer(proj_s, xbc_s, mix_s, starts_sequence, items):
        if starts_sequence is not False:
            keep = jnp.where(starts_sequence, 0.0, 1.0)
            for ref in (hist_s, st_s, s2_s):
                ref[...] = ref[...] * keep
            kext_s[0:BLK, :] = kext_s[0:BLK, :] * keep
            vext_s[0:BLK, :] = vext_s[0:BLK, :] * keep
        queue = list(items)
        calls = [0]

        def tick():
            calls[0] += 1
            if queue and calls[0] % TICKS_PER_ITEM == 0:
                queue.pop(0)()

        for blk in range(chunk // BLK):
            block(blk, proj_s, xbc_s, mix_s, starts_sequence, tick)
        while queue:
            queue.pop(0)()
        hist_s[...] = xbc_s[chunk - SUBLANES:chunk, :]

    def snapshot_states():
        st_snap[...] = st_s[...]
        s2_snap[...] = s2_s[...]
        hist_snap[...] = hist_s[...]
        k_snap[...] = kext_s[0:BLK, :]
        v_snap[...] = vext_s[0:BLK, :]

    def write_states():
        st = st_snap[...]
        stc = st[:SSD_STATE] + st[SSD_STATE:]
        ssm_ref[0, 0] = jnp.concatenate([stc, stc], axis=0).T[:, :SSD_STATE]
        conv_ref[0, 0] = hist_snap[SUBLANES - (SSD_CONV - 1):SUBLANES, :]
        s2 = s2_snap[...]
        w = s2[:, :LANES] + s2[:, LANES:]
        gla_ref[0, 0] = w[:, :GLA_DV] + w[:, GLA_DV:]
        ko_ref[0, 0] = k_snap[...].T
        vo_ref[0, 0] = v_snap[...].T

    even = slice(0, chunk)
    odd = slice(chunk, 2 * chunk)
    mixer(proj_o, xbc_o, mix_o, False, merge(project_items(even, proj_e, xbc_e), epilogue_items(even, mix_e)))
    snapshot_states()
    mixer(proj_e, xbc_e, mix_e, _mod(2 * k_idx, chunks_per_seq) == 0,
          merge(project_items(odd, proj_o, xbc_o), epilogue_items(odd, mix_o)))

    @pl.when((k_idx >= 1) & (_mod(2 * k_idx - 1, chunks_per_seq) == chunks_per_seq - 1))
    def _():
        write_states()


N_FRONT_INPUTS = 12
SAMPLE_POSITIONS_PER_STEP = 2


def _sample_front_kernel(seq, n_aliased, *refs):
    (h_ref, cst_ref, nw_ref, win_ref, cw_ref, cb_ref, dtb_ref, alog_ref, wgk_ref, bgk_ref,
     qnw_ref, knw_ref) = refs[:N_FRONT_INPUTS]
    (xs_ref, gates_ref, xt_ref, bt_ref, ct_ref, at_ref, qt_ref, kt_ref, egt_ref, vt_ref,
     qn_ref, kn_ref, vn_ref, conv_ref, u_s, xbc_s, proj_s) = refs[N_FRONT_INPUTS + n_aliased:]
    n_seq = BLK
    per_step = h_ref.shape[0] // n_seq
    step = pl.program_id(0)
    ht = h_ref[...]
    ms = jnp.mean(ht * ht, axis=-1, keepdims=True)
    u_s[...] = (ht * lax.rsqrt(ms + EPS) * nw_ref[...]).astype(BF16)
    xbc_s[pl.ds(pl.multiple_of(step * per_step * n_seq, n_seq), per_step * n_seq), :] = jnp.dot(
        u_s[...], win_ref[:, :XBC_W], preferred_element_type=F32)
    proj_s[...] = jnp.dot(u_s[...], win_ref[:, XBC_W:], preferred_element_type=F32)
    cw = cw_ref[...]
    a_row = jnp.where(_iota((1, LANES), 1) < SSD_HEADS, -jnp.exp(alog_ref[...]), 0.0)
    for i in range(per_step):
        t = step * per_step + i
        rows = slice(i * n_seq, (i + 1) * n_seq)

        def raw_xbc(back):
            cur = xbc_s[pl.ds(pl.multiple_of(jnp.maximum(t - back, 0) * n_seq, n_seq), n_seq), :]
            if back == 0:
                return cur
            old = cst_ref[0, jnp.clip(SSD_CONV - 1 + t - back, 0, SSD_CONV - 2)]
            return jnp.where(t >= back, cur, old)

        acc = raw_xbc(SSD_CONV - 1) * cw[0:1, :]
        for k in range(1, SSD_CONV):
            acc = acc + raw_xbc(SSD_CONV - 1 - k) * cw[k:k + 1, :]
        xbc_c = _silu(acc + cb_ref[...])
        xs = xbc_c[:, :SSD_WIDTH]
        dtlr = proj_s[rows, P_DTLR:P_DTLR + LANES]
        dtv = _softplus(dtlr + dtb_ref[...])
        xs_ref[rows, :] = xs
        xt_ref[i] = (xs * _expand_heads(dtv, SSD_HEADS)).T
        bt_ref[i] = xbc_c[:, SSD_WIDTH:SSD_WIDTH + LANES].T
        ct_ref[i] = xbc_c[:, SSD_WIDTH + LANES:].T
        at_ref[i] = jnp.exp(dtv * a_row).T[:SSD_HEADS, :]
        glog = _log_sigmoid(_dot(dtlr, wgk_ref[...]) + bgk_ref[...]) * (1.0 / GLA_GATE_NORM)
        qt_ref[i] = (proj_s[rows, P_GQ:P_GQ + LANES] * (GLA_DK ** -0.5)).T
        kt_ref[i] = proj_s[rows, P_GK:P_GK + LANES].T
        egt_ref[i] = jnp.exp(glog).T
        vt_ref[i] = proj_s[rows, P_GV:P_GV + GLA_WIDTH].T
        sq = proj_s[rows, P_SQ:P_SQ + SWA_WIDTH]
        qn_ref[rows, :] = sq * _head_rms_scale(sq) * qnw_ref[...] * (SWA_HEAD_DIM ** -0.5)
        sk = proj_s[rows, P_SK:P_SK + LANES]
        kn_ref[rows, :] = sk * _head_rms_scale(sk) * knw_ref[...]
        vn_ref[rows, :] = proj_s[rows, P_SV:P_SV + LANES]
        gates_ref[rows, :] = jnp.concatenate(
            [_silu(proj_s[rows, P_Z:P_Z + SSD_WIDTH]), _silu(proj_s[rows, P_GG:P_GG + GLA_WIDTH]),
             _silu(proj_s[rows, P_SG:P_SG + SWA_WIDTH])], axis=1)
        first_kept = seq - (SSD_CONV - 1)

        @pl.when(t >= first_kept)
        def _():
            conv_ref[0, jnp.maximum(t - first_kept, 0)] = raw_xbc(0)


N_STATE_INPUTS = 19
SEQ_PER_STEP = 16


def _sample_state_kernel(seq, n_aliased, *refs):
    (xt_ref, bt_ref, ct_ref, at_ref, qt_ref, kt_ref, egt_ref, vt_ref, qn_ref, kn_ref, vn_ref,
     ssm_ref, gla_ref, kc_ref, vc_ref, bucket_c_ref, bucket_n_ref, rel_ref, sink_ref) = refs[:N_STATE_INPUTS]
    (ssm_o, gla_o, ko_ref, vo_ref, yt_ref, ot_ref, oswa_ref,
     qa_s, qb_s, krow_s, vrow_s, oa_s, ob_s, biasc_s, biasn_s) = refs[N_STATE_INPUTS + n_aliased:]
    j = pl.program_id(0)
    n_seq = LANES
    head_of_row = _div(_iota((SWA_HEADS * seq, LANES), 0), seq)

    def by_head(values):
        out = values[SWA_HEADS - 1]
        for hh in range(SWA_HEADS - 2, -1, -1):
            out = jnp.where(head_of_row == hh, values[hh], out)
        return out

    @pl.when(j == 0)
    def _():
        biasc_s[...] = by_head(_build_bias(bucket_c_ref[...], rel_ref))
        biasn_s[...] = by_head(_build_bias(bucket_n_ref[...], rel_ref))

    sub = _iota((SUBLANES, LANES), 0)
    a_rows = [jnp.sum(jnp.where(sub == j, at_ref[t], 0.0), axis=0, keepdims=True) for t in range(seq)]

    def ssd_body(p8, carry):
        r8 = pl.multiple_of(p8 * SUBLANES, SUBLANES)
        x_tiles = [xt_ref[t, pl.ds(r8, SUBLANES), :] for t in range(seq)]
        y_rows = [[] for _ in range(seq)]
        for pp in range(SUBLANES):
            r64 = pl.multiple_of((p8 * SUBLANES + pp) * SSD_STATE, SSD_STATE)
            slab = ssm_ref[0, 0, pl.ds(r64, SSD_STATE), :]
            for t in range(seq):
                slab = slab * a_rows[t] + x_tiles[t][pp:pp + 1, :] * bt_ref[t]
                y_rows[t].append(jnp.sum(ct_ref[t] * slab, axis=0, keepdims=True))
            ssm_o[0, 0, pl.ds(r64, SSD_STATE), :] = slab
        for t in range(seq):
            yt_ref[t, pl.ds(r8, SUBLANES), :] = jnp.concatenate(y_rows[t], axis=0)
        return carry

    lax.fori_loop(0, SSD_HEAD_DIM // SUBLANES, ssd_body, 0)

    @pl.when(j < GLA_HEADS)
    def _():
        for t in range(seq):
            ot_ref[t] = jnp.zeros((GLA_DV, LANES), F32)

        def gla_body(d8, carry):
            r8 = pl.multiple_of(d8 * SUBLANES, SUBLANES)
            q_tiles = [qt_ref[t, pl.ds(r8, SUBLANES), :] for t in range(seq)]
            k_tiles = [kt_ref[t, pl.ds(r8, SUBLANES), :] for t in range(seq)]
            g_tiles = [egt_ref[t, pl.ds(r8, SUBLANES), :] for t in range(seq)]
            for dd in range(SUBLANES):
                r64 = pl.multiple_of((d8 * SUBLANES + dd) * GLA_DV, GLA_DV)
                slab = gla_ref[0, 0, pl.ds(r64, GLA_DV), :]
                for t in range(seq):
                    slab = slab * g_tiles[t][dd:dd + 1, :] + k_tiles[t][dd:dd + 1, :] * vt_ref[t]
                    ot_ref[t] = ot_ref[t] + q_tiles[t][dd:dd + 1, :] * slab
                gla_o[0, 0, pl.ds(r64, GLA_DV), :] = slab
            return carry

        lax.fori_loop(0, GLA_DK // SUBLANES, gla_body, 0)

    base = pl.multiple_of(j * SEQ_PER_STEP, SEQ_PER_STEP)
    for t in range(seq):
        src = pl.ds(t * n_seq + base, SEQ_PER_STEP)
        dst = pl.ds(t, SEQ_PER_STEP, stride=seq)
        qa_s[dst, :] = qn_ref[src, 0:LANES]
        qb_s[dst, :] = qn_ref[src, LANES:2 * LANES]
        krow_s[dst, :] = kn_ref[src, :]
        vrow_s[dst, :] = vn_ref[src, :]
    kn_t = krow_s[...].T
    vn_t = vrow_s[...].T
    keep_old = _iota((LANES, WINDOW), 1) < WINDOW - seq
    lo8 = _iota((seq, LANES), 1) < HALF
    sink_col = by_head([jnp.full((SWA_HEADS * seq, LANES), sink_ref[hh], F32) for hh in range(SWA_HEADS)])[:, 0:1]

    def swa_stages(bl):
        r8 = pl.multiple_of(bl * seq, seq)
        v = {}

        def logits():
            qa = qa_s[pl.ds(r8, seq), :]
            qb = qb_s[pl.ds(r8, seq), :]
            qs = jnp.concatenate([jnp.where(lo8, qa, 0.0), jnp.where(lo8, qb, 0.0),
                                  jnp.where(lo8, 0.0, qa), jnp.where(lo8, 0.0, qb)], axis=0)
            v["lc"] = _dot(qs, kc_ref[0, bl]) + biasc_s[...]
            v["ln"] = _dot_nt(qs, krow_s[pl.ds(r8, seq), :]) + biasn_s[:, 0:seq]

        def softmax():
            lc, ln = v["lc"], v["ln"]
            m = jnp.maximum(jnp.maximum(jnp.max(lc, axis=-1, keepdims=True), jnp.max(ln, axis=-1, keepdims=True)),
                            sink_col)
            v["ec"] = jnp.exp(lc - m)
            v["en"] = jnp.exp(ln - m)
            v["inv"] = 1.0 / (jnp.sum(v["ec"], axis=-1, keepdims=True) + jnp.sum(v["en"], axis=-1, keepdims=True)
                              + jnp.exp(sink_col - m))

        def values():
            o = (_dot_nt(v["ec"], vc_ref[0, bl]) + _dot(v["en"], vrow_s[pl.ds(r8, seq), :])) * v["inv"]
            oa_s[pl.ds(r8, seq), :] = jnp.where(lo8, o[0:seq], o[2 * seq:3 * seq])
            ob_s[pl.ds(r8, seq), :] = jnp.where(lo8, o[seq:2 * seq], o[3 * seq:4 * seq])

        def window():
            ko_ref[0, bl] = jnp.where(keep_old, pltpu.roll(kc_ref[0, bl], WINDOW - seq, axis=1),
                                      pltpu.roll(kn_t, WINDOW - seq - r8, axis=1))
            vo_ref[0, bl] = jnp.where(keep_old, pltpu.roll(vc_ref[0, bl], WINDOW - seq, axis=1),
                                      pltpu.roll(vn_t, WINDOW - seq - r8, axis=1))

        return [logits, softmax, values, window]

    all_stages = [swa_stages(bl) for bl in range(SEQ_PER_STEP)]
    for k in range(len(all_stages[0])):
        for stages in all_stages:
            stages[k]()
    for t in range(seq):
        src = pl.ds(t, SEQ_PER_STEP, stride=seq)
        oswa_ref[t] = jnp.concatenate([oa_s[src, :], ob_s[src, :]], axis=1)


def _sample_back_kernel(yt_ref, ot_ref, oswa_ref, xs_ref, gates_ref, h_ref, p_ref, dsk_ref, snw_ref, gnw_ref,
                        wout_ref, wpe_ref, wpg_ref, y_ref, mix_s):
    n_seq = BLK
    for t in range(yt_ref.shape[0]):
        rows = slice(t * n_seq, (t + 1) * n_seq)
        y = (yt_ref[t].T + dsk_ref[...] * xs_ref[rows, :]) * gates_ref[rows, 0:SSD_WIDTH]
        mix_s[rows, 0:SSD_WIDTH] = _group_rmsnorm(y, snw_ref[...]).astype(BF16)
        o = ot_ref[t].T
        y_gla = o * _head_rms_scale(o) * gnw_ref[...] * gates_ref[rows, SSD_WIDTH:SSD_WIDTH + GLA_WIDTH]
        mix_s[rows, SSD_WIDTH:SSD_WIDTH + GLA_WIDTH] = y_gla.astype(BF16)
        mix_s[rows, SSD_WIDTH + GLA_WIDTH:] = (oswa_ref[t] * gates_ref[rows, SSD_WIDTH + GLA_WIDTH:]).astype(BF16)
    y_ref[...] = _epilogue(h_ref[...], mix_s[...], p_ref[0], wout_ref, wpg_ref, wpe_ref)


def _const_spec(shape):
    nd = len(shape)
    return pl.BlockSpec(shape, lambda *_: (0,) * nd)


def _smem_spec():
    return pl.BlockSpec(memory_space=pltpu.SMEM)


def _layer_spec(arr, layer):
    return pl.BlockSpec((None,) + arr.shape[1:], lambda *_: (layer, 0, 0), pipeline_mode=pl.Buffered(1))


def _layer_weights(layer, w):
    ops = [w[name] for name in ("norm_w", "w_in", "conv_w", "conv_b", "dt_bias", "a_log", "d_skip", "ssd_norm_w",
                                "gla_w_gk", "gla_b_gk", "gla_norm_w", "q_norm_w", "k_norm_w", "w_out", "w_pe",
                                "w_pg")]
    return ops, [_layer_spec(o, layer) for o in ops]


def _prompt_layer(layer, depth, bsz, h, p_all, prev_states, bucket, rel, sinks, wops, wspecs):
    rows_total, _ = h.shape
    seq_len = rows_total // bsz
    chunk = PROMPT_CHUNK_ROWS
    pair = 2 * chunk
    chunks_per_seq = seq_len // chunk
    assert seq_len % pair == 0 and chunks_per_seq & (chunks_per_seq - 1) == 0
    n_pairs = rows_total // pair
    kern = functools.partial(_prompt_kernel, chunks_per_seq, len(prev_states))
    proj_rows = pl.BlockSpec((pair, D_MODEL), lambda k: (jnp.minimum(k, n_pairs - 1), 0))
    out_rows = pl.BlockSpec((pair, D_MODEL), lambda k: (jnp.maximum(k - 1, 0), 0))
    p_spec = pl.BlockSpec((1, pair, PLE_DIM), lambda k: (layer, jnp.maximum(k - 1, 0), 0))
    per_seq = lambda s: pl.BlockSpec(
        (1, 1) + s, lambda k: (layer, jnp.maximum(2 * k - 1, 0) // chunks_per_seq) + (0,) * len(s))
    state_shapes = ((SSD_WIDTH, SSD_STATE), (SSD_CONV - 1, SSD_CONV_DIM), (GLA_HEADS * GLA_DK, GLA_DV),
                    (LANES, WINDOW), (LANES, WINDOW))
    out_shape = (jax.ShapeDtypeStruct((rows_total, D_MODEL), F32),) + tuple(
        jax.ShapeDtypeStruct((depth, bsz) + s, F32) for s in state_shapes)
    return pl.pallas_call(
        kern,
        grid=(n_pairs + 1,),
        in_specs=[proj_rows, out_rows, p_spec, _const_spec(bucket.shape), _smem_spec(), _smem_spec()]
        + wspecs + [pl.BlockSpec(memory_space=pl.ANY)] * len(prev_states),
        out_specs=(out_rows,) + tuple(per_seq(s) for s in state_shapes),
        out_shape=out_shape,
        input_output_aliases={N_PROMPT_INPUTS + k: 1 + k for k in range(len(prev_states))},
        scratch_shapes=[
            pltpu.VMEM((chunk, PROJ_W), F32), pltpu.VMEM((chunk, PROJ_W), F32),
            pltpu.VMEM((chunk, XBC_W), F32), pltpu.VMEM((chunk, XBC_W), F32),
            pltpu.VMEM((chunk, D_MODEL), BF16), pltpu.VMEM((chunk, D_MODEL), BF16),
            pltpu.VMEM((chunk, D_MODEL), BF16),
            pltpu.VMEM((chunk, D_MODEL), F32),
            pltpu.VMEM((chunk, D_MODEL), BF16),
            pltpu.VMEM((SUBLANES, XBC_W), F32),
            pltpu.VMEM((BLK, SSD_WIDTH), F32),
            pltpu.VMEM((LANES, GLA_WIDTH), F32),
            pltpu.VMEM((2 * BLK, LANES), F32),
            pltpu.VMEM((2 * BLK, LANES), F32),
            pltpu.VMEM((2 * SWA_HEADS, BLK, 2 * BLK), F32),
            pltpu.VMEM((SUBLANES, XBC_W), F32), pltpu.VMEM((BLK, SSD_WIDTH), F32),
            pltpu.VMEM((LANES, GLA_WIDTH), F32), pltpu.VMEM((BLK, LANES), F32),
            pltpu.VMEM((BLK, LANES), F32),
        ],
        compiler_params=pltpu.CompilerParams(
            dimension_semantics=("arbitrary",), vmem_limit_bytes=VMEM_LIMIT_BYTES),
        name="prompt_layer",
    )(h, h, p_all, bucket, rel, sinks, *wops, *prev_states)


def _whole(shape, layer=None):
    if layer is None:
        return pl.BlockSpec(shape, lambda *_: (0,) * len(shape), pipeline_mode=pl.Buffered(1))
    return pl.BlockSpec((1,) + shape[1:], lambda *_: (layer,) + (0,) * (len(shape) - 1),
                        pipeline_mode=pl.Buffered(1))


def _sample_layer_native(layer, depth, seq, h, p_all, conv_in, ssm_in, gla_in, kc_in, vc_in, prev_states,
                         buckets, rel, sinks, wops):
    (nw, win, cw, cb, dtb, alog, dsk, snw, wgk, bgk, gnw, qnw, knw, wout, wpe, wpg) = wops
    rows = h.shape[0]
    n_seq = rows // seq
    assert n_seq == LANES and n_seq % SEQ_PER_STEP == 0 and SSD_HEADS * SEQ_PER_STEP == n_seq
    prev_conv, prev_rest = (prev_states[:1], prev_states[1:]) if prev_states else ((), ())
    f32 = lambda *s: jax.ShapeDtypeStruct(s, F32)
    cparams = lambda sem: pltpu.CompilerParams(dimension_semantics=sem, vmem_limit_bytes=VMEM_LIMIT_BYTES)

    per_step = SAMPLE_POSITIONS_PER_STEP
    step_rows = per_step * n_seq
    assert seq % per_step == 0
    row_blk = lambda w: pl.BlockSpec((step_rows, w), lambda s: (s, 0))
    pos_blk = lambda n: pl.BlockSpec((per_step, n, n_seq), lambda s: (s, 0, 0))
    front_in = [h, conv_in, nw, win, cw, cb, dtb, alog, wgk, bgk, qnw, knw]
    front_specs = [row_blk(D_MODEL), _whole(conv_in.shape, layer)] + [_layer_spec(a, layer) for a in front_in[2:]]
    front_out = (f32(rows, SSD_WIDTH), f32(rows, D_MODEL),
                 f32(seq, SSD_WIDTH, n_seq), f32(seq, LANES, n_seq), f32(seq, LANES, n_seq),
                 f32(seq, SSD_HEADS, n_seq), f32(seq, LANES, n_seq), f32(seq, LANES, n_seq), f32(seq, LANES, n_seq),
                 f32(seq, GLA_WIDTH, n_seq), f32(rows, SWA_WIDTH), f32(rows, LANES), f32(rows, LANES),
                 f32(*conv_in.shape))
    front_out_specs = (row_blk(SSD_WIDTH), row_blk(D_MODEL), pos_blk(SSD_WIDTH), pos_blk(LANES), pos_blk(LANES),
                       pos_blk(SSD_HEADS), pos_blk(LANES), pos_blk(LANES), pos_blk(LANES), pos_blk(GLA_WIDTH),
                       row_blk(SWA_WIDTH), row_blk(LANES), row_blk(LANES), _whole(conv_in.shape, layer))
    (xs, gates, xt, bt, ct, at, qt, kt, egt, vt, qn, kn, vn, conv_o) = pl.pallas_call(
        functools.partial(_sample_front_kernel, seq, len(prev_conv)),
        grid=(seq // per_step,),
        in_specs=front_specs + [pl.BlockSpec(memory_space=pl.ANY)] * len(prev_conv),
        out_specs=front_out_specs,
        out_shape=front_out,
        input_output_aliases={len(front_in) + k: len(front_out) - 1 + k for k in range(len(prev_conv))},
        scratch_shapes=[pltpu.VMEM((step_rows, D_MODEL), BF16), pltpu.VMEM((rows, XBC_W), F32),
                        pltpu.VMEM((step_rows, PROJ_W), F32)],
        compiler_params=cparams(("arbitrary",)),
        name="sample_front",
    )(*front_in, *prev_conv)

    n_steps = SSD_HEADS
    per_group = SSD_HEADS // SSD_GROUPS
    gla_head = lambda j: jnp.minimum(j, GLA_HEADS - 1)
    blk3 = lambda n, f: pl.BlockSpec((seq, n, n_seq), lambda j: (0, f(j), 0))
    state_in = [xt, bt, ct, at, qt, kt, egt, vt, qn, kn, vn, ssm_in, gla_in, kc_in, vc_in, buckets[0], buckets[1],
                rel, sinks]
    ssm_spec = pl.BlockSpec((1, 1) + ssm_in.shape[2:], lambda j: (layer, j, 0, 0))
    gla_spec = pl.BlockSpec((1, 1) + gla_in.shape[2:], lambda j: (layer, gla_head(j), 0, 0))
    kv_spec = pl.BlockSpec((1, SEQ_PER_STEP) + kc_in.shape[2:], lambda j: (layer, j, 0, 0))
    state_specs = [blk3(SSD_HEAD_DIM, lambda j: j), blk3(SSD_STATE, lambda j: j // per_group),
                   blk3(SSD_STATE, lambda j: j // per_group), _const_spec(at.shape),
                   blk3(GLA_DK, gla_head), blk3(GLA_DK, gla_head), blk3(GLA_DK, gla_head), blk3(GLA_DV, gla_head),
                   _const_spec(qn.shape), _const_spec(kn.shape), _const_spec(vn.shape),
                   ssm_spec, gla_spec, kv_spec, kv_spec,
                   _const_spec(buckets[0].shape), _const_spec(buckets[1].shape), _smem_spec(), _smem_spec()]
    state_out = (f32(*ssm_in.shape), f32(*gla_in.shape), f32(*kc_in.shape), f32(*vc_in.shape),
                 f32(seq, SSD_WIDTH, n_seq), f32(seq, GLA_WIDTH, n_seq), f32(seq, n_seq, SWA_WIDTH))
    ssm_o, gla_o, ko, vo, yt, ot, oswa = pl.pallas_call(
        functools.partial(_sample_state_kernel, seq, len(prev_rest)),
        grid=(n_steps,),
        in_specs=state_specs + [pl.BlockSpec(memory_space=pl.ANY)] * len(prev_rest),
        out_specs=(ssm_spec, gla_spec, kv_spec, kv_spec, blk3(SSD_HEAD_DIM, lambda j: j), blk3(GLA_DV, gla_head),
                   pl.BlockSpec((seq, SEQ_PER_STEP, SWA_WIDTH), lambda j: (0, j, 0))),
        out_shape=state_out,
        input_output_aliases={len(state_in) + k: k for k in range(len(prev_rest))},
        scratch_shapes=[pltpu.VMEM((SEQ_PER_STEP * seq, LANES), F32)] * 6
        + [pltpu.VMEM((SWA_HEADS * seq, LANES), F32)] * 2,
        compiler_params=cparams(("arbitrary",)),
        name="sample_state",
    )(*state_in, *prev_rest)

    back_in = [yt, ot, oswa, xs, gates, h, p_all, dsk, snw, gnw, wout, wpe, wpg]
    back_specs = [pos_blk(SSD_WIDTH), pos_blk(GLA_WIDTH),
                  pl.BlockSpec((per_step, n_seq, SWA_WIDTH), lambda s: (s, 0, 0)),
                  row_blk(SSD_WIDTH), row_blk(D_MODEL), row_blk(D_MODEL),
                  pl.BlockSpec((1, step_rows, PLE_DIM), lambda s: (layer, s, 0))] + [
        _layer_spec(a, layer) for a in back_in[7:]]
    y = pl.pallas_call(
        _sample_back_kernel,
        grid=(seq // per_step,),
        in_specs=back_specs,
        out_specs=row_blk(D_MODEL),
        out_shape=f32(rows, D_MODEL),
        scratch_shapes=[pltpu.VMEM((step_rows, D_MODEL), BF16)],
        compiler_params=cparams(("arbitrary",)),
        name="sample_back",
    )(*back_in)
    return y, (conv_o, ssm_o, gla_o, ko, vo)


SWA_HEAD_ORDER = (0, 2, 1, 3)


def _win_tile_runs():
    sizes = (SSD_WIDTH, SSD_CONV_DIM, SSD_HEADS, GLA_HEADS * GLA_DK, GLA_HEADS * GLA_DK, GLA_WIDTH, GLA_WIDTH,
             GLA_RANK, SWA_WIDTH, SWA_KV_HEADS * SWA_HEAD_DIM, SWA_KV_HEADS * SWA_HEAD_DIM, SWA_WIDTH)
    offs = np.concatenate([[0], np.cumsum(sizes)])
    seg = lambda k: np.arange(offs[k], offs[k + 1])
    z, xbc, dt, gq, gk, gv, gg, glr, sq, sk, sv, sg = [seg(k) for k in range(len(sizes))]
    heads = lambda a: np.concatenate([a[h * SWA_HEAD_DIM:(h + 1) * SWA_HEAD_DIM] for h in SWA_HEAD_ORDER])
    pad = np.full(LANES - SSD_HEADS - GLA_RANK, -1)
    src = np.concatenate([xbc, z, gq, gk, gv, gg, heads(sq), sk, sv, heads(sg), dt, glr, pad])
    assert src.size == XBC_W + PROJ_W
    tiles = []
    for j in range(src.size // LANES):
        idx = src[j * LANES:(j + 1) * LANES]
        cuts = [0] + [k for k in range(1, LANES) if (idx[k] != idx[k - 1] + 1 and not (idx[k] == -1 == idx[k - 1]))]
        runs = [(int(idx[a]), b - a) for a, b in zip(cuts, cuts[1:] + [LANES])]
        assert all(n % SUBLANES == 0 and (s < 0 or s % SUBLANES == 0) for s, n in runs)
        tiles.append(runs)
    return tiles


def _win_prep_kernel(tile_runs, wt_ref, out_ref):
    for j, runs in enumerate(tile_runs):
        parts = [jnp.zeros((n, D_MODEL), F32) if s < 0 else wt_ref[0, s:s + n, :] for s, n in runs]
        tile = parts[0] if len(parts) == 1 else jnp.concatenate(parts, axis=0)
        out_ref[0, :, j * LANES:(j + 1) * LANES] = tile.T.astype(BF16)


def _prepare_w_in(w_in):
    depth, d_model, d_in = w_in.shape
    w_t = jnp.swapaxes(w_in, 1, 2)
    return pl.pallas_call(
        functools.partial(_win_prep_kernel, _win_tile_runs()),
        grid=(depth,),
        in_specs=[pl.BlockSpec((1, d_in, d_model), lambda l: (l, 0, 0))],
        out_specs=pl.BlockSpec((1, d_model, XBC_W + PROJ_W), lambda l: (l, 0, 0)),
        out_shape=jax.ShapeDtypeStruct((depth, d_model, XBC_W + PROJ_W), BF16),
        compiler_params=pltpu.CompilerParams(
            dimension_semantics=("arbitrary",), vmem_limit_bytes=VMEM_LIMIT_BYTES),
        name="w_in_prep",
    )(w_t)


def _prepare_weights(norm_w, w_in, conv_w, conv_b, dt_bias, a_log, d_skip, ssd_norm_w, gla_w_gk, gla_b_gk,
                     gla_norm_w, q_norm_w, k_norm_w, w_out, w_pe, w_pg):
    mix_w = SSD_WIDTH + GLA_WIDTH
    w_out_p = jnp.concatenate(
        [w_out[:, :mix_w, :]] + [w_out[:, mix_w + h * SWA_HEAD_DIM:mix_w + (h + 1) * SWA_HEAD_DIM, :]
                                 for h in SWA_HEAD_ORDER], axis=1).astype(BF16)
    lane_pad = lambda x: jnp.pad(x, ((0, 0), (0, LANES - x.shape[-1])))[:, None, :]
    wgk_p = jnp.pad(gla_w_gk, ((0, 0), (LR_LANE0, LANES - LR_LANE0 - GLA_RANK), (0, 0))).astype(BF16)
    return dict(
        norm_w=norm_w[:, None, :], w_in=_prepare_w_in(w_in), conv_w=conv_w, conv_b=conv_b[:, None, :],
        dt_bias=lane_pad(dt_bias), a_log=lane_pad(a_log),
        d_skip=jnp.repeat(d_skip, SSD_HEAD_DIM, axis=-1)[:, None, :], ssd_norm_w=ssd_norm_w[:, None, :],
        gla_w_gk=wgk_p, gla_b_gk=gla_b_gk[:, None, :],
        gla_norm_w=jnp.tile(gla_norm_w, (1, GLA_HEADS))[:, None, :],
        q_norm_w=jnp.tile(q_norm_w, (1, SWA_HEADS))[:, None, :],
        k_norm_w=jnp.tile(k_norm_w, (1, SWA_KV_HEADS))[:, None, :],
        w_out=w_out_p, w_pe=w_pe.astype(BF16), w_pg=w_pg.astype(BF16))


def kernel(x_prompt, x_sample, state_ssm, state_conv, state_gla, cache_swa_k, cache_swa_v, p_prompt, p_sample, rel_bias, norm_w, w_in, conv_w, conv_b, dt_bias, a_log, d_skip, ssd_norm_w, gla_w_gk, gla_b_gk, gla_norm_w, q_norm_w, k_norm_w, attn_sinks, w_out, w_pe, w_pg):
    depth = w_in.shape[0]
    bp, seq_p, _ = x_prompt.shape
    bs, seq_s, _ = x_sample.shape
    assert seq_s == SUBLANES and BLK % seq_s == 0 and (bs * seq_s) % BLK == 0
    assert cache_swa_k.shape[2] == WINDOW
    w = _prepare_weights(norm_w, w_in, conv_w, conv_b, dt_bias, a_log, d_skip, ssd_norm_w, gla_w_gk, gla_b_gk,
                         gla_norm_w, q_norm_w, k_norm_w, w_out, w_pe, w_pg)
    rel_flat = rel_bias.reshape(-1)
    dist_p = WINDOW + np.arange(BLK)[:, None] - np.arange(2 * BLK)[None, :]
    bucket_p = jnp.asarray(_bucket_table(dist_p))
    t_of_row = np.tile(np.arange(seq_s), SWA_HEADS)[:, None]
    bucket_c = jnp.asarray(_bucket_table(WINDOW + t_of_row - np.arange(WINDOW)[None, :]))
    dist_n = np.where(np.arange(LANES)[None, :] < seq_s, t_of_row - np.arange(LANES)[None, :], -1)
    bucket_n = jnp.asarray(_bucket_table(dist_n))

    ssm_in = jnp.transpose(state_ssm, (0, 2, 3, 4, 1)).reshape(depth, SSD_HEADS, SSD_HEAD_DIM * SSD_STATE, bs)
    gla_in = jnp.transpose(state_gla, (0, 2, 3, 4, 1)).reshape(depth, GLA_HEADS, GLA_DK * GLA_DV, bs)
    kv_in = lambda a: jnp.transpose(a, (0, 1, 3, 4, 2)).reshape(depth, bs, SWA_KV_HEADS * SWA_HEAD_DIM, WINDOW)
    kc_in, vc_in = kv_in(cache_swa_k), kv_in(cache_swa_v)
    conv_in = jnp.transpose(state_conv, (0, 2, 1, 3))

    hp = x_prompt.reshape(bp * seq_p, D_MODEL)
    p_prompt_rows = p_prompt.reshape(depth, bp * seq_p, PLE_DIM)
    hs = jnp.transpose(x_sample, (1, 0, 2)).reshape(seq_s * bs, D_MODEL)
    p_sample_rows = jnp.transpose(p_sample, (0, 2, 1, 3)).reshape(depth, seq_s * bs, PLE_DIM)
    states_p = ()
    states_s = ()
    for i in range(depth):
        wops, wspecs = _layer_weights(i, w)
        hp, *states_p = _prompt_layer(i, depth, bp, hp, p_prompt_rows, tuple(states_p), bucket_p, rel_flat,
                                      attn_sinks[i], wops, wspecs)
        hs, states_s = _sample_layer_native(i, depth, seq_s, hs, p_sample_rows, conv_in, ssm_in, gla_in, kc_in,
                                            vc_in, states_s, (bucket_c, bucket_n), rel_flat, attn_sinks[i], wops)
    ssm_p, conv_p, gla_p, kt_p, vt_p = states_p
    conv_s, ssm_s, gla_s, kt_s, vt_s = states_s
    unpack_kv = lambda a: jnp.transpose(
        a.reshape(a.shape[:2] + (SWA_KV_HEADS, SWA_HEAD_DIM, WINDOW)), (0, 1, 4, 2, 3))
    outs_p = (ssm_p.reshape(depth, bp, SSD_HEADS, SSD_HEAD_DIM, SSD_STATE), conv_p,
              gla_p.reshape(depth, bp, GLA_HEADS, GLA_DK, GLA_DV), unpack_kv(kt_p), unpack_kv(vt_p))
    seq_last = lambda a, dims: jnp.transpose(a.reshape(a.shape[:2] + dims + (bs,)), (0, 4, 1, 2, 3))
    outs_s = (seq_last(ssm_s, (SSD_HEAD_DIM, SSD_STATE)), jnp.transpose(conv_s, (0, 2, 1, 3)),
              seq_last(gla_s, (GLA_DK, GLA_DV)), unpack_kv(kt_s), unpack_kv(vt_s))
    y_sample = jnp.transpose(hs.reshape(seq_s, bs, D_MODEL), (1, 0, 2))
    return (hp.reshape(bp, seq_p, D_MODEL), y_sample) + outs_p + outs_s
```

```python
import functools
import math

import numpy as np
import jax
import jax.numpy as jnp
from jax import lax
from jax.experimental import pallas as pl
from jax.experimental.pallas import tpu as pltpu

D_MODEL = 1024
DEPTH = 2
SSD_HEADS = 8
SSD_HEAD_DIM = 64
SSD_WIDTH = SSD_HEADS * SSD_HEAD_DIM
SSD_GROUPS = 2
SSD_STATE = 64
SSD_CONV = 4
SSD_CONV_DIM = SSD_WIDTH + 2 * SSD_GROUPS * SSD_STATE
SSD_CHUNK = 128
GLA_HEADS = 4
GLA_DK = 32
GLA_DV = 64
GLA_WIDTH = GLA_HEADS * GLA_DV
GLA_RANK = 16
GLA_GATE_NORM = 16.0
GLA_CHUNK = 64
SWA_HEADS = 4
SWA_KV_HEADS = 2
SWA_HEAD_DIM = 64
SWA_WIDTH = SWA_HEADS * SWA_HEAD_DIM
WINDOW = 128
REL_BUCKETS = 32
REL_MAX_DIST = 128
PLE_DIM = 256
EPS = 1e-6

LANES = 128
SUBLANES = 8
HALF = LANES // 2
BLK = 128
VMEM_LIMIT_BYTES = 56 * 1024 * 1024

XBC_W = SSD_CONV_DIM
P_Z = 0
P_GQ = P_Z + SSD_WIDTH
P_GK = P_GQ + LANES
P_GV = P_GK + LANES
P_GG = P_GV + GLA_WIDTH
P_SQ = P_GG + GLA_WIDTH
P_SK = P_SQ + SWA_WIDTH
P_SV = P_SK + LANES
P_SG = P_SV + LANES
P_DTLR = P_SG + SWA_WIDTH
PROJ_W = P_DTLR + LANES
LR_LANE0 = SSD_HEADS

F32 = jnp.float32
BF16 = jnp.bfloat16
NEG_INF = float("-inf")
N_PROMPT_INPUTS = 22
PROMPT_CHUNK_ROWS = 2 * BLK
TICKS_PER_ITEM = 2
NT_DIMS = (((1,), (1,)), ((), ()))


def _iota(shape, dim):
    return lax.broadcasted_iota(jnp.int32, shape, dim)


def _div(x, d):
    return x >> (d.bit_length() - 1)


def _mod(x, d):
    return x & (d - 1)


def _softplus(x):
    e = jnp.exp(-jnp.abs(x))
    u = 1.0 + e
    d = u - 1.0
    log1p_e = jnp.where(d == 0.0, e, jnp.log(u) * (e / jnp.where(d == 0.0, 1.0, d)))
    return jnp.maximum(x, 0.0) + log1p_e


def _log_sigmoid(x):
    return jnp.minimum(x, 0.0) - jnp.log(1.0 + jnp.exp(-jnp.abs(x)))


def _silu(x):
    return x * jax.nn.sigmoid(x)


def _dot(a, b):
    return jnp.dot(a.astype(BF16), b.astype(BF16), preferred_element_type=F32)


def _dot_nt(a, b):
    return lax.dot_general(a.astype(BF16), b.astype(BF16), NT_DIMS, preferred_element_type=F32)


def _dot_exact(sel, x):
    x1 = x.astype(BF16)
    r1 = x - x1.astype(F32)
    x2 = r1.astype(BF16)
    x3 = (r1 - x2.astype(F32)).astype(BF16)
    dot = functools.partial(jnp.dot, sel, preferred_element_type=F32)
    return dot(x1) + dot(x2) + dot(x3)


def _expand_heads(x, n_heads):
    rows = x.shape[0]
    lo = _iota((rows, LANES), 1) < HALF
    tiles = []
    for j in range(n_heads // 2):
        a = jnp.broadcast_to(x[:, 2 * j:2 * j + 1], (rows, LANES))
        b = jnp.broadcast_to(x[:, 2 * j + 1:2 * j + 2], (rows, LANES))
        tiles.append(jnp.where(lo, a, b))
    return jnp.concatenate(tiles, axis=1)


def _head_rms_scale(x):
    rows, width = x.shape
    lo = _iota((rows, LANES), 1) < HALF
    outs = []
    for j in range(width // LANES):
        t = x[:, j * LANES:(j + 1) * LANES]
        sq = t * t
        s_lo = jnp.sum(jnp.where(lo, sq, 0.0), axis=-1, keepdims=True)
        s_hi = jnp.sum(jnp.where(lo, 0.0, sq), axis=-1, keepdims=True)
        outs.append(lax.rsqrt(jnp.where(lo, s_lo, s_hi) * (1.0 / HALF) + EPS))
    return outs[0] if len(outs) == 1 else jnp.concatenate(outs, axis=1)


def _group_rmsnorm(y, w):
    gw = SSD_WIDTH // SSD_GROUPS
    outs = []
    for g in range(SSD_GROUPS):
        t = y[:, g * gw:(g + 1) * gw]
        ms = jnp.sum(t * t, axis=-1, keepdims=True) * (1.0 / gw)
        outs.append(t * lax.rsqrt(ms + EPS))
    return jnp.concatenate(outs, axis=1) * w


def _rel_bucket_np(dist):
    n = np.maximum(dist, 0)
    exact = REL_BUCKETS // 2
    nf = np.maximum(n, 1).astype(np.float64)
    large = exact + (np.log(nf / exact) / math.log(REL_MAX_DIST / exact) * (REL_BUCKETS - exact)).astype(np.int32)
    large = np.minimum(large, REL_BUCKETS - 1)
    return np.where(n < exact, n, large).astype(np.int32)


def _bucket_table(dist):
    return np.where((dist >= 0) & (dist < WINDOW), _rel_bucket_np(dist), -1).astype(np.int32)


def _no_tick():
    pass


def _ssd_intra(xbc_c, dtv, acum, pair_mask, tick=_no_tick):
    xs = xbc_c[:, :SSD_WIDTH]
    bm = xbc_c[:, SSD_WIDTH:SSD_WIDTH + LANES]
    cm = xbc_c[:, SSD_WIDTH + LANES:]
    lane = _iota((BLK, LANES), 1)
    lo = lane < HALF
    acum_t = acum.T
    eacum = jnp.exp(acum)
    tail = jnp.exp(acum[BLK - 1:BLK, :] - acum)
    dtv_e = _expand_heads(dtv, SSD_HEADS)
    eacum_e = _expand_heads(eacum, SSD_HEADS)
    tail_e = _expand_heads(tail, SSD_HEADS)
    tick()
    xdt = xs * dtv_e
    xw = xdt * tail_e
    cb = [_dot_nt(jnp.where(lo, cm, 0.0), bm), _dot_nt(jnp.where(lo, 0.0, cm), bm)]
    y_pairs = []
    for j in range(SSD_HEADS // 2):
        tick()
        g = (2 * j) // (SSD_HEADS // SSD_GROUPS)
        ms = []
        for k in range(2):
            h = 2 * j + k
            seg = acum[:, h:h + 1] - acum_t[h:h + 1, :]
            dec = jnp.where(pair_mask, jnp.exp(seg), 0.0)
            ms.append((cb[g] * dec).astype(BF16))
        xp = xdt[:, j * LANES:(j + 1) * LANES]
        rhs = jnp.concatenate([jnp.where(lo, xp, 0.0), jnp.where(lo, 0.0, xp)], axis=0)
        y_pairs.append(_dot(jnp.concatenate(ms, axis=1), rhs))
    y_intra = jnp.concatenate(y_pairs, axis=1)
    return y_intra, xs, bm, cm, xw, eacum, eacum_e


def _gla_intra(gq, gk, gv, bcs, att_mask, tick=_no_tick):
    eb = jnp.exp(bcs)
    qe = gq * (GLA_DK ** -0.5) * eb
    ke = gk * jnp.exp(-bcs)
    btot = jnp.concatenate(
        [jnp.broadcast_to(bcs[(c2 + 1) * GLA_CHUNK - 1:(c2 + 1) * GLA_CHUNK, :], (GLA_CHUNK, LANES))
         for c2 in range(BLK // GLA_CHUNK)], axis=0)
    kd = gk * jnp.exp(btot - bcs)
    lane_k = _iota((GLA_CHUNK, LANES), 1)
    lane_v = _iota((GLA_CHUNK, GLA_WIDTH), 1)
    outs = []
    for c2 in range(BLK // GLA_CHUNK):
        tick()
        rs = slice(c2 * GLA_CHUNK, (c2 + 1) * GLA_CHUNK)
        ke_c = ke[rs]
        v_c = gv[rs]
        kbd = jnp.concatenate(
            [jnp.where(_div(lane_k, GLA_DK) == h, ke_c, 0.0) for h in range(GLA_HEADS)], axis=0)
        att = _dot_nt(qe[rs], kbd)
        att = jnp.where(att_mask, att, 0.0)
        vbd = jnp.concatenate(
            [jnp.where(_div(lane_v, GLA_DV) == h, v_c, 0.0) for h in range(GLA_HEADS)], axis=0)
        outs.append(_dot(att, vbd))
    return jnp.concatenate(outs, axis=0), qe, kd, jnp.exp(btot)


def _build_bias(bucket, rel_ref):
    accs = [jnp.full(bucket.shape, NEG_INF, F32) for _ in range(SWA_HEADS)]
    for b in range(REL_BUCKETS):
        hit = bucket == b
        for h in range(SWA_HEADS):
            accs[h] = jnp.where(hit, rel_ref[b * SWA_HEADS + h], accs[h])
    return accs


def _epilogue(h, mix, p, wout_ref, wpg_ref, wpe_ref):
    h1 = h + jnp.dot(mix, wout_ref[...], preferred_element_type=F32)
    gate = jax.nn.sigmoid(jnp.dot(h1.astype(BF16), wpg_ref[...], preferred_element_type=F32))
    pe = jnp.dot(p.astype(BF16), wpe_ref[...], preferred_element_type=F32)
    return h1 + gate * pe


def _prompt_kernel(chunks_per_seq, n_aliased, *refs):
    (ha_ref, hc_ref, p_ref, bucket_ref, rel_ref, sink_ref, nw_ref, win_ref, cw_ref, cb_ref, dtb_ref,
     alog_ref, dsk_ref, snw_ref, wgk_ref, bgk_ref, gnw_ref, qnw_ref, knw_ref, wout_ref,
     wpe_ref, wpg_ref) = refs[:N_PROMPT_INPUTS]
    (y_ref, ssm_ref, conv_ref, gla_ref, ko_ref, vo_ref,
     proj_e, proj_o, xbc_e, xbc_o, mix_e, mix_o, u_s, h1_s, h1b_s, hist_s, st_s, s2_s, kext_s, vext_s,
     bias_s, hist_snap, st_snap, s2_snap, k_snap, v_snap) = refs[N_PROMPT_INPUTS + n_aliased:]
    k_idx = pl.program_id(0)

    @pl.when(k_idx == 0)
    def _():
        accs = _build_bias(bucket_ref[...], rel_ref)
        own_block = _iota((BLK, 2 * BLK), 1) >= BLK
        for hh in range(SWA_HEADS):
            bias_s[hh] = accs[hh]
            bias_s[SWA_HEADS + hh] = jnp.where(own_block, accs[hh], NEG_INF)
        for ref in (proj_o, xbc_o, mix_e, mix_o, hist_s, st_s, s2_s, kext_s, vext_s):
            ref[...] = jnp.zeros(ref.shape, ref.dtype)

    row = _iota((BLK, BLK), 0)
    col = _iota((BLK, BLK), 1)
    causal = row >= col
    tri = jnp.where(causal, 1.0, 0.0).astype(BF16)
    lo = col < HALF
    lane_row = _iota((1, LANES), 1)
    a_row = jnp.where(lane_row < SSD_HEADS, -jnp.exp(alog_ref[...]), 0.0)
    bd_mask = _div(_iota((LANES, GLA_WIDTH), 0), GLA_DK) == _div(_iota((LANES, GLA_WIDTH), 1), GLA_DV)
    att_t = _iota((GLA_CHUNK, GLA_WIDTH), 0)
    att_s = _mod(_iota((GLA_CHUNK, GLA_WIDTH), 1), GLA_CHUNK)
    att_mask = att_s <= att_t
    lo2 = _iota((2 * BLK, LANES), 1) < HALF

    group_w = SSD_WIDTH // SSD_GROUPS

    def block(blk, proj_s, xbc_s, mix_s, starts_sequence, tick):
        rows = slice(blk * BLK, (blk + 1) * BLK)
        cw = cw_ref[...]
        tick()
        if blk == 0:
            xwin = jnp.concatenate([hist_s[...], xbc_s[0:BLK, :]], axis=0)
        else:
            xwin = xbc_s[blk * BLK - SUBLANES:(blk + 1) * BLK, :]
        acc = xwin[SUBLANES - 3:SUBLANES - 3 + BLK, :] * cw[0:1, :]
        for k in range(1, SSD_CONV):
            acc = acc + xwin[SUBLANES - 3 + k:SUBLANES - 3 + k + BLK, :] * cw[k:k + 1, :]
        xbc_c = _silu(acc + cb_ref[...])
        tick()
        dtlr = proj_s[rows, P_DTLR:P_DTLR + LANES]
        dtv_t = _softplus((dtlr + dtb_ref[...]).T[0:SSD_HEADS, :])
        dtv = jnp.concatenate([dtv_t, jnp.zeros((LANES - SSD_HEADS, BLK), F32)], axis=0).T
        glog = _log_sigmoid(_dot(dtlr, wgk_ref[...]) + bgk_ref[...]) * (1.0 / GLA_GATE_NORM)
        sums = _dot_exact(tri, jnp.concatenate([dtv * a_row, glog], axis=1))
        acum = sums[:, :LANES]
        gsum = sums[:, LANES:]
        bcs = gsum
        for c2 in range(1, BLK // GLA_CHUNK):
            before = gsum[c2 * GLA_CHUNK - 1:c2 * GLA_CHUNK, :]
            bcs = jnp.where(_div(row, GLA_CHUNK) == c2, gsum - before, bcs)
        tick()
        y_intra, xs, bm, cm, xw, eacum, eacum_e = _ssd_intra(xbc_c, dtv, acum, causal, tick)
        tick()
        st = st_s[...]
        y = y_intra + _dot(cm, st) * eacum_e + dsk_ref[...] * xs
        bm_t = bm.T
        for g in range(SSD_GROUPS):
            gr = slice(g * SSD_STATE, (g + 1) * SSD_STATE)
            gc = slice(g * group_w, (g + 1) * group_w)
            st_s[gr, gc] = st[gr, gc] * eacum_e[BLK - 1:BLK, gc] + _dot(bm_t[gr, :], xw[:, gc])
        tick()
        y = y * _silu(proj_s[rows, P_Z:P_Z + SSD_WIDTH])
        mix_s[rows, 0:SSD_WIDTH] = _group_rmsnorm(y, snw_ref[...]).astype(BF16)
        tick()
        gk = proj_s[rows, P_GK:P_GK + LANES]
        gv = proj_s[rows, P_GV:P_GV + GLA_WIDTH]
        o_intra, qe, kd, ebt = _gla_intra(proj_s[rows, P_GQ:P_GQ + LANES], gk, gv, bcs, att_mask, tick)
        kd_t = kd.T
        ebt_t = ebt.T
        s2 = s2_s[...]
        o_parts = []
        for c2 in range(BLK // GLA_CHUNK):
            tick()
            rs = slice(c2 * GLA_CHUNK, (c2 + 1) * GLA_CHUNK)
            o_parts.append(o_intra[rs] + _dot(qe[rs], s2))
            u2 = _dot(jnp.where(_div(col, GLA_CHUNK) == c2, kd_t, 0.0), gv)
            last = (c2 + 1) * GLA_CHUNK - 1
            s2 = s2 * ebt_t[:, last:last + 1] + jnp.where(bd_mask, u2, 0.0)
        s2_s[...] = s2
        o = jnp.concatenate(o_parts, axis=0)
        y_gla = o * _head_rms_scale(o) * gnw_ref[...] * _silu(proj_s[rows, P_GG:P_GG + GLA_WIDTH])
        mix_s[rows, SSD_WIDTH:SSD_WIDTH + GLA_WIDTH] = y_gla.astype(BF16)
        tick()
        sq = proj_s[rows, P_SQ:P_SQ + SWA_WIDTH]
        qn = sq * _head_rms_scale(sq) * qnw_ref[...] * (SWA_HEAD_DIM ** -0.5)
        sk = proj_s[rows, P_SK:P_SK + LANES]
        kn = sk * _head_rms_scale(sk) * knw_ref[...]
        vn = proj_s[rows, P_SV:P_SV + LANES]
        kext_s[BLK:2 * BLK, :] = kn
        vext_s[BLK:2 * BLK, :] = vn
        kext = kext_s[...]
        vext = vext_s[...]
        qa = qn[:, :LANES]
        qb = qn[:, LANES:]
        qs = jnp.concatenate([jnp.where(lo, qa, 0.0), jnp.where(lo, qb, 0.0),
                              jnp.where(lo, 0.0, qa), jnp.where(lo, 0.0, qb)], axis=0)
        logits = _dot_nt(qs, kext)
        tick()
        if blk == 0 and starts_sequence is not False:
            bias_row0 = jnp.where(starts_sequence, SWA_HEADS, 0)
        else:
            bias_row0 = 0
        es = []
        invs = []
        for hh in range(SWA_HEADS):
            tick()
            sink = sink_ref[hh]
            l = logits[hh * BLK:(hh + 1) * BLK] + bias_s[bias_row0 + hh]
            m = jnp.maximum(jnp.max(l, axis=-1, keepdims=True), sink)
            e = jnp.exp(l - m)
            den = jnp.sum(e, axis=-1, keepdims=True) + jnp.exp(sink - m)
            es.append(e.astype(BF16))
            invs.append(1.0 / den)
        v_stack = jnp.concatenate([jnp.where(lo2, vext, 0.0), jnp.where(lo2, 0.0, vext)], axis=0)
        tile_a = _dot(jnp.concatenate([es[0], es[2]], axis=1), v_stack) * jnp.where(lo, invs[0], invs[2])
        tile_b = _dot(jnp.concatenate([es[1], es[3]], axis=1), v_stack) * jnp.where(lo, invs[1], invs[3])
        oa = jnp.concatenate([tile_a, tile_b], axis=1)
        y_swa = oa * _silu(proj_s[rows, P_SG:P_SG + SWA_WIDTH])
        mix_s[rows, SSD_WIDTH + GLA_WIDTH:] = y_swa.astype(BF16)
        kext_s[0:BLK, :] = kn
        vext_s[0:BLK, :] = vn

    chunk = proj_e.shape[0]

    def project_items(rows, proj_s, xbc_s):
        def norm():
            h = ha_ref[rows, :]
            ms = jnp.mean(h * h, axis=-1, keepdims=True)
            u_s[...] = (h * lax.rsqrt(ms + EPS) * nw_ref[...]).astype(BF16)

        def cols(dst, lo_c, hi_c, w_off):
            def item():
                dst[:, lo_c:hi_c] = jnp.dot(u_s[...], win_ref[:, w_off + lo_c:w_off + hi_c],
                                            preferred_element_type=F32)
            return item

        step = 2 * LANES
        items = [norm]
        items += [cols(xbc_s, c, min(c + step, XBC_W), 0) for c in range(0, XBC_W, step)]
        items += [cols(proj_s, c, min(c + step, PROJ_W), XBC_W) for c in range(0, PROJ_W, step)]
        return items

    def epilogue_items(rows, mix_s):
        half_w = 2 * LANES

        def residual(c):
            def item():
                h1 = hc_ref[rows, c:c + half_w] + jnp.dot(
                    mix_s[...], wout_ref[:, c:c + half_w], preferred_element_type=F32)
                h1_s[:, c:c + half_w] = h1
                h1b_s[:, c:c + half_w] = h1.astype(BF16)
            return item

        def gated(c):
            def item():
                gate = jax.nn.sigmoid(jnp.dot(h1b_s[...], wpg_ref[:, c:c + half_w], preferred_element_type=F32))
                pe = jnp.dot(p_ref[0, rows, :].astype(BF16), wpe_ref[:, c:c + half_w],
                             preferred_element_type=F32)
                y_ref[rows, c:c + half_w] = h1_s[:, c:c + half_w] + gate * pe
            return item

        col0 = range(0, D_MODEL, half_w)
        return [residual(c) for c in col0] + [gated(c) for c in col0]

    def merge(first, second):
        out = list(first)
        for j, item in enumerate(second):
            out.insert(((j + 1) * len(first)) // len(second) + j, item)
        return out

    def mixer(proj_s, xbc_s, mix_s, starts_sequence, items):
        if starts_sequence is not False:
            keep = jnp.where(starts_sequence, 0.0, 1.0)
            for ref in (hist_s, st_s, s2_s):
                ref[...] = ref[...] * keep
            kext_s[0:BLK, :] = kext_s[0:BLK, :] * keep
            vext_s[0:BLK, :] = vext_s[0:BLK, :] * keep
        queue = list(items)
        calls = [0]

        def tick():
            calls[0] += 1
            if queue and calls[0] % TICKS_PER_ITEM == 0:
                queue.pop(0)()

        for blk in range(chunk // BLK):
            block(blk, proj_s, xbc_s, mix_s, starts_sequence, tick)
        while queue:
            queue.pop(0)()
        hist_s[...] = xbc_s[chunk - SUBLANES:chunk, :]

    def snapshot_states():
        st_snap[...] = st_s[...]
        s2_snap[...] = s2_s[...]
        hist_snap[...] = hist_s[...]
        k_snap[...] = kext_s[0:BLK, :]
        v_snap[...] = vext_s[0:BLK, :]

    def write_states():
        st = st_snap[...]
        stc = st[:SSD_STATE] + st[SSD_STATE:]
        ssm_ref[0, 0] = jnp.concatenate([stc, stc], axis=0).T[:, :SSD_STATE]
        conv_ref[0, 0] = hist_snap[SUBLANES - (SSD_CONV - 1):SUBLANES, :]
        s2 = s2_snap[...]
        w = s2[:, :LANES] + s2[:, LANES:]
        gla_ref[0, 0] = w[:, :GLA_DV] + w[:, GLA_DV:]
        ko_ref[0, 0] = k_snap[...].T
        vo_ref[0, 0] = v_snap[...].T

    even = slice(0, chunk)
    odd = slice(chunk, 2 * chunk)
    mixer(proj_o, xbc_o, mix_o, False, merge(project_items(even, proj_e, xbc_e), epilogue_items(even, mix_e)))
    snapshot_states()
    mixer(proj_e, xbc_e, mix_e, _mod(2 * k_idx, chunks_per_seq) == 0,
          merge(project_items(odd, proj_o, xbc_o), epilogue_items(odd, mix_o)))

    @pl.when((k_idx >= 1) & (_mod(2 * k_idx - 1, chunks_per_seq) == chunks_per_seq - 1))
    def _():
        write_states()


N_FRONT_INPUTS = 12
SAMPLE_POSITIONS_PER_STEP = 2


def _sample_front_kernel(seq, n_aliased, *refs):
    (h_ref, cst_ref, nw_ref, win_ref, cw_ref, cb_ref, dtb_ref, alog_ref, wgk_ref, bgk_ref,
     qnw_ref, knw_ref) = refs[:N_FRONT_INPUTS]
    (xs_ref, gates_ref, xt_ref, bt_ref, ct_ref, at_ref, qt_ref, kt_ref, egt_ref, vt_ref,
     qn_ref, kn_ref, vn_ref, conv_ref, u_s, xbc_s, proj_s) = refs[N_FRONT_INPUTS + n_aliased:]
    n_seq = BLK
    per_step = h_ref.shape[0] // n_seq
    step = pl.program_id(0)
    ht = h_ref[...]
    ms = jnp.mean(ht * ht, axis=-1, keepdims=True)
    u_s[...] = (ht * lax.rsqrt(ms + EPS) * nw_ref[...]).astype(BF16)
    xbc_s[pl.ds(pl.multiple_of(step * per_step * n_seq, n_seq), per_step * n_seq), :] = jnp.dot(
        u_s[...], win_ref[:, :XBC_W], preferred_element_type=F32)
    proj_s[...] = jnp.dot(u_s[...], win_ref[:, XBC_W:], preferred_element_type=F32)
    cw = cw_ref[...]
    a_row = jnp.where(_iota((1, LANES), 1) < SSD_HEADS, -jnp.exp(alog_ref[...]), 0.0)
    for i in range(per_step):
        t = step * per_step + i
        rows = slice(i * n_seq, (i + 1) * n_seq)

        def raw_xbc(back):
            cur = xbc_s[pl.ds(pl.multiple_of(jnp.maximum(t - back, 0) * n_seq, n_seq), n_seq), :]
            if back == 0:
                return cur
            old = cst_ref[0, jnp.clip(SSD_CONV - 1 + t - back, 0, SSD_CONV - 2)]
            return jnp.where(t >= back, cur, old)

        acc = raw_xbc(SSD_CONV - 1) * cw[0:1, :]
        for k in range(1, SSD_CONV):
            acc = acc + raw_xbc(SSD_CONV - 1 - k) * cw[k:k + 1, :]
        xbc_c = _silu(acc + cb_ref[...])
        xs = xbc_c[:, :SSD_WIDTH]
        dtlr = proj_s[rows, P_DTLR:P_DTLR + LANES]
        dtv = _softplus(dtlr + dtb_ref[...])
        xs_ref[rows, :] = xs
        xt_ref[i] = (xs * _expand_heads(dtv, SSD_HEADS)).T
        bt_ref[i] = xbc_c[:, SSD_WIDTH:SSD_WIDTH + LANES].T
        ct_ref[i] = xbc_c[:, SSD_WIDTH + LANES:].T
        at_ref[i] = jnp.exp(dtv * a_row).T[:SSD_HEADS, :]
        glog = _log_sigmoid(_dot(dtlr, wgk_ref[...]) + bgk_ref[...]) * (1.0 / GLA_GATE_NORM)
        qt_ref[i] = (proj_s[rows, P_GQ:P_GQ + LANES] * (GLA_DK ** -0.5)).T
        kt_ref[i] = proj_s[rows, P_GK:P_GK + LANES].T
        egt_ref[i] = jnp.exp(glog).T
        vt_ref[i] = proj_s[rows, P_GV:P_GV + GLA_WIDTH].T
        sq = proj_s[rows, P_SQ:P_SQ + SWA_WIDTH]
        qn_ref[rows, :] = sq * _head_rms_scale(sq) * qnw_ref[...] * (SWA_HEAD_DIM ** -0.5)
        sk = proj_s[rows, P_SK:P_SK + LANES]
        kn_ref[rows, :] = sk * _head_rms_scale(sk) * knw_ref[...]
        vn_ref[rows, :] = proj_s[rows, P_SV:P_SV + LANES]
        gates_ref[rows, :] = jnp.concatenate(
            [_silu(proj_s[rows, P_Z:P_Z + SSD_WIDTH]), _silu(proj_s[rows, P_GG:P_GG + GLA_WIDTH]),
             _silu(proj_s[rows, P_SG:P_SG + SWA_WIDTH])], axis=1)
        first_kept = seq - (SSD_CONV - 1)

        @pl.when(t >= first_kept)
        def _():
            conv_ref[0, jnp.maximum(t - first_kept, 0)] = raw_xbc(0)


N_STATE_INPUTS = 19
SEQ_PER_STEP = 16


def _sample_state_kernel(seq, n_aliased, *refs):
    (xt_ref, bt_ref, ct_ref, at_ref, qt_ref, kt_ref, egt_ref, vt_ref, qn_ref, kn_ref, vn_ref,
     ssm_ref, gla_ref, kc_ref, vc_ref, bucket_c_ref, bucket_n_ref, rel_ref, sink_ref) = refs[:N_STATE_INPUTS]
    (ssm_o, gla_o, ko_ref, vo_ref, yt_ref, ot_ref, oswa_ref,
     qa_s, qb_s, krow_s, vrow_s, oa_s, ob_s, biasc_s, biasn_s) = refs[N_STATE_INPUTS + n_aliased:]
    j = pl.program_id(0)
    n_seq = LANES
    head_of_row = _div(_iota((SWA_HEADS * seq, LANES), 0), seq)

    def by_head(values):
        out = values[SWA_HEADS - 1]
        for hh in range(SWA_HEADS - 2, -1, -1):
            out = jnp.where(head_of_row == hh, values[hh], out)
        return out

    @pl.when(j == 0)
    def _():
        biasc_s[...] = by_head(_build_bias(bucket_c_ref[...], rel_ref))
        biasn_s[...] = by_head(_build_bias(bucket_n_ref[...], rel_ref))

    sub = _iota((SUBLANES, LANES), 0)
    a_rows = [jnp.sum(jnp.where(sub == j, at_ref[t], 0.0), axis=0, keepdims=True) for t in range(seq)]

    def ssd_body(p8, carry):
        r8 = pl.multiple_of(p8 * SUBLANES, SUBLANES)
        x_tiles = [xt_ref[t, pl.ds(r8, SUBLANES), :] for t in range(seq)]
        y_rows = [[] for _ in range(seq)]
        for pp in range(SUBLANES):
            r64 = pl.multiple_of((p8 * SUBLANES + pp) * SSD_STATE, SSD_STATE)
            slab = ssm_ref[0, 0, pl.ds(r64, SSD_STATE), :]
            for t in range(seq):
                slab = slab * a_rows[t] + x_tiles[t][pp:pp + 1, :] * bt_ref[t]
                y_rows[t].append(jnp.sum(ct_ref[t] * slab, axis=0, keepdims=True))
            ssm_o[0, 0, pl.ds(r64, SSD_STATE), :] = slab
        for t in range(seq):
            yt_ref[t, pl.ds(r8, SUBLANES), :] = jnp.concatenate(y_rows[t], axis=0)
        return carry

    lax.fori_loop(0, SSD_HEAD_DIM // SUBLANES, ssd_body, 0)

    @pl.when(j < GLA_HEADS)
    def _():
        for t in range(seq):
            ot_ref[t] = jnp.zeros((GLA_DV, LANES), F32)

        def gla_body(d8, carry):
            r8 = pl.multiple_of(d8 * SUBLANES, SUBLANES)
            q_tiles = [qt_ref[t, pl.ds(r8, SUBLANES), :] for t in range(seq)]
            k_tiles = [kt_ref[t, pl.ds(r8, SUBLANES), :] for t in range(seq)]
            g_tiles = [egt_ref[t, pl.ds(r8, SUBLANES), :] for t in range(seq)]
            for dd in range(SUBLANES):
                r64 = pl.multiple_of((d8 * SUBLANES + dd) * GLA_DV, GLA_DV)
                slab = gla_ref[0, 0, pl.ds(r64, GLA_DV), :]
                for t in range(seq):
                    slab = slab * g_tiles[t][dd:dd + 1, :] + k_tiles[t][dd:dd + 1, :] * vt_ref[t]
                    ot_ref[t] = ot_ref[t] + q_tiles[t][dd:dd + 1, :] * slab
                gla_o[0, 0, pl.ds(r64, GLA_DV), :] = slab
            return carry

        lax.fori_loop(0, GLA_DK // SUBLANES, gla_body, 0)

    base = pl.multiple_of(j * SEQ_PER_STEP, SEQ_PER_STEP)
    for t in range(seq):
        src = pl.ds(t * n_seq + base, SEQ_PER_STEP)
        dst = pl.ds(t, SEQ_PER_STEP, stride=seq)
        qa_s[dst, :] = qn_ref[src, 0:LANES]
        qb_s[dst, :] = qn_ref[src, LANES:2 * LANES]
        krow_s[dst, :] = kn_ref[src, :]
        vrow_s[dst, :] = vn_ref[src, :]
    kn_t = krow_s[...].T
    vn_t = vrow_s[...].T
    keep_old = _iota((LANES, WINDOW), 1) < WINDOW - seq
    lo8 = _iota((seq, LANES), 1) < HALF
    sink_col = by_head([jnp.full((SWA_HEADS * seq, LANES), sink_ref[hh], F32) for hh in range(SWA_HEADS)])[:, 0:1]

    def swa_stages(bl):
        r8 = pl.multiple_of(bl * seq, seq)
        v = {}

        def logits():
            qa = qa_s[pl.ds(r8, seq), :]
            qb = qb_s[pl.ds(r8, seq), :]
            qs = jnp.concatenate([jnp.where(lo8, qa, 0.0), jnp.where(lo8, qb, 0.0),
                                  jnp.where(lo8, 0.0, qa), jnp.where(lo8, 0.0, qb)], axis=0)
            v["lc"] = _dot(qs, kc_ref[0, bl]) + biasc_s[...]
            v["ln"] = _dot_nt(qs, krow_s[pl.ds(r8, seq), :]) + biasn_s[:, 0:seq]

        def softmax():
            lc, ln = v["lc"], v["ln"]
            m = jnp.maximum(jnp.maximum(jnp.max(lc, axis=-1, keepdims=True), jnp.max(ln, axis=-1, keepdims=True)),
                            sink_col)
            v["ec"] = jnp.exp(lc - m)
            v["en"] = jnp.exp(ln - m)
            v["inv"] = 1.0 / (jnp.sum(v["ec"], axis=-1, keepdims=True) + jnp.sum(v["en"], axis=-1, keepdims=True)
                              + jnp.exp(sink_col - m))

        def values():
            o = (_dot_nt(v["ec"], vc_ref[0, bl]) + _dot(v["en"], vrow_s[pl.ds(r8, seq), :])) * v["inv"]
            oa_s[pl.ds(r8, seq), :] = jnp.where(lo8, o[0:seq], o[2 * seq:3 * seq])
            ob_s[pl.ds(r8, seq), :] = jnp.where(lo8, o[seq:2 * seq], o[3 * seq:4 * seq])

        def window():
            ko_ref[0, bl] = jnp.where(keep_old, pltpu.roll(kc_ref[0, bl], WINDOW - seq, axis=1),
                                      pltpu.roll(kn_t, WINDOW - seq - r8, axis=1))
            vo_ref[0, bl] = jnp.where(keep_old, pltpu.roll(vc_ref[0, bl], WINDOW - seq, axis=1),
                                      pltpu.roll(vn_t, WINDOW - seq - r8, axis=1))

        return [logits, softmax, values, window]

    all_stages = [swa_stages(bl) for bl in range(SEQ_PER_STEP)]
    for k in range(len(all_stages[0])):
        for stages in all_stages:
            stages[k]()
    for t in range(seq):
        src = pl.ds(t, SEQ_PER_STEP, stride=seq)
        oswa_ref[t] = jnp.concatenate([oa_s[src, :], ob_s[src, :]], axis=1)


def _sample_back_kernel(yt_ref, ot_ref, oswa_ref, xs_ref, gates_ref, h_ref, p_ref, dsk_ref, snw_ref, gnw_ref,
                        wout_ref, wpe_ref, wpg_ref, y_ref, mix_s):
    n_seq = BLK
    for t in range(yt_ref.shape[0]):
        rows = slice(t * n_seq, (t + 1) * n_seq)
        y = (yt_ref[t].T + dsk_ref[...] * xs_ref[rows, :]) * gates_ref[rows, 0:SSD_WIDTH]
        mix_s[rows, 0:SSD_WIDTH] = _group_rmsnorm(y, snw_ref[...]).astype(BF16)
        o = ot_ref[t].T
        y_gla = o * _head_rms_scale(o) * gnw_ref[...] * gates_ref[rows, SSD_WIDTH:SSD_WIDTH + GLA_WIDTH]
        mix_s[rows, SSD_WIDTH:SSD_WIDTH + GLA_WIDTH] = y_gla.astype(BF16)
        mix_s[rows, SSD_WIDTH + GLA_WIDTH:] = (oswa_ref[t] * gates_ref[rows, SSD_WIDTH + GLA_WIDTH:]).astype(BF16)
    y_ref[...] = _epilogue(h_ref[...], mix_s[...], p_ref[0], wout_ref, wpg_ref, wpe_ref)


def _const_spec(shape):
    nd = len(shape)
    return pl.BlockSpec(shape, lambda *_: (0,) * nd)


def _smem_spec():
    return pl.BlockSpec(memory_space=pltpu.SMEM)


def _layer_spec(arr, layer):
    return pl.BlockSpec((None,) + arr.shape[1:], lambda *_: (layer, 0, 0), pipeline_mode=pl.Buffered(1))


def _layer_weights(layer, w):
    ops = [w[name] for name in ("norm_w", "w_in", "conv_w", "conv_b", "dt_bias", "a_log", "d_skip", "ssd_norm_w",
                                "gla_w_gk", "gla_b_gk", "gla_norm_w", "q_norm_w", "k_norm_w", "w_out", "w_pe",
                                "w_pg")]
    return ops, [_layer_spec(o, layer) for o in ops]


def _prompt_layer(layer, depth, bsz, h, p_all, prev_states, bucket, rel, sinks, wops, wspecs):
    rows_total, _ = h.shape
    seq_len = rows_total // bsz
    chunk = PROMPT_CHUNK_ROWS
    pair = 2 * chunk
    chunks_per_seq = seq_len // chunk
    assert seq_len % pair == 0 and chunks_per_seq & (chunks_per_seq - 1) == 0
    n_pairs = rows_total // pair
    kern = functools.partial(_prompt_kernel, chunks_per_seq, len(prev_states))
    proj_rows = pl.BlockSpec((pair, D_MODEL), lambda k: (jnp.minimum(k, n_pairs - 1), 0))
    out_rows = pl.BlockSpec((pair, D_MODEL), lambda k: (jnp.maximum(k - 1, 0), 0))
    p_spec = pl.BlockSpec((1, pair, PLE_DIM), lambda k: (layer, jnp.maximum(k - 1, 0), 0))
    per_seq = lambda s: pl.BlockSpec(
        (1, 1) + s, lambda k: (layer, jnp.maximum(2 * k - 1, 0) // chunks_per_seq) + (0,) * len(s))
    state_shapes = ((SSD_WIDTH, SSD_STATE), (SSD_CONV - 1, SSD_CONV_DIM), (GLA_HEADS * GLA_DK, GLA_DV),
                    (LANES, WINDOW), (LANES, WINDOW))
    out_shape = (jax.ShapeDtypeStruct((rows_total, D_MODEL), F32),) + tuple(
        jax.ShapeDtypeStruct((depth, bsz) + s, F32) for s in state_shapes)
    return pl.pallas_call(
        kern,
        grid=(n_pairs + 1,),
        in_specs=[proj_rows, out_rows, p_spec, _const_spec(bucket.shape), _smem_spec(), _smem_spec()]
        + wspecs + [pl.BlockSpec(memory_space=pl.ANY)] * len(prev_states),
        out_specs=(out_rows,) + tuple(per_seq(s) for s in state_shapes),
        out_shape=out_shape,
        input_output_aliases={N_PROMPT_INPUTS + k: 1 + k for k in range(len(prev_states))},
        scratch_shapes=[
            pltpu.VMEM((chunk, PROJ_W), F32), pltpu.VMEM((chunk, PROJ_W), F32),
            pltpu.VMEM((chunk, XBC_W), F32), pltpu.VMEM((chunk, XBC_W), F32),
            pltpu.VMEM((chunk, D_MODEL), BF16), pltpu.VMEM((chunk, D_MODEL), BF16),
            pltpu.VMEM((chunk, D_MODEL), BF16),
            pltpu.VMEM((chunk, D_MODEL), F32),
            pltpu.VMEM((chunk, D_MODEL), BF16),
            pltpu.VMEM((SUBLANES, XBC_W), F32),
            pltpu.VMEM((BLK, SSD_WIDTH), F32),
            pltpu.VMEM((LANES, GLA_WIDTH), F32),
            pltpu.VMEM((2 * BLK, LANES), F32),
            pltpu.VMEM((2 * BLK, LANES), F32),
            pltpu.VMEM((2 * SWA_HEADS, BLK, 2 * BLK), F32),
            pltpu.VMEM((SUBLANES, XBC_W), F32), pltpu.VMEM((BLK, SSD_WIDTH), F32),
            pltpu.VMEM((LANES, GLA_WIDTH), F32), pltpu.VMEM((BLK, LANES), F32),
            pltpu.VMEM((BLK, LANES), F32),
        ],
        compiler_params=pltpu.CompilerParams(
            dimension_semantics=("arbitrary",), vmem_limit_bytes=VMEM_LIMIT_BYTES),
        name="prompt_layer",
    )(h, h, p_all, bucket, rel, sinks, *wops, *prev_states)


def _whole(shape, layer=None):
    if layer is None:
        return pl.BlockSpec(shape, lambda *_: (0,) * len(shape), pipeline_mode=pl.Buffered(1))
    return pl.BlockSpec((1,) + shape[1:], lambda *_: (layer,) + (0,) * (len(shape) - 1),
                        pipeline_mode=pl.Buffered(1))


def _sample_layer_native(layer, depth, seq, h, p_all, conv_in, ssm_in, gla_in, kc_in, vc_in, prev_states,
                         buckets, rel, sinks, wops):
    (nw, win, cw, cb, dtb, alog, dsk, snw, wgk, bgk, gnw, qnw, knw, wout, wpe, wpg) = wops
    rows = h.shape[0]
    n_seq = rows // seq
    assert n_seq == LANES and n_seq % SEQ_PER_STEP == 0 and SSD_HEADS * SEQ_PER_STEP == n_seq
    prev_conv, prev_rest = (prev_states[:1], prev_states[1:]) if prev_states else ((), ())
    f32 = lambda *s: jax.ShapeDtypeStruct(s, F32)
    cparams = lambda sem: pltpu.CompilerParams(dimension_semantics=sem, vmem_limit_bytes=VMEM_LIMIT_BYTES)

    per_step = SAMPLE_POSITIONS_PER_STEP
    step_rows = per_step * n_seq
    assert seq % per_step == 0
    row_blk = lambda w: pl.BlockSpec((step_rows, w), lambda s: (s, 0))
    pos_blk = lambda n: pl.BlockSpec((per_step, n, n_seq), lambda s: (s, 0, 0))
    front_in = [h, conv_in, nw, win, cw, cb, dtb, alog, wgk, bgk, qnw, knw]
    front_specs = [row_blk(D_MODEL), _whole(conv_in.shape, layer)] + [_layer_spec(a, layer) for a in front_in[2:]]
    front_out = (f32(rows, SSD_WIDTH), f32(rows, D_MODEL),
                 f32(seq, SSD_WIDTH, n_seq), f32(seq, LANES, n_seq), f32(seq, LANES, n_seq),
                 f32(seq, SSD_HEADS, n_seq), f32(seq, LANES, n_seq), f32(seq, LANES, n_seq), f32(seq, LANES, n_seq),
                 f32(seq, GLA_WIDTH, n_seq), f32(rows, SWA_WIDTH), f32(rows, LANES), f32(rows, LANES),
                 f32(*conv_in.shape))
    front_out_specs = (row_blk(SSD_WIDTH), row_blk(D_MODEL), pos_blk(SSD_WIDTH), pos_blk(LANES), pos_blk(LANES),
                       pos_blk(SSD_HEADS), pos_blk(LANES), pos_blk(LANES), pos_blk(LANES), pos_blk(GLA_WIDTH),
                       row_blk(SWA_WIDTH), row_blk(LANES), row_blk(LANES), _whole(conv_in.shape, layer))
    (xs, gates, xt, bt, ct, at, qt, kt, egt, vt, qn, kn, vn, conv_o) = pl.pallas_call(
        functools.partial(_sample_front_kernel, seq, len(prev_conv)),
        grid=(seq // per_step,),
        in_specs=front_specs + [pl.BlockSpec(memory_space=pl.ANY)] * len(prev_conv),
        out_specs=front_out_specs,
        out_shape=front_out,
        input_output_aliases={len(front_in) + k: len(front_out) - 1 + k for k in range(len(prev_conv))},
        scratch_shapes=[pltpu.VMEM((step_rows, D_MODEL), BF16), pltpu.VMEM((rows, XBC_W), F32),
                        pltpu.VMEM((step_rows, PROJ_W), F32)],
        compiler_params=cparams(("arbitrary",)),
        name="sample_front",
    )(*front_in, *prev_conv)

    n_steps = SSD_HEADS
    per_group = SSD_HEADS // SSD_GROUPS
    gla_head = lambda j: jnp.minimum(j, GLA_HEADS - 1)
    blk3 = lambda n, f: pl.BlockSpec((seq, n, n_seq), lambda j: (0, f(j), 0))
    state_in = [xt, bt, ct, at, qt, kt, egt, vt, qn, kn, vn, ssm_in, gla_in, kc_in, vc_in, buckets[0], buckets[1],
                rel, sinks]
    ssm_spec = pl.BlockSpec((1, 1) + ssm_in.shape[2:], lambda j: (layer, j, 0, 0))
    gla_spec = pl.BlockSpec((1, 1) + gla_in.shape[2:], lambda j: (layer, gla_head(j), 0, 0))
    kv_spec = pl.BlockSpec((1, SEQ_PER_STEP) + kc_in.shape[2:], lambda j: (layer, j, 0, 0))
    state_specs = [blk3(SSD_HEAD_DIM, lambda j: j), blk3(SSD_STATE, lambda j: j // per_group),
                   blk3(SSD_STATE, lambda j: j // per_group), _const_spec(at.shape),
                   blk3(GLA_DK, gla_head), blk3(GLA_DK, gla_head), blk3(GLA_DK, gla_head), blk3(GLA_DV, gla_head),
                   _const_spec(qn.shape), _const_spec(kn.shape), _const_spec(vn.shape),
                   ssm_spec, gla_spec, kv_spec, kv_spec,
                   _const_spec(buckets[0].shape), _const_spec(buckets[1].shape), _smem_spec(), _smem_spec()]
    state_out = (f32(*ssm_in.shape), f32(*gla_in.shape), f32(*kc_in.shape), f32(*vc_in.shape),
                 f32(seq, SSD_WIDTH, n_seq), f32(seq, GLA_WIDTH, n_seq), f32(seq, n_seq, SWA_WIDTH))
    ssm_o, gla_o, ko, vo, yt, ot, oswa = pl.pallas_call(
        functools.partial(_sample_state_kernel, seq, len(prev_rest)),
        grid=(n_steps,),
        in_specs=state_specs + [pl.BlockSpec(memory_space=pl.ANY)] * len(prev_rest),
        out_specs=(ssm_spec, gla_spec, kv_spec, kv_spec, blk3(SSD_HEAD_DIM, lambda j: j), blk3(GLA_DV, gla_head),
                   pl.BlockSpec((seq, SEQ_PER_STEP, SWA_WIDTH), lambda j: (0, j, 0))),
        out_shape=state_out,
        input_output_aliases={len(state_in) + k: k for k in range(len(prev_rest))},
        scratch_shapes=[pltpu.VMEM((SEQ_PER_STEP * seq, LANES), F32)] * 6
        + [pltpu.VMEM((SWA_HEADS * seq, LANES), F32)] * 2,
        compiler_params=cparams(("arbitrary",)),
        name="sample_state",
    )(*state_in, *prev_rest)

    back_in = [yt, ot, oswa, xs, gates, h, p_all, dsk, snw, gnw, wout, wpe, wpg]
    back_specs = [pos_blk(SSD_WIDTH), pos_blk(GLA_WIDTH),
                  pl.BlockSpec((per_step, n_seq, SWA_WIDTH), lambda s: (s, 0, 0)),
                  row_blk(SSD_WIDTH), row_blk(D_MODEL), row_blk(D_MODEL),
                  pl.BlockSpec((1, step_rows, PLE_DIM), lambda s: (layer, s, 0))] + [
        _layer_spec(a, layer) for a in back_in[7:]]
    y = pl.pallas_call(
        _sample_back_kernel,
        grid=(seq // per_step,),
        in_specs=back_specs,
        out_specs=row_blk(D_MODEL),
        out_shape=f32(rows, D_MODEL),
        scratch_shapes=[pltpu.VMEM((step_rows, D_MODEL), BF16)],
        compiler_params=cparams(("arbitrary",)),
        name="sample_back",
    )(*back_in)
    return y, (conv_o, ssm_o, gla_o, ko, vo)


SWA_HEAD_ORDER = (0, 2, 1, 3)


def _win_tile_runs():
    sizes = (SSD_WIDTH, SSD_CONV_DIM, SSD_HEADS, GLA_HEADS * GLA_DK, GLA_HEADS * GLA_DK, GLA_WIDTH, GLA_WIDTH,
             GLA_RANK, SWA_WIDTH, SWA_KV_HEADS * SWA_HEAD_DIM, SWA_KV_HEADS * SWA_HEAD_DIM, SWA_WIDTH)
    offs = np.concatenate([[0], np.cumsum(sizes)])
    seg = lambda k: np.arange(offs[k], offs[k + 1])
    z, xbc, dt, gq, gk, gv, gg, glr, sq, sk, sv, sg = [seg(k) for k in range(len(sizes))]
    heads = lambda a: np.concatenate([a[h * SWA_HEAD_DIM:(h + 1) * SWA_HEAD_DIM] for h in SWA_HEAD_ORDER])
    pad = np.full(LANES - SSD_HEADS - GLA_RANK, -1)
    src = np.concatenate([xbc, z, gq, gk, gv, gg, heads(sq), sk, sv, heads(sg), dt, glr, pad])
    assert src.size == XBC_W + PROJ_W
    tiles = []
    for j in range(src.size // LANES):
        idx = src[j * LANES:(j + 1) * LANES]
        cuts = [0] + [k for k in range(1, LANES) if (idx[k] != idx[k - 1] + 1 and not (idx[k] == -1 == idx[k - 1]))]
        runs = [(int(idx[a]), b - a) for a, b in zip(cuts, cuts[1:] + [LANES])]
        assert all(n % SUBLANES == 0 and (s < 0 or s % SUBLANES == 0) for s, n in runs)
        tiles.append(runs)
    return tiles


def _win_prep_kernel(tile_runs, wt_ref, out_ref):
    for j, runs in enumerate(tile_runs):
        parts = [jnp.zeros((n, D_MODEL), F32) if s < 0 else wt_ref[0, s:s + n, :] for s, n in runs]
        tile = parts[0] if len(parts) == 1 else jnp.concatenate(parts, axis=0)
        out_ref[0, :, j * LANES:(j + 1) * LANES] = tile.T.astype(BF16)


def _prepare_w_in(w_in):
    depth, d_model, d_in = w_in.shape
    w_t = jnp.swapaxes(w_in, 1, 2)
    return pl.pallas_call(
        functools.partial(_win_prep_kernel, _win_tile_runs()),
        grid=(depth,),
        in_specs=[pl.BlockSpec((1, d_in, d_model), lambda l: (l, 0, 0))],
        out_specs=pl.BlockSpec((1, d_model, XBC_W + PROJ_W), lambda l: (l, 0, 0)),
        out_shape=jax.ShapeDtypeStruct((depth, d_model, XBC_W + PROJ_W), BF16),
        compiler_params=pltpu.CompilerParams(
            dimension_semantics=("arbitrary",), vmem_limit_bytes=VMEM_LIMIT_BYTES),
        name="w_in_prep",
    )(w_t)


def _cast_prep_kernel(wout_ref, wpe_ref, wpg_ref, wout_o, wpe_o, wpg_o):
    mix_w = SSD_WIDTH + GLA_WIDTH
    wout_o[0, 0:mix_w, :] = wout_ref[0, 0:mix_w, :].astype(BF16)
    for slot, head in enumerate(SWA_HEAD_ORDER):
        src = slice(mix_w + head * SWA_HEAD_DIM, mix_w + (head + 1) * SWA_HEAD_DIM)
        dst = slice(mix_w + slot * SWA_HEAD_DIM, mix_w + (slot + 1) * SWA_HEAD_DIM)
        wout_o[0, dst, :] = wout_ref[0, src, :].astype(BF16)
    wpe_o[0] = wpe_ref[0].astype(BF16)
    wpg_o[0] = wpg_ref[0].astype(BF16)


def _prepare_out_weights(w_out, w_pe, w_pg):
    depth = w_out.shape[0]
    ops = (w_out, w_pe, w_pg)
    spec = lambda a: pl.BlockSpec((1,) + a.shape[1:], lambda l: (l, 0, 0))
    return pl.pallas_call(
        _cast_prep_kernel,
        grid=(depth,),
        in_specs=[spec(a) for a in ops],
        out_specs=tuple(spec(a) for a in ops),
        out_shape=tuple(jax.ShapeDtypeStruct(a.shape, BF16) for a in ops),
        compiler_params=pltpu.CompilerParams(
            dimension_semantics=("arbitrary",), vmem_limit_bytes=VMEM_LIMIT_BYTES),
        name="w_out_prep",
    )(*ops)


def _prepare_weights(norm_w, w_in, conv_w, conv_b, dt_bias, a_log, d_skip, ssd_norm_w, gla_w_gk, gla_b_gk,
                     gla_norm_w, q_norm_w, k_norm_w, w_out, w_pe, w_pg):
    w_out_p, w_pe_p, w_pg_p = _prepare_out_weights(w_out, w_pe, w_pg)
    lane_pad = lambda x: jnp.pad(x, ((0, 0), (0, LANES - x.shape[-1])))[:, None, :]
    wgk_p = jnp.pad(gla_w_gk, ((0, 0), (LR_LANE0, LANES - LR_LANE0 - GLA_RANK), (0, 0))).astype(BF16)
    return dict(
        norm_w=norm_w[:, None, :], w_in=_prepare_w_in(w_in), conv_w=conv_w, conv_b=conv_b[:, None, :],
        dt_bias=lane_pad(dt_bias), a_log=lane_pad(a_log),
        d_skip=jnp.repeat(d_skip, SSD_HEAD_DIM, axis=-1)[:, None, :], ssd_norm_w=ssd_norm_w[:, None, :],
        gla_w_gk=wgk_p, gla_b_gk=gla_b_gk[:, None, :],
        gla_norm_w=jnp.tile(gla_norm_w, (1, GLA_HEADS))[:, None, :],
        q_norm_w=jnp.tile(q_norm_w, (1, SWA_HEADS))[:, None, :],
        k_norm_w=jnp.tile(k_norm_w, (1, SWA_KV_HEADS))[:, None, :],
        w_out=w_out_p, w_pe=w_pe_p, w_pg=w_pg_p)


def kernel(x_prompt, x_sample, state_ssm, state_conv, state_gla, cache_swa_k, cache_swa_v, p_prompt, p_sample, rel_bias, norm_w, w_in, conv_w, conv_b, dt_bias, a_log, d_skip, ssd_norm_w, gla_w_gk, gla_b_gk, gla_norm_w, q_norm_w, k_norm_w, attn_sinks, w_out, w_pe, w_pg):
    depth = w_in.shape[0]
    bp, seq_p, _ = x_prompt.shape
    bs, seq_s, _ = x_sample.shape
    assert seq_s == SUBLANES and BLK % seq_s == 0 and (bs * seq_s) % BLK == 0
    assert cache_swa_k.shape[2] == WINDOW
    w = _prepare_weights(norm_w, w_in, conv_w, conv_b, dt_bias, a_log, d_skip, ssd_norm_w, gla_w_gk, gla_b_gk,
                         gla_norm_w, q_norm_w, k_norm_w, w_out, w_pe, w_pg)
    rel_flat = rel_bias.reshape(-1)
    dist_p = WINDOW + np.arange(BLK)[:, None] - np.arange(2 * BLK)[None, :]
    bucket_p = jnp.asarray(_bucket_table(dist_p))
    t_of_row = np.tile(np.arange(seq_s), SWA_HEADS)[:, None]
    bucket_c = jnp.asarray(_bucket_table(WINDOW + t_of_row - np.arange(WINDOW)[None, :]))
    dist_n = np.where(np.arange(LANES)[None, :] < seq_s, t_of_row - np.arange(LANES)[None, :], -1)
    bucket_n = jnp.asarray(_bucket_table(dist_n))

    ssm_in = jnp.transpose(state_ssm, (0, 2, 3, 4, 1)).reshape(depth, SSD_HEADS, SSD_HEAD_DIM * SSD_STATE, bs)
    gla_in = jnp.transpose(state_gla, (0, 2, 3, 4, 1)).reshape(depth, GLA_HEADS, GLA_DK * GLA_DV, bs)
    kv_in = lambda a: jnp.transpose(a, (0, 1, 3, 4, 2)).reshape(depth, bs, SWA_KV_HEADS * SWA_HEAD_DIM, WINDOW)
    kc_in, vc_in = kv_in(cache_swa_k), kv_in(cache_swa_v)
    conv_in = jnp.transpose(state_conv, (0, 2, 1, 3))

    hp = x_prompt.reshape(bp * seq_p, D_MODEL)
    p_prompt_rows = p_prompt.reshape(depth, bp * seq_p, PLE_DIM)
    hs = jnp.transpose(x_sample, (1, 0, 2)).reshape(seq_s * bs, D_MODEL)
    p_sample_rows = jnp.transpose(p_sample, (0, 2, 1, 3)).reshape(depth, seq_s * bs, PLE_DIM)
    states_p = ()
    states_s = ()
    for i in range(depth):
        wops, wspecs = _layer_weights(i, w)
        hp, *states_p = _prompt_layer(i, depth, bp, hp, p_prompt_rows, tuple(states_p), bucket_p, rel_flat,
                                      attn_sinks[i], wops, wspecs)
        hs, states_s = _sample_layer_native(i, depth, seq_s, hs, p_sample_rows, conv_in, ssm_in, gla_in, kc_in,
                                            vc_in, states_s, (bucket_c, bucket_n), rel_flat, attn_sinks[i], wops)
    ssm_p, conv_p, gla_p, kt_p, vt_p = states_p
    conv_s, ssm_s, gla_s, kt_s, vt_s = states_s
    unpack_kv = lambda a: jnp.transpose(
        a.reshape(a.shape[:2] + (SWA_KV_HEADS, SWA_HEAD_DIM, WINDOW)), (0, 1, 4, 2, 3))
    outs_p = (ssm_p.reshape(depth, bp, SSD_HEADS, SSD_HEAD_DIM, SSD_STATE), conv_p,
              gla_p.reshape(depth, bp, GLA_HEADS, GLA_DK, GLA_DV), unpack_kv(kt_p), unpack_kv(vt_p))
    seq_last = lambda a, dims: jnp.transpose(a.reshape(a.shape[:2] + dims + (bs,)), (0, 4, 1, 2, 3))
    outs_s = (seq_last(ssm_s, (SSD_HEAD_DIM, SSD_STATE)), jnp.transpose(conv_s, (0, 2, 1, 3)),
              seq_last(gla_s, (GLA_DK, GLA_DV)), unpack_kv(kt_s), unpack_kv(vt_s))
    y_sample = jnp.transpose(hs.reshape(seq_s, bs, D_MODEL), (1, 0, 2))
    return (hp.reshape(bp, seq_p, D_MODEL), y_sample) + outs_p + outs_s
```

```python
import functools
import math

import numpy as np
import jax
import jax.numpy as jnp
from jax import lax
from jax.experimental import pallas as pl
from jax.experimental.pallas import tpu as pltpu

D_MODEL = 1024
DEPTH = 2
SSD_HEADS = 8
SSD_HEAD_DIM = 64
SSD_WIDTH = SSD_HEADS * SSD_HEAD_DIM
SSD_GROUPS = 2
SSD_STATE = 64
SSD_CONV = 4
SSD_CONV_DIM = SSD_WIDTH + 2 * SSD_GROUPS * SSD_STATE
SSD_CHUNK = 128
GLA_HEADS = 4
GLA_DK = 32
GLA_DV = 64
GLA_WIDTH = GLA_HEADS * GLA_DV
GLA_RANK = 16
GLA_GATE_NORM = 16.0
GLA_CHUNK = 64
SWA_HEADS = 4
SWA_KV_HEADS = 2
SWA_HEAD_DIM = 64
SWA_WIDTH = SWA_HEADS * SWA_HEAD_DIM
WINDOW = 128
REL_BUCKETS = 32
REL_MAX_DIST = 128
PLE_DIM = 256
EPS = 1e-6

LANES = 128
SUBLANES = 8
HALF = LANES // 2
BLK = 128
VMEM_LIMIT_BYTES = 56 * 1024 * 1024

XBC_W = SSD_CONV_DIM
P_Z = 0
P_GQ = P_Z + SSD_WIDTH
P_GK = P_GQ + LANES
P_GV = P_GK + LANES
P_GG = P_GV + GLA_WIDTH
P_SQ = P_GG + GLA_WIDTH
P_SK = P_SQ + SWA_WIDTH
P_SV = P_SK + LANES
P_SG = P_SV + LANES
P_DTLR = P_SG + SWA_WIDTH
PROJ_W = P_DTLR + LANES
LR_LANE0 = SSD_HEADS

F32 = jnp.float32
BF16 = jnp.bfloat16
NEG_INF = float("-inf")
LOG2E = math.log2(math.e)
N_PROMPT_INPUTS = 22
PROMPT_CHUNK_ROWS = 2 * BLK
TICKS_PER_ITEM = 2
NT_DIMS = (((1,), (1,)), ((), ()))


def _iota(shape, dim):
    return lax.broadcasted_iota(jnp.int32, shape, dim)


def _div(x, d):
    return x >> (d.bit_length() - 1)


def _mod(x, d):
    return x & (d - 1)


def _softplus(x):
    e = jnp.exp(-jnp.abs(x))
    u = 1.0 + e
    d = u - 1.0
    log1p_e = jnp.where(d == 0.0, e, jnp.log(u) * (e / jnp.where(d == 0.0, 1.0, d)))
    return jnp.maximum(x, 0.0) + log1p_e


def _log_sigmoid(x):
    return jnp.minimum(x, 0.0) - jnp.log(1.0 + jnp.exp(-jnp.abs(x)))


def _silu(x):
    return x * jax.nn.sigmoid(x)


def _dot(a, b):
    return jnp.dot(a.astype(BF16), b.astype(BF16), preferred_element_type=F32)


def _dot_nt(a, b):
    return lax.dot_general(a.astype(BF16), b.astype(BF16), NT_DIMS, preferred_element_type=F32)


def _dot_exact(sel, x):
    x1 = x.astype(BF16)
    r1 = x - x1.astype(F32)
    x2 = r1.astype(BF16)
    x3 = (r1 - x2.astype(F32)).astype(BF16)
    dot = functools.partial(jnp.dot, sel, preferred_element_type=F32)
    return dot(x1) + dot(x2) + dot(x3)


def _expand_heads(x, n_heads):
    rows = x.shape[0]
    lo = _iota((rows, LANES), 1) < HALF
    tiles = []
    for j in range(n_heads // 2):
        a = jnp.broadcast_to(x[:, 2 * j:2 * j + 1], (rows, LANES))
        b = jnp.broadcast_to(x[:, 2 * j + 1:2 * j + 2], (rows, LANES))
        tiles.append(jnp.where(lo, a, b))
    return jnp.concatenate(tiles, axis=1)


def _head_rms_scale(x):
    rows, width = x.shape
    lo = _iota((rows, LANES), 1) < HALF
    outs = []
    for j in range(width // LANES):
        t = x[:, j * LANES:(j + 1) * LANES]
        sq = t * t
        s_lo = jnp.sum(jnp.where(lo, sq, 0.0), axis=-1, keepdims=True)
        s_hi = jnp.sum(jnp.where(lo, 0.0, sq), axis=-1, keepdims=True)
        outs.append(lax.rsqrt(jnp.where(lo, s_lo, s_hi) * (1.0 / HALF) + EPS))
    return outs[0] if len(outs) == 1 else jnp.concatenate(outs, axis=1)


def _group_rmsnorm(y, w):
    gw = SSD_WIDTH // SSD_GROUPS
    outs = []
    for g in range(SSD_GROUPS):
        t = y[:, g * gw:(g + 1) * gw]
        ms = jnp.sum(t * t, axis=-1, keepdims=True) * (1.0 / gw)
        outs.append(t * lax.rsqrt(ms + EPS))
    return jnp.concatenate(outs, axis=1) * w


def _rel_bucket_np(dist):
    n = np.maximum(dist, 0)
    exact = REL_BUCKETS // 2
    nf = np.maximum(n, 1).astype(np.float64)
    large = exact + (np.log(nf / exact) / math.log(REL_MAX_DIST / exact) * (REL_BUCKETS - exact)).astype(np.int32)
    large = np.minimum(large, REL_BUCKETS - 1)
    return np.where(n < exact, n, large).astype(np.int32)


def _bucket_table(dist):
    return np.where((dist >= 0) & (dist < WINDOW), _rel_bucket_np(dist), -1).astype(np.int32)


def _no_tick():
    pass


def _ssd_intra(xbc_c, dtv, acum, pair_mask, tick=_no_tick):
    xs = xbc_c[:, :SSD_WIDTH]
    bm = xbc_c[:, SSD_WIDTH:SSD_WIDTH + LANES]
    cm = xbc_c[:, SSD_WIDTH + LANES:]
    lane = _iota((BLK, LANES), 1)
    lo = lane < HALF
    acum_t = acum.T
    eacum = jnp.exp2(acum)
    tail = jnp.exp2(acum[BLK - 1:BLK, :] - acum)
    dtv_e = _expand_heads(dtv, SSD_HEADS)
    eacum_e = _expand_heads(eacum, SSD_HEADS)
    tail_e = _expand_heads(tail, SSD_HEADS)
    tick()
    xdt = xs * dtv_e
    xw = xdt * tail_e
    cb = [_dot_nt(jnp.where(lo, cm, 0.0), bm), _dot_nt(jnp.where(lo, 0.0, cm), bm)]
    y_pairs = []
    for j in range(SSD_HEADS // 2):
        tick()
        g = (2 * j) // (SSD_HEADS // SSD_GROUPS)
        ms = []
        for k in range(2):
            h = 2 * j + k
            seg = acum[:, h:h + 1] - acum_t[h:h + 1, :]
            dec = jnp.where(pair_mask, jnp.exp2(seg), 0.0)
            ms.append((cb[g] * dec).astype(BF16))
        xp = xdt[:, j * LANES:(j + 1) * LANES]
        rhs = jnp.concatenate([jnp.where(lo, xp, 0.0), jnp.where(lo, 0.0, xp)], axis=0)
        y_pairs.append(_dot(jnp.concatenate(ms, axis=1), rhs))
    y_intra = jnp.concatenate(y_pairs, axis=1)
    return y_intra, xs, bm, cm, xw, eacum, eacum_e


def _gla_intra(gq, gk, gv, bcs, att_mask, tick=_no_tick):
    eb = jnp.exp2(bcs)
    qe = gq * (GLA_DK ** -0.5) * eb
    ke = gk * jnp.exp2(-bcs)
    btot = jnp.concatenate(
        [jnp.broadcast_to(bcs[(c2 + 1) * GLA_CHUNK - 1:(c2 + 1) * GLA_CHUNK, :], (GLA_CHUNK, LANES))
         for c2 in range(BLK // GLA_CHUNK)], axis=0)
    kd = gk * jnp.exp2(btot - bcs)
    lane_k = _iota((GLA_CHUNK, LANES), 1)
    lane_v = _iota((GLA_CHUNK, GLA_WIDTH), 1)
    outs = []
    for c2 in range(BLK // GLA_CHUNK):
        tick()
        rs = slice(c2 * GLA_CHUNK, (c2 + 1) * GLA_CHUNK)
        ke_c = ke[rs]
        v_c = gv[rs]
        kbd = jnp.concatenate(
            [jnp.where(_div(lane_k, GLA_DK) == h, ke_c, 0.0) for h in range(GLA_HEADS)], axis=0)
        att = _dot_nt(qe[rs], kbd)
        att = jnp.where(att_mask, att, 0.0)
        vbd = jnp.concatenate(
            [jnp.where(_div(lane_v, GLA_DV) == h, v_c, 0.0) for h in range(GLA_HEADS)], axis=0)
        outs.append(_dot(att, vbd))
    return jnp.concatenate(outs, axis=0), qe, kd, jnp.exp2(btot)


def _build_bias(bucket, rel_ref):
    accs = [jnp.full(bucket.shape, NEG_INF, F32) for _ in range(SWA_HEADS)]
    for b in range(REL_BUCKETS):
        hit = bucket == b
        for h in range(SWA_HEADS):
            accs[h] = jnp.where(hit, rel_ref[b * SWA_HEADS + h], accs[h])
    return accs


def _epilogue(h, mix, p, wout_ref, wpg_ref, wpe_ref):
    h1 = h + jnp.dot(mix, wout_ref[...], preferred_element_type=F32)
    gate = jax.nn.sigmoid(jnp.dot(h1.astype(BF16), wpg_ref[...], preferred_element_type=F32))
    pe = jnp.dot(p.astype(BF16), wpe_ref[...], preferred_element_type=F32)
    return h1 + gate * pe


def _prompt_kernel(chunks_per_seq, n_aliased, *refs):
    (ha_ref, hc_ref, p_ref, bucket_ref, rel_ref, sink_ref, nw_ref, win_ref, cw_ref, cb_ref, dtb_ref,
     alog_ref, dsk_ref, snw_ref, wgk_ref, bgk_ref, gnw_ref, qnw_ref, knw_ref, wout_ref,
     wpe_ref, wpg_ref) = refs[:N_PROMPT_INPUTS]
    (y_ref, ssm_ref, conv_ref, gla_ref, ko_ref, vo_ref,
     proj_e, proj_o, xbc_e, xbc_o, mix_e, mix_o, u_s, h1_s, h1b_s, hist_s, st_s, s2_s, kext_s, vext_s,
     bias_s, hist_snap, st_snap, s2_snap, k_snap, v_snap) = refs[N_PROMPT_INPUTS + n_aliased:]
    k_idx = pl.program_id(0)

    @pl.when(k_idx == 0)
    def _():
        accs = [a * LOG2E for a in _build_bias(bucket_ref[...], rel_ref)]
        own_block = _iota((BLK, 2 * BLK), 1) >= BLK
        for hh in range(SWA_HEADS):
            bias_s[hh] = accs[hh]
            bias_s[SWA_HEADS + hh] = jnp.where(own_block, accs[hh], NEG_INF)
        for ref in (proj_o, xbc_o, mix_e, mix_o, hist_s, st_s, s2_s, kext_s, vext_s):
            ref[...] = jnp.zeros(ref.shape, ref.dtype)

    row = _iota((BLK, BLK), 0)
    col = _iota((BLK, BLK), 1)
    causal = row >= col
    tri = jnp.where(causal, 1.0, 0.0).astype(BF16)
    lo = col < HALF
    lane_row = _iota((1, LANES), 1)
    a_row = jnp.where(lane_row < SSD_HEADS, -jnp.exp(alog_ref[...]) * LOG2E, 0.0)
    bd_mask = _div(_iota((LANES, GLA_WIDTH), 0), GLA_DK) == _div(_iota((LANES, GLA_WIDTH), 1), GLA_DV)
    att_t = _iota((GLA_CHUNK, GLA_WIDTH), 0)
    att_s = _mod(_iota((GLA_CHUNK, GLA_WIDTH), 1), GLA_CHUNK)
    att_mask = att_s <= att_t
    lo2 = _iota((2 * BLK, LANES), 1) < HALF

    group_w = SSD_WIDTH // SSD_GROUPS

    def block(blk, proj_s, xbc_s, mix_s, starts_sequence, tick):
        rows = slice(blk * BLK, (blk + 1) * BLK)
        cw = cw_ref[...]
        tick()
        if blk == 0:
            xwin = jnp.concatenate([hist_s[...], xbc_s[0:BLK, :]], axis=0)
        else:
            xwin = xbc_s[blk * BLK - SUBLANES:(blk + 1) * BLK, :]
        acc = xwin[SUBLANES - 3:SUBLANES - 3 + BLK, :] * cw[0:1, :]
        for k in range(1, SSD_CONV):
            acc = acc + xwin[SUBLANES - 3 + k:SUBLANES - 3 + k + BLK, :] * cw[k:k + 1, :]
        xbc_c = _silu(acc + cb_ref[...])
        tick()
        dtlr = proj_s[rows, P_DTLR:P_DTLR + LANES]
        dtv_t = _softplus((dtlr + dtb_ref[...]).T[0:SSD_HEADS, :])
        dtv = jnp.concatenate([dtv_t, jnp.zeros((LANES - SSD_HEADS, BLK), F32)], axis=0).T
        glog = _log_sigmoid(_dot(dtlr, wgk_ref[...]) + bgk_ref[...]) * (LOG2E / GLA_GATE_NORM)
        sums = _dot_exact(tri, jnp.concatenate([dtv * a_row, glog], axis=1))
        acum = sums[:, :LANES]
        gsum = sums[:, LANES:]
        bcs = gsum
        for c2 in range(1, BLK // GLA_CHUNK):
            before = gsum[c2 * GLA_CHUNK - 1:c2 * GLA_CHUNK, :]
            bcs = jnp.where(_div(row, GLA_CHUNK) == c2, gsum - before, bcs)
        tick()
        y_intra, xs, bm, cm, xw, eacum, eacum_e = _ssd_intra(xbc_c, dtv, acum, causal, tick)
        tick()
        st = st_s[...]
        y = y_intra + _dot(cm, st) * eacum_e + dsk_ref[...] * xs
        bm_t = bm.T
        for g in range(SSD_GROUPS):
            gr = slice(g * SSD_STATE, (g + 1) * SSD_STATE)
            gc = slice(g * group_w, (g + 1) * group_w)
            st_s[gr, gc] = st[gr, gc] * eacum_e[BLK - 1:BLK, gc] + _dot(bm_t[gr, :], xw[:, gc])
        tick()
        y = y * _silu(proj_s[rows, P_Z:P_Z + SSD_WIDTH])
        mix_s[rows, 0:SSD_WIDTH] = _group_rmsnorm(y, snw_ref[...]).astype(BF16)
        tick()
        gk = proj_s[rows, P_GK:P_GK + LANES]
        gv = proj_s[rows, P_GV:P_GV + GLA_WIDTH]
        o_intra, qe, kd, ebt = _gla_intra(proj_s[rows, P_GQ:P_GQ + LANES], gk, gv, bcs, att_mask, tick)
        kd_t = kd.T
        ebt_t = ebt.T
        s2 = s2_s[...]
        o_parts = []
        for c2 in range(BLK // GLA_CHUNK):
            tick()
            rs = slice(c2 * GLA_CHUNK, (c2 + 1) * GLA_CHUNK)
            o_parts.append(o_intra[rs] + _dot(qe[rs], s2))
            u2 = _dot(jnp.where(_div(col, GLA_CHUNK) == c2, kd_t, 0.0), gv)
            last = (c2 + 1) * GLA_CHUNK - 1
            s2 = s2 * ebt_t[:, last:last + 1] + jnp.where(bd_mask, u2, 0.0)
        s2_s[...] = s2
        o = jnp.concatenate(o_parts, axis=0)
        y_gla = o * _head_rms_scale(o) * gnw_ref[...] * _silu(proj_s[rows, P_GG:P_GG + GLA_WIDTH])
        mix_s[rows, SSD_WIDTH:SSD_WIDTH + GLA_WIDTH] = y_gla.astype(BF16)
        tick()
        sq = proj_s[rows, P_SQ:P_SQ + SWA_WIDTH]
        qn = sq * _head_rms_scale(sq) * qnw_ref[...] * (SWA_HEAD_DIM ** -0.5 * LOG2E)
        sk = proj_s[rows, P_SK:P_SK + LANES]
        kn = sk * _head_rms_scale(sk) * knw_ref[...]
        vn = proj_s[rows, P_SV:P_SV + LANES]
        kext_s[BLK:2 * BLK, :] = kn
        vext_s[BLK:2 * BLK, :] = vn
        kext = kext_s[...]
        vext = vext_s[...]
        qa = qn[:, :LANES]
        qb = qn[:, LANES:]
        qs = jnp.concatenate([jnp.where(lo, qa, 0.0), jnp.where(lo, qb, 0.0),
                              jnp.where(lo, 0.0, qa), jnp.where(lo, 0.0, qb)], axis=0)
        logits = _dot_nt(qs, kext)
        tick()
        if blk == 0 and starts_sequence is not False:
            bias_row0 = jnp.where(starts_sequence, SWA_HEADS, 0)
        else:
            bias_row0 = 0
        es = []
        invs = []
        for hh in range(SWA_HEADS):
            tick()
            sink = sink_ref[hh] * LOG2E
            l = logits[hh * BLK:(hh + 1) * BLK] + bias_s[bias_row0 + hh]
            m = jnp.maximum(jnp.max(l, axis=-1, keepdims=True), sink)
            e = jnp.exp2(l - m)
            den = jnp.sum(e, axis=-1, keepdims=True) + jnp.exp2(sink - m)
            es.append(e.astype(BF16))
            invs.append(1.0 / den)
        v_stack = jnp.concatenate([jnp.where(lo2, vext, 0.0), jnp.where(lo2, 0.0, vext)], axis=0)
        tile_a = _dot(jnp.concatenate([es[0], es[2]], axis=1), v_stack) * jnp.where(lo, invs[0], invs[2])
        tile_b = _dot(jnp.concatenate([es[1], es[3]], axis=1), v_stack) * jnp.where(lo, invs[1], invs[3])
        oa = jnp.concatenate([tile_a, tile_b], axis=1)
        y_swa = oa * _silu(proj_s[rows, P_SG:P_SG + SWA_WIDTH])
        mix_s[rows, SSD_WIDTH + GLA_WIDTH:] = y_swa.astype(BF16)
        kext_s[0:BLK, :] = kn
        vext_s[0:BLK, :] = vn

    chunk = proj_e.shape[0]

    def project_items(rows, proj_s, xbc_s):
        def norm():
            h = ha_ref[rows, :]
            ms = jnp.mean(h * h, axis=-1, keepdims=True)
            u_s[...] = (h * lax.rsqrt(ms + EPS) * nw_ref[...]).astype(BF16)

        def cols(dst, lo_c, hi_c, w_off):
            def item():
                dst[:, lo_c:hi_c] = jnp.dot(u_s[...], win_ref[:, w_off + lo_c:w_off + hi_c],
                                            preferred_element_type=F32)
            return item

        step = 2 * LANES
        items = [norm]
        items += [cols(xbc_s, c, min(c + step, XBC_W), 0) for c in range(0, XBC_W, step)]
        items += [cols(proj_s, c, min(c + step, PROJ_W), XBC_W) for c in range(0, PROJ_W, step)]
        return items

    def epilogue_items(rows, mix_s):
        half_w = 2 * LANES

        def residual(c):
            def item():
                h1 = hc_ref[rows, c:c + half_w] + jnp.dot(
                    mix_s[...], wout_ref[:, c:c + half_w], preferred_element_type=F32)
                h1_s[:, c:c + half_w] = h1
                h1b_s[:, c:c + half_w] = h1.astype(BF16)
            return item

        def gated(c):
            def item():
                gate = jax.nn.sigmoid(jnp.dot(h1b_s[...], wpg_ref[:, c:c + half_w], preferred_element_type=F32))
                pe = jnp.dot(p_ref[0, rows, :].astype(BF16), wpe_ref[:, c:c + half_w],
                             preferred_element_type=F32)
                y_ref[rows, c:c + half_w] = h1_s[:, c:c + half_w] + gate * pe
            return item

        col0 = range(0, D_MODEL, half_w)
        return [residual(c) for c in col0] + [gated(c) for c in col0]

    def merge(first, second):
        out = list(first)
        for j, item in enumerate(second):
            out.insert(((j + 1) * len(first)) // len(second) + j, item)
        return out

    def mixer(proj_s, xbc_s, mix_s, starts_sequence, items):
        if starts_sequence is not False:
            keep = jnp.where(starts_sequence, 0.0, 1.0)
            for ref in (hist_s, st_s, s2_s):
                ref[...] = ref[...] * keep
            kext_s[0:BLK, :] = kext_s[0:BLK, :] * keep
            vext_s[0:BLK, :] = vext_s[0:BLK, :] * keep
        queue = list(items)
        calls = [0]

        def tick():
            calls[0] += 1
            if queue and calls[0] % TICKS_PER_ITEM == 0:
                queue.pop(0)()

        for blk in range(chunk // BLK):
            block(blk, proj_s, xbc_s, mix_s, starts_sequence, tick)
        while queue:
            queue.pop(0)()
        hist_s[...] = xbc_s[chunk - SUBLANES:chunk, :]

    def snapshot_states():
        st_snap[...] = st_s[...]
        s2_snap[...] = s2_s[...]
        hist_snap[...] = hist_s[...]
        k_snap[...] = kext_s[0:BLK, :]
        v_snap[...] = vext_s[0:BLK, :]

    def write_states():
        st = st_snap[...]
        stc = st[:SSD_STATE] + st[SSD_STATE:]
        ssm_ref[0, 0] = jnp.concatenate([stc, stc], axis=0).T[:, :SSD_STATE]
        conv_ref[0, 0] = hist_snap[SUBLANES - (SSD_CONV - 1):SUBLANES, :]
        s2 = s2_snap[...]
        w = s2[:, :LANES] + s2[:, LANES:]
        gla_ref[0, 0] = w[:, :GLA_DV] + w[:, GLA_DV:]
        ko_ref[0, 0] = k_snap[...].T
        vo_ref[0, 0] = v_snap[...].T

    even = slice(0, chunk)
    odd = slice(chunk, 2 * chunk)
    mixer(proj_o, xbc_o, mix_o, False, merge(project_items(even, proj_e, xbc_e), epilogue_items(even, mix_e)))
    snapshot_states()
    mixer(proj_e, xbc_e, mix_e, _mod(2 * k_idx, chunks_per_seq) == 0,
          merge(project_items(odd, proj_o, xbc_o), epilogue_items(odd, mix_o)))

    @pl.when((k_idx >= 1) & (_mod(2 * k_idx - 1, chunks_per_seq) == chunks_per_seq - 1))
    def _():
        write_states()


N_FRONT_INPUTS = 12
SAMPLE_POSITIONS_PER_STEP = 2


def _sample_front_kernel(seq, n_aliased, *refs):
    (h_ref, cst_ref, nw_ref, win_ref, cw_ref, cb_ref, dtb_ref, alog_ref, wgk_ref, bgk_ref,
     qnw_ref, knw_ref) = refs[:N_FRONT_INPUTS]
    (xs_ref, gates_ref, xt_ref, bt_ref, ct_ref, at_ref, qt_ref, kt_ref, egt_ref, vt_ref,
     qn_ref, kn_ref, vn_ref, conv_ref, u_s, xbc_s, proj_s) = refs[N_FRONT_INPUTS + n_aliased:]
    n_seq = BLK
    per_step = h_ref.shape[0] // n_seq
    step = pl.program_id(0)
    ht = h_ref[...]
    ms = jnp.mean(ht * ht, axis=-1, keepdims=True)
    u_s[...] = (ht * lax.rsqrt(ms + EPS) * nw_ref[...]).astype(BF16)
    xbc_s[pl.ds(pl.multiple_of(step * per_step * n_seq, n_seq), per_step * n_seq), :] = jnp.dot(
        u_s[...], win_ref[:, :XBC_W], preferred_element_type=F32)
    proj_s[...] = jnp.dot(u_s[...], win_ref[:, XBC_W:], preferred_element_type=F32)
    cw = cw_ref[...]
    a_row = jnp.where(_iota((1, LANES), 1) < SSD_HEADS, -jnp.exp(alog_ref[...]), 0.0)
    for i in range(per_step):
        t = step * per_step + i
        rows = slice(i * n_seq, (i + 1) * n_seq)

        def raw_xbc(back):
            cur = xbc_s[pl.ds(pl.multiple_of(jnp.maximum(t - back, 0) * n_seq, n_seq), n_seq), :]
            if back == 0:
                return cur
            old = cst_ref[0, jnp.clip(SSD_CONV - 1 + t - back, 0, SSD_CONV - 2)]
            return jnp.where(t >= back, cur, old)

        acc = raw_xbc(SSD_CONV - 1) * cw[0:1, :]
        for k in range(1, SSD_CONV):
            acc = acc + raw_xbc(SSD_CONV - 1 - k) * cw[k:k + 1, :]
        xbc_c = _silu(acc + cb_ref[...])
        xs = xbc_c[:, :SSD_WIDTH]
        dtlr = proj_s[rows, P_DTLR:P_DTLR + LANES]
        dtv = _softplus(dtlr + dtb_ref[...])
        xs_ref[rows, :] = xs
        xt_ref[i] = (xs * _expand_heads(dtv, SSD_HEADS)).T
        bt_ref[i] = xbc_c[:, SSD_WIDTH:SSD_WIDTH + LANES].T
        ct_ref[i] = xbc_c[:, SSD_WIDTH + LANES:].T
        at_ref[i] = jnp.exp(dtv * a_row).T[:SSD_HEADS, :]
        glog = _log_sigmoid(_dot(dtlr, wgk_ref[...]) + bgk_ref[...]) * (1.0 / GLA_GATE_NORM)
        qt_ref[i] = (proj_s[rows, P_GQ:P_GQ + LANES] * (GLA_DK ** -0.5)).T
        kt_ref[i] = proj_s[rows, P_GK:P_GK + LANES].T
        egt_ref[i] = jnp.exp(glog).T
        vt_ref[i] = proj_s[rows, P_GV:P_GV + GLA_WIDTH].T
        sq = proj_s[rows, P_SQ:P_SQ + SWA_WIDTH]
        qn_ref[rows, :] = sq * _head_rms_scale(sq) * qnw_ref[...] * (SWA_HEAD_DIM ** -0.5)
        sk = proj_s[rows, P_SK:P_SK + LANES]
        kn_ref[rows, :] = sk * _head_rms_scale(sk) * knw_ref[...]
        vn_ref[rows, :] = proj_s[rows, P_SV:P_SV + LANES]
        gates_ref[rows, :] = jnp.concatenate(
            [_silu(proj_s[rows, P_Z:P_Z + SSD_WIDTH]), _silu(proj_s[rows, P_GG:P_GG + GLA_WIDTH]),
             _silu(proj_s[rows, P_SG:P_SG + SWA_WIDTH])], axis=1)
        first_kept = seq - (SSD_CONV - 1)

        @pl.when(t >= first_kept)
        def _():
            conv_ref[0, jnp.maximum(t - first_kept, 0)] = raw_xbc(0)


N_STATE_INPUTS = 19
SEQ_PER_STEP = 16


def _sample_state_kernel(seq, n_aliased, *refs):
    (xt_ref, bt_ref, ct_ref, at_ref, qt_ref, kt_ref, egt_ref, vt_ref, qn_ref, kn_ref, vn_ref,
     ssm_ref, gla_ref, kc_ref, vc_ref, bucket_c_ref, bucket_n_ref, rel_ref, sink_ref) = refs[:N_STATE_INPUTS]
    (ssm_o, gla_o, ko_ref, vo_ref, yt_ref, ot_ref, oswa_ref,
     qa_s, qb_s, krow_s, vrow_s, oa_s, ob_s, biasc_s, biasn_s) = refs[N_STATE_INPUTS + n_aliased:]
    j = pl.program_id(0)
    n_seq = LANES
    head_of_row = _div(_iota((SWA_HEADS * seq, LANES), 0), seq)

    def by_head(values):
        out = values[SWA_HEADS - 1]
        for hh in range(SWA_HEADS - 2, -1, -1):
            out = jnp.where(head_of_row == hh, values[hh], out)
        return out

    @pl.when(j == 0)
    def _():
        biasc_s[...] = by_head(_build_bias(bucket_c_ref[...], rel_ref))
        biasn_s[...] = by_head(_build_bias(bucket_n_ref[...], rel_ref))

    sub = _iota((SUBLANES, LANES), 0)
    a_rows = [jnp.sum(jnp.where(sub == j, at_ref[t], 0.0), axis=0, keepdims=True) for t in range(seq)]

    def ssd_body(p8, carry):
        r8 = pl.multiple_of(p8 * SUBLANES, SUBLANES)
        x_tiles = [xt_ref[t, pl.ds(r8, SUBLANES), :] for t in range(seq)]
        y_rows = [[] for _ in range(seq)]
        for pp in range(SUBLANES):
            r64 = pl.multiple_of((p8 * SUBLANES + pp) * SSD_STATE, SSD_STATE)
            slab = ssm_ref[0, 0, pl.ds(r64, SSD_STATE), :]
            for t in range(seq):
                slab = slab * a_rows[t] + x_tiles[t][pp:pp + 1, :] * bt_ref[t]
                y_rows[t].append(jnp.sum(ct_ref[t] * slab, axis=0, keepdims=True))
            ssm_o[0, 0, pl.ds(r64, SSD_STATE), :] = slab
        for t in range(seq):
            yt_ref[t, pl.ds(r8, SUBLANES), :] = jnp.concatenate(y_rows[t], axis=0)
        return carry

    lax.fori_loop(0, SSD_HEAD_DIM // SUBLANES, ssd_body, 0)

    @pl.when(j < GLA_HEADS)
    def _():
        for t in range(seq):
            ot_ref[t] = jnp.zeros((GLA_DV, LANES), F32)

        def gla_body(d8, carry):
            r8 = pl.multiple_of(d8 * SUBLANES, SUBLANES)
            q_tiles = [qt_ref[t, pl.ds(r8, SUBLANES), :] for t in range(seq)]
            k_tiles = [kt_ref[t, pl.ds(r8, SUBLANES), :] for t in range(seq)]
            g_tiles = [egt_ref[t, pl.ds(r8, SUBLANES), :] for t in range(seq)]
            for dd in range(SUBLANES):
                r64 = pl.multiple_of((d8 * SUBLANES + dd) * GLA_DV, GLA_DV)
                slab = gla_ref[0, 0, pl.ds(r64, GLA_DV), :]
                for t in range(seq):
                    slab = slab * g_tiles[t][dd:dd + 1, :] + k_tiles[t][dd:dd + 1, :] * vt_ref[t]
                    ot_ref[t] = ot_ref[t] + q_tiles[t][dd:dd + 1, :] * slab
                gla_o[0, 0, pl.ds(r64, GLA_DV), :] = slab
            return carry

        lax.fori_loop(0, GLA_DK // SUBLANES, gla_body, 0)

    base = pl.multiple_of(j * SEQ_PER_STEP, SEQ_PER_STEP)
    for t in range(seq):
        src = pl.ds(t * n_seq + base, SEQ_PER_STEP)
        dst = pl.ds(t, SEQ_PER_STEP, stride=seq)
        qa_s[dst, :] = qn_ref[src, 0:LANES]
        qb_s[dst, :] = qn_ref[src, LANES:2 * LANES]
        krow_s[dst, :] = kn_ref[src, :]
        vrow_s[dst, :] = vn_ref[src, :]
    kn_t = krow_s[...].T
    vn_t = vrow_s[...].T
    keep_old = _iota((LANES, WINDOW), 1) < WINDOW - seq
    lo8 = _iota((seq, LANES), 1) < HALF
    sink_col = by_head([jnp.full((SWA_HEADS * seq, LANES), sink_ref[hh], F32) for hh in range(SWA_HEADS)])[:, 0:1]

    def swa_stages(bl):
        r8 = pl.multiple_of(bl * seq, seq)
        v = {}

        def logits():
            qa = qa_s[pl.ds(r8, seq), :]
            qb = qb_s[pl.ds(r8, seq), :]
            qs = jnp.concatenate([jnp.where(lo8, qa, 0.0), jnp.where(lo8, qb, 0.0),
                                  jnp.where(lo8, 0.0, qa), jnp.where(lo8, 0.0, qb)], axis=0)
            v["lc"] = _dot(qs, kc_ref[0, bl]) + biasc_s[...]
            v["ln"] = _dot_nt(qs, krow_s[pl.ds(r8, seq), :]) + biasn_s[:, 0:seq]

        def softmax():
            lc, ln = v["lc"], v["ln"]
            m = jnp.maximum(jnp.maximum(jnp.max(lc, axis=-1, keepdims=True), jnp.max(ln, axis=-1, keepdims=True)),
                            sink_col)
            v["ec"] = jnp.exp(lc - m)
            v["en"] = jnp.exp(ln - m)
            v["inv"] = 1.0 / (jnp.sum(v["ec"], axis=-1, keepdims=True) + jnp.sum(v["en"], axis=-1, keepdims=True)
                              + jnp.exp(sink_col - m))

        def values():
            o = (_dot_nt(v["ec"], vc_ref[0, bl]) + _dot(v["en"], vrow_s[pl.ds(r8, seq), :])) * v["inv"]
            oa_s[pl.ds(r8, seq), :] = jnp.where(lo8, o[0:seq], o[2 * seq:3 * seq])
            ob_s[pl.ds(r8, seq), :] = jnp.where(lo8, o[seq:2 * seq], o[3 * seq:4 * seq])

        def window():
            ko_ref[0, bl] = jnp.where(keep_old, pltpu.roll(kc_ref[0, bl], WINDOW - seq, axis=1),
                                      pltpu.roll(kn_t, WINDOW - seq - r8, axis=1))
            vo_ref[0, bl] = jnp.where(keep_old, pltpu.roll(vc_ref[0, bl], WINDOW - seq, axis=1),
                                      pltpu.roll(vn_t, WINDOW - seq - r8, axis=1))

        return [logits, softmax, values, window]

    all_stages = [swa_stages(bl) for bl in range(SEQ_PER_STEP)]
    for k in range(len(all_stages[0])):
        for stages in all_stages:
            stages[k]()
    for t in range(seq):
        src = pl.ds(t, SEQ_PER_STEP, stride=seq)
        oswa_ref[t] = jnp.concatenate([oa_s[src, :], ob_s[src, :]], axis=1)


def _sample_back_kernel(yt_ref, ot_ref, oswa_ref, xs_ref, gates_ref, h_ref, p_ref, dsk_ref, snw_ref, gnw_ref,
                        wout_ref, wpe_ref, wpg_ref, y_ref, mix_s):
    n_seq = BLK
    for t in range(yt_ref.shape[0]):
        rows = slice(t * n_seq, (t + 1) * n_seq)
        y = (yt_ref[t].T + dsk_ref[...] * xs_ref[rows, :]) * gates_ref[rows, 0:SSD_WIDTH]
        mix_s[rows, 0:SSD_WIDTH] = _group_rmsnorm(y, snw_ref[...]).astype(BF16)
        o = ot_ref[t].T
        y_gla = o * _head_rms_scale(o) * gnw_ref[...] * gates_ref[rows, SSD_WIDTH:SSD_WIDTH + GLA_WIDTH]
        mix_s[rows, SSD_WIDTH:SSD_WIDTH + GLA_WIDTH] = y_gla.astype(BF16)
        mix_s[rows, SSD_WIDTH + GLA_WIDTH:] = (oswa_ref[t] * gates_ref[rows, SSD_WIDTH + GLA_WIDTH:]).astype(BF16)
    y_ref[...] = _epilogue(h_ref[...], mix_s[...], p_ref[0], wout_ref, wpg_ref, wpe_ref)


def _const_spec(shape):
    nd = len(shape)
    return pl.BlockSpec(shape, lambda *_: (0,) * nd)


def _smem_spec():
    return pl.BlockSpec(memory_space=pltpu.SMEM)


def _layer_spec(arr, layer):
    return pl.BlockSpec((None,) + arr.shape[1:], lambda *_: (layer, 0, 0), pipeline_mode=pl.Buffered(1))


def _layer_weights(layer, w):
    ops = [w[name] for name in ("norm_w", "w_in", "conv_w", "conv_b", "dt_bias", "a_log", "d_skip", "ssd_norm_w",
                                "gla_w_gk", "gla_b_gk", "gla_norm_w", "q_norm_w", "k_norm_w", "w_out", "w_pe",
                                "w_pg")]
    return ops, [_layer_spec(o, layer) for o in ops]


def _prompt_layer(layer, depth, bsz, h, p_all, prev_states, bucket, rel, sinks, wops, wspecs):
    rows_total, _ = h.shape
    seq_len = rows_total // bsz
    chunk = PROMPT_CHUNK_ROWS
    pair = 2 * chunk
    chunks_per_seq = seq_len // chunk
    assert seq_len % pair == 0 and chunks_per_seq & (chunks_per_seq - 1) == 0
    n_pairs = rows_total // pair
    kern = functools.partial(_prompt_kernel, chunks_per_seq, len(prev_states))
    proj_rows = pl.BlockSpec((pair, D_MODEL), lambda k: (jnp.minimum(k, n_pairs - 1), 0))
    out_rows = pl.BlockSpec((pair, D_MODEL), lambda k: (jnp.maximum(k - 1, 0), 0))
    p_spec = pl.BlockSpec((1, pair, PLE_DIM), lambda k: (layer, jnp.maximum(k - 1, 0), 0))
    per_seq = lambda s: pl.BlockSpec(
        (1, 1) + s, lambda k: (layer, jnp.maximum(2 * k - 1, 0) // chunks_per_seq) + (0,) * len(s))
    state_shapes = ((SSD_WIDTH, SSD_STATE), (SSD_CONV - 1, SSD_CONV_DIM), (GLA_HEADS * GLA_DK, GLA_DV),
                    (LANES, WINDOW), (LANES, WINDOW))
    out_shape = (jax.ShapeDtypeStruct((rows_total, D_MODEL), F32),) + tuple(
        jax.ShapeDtypeStruct((depth, bsz) + s, F32) for s in state_shapes)
    return pl.pallas_call(
        kern,
        grid=(n_pairs + 1,),
        in_specs=[proj_rows, out_rows, p_spec, _const_spec(bucket.shape), _smem_spec(), _smem_spec()]
        + wspecs + [pl.BlockSpec(memory_space=pl.ANY)] * len(prev_states),
        out_specs=(out_rows,) + tuple(per_seq(s) for s in state_shapes),
        out_shape=out_shape,
        input_output_aliases={N_PROMPT_INPUTS + k: 1 + k for k in range(len(prev_states))},
        scratch_shapes=[
            pltpu.VMEM((chunk, PROJ_W), F32), pltpu.VMEM((chunk, PROJ_W), F32),
            pltpu.VMEM((chunk, XBC_W), F32), pltpu.VMEM((chunk, XBC_W), F32),
            pltpu.VMEM((chunk, D_MODEL), BF16), pltpu.VMEM((chunk, D_MODEL), BF16),
            pltpu.VMEM((chunk, D_MODEL), BF16),
            pltpu.VMEM((chunk, D_MODEL), F32),
            pltpu.VMEM((chunk, D_MODEL), BF16),
            pltpu.VMEM((SUBLANES, XBC_W), F32),
            pltpu.VMEM((BLK, SSD_WIDTH), F32),
            pltpu.VMEM((LANES, GLA_WIDTH), F32),
            pltpu.VMEM((2 * BLK, LANES), F32),
            pltpu.VMEM((2 * BLK, LANES), F32),
            pltpu.VMEM((2 * SWA_HEADS, BLK, 2 * BLK), F32),
            pltpu.VMEM((SUBLANES, XBC_W), F32), pltpu.VMEM((BLK, SSD_WIDTH), F32),
            pltpu.VMEM((LANES, GLA_WIDTH), F32), pltpu.VMEM((BLK, LANES), F32),
            pltpu.VMEM((BLK, LANES), F32),
        ],
        compiler_params=pltpu.CompilerParams(
            dimension_semantics=("arbitrary",), vmem_limit_bytes=VMEM_LIMIT_BYTES),
        name="prompt_layer",
    )(h, h, p_all, bucket, rel, sinks, *wops, *prev_states)


def _whole(shape, layer=None):
    if layer is None:
        return pl.BlockSpec(shape, lambda *_: (0,) * len(shape), pipeline_mode=pl.Buffered(1))
    return pl.BlockSpec((1,) + shape[1:], lambda *_: (layer,) + (0,) * (len(shape) - 1),
                        pipeline_mode=pl.Buffered(1))


def _sample_layer_native(layer, depth, seq, h, p_all, conv_in, ssm_in, gla_in, kc_in, vc_in, prev_states,
                         buckets, rel, sinks, wops):
    (nw, win, cw, cb, dtb, alog, dsk, snw, wgk, bgk, gnw, qnw, knw, wout, wpe, wpg) = wops
    rows = h.shape[0]
    n_seq = rows // seq
    assert n_seq == LANES and n_seq % SEQ_PER_STEP == 0 and SSD_HEADS * SEQ_PER_STEP == n_seq
    prev_conv, prev_rest = (prev_states[:1], prev_states[1:]) if prev_states else ((), ())
    f32 = lambda *s: jax.ShapeDtypeStruct(s, F32)
    cparams = lambda sem: pltpu.CompilerParams(dimension_semantics=sem, vmem_limit_bytes=VMEM_LIMIT_BYTES)

    per_step = SAMPLE_POSITIONS_PER_STEP
    step_rows = per_step * n_seq
    assert seq % per_step == 0
    row_blk = lambda w: pl.BlockSpec((step_rows, w), lambda s: (s, 0))
    pos_blk = lambda n: pl.BlockSpec((per_step, n, n_seq), lambda s: (s, 0, 0))
    front_in = [h, conv_in, nw, win, cw, cb, dtb, alog, wgk, bgk, qnw, knw]
    front_specs = [row_blk(D_MODEL), _whole(conv_in.shape, layer)] + [_layer_spec(a, layer) for a in front_in[2:]]
    front_out = (f32(rows, SSD_WIDTH), f32(rows, D_MODEL),
                 f32(seq, SSD_WIDTH, n_seq), f32(seq, LANES, n_seq), f32(seq, LANES, n_seq),
                 f32(seq, SSD_HEADS, n_seq), f32(seq, LANES, n_seq), f32(seq, LANES, n_seq), f32(seq, LANES, n_seq),
                 f32(seq, GLA_WIDTH, n_seq), f32(rows, SWA_WIDTH), f32(rows, LANES), f32(rows, LANES),
                 f32(*conv_in.shape))
    front_out_specs = (row_blk(SSD_WIDTH), row_blk(D_MODEL), pos_blk(SSD_WIDTH), pos_blk(LANES), pos_blk(LANES),
                       pos_blk(SSD_HEADS), pos_blk(LANES), pos_blk(LANES), pos_blk(LANES), pos_blk(GLA_WIDTH),
                       row_blk(SWA_WIDTH), row_blk(LANES), row_blk(LANES), _whole(conv_in.shape, layer))
    (xs, gates, xt, bt, ct, at, qt, kt, egt, vt, qn, kn, vn, conv_o) = pl.pallas_call(
        functools.partial(_sample_front_kernel, seq, len(prev_conv)),
        grid=(seq // per_step,),
        in_specs=front_specs + [pl.BlockSpec(memory_space=pl.ANY)] * len(prev_conv),
        out_specs=front_out_specs,
        out_shape=front_out,
        input_output_aliases={len(front_in) + k: len(front_out) - 1 + k for k in range(len(prev_conv))},
        scratch_shapes=[pltpu.VMEM((step_rows, D_MODEL), BF16), pltpu.VMEM((rows, XBC_W), F32),
                        pltpu.VMEM((step_rows, PROJ_W), F32)],
        compiler_params=cparams(("arbitrary",)),
        name="sample_front",
    )(*front_in, *prev_conv)

    n_steps = SSD_HEADS
    per_group = SSD_HEADS // SSD_GROUPS
    gla_head = lambda j: jnp.minimum(j, GLA_HEADS - 1)
    blk3 = lambda n, f: pl.BlockSpec((seq, n, n_seq), lambda j: (0, f(j), 0))
    state_in = [xt, bt, ct, at, qt, kt, egt, vt, qn, kn, vn, ssm_in, gla_in, kc_in, vc_in, buckets[0], buckets[1],
                rel, sinks]
    ssm_spec = pl.BlockSpec((1, 1) + ssm_in.shape[2:], lambda j: (layer, j, 0, 0))
    gla_spec = pl.BlockSpec((1, 1) + gla_in.shape[2:], lambda j: (layer, gla_head(j), 0, 0))
    kv_spec = pl.BlockSpec((1, SEQ_PER_STEP) + kc_in.shape[2:], lambda j: (layer, j, 0, 0))
    state_specs = [blk3(SSD_HEAD_DIM, lambda j: j), blk3(SSD_STATE, lambda j: j // per_group),
                   blk3(SSD_STATE, lambda j: j // per_group), _const_spec(at.shape),
                   blk3(GLA_DK, gla_head), blk3(GLA_DK, gla_head), blk3(GLA_DK, gla_head), blk3(GLA_DV, gla_head),
                   _const_spec(qn.shape), _const_spec(kn.shape), _const_spec(vn.shape),
                   ssm_spec, gla_spec, kv_spec, kv_spec,
                   _const_spec(buckets[0].shape), _const_spec(buckets[1].shape), _smem_spec(), _smem_spec()]
    state_out = (f32(*ssm_in.shape), f32(*gla_in.shape), f32(*kc_in.shape), f32(*vc_in.shape),
                 f32(seq, SSD_WIDTH, n_seq), f32(seq, GLA_WIDTH, n_seq), f32(seq, n_seq, SWA_WIDTH))
    ssm_o, gla_o, ko, vo, yt, ot, oswa = pl.pallas_call(
        functools.partial(_sample_state_kernel, seq, len(prev_rest)),
        grid=(n_steps,),
        in_specs=state_specs + [pl.BlockSpec(memory_space=pl.ANY)] * len(prev_rest),
        out_specs=(ssm_spec, gla_spec, kv_spec, kv_spec, blk3(SSD_HEAD_DIM, lambda j: j), blk3(GLA_DV, gla_head),
                   pl.BlockSpec((seq, SEQ_PER_STEP, SWA_WIDTH), lambda j: (0, j, 0))),
        out_shape=state_out,
        input_output_aliases={len(state_in) + k: k for k in range(len(prev_rest))},
        scratch_shapes=[pltpu.VMEM((SEQ_PER_STEP * seq, LANES), F32)] * 6
        + [pltpu.VMEM((SWA_HEADS * seq, LANES), F32)] * 2,
        compiler_params=cparams(("arbitrary",)),
        name="sample_state",
    )(*state_in, *prev_rest)

    back_in = [yt, ot, oswa, xs, gates, h, p_all, dsk, snw, gnw, wout, wpe, wpg]
    back_specs = [pos_blk(SSD_WIDTH), pos_blk(GLA_WIDTH),
                  pl.BlockSpec((per_step, n_seq, SWA_WIDTH), lambda s: (s, 0, 0)),
                  row_blk(SSD_WIDTH), row_blk(D_MODEL), row_blk(D_MODEL),
                  pl.BlockSpec((1, step_rows, PLE_DIM), lambda s: (layer, s, 0))] + [
        _layer_spec(a, layer) for a in back_in[7:]]
    y = pl.pallas_call(
        _sample_back_kernel,
        grid=(seq // per_step,),
        in_specs=back_specs,
        out_specs=row_blk(D_MODEL),
        out_shape=f32(rows, D_MODEL),
        scratch_shapes=[pltpu.VMEM((step_rows, D_MODEL), BF16)],
        compiler_params=cparams(("arbitrary",)),
        name="sample_back",
    )(*back_in)
    return y, (conv_o, ssm_o, gla_o, ko, vo)


SWA_HEAD_ORDER = (0, 2, 1, 3)


def _win_tile_runs():
    sizes = (SSD_WIDTH, SSD_CONV_DIM, SSD_HEADS, GLA_HEADS * GLA_DK, GLA_HEADS * GLA_DK, GLA_WIDTH, GLA_WIDTH,
             GLA_RANK, SWA_WIDTH, SWA_KV_HEADS * SWA_HEAD_DIM, SWA_KV_HEADS * SWA_HEAD_DIM, SWA_WIDTH)
    offs = np.concatenate([[0], np.cumsum(sizes)])
    seg = lambda k: np.arange(offs[k], offs[k + 1])
    z, xbc, dt, gq, gk, gv, gg, glr, sq, sk, sv, sg = [seg(k) for k in range(len(sizes))]
    heads = lambda a: np.concatenate([a[h * SWA_HEAD_DIM:(h + 1) * SWA_HEAD_DIM] for h in SWA_HEAD_ORDER])
    pad = np.full(LANES - SSD_HEADS - GLA_RANK, -1)
    src = np.concatenate([xbc, z, gq, gk, gv, gg, heads(sq), sk, sv, heads(sg), dt, glr, pad])
    assert src.size == XBC_W + PROJ_W
    tiles = []
    for j in range(src.size // LANES):
        idx = src[j * LANES:(j + 1) * LANES]
        cuts = [0] + [k for k in range(1, LANES) if (idx[k] != idx[k - 1] + 1 and not (idx[k] == -1 == idx[k - 1]))]
        runs = [(int(idx[a]), b - a) for a, b in zip(cuts, cuts[1:] + [LANES])]
        assert all(n % SUBLANES == 0 and (s < 0 or s % SUBLANES == 0) for s, n in runs)
        tiles.append(runs)
    return tiles


def _win_prep_kernel(tile_runs, wt_ref, out_ref):
    for j, runs in enumerate(tile_runs):
        parts = [jnp.zeros((n, D_MODEL), F32) if s < 0 else wt_ref[0, s:s + n, :] for s, n in runs]
        tile = parts[0] if len(parts) == 1 else jnp.concatenate(parts, axis=0)
        out_ref[0, :, j * LANES:(j + 1) * LANES] = tile.T.astype(BF16)


def _prepare_w_in(w_in):
    depth, d_model, d_in = w_in.shape
    w_t = jnp.swapaxes(w_in, 1, 2)
    return pl.pallas_call(
        functools.partial(_win_prep_kernel, _win_tile_runs()),
        grid=(depth,),
        in_specs=[pl.BlockSpec((1, d_in, d_model), lambda l: (l, 0, 0))],
        out_specs=pl.BlockSpec((1, d_model, XBC_W + PROJ_W), lambda l: (l, 0, 0)),
        out_shape=jax.ShapeDtypeStruct((depth, d_model, XBC_W + PROJ_W), BF16),
        compiler_params=pltpu.CompilerParams(
            dimension_semantics=("arbitrary",), vmem_limit_bytes=VMEM_LIMIT_BYTES),
        name="w_in_prep",
    )(w_t)


def _cast_prep_kernel(wout_ref, wpe_ref, wpg_ref, wout_o, wpe_o, wpg_o):
    mix_w = SSD_WIDTH + GLA_WIDTH
    wout_o[0, 0:mix_w, :] = wout_ref[0, 0:mix_w, :].astype(BF16)
    for slot, head in enumerate(SWA_HEAD_ORDER):
        src = slice(mix_w + head * SWA_HEAD_DIM, mix_w + (head + 1) * SWA_HEAD_DIM)
        dst = slice(mix_w + slot * SWA_HEAD_DIM, mix_w + (slot + 1) * SWA_HEAD_DIM)
        wout_o[0, dst, :] = wout_ref[0, src, :].astype(BF16)
    wpe_o[0] = wpe_ref[0].astype(BF16)
    wpg_o[0] = wpg_ref[0].astype(BF16)


def _prepare_out_weights(w_out, w_pe, w_pg):
    depth = w_out.shape[0]
    ops = (w_out, w_pe, w_pg)
    spec = lambda a: pl.BlockSpec((1,) + a.shape[1:], lambda l: (l, 0, 0))
    return pl.pallas_call(
        _cast_prep_kernel,
        grid=(depth,),
        in_specs=[spec(a) for a in ops],
        out_specs=tuple(spec(a) for a in ops),
        out_shape=tuple(jax.ShapeDtypeStruct(a.shape, BF16) for a in ops),
        compiler_params=pltpu.CompilerParams(
            dimension_semantics=("arbitrary",), vmem_limit_bytes=VMEM_LIMIT_BYTES),
        name="w_out_prep",
    )(*ops)


def _prepare_weights(norm_w, w_in, conv_w, conv_b, dt_bias, a_log, d_skip, ssd_norm_w, gla_w_gk, gla_b_gk,
                     gla_norm_w, q_norm_w, k_norm_w, w_out, w_pe, w_pg):
    w_out_p, w_pe_p, w_pg_p = _prepare_out_weights(w_out, w_pe, w_pg)
    lane_pad = lambda x: jnp.pad(x, ((0, 0), (0, LANES - x.shape[-1])))[:, None, :]
    wgk_p = jnp.pad(gla_w_gk, ((0, 0), (LR_LANE0, LANES - LR_LANE0 - GLA_RANK), (0, 0))).astype(BF16)
    return dict(
        norm_w=norm_w[:, None, :], w_in=_prepare_w_in(w_in), conv_w=conv_w, conv_b=conv_b[:, None, :],
        dt_bias=lane_pad(dt_bias), a_log=lane_pad(a_log),
        d_skip=jnp.repeat(d_skip, SSD_HEAD_DIM, axis=-1)[:, None, :], ssd_norm_w=ssd_norm_w[:, None, :],
        gla_w_gk=wgk_p, gla_b_gk=gla_b_gk[:, None, :],
        gla_norm_w=jnp.tile(gla_norm_w, (1, GLA_HEADS))[:, None, :],
        q_norm_w=jnp.tile(q_norm_w, (1, SWA_HEADS))[:, None, :],
        k_norm_w=jnp.tile(k_norm_w, (1, SWA_KV_HEADS))[:, None, :],
        w_out=w_out_p, w_pe=w_pe_p, w_pg=w_pg_p)


def kernel(x_prompt, x_sample, state_ssm, state_conv, state_gla, cache_swa_k, cache_swa_v, p_prompt, p_sample, rel_bias, norm_w, w_in, conv_w, conv_b, dt_bias, a_log, d_skip, ssd_norm_w, gla_w_gk, gla_b_gk, gla_norm_w, q_norm_w, k_norm_w, attn_sinks, w_out, w_pe, w_pg):
    depth = w_in.shape[0]
    bp, seq_p, _ = x_prompt.shape
    bs, seq_s, _ = x_sample.shape
    assert seq_s == SUBLANES and BLK % seq_s == 0 and (bs * seq_s) % BLK == 0
    assert cache_swa_k.shape[2] == WINDOW
    w = _prepare_weights(norm_w, w_in, conv_w, conv_b, dt_bias, a_log, d_skip, ssd_norm_w, gla_w_gk, gla_b_gk,
                         gla_norm_w, q_norm_w, k_norm_w, w_out, w_pe, w_pg)
    rel_flat = rel_bias.reshape(-1)
    dist_p = WINDOW + np.arange(BLK)[:, None] - np.arange(2 * BLK)[None, :]
    bucket_p = jnp.asarray(_bucket_table(dist_p))
    t_of_row = np.tile(np.arange(seq_s), SWA_HEADS)[:, None]
    bucket_c = jnp.asarray(_bucket_table(WINDOW + t_of_row - np.arange(WINDOW)[None, :]))
    dist_n = np.where(np.arange(LANES)[None, :] < seq_s, t_of_row - np.arange(LANES)[None, :], -1)
    bucket_n = jnp.asarray(_bucket_table(dist_n))

    ssm_in = jnp.transpose(state_ssm, (0, 2, 3, 4, 1)).reshape(depth, SSD_HEADS, SSD_HEAD_DIM * SSD_STATE, bs)
    gla_in = jnp.transpose(state_gla, (0, 2, 3, 4, 1)).reshape(depth, GLA_HEADS, GLA_DK * GLA_DV, bs)
    kv_in = lambda a: jnp.transpose(a, (0, 1, 3, 4, 2)).reshape(depth, bs, SWA_KV_HEADS * SWA_HEAD_DIM, WINDOW)
    kc_in, vc_in = kv_in(cache_swa_k), kv_in(cache_swa_v)
    conv_in = jnp.transpose(state_conv, (0, 2, 1, 3))

    hp = x_prompt.reshape(bp * seq_p, D_MODEL)
    p_prompt_rows = p_prompt.reshape(depth, bp * seq_p, PLE_DIM)
    hs = jnp.transpose(x_sample, (1, 0, 2)).reshape(seq_s * bs, D_MODEL)
    p_sample_rows = jnp.transpose(p_sample, (0, 2, 1, 3)).reshape(depth, seq_s * bs, PLE_DIM)
    states_p = ()
    states_s = ()
    for i in range(depth):
        wops, wspecs = _layer_weights(i, w)
        hp, *states_p = _prompt_layer(i, depth, bp, hp, p_prompt_rows, tuple(states_p), bucket_p, rel_flat,
                                      attn_sinks[i], wops, wspecs)
        hs, states_s = _sample_layer_native(i, depth, seq_s, hs, p_sample_rows, conv_in, ssm_in, gla_in, kc_in,
                                            vc_in, states_s, (bucket_c, bucket_n), rel_flat, attn_sinks[i], wops)
    ssm_p, conv_p, gla_p, kt_p, vt_p = states_p
    conv_s, ssm_s, gla_s, kt_s, vt_s = states_s
    unpack_kv = lambda a: jnp.transpose(
        a.reshape(a.shape[:2] + (SWA_KV_HEADS, SWA_HEAD_DIM, WINDOW)), (0, 1, 4, 2, 3))
    outs_p = (ssm_p.reshape(depth, bp, SSD_HEADS, SSD_HEAD_DIM, SSD_STATE), conv_p,
              gla_p.reshape(depth, bp, GLA_HEADS, GLA_DK, GLA_DV), unpack_kv(kt_p), unpack_kv(vt_p))
    seq_last = lambda a, dims: jnp.transpose(a.reshape(a.shape[:2] + dims + (bs,)), (0, 4, 1, 2, 3))
    outs_s = (seq_last(ssm_s, (SSD_HEAD_DIM, SSD_STATE)), jnp.transpose(conv_s, (0, 2, 1, 3)),
              seq_last(gla_s, (GLA_DK, GLA_DV)), unpack_kv(kt_s), unpack_kv(vt_s))
    y_sample = jnp.transpose(hs.reshape(seq_s, bs, D_MODEL), (1, 0, 2))
    return (hp.reshape(bp, seq_p, D_MODEL), y_sample) + outs_p + outs_s
```

```python
import functools
import math

import numpy as np
import jax
import jax.numpy as jnp
from jax import lax
from jax.experimental import pallas as pl
from jax.experimental.pallas import tpu as pltpu

D_MODEL = 1024
DEPTH = 2
SSD_HEADS = 8
SSD_HEAD_DIM = 64
SSD_WIDTH = SSD_HEADS * SSD_HEAD_DIM
SSD_GROUPS = 2
SSD_STATE = 64
SSD_CONV = 4
SSD_CONV_DIM = SSD_WIDTH + 2 * SSD_GROUPS * SSD_STATE
SSD_CHUNK = 128
GLA_HEADS = 4
GLA_DK = 32
GLA_DV = 64
GLA_WIDTH = GLA_HEADS * GLA_DV
GLA_RANK = 16
GLA_GATE_NORM = 16.0
GLA_CHUNK = 64
SWA_HEADS = 4
SWA_KV_HEADS = 2
SWA_HEAD_DIM = 64
SWA_WIDTH = SWA_HEADS * SWA_HEAD_DIM
WINDOW = 128
REL_BUCKETS = 32
REL_MAX_DIST = 128
PLE_DIM = 256
EPS = 1e-6

LANES = 128
SUBLANES = 8
HALF = LANES // 2
BLK = 128
VMEM_LIMIT_BYTES = 56 * 1024 * 1024

XBC_W = SSD_CONV_DIM
P_Z = 0
P_GQ = P_Z + SSD_WIDTH
P_GK = P_GQ + LANES
P_GV = P_GK + LANES
P_GG = P_GV + GLA_WIDTH
P_SQ = P_GG + GLA_WIDTH
P_SK = P_SQ + SWA_WIDTH
P_SV = P_SK + LANES
P_SG = P_SV + LANES
P_DTLR = P_SG + SWA_WIDTH
PROJ_W = P_DTLR + LANES
LR_LANE0 = SSD_HEADS

F32 = jnp.float32
BF16 = jnp.bfloat16
NEG_INF = float("-inf")
LOG2E = math.log2(math.e)
N_PROMPT_INPUTS = 22
PROMPT_CHUNK_ROWS = 2 * BLK
TICKS_PER_ITEM = 2
NT_DIMS = (((1,), (1,)), ((), ()))


def _iota(shape, dim):
    return lax.broadcasted_iota(jnp.int32, shape, dim)


def _div(x, d):
    return x >> (d.bit_length() - 1)


def _mod(x, d):
    return x & (d - 1)


def _softplus(x):
    e = jnp.exp(-jnp.abs(x))
    u = 1.0 + e
    d = u - 1.0
    log1p_e = jnp.where(d == 0.0, e, jnp.log(u) * (e / jnp.where(d == 0.0, 1.0, d)))
    return jnp.maximum(x, 0.0) + log1p_e


def _log_sigmoid(x):
    return jnp.minimum(x, 0.0) - jnp.log(1.0 + jnp.exp(-jnp.abs(x)))


def _silu(x):
    return x * jax.nn.sigmoid(x)


def _dot(a, b):
    return jnp.dot(a.astype(BF16), b.astype(BF16), preferred_element_type=F32)


def _dot_nt(a, b):
    return lax.dot_general(a.astype(BF16), b.astype(BF16), NT_DIMS, preferred_element_type=F32)


def _dot_exact(sel, x):
    x1 = x.astype(BF16)
    r1 = x - x1.astype(F32)
    x2 = r1.astype(BF16)
    x3 = (r1 - x2.astype(F32)).astype(BF16)
    dot = functools.partial(jnp.dot, sel, preferred_element_type=F32)
    return dot(x1) + dot(x2) + dot(x3)


def _expand_heads(x, n_heads):
    rows = x.shape[0]
    lo = _iota((rows, LANES), 1) < HALF
    tiles = []
    for j in range(n_heads // 2):
        a = jnp.broadcast_to(x[:, 2 * j:2 * j + 1], (rows, LANES))
        b = jnp.broadcast_to(x[:, 2 * j + 1:2 * j + 2], (rows, LANES))
        tiles.append(jnp.where(lo, a, b))
    return jnp.concatenate(tiles, axis=1)


def _head_rms_scale(x):
    rows, width = x.shape
    lo = _iota((rows, LANES), 1) < HALF
    outs = []
    for j in range(width // LANES):
        t = x[:, j * LANES:(j + 1) * LANES]
        sq = t * t
        s_lo = jnp.sum(jnp.where(lo, sq, 0.0), axis=-1, keepdims=True)
        s_hi = jnp.sum(jnp.where(lo, 0.0, sq), axis=-1, keepdims=True)
        outs.append(lax.rsqrt(jnp.where(lo, s_lo, s_hi) * (1.0 / HALF) + EPS))
    return outs[0] if len(outs) == 1 else jnp.concatenate(outs, axis=1)


def _group_rmsnorm(y, w):
    gw = SSD_WIDTH // SSD_GROUPS
    outs = []
    for g in range(SSD_GROUPS):
        t = y[:, g * gw:(g + 1) * gw]
        ms = jnp.sum(t * t, axis=-1, keepdims=True) * (1.0 / gw)
        outs.append(t * lax.rsqrt(ms + EPS))
    return jnp.concatenate(outs, axis=1) * w


def _rel_bucket_np(dist):
    n = np.maximum(dist, 0)
    exact = REL_BUCKETS // 2
    nf = np.maximum(n, 1).astype(np.float64)
    large = exact + (np.log(nf / exact) / math.log(REL_MAX_DIST / exact) * (REL_BUCKETS - exact)).astype(np.int32)
    large = np.minimum(large, REL_BUCKETS - 1)
    return np.where(n < exact, n, large).astype(np.int32)


def _bucket_table(dist):
    return np.where((dist >= 0) & (dist < WINDOW), _rel_bucket_np(dist), -1).astype(np.int32)


def _no_tick():
    pass


def _ssd_intra(xbc_c, dtv, acum, pair_mask, tick=_no_tick):
    xs = xbc_c[:, :SSD_WIDTH]
    bm = xbc_c[:, SSD_WIDTH:SSD_WIDTH + LANES]
    cm = xbc_c[:, SSD_WIDTH + LANES:]
    lane = _iota((BLK, LANES), 1)
    lo = lane < HALF
    acum_t = acum.T
    eacum = jnp.exp2(acum)
    tail = jnp.exp2(acum[BLK - 1:BLK, :] - acum)
    dtv_e = _expand_heads(dtv, SSD_HEADS)
    eacum_e = _expand_heads(eacum, SSD_HEADS)
    tail_e = _expand_heads(tail, SSD_HEADS)
    tick()
    xdt = xs * dtv_e
    xw = xdt * tail_e
    cb = [_dot_nt(jnp.where(lo, cm, 0.0), bm), _dot_nt(jnp.where(lo, 0.0, cm), bm)]
    y_pairs = []
    for j in range(SSD_HEADS // 2):
        tick()
        g = (2 * j) // (SSD_HEADS // SSD_GROUPS)
        ms = []
        for k in range(2):
            h = 2 * j + k
            seg = acum[:, h:h + 1] - acum_t[h:h + 1, :]
            dec = jnp.where(pair_mask, jnp.exp2(seg), 0.0)
            ms.append((cb[g] * dec).astype(BF16))
        xp = xdt[:, j * LANES:(j + 1) * LANES]
        rhs = jnp.concatenate([jnp.where(lo, xp, 0.0), jnp.where(lo, 0.0, xp)], axis=0)
        y_pairs.append(_dot(jnp.concatenate(ms, axis=1), rhs))
    y_intra = jnp.concatenate(y_pairs, axis=1)
    return y_intra, xs, bm, cm, xw, eacum, eacum_e


def _gla_intra(gq, gk, gv, bcs, att_mask, tick=_no_tick):
    eb = jnp.exp2(bcs)
    qe = gq * (GLA_DK ** -0.5) * eb
    ke = gk * jnp.exp2(-bcs)
    btot = jnp.concatenate(
        [jnp.broadcast_to(bcs[(c2 + 1) * GLA_CHUNK - 1:(c2 + 1) * GLA_CHUNK, :], (GLA_CHUNK, LANES))
         for c2 in range(BLK // GLA_CHUNK)], axis=0)
    kd = gk * jnp.exp2(btot - bcs)
    lane_k = _iota((GLA_CHUNK, LANES), 1)
    lane_v = _iota((GLA_CHUNK, GLA_WIDTH), 1)
    outs = []
    for c2 in range(BLK // GLA_CHUNK):
        tick()
        rs = slice(c2 * GLA_CHUNK, (c2 + 1) * GLA_CHUNK)
        ke_c = ke[rs]
        v_c = gv[rs]
        kbd = jnp.concatenate(
            [jnp.where(_div(lane_k, GLA_DK) == h, ke_c, 0.0) for h in range(GLA_HEADS)], axis=0)
        att = _dot_nt(qe[rs], kbd)
        att = jnp.where(att_mask, att, 0.0)
        vbd = jnp.concatenate(
            [jnp.where(_div(lane_v, GLA_DV) == h, v_c, 0.0) for h in range(GLA_HEADS)], axis=0)
        outs.append(_dot(att, vbd))
    return jnp.concatenate(outs, axis=0), qe, kd, jnp.exp2(btot)


def _build_bias(bucket, rel_ref):
    accs = [jnp.full(bucket.shape, NEG_INF, F32) for _ in range(SWA_HEADS)]
    for b in range(REL_BUCKETS):
        hit = bucket == b
        for h in range(SWA_HEADS):
            accs[h] = jnp.where(hit, rel_ref[b * SWA_HEADS + h], accs[h])
    return accs


def _epilogue(h, mix, p, wout_ref, wpg_ref, wpe_ref):
    h1 = h + jnp.dot(mix, wout_ref[...], preferred_element_type=F32)
    gate = jax.nn.sigmoid(jnp.dot(h1.astype(BF16), wpg_ref[...], preferred_element_type=F32))
    pe = jnp.dot(p.astype(BF16), wpe_ref[...], preferred_element_type=F32)
    return h1 + gate * pe


def _prompt_kernel(chunks_per_seq, n_aliased, *refs):
    (ha_ref, hc_ref, p_ref, bucket_ref, rel_ref, sink_ref, nw_ref, win_ref, cw_ref, cb_ref, dtb_ref,
     alog_ref, dsk_ref, snw_ref, wgk_ref, bgk_ref, gnw_ref, qnw_ref, knw_ref, wout_ref,
     wpe_ref, wpg_ref) = refs[:N_PROMPT_INPUTS]
    (y_ref, ssm_ref, conv_ref, gla_ref, ko_ref, vo_ref,
     proj_e, proj_o, xbc_e, xbc_o, mix_e, mix_o, u_s, h1_s, h1b_s, hist_s, st_s, s2_s, kext_s, vext_s,
     bias_s, hist_snap, st_snap, s2_snap, k_snap, v_snap) = refs[N_PROMPT_INPUTS + n_aliased:]
    k_idx = pl.program_id(0)

    @pl.when(k_idx == 0)
    def _():
        accs = [a * LOG2E for a in _build_bias(bucket_ref[...], rel_ref)]
        own_block = _iota((BLK, 2 * BLK), 1) >= BLK
        for hh in range(SWA_HEADS):
            bias_s[hh] = accs[hh]
            bias_s[SWA_HEADS + hh] = jnp.where(own_block, accs[hh], NEG_INF)
        for ref in (proj_o, xbc_o, mix_e, mix_o, hist_s, st_s, s2_s, kext_s, vext_s):
            ref[...] = jnp.zeros(ref.shape, ref.dtype)

    row = _iota((BLK, BLK), 0)
    col = _iota((BLK, BLK), 1)
    causal = row >= col
    tri = jnp.where(causal, 1.0, 0.0).astype(BF16)
    lo = col < HALF
    lane_row = _iota((1, LANES), 1)
    a_row = jnp.where(lane_row < SSD_HEADS, -jnp.exp(alog_ref[...]) * LOG2E, 0.0)
    bd_mask = _div(_iota((LANES, GLA_WIDTH), 0), GLA_DK) == _div(_iota((LANES, GLA_WIDTH), 1), GLA_DV)
    att_t = _iota((GLA_CHUNK, GLA_WIDTH), 0)
    att_s = _mod(_iota((GLA_CHUNK, GLA_WIDTH), 1), GLA_CHUNK)
    att_mask = att_s <= att_t
    lo2 = _iota((2 * BLK, LANES), 1) < HALF

    group_w = SSD_WIDTH // SSD_GROUPS

    def block(blk, proj_s, xbc_s, mix_s, starts_sequence, tick):
        rows = slice(blk * BLK, (blk + 1) * BLK)
        cw = cw_ref[...]
        tick()
        if blk == 0:
            xwin = jnp.concatenate([hist_s[...], xbc_s[0:BLK, :]], axis=0)
        else:
            xwin = xbc_s[blk * BLK - SUBLANES:(blk + 1) * BLK, :]
        acc = xwin[SUBLANES - 3:SUBLANES - 3 + BLK, :] * cw[0:1, :]
        for k in range(1, SSD_CONV):
            acc = acc + xwin[SUBLANES - 3 + k:SUBLANES - 3 + k + BLK, :] * cw[k:k + 1, :]
        xbc_c = _silu(acc + cb_ref[...])
        tick()
        dtlr = proj_s[rows, P_DTLR:P_DTLR + LANES]
        dtv_t = _softplus((dtlr + dtb_ref[...]).T[0:SSD_HEADS, :])
        dtv = jnp.concatenate([dtv_t, jnp.zeros((LANES - SSD_HEADS, BLK), F32)], axis=0).T
        glog = _log_sigmoid(_dot(dtlr, wgk_ref[...]) + bgk_ref[...]) * (LOG2E / GLA_GATE_NORM)
        sums = _dot_exact(tri, jnp.concatenate([dtv * a_row, glog], axis=1))
        acum = sums[:, :LANES]
        gsum = sums[:, LANES:]
        bcs = gsum
        for c2 in range(1, BLK // GLA_CHUNK):
            before = gsum[c2 * GLA_CHUNK - 1:c2 * GLA_CHUNK, :]
            bcs = jnp.where(_div(row, GLA_CHUNK) == c2, gsum - before, bcs)
        tick()
        y_intra, xs, bm, cm, xw, eacum, eacum_e = _ssd_intra(xbc_c, dtv, acum, causal, tick)
        tick()
        st = st_s[...]
        y = y_intra + _dot(cm, st) * eacum_e + dsk_ref[...] * xs
        bm_t = bm.T
        for g in range(SSD_GROUPS):
            gr = slice(g * SSD_STATE, (g + 1) * SSD_STATE)
            gc = slice(g * group_w, (g + 1) * group_w)
            st_s[gr, gc] = st[gr, gc] * eacum_e[BLK - 1:BLK, gc] + _dot(bm_t[gr, :], xw[:, gc])
        tick()
        y = y * _silu(proj_s[rows, P_Z:P_Z + SSD_WIDTH])
        mix_s[rows, 0:SSD_WIDTH] = _group_rmsnorm(y, snw_ref[...]).astype(BF16)
        tick()
        gk = proj_s[rows, P_GK:P_GK + LANES]
        gv = proj_s[rows, P_GV:P_GV + GLA_WIDTH]
        o_intra, qe, kd, ebt = _gla_intra(proj_s[rows, P_GQ:P_GQ + LANES], gk, gv, bcs, att_mask, tick)
        kd_t = kd.T
        ebt_t = ebt.T
        s2 = s2_s[...]
        o_parts = []
        for c2 in range(BLK // GLA_CHUNK):
            tick()
            rs = slice(c2 * GLA_CHUNK, (c2 + 1) * GLA_CHUNK)
            o_parts.append(o_intra[rs] + _dot(qe[rs], s2))
            u2 = _dot(jnp.where(_div(col, GLA_CHUNK) == c2, kd_t, 0.0), gv)
            last = (c2 + 1) * GLA_CHUNK - 1
            s2 = s2 * ebt_t[:, last:last + 1] + jnp.where(bd_mask, u2, 0.0)
        s2_s[...] = s2
        o = jnp.concatenate(o_parts, axis=0)
        y_gla = o * _head_rms_scale(o) * gnw_ref[...] * _silu(proj_s[rows, P_GG:P_GG + GLA_WIDTH])
        mix_s[rows, SSD_WIDTH:SSD_WIDTH + GLA_WIDTH] = y_gla.astype(BF16)
        tick()
        sq = proj_s[rows, P_SQ:P_SQ + SWA_WIDTH]
        qn = sq * _head_rms_scale(sq) * qnw_ref[...] * (SWA_HEAD_DIM ** -0.5 * LOG2E)
        sk = proj_s[rows, P_SK:P_SK + LANES]
        kn = sk * _head_rms_scale(sk) * knw_ref[...]
        vn = proj_s[rows, P_SV:P_SV + LANES]
        kext_s[BLK:2 * BLK, :] = kn
        vext_s[BLK:2 * BLK, :] = vn
        kext = kext_s[...]
        vext = vext_s[...]
        qa = qn[:, :LANES]
        qb = qn[:, LANES:]
        qs = jnp.concatenate([jnp.where(lo, qa, 0.0), jnp.where(lo, qb, 0.0),
                              jnp.where(lo, 0.0, qa), jnp.where(lo, 0.0, qb)], axis=0)
        logits = _dot_nt(qs, kext)
        tick()
        if blk == 0 and starts_sequence is not False:
            bias_row0 = jnp.where(starts_sequence, SWA_HEADS, 0)
        else:
            bias_row0 = 0
        es = []
        invs = []
        for hh in range(SWA_HEADS):
            tick()
            sink = sink_ref[hh] * LOG2E
            l = logits[hh * BLK:(hh + 1) * BLK] + bias_s[bias_row0 + hh]
            m = jnp.maximum(jnp.max(l, axis=-1, keepdims=True), sink)
            e = jnp.exp2(l - m)
            den = jnp.sum(e, axis=-1, keepdims=True) + jnp.exp2(sink - m)
            es.append(e.astype(BF16))
            invs.append(1.0 / den)
        v_stack = jnp.concatenate([jnp.where(lo2, vext, 0.0), jnp.where(lo2, 0.0, vext)], axis=0)
        tile_a = _dot(jnp.concatenate([es[0], es[2]], axis=1), v_stack) * jnp.where(lo, invs[0], invs[2])
        tile_b = _dot(jnp.concatenate([es[1], es[3]], axis=1), v_stack) * jnp.where(lo, invs[1], invs[3])
        oa = jnp.concatenate([tile_a, tile_b], axis=1)
        y_swa = oa * _silu(proj_s[rows, P_SG:P_SG + SWA_WIDTH])
        mix_s[rows, SSD_WIDTH + GLA_WIDTH:] = y_swa.astype(BF16)
        kext_s[0:BLK, :] = kn
        vext_s[0:BLK, :] = vn

    chunk = proj_e.shape[0]

    def project_items(rows, proj_s, xbc_s):
        def norm():
            h = ha_ref[rows, :]
            ms = jnp.mean(h * h, axis=-1, keepdims=True)
            u_s[...] = (h * lax.rsqrt(ms + EPS) * nw_ref[...]).astype(BF16)

        def cols(dst, lo_c, hi_c, w_off):
            def item():
                dst[:, lo_c:hi_c] = jnp.dot(u_s[...], win_ref[:, w_off + lo_c:w_off + hi_c],
                                            preferred_element_type=F32)
            return item

        step = 2 * LANES
        items = [norm]
        items += [cols(xbc_s, c, min(c + step, XBC_W), 0) for c in range(0, XBC_W, step)]
        items += [cols(proj_s, c, min(c + step, PROJ_W), XBC_W) for c in range(0, PROJ_W, step)]
        return items

    def epilogue_items(rows, mix_s):
        half_w = 2 * LANES

        def residual(c):
            def item():
                h1 = hc_ref[rows, c:c + half_w] + jnp.dot(
                    mix_s[...], wout_ref[:, c:c + half_w], preferred_element_type=F32)
                h1_s[:, c:c + half_w] = h1
                h1b_s[:, c:c + half_w] = h1.astype(BF16)
            return item

        def gated(c):
            def item():
                gate = jax.nn.sigmoid(jnp.dot(h1b_s[...], wpg_ref[:, c:c + half_w], preferred_element_type=F32))
                pe = jnp.dot(p_ref[0, rows, :].astype(BF16), wpe_ref[:, c:c + half_w],
                             preferred_element_type=F32)
                y_ref[rows, c:c + half_w] = h1_s[:, c:c + half_w] + gate * pe
            return item

        col0 = range(0, D_MODEL, half_w)
        return [residual(c) for c in col0] + [gated(c) for c in col0]

    def mixer(proj_s, xbc_s, mix_s, starts_sequence, items):
        if starts_sequence is not False:
            keep = jnp.where(starts_sequence, 0.0, 1.0)
            for ref in (hist_s, st_s, s2_s):
                ref[...] = ref[...] * keep
            kext_s[0:BLK, :] = kext_s[0:BLK, :] * keep
            vext_s[0:BLK, :] = vext_s[0:BLK, :] * keep
        queue = list(items)
        calls = [0]

        def tick():
            calls[0] += 1
            if queue and calls[0] % TICKS_PER_ITEM == 0:
                queue.pop(0)()

        for blk in range(chunk // BLK):
            block(blk, proj_s, xbc_s, mix_s, starts_sequence, tick)
        while queue:
            queue.pop(0)()
        hist_s[...] = xbc_s[chunk - SUBLANES:chunk, :]

    def snapshot_states():
        st_snap[...] = st_s[...]
        s2_snap[...] = s2_s[...]
        hist_snap[...] = hist_s[...]
        k_snap[...] = kext_s[0:BLK, :]
        v_snap[...] = vext_s[0:BLK, :]

    def write_states():
        st = st_snap[...]
        stc = st[:SSD_STATE] + st[SSD_STATE:]
        ssm_ref[0, 0] = jnp.concatenate([stc, stc], axis=0).T[:, :SSD_STATE]
        conv_ref[0, 0] = hist_snap[SUBLANES - (SSD_CONV - 1):SUBLANES, :]
        s2 = s2_snap[...]
        w = s2[:, :LANES] + s2[:, LANES:]
        gla_ref[0, 0] = w[:, :GLA_DV] + w[:, GLA_DV:]
        ko_ref[0, 0] = k_snap[...].T
        vo_ref[0, 0] = v_snap[...].T

    even = slice(0, chunk)
    odd = slice(chunk, 2 * chunk)
    mixer(proj_o, xbc_o, mix_o, False, epilogue_items(even, mix_e) + project_items(even, proj_e, xbc_e))
    snapshot_states()
    mixer(proj_e, xbc_e, mix_e, _mod(2 * k_idx, chunks_per_seq) == 0,
          epilogue_items(odd, mix_o) + project_items(odd, proj_o, xbc_o))

    @pl.when((k_idx >= 1) & (_mod(2 * k_idx - 1, chunks_per_seq) == chunks_per_seq - 1))
    def _():
        write_states()


N_FRONT_INPUTS = 12
SAMPLE_POSITIONS_PER_STEP = 2


def _sample_front_kernel(seq, n_aliased, *refs):
    (h_ref, cst_ref, nw_ref, win_ref, cw_ref, cb_ref, dtb_ref, alog_ref, wgk_ref, bgk_ref,
     qnw_ref, knw_ref) = refs[:N_FRONT_INPUTS]
    (xs_ref, gates_ref, xt_ref, bt_ref, ct_ref, at_ref, qt_ref, kt_ref, egt_ref, vt_ref,
     qn_ref, kn_ref, vn_ref, conv_ref, u_s, xbc_s, proj_s) = refs[N_FRONT_INPUTS + n_aliased:]
    n_seq = BLK
    per_step = h_ref.shape[0] // n_seq
    step = pl.program_id(0)
    ht = h_ref[...]
    ms = jnp.mean(ht * ht, axis=-1, keepdims=True)
    u_s[...] = (ht * lax.rsqrt(ms + EPS) * nw_ref[...]).astype(BF16)
    xbc_s[pl.ds(pl.multiple_of(step * per_step * n_seq, n_seq), per_step * n_seq), :] = jnp.dot(
        u_s[...], win_ref[:, :XBC_W], preferred_element_type=F32)
    proj_s[...] = jnp.dot(u_s[...], win_ref[:, XBC_W:], preferred_element_type=F32)
    cw = cw_ref[...]
    a_row = jnp.where(_iota((1, LANES), 1) < SSD_HEADS, -jnp.exp(alog_ref[...]), 0.0)
    for i in range(per_step):
        t = step * per_step + i
        rows = slice(i * n_seq, (i + 1) * n_seq)

        def raw_xbc(back):
            cur = xbc_s[pl.ds(pl.multiple_of(jnp.maximum(t - back, 0) * n_seq, n_seq), n_seq), :]
            if back == 0:
                return cur
            old = cst_ref[0, jnp.clip(SSD_CONV - 1 + t - back, 0, SSD_CONV - 2)]
            return jnp.where(t >= back, cur, old)

        acc = raw_xbc(SSD_CONV - 1) * cw[0:1, :]
        for k in range(1, SSD_CONV):
            acc = acc + raw_xbc(SSD_CONV - 1 - k) * cw[k:k + 1, :]
        xbc_c = _silu(acc + cb_ref[...])
        xs = xbc_c[:, :SSD_WIDTH]
        dtlr = proj_s[rows, P_DTLR:P_DTLR + LANES]
        dtv = _softplus(dtlr + dtb_ref[...])
        xs_ref[rows, :] = xs
        xt_ref[i] = (xs * _expand_heads(dtv, SSD_HEADS)).T
        bt_ref[i] = xbc_c[:, SSD_WIDTH:SSD_WIDTH + LANES].T
        ct_ref[i] = xbc_c[:, SSD_WIDTH + LANES:].T
        at_ref[i] = jnp.exp(dtv * a_row).T[:SSD_HEADS, :]
        glog = _log_sigmoid(_dot(dtlr, wgk_ref[...]) + bgk_ref[...]) * (1.0 / GLA_GATE_NORM)
        qt_ref[i] = (proj_s[rows, P_GQ:P_GQ + LANES] * (GLA_DK ** -0.5)).T
        kt_ref[i] = proj_s[rows, P_GK:P_GK + LANES].T
        egt_ref[i] = jnp.exp(glog).T
        vt_ref[i] = proj_s[rows, P_GV:P_GV + GLA_WIDTH].T
        sq = proj_s[rows, P_SQ:P_SQ + SWA_WIDTH]
        qn_ref[rows, :] = sq * _head_rms_scale(sq) * qnw_ref[...] * (SWA_HEAD_DIM ** -0.5)
        sk = proj_s[rows, P_SK:P_SK + LANES]
        kn_ref[rows, :] = sk * _head_rms_scale(sk) * knw_ref[...]
        vn_ref[rows, :] = proj_s[rows, P_SV:P_SV + LANES]
        gates_ref[rows, :] = jnp.concatenate(
            [_silu(proj_s[rows, P_Z:P_Z + SSD_WIDTH]), _silu(proj_s[rows, P_GG:P_GG + GLA_WIDTH]),
             _silu(proj_s[rows, P_SG:P_SG + SWA_WIDTH])], axis=1)
        first_kept = seq - (SSD_CONV - 1)

        @pl.when(t >= first_kept)
        def _():
            conv_ref[0, jnp.maximum(t - first_kept, 0)] = raw_xbc(0)


N_STATE_INPUTS = 19
SEQ_PER_STEP = 16


def _sample_state_kernel(seq, n_aliased, *refs):
    (xt_ref, bt_ref, ct_ref, at_ref, qt_ref, kt_ref, egt_ref, vt_ref, qn_ref, kn_ref, vn_ref,
     ssm_ref, gla_ref, kc_ref, vc_ref, bucket_c_ref, bucket_n_ref, rel_ref, sink_ref) = refs[:N_STATE_INPUTS]
    (ssm_o, gla_o, ko_ref, vo_ref, yt_ref, ot_ref, oswa_ref,
     qa_s, qb_s, krow_s, vrow_s, oa_s, ob_s, biasc_s, biasn_s) = refs[N_STATE_INPUTS + n_aliased:]
    j = pl.program_id(0)
    n_seq = LANES
    head_of_row = _div(_iota((SWA_HEADS * seq, LANES), 0), seq)

    def by_head(values):
        out = values[SWA_HEADS - 1]
        for hh in range(SWA_HEADS - 2, -1, -1):
            out = jnp.where(head_of_row == hh, values[hh], out)
        return out

    @pl.when(j == 0)
    def _():
        biasc_s[...] = by_head(_build_bias(bucket_c_ref[...], rel_ref))
        biasn_s[...] = by_head(_build_bias(bucket_n_ref[...], rel_ref))

    sub = _iota((SUBLANES, LANES), 0)
    a_rows = [jnp.sum(jnp.where(sub == j, at_ref[t], 0.0), axis=0, keepdims=True) for t in range(seq)]

    def ssd_body(p8, carry):
        r8 = pl.multiple_of(p8 * SUBLANES, SUBLANES)
        x_tiles = [xt_ref[t, pl.ds(r8, SUBLANES), :] for t in range(seq)]
        y_rows = [[] for _ in range(seq)]
        for pp in range(SUBLANES):
            r64 = pl.multiple_of((p8 * SUBLANES + pp) * SSD_STATE, SSD_STATE)
            slab = ssm_ref[0, 0, pl.ds(r64, SSD_STATE), :]
            for t in range(seq):
                slab = slab * a_rows[t] + x_tiles[t][pp:pp + 1, :] * bt_ref[t]
                y_rows[t].append(jnp.sum(ct_ref[t] * slab, axis=0, keepdims=True))
            ssm_o[0, 0, pl.ds(r64, SSD_STATE), :] = slab
        for t in range(seq):
            yt_ref[t, pl.ds(r8, SUBLANES), :] = jnp.concatenate(y_rows[t], axis=0)
        return carry

    lax.fori_loop(0, SSD_HEAD_DIM // SUBLANES, ssd_body, 0)

    @pl.when(j < GLA_HEADS)
    def _():
        for t in range(seq):
            ot_ref[t] = jnp.zeros((GLA_DV, LANES), F32)

        def gla_body(d8, carry):
            r8 = pl.multiple_of(d8 * SUBLANES, SUBLANES)
            q_tiles = [qt_ref[t, pl.ds(r8, SUBLANES), :] for t in range(seq)]
            k_tiles = [kt_ref[t, pl.ds(r8, SUBLANES), :] for t in range(seq)]
            g_tiles = [egt_ref[t, pl.ds(r8, SUBLANES), :] for t in range(seq)]
            for dd in range(SUBLANES):
                r64 = pl.multiple_of((d8 * SUBLANES + dd) * GLA_DV, GLA_DV)
                slab = gla_ref[0, 0, pl.ds(r64, GLA_DV), :]
                for t in range(seq):
                    slab = slab * g_tiles[t][dd:dd + 1, :] + k_tiles[t][dd:dd + 1, :] * vt_ref[t]
                    ot_ref[t] = ot_ref[t] + q_tiles[t][dd:dd + 1, :] * slab
                gla_o[0, 0, pl.ds(r64, GLA_DV), :] = slab
            return carry

        lax.fori_loop(0, GLA_DK // SUBLANES, gla_body, 0)

    base = pl.multiple_of(j * SEQ_PER_STEP, SEQ_PER_STEP)
    for t in range(seq):
        src = pl.ds(t * n_seq + base, SEQ_PER_STEP)
        dst = pl.ds(t, SEQ_PER_STEP, stride=seq)
        qa_s[dst, :] = qn_ref[src, 0:LANES]
        qb_s[dst, :] = qn_ref[src, LANES:2 * LANES]
        krow_s[dst, :] = kn_ref[src, :]
        vrow_s[dst, :] = vn_ref[src, :]
    kn_t = krow_s[...].T
    vn_t = vrow_s[...].T
    keep_old = _iota((LANES, WINDOW), 1) < WINDOW - seq
    lo8 = _iota((seq, LANES), 1) < HALF
    sink_col = by_head([jnp.full((SWA_HEADS * seq, LANES), sink_ref[hh], F32) for hh in range(SWA_HEADS)])[:, 0:1]

    def swa_stages(bl):
        r8 = pl.multiple_of(bl * seq, seq)
        v = {}

        def logits():
            qa = qa_s[pl.ds(r8, seq), :]
            qb = qb_s[pl.ds(r8, seq), :]
            qs = jnp.concatenate([jnp.where(lo8, qa, 0.0), jnp.where(lo8, qb, 0.0),
                                  jnp.where(lo8, 0.0, qa), jnp.where(lo8, 0.0, qb)], axis=0)
            v["lc"] = _dot(qs, kc_ref[0, bl]) + biasc_s[...]
            v["ln"] = _dot_nt(qs, krow_s[pl.ds(r8, seq), :]) + biasn_s[:, 0:seq]

        def softmax():
            lc, ln = v["lc"], v["ln"]
            m = jnp.maximum(jnp.maximum(jnp.max(lc, axis=-1, keepdims=True), jnp.max(ln, axis=-1, keepdims=True)),
                            sink_col)
            v["ec"] = jnp.exp(lc - m)
            v["en"] = jnp.exp(ln - m)
            v["inv"] = 1.0 / (jnp.sum(v["ec"], axis=-1, keepdims=True) + jnp.sum(v["en"], axis=-1, keepdims=True)
                              + jnp.exp(sink_col - m))

        def values():
            o = (_dot_nt(v["ec"], vc_ref[0, bl]) + _dot(v["en"], vrow_s[pl.ds(r8, seq), :])) * v["inv"]
            oa_s[pl.ds(r8, seq), :] = jnp.where(lo8, o[0:seq], o[2 * seq:3 * seq])
            ob_s[pl.ds(r8, seq), :] = jnp.where(lo8, o[seq:2 * seq], o[3 * seq:4 * seq])

        def window():
            ko_ref[0, bl] = jnp.where(keep_old, pltpu.roll(kc_ref[0, bl], WINDOW - seq, axis=1),
                                      pltpu.roll(kn_t, WINDOW - seq - r8, axis=1))
            vo_ref[0, bl] = jnp.where(keep_old, pltpu.roll(vc_ref[0, bl], WINDOW - seq, axis=1),
                                      pltpu.roll(vn_t, WINDOW - seq - r8, axis=1))

        return [logits, softmax, values, window]

    all_stages = [swa_stages(bl) for bl in range(SEQ_PER_STEP)]
    for k in range(len(all_stages[0])):
        for stages in all_stages:
            stages[k]()
    for t in range(seq):
        src = pl.ds(t, SEQ_PER_STEP, stride=seq)
        oswa_ref[t] = jnp.concatenate([oa_s[src, :], ob_s[src, :]], axis=1)


def _sample_back_kernel(yt_ref, ot_ref, oswa_ref, xs_ref, gates_ref, h_ref, p_ref, dsk_ref, snw_ref, gnw_ref,
                        wout_ref, wpe_ref, wpg_ref, y_ref, mix_s):
    n_seq = BLK
    for t in range(yt_ref.shape[0]):
        rows = slice(t * n_seq, (t + 1) * n_seq)
        y = (yt_ref[t].T + dsk_ref[...] * xs_ref[rows, :]) * gates_ref[rows, 0:SSD_WIDTH]
        mix_s[rows, 0:SSD_WIDTH] = _group_rmsnorm(y, snw_ref[...]).astype(BF16)
        o = ot_ref[t].T
        y_gla = o * _head_rms_scale(o) * gnw_ref[...] * gates_ref[rows, SSD_WIDTH:SSD_WIDTH + GLA_WIDTH]
        mix_s[rows, SSD_WIDTH:SSD_WIDTH + GLA_WIDTH] = y_gla.astype(BF16)
        mix_s[rows, SSD_WIDTH + GLA_WIDTH:] = (oswa_ref[t] * gates_ref[rows, SSD_WIDTH + GLA_WIDTH:]).astype(BF16)
    y_ref[...] = _epilogue(h_ref[...], mix_s[...], p_ref[0], wout_ref, wpg_ref, wpe_ref)


def _const_spec(shape):
    nd = len(shape)
    return pl.BlockSpec(shape, lambda *_: (0,) * nd)


def _smem_spec():
    return pl.BlockSpec(memory_space=pltpu.SMEM)


def _layer_spec(arr, layer):
    return pl.BlockSpec((None,) + arr.shape[1:], lambda *_: (layer, 0, 0), pipeline_mode=pl.Buffered(1))


def _layer_weights(layer, w):
    ops = [w[name] for name in ("norm_w", "w_in", "conv_w", "conv_b", "dt_bias", "a_log", "d_skip", "ssd_norm_w",
                                "gla_w_gk", "gla_b_gk", "gla_norm_w", "q_norm_w", "k_norm_w", "w_out", "w_pe",
                                "w_pg")]
    return ops, [_layer_spec(o, layer) for o in ops]


def _prompt_layer(layer, depth, bsz, h, p_all, prev_states, bucket, rel, sinks, wops, wspecs):
    rows_total, _ = h.shape
    seq_len = rows_total // bsz
    chunk = PROMPT_CHUNK_ROWS
    pair = 2 * chunk
    chunks_per_seq = seq_len // chunk
    assert seq_len % pair == 0 and chunks_per_seq & (chunks_per_seq - 1) == 0
    n_pairs = rows_total // pair
    kern = functools.partial(_prompt_kernel, chunks_per_seq, len(prev_states))
    proj_rows = pl.BlockSpec((pair, D_MODEL), lambda k: (jnp.minimum(k, n_pairs - 1), 0))
    out_rows = pl.BlockSpec((pair, D_MODEL), lambda k: (jnp.maximum(k - 1, 0), 0))
    p_spec = pl.BlockSpec((1, pair, PLE_DIM), lambda k: (layer, jnp.maximum(k - 1, 0), 0))
    per_seq = lambda s: pl.BlockSpec(
        (1, 1) + s, lambda k: (layer, jnp.maximum(2 * k - 1, 0) // chunks_per_seq) + (0,) * len(s))
    state_shapes = ((SSD_WIDTH, SSD_STATE), (SSD_CONV - 1, SSD_CONV_DIM), (GLA_HEADS * GLA_DK, GLA_DV),
                    (LANES, WINDOW), (LANES, WINDOW))
    out_shape = (jax.ShapeDtypeStruct((rows_total, D_MODEL), F32),) + tuple(
        jax.ShapeDtypeStruct((depth, bsz) + s, F32) for s in state_shapes)
    return pl.pallas_call(
        kern,
        grid=(n_pairs + 1,),
        in_specs=[proj_rows, out_rows, p_spec, _const_spec(bucket.shape), _smem_spec(), _smem_spec()]
        + wspecs + [pl.BlockSpec(memory_space=pl.ANY)] * len(prev_states),
        out_specs=(out_rows,) + tuple(per_seq(s) for s in state_shapes),
        out_shape=out_shape,
        input_output_aliases={N_PROMPT_INPUTS + k: 1 + k for k in range(len(prev_states))},
        scratch_shapes=[
            pltpu.VMEM((chunk, PROJ_W), F32), pltpu.VMEM((chunk, PROJ_W), F32),
            pltpu.VMEM((chunk, XBC_W), F32), pltpu.VMEM((chunk, XBC_W), F32),
            pltpu.VMEM((chunk, D_MODEL), BF16), pltpu.VMEM((chunk, D_MODEL), BF16),
            pltpu.VMEM((chunk, D_MODEL), BF16),
            pltpu.VMEM((chunk, D_MODEL), F32),
            pltpu.VMEM((chunk, D_MODEL), BF16),
            pltpu.VMEM((SUBLANES, XBC_W), F32),
            pltpu.VMEM((BLK, SSD_WIDTH), F32),
            pltpu.VMEM((LANES, GLA_WIDTH), F32),
            pltpu.VMEM((2 * BLK, LANES), F32),
            pltpu.VMEM((2 * BLK, LANES), F32),
            pltpu.VMEM((2 * SWA_HEADS, BLK, 2 * BLK), F32),
            pltpu.VMEM((SUBLANES, XBC_W), F32), pltpu.VMEM((BLK, SSD_WIDTH), F32),
            pltpu.VMEM((LANES, GLA_WIDTH), F32), pltpu.VMEM((BLK, LANES), F32),
            pltpu.VMEM((BLK, LANES), F32),
        ],
        compiler_params=pltpu.CompilerParams(
            dimension_semantics=("arbitrary",), vmem_limit_bytes=VMEM_LIMIT_BYTES),
        name="prompt_layer",
    )(h, h, p_all, bucket, rel, sinks, *wops, *prev_states)


def _whole(shape, layer=None):
    if layer is None:
        return pl.BlockSpec(shape, lambda *_: (0,) * len(shape), pipeline_mode=pl.Buffered(1))
    return pl.BlockSpec((1,) + shape[1:], lambda *_: (layer,) + (0,) * (len(shape) - 1),
                        pipeline_mode=pl.Buffered(1))


def _sample_layer_native(layer, depth, seq, h, p_all, conv_in, ssm_in, gla_in, kc_in, vc_in, prev_states,
                         buckets, rel, sinks, wops):
    (nw, win, cw, cb, dtb, alog, dsk, snw, wgk, bgk, gnw, qnw, knw, wout, wpe, wpg) = wops
    rows = h.shape[0]
    n_seq = rows // seq
    assert n_seq == LANES and n_seq % SEQ_PER_STEP == 0 and SSD_HEADS * SEQ_PER_STEP == n_seq
    prev_conv, prev_rest = (prev_states[:1], prev_states[1:]) if prev_states else ((), ())
    f32 = lambda *s: jax.ShapeDtypeStruct(s, F32)
    cparams = lambda sem: pltpu.CompilerParams(dimension_semantics=sem, vmem_limit_bytes=VMEM_LIMIT_BYTES)

    per_step = SAMPLE_POSITIONS_PER_STEP
    step_rows = per_step * n_seq
    assert seq % per_step == 0
    row_blk = lambda w: pl.BlockSpec((step_rows, w), lambda s: (s, 0))
    pos_blk = lambda n: pl.BlockSpec((per_step, n, n_seq), lambda s: (s, 0, 0))
    front_in = [h, conv_in, nw, win, cw, cb, dtb, alog, wgk, bgk, qnw, knw]
    front_specs = [row_blk(D_MODEL), _whole(conv_in.shape, layer)] + [_layer_spec(a, layer) for a in front_in[2:]]
    front_out = (f32(rows, SSD_WIDTH), f32(rows, D_MODEL),
                 f32(seq, SSD_WIDTH, n_seq), f32(seq, LANES, n_seq), f32(seq, LANES, n_seq),
                 f32(seq, SSD_HEADS, n_seq), f32(seq, LANES, n_seq), f32(seq, LANES, n_seq), f32(seq, LANES, n_seq),
                 f32(seq, GLA_WIDTH, n_seq), f32(rows, SWA_WIDTH), f32(rows, LANES), f32(rows, LANES),
                 f32(*conv_in.shape))
    front_out_specs = (row_blk(SSD_WIDTH), row_blk(D_MODEL), pos_blk(SSD_WIDTH), pos_blk(LANES), pos_blk(LANES),
                       pos_blk(SSD_HEADS), pos_blk(LANES), pos_blk(LANES), pos_blk(LANES), pos_blk(GLA_WIDTH),
                       row_blk(SWA_WIDTH), row_blk(LANES), row_blk(LANES), _whole(conv_in.shape, layer))
    (xs, gates, xt, bt, ct, at, qt, kt, egt, vt, qn, kn, vn, conv_o) = pl.pallas_call(
        functools.partial(_sample_front_kernel, seq, len(prev_conv)),
        grid=(seq // per_step,),
        in_specs=front_specs + [pl.BlockSpec(memory_space=pl.ANY)] * len(prev_conv),
        out_specs=front_out_specs,
        out_shape=front_out,
        input_output_aliases={len(front_in) + k: len(front_out) - 1 + k for k in range(len(prev_conv))},
        scratch_shapes=[pltpu.VMEM((step_rows, D_MODEL), BF16), pltpu.VMEM((rows, XBC_W), F32),
                        pltpu.VMEM((step_rows, PROJ_W), F32)],
        compiler_params=cparams(("arbitrary",)),
        name="sample_front",
    )(*front_in, *prev_conv)

    n_steps = SSD_HEADS
    per_group = SSD_HEADS // SSD_GROUPS
    gla_head = lambda j: jnp.minimum(j, GLA_HEADS - 1)
    blk3 = lambda n, f: pl.BlockSpec((seq, n, n_seq), lambda j: (0, f(j), 0))
    state_in = [xt, bt, ct, at, qt, kt, egt, vt, qn, kn, vn, ssm_in, gla_in, kc_in, vc_in, buckets[0], buckets[1],
                rel, sinks]
    ssm_spec = pl.BlockSpec((1, 1) + ssm_in.shape[2:], lambda j: (layer, j, 0, 0))
    gla_spec = pl.BlockSpec((1, 1) + gla_in.shape[2:], lambda j: (layer, gla_head(j), 0, 0))
    kv_spec = pl.BlockSpec((1, SEQ_PER_STEP) + kc_in.shape[2:], lambda j: (layer, j, 0, 0))
    state_specs = [blk3(SSD_HEAD_DIM, lambda j: j), blk3(SSD_STATE, lambda j: j // per_group),
                   blk3(SSD_STATE, lambda j: j // per_group), _const_spec(at.shape),
                   blk3(GLA_DK, gla_head), blk3(GLA_DK, gla_head), blk3(GLA_DK, gla_head), blk3(GLA_DV, gla_head),
                   _const_spec(qn.shape), _const_spec(kn.shape), _const_spec(vn.shape),
                   ssm_spec, gla_spec, kv_spec, kv_spec,
                   _const_spec(buckets[0].shape), _const_spec(buckets[1].shape), _smem_spec(), _smem_spec()]
    state_out = (f32(*ssm_in.shape), f32(*gla_in.shape), f32(*kc_in.shape), f32(*vc_in.shape),
                 f32(seq, SSD_WIDTH, n_seq), f32(seq, GLA_WIDTH, n_seq), f32(seq, n_seq, SWA_WIDTH))
    ssm_o, gla_o, ko, vo, yt, ot, oswa = pl.pallas_call(
        functools.partial(_sample_state_kernel, seq, len(prev_rest)),
        grid=(n_steps,),
        in_specs=state_specs + [pl.BlockSpec(memory_space=pl.ANY)] * len(prev_rest),
        out_specs=(ssm_spec, gla_spec, kv_spec, kv_spec, blk3(SSD_HEAD_DIM, lambda j: j), blk3(GLA_DV, gla_head),
                   pl.BlockSpec((seq, SEQ_PER_STEP, SWA_WIDTH), lambda j: (0, j, 0))),
        out_shape=state_out,
        input_output_aliases={len(state_in) + k: k for k in range(len(prev_rest))},
        scratch_shapes=[pltpu.VMEM((SEQ_PER_STEP * seq, LANES), F32)] * 6
        + [pltpu.VMEM((SWA_HEADS * seq, LANES), F32)] * 2,
        compiler_params=cparams(("arbitrary",)),
        name="sample_state",
    )(*state_in, *prev_rest)

    back_in = [yt, ot, oswa, xs, gates, h, p_all, dsk, snw, gnw, wout, wpe, wpg]
    back_specs = [pos_blk(SSD_WIDTH), pos_blk(GLA_WIDTH),
                  pl.BlockSpec((per_step, n_seq, SWA_WIDTH), lambda s: (s, 0, 0)),
                  row_blk(SSD_WIDTH), row_blk(D_MODEL), row_blk(D_MODEL),
                  pl.BlockSpec((1, step_rows, PLE_DIM), lambda s: (layer, s, 0))] + [
        _layer_spec(a, layer) for a in back_in[7:]]
    y = pl.pallas_call(
        _sample_back_kernel,
        grid=(seq // per_step,),
        in_specs=back_specs,
        out_specs=row_blk(D_MODEL),
        out_shape=f32(rows, D_MODEL),
        scratch_shapes=[pltpu.VMEM((step_rows, D_MODEL), BF16)],
        compiler_params=cparams(("arbitrary",)),
        name="sample_back",
    )(*back_in)
    return y, (conv_o, ssm_o, gla_o, ko, vo)


SWA_HEAD_ORDER = (0, 2, 1, 3)


def _win_tile_runs():
    sizes = (SSD_WIDTH, SSD_CONV_DIM, SSD_HEADS, GLA_HEADS * GLA_DK, GLA_HEADS * GLA_DK, GLA_WIDTH, GLA_WIDTH,
             GLA_RANK, SWA_WIDTH, SWA_KV_HEADS * SWA_HEAD_DIM, SWA_KV_HEADS * SWA_HEAD_DIM, SWA_WIDTH)
    offs = np.concatenate([[0], np.cumsum(sizes)])
    seg = lambda k: np.arange(offs[k], offs[k + 1])
    z, xbc, dt, gq, gk, gv, gg, glr, sq, sk, sv, sg = [seg(k) for k in range(len(sizes))]
    heads = lambda a: np.concatenate([a[h * SWA_HEAD_DIM:(h + 1) * SWA_HEAD_DIM] for h in SWA_HEAD_ORDER])
    pad = np.full(LANES - SSD_HEADS - GLA_RANK, -1)
    src = np.concatenate([xbc, z, gq, gk, gv, gg, heads(sq), sk, sv, heads(sg), dt, glr, pad])
    assert src.size == XBC_W + PROJ_W
    tiles = []
    for j in range(src.size // LANES):
        idx = src[j * LANES:(j + 1) * LANES]
        cuts = [0] + [k for k in range(1, LANES) if (idx[k] != idx[k - 1] + 1 and not (idx[k] == -1 == idx[k - 1]))]
        runs = [(int(idx[a]), b - a) for a, b in zip(cuts, cuts[1:] + [LANES])]
        assert all(n % SUBLANES == 0 and (s < 0 or s % SUBLANES == 0) for s, n in runs)
        tiles.append(runs)
    return tiles


def _win_prep_kernel(tile_runs, wt_ref, out_ref):
    for j, runs in enumerate(tile_runs):
        parts = [jnp.zeros((n, D_MODEL), F32) if s < 0 else wt_ref[0, s:s + n, :] for s, n in runs]
        tile = parts[0] if len(parts) == 1 else jnp.concatenate(parts, axis=0)
        out_ref[0, :, j * LANES:(j + 1) * LANES] = tile.T.astype(BF16)


def _prepare_w_in(w_in):
    depth, d_model, d_in = w_in.shape
    w_t = jnp.swapaxes(w_in, 1, 2)
    return pl.pallas_call(
        functools.partial(_win_prep_kernel, _win_tile_runs()),
        grid=(depth,),
        in_specs=[pl.BlockSpec((1, d_in, d_model), lambda l: (l, 0, 0))],
        out_specs=pl.BlockSpec((1, d_model, XBC_W + PROJ_W), lambda l: (l, 0, 0)),
        out_shape=jax.ShapeDtypeStruct((depth, d_model, XBC_W + PROJ_W), BF16),
        compiler_params=pltpu.CompilerParams(
            dimension_semantics=("arbitrary",), vmem_limit_bytes=VMEM_LIMIT_BYTES),
        name="w_in_prep",
    )(w_t)


def _cast_prep_kernel(wout_ref, wpe_ref, wpg_ref, wout_o, wpe_o, wpg_o):
    mix_w = SSD_WIDTH + GLA_WIDTH
    wout_o[0, 0:mix_w, :] = wout_ref[0, 0:mix_w, :].astype(BF16)
    for slot, head in enumerate(SWA_HEAD_ORDER):
        src = slice(mix_w + head * SWA_HEAD_DIM, mix_w + (head + 1) * SWA_HEAD_DIM)
        dst = slice(mix_w + slot * SWA_HEAD_DIM, mix_w + (slot + 1) * SWA_HEAD_DIM)
        wout_o[0, dst, :] = wout_ref[0, src, :].astype(BF16)
    wpe_o[0] = wpe_ref[0].astype(BF16)
    wpg_o[0] = wpg_ref[0].astype(BF16)


def _prepare_out_weights(w_out, w_pe, w_pg):
    depth = w_out.shape[0]
    ops = (w_out, w_pe, w_pg)
    spec = lambda a: pl.BlockSpec((1,) + a.shape[1:], lambda l: (l, 0, 0))
    return pl.pallas_call(
        _cast_prep_kernel,
        grid=(depth,),
        in_specs=[spec(a) for a in ops],
        out_specs=tuple(spec(a) for a in ops),
        out_shape=tuple(jax.ShapeDtypeStruct(a.shape, BF16) for a in ops),
        compiler_params=pltpu.CompilerParams(
            dimension_semantics=("arbitrary",), vmem_limit_bytes=VMEM_LIMIT_BYTES),
        name="w_out_prep",
    )(*ops)


def _prepare_weights(norm_w, w_in, conv_w, conv_b, dt_bias, a_log, d_skip, ssd_norm_w, gla_w_gk, gla_b_gk,
                     gla_norm_w, q_norm_w, k_norm_w, w_out, w_pe, w_pg):
    w_out_p, w_pe_p, w_pg_p = _prepare_out_weights(w_out, w_pe, w_pg)
    lane_pad = lambda x: jnp.pad(x, ((0, 0), (0, LANES - x.shape[-1])))[:, None, :]
    wgk_p = jnp.pad(gla_w_gk, ((0, 0), (LR_LANE0, LANES - LR_LANE0 - GLA_RANK), (0, 0))).astype(BF16)
    return dict(
        norm_w=norm_w[:, None, :], w_in=_prepare_w_in(w_in), conv_w=conv_w, conv_b=conv_b[:, None, :],
        dt_bias=lane_pad(dt_bias), a_log=lane_pad(a_log),
        d_skip=jnp.repeat(d_skip, SSD_HEAD_DIM, axis=-1)[:, None, :], ssd_norm_w=ssd_norm_w[:, None, :],
        gla_w_gk=wgk_p, gla_b_gk=gla_b_gk[:, None, :],
        gla_norm_w=jnp.tile(gla_norm_w, (1, GLA_HEADS))[:, None, :],
        q_norm_w=jnp.tile(q_norm_w, (1, SWA_HEADS))[:, None, :],
        k_norm_w=jnp.tile(k_norm_w, (1, SWA_KV_HEADS))[:, None, :],
        w_out=w_out_p, w_pe=w_pe_p, w_pg=w_pg_p)


def kernel(x_prompt, x_sample, state_ssm, state_conv, state_gla, cache_swa_k, cache_swa_v, p_prompt, p_sample, rel_bias, norm_w, w_in, conv_w, conv_b, dt_bias, a_log, d_skip, ssd_norm_w, gla_w_gk, gla_b_gk, gla_norm_w, q_norm_w, k_norm_w, attn_sinks, w_out, w_pe, w_pg):
    depth = w_in.shape[0]
    bp, seq_p, _ = x_prompt.shape
    bs, seq_s, _ = x_sample.shape
    assert seq_s == SUBLANES and BLK % seq_s == 0 and (bs * seq_s) % BLK == 0
    assert cache_swa_k.shape[2] == WINDOW
    w = _prepare_weights(norm_w, w_in, conv_w, conv_b, dt_bias, a_log, d_skip, ssd_norm_w, gla_w_gk, gla_b_gk,
                         gla_norm_w, q_norm_w, k_norm_w, w_out, w_pe, w_pg)
    rel_flat = rel_bias.reshape(-1)
    dist_p = WINDOW + np.arange(BLK)[:, None] - np.arange(2 * BLK)[None, :]
    bucket_p = jnp.asarray(_bucket_table(dist_p))
    t_of_row = np.tile(np.arange(seq_s), SWA_HEADS)[:, None]
    bucket_c = jnp.asarray(_bucket_table(WINDOW + t_of_row - np.arange(WINDOW)[None, :]))
    dist_n = np.where(np.arange(LANES)[None, :] < seq_s, t_of_row - np.arange(LANES)[None, :], -1)
    bucket_n = jnp.asarray(_bucket_table(dist_n))

    ssm_in = jnp.transpose(state_ssm, (0, 2, 3, 4, 1)).reshape(depth, SSD_HEADS, SSD_HEAD_DIM * SSD_STATE, bs)
    gla_in = jnp.transpose(state_gla, (0, 2, 3, 4, 1)).reshape(depth, GLA_HEADS, GLA_DK * GLA_DV, bs)
    kv_in = lambda a: jnp.transpose(a, (0, 1, 3, 4, 2)).reshape(depth, bs, SWA_KV_HEADS * SWA_HEAD_DIM, WINDOW)
    kc_in, vc_in = kv_in(cache_swa_k), kv_in(cache_swa_v)
    conv_in = jnp.transpose(state_conv, (0, 2, 1, 3))

    hp = x_prompt.reshape(bp * seq_p, D_MODEL)
    p_prompt_rows = p_prompt.reshape(depth, bp * seq_p, PLE_DIM)
    hs = jnp.transpose(x_sample, (1, 0, 2)).reshape(seq_s * bs, D_MODEL)
    p_sample_rows = jnp.transpose(p_sample, (0, 2, 1, 3)).reshape(depth, seq_s * bs, PLE_DIM)
    states_p = ()
    states_s = ()
    for i in range(depth):
        wops, wspecs = _layer_weights(i, w)
        hp, *states_p = _prompt_layer(i, depth, bp, hp, p_prompt_rows, tuple(states_p), bucket_p, rel_flat,
                                      attn_sinks[i], wops, wspecs)
        hs, states_s = _sample_layer_native(i, depth, seq_s, hs, p_sample_rows, conv_in, ssm_in, gla_in, kc_in,
                                            vc_in, states_s, (bucket_c, bucket_n), rel_flat, attn_sinks[i], wops)
    ssm_p, conv_p, gla_p, kt_p, vt_p = states_p
    conv_s, ssm_s, gla_s, kt_s, vt_s = states_s
    unpack_kv = lambda a: jnp.transpose(
        a.reshape(a.shape[:2] + (SWA_KV_HEADS, SWA_HEAD_DIM, WINDOW)), (0, 1, 4, 2, 3))
    outs_p = (ssm_p.reshape(depth, bp, SSD_HEADS, SSD_HEAD_DIM, SSD_STATE), conv_p,
              gla_p.reshape(depth, bp, GLA_HEADS, GLA_DK, GLA_DV), unpack_kv(kt_p), unpack_kv(vt_p))
    seq_last = lambda a, dims: jnp.transpose(a.reshape(a.shape[:2] + dims + (bs,)), (0, 4, 1, 2, 3))
    outs_s = (seq_last(ssm_s, (SSD_HEAD_DIM, SSD_STATE)), jnp.transpose(conv_s, (0, 2, 1, 3)),
              seq_last(gla_s, (GLA_DK, GLA_DV)), unpack_kv(kt_s), unpack_kv(vt_s))
    y_sample = jnp.transpose(hs.reshape(seq_s, bs, D_MODEL), (1, 0, 2))
    return (hp.reshape(bp, seq_p, D_MODEL), y_sample) + outs_p + outs_s
```

```python
import functools
import math

import numpy as np
import jax
import jax.numpy as jnp
from jax import lax
from jax.experimental import pallas as pl
from jax.experimental.pallas import tpu as pltpu

D_MODEL = 1024
DEPTH = 2
SSD_HEADS = 8
SSD_HEAD_DIM = 64
SSD_WIDTH = SSD_HEADS * SSD_HEAD_DIM
SSD_GROUPS = 2
SSD_STATE = 64
SSD_CONV = 4
SSD_CONV_DIM = SSD_WIDTH + 2 * SSD_GROUPS * SSD_STATE
SSD_CHUNK = 128
GLA_HEADS = 4
GLA_DK = 32
GLA_DV = 64
GLA_WIDTH = GLA_HEADS * GLA_DV
GLA_RANK = 16
GLA_GATE_NORM = 16.0
GLA_CHUNK = 64
SWA_HEADS = 4
SWA_KV_HEADS = 2
SWA_HEAD_DIM = 64
SWA_WIDTH = SWA_HEADS * SWA_HEAD_DIM
WINDOW = 128
REL_BUCKETS = 32
REL_MAX_DIST = 128
PLE_DIM = 256
EPS = 1e-6

LANES = 128
SUBLANES = 8
HALF = LANES // 2
BLK = 128
VMEM_LIMIT_BYTES = 56 * 1024 * 1024

XBC_W = SSD_CONV_DIM
P_Z = 0
P_GQ = P_Z + SSD_WIDTH
P_GK = P_GQ + LANES
P_GV = P_GK + LANES
P_GG = P_GV + GLA_WIDTH
P_SQ = P_GG + GLA_WIDTH
P_SK = P_SQ + SWA_WIDTH
P_SV = P_SK + LANES
P_SG = P_SV + LANES
P_DTLR = P_SG + SWA_WIDTH
PROJ_W = P_DTLR + LANES
LR_LANE0 = SSD_HEADS

F32 = jnp.float32
BF16 = jnp.bfloat16
NEG_INF = float("-inf")
LOG2E = math.log2(math.e)
N_PROMPT_INPUTS = 22
PROMPT_CHUNK_ROWS = 2 * BLK
TICKS_PER_ITEM = 2
GATED_AFTER_PROJ_ITEMS = 9
NT_DIMS = (((1,), (1,)), ((), ()))


def _iota(shape, dim):
    return lax.broadcasted_iota(jnp.int32, shape, dim)


def _div(x, d):
    return x >> (d.bit_length() - 1)


def _mod(x, d):
    return x & (d - 1)


def _softplus(x):
    e = jnp.exp(-jnp.abs(x))
    u = 1.0 + e
    d = u - 1.0
    log1p_e = jnp.where(d == 0.0, e, jnp.log(u) * (e / jnp.where(d == 0.0, 1.0, d)))
    return jnp.maximum(x, 0.0) + log1p_e


def _log_sigmoid(x):
    return jnp.minimum(x, 0.0) - jnp.log(1.0 + jnp.exp(-jnp.abs(x)))


def _silu(x):
    return x * jax.nn.sigmoid(x)


def _dot(a, b):
    return jnp.dot(a.astype(BF16), b.astype(BF16), preferred_element_type=F32)


def _dot_nt(a, b):
    return lax.dot_general(a.astype(BF16), b.astype(BF16), NT_DIMS, preferred_element_type=F32)


def _dot_exact(sel, x):
    x1 = x.astype(BF16)
    r1 = x - x1.astype(F32)
    x2 = r1.astype(BF16)
    x3 = (r1 - x2.astype(F32)).astype(BF16)
    dot = functools.partial(jnp.dot, sel, preferred_element_type=F32)
    return dot(x1) + dot(x2) + dot(x3)


def _expand_heads(x, n_heads):
    rows = x.shape[0]
    lo = _iota((rows, LANES), 1) < HALF
    tiles = []
    for j in range(n_heads // 2):
        a = jnp.broadcast_to(x[:, 2 * j:2 * j + 1], (rows, LANES))
        b = jnp.broadcast_to(x[:, 2 * j + 1:2 * j + 2], (rows, LANES))
        tiles.append(jnp.where(lo, a, b))
    return jnp.concatenate(tiles, axis=1)


def _head_rms_scale(x):
    rows, width = x.shape
    lo = _iota((rows, LANES), 1) < HALF
    outs = []
    for j in range(width // LANES):
        t = x[:, j * LANES:(j + 1) * LANES]
        sq = t * t
        s_lo = jnp.sum(jnp.where(lo, sq, 0.0), axis=-1, keepdims=True)
        s_hi = jnp.sum(jnp.where(lo, 0.0, sq), axis=-1, keepdims=True)
        outs.append(lax.rsqrt(jnp.where(lo, s_lo, s_hi) * (1.0 / HALF) + EPS))
    return outs[0] if len(outs) == 1 else jnp.concatenate(outs, axis=1)


def _group_rmsnorm(y, w):
    gw = SSD_WIDTH // SSD_GROUPS
    outs = []
    for g in range(SSD_GROUPS):
        t = y[:, g * gw:(g + 1) * gw]
        ms = jnp.sum(t * t, axis=-1, keepdims=True) * (1.0 / gw)
        outs.append(t * lax.rsqrt(ms + EPS))
    return jnp.concatenate(outs, axis=1) * w


def _rel_bucket_np(dist):
    n = np.maximum(dist, 0)
    exact = REL_BUCKETS // 2
    nf = np.maximum(n, 1).astype(np.float64)
    large = exact + (np.log(nf / exact) / math.log(REL_MAX_DIST / exact) * (REL_BUCKETS - exact)).astype(np.int32)
    large = np.minimum(large, REL_BUCKETS - 1)
    return np.where(n < exact, n, large).astype(np.int32)


def _bucket_table(dist):
    return np.where((dist >= 0) & (dist < WINDOW), _rel_bucket_np(dist), -1).astype(np.int32)


def _no_tick():
    pass


def _ssd_intra(xbc_c, dtv, acum, pair_mask, tick=_no_tick):
    xs = xbc_c[:, :SSD_WIDTH]
    bm = xbc_c[:, SSD_WIDTH:SSD_WIDTH + LANES]
    cm = xbc_c[:, SSD_WIDTH + LANES:]
    lane = _iota((BLK, LANES), 1)
    lo = lane < HALF
    acum_t = acum.T
    eacum = jnp.exp2(acum)
    tail = jnp.exp2(acum[BLK - 1:BLK, :] - acum)
    dtv_e = _expand_heads(dtv, SSD_HEADS)
    eacum_e = _expand_heads(eacum, SSD_HEADS)
    tail_e = _expand_heads(tail, SSD_HEADS)
    tick()
    xdt = xs * dtv_e
    xw = xdt * tail_e
    cb = [_dot_nt(jnp.where(lo, cm, 0.0), bm), _dot_nt(jnp.where(lo, 0.0, cm), bm)]
    y_pairs = []
    for j in range(SSD_HEADS // 2):
        tick()
        g = (2 * j) // (SSD_HEADS // SSD_GROUPS)
        ms = []
        for k in range(2):
            h = 2 * j + k
            seg = acum[:, h:h + 1] - acum_t[h:h + 1, :]
            dec = jnp.where(pair_mask, jnp.exp2(seg), 0.0)
            ms.append((cb[g] * dec).astype(BF16))
        xp = xdt[:, j * LANES:(j + 1) * LANES]
        rhs = jnp.concatenate([jnp.where(lo, xp, 0.0), jnp.where(lo, 0.0, xp)], axis=0)
        y_pairs.append(_dot(jnp.concatenate(ms, axis=1), rhs))
    y_intra = jnp.concatenate(y_pairs, axis=1)
    return y_intra, xs, bm, cm, xw, eacum, eacum_e


def _gla_intra(gq, gk, gv, bcs, att_mask, tick=_no_tick):
    eb = jnp.exp2(bcs)
    qe = gq * (GLA_DK ** -0.5) * eb
    ke = gk * jnp.exp2(-bcs)
    btot = jnp.concatenate(
        [jnp.broadcast_to(bcs[(c2 + 1) * GLA_CHUNK - 1:(c2 + 1) * GLA_CHUNK, :], (GLA_CHUNK, LANES))
         for c2 in range(BLK // GLA_CHUNK)], axis=0)
    kd = gk * jnp.exp2(btot - bcs)
    lane_k = _iota((GLA_CHUNK, LANES), 1)
    lane_v = _iota((GLA_CHUNK, GLA_WIDTH), 1)
    outs = []
    for c2 in range(BLK // GLA_CHUNK):
        tick()
        rs = slice(c2 * GLA_CHUNK, (c2 + 1) * GLA_CHUNK)
        ke_c = ke[rs]
        v_c = gv[rs]
        kbd = jnp.concatenate(
            [jnp.where(_div(lane_k, GLA_DK) == h, ke_c, 0.0) for h in range(GLA_HEADS)], axis=0)
        att = _dot_nt(qe[rs], kbd)
        att = jnp.where(att_mask, att, 0.0)
        vbd = jnp.concatenate(
            [jnp.where(_div(lane_v, GLA_DV) == h, v_c, 0.0) for h in range(GLA_HEADS)], axis=0)
        outs.append(_dot(att, vbd))
    return jnp.concatenate(outs, axis=0), qe, kd, jnp.exp2(btot)


def _build_bias(bucket, rel_ref):
    accs = [jnp.full(bucket.shape, NEG_INF, F32) for _ in range(SWA_HEADS)]
    for b in range(REL_BUCKETS):
        hit = bucket == b
        for h in range(SWA_HEADS):
            accs[h] = jnp.where(hit, rel_ref[b * SWA_HEADS + h], accs[h])
    return accs


def _epilogue(h, mix, p, wout_ref, wpg_ref, wpe_ref):
    h1 = h + jnp.dot(mix, wout_ref[...], preferred_element_type=F32)
    gate = jax.nn.sigmoid(jnp.dot(h1.astype(BF16), wpg_ref[...], preferred_element_type=F32))
    pe = jnp.dot(p.astype(BF16), wpe_ref[...], preferred_element_type=F32)
    return h1 + gate * pe


def _prompt_kernel(chunks_per_seq, n_aliased, *refs):
    (ha_ref, hc_ref, p_ref, bucket_ref, rel_ref, sink_ref, nw_ref, win_ref, cw_ref, cb_ref, dtb_ref,
     alog_ref, dsk_ref, snw_ref, wgk_ref, bgk_ref, gnw_ref, qnw_ref, knw_ref, wout_ref,
     wpe_ref, wpg_ref) = refs[:N_PROMPT_INPUTS]
    (y_ref, ssm_ref, conv_ref, gla_ref, ko_ref, vo_ref,
     proj_e, proj_o, xbc_e, xbc_o, mix_e, mix_o, u_s, h1_s, h1b_s, hist_s, st_s, s2_s, kext_s, vext_s,
     bias_s, hist_snap, st_snap, s2_snap, k_snap, v_snap) = refs[N_PROMPT_INPUTS + n_aliased:]
    k_idx = pl.program_id(0)

    @pl.when(k_idx == 0)
    def _():
        accs = [a * LOG2E for a in _build_bias(bucket_ref[...], rel_ref)]
        own_block = _iota((BLK, 2 * BLK), 1) >= BLK
        for hh in range(SWA_HEADS):
            bias_s[hh] = accs[hh]
            bias_s[SWA_HEADS + hh] = jnp.where(own_block, accs[hh], NEG_INF)
        for ref in (proj_o, xbc_o, mix_e, mix_o, hist_s, st_s, s2_s, kext_s, vext_s):
            ref[...] = jnp.zeros(ref.shape, ref.dtype)

    row = _iota((BLK, BLK), 0)
    col = _iota((BLK, BLK), 1)
    causal = row >= col
    tri = jnp.where(causal, 1.0, 0.0).astype(BF16)
    lo = col < HALF
    lane_row = _iota((1, LANES), 1)
    a_row = jnp.where(lane_row < SSD_HEADS, -jnp.exp(alog_ref[...]) * LOG2E, 0.0)
    bd_mask = _div(_iota((LANES, GLA_WIDTH), 0), GLA_DK) == _div(_iota((LANES, GLA_WIDTH), 1), GLA_DV)
    att_t = _iota((GLA_CHUNK, GLA_WIDTH), 0)
    att_s = _mod(_iota((GLA_CHUNK, GLA_WIDTH), 1), GLA_CHUNK)
    att_mask = att_s <= att_t
    lo2 = _iota((2 * BLK, LANES), 1) < HALF

    group_w = SSD_WIDTH // SSD_GROUPS

    def block(blk, proj_s, xbc_s, mix_s, starts_sequence, tick):
        rows = slice(blk * BLK, (blk + 1) * BLK)
        cw = cw_ref[...]
        tick()
        if blk == 0:
            xwin = jnp.concatenate([hist_s[...], xbc_s[0:BLK, :]], axis=0)
        else:
            xwin = xbc_s[blk * BLK - SUBLANES:(blk + 1) * BLK, :]
        acc = xwin[SUBLANES - 3:SUBLANES - 3 + BLK, :] * cw[0:1, :]
        for k in range(1, SSD_CONV):
            acc = acc + xwin[SUBLANES - 3 + k:SUBLANES - 3 + k + BLK, :] * cw[k:k + 1, :]
        xbc_c = _silu(acc + cb_ref[...])
        tick()
        dtlr = proj_s[rows, P_DTLR:P_DTLR + LANES]
        dtv_t = _softplus((dtlr + dtb_ref[...]).T[0:SSD_HEADS, :])
        dtv = jnp.concatenate([dtv_t, jnp.zeros((LANES - SSD_HEADS, BLK), F32)], axis=0).T
        glog = _log_sigmoid(_dot(dtlr, wgk_ref[...]) + bgk_ref[...]) * (LOG2E / GLA_GATE_NORM)
        sums = _dot_exact(tri, jnp.concatenate([dtv * a_row, glog], axis=1))
        acum = sums[:, :LANES]
        gsum = sums[:, LANES:]
        bcs = gsum
        for c2 in range(1, BLK // GLA_CHUNK):
            before = gsum[c2 * GLA_CHUNK - 1:c2 * GLA_CHUNK, :]
            bcs = jnp.where(_div(row, GLA_CHUNK) == c2, gsum - before, bcs)
        tick()
        y_intra, xs, bm, cm, xw, eacum, eacum_e = _ssd_intra(xbc_c, dtv, acum, causal, tick)
        tick()
        st = st_s[...]
        y = y_intra + _dot(cm, st) * eacum_e + dsk_ref[...] * xs
        bm_t = bm.T
        for g in range(SSD_GROUPS):
            gr = slice(g * SSD_STATE, (g + 1) * SSD_STATE)
            gc = slice(g * group_w, (g + 1) * group_w)
            st_s[gr, gc] = st[gr, gc] * eacum_e[BLK - 1:BLK, gc] + _dot(bm_t[gr, :], xw[:, gc])
        tick()
        y = y * _silu(proj_s[rows, P_Z:P_Z + SSD_WIDTH])
        mix_s[rows, 0:SSD_WIDTH] = _group_rmsnorm(y, snw_ref[...]).astype(BF16)
        tick()
        gk = proj_s[rows, P_GK:P_GK + LANES]
        gv = proj_s[rows, P_GV:P_GV + GLA_WIDTH]
        o_intra, qe, kd, ebt = _gla_intra(proj_s[rows, P_GQ:P_GQ + LANES], gk, gv, bcs, att_mask, tick)
        kd_t = kd.T
        ebt_t = ebt.T
        s2 = s2_s[...]
        o_parts = []
        for c2 in range(BLK // GLA_CHUNK):
            tick()
            rs = slice(c2 * GLA_CHUNK, (c2 + 1) * GLA_CHUNK)
            o_parts.append(o_intra[rs] + _dot(qe[rs], s2))
            u2 = _dot(jnp.where(_div(col, GLA_CHUNK) == c2, kd_t, 0.0), gv)
            last = (c2 + 1) * GLA_CHUNK - 1
            s2 = s2 * ebt_t[:, last:last + 1] + jnp.where(bd_mask, u2, 0.0)
        s2_s[...] = s2
        o = jnp.concatenate(o_parts, axis=0)
        y_gla = o * _head_rms_scale(o) * gnw_ref[...] * _silu(proj_s[rows, P_GG:P_GG + GLA_WIDTH])
        mix_s[rows, SSD_WIDTH:SSD_WIDTH + GLA_WIDTH] = y_gla.astype(BF16)
        tick()
        sq = proj_s[rows, P_SQ:P_SQ + SWA_WIDTH]
        qn = sq * _head_rms_scale(sq) * qnw_ref[...] * (SWA_HEAD_DIM ** -0.5 * LOG2E)
        sk = proj_s[rows, P_SK:P_SK + LANES]
        kn = sk * _head_rms_scale(sk) * knw_ref[...]
        vn = proj_s[rows, P_SV:P_SV + LANES]
        kext_s[BLK:2 * BLK, :] = kn
        vext_s[BLK:2 * BLK, :] = vn
        kext = kext_s[...]
        vext = vext_s[...]
        qa = qn[:, :LANES]
        qb = qn[:, LANES:]
        qs = jnp.concatenate([jnp.where(lo, qa, 0.0), jnp.where(lo, qb, 0.0),
                              jnp.where(lo, 0.0, qa), jnp.where(lo, 0.0, qb)], axis=0)
        logits = _dot_nt(qs, kext)
        tick()
        if blk == 0 and starts_sequence is not False:
            bias_row0 = jnp.where(starts_sequence, SWA_HEADS, 0)
        else:
            bias_row0 = 0
        es = []
        invs = []
        for hh in range(SWA_HEADS):
            tick()
            sink = sink_ref[hh] * LOG2E
            l = logits[hh * BLK:(hh + 1) * BLK] + bias_s[bias_row0 + hh]
            m = jnp.maximum(jnp.max(l, axis=-1, keepdims=True), sink)
            e = jnp.exp2(l - m)
            den = jnp.sum(e, axis=-1, keepdims=True) + jnp.exp2(sink - m)
            es.append(e.astype(BF16))
            invs.append(1.0 / den)
        v_stack = jnp.concatenate([jnp.where(lo2, vext, 0.0), jnp.where(lo2, 0.0, vext)], axis=0)
        tile_a = _dot(jnp.concatenate([es[0], es[2]], axis=1), v_stack) * jnp.where(lo, invs[0], invs[2])
        tile_b = _dot(jnp.concatenate([es[1], es[3]], axis=1), v_stack) * jnp.where(lo, invs[1], invs[3])
        oa = jnp.concatenate([tile_a, tile_b], axis=1)
        y_swa = oa * _silu(proj_s[rows, P_SG:P_SG + SWA_WIDTH])
        mix_s[rows, SSD_WIDTH + GLA_WIDTH:] = y_swa.astype(BF16)
        kext_s[0:BLK, :] = kn
        vext_s[0:BLK, :] = vn

    chunk = proj_e.shape[0]

    def project_items(rows, proj_s, xbc_s):
        def norm():
            h = ha_ref[rows, :]
            ms = jnp.mean(h * h, axis=-1, keepdims=True)
            u_s[...] = (h * lax.rsqrt(ms + EPS) * nw_ref[...]).astype(BF16)

        def cols(dst, lo_c, hi_c, w_off):
            def item():
                dst[:, lo_c:hi_c] = jnp.dot(u_s[...], win_ref[:, w_off + lo_c:w_off + hi_c],
                                            preferred_element_type=F32)
            return item

        step = 2 * LANES
        items = [norm]
        items += [cols(xbc_s, c, min(c + step, XBC_W), 0) for c in range(0, XBC_W, step)]
        items += [cols(proj_s, c, min(c + step, PROJ_W), XBC_W) for c in range(0, PROJ_W, step)]
        return items

    def epilogue_items(rows, mix_s):
        half_w = 2 * LANES

        def residual(c):
            def item():
                h1 = hc_ref[rows, c:c + half_w] + jnp.dot(
                    mix_s[...], wout_ref[:, c:c + half_w], preferred_element_type=F32)
                h1_s[:, c:c + half_w] = h1
                h1b_s[:, c:c + half_w] = h1.astype(BF16)
            return item

        def gated(c):
            def item():
                gate = jax.nn.sigmoid(jnp.dot(h1b_s[...], wpg_ref[:, c:c + half_w], preferred_element_type=F32))
                pe = jnp.dot(p_ref[0, rows, :].astype(BF16), wpe_ref[:, c:c + half_w],
                             preferred_element_type=F32)
                y_ref[rows, c:c + half_w] = h1_s[:, c:c + half_w] + gate * pe
            return item

        col0 = range(0, D_MODEL, half_w)
        return [residual(c) for c in col0], [gated(c) for c in col0]

    def mixer(proj_s, xbc_s, mix_s, starts_sequence, items):
        if starts_sequence is not False:
            keep = jnp.where(starts_sequence, 0.0, 1.0)
            for ref in (hist_s, st_s, s2_s):
                ref[...] = ref[...] * keep
            kext_s[0:BLK, :] = kext_s[0:BLK, :] * keep
            vext_s[0:BLK, :] = vext_s[0:BLK, :] * keep
        queue = list(items)
        calls = [0]

        def tick():
            calls[0] += 1
            if queue and calls[0] % TICKS_PER_ITEM == 0:
                queue.pop(0)()

        for blk in range(chunk // BLK):
            block(blk, proj_s, xbc_s, mix_s, starts_sequence, tick)
        while queue:
            queue.pop(0)()
        hist_s[...] = xbc_s[chunk - SUBLANES:chunk, :]

    def snapshot_states():
        st_snap[...] = st_s[...]
        s2_snap[...] = s2_s[...]
        hist_snap[...] = hist_s[...]
        k_snap[...] = kext_s[0:BLK, :]
        v_snap[...] = vext_s[0:BLK, :]

    def write_states():
        st = st_snap[...]
        stc = st[:SSD_STATE] + st[SSD_STATE:]
        ssm_ref[0, 0] = jnp.concatenate([stc, stc], axis=0).T[:, :SSD_STATE]
        conv_ref[0, 0] = hist_snap[SUBLANES - (SSD_CONV - 1):SUBLANES, :]
        s2 = s2_snap[...]
        w = s2[:, :LANES] + s2[:, LANES:]
        gla_ref[0, 0] = w[:, :GLA_DV] + w[:, GLA_DV:]
        ko_ref[0, 0] = k_snap[...].T
        vo_ref[0, 0] = v_snap[...].T

    even = slice(0, chunk)
    odd = slice(chunk, 2 * chunk)
    def stage_items(rows, proj_w, xbc_w, mix_r):
        residuals, gateds = epilogue_items(rows, mix_r)
        proj = project_items(rows, proj_w, xbc_w)
        return residuals + proj[:GATED_AFTER_PROJ_ITEMS] + gateds + proj[GATED_AFTER_PROJ_ITEMS:]

    mixer(proj_o, xbc_o, mix_o, False, stage_items(even, proj_e, xbc_e, mix_e))
    snapshot_states()
    mixer(proj_e, xbc_e, mix_e, _mod(2 * k_idx, chunks_per_seq) == 0, stage_items(odd, proj_o, xbc_o, mix_o))

    @pl.when((k_idx >= 1) & (_mod(2 * k_idx - 1, chunks_per_seq) == chunks_per_seq - 1))
    def _():
        write_states()


N_FRONT_INPUTS = 12
SAMPLE_POSITIONS_PER_STEP = 2


def _sample_front_kernel(seq, n_aliased, *refs):
    (h_ref, cst_ref, nw_ref, win_ref, cw_ref, cb_ref, dtb_ref, alog_ref, wgk_ref, bgk_ref,
     qnw_ref, knw_ref) = refs[:N_FRONT_INPUTS]
    (xs_ref, gates_ref, xt_ref, bt_ref, ct_ref, at_ref, qt_ref, kt_ref, egt_ref, vt_ref,
     qn_ref, kn_ref, vn_ref, conv_ref, u_s, xbc_s, proj_s) = refs[N_FRONT_INPUTS + n_aliased:]
    n_seq = BLK
    per_step = h_ref.shape[0] // n_seq
    step = pl.program_id(0)
    ht = h_ref[...]
    ms = jnp.mean(ht * ht, axis=-1, keepdims=True)
    u_s[...] = (ht * lax.rsqrt(ms + EPS) * nw_ref[...]).astype(BF16)
    xbc_s[pl.ds(pl.multiple_of(step * per_step * n_seq, n_seq), per_step * n_seq), :] = jnp.dot(
        u_s[...], win_ref[:, :XBC_W], preferred_element_type=F32)
    proj_s[...] = jnp.dot(u_s[...], win_ref[:, XBC_W:], preferred_element_type=F32)
    cw = cw_ref[...]
    a_row = jnp.where(_iota((1, LANES), 1) < SSD_HEADS, -jnp.exp(alog_ref[...]), 0.0)
    for i in range(per_step):
        t = step * per_step + i
        rows = slice(i * n_seq, (i + 1) * n_seq)

        def raw_xbc(back):
            cur = xbc_s[pl.ds(pl.multiple_of(jnp.maximum(t - back, 0) * n_seq, n_seq), n_seq), :]
            if back == 0:
                return cur
            old = cst_ref[0, jnp.clip(SSD_CONV - 1 + t - back, 0, SSD_CONV - 2)]
            return jnp.where(t >= back, cur, old)

        acc = raw_xbc(SSD_CONV - 1) * cw[0:1, :]
        for k in range(1, SSD_CONV):
            acc = acc + raw_xbc(SSD_CONV - 1 - k) * cw[k:k + 1, :]
        xbc_c = _silu(acc + cb_ref[...])
        xs = xbc_c[:, :SSD_WIDTH]
        dtlr = proj_s[rows, P_DTLR:P_DTLR + LANES]
        dtv = _softplus(dtlr + dtb_ref[...])
        xs_ref[rows, :] = xs
        xt_ref[i] = (xs * _expand_heads(dtv, SSD_HEADS)).T
        bt_ref[i] = xbc_c[:, SSD_WIDTH:SSD_WIDTH + LANES].T
        ct_ref[i] = xbc_c[:, SSD_WIDTH + LANES:].T
        at_ref[i] = jnp.exp(dtv * a_row).T[:SSD_HEADS, :]
        glog = _log_sigmoid(_dot(dtlr, wgk_ref[...]) + bgk_ref[...]) * (1.0 / GLA_GATE_NORM)
        qt_ref[i] = (proj_s[rows, P_GQ:P_GQ + LANES] * (GLA_DK ** -0.5)).T
        kt_ref[i] = proj_s[rows, P_GK:P_GK + LANES].T
        egt_ref[i] = jnp.exp(glog).T
        vt_ref[i] = proj_s[rows, P_GV:P_GV + GLA_WIDTH].T
        sq = proj_s[rows, P_SQ:P_SQ + SWA_WIDTH]
        qn_ref[rows, :] = sq * _head_rms_scale(sq) * qnw_ref[...] * (SWA_HEAD_DIM ** -0.5)
        sk = proj_s[rows, P_SK:P_SK + LANES]
        kn_ref[rows, :] = sk * _head_rms_scale(sk) * knw_ref[...]
        vn_ref[rows, :] = proj_s[rows, P_SV:P_SV + LANES]
        gates_ref[rows, :] = jnp.concatenate(
            [_silu(proj_s[rows, P_Z:P_Z + SSD_WIDTH]), _silu(proj_s[rows, P_GG:P_GG + GLA_WIDTH]),
             _silu(proj_s[rows, P_SG:P_SG + SWA_WIDTH])], axis=1)
        first_kept = seq - (SSD_CONV - 1)

        @pl.when(t >= first_kept)
        def _():
            conv_ref[0, jnp.maximum(t - first_kept, 0)] = raw_xbc(0)


N_STATE_INPUTS = 19
SEQ_PER_STEP = 16


def _sample_state_kernel(seq, n_aliased, *refs):
    (xt_ref, bt_ref, ct_ref, at_ref, qt_ref, kt_ref, egt_ref, vt_ref, qn_ref, kn_ref, vn_ref,
     ssm_ref, gla_ref, kc_ref, vc_ref, bucket_c_ref, bucket_n_ref, rel_ref, sink_ref) = refs[:N_STATE_INPUTS]
    (ssm_o, gla_o, ko_ref, vo_ref, yt_ref, ot_ref, oswa_ref,
     qa_s, qb_s, krow_s, vrow_s, oa_s, ob_s, biasc_s, biasn_s) = refs[N_STATE_INPUTS + n_aliased:]
    j = pl.program_id(0)
    n_seq = LANES
    head_of_row = _div(_iota((SWA_HEADS * seq, LANES), 0), seq)

    def by_head(values):
        out = values[SWA_HEADS - 1]
        for hh in range(SWA_HEADS - 2, -1, -1):
            out = jnp.where(head_of_row == hh, values[hh], out)
        return out

    @pl.when(j == 0)
    def _():
        biasc_s[...] = by_head(_build_bias(bucket_c_ref[...], rel_ref))
        biasn_s[...] = by_head(_build_bias(bucket_n_ref[...], rel_ref))

    sub = _iota((SUBLANES, LANES), 0)
    a_rows = [jnp.sum(jnp.where(sub == j, at_ref[t], 0.0), axis=0, keepdims=True) for t in range(seq)]

    def ssd_body(p8, carry):
        r8 = pl.multiple_of(p8 * SUBLANES, SUBLANES)
        x_tiles = [xt_ref[t, pl.ds(r8, SUBLANES), :] for t in range(seq)]
        y_rows = [[] for _ in range(seq)]
        for pp in range(SUBLANES):
            r64 = pl.multiple_of((p8 * SUBLANES + pp) * SSD_STATE, SSD_STATE)
            slab = ssm_ref[0, 0, pl.ds(r64, SSD_STATE), :]
            for t in range(seq):
                slab = slab * a_rows[t] + x_tiles[t][pp:pp + 1, :] * bt_ref[t]
                y_rows[t].append(jnp.sum(ct_ref[t] * slab, axis=0, keepdims=True))
            ssm_o[0, 0, pl.ds(r64, SSD_STATE), :] = slab
        for t in range(seq):
            yt_ref[t, pl.ds(r8, SUBLANES), :] = jnp.concatenate(y_rows[t], axis=0)
        return carry

    lax.fori_loop(0, SSD_HEAD_DIM // SUBLANES, ssd_body, 0)

    @pl.when(j < GLA_HEADS)
    def _():
        for t in range(seq):
            ot_ref[t] = jnp.zeros((GLA_DV, LANES), F32)

        def gla_body(d8, carry):
            r8 = pl.multiple_of(d8 * SUBLANES, SUBLANES)
            q_tiles = [qt_ref[t, pl.ds(r8, SUBLANES), :] for t in range(seq)]
            k_tiles = [kt_ref[t, pl.ds(r8, SUBLANES), :] for t in range(seq)]
            g_tiles = [egt_ref[t, pl.ds(r8, SUBLANES), :] for t in range(seq)]
            for dd in range(SUBLANES):
                r64 = pl.multiple_of((d8 * SUBLANES + dd) * GLA_DV, GLA_DV)
                slab = gla_ref[0, 0, pl.ds(r64, GLA_DV), :]
                for t in range(seq):
                    slab = slab * g_tiles[t][dd:dd + 1, :] + k_tiles[t][dd:dd + 1, :] * vt_ref[t]
                    ot_ref[t] = ot_ref[t] + q_tiles[t][dd:dd + 1, :] * slab
                gla_o[0, 0, pl.ds(r64, GLA_DV), :] = slab
            return carry

        lax.fori_loop(0, GLA_DK // SUBLANES, gla_body, 0)

    base = pl.multiple_of(j * SEQ_PER_STEP, SEQ_PER_STEP)
    for t in range(seq):
        src = pl.ds(t * n_seq + base, SEQ_PER_STEP)
        dst = pl.ds(t, SEQ_PER_STEP, stride=seq)
        qa_s[dst, :] = qn_ref[src, 0:LANES]
        qb_s[dst, :] = qn_ref[src, LANES:2 * LANES]
        krow_s[dst, :] = kn_ref[src, :]
        vrow_s[dst, :] = vn_ref[src, :]
    kn_t = krow_s[...].T
    vn_t = vrow_s[...].T
    keep_old = _iota((LANES, WINDOW), 1) < WINDOW - seq
    lo8 = _iota((seq, LANES), 1) < HALF
    sink_col = by_head([jnp.full((SWA_HEADS * seq, LANES), sink_ref[hh], F32) for hh in range(SWA_HEADS)])[:, 0:1]

    def swa_stages(bl):
        r8 = pl.multiple_of(bl * seq, seq)
        v = {}

        def logits():
            qa = qa_s[pl.ds(r8, seq), :]
            qb = qb_s[pl.ds(r8, seq), :]
            qs = jnp.concatenate([jnp.where(lo8, qa, 0.0), jnp.where(lo8, qb, 0.0),
                                  jnp.where(lo8, 0.0, qa), jnp.where(lo8, 0.0, qb)], axis=0)
            v["lc"] = _dot(qs, kc_ref[0, bl]) + biasc_s[...]
            v["ln"] = _dot_nt(qs, krow_s[pl.ds(r8, seq), :]) + biasn_s[:, 0:seq]

        def softmax():
            lc, ln = v["lc"], v["ln"]
            m = jnp.maximum(jnp.maximum(jnp.max(lc, axis=-1, keepdims=True), jnp.max(ln, axis=-1, keepdims=True)),
                            sink_col)
            v["ec"] = jnp.exp(lc - m)
            v["en"] = jnp.exp(ln - m)
            v["inv"] = 1.0 / (jnp.sum(v["ec"], axis=-1, keepdims=True) + jnp.sum(v["en"], axis=-1, keepdims=True)
                              + jnp.exp(sink_col - m))

        def values():
            o = (_dot_nt(v["ec"], vc_ref[0, bl]) + _dot(v["en"], vrow_s[pl.ds(r8, seq), :])) * v["inv"]
            oa_s[pl.ds(r8, seq), :] = jnp.where(lo8, o[0:seq], o[2 * seq:3 * seq])
            ob_s[pl.ds(r8, seq), :] = jnp.where(lo8, o[seq:2 * seq], o[3 * seq:4 * seq])

        def window():
            ko_ref[0, bl] = jnp.where(keep_old, pltpu.roll(kc_ref[0, bl], WINDOW - seq, axis=1),
                                      pltpu.roll(kn_t, WINDOW - seq - r8, axis=1))
            vo_ref[0, bl] = jnp.where(keep_old, pltpu.roll(vc_ref[0, bl], WINDOW - seq, axis=1),
                                      pltpu.roll(vn_t, WINDOW - seq - r8, axis=1))

        return [logits, softmax, values, window]

    all_stages = [swa_stages(bl) for bl in range(SEQ_PER_STEP)]
    for k in range(len(all_stages[0])):
        for stages in all_stages:
            stages[k]()
    for t in range(seq):
        src = pl.ds(t, SEQ_PER_STEP, stride=seq)
        oswa_ref[t] = jnp.concatenate([oa_s[src, :], ob_s[src, :]], axis=1)


def _sample_back_kernel(yt_ref, ot_ref, oswa_ref, xs_ref, gates_ref, h_ref, p_ref, dsk_ref, snw_ref, gnw_ref,
                        wout_ref, wpe_ref, wpg_ref, y_ref, mix_s):
    n_seq = BLK
    for t in range(yt_ref.shape[0]):
        rows = slice(t * n_seq, (t + 1) * n_seq)
        y = (yt_ref[t].T + dsk_ref[...] * xs_ref[rows, :]) * gates_ref[rows, 0:SSD_WIDTH]
        mix_s[rows, 0:SSD_WIDTH] = _group_rmsnorm(y, snw_ref[...]).astype(BF16)
        o = ot_ref[t].T
        y_gla = o * _head_rms_scale(o) * gnw_ref[...] * gates_ref[rows, SSD_WIDTH:SSD_WIDTH + GLA_WIDTH]
        mix_s[rows, SSD_WIDTH:SSD_WIDTH + GLA_WIDTH] = y_gla.astype(BF16)
        mix_s[rows, SSD_WIDTH + GLA_WIDTH:] = (oswa_ref[t] * gates_ref[rows, SSD_WIDTH + GLA_WIDTH:]).astype(BF16)
    y_ref[...] = _epilogue(h_ref[...], mix_s[...], p_ref[0], wout_ref, wpg_ref, wpe_ref)


def _const_spec(shape):
    nd = len(shape)
    return pl.BlockSpec(shape, lambda *_: (0,) * nd)


def _smem_spec():
    return pl.BlockSpec(memory_space=pltpu.SMEM)


def _layer_spec(arr, layer):
    return pl.BlockSpec((None,) + arr.shape[1:], lambda *_: (layer, 0, 0), pipeline_mode=pl.Buffered(1))


def _layer_weights(layer, w):
    ops = [w[name] for name in ("norm_w", "w_in", "conv_w", "conv_b", "dt_bias", "a_log", "d_skip", "ssd_norm_w",
                                "gla_w_gk", "gla_b_gk", "gla_norm_w", "q_norm_w", "k_norm_w", "w_out", "w_pe",
                                "w_pg")]
    return ops, [_layer_spec(o, layer) for o in ops]


def _prompt_layer(layer, depth, bsz, h, p_all, prev_states, bucket, rel, sinks, wops, wspecs):
    rows_total, _ = h.shape
    seq_len = rows_total // bsz
    chunk = PROMPT_CHUNK_ROWS
    pair = 2 * chunk
    chunks_per_seq = seq_len // chunk
    assert seq_len % pair == 0 and chunks_per_seq & (chunks_per_seq - 1) == 0
    n_pairs = rows_total // pair
    kern = functools.partial(_prompt_kernel, chunks_per_seq, len(prev_states))
    proj_rows = pl.BlockSpec((pair, D_MODEL), lambda k: (jnp.minimum(k, n_pairs - 1), 0))
    out_rows = pl.BlockSpec((pair, D_MODEL), lambda k: (jnp.maximum(k - 1, 0), 0))
    p_spec = pl.BlockSpec((1, pair, PLE_DIM), lambda k: (layer, jnp.maximum(k - 1, 0), 0))
    per_seq = lambda s: pl.BlockSpec(
        (1, 1) + s, lambda k: (layer, jnp.maximum(2 * k - 1, 0) // chunks_per_seq) + (0,) * len(s))
    state_shapes = ((SSD_WIDTH, SSD_STATE), (SSD_CONV - 1, SSD_CONV_DIM), (GLA_HEADS * GLA_DK, GLA_DV),
                    (LANES, WINDOW), (LANES, WINDOW))
    out_shape = (jax.ShapeDtypeStruct((rows_total, D_MODEL), F32),) + tuple(
        jax.ShapeDtypeStruct((depth, bsz) + s, F32) for s in state_shapes)
    return pl.pallas_call(
        kern,
        grid=(n_pairs + 1,),
        in_specs=[proj_rows, out_rows, p_spec, _const_spec(bucket.shape), _smem_spec(), _smem_spec()]
        + wspecs + [pl.BlockSpec(memory_space=pl.ANY)] * len(prev_states),
        out_specs=(out_rows,) + tuple(per_seq(s) for s in state_shapes),
        out_shape=out_shape,
        input_output_aliases={N_PROMPT_INPUTS + k: 1 + k for k in range(len(prev_states))},
        scratch_shapes=[
            pltpu.VMEM((chunk, PROJ_W), F32), pltpu.VMEM((chunk, PROJ_W), F32),
            pltpu.VMEM((chunk, XBC_W), F32), pltpu.VMEM((chunk, XBC_W), F32),
            pltpu.VMEM((chunk, D_MODEL), BF16), pltpu.VMEM((chunk, D_MODEL), BF16),
            pltpu.VMEM((chunk, D_MODEL), BF16),
            pltpu.VMEM((chunk, D_MODEL), F32),
            pltpu.VMEM((chunk, D_MODEL), BF16),
            pltpu.VMEM((SUBLANES, XBC_W), F32),
            pltpu.VMEM((BLK, SSD_WIDTH), F32),
            pltpu.VMEM((LANES, GLA_WIDTH), F32),
            pltpu.VMEM((2 * BLK, LANES), F32),
            pltpu.VMEM((2 * BLK, LANES), F32),
            pltpu.VMEM((2 * SWA_HEADS, BLK, 2 * BLK), F32),
            pltpu.VMEM((SUBLANES, XBC_W), F32), pltpu.VMEM((BLK, SSD_WIDTH), F32),
            pltpu.VMEM((LANES, GLA_WIDTH), F32), pltpu.VMEM((BLK, LANES), F32),
            pltpu.VMEM((BLK, LANES), F32),
        ],
        compiler_params=pltpu.CompilerParams(
            dimension_semantics=("arbitrary",), vmem_limit_bytes=VMEM_LIMIT_BYTES),
        name="prompt_layer",
    )(h, h, p_all, bucket, rel, sinks, *wops, *prev_states)


def _whole(shape, layer=None):
    if layer is None:
        return pl.BlockSpec(shape, lambda *_: (0,) * len(shape), pipeline_mode=pl.Buffered(1))
    return pl.BlockSpec((1,) + shape[1:], lambda *_: (layer,) + (0,) * (len(shape) - 1),
                        pipeline_mode=pl.Buffered(1))


def _sample_layer_native(layer, depth, seq, h, p_all, conv_in, ssm_in, gla_in, kc_in, vc_in, prev_states,
                         buckets, rel, sinks, wops):
    (nw, win, cw, cb, dtb, alog, dsk, snw, wgk, bgk, gnw, qnw, knw, wout, wpe, wpg) = wops
    rows = h.shape[0]
    n_seq = rows // seq
    assert n_seq == LANES and n_seq % SEQ_PER_STEP == 0 and SSD_HEADS * SEQ_PER_STEP == n_seq
    prev_conv, prev_rest = (prev_states[:1], prev_states[1:]) if prev_states else ((), ())
    f32 = lambda *s: jax.ShapeDtypeStruct(s, F32)
    cparams = lambda sem: pltpu.CompilerParams(dimension_semantics=sem, vmem_limit_bytes=VMEM_LIMIT_BYTES)

    per_step = SAMPLE_POSITIONS_PER_STEP
    step_rows = per_step * n_seq
    assert seq % per_step == 0
    row_blk = lambda w: pl.BlockSpec((step_rows, w), lambda s: (s, 0))
    pos_blk = lambda n: pl.BlockSpec((per_step, n, n_seq), lambda s: (s, 0, 0))
    front_in = [h, conv_in, nw, win, cw, cb, dtb, alog, wgk, bgk, qnw, knw]
    front_specs = [row_blk(D_MODEL), _whole(conv_in.shape, layer)] + [_layer_spec(a, layer) for a in front_in[2:]]
    front_out = (f32(rows, SSD_WIDTH), f32(rows, D_MODEL),
                 f32(seq, SSD_WIDTH, n_seq), f32(seq, LANES, n_seq), f32(seq, LANES, n_seq),
                 f32(seq, SSD_HEADS, n_seq), f32(seq, LANES, n_seq), f32(seq, LANES, n_seq), f32(seq, LANES, n_seq),
                 f32(seq, GLA_WIDTH, n_seq), f32(rows, SWA_WIDTH), f32(rows, LANES), f32(rows, LANES),
                 f32(*conv_in.shape))
    front_out_specs = (row_blk(SSD_WIDTH), row_blk(D_MODEL), pos_blk(SSD_WIDTH), pos_blk(LANES), pos_blk(LANES),
                       pos_blk(SSD_HEADS), pos_blk(LANES), pos_blk(LANES), pos_blk(LANES), pos_blk(GLA_WIDTH),
                       row_blk(SWA_WIDTH), row_blk(LANES), row_blk(LANES), _whole(conv_in.shape, layer))
    (xs, gates, xt, bt, ct, at, qt, kt, egt, vt, qn, kn, vn, conv_o) = pl.pallas_call(
        functools.partial(_sample_front_kernel, seq, len(prev_conv)),
        grid=(seq // per_step,),
        in_specs=front_specs + [pl.BlockSpec(memory_space=pl.ANY)] * len(prev_conv),
        out_specs=front_out_specs,
        out_shape=front_out,
        input_output_aliases={len(front_in) + k: len(front_out) - 1 + k for k in range(len(prev_conv))},
        scratch_shapes=[pltpu.VMEM((step_rows, D_MODEL), BF16), pltpu.VMEM((rows, XBC_W), F32),
                        pltpu.VMEM((step_rows, PROJ_W), F32)],
        compiler_params=cparams(("arbitrary",)),
        name="sample_front",
    )(*front_in, *prev_conv)

    n_steps = SSD_HEADS
    per_group = SSD_HEADS // SSD_GROUPS
    gla_head = lambda j: jnp.minimum(j, GLA_HEADS - 1)
    blk3 = lambda n, f: pl.BlockSpec((seq, n, n_seq), lambda j: (0, f(j), 0))
    state_in = [xt, bt, ct, at, qt, kt, egt, vt, qn, kn, vn, ssm_in, gla_in, kc_in, vc_in, buckets[0], buckets[1],
                rel, sinks]
    ssm_spec = pl.BlockSpec((1, 1) + ssm_in.shape[2:], lambda j: (layer, j, 0, 0))
    gla_spec = pl.BlockSpec((1, 1) + gla_in.shape[2:], lambda j: (layer, gla_head(j), 0, 0))
    kv_spec = pl.BlockSpec((1, SEQ_PER_STEP) + kc_in.shape[2:], lambda j: (layer, j, 0, 0))
    state_specs = [blk3(SSD_HEAD_DIM, lambda j: j), blk3(SSD_STATE, lambda j: j // per_group),
                   blk3(SSD_STATE, lambda j: j // per_group), _const_spec(at.shape),
                   blk3(GLA_DK, gla_head), blk3(GLA_DK, gla_head), blk3(GLA_DK, gla_head), blk3(GLA_DV, gla_head),
                   _const_spec(qn.shape), _const_spec(kn.shape), _const_spec(vn.shape),
                   ssm_spec, gla_spec, kv_spec, kv_spec,
                   _const_spec(buckets[0].shape), _const_spec(buckets[1].shape), _smem_spec(), _smem_spec()]
    state_out = (f32(*ssm_in.shape), f32(*gla_in.shape), f32(*kc_in.shape), f32(*vc_in.shape),
                 f32(seq, SSD_WIDTH, n_seq), f32(seq, GLA_WIDTH, n_seq), f32(seq, n_seq, SWA_WIDTH))
    ssm_o, gla_o, ko, vo, yt, ot, oswa = pl.pallas_call(
        functools.partial(_sample_state_kernel, seq, len(prev_rest)),
        grid=(n_steps,),
        in_specs=state_specs + [pl.BlockSpec(memory_space=pl.ANY)] * len(prev_rest),
        out_specs=(ssm_spec, gla_spec, kv_spec, kv_spec, blk3(SSD_HEAD_DIM, lambda j: j), blk3(GLA_DV, gla_head),
                   pl.BlockSpec((seq, SEQ_PER_STEP, SWA_WIDTH), lambda j: (0, j, 0))),
        out_shape=state_out,
        input_output_aliases={len(state_in) + k: k for k in range(len(prev_rest))},
        scratch_shapes=[pltpu.VMEM((SEQ_PER_STEP * seq, LANES), F32)] * 6
        + [pltpu.VMEM((SWA_HEADS * seq, LANES), F32)] * 2,
        compiler_params=cparams(("arbitrary",)),
        name="sample_state",
    )(*state_in, *prev_rest)

    back_in = [yt, ot, oswa, xs, gates, h, p_all, dsk, snw, gnw, wout, wpe, wpg]
    back_specs = [pos_blk(SSD_WIDTH), pos_blk(GLA_WIDTH),
                  pl.BlockSpec((per_step, n_seq, SWA_WIDTH), lambda s: (s, 0, 0)),
                  row_blk(SSD_WIDTH), row_blk(D_MODEL), row_blk(D_MODEL),
                  pl.BlockSpec((1, step_rows, PLE_DIM), lambda s: (layer, s, 0))] + [
        _layer_spec(a, layer) for a in back_in[7:]]
    y = pl.pallas_call(
        _sample_back_kernel,
        grid=(seq // per_step,),
        in_specs=back_specs,
        out_specs=row_blk(D_MODEL),
        out_shape=f32(rows, D_MODEL),
        scratch_shapes=[pltpu.VMEM((step_rows, D_MODEL), BF16)],
        compiler_params=cparams(("arbitrary",)),
        name="sample_back",
    )(*back_in)
    return y, (conv_o, ssm_o, gla_o, ko, vo)


SWA_HEAD_ORDER = (0, 2, 1, 3)


def _win_tile_runs():
    sizes = (SSD_WIDTH, SSD_CONV_DIM, SSD_HEADS, GLA_HEADS * GLA_DK, GLA_HEADS * GLA_DK, GLA_WIDTH, GLA_WIDTH,
             GLA_RANK, SWA_WIDTH, SWA_KV_HEADS * SWA_HEAD_DIM, SWA_KV_HEADS * SWA_HEAD_DIM, SWA_WIDTH)
    offs = np.concatenate([[0], np.cumsum(sizes)])
    seg = lambda k: np.arange(offs[k], offs[k + 1])
    z, xbc, dt, gq, gk, gv, gg, glr, sq, sk, sv, sg = [seg(k) for k in range(len(sizes))]
    heads = lambda a: np.concatenate([a[h * SWA_HEAD_DIM:(h + 1) * SWA_HEAD_DIM] for h in SWA_HEAD_ORDER])
    pad = np.full(LANES - SSD_HEADS - GLA_RANK, -1)
    src = np.concatenate([xbc, z, gq, gk, gv, gg, heads(sq), sk, sv, heads(sg), dt, glr, pad])
    assert src.size == XBC_W + PROJ_W
    tiles = []
    for j in range(src.size // LANES):
        idx = src[j * LANES:(j + 1) * LANES]
        cuts = [0] + [k for k in range(1, LANES) if (idx[k] != idx[k - 1] + 1 and not (idx[k] == -1 == idx[k - 1]))]
        runs = [(int(idx[a]), b - a) for a, b in zip(cuts, cuts[1:] + [LANES])]
        assert all(n % SUBLANES == 0 and (s < 0 or s % SUBLANES == 0) for s, n in runs)
        tiles.append(runs)
    return tiles


def _win_prep_kernel(tile_runs, wt_ref, out_ref):
    for j, runs in enumerate(tile_runs):
        parts = [jnp.zeros((n, D_MODEL), F32) if s < 0 else wt_ref[0, s:s + n, :] for s, n in runs]
        tile = parts[0] if len(parts) == 1 else jnp.concatenate(parts, axis=0)
        out_ref[0, :, j * LANES:(j + 1) * LANES] = tile.T.astype(BF16)


def _prepare_w_in(w_in):
    depth, d_model, d_in = w_in.shape
    w_t = jnp.swapaxes(w_in, 1, 2)
    return pl.pallas_call(
        functools.partial(_win_prep_kernel, _win_tile_runs()),
        grid=(depth,),
        in_specs=[pl.BlockSpec((1, d_in, d_model), lambda l: (l, 0, 0))],
        out_specs=pl.BlockSpec((1, d_model, XBC_W + PROJ_W), lambda l: (l, 0, 0)),
        out_shape=jax.ShapeDtypeStruct((depth, d_model, XBC_W + PROJ_W), BF16),
        compiler_params=pltpu.CompilerParams(
            dimension_semantics=("arbitrary",), vmem_limit_bytes=VMEM_LIMIT_BYTES),
        name="w_in_prep",
    )(w_t)


def _cast_prep_kernel(wout_ref, wpe_ref, wpg_ref, wout_o, wpe_o, wpg_o):
    mix_w = SSD_WIDTH + GLA_WIDTH
    wout_o[0, 0:mix_w, :] = wout_ref[0, 0:mix_w, :].astype(BF16)
    for slot, head in enumerate(SWA_HEAD_ORDER):
        src = slice(mix_w + head * SWA_HEAD_DIM, mix_w + (head + 1) * SWA_HEAD_DIM)
        dst = slice(mix_w + slot * SWA_HEAD_DIM, mix_w + (slot + 1) * SWA_HEAD_DIM)
        wout_o[0, dst, :] = wout_ref[0, src, :].astype(BF16)
    wpe_o[0] = wpe_ref[0].astype(BF16)
    wpg_o[0] = wpg_ref[0].astype(BF16)


def _prepare_out_weights(w_out, w_pe, w_pg):
    depth = w_out.shape[0]
    ops = (w_out, w_pe, w_pg)
    spec = lambda a: pl.BlockSpec((1,) + a.shape[1:], lambda l: (l, 0, 0))
    return pl.pallas_call(
        _cast_prep_kernel,
        grid=(depth,),
        in_specs=[spec(a) for a in ops],
        out_specs=tuple(spec(a) for a in ops),
        out_shape=tuple(jax.ShapeDtypeStruct(a.shape, BF16) for a in ops),
        compiler_params=pltpu.CompilerParams(
            dimension_semantics=("arbitrary",), vmem_limit_bytes=VMEM_LIMIT_BYTES),
        name="w_out_prep",
    )(*ops)


def _prepare_weights(norm_w, w_in, conv_w, conv_b, dt_bias, a_log, d_skip, ssd_norm_w, gla_w_gk, gla_b_gk,
                     gla_norm_w, q_norm_w, k_norm_w, w_out, w_pe, w_pg):
    w_out_p, w_pe_p, w_pg_p = _prepare_out_weights(w_out, w_pe, w_pg)
    lane_pad = lambda x: jnp.pad(x, ((0, 0), (0, LANES - x.shape[-1])))[:, None, :]
    wgk_p = jnp.pad(gla_w_gk, ((0, 0), (LR_LANE0, LANES - LR_LANE0 - GLA_RANK), (0, 0))).astype(BF16)
    return dict(
        norm_w=norm_w[:, None, :], w_in=_prepare_w_in(w_in), conv_w=conv_w, conv_b=conv_b[:, None, :],
        dt_bias=lane_pad(dt_bias), a_log=lane_pad(a_log),
        d_skip=jnp.repeat(d_skip, SSD_HEAD_DIM, axis=-1)[:, None, :], ssd_norm_w=ssd_norm_w[:, None, :],
        gla_w_gk=wgk_p, gla_b_gk=gla_b_gk[:, None, :],
        gla_norm_w=jnp.tile(gla_norm_w, (1, GLA_HEADS))[:, None, :],
        q_norm_w=jnp.tile(q_norm_w, (1, SWA_HEADS))[:, None, :],
        k_norm_w=jnp.tile(k_norm_w, (1, SWA_KV_HEADS))[:, None, :],
        w_out=w_out_p, w_pe=w_pe_p, w_pg=w_pg_p)


def kernel(x_prompt, x_sample, state_ssm, state_conv, state_gla, cache_swa_k, cache_swa_v, p_prompt, p_sample, rel_bias, norm_w, w_in, conv_w, conv_b, dt_bias, a_log, d_skip, ssd_norm_w, gla_w_gk, gla_b_gk, gla_norm_w, q_norm_w, k_norm_w, attn_sinks, w_out, w_pe, w_pg):
    depth = w_in.shape[0]
    bp, seq_p, _ = x_prompt.shape
    bs, seq_s, _ = x_sample.shape
    assert seq_s == SUBLANES and BLK % seq_s == 0 and (bs * seq_s) % BLK == 0
    assert cache_swa_k.shape[2] == WINDOW
    w = _prepare_weights(norm_w, w_in, conv_w, conv_b, dt_bias, a_log, d_skip, ssd_norm_w, gla_w_gk, gla_b_gk,
                         gla_norm_w, q_norm_w, k_norm_w, w_out, w_pe, w_pg)
    rel_flat = rel_bias.reshape(-1)
    dist_p = WINDOW + np.arange(BLK)[:, None] - np.arange(2 * BLK)[None, :]
    bucket_p = jnp.asarray(_bucket_table(dist_p))
    t_of_row = np.tile(np.arange(seq_s), SWA_HEADS)[:, None]
    bucket_c = jnp.asarray(_bucket_table(WINDOW + t_of_row - np.arange(WINDOW)[None, :]))
    dist_n = np.where(np.arange(LANES)[None, :] < seq_s, t_of_row - np.arange(LANES)[None, :], -1)
    bucket_n = jnp.asarray(_bucket_table(dist_n))

    ssm_in = jnp.transpose(state_ssm, (0, 2, 3, 4, 1)).reshape(depth, SSD_HEADS, SSD_HEAD_DIM * SSD_STATE, bs)
    gla_in = jnp.transpose(state_gla, (0, 2, 3, 4, 1)).reshape(depth, GLA_HEADS, GLA_DK * GLA_DV, bs)
    kv_in = lambda a: jnp.transpose(a, (0, 1, 3, 4, 2)).reshape(depth, bs, SWA_KV_HEADS * SWA_HEAD_DIM, WINDOW)
    kc_in, vc_in = kv_in(cache_swa_k), kv_in(cache_swa_v)
    conv_in = jnp.transpose(state_conv, (0, 2, 1, 3))

    hp = x_prompt.reshape(bp * seq_p, D_MODEL)
    p_prompt_rows = p_prompt.reshape(depth, bp * seq_p, PLE_DIM)
    hs = jnp.transpose(x_sample, (1, 0, 2)).reshape(seq_s * bs, D_MODEL)
    p_sample_rows = jnp.transpose(p_sample, (0, 2, 1, 3)).reshape(depth, seq_s * bs, PLE_DIM)
    states_p = ()
    states_s = ()
    for i in range(depth):
        wops, wspecs = _layer_weights(i, w)
        hp, *states_p = _prompt_layer(i, depth, bp, hp, p_prompt_rows, tuple(states_p), bucket_p, rel_flat,
                                      attn_sinks[i], wops, wspecs)
        hs, states_s = _sample_layer_native(i, depth, seq_s, hs, p_sample_rows, conv_in, ssm_in, gla_in, kc_in,
                                            vc_in, states_s, (bucket_c, bucket_n), rel_flat, attn_sinks[i], wops)
    ssm_p, conv_p, gla_p, kt_p, vt_p = states_p
    conv_s, ssm_s, gla_s, kt_s, vt_s = states_s
    unpack_kv = lambda a: jnp.transpose(
        a.reshape(a.shape[:2] + (SWA_KV_HEADS, SWA_HEAD_DIM, WINDOW)), (0, 1, 4, 2, 3))
    outs_p = (ssm_p.reshape(depth, bp, SSD_HEADS, SSD_HEAD_DIM, SSD_STATE), conv_p,
              gla_p.reshape(depth, bp, GLA_HEADS, GLA_DK, GLA_DV), unpack_kv(kt_p), unpack_kv(vt_p))
    seq_last = lambda a, dims: jnp.transpose(a.reshape(a.shape[:2] + dims + (bs,)), (0, 4, 1, 2, 3))
    outs_s = (seq_last(ssm_s, (SSD_HEAD_DIM, SSD_STATE)), jnp.transpose(conv_s, (0, 2, 1, 3)),
              seq_last(gla_s, (GLA_DK, GLA_DV)), unpack_kv(kt_s), unpack_kv(vt_s))
    y_sample = jnp.transpose(hs.reshape(seq_s, bs, D_MODEL), (1, 0, 2))
    return (hp.reshape(bp, seq_p, D_MODEL), y_sample) + outs_p + outs_s
```

```python
import functools
import math

import numpy as np
import jax
import jax.numpy as jnp
from jax import lax
from jax.experimental import pallas as pl
from jax.experimental.pallas import tpu as pltpu

D_MODEL = 1024
DEPTH = 2
SSD_HEADS = 8
SSD_HEAD_DIM = 64
SSD_WIDTH = SSD_HEADS * SSD_HEAD_DIM
SSD_GROUPS = 2
SSD_STATE = 64
SSD_CONV = 4
SSD_CONV_DIM = SSD_WIDTH + 2 * SSD_GROUPS * SSD_STATE
SSD_CHUNK = 128
GLA_HEADS = 4
GLA_DK = 32
GLA_DV = 64
GLA_WIDTH = GLA_HEADS * GLA_DV
GLA_RANK = 16
GLA_GATE_NORM = 16.0
GLA_CHUNK = 64
SWA_HEADS = 4
SWA_KV_HEADS = 2
SWA_HEAD_DIM = 64
SWA_WIDTH = SWA_HEADS * SWA_HEAD_DIM
WINDOW = 128
REL_BUCKETS = 32
REL_MAX_DIST = 128
PLE_DIM = 256
EPS = 1e-6

LANES = 128
SUBLANES = 8
HALF = LANES // 2
BLK = 128
VMEM_LIMIT_BYTES = 56 * 1024 * 1024

XBC_W = SSD_CONV_DIM
P_Z = 0
P_GQ = P_Z + SSD_WIDTH
P_GK = P_GQ + LANES
P_GV = P_GK + LANES
P_GG = P_GV + GLA_WIDTH
P_SQ = P_GG + GLA_WIDTH
P_SK = P_SQ + SWA_WIDTH
P_SV = P_SK + LANES
P_SG = P_SV + LANES
P_DTLR = P_SG + SWA_WIDTH
PROJ_W = P_DTLR + LANES
LR_LANE0 = SSD_HEADS

F32 = jnp.float32
BF16 = jnp.bfloat16
NEG_INF = float("-inf")
LOG2E = math.log2(math.e)
N_PROMPT_INPUTS = 22
PROMPT_CHUNK_ROWS = 2 * BLK
TICKS_PER_ITEM = 2
GATED_AFTER_PROJ_ITEMS = 9
NT_DIMS = (((1,), (1,)), ((), ()))


def _iota(shape, dim):
    return lax.broadcasted_iota(jnp.int32, shape, dim)


def _div(x, d):
    return x >> (d.bit_length() - 1)


def _mod(x, d):
    return x & (d - 1)


def _softplus(x):
    e = jnp.exp(-jnp.abs(x))
    u = 1.0 + e
    d = u - 1.0
    log1p_e = jnp.where(d == 0.0, e, jnp.log(u) * (e / jnp.where(d == 0.0, 1.0, d)))
    return jnp.maximum(x, 0.0) + log1p_e


def _log_sigmoid(x):
    return jnp.minimum(x, 0.0) - jnp.log(1.0 + jnp.exp(-jnp.abs(x)))


def _silu(x):
    return x * jax.nn.sigmoid(x)


def _dot(a, b):
    return jnp.dot(a.astype(BF16), b.astype(BF16), preferred_element_type=F32)


def _dot_nt(a, b):
    return lax.dot_general(a.astype(BF16), b.astype(BF16), NT_DIMS, preferred_element_type=F32)


def _dot_exact(sel, x):
    x1 = x.astype(BF16)
    r1 = x - x1.astype(F32)
    x2 = r1.astype(BF16)
    x3 = (r1 - x2.astype(F32)).astype(BF16)
    dot = functools.partial(jnp.dot, sel, preferred_element_type=F32)
    return dot(x1) + dot(x2) + dot(x3)


def _expand_heads(x, n_heads):
    rows = x.shape[0]
    lo = _iota((rows, LANES), 1) < HALF
    tiles = []
    for j in range(n_heads // 2):
        a = jnp.broadcast_to(x[:, 2 * j:2 * j + 1], (rows, LANES))
        b = jnp.broadcast_to(x[:, 2 * j + 1:2 * j + 2], (rows, LANES))
        tiles.append(jnp.where(lo, a, b))
    return jnp.concatenate(tiles, axis=1)


def _head_rms_scale(x):
    rows, width = x.shape
    lo = _iota((rows, LANES), 1) < HALF
    outs = []
    for j in range(width // LANES):
        t = x[:, j * LANES:(j + 1) * LANES]
        sq = t * t
        s_lo = jnp.sum(jnp.where(lo, sq, 0.0), axis=-1, keepdims=True)
        s_hi = jnp.sum(jnp.where(lo, 0.0, sq), axis=-1, keepdims=True)
        outs.append(lax.rsqrt(jnp.where(lo, s_lo, s_hi) * (1.0 / HALF) + EPS))
    return outs[0] if len(outs) == 1 else jnp.concatenate(outs, axis=1)


def _group_rmsnorm(y, w):
    gw = SSD_WIDTH // SSD_GROUPS
    outs = []
    for g in range(SSD_GROUPS):
        t = y[:, g * gw:(g + 1) * gw]
        ms = jnp.sum(t * t, axis=-1, keepdims=True) * (1.0 / gw)
        outs.append(t * lax.rsqrt(ms + EPS))
    return jnp.concatenate(outs, axis=1) * w


def _rel_bucket_np(dist):
    n = np.maximum(dist, 0)
    exact = REL_BUCKETS // 2
    nf = np.maximum(n, 1).astype(np.float64)
    large = exact + (np.log(nf / exact) / math.log(REL_MAX_DIST / exact) * (REL_BUCKETS - exact)).astype(np.int32)
    large = np.minimum(large, REL_BUCKETS - 1)
    return np.where(n < exact, n, large).astype(np.int32)


def _bucket_table(dist):
    return np.where((dist >= 0) & (dist < WINDOW), _rel_bucket_np(dist), -1).astype(np.int32)


def _no_tick():
    pass


def _ssd_intra(xbc_c, dtv, acum, pair_mask, tick=_no_tick):
    xs = xbc_c[:, :SSD_WIDTH]
    bm = xbc_c[:, SSD_WIDTH:SSD_WIDTH + LANES]
    cm = xbc_c[:, SSD_WIDTH + LANES:]
    lane = _iota((BLK, LANES), 1)
    lo = lane < HALF
    acum_t = acum.T
    eacum = jnp.exp2(acum)
    tail = jnp.exp2(acum[BLK - 1:BLK, :] - acum)
    dtv_e = _expand_heads(dtv, SSD_HEADS)
    eacum_e = _expand_heads(eacum, SSD_HEADS)
    tail_e = _expand_heads(tail, SSD_HEADS)
    tick()
    xdt = xs * dtv_e
    xw = xdt * tail_e
    cb = [_dot_nt(jnp.where(lo, cm, 0.0), bm), _dot_nt(jnp.where(lo, 0.0, cm), bm)]
    y_pairs = []
    for j in range(SSD_HEADS // 2):
        tick()
        g = (2 * j) // (SSD_HEADS // SSD_GROUPS)
        ms = []
        for k in range(2):
            h = 2 * j + k
            seg = acum[:, h:h + 1] - acum_t[h:h + 1, :]
            dec = jnp.where(pair_mask, jnp.exp2(seg), 0.0)
            ms.append((cb[g] * dec).astype(BF16))
        xp = xdt[:, j * LANES:(j + 1) * LANES]
        rhs = jnp.concatenate([jnp.where(lo, xp, 0.0), jnp.where(lo, 0.0, xp)], axis=0)
        y_pairs.append(_dot(jnp.concatenate(ms, axis=1), rhs))
    y_intra = jnp.concatenate(y_pairs, axis=1)
    return y_intra, xs, bm, cm, xw, eacum, eacum_e


def _gla_intra(gq, gk, gv, bcs, att_mask, tick=_no_tick):
    eb = jnp.exp2(bcs)
    qe = gq * (GLA_DK ** -0.5) * eb
    ke = gk * jnp.exp2(-bcs)
    btot = jnp.concatenate(
        [jnp.broadcast_to(bcs[(c2 + 1) * GLA_CHUNK - 1:(c2 + 1) * GLA_CHUNK, :], (GLA_CHUNK, LANES))
         for c2 in range(BLK // GLA_CHUNK)], axis=0)
    kd = gk * jnp.exp2(btot - bcs)
    lane_k = _iota((GLA_CHUNK, LANES), 1)
    lane_v = _iota((GLA_CHUNK, GLA_WIDTH), 1)
    outs = []
    for c2 in range(BLK // GLA_CHUNK):
        tick()
        rs = slice(c2 * GLA_CHUNK, (c2 + 1) * GLA_CHUNK)
        ke_c = ke[rs]
        v_c = gv[rs]
        kbd = jnp.concatenate(
            [jnp.where(_div(lane_k, GLA_DK) == h, ke_c, 0.0) for h in range(GLA_HEADS)], axis=0)
        att = _dot_nt(qe[rs], kbd)
        att = jnp.where(att_mask, att, 0.0)
        vbd = jnp.concatenate(
            [jnp.where(_div(lane_v, GLA_DV) == h, v_c, 0.0) for h in range(GLA_HEADS)], axis=0)
        outs.append(_dot(att, vbd))
    return jnp.concatenate(outs, axis=0), qe, kd, jnp.exp2(btot)


def _build_bias(bucket, rel_ref):
    accs = [jnp.full(bucket.shape, NEG_INF, F32) for _ in range(SWA_HEADS)]
    for b in range(REL_BUCKETS):
        hit = bucket == b
        for h in range(SWA_HEADS):
            accs[h] = jnp.where(hit, rel_ref[b * SWA_HEADS + h], accs[h])
    return accs


def _epilogue(h, mix, p, wout_ref, wpg_ref, wpe_ref):
    h1 = h + jnp.dot(mix, wout_ref[...], preferred_element_type=F32)
    gate = jax.nn.sigmoid(jnp.dot(h1.astype(BF16), wpg_ref[...], preferred_element_type=F32))
    pe = jnp.dot(p.astype(BF16), wpe_ref[...], preferred_element_type=F32)
    return h1 + gate * pe


def _prompt_kernel(chunks_per_seq, n_aliased, *refs):
    (ha_ref, hc_ref, p_ref, bucket_ref, rel_ref, sink_ref, nw_ref, win_ref, cw_ref, cb_ref, dtb_ref,
     alog_ref, dsk_ref, snw_ref, wgk_ref, bgk_ref, gnw_ref, qnw_ref, knw_ref, wout_ref,
     wpe_ref, wpg_ref) = refs[:N_PROMPT_INPUTS]
    (y_ref, ssm_ref, conv_ref, gla_ref, ko_ref, vo_ref,
     proj_e, proj_o, xbc_e, xbc_o, mix_e, mix_o, u_s, h1_s, h1b_s, hist_s, st_s, s2_s, kext_s, vext_s,
     bias_s, hist_snap, st_snap, s2_snap, k_snap, v_snap) = refs[N_PROMPT_INPUTS + n_aliased:]
    k_idx = pl.program_id(0)

    @pl.when(k_idx == 0)
    def _():
        accs = [a * LOG2E for a in _build_bias(bucket_ref[...], rel_ref)]
        own_block = _iota((BLK, 2 * BLK), 1) >= BLK
        for hh in range(SWA_HEADS):
            bias_s[hh] = accs[hh]
            bias_s[SWA_HEADS + hh] = jnp.where(own_block, accs[hh], NEG_INF)
        for ref in (proj_o, xbc_o, mix_e, mix_o, hist_s, st_s, s2_s, kext_s, vext_s):
            ref[...] = jnp.zeros(ref.shape, ref.dtype)

    row = _iota((BLK, BLK), 0)
    col = _iota((BLK, BLK), 1)
    causal = row >= col
    tri = jnp.where(causal, 1.0, 0.0).astype(BF16)
    lo = col < HALF
    lane_row = _iota((1, LANES), 1)
    a_row = jnp.where(lane_row < SSD_HEADS, -jnp.exp(alog_ref[...]) * LOG2E, 0.0)
    bd_mask = _div(_iota((LANES, GLA_WIDTH), 0), GLA_DK) == _div(_iota((LANES, GLA_WIDTH), 1), GLA_DV)
    att_t = _iota((GLA_CHUNK, GLA_WIDTH), 0)
    att_s = _mod(_iota((GLA_CHUNK, GLA_WIDTH), 1), GLA_CHUNK)
    att_mask = att_s <= att_t
    lo2 = _iota((2 * BLK, LANES), 1) < HALF

    group_w = SSD_WIDTH // SSD_GROUPS

    def block(blk, proj_s, xbc_s, mix_s, starts_sequence, tick):
        rows = slice(blk * BLK, (blk + 1) * BLK)
        cw = cw_ref[...]
        tick()
        if blk == 0:
            xwin = jnp.concatenate([hist_s[...], xbc_s[0:BLK, :]], axis=0)
        else:
            xwin = xbc_s[blk * BLK - SUBLANES:(blk + 1) * BLK, :]
        acc = xwin[SUBLANES - 3:SUBLANES - 3 + BLK, :] * cw[0:1, :]
        for k in range(1, SSD_CONV):
            acc = acc + xwin[SUBLANES - 3 + k:SUBLANES - 3 + k + BLK, :] * cw[k:k + 1, :]
        xbc_c = _silu(acc + cb_ref[...])
        tick()
        dtlr = proj_s[rows, P_DTLR:P_DTLR + LANES]
        dtv_t = _softplus((dtlr + dtb_ref[...]).T[0:SSD_HEADS, :])
        dtv = jnp.concatenate([dtv_t, jnp.zeros((LANES - SSD_HEADS, BLK), F32)], axis=0).T
        glog = _log_sigmoid(_dot(dtlr, wgk_ref[...]) + bgk_ref[...]) * (LOG2E / GLA_GATE_NORM)
        sums = _dot_exact(tri, jnp.concatenate([dtv * a_row, glog], axis=1))
        acum = sums[:, :LANES]
        gsum = sums[:, LANES:]
        bcs = gsum
        for c2 in range(1, BLK // GLA_CHUNK):
            before = gsum[c2 * GLA_CHUNK - 1:c2 * GLA_CHUNK, :]
            bcs = jnp.where(_div(row, GLA_CHUNK) == c2, gsum - before, bcs)
        tick()
        y_intra, xs, bm, cm, xw, eacum, eacum_e = _ssd_intra(xbc_c, dtv, acum, causal, tick)
        tick()
        st = st_s[...]
        y = y_intra + _dot(cm, st) * eacum_e + dsk_ref[...] * xs
        bm_t = bm.T
        for g in range(SSD_GROUPS):
            gr = slice(g * SSD_STATE, (g + 1) * SSD_STATE)
            gc = slice(g * group_w, (g + 1) * group_w)
            st_s[gr, gc] = st[gr, gc] * eacum_e[BLK - 1:BLK, gc] + _dot(bm_t[gr, :], xw[:, gc])
        tick()
        y = y * _silu(proj_s[rows, P_Z:P_Z + SSD_WIDTH])
        mix_s[rows, 0:SSD_WIDTH] = _group_rmsnorm(y, snw_ref[...]).astype(BF16)
        tick()
        gk = proj_s[rows, P_GK:P_GK + LANES]
        gv = proj_s[rows, P_GV:P_GV + GLA_WIDTH]
        o_intra, qe, kd, ebt = _gla_intra(proj_s[rows, P_GQ:P_GQ + LANES], gk, gv, bcs, att_mask, tick)
        kd_t = kd.T
        ebt_t = ebt.T
        s2 = s2_s[...]
        o_parts = []
        for c2 in range(BLK // GLA_CHUNK):
            tick()
            rs = slice(c2 * GLA_CHUNK, (c2 + 1) * GLA_CHUNK)
            o_parts.append(o_intra[rs] + _dot(qe[rs], s2))
            u2 = _dot(jnp.where(_div(col, GLA_CHUNK) == c2, kd_t, 0.0), gv)
            last = (c2 + 1) * GLA_CHUNK - 1
            s2 = s2 * ebt_t[:, last:last + 1] + jnp.where(bd_mask, u2, 0.0)
        s2_s[...] = s2
        o = jnp.concatenate(o_parts, axis=0)
        y_gla = o * _head_rms_scale(o) * gnw_ref[...] * _silu(proj_s[rows, P_GG:P_GG + GLA_WIDTH])
        mix_s[rows, SSD_WIDTH:SSD_WIDTH + GLA_WIDTH] = y_gla.astype(BF16)
        tick()
        sq = proj_s[rows, P_SQ:P_SQ + SWA_WIDTH]
        qn = sq * _head_rms_scale(sq) * qnw_ref[...] * (SWA_HEAD_DIM ** -0.5 * LOG2E)
        sk = proj_s[rows, P_SK:P_SK + LANES]
        kn = sk * _head_rms_scale(sk) * knw_ref[...]
        vn = proj_s[rows, P_SV:P_SV + LANES]
        kext_s[BLK:2 * BLK, :] = kn
        vext_s[BLK:2 * BLK, :] = vn
        kext = kext_s[...]
        vext = vext_s[...]
        qa = qn[:, :LANES]
        qb = qn[:, LANES:]
        qs = jnp.concatenate([jnp.where(lo, qa, 0.0), jnp.where(lo, qb, 0.0),
                              jnp.where(lo, 0.0, qa), jnp.where(lo, 0.0, qb)], axis=0)
        logits = _dot_nt(qs, kext)
        tick()
        if blk == 0 and starts_sequence is not False:
            bias_row0 = jnp.where(starts_sequence, SWA_HEADS, 0)
        else:
            bias_row0 = 0
        es = []
        invs = []
        for hh in range(SWA_HEADS):
            tick()
            sink = sink_ref[hh] * LOG2E
            l = logits[hh * BLK:(hh + 1) * BLK] + bias_s[bias_row0 + hh]
            m = jnp.maximum(jnp.max(l, axis=-1, keepdims=True), sink)
            e = jnp.exp2(l - m)
            den = jnp.sum(e, axis=-1, keepdims=True) + jnp.exp2(sink - m)
            es.append(e.astype(BF16))
            invs.append(1.0 / den)
        v_stack = jnp.concatenate([jnp.where(lo2, vext, 0.0), jnp.where(lo2, 0.0, vext)], axis=0)
        tile_a = _dot(jnp.concatenate([es[0], es[2]], axis=1), v_stack) * jnp.where(lo, invs[0], invs[2])
        tile_b = _dot(jnp.concatenate([es[1], es[3]], axis=1), v_stack) * jnp.where(lo, invs[1], invs[3])
        oa = jnp.concatenate([tile_a, tile_b], axis=1)
        y_swa = oa * _silu(proj_s[rows, P_SG:P_SG + SWA_WIDTH])
        mix_s[rows, SSD_WIDTH + GLA_WIDTH:] = y_swa.astype(BF16)
        kext_s[0:BLK, :] = kn
        vext_s[0:BLK, :] = vn

    chunk = proj_e.shape[0]

    def project_items(rows, proj_s, xbc_s):
        def norm():
            h = ha_ref[rows, :]
            ms = jnp.mean(h * h, axis=-1, keepdims=True)
            u_s[...] = (h * lax.rsqrt(ms + EPS) * nw_ref[...]).astype(BF16)

        def cols(dst, lo_c, hi_c, w_off):
            def item():
                dst[:, lo_c:hi_c] = jnp.dot(u_s[...], win_ref[:, w_off + lo_c:w_off + hi_c],
                                            preferred_element_type=F32)
            return item

        step = 2 * LANES
        items = [norm]
        items += [cols(xbc_s, c, min(c + step, XBC_W), 0) for c in range(0, XBC_W, step)]
        items += [cols(proj_s, c, min(c + step, PROJ_W), XBC_W) for c in range(0, PROJ_W, step)]
        return items

    def epilogue_items(rows, mix_s):
        half_w = 2 * LANES

        def residual(c):
            def item():
                h1 = hc_ref[rows, c:c + half_w] + jnp.dot(
                    mix_s[...], wout_ref[:, c:c + half_w], preferred_element_type=F32)
                h1_s[:, c:c + half_w] = h1
                h1b_s[:, c:c + half_w] = h1.astype(BF16)
            return item

        def gated(c):
            def item():
                gate = jax.nn.sigmoid(jnp.dot(h1b_s[...], wpg_ref[:, c:c + half_w], preferred_element_type=F32))
                pe = jnp.dot(p_ref[0, rows, :].astype(BF16), wpe_ref[:, c:c + half_w],
                             preferred_element_type=F32)
                y_ref[rows, c:c + half_w] = h1_s[:, c:c + half_w] + gate * pe
            return item

        col0 = range(0, D_MODEL, half_w)
        return [residual(c) for c in col0], [gated(c) for c in col0]

    def mixer(proj_s, xbc_s, mix_s, starts_sequence, items):
        if starts_sequence is not False:
            keep = jnp.where(starts_sequence, 0.0, 1.0)
            for ref in (hist_s, st_s, s2_s):
                ref[...] = ref[...] * keep
            kext_s[0:BLK, :] = kext_s[0:BLK, :] * keep
            vext_s[0:BLK, :] = vext_s[0:BLK, :] * keep
        queue = list(items)
        calls = [0]

        def tick():
            calls[0] += 1
            if queue and calls[0] % TICKS_PER_ITEM == 0:
                queue.pop(0)()

        for blk in range(chunk // BLK):
            block(blk, proj_s, xbc_s, mix_s, starts_sequence, tick)
        while queue:
            queue.pop(0)()
        hist_s[...] = xbc_s[chunk - SUBLANES:chunk, :]

    def snapshot_states():
        st_snap[...] = st_s[...]
        s2_snap[...] = s2_s[...]
        hist_snap[...] = hist_s[...]
        k_snap[...] = kext_s[0:BLK, :]
        v_snap[...] = vext_s[0:BLK, :]

    def write_states():
        st = st_snap[...]
        stc = st[:SSD_STATE] + st[SSD_STATE:]
        ssm_ref[0, 0] = jnp.concatenate([stc, stc], axis=0).T[:, :SSD_STATE]
        conv_ref[0, 0] = hist_snap[SUBLANES - (SSD_CONV - 1):SUBLANES, :]
        s2 = s2_snap[...]
        w = s2[:, :LANES] + s2[:, LANES:]
        gla_ref[0, 0] = w[:, :GLA_DV] + w[:, GLA_DV:]
        ko_ref[0, 0] = k_snap[...].T
        vo_ref[0, 0] = v_snap[...].T

    even = slice(0, chunk)
    odd = slice(chunk, 2 * chunk)
    def stage_items(rows, proj_w, xbc_w, mix_r):
        residuals, gateds = epilogue_items(rows, mix_r)
        proj = project_items(rows, proj_w, xbc_w)
        return residuals + proj[:GATED_AFTER_PROJ_ITEMS] + gateds + proj[GATED_AFTER_PROJ_ITEMS:]

    mixer(proj_o, xbc_o, mix_o, False, stage_items(even, proj_e, xbc_e, mix_e))
    snapshot_states()
    mixer(proj_e, xbc_e, mix_e, _mod(2 * k_idx, chunks_per_seq) == 0, stage_items(odd, proj_o, xbc_o, mix_o))

    @pl.when((k_idx >= 1) & (_mod(2 * k_idx - 1, chunks_per_seq) == chunks_per_seq - 1))
    def _():
        write_states()


N_FRONT_INPUTS = 12
SAMPLE_POSITIONS_PER_STEP = 2


def _sample_front_kernel(seq, n_aliased, *refs):
    (h_ref, cst_ref, nw_ref, win_ref, cw_ref, cb_ref, dtb_ref, alog_ref, wgk_ref, bgk_ref,
     qnw_ref, knw_ref) = refs[:N_FRONT_INPUTS]
    (xs_ref, gates_ref, xt_ref, bt_ref, ct_ref, at_ref, qt_ref, kt_ref, egt_ref, vt_ref,
     qn_ref, kn_ref, vn_ref, conv_ref, u_s, xbc_s, proj_s) = refs[N_FRONT_INPUTS + n_aliased:]
    n_seq = BLK
    per_step = h_ref.shape[0] // n_seq
    step = pl.program_id(0)
    ht = h_ref[...]
    ms = jnp.mean(ht * ht, axis=-1, keepdims=True)
    u_s[...] = (ht * lax.rsqrt(ms + EPS) * nw_ref[...]).astype(BF16)
    xbc_s[pl.ds(pl.multiple_of(step * per_step * n_seq, n_seq), per_step * n_seq), :] = jnp.dot(
        u_s[...], win_ref[:, :XBC_W], preferred_element_type=F32)
    proj_s[...] = jnp.dot(u_s[...], win_ref[:, XBC_W:], preferred_element_type=F32)
    cw = cw_ref[...]
    a_row = jnp.where(_iota((1, LANES), 1) < SSD_HEADS, -jnp.exp(alog_ref[...]), 0.0)
    for i in range(per_step):
        t = step * per_step + i
        rows = slice(i * n_seq, (i + 1) * n_seq)

        def raw_xbc(back):
            cur = xbc_s[pl.ds(pl.multiple_of(jnp.maximum(t - back, 0) * n_seq, n_seq), n_seq), :]
            if back == 0:
                return cur
            old = cst_ref[0, jnp.clip(SSD_CONV - 1 + t - back, 0, SSD_CONV - 2)]
            return jnp.where(t >= back, cur, old)

        acc = raw_xbc(SSD_CONV - 1) * cw[0:1, :]
        for k in range(1, SSD_CONV):
            acc = acc + raw_xbc(SSD_CONV - 1 - k) * cw[k:k + 1, :]
        xbc_c = _silu(acc + cb_ref[...])
        xs = xbc_c[:, :SSD_WIDTH]
        dtlr = proj_s[rows, P_DTLR:P_DTLR + LANES]
        dtv = _softplus(dtlr + dtb_ref[...])
        xs_ref[rows, :] = xs
        xt_ref[i] = (xs * _expand_heads(dtv, SSD_HEADS)).T
        bt_ref[i] = xbc_c[:, SSD_WIDTH:SSD_WIDTH + LANES].T
        ct_ref[i] = xbc_c[:, SSD_WIDTH + LANES:].T
        at_ref[i] = jnp.exp(dtv * a_row).T[:SSD_HEADS, :]
        glog = _log_sigmoid(_dot(dtlr, wgk_ref[...]) + bgk_ref[...]) * (1.0 / GLA_GATE_NORM)
        qt_ref[i] = (proj_s[rows, P_GQ:P_GQ + LANES] * (GLA_DK ** -0.5)).T
        kt_ref[i] = proj_s[rows, P_GK:P_GK + LANES].T
        egt_ref[i] = jnp.exp(glog).T
        vt_ref[i] = proj_s[rows, P_GV:P_GV + GLA_WIDTH].T
        sq = proj_s[rows, P_SQ:P_SQ + SWA_WIDTH]
        qn_ref[rows, :] = sq * _head_rms_scale(sq) * qnw_ref[...] * (SWA_HEAD_DIM ** -0.5)
        sk = proj_s[rows, P_SK:P_SK + LANES]
        kn_ref[rows, :] = sk * _head_rms_scale(sk) * knw_ref[...]
        vn_ref[rows, :] = proj_s[rows, P_SV:P_SV + LANES]
        gates_ref[rows, :] = jnp.concatenate(
            [_silu(proj_s[rows, P_Z:P_Z + SSD_WIDTH]), _silu(proj_s[rows, P_GG:P_GG + GLA_WIDTH]),
             _silu(proj_s[rows, P_SG:P_SG + SWA_WIDTH])], axis=1)
        first_kept = seq - (SSD_CONV - 1)

        @pl.when(t >= first_kept)
        def _():
            conv_ref[0, jnp.maximum(t - first_kept, 0)] = raw_xbc(0)


N_STATE_INPUTS = 19
SEQ_PER_STEP = 16


def _sample_state_kernel(seq, n_aliased, *refs):
    (xt_ref, bt_ref, ct_ref, at_ref, qt_ref, kt_ref, egt_ref, vt_ref, qn_ref, kn_ref, vn_ref,
     ssm_ref, gla_ref, kc_ref, vc_ref, bucket_c_ref, bucket_n_ref, rel_ref, sink_ref) = refs[:N_STATE_INPUTS]
    (ssm_o, gla_o, ko_ref, vo_ref, yt_ref, ot_ref, oswa_ref,
     qa_s, qb_s, krow_s, vrow_s, oa_s, ob_s, biasc_s, biasn_s) = refs[N_STATE_INPUTS + n_aliased:]
    j = pl.program_id(0)
    n_seq = LANES
    head_of_row = _div(_iota((SWA_HEADS * seq, LANES), 0), seq)

    def by_head(values):
        out = values[SWA_HEADS - 1]
        for hh in range(SWA_HEADS - 2, -1, -1):
            out = jnp.where(head_of_row == hh, values[hh], out)
        return out

    @pl.when(j == 0)
    def _():
        biasc_s[...] = by_head(_build_bias(bucket_c_ref[...], rel_ref))
        biasn_s[...] = by_head(_build_bias(bucket_n_ref[...], rel_ref))

    sub = _iota((SUBLANES, LANES), 0)
    a_rows = [jnp.sum(jnp.where(sub == j, at_ref[t], 0.0), axis=0, keepdims=True) for t in range(seq)]

    def ssd_body(p8, carry):
        r8 = pl.multiple_of(p8 * SUBLANES, SUBLANES)
        x_tiles = [xt_ref[t, pl.ds(r8, SUBLANES), :] for t in range(seq)]
        y_rows = [[] for _ in range(seq)]
        for pp in range(SUBLANES):
            r64 = pl.multiple_of((p8 * SUBLANES + pp) * SSD_STATE, SSD_STATE)
            slab = ssm_ref[0, 0, pl.ds(r64, SSD_STATE), :]
            for t in range(seq):
                slab = slab * a_rows[t] + x_tiles[t][pp:pp + 1, :] * bt_ref[t]
                y_rows[t].append(jnp.sum(ct_ref[t] * slab, axis=0, keepdims=True))
            ssm_o[0, 0, pl.ds(r64, SSD_STATE), :] = slab
        for t in range(seq):
            yt_ref[t, pl.ds(r8, SUBLANES), :] = jnp.concatenate(y_rows[t], axis=0)
        return carry

    lax.fori_loop(0, SSD_HEAD_DIM // SUBLANES, ssd_body, 0)

    @pl.when(j < GLA_HEADS)
    def _():
        for t in range(seq):
            ot_ref[t] = jnp.zeros((GLA_DV, LANES), F32)

        def gla_body(d8, carry):
            r8 = pl.multiple_of(d8 * SUBLANES, SUBLANES)
            q_tiles = [qt_ref[t, pl.ds(r8, SUBLANES), :] for t in range(seq)]
            k_tiles = [kt_ref[t, pl.ds(r8, SUBLANES), :] for t in range(seq)]
            g_tiles = [egt_ref[t, pl.ds(r8, SUBLANES), :] for t in range(seq)]
            for dd in range(SUBLANES):
                r64 = pl.multiple_of((d8 * SUBLANES + dd) * GLA_DV, GLA_DV)
                slab = gla_ref[0, 0, pl.ds(r64, GLA_DV), :]
                for t in range(seq):
                    slab = slab * g_tiles[t][dd:dd + 1, :] + k_tiles[t][dd:dd + 1, :] * vt_ref[t]
                    ot_ref[t] = ot_ref[t] + q_tiles[t][dd:dd + 1, :] * slab
                gla_o[0, 0, pl.ds(r64, GLA_DV), :] = slab
            return carry

        lax.fori_loop(0, GLA_DK // SUBLANES, gla_body, 0)

    base = pl.multiple_of(j * SEQ_PER_STEP, SEQ_PER_STEP)
    for t in range(seq):
        src = pl.ds(t * n_seq + base, SEQ_PER_STEP)
        dst = pl.ds(t, SEQ_PER_STEP, stride=seq)
        qa_s[dst, :] = qn_ref[src, 0:LANES]
        qb_s[dst, :] = qn_ref[src, LANES:2 * LANES]
        krow_s[dst, :] = kn_ref[src, :]
        vrow_s[dst, :] = vn_ref[src, :]
    kn_t = krow_s[...].T
    vn_t = vrow_s[...].T
    keep_old = _iota((LANES, WINDOW), 1) < WINDOW - seq
    lo8 = _iota((seq, LANES), 1) < HALF
    sink_col = by_head([jnp.full((SWA_HEADS * seq, LANES), sink_ref[hh], F32) for hh in range(SWA_HEADS)])[:, 0:1]

    def swa_stages(bl):
        r8 = pl.multiple_of(bl * seq, seq)
        v = {}

        def logits():
            qa = qa_s[pl.ds(r8, seq), :]
            qb = qb_s[pl.ds(r8, seq), :]
            qs = jnp.concatenate([jnp.where(lo8, qa, 0.0), jnp.where(lo8, qb, 0.0),
                                  jnp.where(lo8, 0.0, qa), jnp.where(lo8, 0.0, qb)], axis=0)
            v["lc"] = _dot(qs, kc_ref[0, bl]) + biasc_s[...]
            v["ln"] = _dot_nt(qs, krow_s[pl.ds(r8, seq), :]) + biasn_s[:, 0:seq]

        def softmax():
            lc, ln = v["lc"], v["ln"]
            m = jnp.maximum(jnp.maximum(jnp.max(lc, axis=-1, keepdims=True), jnp.max(ln, axis=-1, keepdims=True)),
                            sink_col)
            v["ec"] = jnp.exp(lc - m)
            v["en"] = jnp.exp(ln - m)
            v["inv"] = 1.0 / (jnp.sum(v["ec"], axis=-1, keepdims=True) + jnp.sum(v["en"], axis=-1, keepdims=True)
                              + jnp.exp(sink_col - m))

        def values():
            o = (_dot_nt(v["ec"], vc_ref[0, bl]) + _dot(v["en"], vrow_s[pl.ds(r8, seq), :])) * v["inv"]
            oa_s[pl.ds(r8, seq), :] = jnp.where(lo8, o[0:seq], o[2 * seq:3 * seq])
            ob_s[pl.ds(r8, seq), :] = jnp.where(lo8, o[seq:2 * seq], o[3 * seq:4 * seq])

        def window():
            ko_ref[0, bl] = jnp.where(keep_old, pltpu.roll(kc_ref[0, bl], WINDOW - seq, axis=1),
                                      pltpu.roll(kn_t, WINDOW - seq - r8, axis=1))
            vo_ref[0, bl] = jnp.where(keep_old, pltpu.roll(vc_ref[0, bl], WINDOW - seq, axis=1),
                                      pltpu.roll(vn_t, WINDOW - seq - r8, axis=1))

        return [logits, softmax, values, window]

    all_stages = [swa_stages(bl) for bl in range(SEQ_PER_STEP)]
    for k in range(len(all_stages[0])):
        for stages in all_stages:
            stages[k]()
    for t in range(seq):
        src = pl.ds(t, SEQ_PER_STEP, stride=seq)
        oswa_ref[t] = jnp.concatenate([oa_s[src, :], ob_s[src, :]], axis=1)


def _sample_back_kernel(yt_ref, ot_ref, oswa_ref, xs_ref, gates_ref, h_ref, p_ref, dsk_ref, snw_ref, gnw_ref,
                        wout_ref, wpe_ref, wpg_ref, y_ref, mix_s):
    n_seq = BLK
    for t in range(yt_ref.shape[0]):
        rows = slice(t * n_seq, (t + 1) * n_seq)
        y = (yt_ref[t].T + dsk_ref[...] * xs_ref[rows, :]) * gates_ref[rows, 0:SSD_WIDTH]
        mix_s[rows, 0:SSD_WIDTH] = _group_rmsnorm(y, snw_ref[...]).astype(BF16)
        o = ot_ref[t].T
        y_gla = o * _head_rms_scale(o) * gnw_ref[...] * gates_ref[rows, SSD_WIDTH:SSD_WIDTH + GLA_WIDTH]
        mix_s[rows, SSD_WIDTH:SSD_WIDTH + GLA_WIDTH] = y_gla.astype(BF16)
        mix_s[rows, SSD_WIDTH + GLA_WIDTH:] = (oswa_ref[t] * gates_ref[rows, SSD_WIDTH + GLA_WIDTH:]).astype(BF16)
    y_ref[...] = _epilogue(h_ref[...], mix_s[...], p_ref[0], wout_ref, wpg_ref, wpe_ref)


def _const_spec(shape):
    nd = len(shape)
    return pl.BlockSpec(shape, lambda *_: (0,) * nd)


def _smem_spec():
    return pl.BlockSpec(memory_space=pltpu.SMEM)


def _layer_spec(arr, layer):
    return pl.BlockSpec((None,) + arr.shape[1:], lambda *_: (layer, 0, 0), pipeline_mode=pl.Buffered(1))


def _layer_weights(layer, w):
    ops = [w[name] for name in ("norm_w", "w_in", "conv_w", "conv_b", "dt_bias", "a_log", "d_skip", "ssd_norm_w",
                                "gla_w_gk", "gla_b_gk", "gla_norm_w", "q_norm_w", "k_norm_w", "w_out", "w_pe",
                                "w_pg")]
    return ops, [_layer_spec(o, layer) for o in ops]


def _prompt_layer(layer, depth, bsz, h, p_all, prev_states, bucket, rel, sinks, wops, wspecs):
    rows_total, _ = h.shape
    seq_len = rows_total // bsz
    chunk = PROMPT_CHUNK_ROWS
    pair = 2 * chunk
    chunks_per_seq = seq_len // chunk
    assert seq_len % pair == 0 and chunks_per_seq & (chunks_per_seq - 1) == 0
    n_pairs = rows_total // pair
    kern = functools.partial(_prompt_kernel, chunks_per_seq, len(prev_states))
    proj_rows = pl.BlockSpec((pair, D_MODEL), lambda k: (jnp.minimum(k, n_pairs - 1), 0))
    out_rows = pl.BlockSpec((pair, D_MODEL), lambda k: (jnp.maximum(k - 1, 0), 0))
    p_spec = pl.BlockSpec((1, pair, PLE_DIM), lambda k: (layer, jnp.maximum(k - 1, 0), 0))
    per_seq = lambda s: pl.BlockSpec(
        (1, 1) + s, lambda k: (layer, jnp.maximum(2 * k - 1, 0) // chunks_per_seq) + (0,) * len(s))
    state_shapes = ((SSD_WIDTH, SSD_STATE), (SSD_CONV - 1, SSD_CONV_DIM), (GLA_HEADS * GLA_DK, GLA_DV),
                    (LANES, WINDOW), (LANES, WINDOW))
    out_shape = (jax.ShapeDtypeStruct((rows_total, D_MODEL), F32),) + tuple(
        jax.ShapeDtypeStruct((depth, bsz) + s, F32) for s in state_shapes)
    return pl.pallas_call(
        kern,
        grid=(n_pairs + 1,),
        in_specs=[proj_rows, out_rows, p_spec, _const_spec(bucket.shape), _smem_spec(), _smem_spec()]
        + wspecs + [pl.BlockSpec(memory_space=pl.ANY)] * len(prev_states),
        out_specs=(out_rows,) + tuple(per_seq(s) for s in state_shapes),
        out_shape=out_shape,
        input_output_aliases={N_PROMPT_INPUTS + k: 1 + k for k in range(len(prev_states))},
        scratch_shapes=[
            pltpu.VMEM((chunk, PROJ_W), F32), pltpu.VMEM((chunk, PROJ_W), F32),
            pltpu.VMEM((chunk, XBC_W), F32), pltpu.VMEM((chunk, XBC_W), F32),
            pltpu.VMEM((chunk, D_MODEL), BF16), pltpu.VMEM((chunk, D_MODEL), BF16),
            pltpu.VMEM((chunk, D_MODEL), BF16),
            pltpu.VMEM((chunk, D_MODEL), F32),
            pltpu.VMEM((chunk, D_MODEL), BF16),
            pltpu.VMEM((SUBLANES, XBC_W), F32),
            pltpu.VMEM((BLK, SSD_WIDTH), F32),
            pltpu.VMEM((LANES, GLA_WIDTH), F32),
            pltpu.VMEM((2 * BLK, LANES), F32),
            pltpu.VMEM((2 * BLK, LANES), F32),
            pltpu.VMEM((2 * SWA_HEADS, BLK, 2 * BLK), F32),
            pltpu.VMEM((SUBLANES, XBC_W), F32), pltpu.VMEM((BLK, SSD_WIDTH), F32),
            pltpu.VMEM((LANES, GLA_WIDTH), F32), pltpu.VMEM((BLK, LANES), F32),
            pltpu.VMEM((BLK, LANES), F32),
        ],
        compiler_params=pltpu.CompilerParams(
            dimension_semantics=("arbitrary",), vmem_limit_bytes=VMEM_LIMIT_BYTES),
        name="prompt_layer",
    )(h, h, p_all, bucket, rel, sinks, *wops, *prev_states)


def _whole(shape, layer=None):
    if layer is None:
        return pl.BlockSpec(shape, lambda *_: (0,) * len(shape), pipeline_mode=pl.Buffered(1))
    return pl.BlockSpec((1,) + shape[1:], lambda *_: (layer,) + (0,) * (len(shape) - 1),
                        pipeline_mode=pl.Buffered(1))


def _sample_layer_native(layer, depth, seq, h, p_all, conv_in, ssm_in, gla_in, kc_in, vc_in, prev_states,
                         buckets, rel, sinks, wops):
    (nw, win, cw, cb, dtb, alog, dsk, snw, wgk, bgk, gnw, qnw, knw, wout, wpe, wpg) = wops
    rows = h.shape[0]
    n_seq = rows // seq
    assert n_seq == LANES and n_seq % SEQ_PER_STEP == 0 and SSD_HEADS * SEQ_PER_STEP == n_seq
    prev_conv, prev_rest = (prev_states[:1], prev_states[1:]) if prev_states else ((), ())
    f32 = lambda *s: jax.ShapeDtypeStruct(s, F32)
    cparams = lambda sem: pltpu.CompilerParams(dimension_semantics=sem, vmem_limit_bytes=VMEM_LIMIT_BYTES)

    per_step = SAMPLE_POSITIONS_PER_STEP
    step_rows = per_step * n_seq
    assert seq % per_step == 0
    row_blk = lambda w: pl.BlockSpec((step_rows, w), lambda s: (s, 0))
    pos_blk = lambda n: pl.BlockSpec((per_step, n, n_seq), lambda s: (s, 0, 0))
    front_in = [h, conv_in, nw, win, cw, cb, dtb, alog, wgk, bgk, qnw, knw]
    front_specs = [row_blk(D_MODEL), _whole(conv_in.shape, layer)] + [_layer_spec(a, layer) for a in front_in[2:]]
    front_out = (f32(rows, SSD_WIDTH), f32(rows, D_MODEL),
                 f32(seq, SSD_WIDTH, n_seq), f32(seq, LANES, n_seq), f32(seq, LANES, n_seq),
                 f32(seq, SSD_HEADS, n_seq), f32(seq, LANES, n_seq), f32(seq, LANES, n_seq), f32(seq, LANES, n_seq),
                 f32(seq, GLA_WIDTH, n_seq), f32(rows, SWA_WIDTH), f32(rows, LANES), f32(rows, LANES),
                 f32(*conv_in.shape))
    front_out_specs = (row_blk(SSD_WIDTH), row_blk(D_MODEL), pos_blk(SSD_WIDTH), pos_blk(LANES), pos_blk(LANES),
                       pos_blk(SSD_HEADS), pos_blk(LANES), pos_blk(LANES), pos_blk(LANES), pos_blk(GLA_WIDTH),
                       row_blk(SWA_WIDTH), row_blk(LANES), row_blk(LANES), _whole(conv_in.shape, layer))
    (xs, gates, xt, bt, ct, at, qt, kt, egt, vt, qn, kn, vn, conv_o) = pl.pallas_call(
        functools.partial(_sample_front_kernel, seq, len(prev_conv)),
        grid=(seq // per_step,),
        in_specs=front_specs + [pl.BlockSpec(memory_space=pl.ANY)] * len(prev_conv),
        out_specs=front_out_specs,
        out_shape=front_out,
        input_output_aliases={len(front_in) + k: len(front_out) - 1 + k for k in range(len(prev_conv))},
        scratch_shapes=[pltpu.VMEM((step_rows, D_MODEL), BF16), pltpu.VMEM((rows, XBC_W), F32),
                        pltpu.VMEM((step_rows, PROJ_W), F32)],
        compiler_params=cparams(("arbitrary",)),
        name="sample_front",
    )(*front_in, *prev_conv)

    n_steps = SSD_HEADS
    per_group = SSD_HEADS // SSD_GROUPS
    gla_head = lambda j: jnp.minimum(j, GLA_HEADS - 1)
    blk3 = lambda n, f: pl.BlockSpec((seq, n, n_seq), lambda j: (0, f(j), 0))
    state_in = [xt, bt, ct, at, qt, kt, egt, vt, qn, kn, vn, ssm_in, gla_in, kc_in, vc_in, buckets[0], buckets[1],
                rel, sinks]
    ssm_spec = pl.BlockSpec((1, 1) + ssm_in.shape[2:], lambda j: (layer, j, 0, 0))
    gla_spec = pl.BlockSpec((1, 1) + gla_in.shape[2:], lambda j: (layer, gla_head(j), 0, 0))
    kv_spec = pl.BlockSpec((1, SEQ_PER_STEP) + kc_in.shape[2:], lambda j: (layer, j, 0, 0))
    state_specs = [blk3(SSD_HEAD_DIM, lambda j: j), blk3(SSD_STATE, lambda j: j // per_group),
                   blk3(SSD_STATE, lambda j: j // per_group), _const_spec(at.shape),
                   blk3(GLA_DK, gla_head), blk3(GLA_DK, gla_head), blk3(GLA_DK, gla_head), blk3(GLA_DV, gla_head),
                   _const_spec(qn.shape), _const_spec(kn.shape), _const_spec(vn.shape),
                   ssm_spec, gla_spec, kv_spec, kv_spec,
                   _const_spec(buckets[0].shape), _const_spec(buckets[1].shape), _smem_spec(), _smem_spec()]
    state_out = (f32(*ssm_in.shape), f32(*gla_in.shape), f32(*kc_in.shape), f32(*vc_in.shape),
                 f32(seq, SSD_WIDTH, n_seq), f32(seq, GLA_WIDTH, n_seq), f32(seq, n_seq, SWA_WIDTH))
    ssm_o, gla_o, ko, vo, yt, ot, oswa = pl.pallas_call(
        functools.partial(_sample_state_kernel, seq, len(prev_rest)),
        grid=(n_steps,),
        in_specs=state_specs + [pl.BlockSpec(memory_space=pl.ANY)] * len(prev_rest),
        out_specs=(ssm_spec, gla_spec, kv_spec, kv_spec, blk3(SSD_HEAD_DIM, lambda j: j), blk3(GLA_DV, gla_head),
                   pl.BlockSpec((seq, SEQ_PER_STEP, SWA_WIDTH), lambda j: (0, j, 0))),
        out_shape=state_out,
        input_output_aliases={len(state_in) + k: k for k in range(len(prev_rest))},
        scratch_shapes=[pltpu.VMEM((SEQ_PER_STEP * seq, LANES), F32)] * 6
        + [pltpu.VMEM((SWA_HEADS * seq, LANES), F32)] * 2,
        compiler_params=cparams(("arbitrary",)),
        name="sample_state",
    )(*state_in, *prev_rest)

    back_in = [yt, ot, oswa, xs, gates, h, p_all, dsk, snw, gnw, wout, wpe, wpg]
    back_specs = [pos_blk(SSD_WIDTH), pos_blk(GLA_WIDTH),
                  pl.BlockSpec((per_step, n_seq, SWA_WIDTH), lambda s: (s, 0, 0)),
                  row_blk(SSD_WIDTH), row_blk(D_MODEL), row_blk(D_MODEL),
                  pl.BlockSpec((1, step_rows, PLE_DIM), lambda s: (layer, s, 0))] + [
        _layer_spec(a, layer) for a in back_in[7:]]
    y = pl.pallas_call(
        _sample_back_kernel,
        grid=(seq // per_step,),
        in_specs=back_specs,
        out_specs=row_blk(D_MODEL),
        out_shape=f32(rows, D_MODEL),
        scratch_shapes=[pltpu.VMEM((step_rows, D_MODEL), BF16)],
        compiler_params=cparams(("arbitrary",)),
        name="sample_back",
    )(*back_in)
    return y, (conv_o, ssm_o, gla_o, ko, vo)


SWA_HEAD_ORDER = (0, 2, 1, 3)


def _win_tile_runs():
    sizes = (SSD_WIDTH, SSD_CONV_DIM, SSD_HEADS, GLA_HEADS * GLA_DK, GLA_HEADS * GLA_DK, GLA_WIDTH, GLA_WIDTH,
             GLA_RANK, SWA_WIDTH, SWA_KV_HEADS * SWA_HEAD_DIM, SWA_KV_HEADS * SWA_HEAD_DIM, SWA_WIDTH)
    offs = np.concatenate([[0], np.cumsum(sizes)])
    seg = lambda k: np.arange(offs[k], offs[k + 1])
    z, xbc, dt, gq, gk, gv, gg, glr, sq, sk, sv, sg = [seg(k) for k in range(len(sizes))]
    heads = lambda a: np.concatenate([a[h * SWA_HEAD_DIM:(h + 1) * SWA_HEAD_DIM] for h in SWA_HEAD_ORDER])
    pad = np.full(LANES - SSD_HEADS - GLA_RANK, -1)
    src = np.concatenate([xbc, z, gq, gk, gv, gg, heads(sq), sk, sv, heads(sg), dt, glr, pad])
    assert src.size == XBC_W + PROJ_W
    tiles = []
    for j in range(src.size // LANES):
        idx = src[j * LANES:(j + 1) * LANES]
        cuts = [0] + [k for k in range(1, LANES) if (idx[k] != idx[k - 1] + 1 and not (idx[k] == -1 == idx[k - 1]))]
        runs = [(int(idx[a]), b - a) for a, b in zip(cuts, cuts[1:] + [LANES])]
        assert all(n % SUBLANES == 0 and (s < 0 or s % SUBLANES == 0) for s, n in runs)
        tiles.append(runs)
    return tiles


WEIGHT_PREP_COLS = 2 * LANES


def _win_prep_kernel(tile_runs, wt_ref, out_ref):
    for j, runs in enumerate(tile_runs):
        parts = [jnp.zeros((n, wt_ref.shape[2]), F32) if s < 0 else wt_ref[0, s:s + n, :] for s, n in runs]
        tile = parts[0] if len(parts) == 1 else jnp.concatenate(parts, axis=0)
        out_ref[0, :, j * LANES:(j + 1) * LANES] = tile.T.astype(BF16)


def _prepare_w_in(w_in):
    depth, d_model, d_in = w_in.shape
    w_t = jnp.swapaxes(w_in, 1, 2)
    assert d_model % WEIGHT_PREP_COLS == 0
    return pl.pallas_call(
        functools.partial(_win_prep_kernel, _win_tile_runs()),
        grid=(depth, d_model // WEIGHT_PREP_COLS),
        in_specs=[pl.BlockSpec((1, d_in, WEIGHT_PREP_COLS), lambda l, c: (l, 0, c))],
        out_specs=pl.BlockSpec((1, WEIGHT_PREP_COLS, XBC_W + PROJ_W), lambda l, c: (l, c, 0)),
        out_shape=jax.ShapeDtypeStruct((depth, d_model, XBC_W + PROJ_W), BF16),
        compiler_params=pltpu.CompilerParams(
            dimension_semantics=("arbitrary", "arbitrary"), vmem_limit_bytes=VMEM_LIMIT_BYTES),
        name="w_in_prep",
    )(w_t)


def _cast_prep_kernel(wout_ref, wpe_ref, wpg_ref, wout_o, wpe_o, wpg_o):
    mix_w = SSD_WIDTH + GLA_WIDTH
    wout_o[0, 0:mix_w, :] = wout_ref[0, 0:mix_w, :].astype(BF16)
    for slot, head in enumerate(SWA_HEAD_ORDER):
        src = slice(mix_w + head * SWA_HEAD_DIM, mix_w + (head + 1) * SWA_HEAD_DIM)
        dst = slice(mix_w + slot * SWA_HEAD_DIM, mix_w + (slot + 1) * SWA_HEAD_DIM)
        wout_o[0, dst, :] = wout_ref[0, src, :].astype(BF16)
    wpe_o[0] = wpe_ref[0].astype(BF16)
    wpg_o[0] = wpg_ref[0].astype(BF16)


def _prepare_out_weights(w_out, w_pe, w_pg):
    depth = w_out.shape[0]
    ops = (w_out, w_pe, w_pg)
    assert all(a.shape[2] == D_MODEL for a in ops)
    spec = lambda a: pl.BlockSpec((1, a.shape[1], WEIGHT_PREP_COLS), lambda l, c: (l, 0, c))
    return pl.pallas_call(
        _cast_prep_kernel,
        grid=(depth, D_MODEL // WEIGHT_PREP_COLS),
        in_specs=[spec(a) for a in ops],
        out_specs=tuple(spec(a) for a in ops),
        out_shape=tuple(jax.ShapeDtypeStruct(a.shape, BF16) for a in ops),
        compiler_params=pltpu.CompilerParams(
            dimension_semantics=("arbitrary", "arbitrary"), vmem_limit_bytes=VMEM_LIMIT_BYTES),
        name="w_out_prep",
    )(*ops)


def _prepare_weights(norm_w, w_in, conv_w, conv_b, dt_bias, a_log, d_skip, ssd_norm_w, gla_w_gk, gla_b_gk,
                     gla_norm_w, q_norm_w, k_norm_w, w_out, w_pe, w_pg):
    w_out_p, w_pe_p, w_pg_p = _prepare_out_weights(w_out, w_pe, w_pg)
    lane_pad = lambda x: jnp.pad(x, ((0, 0), (0, LANES - x.shape[-1])))[:, None, :]
    wgk_p = jnp.pad(gla_w_gk, ((0, 0), (LR_LANE0, LANES - LR_LANE0 - GLA_RANK), (0, 0))).astype(BF16)
    return dict(
        norm_w=norm_w[:, None, :], w_in=_prepare_w_in(w_in), conv_w=conv_w, conv_b=conv_b[:, None, :],
        dt_bias=lane_pad(dt_bias), a_log=lane_pad(a_log),
        d_skip=jnp.repeat(d_skip, SSD_HEAD_DIM, axis=-1)[:, None, :], ssd_norm_w=ssd_norm_w[:, None, :],
        gla_w_gk=wgk_p, gla_b_gk=gla_b_gk[:, None, :],
        gla_norm_w=jnp.tile(gla_norm_w, (1, GLA_HEADS))[:, None, :],
        q_norm_w=jnp.tile(q_norm_w, (1, SWA_HEADS))[:, None, :],
        k_norm_w=jnp.tile(k_norm_w, (1, SWA_KV_HEADS))[:, None, :],
        w_out=w_out_p, w_pe=w_pe_p, w_pg=w_pg_p)


def kernel(x_prompt, x_sample, state_ssm, state_conv, state_gla, cache_swa_k, cache_swa_v, p_prompt, p_sample, rel_bias, norm_w, w_in, conv_w, conv_b, dt_bias, a_log, d_skip, ssd_norm_w, gla_w_gk, gla_b_gk, gla_norm_w, q_norm_w, k_norm_w, attn_sinks, w_out, w_pe, w_pg):
    depth = w_in.shape[0]
    bp, seq_p, _ = x_prompt.shape
    bs, seq_s, _ = x_sample.shape
    assert seq_s == SUBLANES and BLK % seq_s == 0 and (bs * seq_s) % BLK == 0
    assert cache_swa_k.shape[2] == WINDOW
    w = _prepare_weights(norm_w, w_in, conv_w, conv_b, dt_bias, a_log, d_skip, ssd_norm_w, gla_w_gk, gla_b_gk,
                         gla_norm_w, q_norm_w, k_norm_w, w_out, w_pe, w_pg)
    rel_flat = rel_bias.reshape(-1)
    dist_p = WINDOW + np.arange(BLK)[:, None] - np.arange(2 * BLK)[None, :]
    bucket_p = jnp.asarray(_bucket_table(dist_p))
    t_of_row = np.tile(np.arange(seq_s), SWA_HEADS)[:, None]
    bucket_c = jnp.asarray(_bucket_table(WINDOW + t_of_row - np.arange(WINDOW)[None, :]))
    dist_n = np.where(np.arange(LANES)[None, :] < seq_s, t_of_row - np.arange(LANES)[None, :], -1)
    bucket_n = jnp.asarray(_bucket_table(dist_n))

    ssm_in = jnp.transpose(state_ssm, (0, 2, 3, 4, 1)).reshape(depth, SSD_HEADS, SSD_HEAD_DIM * SSD_STATE, bs)
    gla_in = jnp.transpose(state_gla, (0, 2, 3, 4, 1)).reshape(depth, GLA_HEADS, GLA_DK * GLA_DV, bs)
    kv_in = lambda a: jnp.transpose(a, (0, 1, 3, 4, 2)).reshape(depth, bs, SWA_KV_HEADS * SWA_HEAD_DIM, WINDOW)
    kc_in, vc_in = kv_in(cache_swa_k), kv_in(cache_swa_v)
    conv_in = jnp.transpose(state_conv, (0, 2, 1, 3))

    hp = x_prompt.reshape(bp * seq_p, D_MODEL)
    p_prompt_rows = p_prompt.reshape(depth, bp * seq_p, PLE_DIM)
    hs = jnp.transpose(x_sample, (1, 0, 2)).reshape(seq_s * bs, D_MODEL)
    p_sample_rows = jnp.transpose(p_sample, (0, 2, 1, 3)).reshape(depth, seq_s * bs, PLE_DIM)
    states_p = ()
    states_s = ()
    for i in range(depth):
        wops, wspecs = _layer_weights(i, w)
        hp, *states_p = _prompt_layer(i, depth, bp, hp, p_prompt_rows, tuple(states_p), bucket_p, rel_flat,
                                      attn_sinks[i], wops, wspecs)
        hs, states_s = _sample_layer_native(i, depth, seq_s, hs, p_sample_rows, conv_in, ssm_in, gla_in, kc_in,
                                            vc_in, states_s, (bucket_c, bucket_n), rel_flat, attn_sinks[i], wops)
    ssm_p, conv_p, gla_p, kt_p, vt_p = states_p
    conv_s, ssm_s, gla_s, kt_s, vt_s = states_s
    unpack_kv = lambda a: jnp.transpose(
        a.reshape(a.shape[:2] + (SWA_KV_HEADS, SWA_HEAD_DIM, WINDOW)), (0, 1, 4, 2, 3))
    outs_p = (ssm_p.reshape(depth, bp, SSD_HEADS, SSD_HEAD_DIM, SSD_STATE), conv_p,
              gla_p.reshape(depth, bp, GLA_HEADS, GLA_DK, GLA_DV), unpack_kv(kt_p), unpack_kv(vt_p))
    seq_last = lambda a, dims: jnp.transpose(a.reshape(a.shape[:2] + dims + (bs,)), (0, 4, 1, 2, 3))
    outs_s = (seq_last(ssm_s, (SSD_HEAD_DIM, SSD_STATE)), jnp.transpose(conv_s, (0, 2, 1, 3)),
              seq_last(gla_s, (GLA_DK, GLA_DV)), unpack_kv(kt_s), unpack_kv(vt_s))
    y_sample = jnp.transpose(hs.reshape(seq_s, bs, D_MODEL), (1, 0, 2))
    return (hp.reshape(bp, seq_p, D_MODEL), y_sample) + outs_p + outs_s
```

```python
import functools
import math

import numpy as np
import jax
import jax.numpy as jnp
from jax import lax
from jax.experimental import pallas as pl
from jax.experimental.pallas import tpu as pltpu

D_MODEL = 1024
DEPTH = 2
SSD_HEADS = 8
SSD_HEAD_DIM = 64
SSD_WIDTH = SSD_HEADS * SSD_HEAD_DIM
SSD_GROUPS = 2
SSD_STATE = 64
SSD_CONV = 4
SSD_CONV_DIM = SSD_WIDTH + 2 * SSD_GROUPS * SSD_STATE
SSD_CHUNK = 128
GLA_HEADS = 4
GLA_DK = 32
GLA_DV = 64
GLA_WIDTH = GLA_HEADS * GLA_DV
GLA_RANK = 16
GLA_GATE_NORM = 16.0
GLA_CHUNK = 64
SWA_HEADS = 4
SWA_KV_HEADS = 2
SWA_HEAD_DIM = 64
SWA_WIDTH = SWA_HEADS * SWA_HEAD_DIM
WINDOW = 128
REL_BUCKETS = 32
REL_MAX_DIST = 128
PLE_DIM = 256
EPS = 1e-6

LANES = 128
SUBLANES = 8
HALF = LANES // 2
BLK = 128
VMEM_LIMIT_BYTES = 56 * 1024 * 1024

XBC_W = SSD_CONV_DIM
P_Z = 0
P_GQ = P_Z + SSD_WIDTH
P_GK = P_GQ + LANES
P_GV = P_GK + LANES
P_GG = P_GV + GLA_WIDTH
P_SQ = P_GG + GLA_WIDTH
P_SK = P_SQ + SWA_WIDTH
P_SV = P_SK + LANES
P_SG = P_SV + LANES
P_DTLR = P_SG + SWA_WIDTH
PROJ_W = P_DTLR + LANES
LR_LANE0 = SSD_HEADS

F32 = jnp.float32
BF16 = jnp.bfloat16
NEG_INF = float("-inf")
LOG2E = math.log2(math.e)
N_PROMPT_INPUTS = 22
PROMPT_CHUNK_ROWS = 2 * BLK
TICKS_PER_ITEM = 2
GATED_AFTER_PROJ_ITEMS = 9
NT_DIMS = (((1,), (1,)), ((), ()))


def _iota(shape, dim):
    return lax.broadcasted_iota(jnp.int32, shape, dim)


def _div(x, d):
    return x >> (d.bit_length() - 1)


def _mod(x, d):
    return x & (d - 1)


def _softplus(x):
    e = jnp.exp(-jnp.abs(x))
    u = 1.0 + e
    d = u - 1.0
    log1p_e = jnp.where(d == 0.0, e, jnp.log(u) * (e / jnp.where(d == 0.0, 1.0, d)))
    return jnp.maximum(x, 0.0) + log1p_e


def _log_sigmoid(x):
    return jnp.minimum(x, 0.0) - jnp.log(1.0 + jnp.exp(-jnp.abs(x)))


def _silu(x):
    return x * jax.nn.sigmoid(x)


def _dot(a, b):
    return jnp.dot(a.astype(BF16), b.astype(BF16), preferred_element_type=F32)


def _dot_nt(a, b):
    return lax.dot_general(a.astype(BF16), b.astype(BF16), NT_DIMS, preferred_element_type=F32)


def _dot_exact(sel, x):
    x1 = x.astype(BF16)
    r1 = x - x1.astype(F32)
    x2 = r1.astype(BF16)
    x3 = (r1 - x2.astype(F32)).astype(BF16)
    dot = functools.partial(jnp.dot, sel, preferred_element_type=F32)
    return dot(x1) + dot(x2) + dot(x3)


def _expand_heads(x, n_heads):
    rows = x.shape[0]
    lo = _iota((rows, LANES), 1) < HALF
    tiles = []
    for j in range(n_heads // 2):
        a = jnp.broadcast_to(x[:, 2 * j:2 * j + 1], (rows, LANES))
        b = jnp.broadcast_to(x[:, 2 * j + 1:2 * j + 2], (rows, LANES))
        tiles.append(jnp.where(lo, a, b))
    return jnp.concatenate(tiles, axis=1)


def _head_rms_scale(x):
    rows, width = x.shape
    lo = _iota((rows, LANES), 1) < HALF
    outs = []
    for j in range(width // LANES):
        t = x[:, j * LANES:(j + 1) * LANES]
        sq = t * t
        s_lo = jnp.sum(jnp.where(lo, sq, 0.0), axis=-1, keepdims=True)
        s_hi = jnp.sum(jnp.where(lo, 0.0, sq), axis=-1, keepdims=True)
        outs.append(lax.rsqrt(jnp.where(lo, s_lo, s_hi) * (1.0 / HALF) + EPS))
    return outs[0] if len(outs) == 1 else jnp.concatenate(outs, axis=1)


def _group_rmsnorm(y, w):
    gw = SSD_WIDTH // SSD_GROUPS
    outs = []
    for g in range(SSD_GROUPS):
        t = y[:, g * gw:(g + 1) * gw]
        ms = jnp.sum(t * t, axis=-1, keepdims=True) * (1.0 / gw)
        outs.append(t * lax.rsqrt(ms + EPS))
    return jnp.concatenate(outs, axis=1) * w


def _rel_bucket_np(dist):
    n = np.maximum(dist, 0)
    exact = REL_BUCKETS // 2
    nf = np.maximum(n, 1).astype(np.float64)
    large = exact + (np.log(nf / exact) / math.log(REL_MAX_DIST / exact) * (REL_BUCKETS - exact)).astype(np.int32)
    large = np.minimum(large, REL_BUCKETS - 1)
    return np.where(n < exact, n, large).astype(np.int32)


def _bucket_table(dist):
    return np.where((dist >= 0) & (dist < WINDOW), _rel_bucket_np(dist), -1).astype(np.int32)


def _no_tick():
    pass


def _ssd_intra(xbc_c, dtv, acum, pair_mask, tick=_no_tick):
    xs = xbc_c[:, :SSD_WIDTH]
    bm = xbc_c[:, SSD_WIDTH:SSD_WIDTH + LANES]
    cm = xbc_c[:, SSD_WIDTH + LANES:]
    lane = _iota((BLK, LANES), 1)
    lo = lane < HALF
    acum_t = acum.T
    eacum = jnp.exp2(acum)
    tail = jnp.exp2(acum[BLK - 1:BLK, :] - acum)
    dtv_e = _expand_heads(dtv, SSD_HEADS)
    eacum_e = _expand_heads(eacum, SSD_HEADS)
    tail_e = _expand_heads(tail, SSD_HEADS)
    tick()
    xdt = xs * dtv_e
    xw = xdt * tail_e
    cb = [_dot_nt(jnp.where(lo, cm, 0.0), bm), _dot_nt(jnp.where(lo, 0.0, cm), bm)]
    y_pairs = []
    for j in range(SSD_HEADS // 2):
        tick()
        g = (2 * j) // (SSD_HEADS // SSD_GROUPS)
        ms = []
        for k in range(2):
            h = 2 * j + k
            seg = acum[:, h:h + 1] - acum_t[h:h + 1, :]
            dec = jnp.where(pair_mask, jnp.exp2(seg), 0.0)
            ms.append((cb[g] * dec).astype(BF16))
        xp = xdt[:, j * LANES:(j + 1) * LANES]
        rhs = jnp.concatenate([jnp.where(lo, xp, 0.0), jnp.where(lo, 0.0, xp)], axis=0)
        y_pairs.append(_dot(jnp.concatenate(ms, axis=1), rhs))
    y_intra = jnp.concatenate(y_pairs, axis=1)
    return y_intra, xs, bm, cm, xw, eacum, eacum_e


def _gla_intra(gq, gk, gv, bcs, att_mask, tick=_no_tick):
    eb = jnp.exp2(bcs)
    qe = gq * (GLA_DK ** -0.5) * eb
    ke = gk * jnp.exp2(-bcs)
    btot = jnp.concatenate(
        [jnp.broadcast_to(bcs[(c2 + 1) * GLA_CHUNK - 1:(c2 + 1) * GLA_CHUNK, :], (GLA_CHUNK, LANES))
         for c2 in range(BLK // GLA_CHUNK)], axis=0)
    kd = gk * jnp.exp2(btot - bcs)
    lane_k = _iota((GLA_CHUNK, LANES), 1)
    lane_v = _iota((GLA_CHUNK, GLA_WIDTH), 1)
    outs = []
    for c2 in range(BLK // GLA_CHUNK):
        tick()
        rs = slice(c2 * GLA_CHUNK, (c2 + 1) * GLA_CHUNK)
        ke_c = ke[rs]
        v_c = gv[rs]
        kbd = jnp.concatenate(
            [jnp.where(_div(lane_k, GLA_DK) == h, ke_c, 0.0) for h in range(GLA_HEADS)], axis=0)
        att = _dot_nt(qe[rs], kbd)
        att = jnp.where(att_mask, att, 0.0)
        vbd = jnp.concatenate(
            [jnp.where(_div(lane_v, GLA_DV) == h, v_c, 0.0) for h in range(GLA_HEADS)], axis=0)
        outs.append(_dot(att, vbd))
    return jnp.concatenate(outs, axis=0), qe, kd, jnp.exp2(btot)


def _build_bias(bucket, rel_ref):
    accs = [jnp.full(bucket.shape, NEG_INF, F32) for _ in range(SWA_HEADS)]
    for b in range(REL_BUCKETS):
        hit = bucket == b
        for h in range(SWA_HEADS):
            accs[h] = jnp.where(hit, rel_ref[b * SWA_HEADS + h], accs[h])
    return accs


def _epilogue(h, mix, p, wout_ref, wpg_ref, wpe_ref):
    h1 = h + jnp.dot(mix, wout_ref[...], preferred_element_type=F32)
    gate = jax.nn.sigmoid(jnp.dot(h1.astype(BF16), wpg_ref[...], preferred_element_type=F32))
    pe = jnp.dot(p.astype(BF16), wpe_ref[...], preferred_element_type=F32)
    return h1 + gate * pe


def _prompt_kernel(chunks_per_seq, n_aliased, *refs):
    (ha_ref, hc_ref, p_ref, bucket_ref, rel_ref, sink_ref, nw_ref, win_ref, cw_ref, cb_ref, dtb_ref,
     alog_ref, dsk_ref, snw_ref, wgk_ref, bgk_ref, gnw_ref, qnw_ref, knw_ref, wout_ref,
     wpe_ref, wpg_ref) = refs[:N_PROMPT_INPUTS]
    (y_ref, ssm_ref, conv_ref, gla_ref, ko_ref, vo_ref,
     proj_e, proj_o, xbc_e, xbc_o, mix_e, mix_o, u_s, h1_s, h1b_s, hist_s, st_s, s2_s, kext_s, vext_s,
     bias_s, hist_snap, st_snap, s2_snap, k_snap, v_snap) = refs[N_PROMPT_INPUTS + n_aliased:]
    k_idx = pl.program_id(0)

    @pl.when(k_idx == 0)
    def _():
        accs = [a * LOG2E for a in _build_bias(bucket_ref[...], rel_ref)]
        own_block = _iota((BLK, 2 * BLK), 1) >= BLK
        for hh in range(SWA_HEADS):
            bias_s[hh] = accs[hh]
            bias_s[SWA_HEADS + hh] = jnp.where(own_block, accs[hh], NEG_INF)
        for ref in (proj_o, xbc_o, mix_e, mix_o, hist_s, st_s, s2_s, kext_s, vext_s):
            ref[...] = jnp.zeros(ref.shape, ref.dtype)

    row = _iota((BLK, BLK), 0)
    col = _iota((BLK, BLK), 1)
    causal = row >= col
    tri = jnp.where(causal, 1.0, 0.0).astype(BF16)
    lo = col < HALF
    lane_row = _iota((1, LANES), 1)
    a_row = jnp.where(lane_row < SSD_HEADS, -jnp.exp(alog_ref[...]) * LOG2E, 0.0)
    bd_mask = _div(_iota((LANES, GLA_WIDTH), 0), GLA_DK) == _div(_iota((LANES, GLA_WIDTH), 1), GLA_DV)
    att_t = _iota((GLA_CHUNK, GLA_WIDTH), 0)
    att_s = _mod(_iota((GLA_CHUNK, GLA_WIDTH), 1), GLA_CHUNK)
    att_mask = att_s <= att_t
    lo2 = _iota((2 * BLK, LANES), 1) < HALF

    group_w = SSD_WIDTH // SSD_GROUPS

    def block(blk, proj_s, xbc_s, mix_s, starts_sequence, tick):
        rows = slice(blk * BLK, (blk + 1) * BLK)
        cw = cw_ref[...]
        tick()
        if blk == 0:
            xwin = jnp.concatenate([hist_s[...], xbc_s[0:BLK, :]], axis=0)
        else:
            xwin = xbc_s[blk * BLK - SUBLANES:(blk + 1) * BLK, :]
        acc = xwin[SUBLANES - 3:SUBLANES - 3 + BLK, :] * cw[0:1, :]
        for k in range(1, SSD_CONV):
            acc = acc + xwin[SUBLANES - 3 + k:SUBLANES - 3 + k + BLK, :] * cw[k:k + 1, :]
        xbc_c = _silu(acc + cb_ref[...])
        tick()
        dtlr = proj_s[rows, P_DTLR:P_DTLR + LANES]
        dtv_t = _softplus((dtlr + dtb_ref[...]).T[0:SSD_HEADS, :])
        dtv = jnp.concatenate([dtv_t, jnp.zeros((LANES - SSD_HEADS, BLK), F32)], axis=0).T
        glog = _log_sigmoid(_dot(dtlr, wgk_ref[...]) + bgk_ref[...]) * (LOG2E / GLA_GATE_NORM)
        sums = _dot_exact(tri, jnp.concatenate([dtv * a_row, glog], axis=1))
        acum = sums[:, :LANES]
        gsum = sums[:, LANES:]
        bcs = gsum
        for c2 in range(1, BLK // GLA_CHUNK):
            before = gsum[c2 * GLA_CHUNK - 1:c2 * GLA_CHUNK, :]
            bcs = jnp.where(_div(row, GLA_CHUNK) == c2, gsum - before, bcs)
        tick()
        y_intra, xs, bm, cm, xw, eacum, eacum_e = _ssd_intra(xbc_c, dtv, acum, causal, tick)
        tick()
        st = st_s[...]
        y = y_intra + _dot(cm, st) * eacum_e + dsk_ref[...] * xs
        bm_t = bm.T
        for g in range(SSD_GROUPS):
            gr = slice(g * SSD_STATE, (g + 1) * SSD_STATE)
            gc = slice(g * group_w, (g + 1) * group_w)
            st_s[gr, gc] = st[gr, gc] * eacum_e[BLK - 1:BLK, gc] + _dot(bm_t[gr, :], xw[:, gc])
        tick()
        y = y * _silu(proj_s[rows, P_Z:P_Z + SSD_WIDTH])
        mix_s[rows, 0:SSD_WIDTH] = _group_rmsnorm(y, snw_ref[...]).astype(BF16)
        tick()
        gk = proj_s[rows, P_GK:P_GK + LANES]
        gv = proj_s[rows, P_GV:P_GV + GLA_WIDTH]
        o_intra, qe, kd, ebt = _gla_intra(proj_s[rows, P_GQ:P_GQ + LANES], gk, gv, bcs, att_mask, tick)
        kd_t = kd.T
        ebt_t = ebt.T
        s2 = s2_s[...]
        o_parts = []
        for c2 in range(BLK // GLA_CHUNK):
            tick()
            rs = slice(c2 * GLA_CHUNK, (c2 + 1) * GLA_CHUNK)
            o_parts.append(o_intra[rs] + _dot(qe[rs], s2))
            u2 = _dot(jnp.where(_div(col, GLA_CHUNK) == c2, kd_t, 0.0), gv)
            last = (c2 + 1) * GLA_CHUNK - 1
            s2 = s2 * ebt_t[:, last:last + 1] + jnp.where(bd_mask, u2, 0.0)
        s2_s[...] = s2
        o = jnp.concatenate(o_parts, axis=0)
        y_gla = o * _head_rms_scale(o) * gnw_ref[...] * _silu(proj_s[rows, P_GG:P_GG + GLA_WIDTH])
        mix_s[rows, SSD_WIDTH:SSD_WIDTH + GLA_WIDTH] = y_gla.astype(BF16)
        tick()
        sq = proj_s[rows, P_SQ:P_SQ + SWA_WIDTH]
        qn = sq * _head_rms_scale(sq) * qnw_ref[...] * (SWA_HEAD_DIM ** -0.5 * LOG2E)
        sk = proj_s[rows, P_SK:P_SK + LANES]
        kn = sk * _head_rms_scale(sk) * knw_ref[...]
        vn = proj_s[rows, P_SV:P_SV + LANES]
        kext_s[BLK:2 * BLK, :] = kn
        vext_s[BLK:2 * BLK, :] = vn
        kext = kext_s[...]
        vext = vext_s[...]
        qa = qn[:, :LANES]
        qb = qn[:, LANES:]
        qs = jnp.concatenate([jnp.where(lo, qa, 0.0), jnp.where(lo, qb, 0.0),
                              jnp.where(lo, 0.0, qa), jnp.where(lo, 0.0, qb)], axis=0)
        logits = _dot_nt(qs, kext)
        tick()
        if blk == 0 and starts_sequence is not False:
            bias_row0 = jnp.where(starts_sequence, SWA_HEADS, 0)
        else:
            bias_row0 = 0
        es = []
        invs = []
        for hh in range(SWA_HEADS):
            tick()
            sink = sink_ref[hh] * LOG2E
            l = logits[hh * BLK:(hh + 1) * BLK] + bias_s[bias_row0 + hh]
            m = jnp.maximum(jnp.max(l, axis=-1, keepdims=True), sink)
            e = jnp.exp2(l - m)
            den = jnp.sum(e, axis=-1, keepdims=True) + jnp.exp2(sink - m)
            es.append(e.astype(BF16))
            invs.append(1.0 / den)
        v_stack = jnp.concatenate([jnp.where(lo2, vext, 0.0), jnp.where(lo2, 0.0, vext)], axis=0)
        tile_a = _dot(jnp.concatenate([es[0], es[2]], axis=1), v_stack) * jnp.where(lo, invs[0], invs[2])
        tile_b = _dot(jnp.concatenate([es[1], es[3]], axis=1), v_stack) * jnp.where(lo, invs[1], invs[3])
        oa = jnp.concatenate([tile_a, tile_b], axis=1)
        y_swa = oa * _silu(proj_s[rows, P_SG:P_SG + SWA_WIDTH])
        mix_s[rows, SSD_WIDTH + GLA_WIDTH:] = y_swa.astype(BF16)
        kext_s[0:BLK, :] = kn
        vext_s[0:BLK, :] = vn

    chunk = proj_e.shape[0]

    def project_items(rows, proj_s, xbc_s):
        def norm():
            h = ha_ref[rows, :]
            ms = jnp.mean(h * h, axis=-1, keepdims=True)
            u_s[...] = (h * lax.rsqrt(ms + EPS) * nw_ref[...]).astype(BF16)

        def cols(dst, lo_c, hi_c, w_off):
            def item():
                dst[:, lo_c:hi_c] = jnp.dot(u_s[...], win_ref[:, w_off + lo_c:w_off + hi_c],
                                            preferred_element_type=F32)
            return item

        step = 2 * LANES
        items = [norm]
        items += [cols(xbc_s, c, min(c + step, XBC_W), 0) for c in range(0, XBC_W, step)]
        items += [cols(proj_s, c, min(c + step, PROJ_W), XBC_W) for c in range(0, PROJ_W, step)]
        return items

    def epilogue_items(rows, mix_s):
        half_w = 2 * LANES

        def residual(c):
            def item():
                h1 = hc_ref[rows, c:c + half_w] + jnp.dot(
                    mix_s[...], wout_ref[:, c:c + half_w], preferred_element_type=F32)
                h1_s[:, c:c + half_w] = h1
                h1b_s[:, c:c + half_w] = h1.astype(BF16)
            return item

        def gated(c):
            def item():
                gate = jax.nn.sigmoid(jnp.dot(h1b_s[...], wpg_ref[:, c:c + half_w], preferred_element_type=F32))
                pe = jnp.dot(p_ref[0, rows, :].astype(BF16), wpe_ref[:, c:c + half_w],
                             preferred_element_type=F32)
                y_ref[rows, c:c + half_w] = h1_s[:, c:c + half_w] + gate * pe
            return item

        col0 = range(0, D_MODEL, half_w)
        return [residual(c) for c in col0], [gated(c) for c in col0]

    def mixer(proj_s, xbc_s, mix_s, starts_sequence, items):
        if starts_sequence is not False:
            keep = jnp.where(starts_sequence, 0.0, 1.0)
            for ref in (hist_s, st_s, s2_s):
                ref[...] = ref[...] * keep
            kext_s[0:BLK, :] = kext_s[0:BLK, :] * keep
            vext_s[0:BLK, :] = vext_s[0:BLK, :] * keep
        queue = list(items)
        calls = [0]

        def tick():
            calls[0] += 1
            if queue and calls[0] % TICKS_PER_ITEM == 0:
                queue.pop(0)()

        for blk in range(chunk // BLK):
            block(blk, proj_s, xbc_s, mix_s, starts_sequence, tick)
        while queue:
            queue.pop(0)()
        hist_s[...] = xbc_s[chunk - SUBLANES:chunk, :]

    def snapshot_states():
        st_snap[...] = st_s[...]
        s2_snap[...] = s2_s[...]
        hist_snap[...] = hist_s[...]
        k_snap[...] = kext_s[0:BLK, :]
        v_snap[...] = vext_s[0:BLK, :]

    def write_states():
        st = st_snap[...]
        stc = st[:SSD_STATE] + st[SSD_STATE:]
        ssm_ref[0, 0] = jnp.concatenate([stc, stc], axis=0).T[:, :SSD_STATE]
        conv_ref[0, 0] = hist_snap[SUBLANES - (SSD_CONV - 1):SUBLANES, :]
        s2 = s2_snap[...]
        w = s2[:, :LANES] + s2[:, LANES:]
        gla_ref[0, 0] = w[:, :GLA_DV] + w[:, GLA_DV:]
        ko_ref[0, 0] = k_snap[...].T
        vo_ref[0, 0] = v_snap[...].T

    even = slice(0, chunk)
    odd = slice(chunk, 2 * chunk)
    def stage_items(rows, proj_w, xbc_w, mix_r):
        residuals, gateds = epilogue_items(rows, mix_r)
        proj = project_items(rows, proj_w, xbc_w)
        return residuals + proj[:GATED_AFTER_PROJ_ITEMS] + gateds + proj[GATED_AFTER_PROJ_ITEMS:]

    mixer(proj_o, xbc_o, mix_o, False, stage_items(even, proj_e, xbc_e, mix_e))
    snapshot_states()
    mixer(proj_e, xbc_e, mix_e, _mod(2 * k_idx, chunks_per_seq) == 0, stage_items(odd, proj_o, xbc_o, mix_o))

    @pl.when((k_idx >= 1) & (_mod(2 * k_idx - 1, chunks_per_seq) == chunks_per_seq - 1))
    def _():
        write_states()


N_FRONT_INPUTS = 12
SAMPLE_POSITIONS_PER_STEP = 2


def _sample_front_kernel(seq, n_aliased, *refs):
    (h_ref, cst_ref, nw_ref, win_ref, cw_ref, cb_ref, dtb_ref, alog_ref, wgk_ref, bgk_ref,
     qnw_ref, knw_ref) = refs[:N_FRONT_INPUTS]
    (xs_ref, gates_ref, xt_ref, bt_ref, ct_ref, at_ref, qt_ref, kt_ref, egt_ref, vt_ref,
     qn_ref, kn_ref, vn_ref, conv_ref, u_s, xbc_s, proj_s) = refs[N_FRONT_INPUTS + n_aliased:]
    n_seq = BLK
    per_step = h_ref.shape[0] // n_seq
    step = pl.program_id(0)
    ht = h_ref[...]
    ms = jnp.mean(ht * ht, axis=-1, keepdims=True)
    u_s[...] = (ht * lax.rsqrt(ms + EPS) * nw_ref[...]).astype(BF16)
    xbc_s[pl.ds(pl.multiple_of(step * per_step * n_seq, n_seq), per_step * n_seq), :] = jnp.dot(
        u_s[...], win_ref[:, :XBC_W], preferred_element_type=F32)
    proj_s[...] = jnp.dot(u_s[...], win_ref[:, XBC_W:], preferred_element_type=F32)
    cw = cw_ref[...]
    a_row = jnp.where(_iota((1, LANES), 1) < SSD_HEADS, -jnp.exp(alog_ref[...]), 0.0)
    for i in range(per_step):
        t = step * per_step + i
        rows = slice(i * n_seq, (i + 1) * n_seq)

        def raw_xbc(back):
            cur = xbc_s[pl.ds(pl.multiple_of(jnp.maximum(t - back, 0) * n_seq, n_seq), n_seq), :]
            if back == 0:
                return cur
            old = cst_ref[0, jnp.clip(SSD_CONV - 1 + t - back, 0, SSD_CONV - 2)]
            return jnp.where(t >= back, cur, old)

        acc = raw_xbc(SSD_CONV - 1) * cw[0:1, :]
        for k in range(1, SSD_CONV):
            acc = acc + raw_xbc(SSD_CONV - 1 - k) * cw[k:k + 1, :]
        xbc_c = _silu(acc + cb_ref[...])
        xs = xbc_c[:, :SSD_WIDTH]
        dtlr = proj_s[rows, P_DTLR:P_DTLR + LANES]
        dtv = _softplus(dtlr + dtb_ref[...])
        xs_ref[rows, :] = xs
        xt_ref[i] = (xs * _expand_heads(dtv, SSD_HEADS)).T
        bt_ref[i] = xbc_c[:, SSD_WIDTH:SSD_WIDTH + LANES].T
        ct_ref[i] = xbc_c[:, SSD_WIDTH + LANES:].T
        at_ref[i] = jnp.exp(dtv * a_row).T[:SSD_HEADS, :]
        glog = _log_sigmoid(_dot(dtlr, wgk_ref[...]) + bgk_ref[...]) * (1.0 / GLA_GATE_NORM)
        qt_ref[i] = (proj_s[rows, P_GQ:P_GQ + LANES] * (GLA_DK ** -0.5)).T
        kt_ref[i] = proj_s[rows, P_GK:P_GK + LANES].T
        egt_ref[i] = jnp.exp(glog).T
        vt_ref[i] = proj_s[rows, P_GV:P_GV + GLA_WIDTH].T
        sq = proj_s[rows, P_SQ:P_SQ + SWA_WIDTH]
        qn_ref[rows, :] = sq * _head_rms_scale(sq) * qnw_ref[...] * (SWA_HEAD_DIM ** -0.5)
        sk = proj_s[rows, P_SK:P_SK + LANES]
        kn_ref[rows, :] = sk * _head_rms_scale(sk) * knw_ref[...]
        vn_ref[rows, :] = proj_s[rows, P_SV:P_SV + LANES]
        gates_ref[rows, :] = jnp.concatenate(
            [_silu(proj_s[rows, P_Z:P_Z + SSD_WIDTH]), _silu(proj_s[rows, P_GG:P_GG + GLA_WIDTH]),
             _silu(proj_s[rows, P_SG:P_SG + SWA_WIDTH])], axis=1)
        first_kept = seq - (SSD_CONV - 1)

        @pl.when(t >= first_kept)
        def _():
            conv_ref[0, jnp.maximum(t - first_kept, 0)] = raw_xbc(0)


N_STATE_INPUTS = 19
SEQ_PER_STEP = 16


def _sample_state_kernel(seq, n_aliased, *refs):
    (xt_ref, bt_ref, ct_ref, at_ref, qt_ref, kt_ref, egt_ref, vt_ref, qn_ref, kn_ref, vn_ref,
     ssm_ref, gla_ref, kc_ref, vc_ref, bucket_c_ref, bucket_n_ref, rel_ref, sink_ref) = refs[:N_STATE_INPUTS]
    (ssm_o, gla_o, ko_ref, vo_ref, yt_ref, ot_ref, oswa_ref,
     qa_s, qb_s, krow_s, vrow_s, oa_s, ob_s, biasc_s, biasn_s) = refs[N_STATE_INPUTS + n_aliased:]
    j = pl.program_id(0)
    n_seq = LANES
    head_of_row = _div(_iota((SWA_HEADS * seq, LANES), 0), seq)

    def by_head(values):
        out = values[SWA_HEADS - 1]
        for hh in range(SWA_HEADS - 2, -1, -1):
            out = jnp.where(head_of_row == hh, values[hh], out)
        return out

    @pl.when(j == 0)
    def _():
        biasc_s[...] = by_head(_build_bias(bucket_c_ref[...], rel_ref))
        biasn_s[...] = by_head(_build_bias(bucket_n_ref[...], rel_ref))

    sub = _iota((SUBLANES, LANES), 0)
    a_rows = [jnp.sum(jnp.where(sub == j, at_ref[t], 0.0), axis=0, keepdims=True) for t in range(seq)]

    def ssd_body(p8, carry):
        r8 = pl.multiple_of(p8 * SUBLANES, SUBLANES)
        x_tiles = [xt_ref[t, pl.ds(r8, SUBLANES), :] for t in range(seq)]
        y_rows = [[] for _ in range(seq)]
        for pp in range(SUBLANES):
            r64 = pl.multiple_of((p8 * SUBLANES + pp) * SSD_STATE, SSD_STATE)
            slab = ssm_ref[0, 0, pl.ds(r64, SSD_STATE), :]
            for t in range(seq):
                slab = slab * a_rows[t] + x_tiles[t][pp:pp + 1, :] * bt_ref[t]
                y_rows[t].append(jnp.sum(ct_ref[t] * slab, axis=0, keepdims=True))
            ssm_o[0, 0, pl.ds(r64, SSD_STATE), :] = slab
        for t in range(seq):
            yt_ref[t, pl.ds(r8, SUBLANES), :] = jnp.concatenate(y_rows[t], axis=0)
        return carry

    lax.fori_loop(0, SSD_HEAD_DIM // SUBLANES, ssd_body, 0)

    @pl.when(j < GLA_HEADS)
    def _():
        for t in range(seq):
            ot_ref[t] = jnp.zeros((GLA_DV, LANES), F32)

        def gla_body(d8, carry):
            r8 = pl.multiple_of(d8 * SUBLANES, SUBLANES)
            q_tiles = [qt_ref[t, pl.ds(r8, SUBLANES), :] for t in range(seq)]
            k_tiles = [kt_ref[t, pl.ds(r8, SUBLANES), :] for t in range(seq)]
            g_tiles = [egt_ref[t, pl.ds(r8, SUBLANES), :] for t in range(seq)]
            for dd in range(SUBLANES):
                r64 = pl.multiple_of((d8 * SUBLANES + dd) * GLA_DV, GLA_DV)
                slab = gla_ref[0, 0, pl.ds(r64, GLA_DV), :]
                for t in range(seq):
                    slab = slab * g_tiles[t][dd:dd + 1, :] + k_tiles[t][dd:dd + 1, :] * vt_ref[t]
                    ot_ref[t] = ot_ref[t] + q_tiles[t][dd:dd + 1, :] * slab
                gla_o[0, 0, pl.ds(r64, GLA_DV), :] = slab
            return carry

        lax.fori_loop(0, GLA_DK // SUBLANES, gla_body, 0)

    base = pl.multiple_of(j * SEQ_PER_STEP, SEQ_PER_STEP)
    for t in range(seq):
        src = pl.ds(t * n_seq + base, SEQ_PER_STEP)
        dst = pl.ds(t, SEQ_PER_STEP, stride=seq)
        qa_s[dst, :] = qn_ref[src, 0:LANES]
        qb_s[dst, :] = qn_ref[src, LANES:2 * LANES]
        krow_s[dst, :] = kn_ref[src, :]
        vrow_s[dst, :] = vn_ref[src, :]
    kn_t = krow_s[...].T
    vn_t = vrow_s[...].T
    keep_old = _iota((LANES, WINDOW), 1) < WINDOW - seq
    lo8 = _iota((seq, LANES), 1) < HALF
    sink_col = by_head([jnp.full((SWA_HEADS * seq, LANES), sink_ref[hh], F32) for hh in range(SWA_HEADS)])[:, 0:1]

    def swa_stages(bl):
        r8 = pl.multiple_of(bl * seq, seq)
        v = {}

        def logits():
            qa = qa_s[pl.ds(r8, seq), :]
            qb = qb_s[pl.ds(r8, seq), :]
            qs = jnp.concatenate([jnp.where(lo8, qa, 0.0), jnp.where(lo8, qb, 0.0),
                                  jnp.where(lo8, 0.0, qa), jnp.where(lo8, 0.0, qb)], axis=0)
            v["lc"] = _dot(qs, kc_ref[0, bl]) + biasc_s[...]
            v["ln"] = _dot_nt(qs, krow_s[pl.ds(r8, seq), :]) + biasn_s[:, 0:seq]

        def softmax():
            lc, ln = v["lc"], v["ln"]
            m = jnp.maximum(jnp.maximum(jnp.max(lc, axis=-1, keepdims=True), jnp.max(ln, axis=-1, keepdims=True)),
                            sink_col)
            v["ec"] = jnp.exp(lc - m)
            v["en"] = jnp.exp(ln - m)
            v["inv"] = 1.0 / (jnp.sum(v["ec"], axis=-1, keepdims=True) + jnp.sum(v["en"], axis=-1, keepdims=True)
                              + jnp.exp(sink_col - m))

        def values():
            o = (_dot_nt(v["ec"], vc_ref[0, bl]) + _dot(v["en"], vrow_s[pl.ds(r8, seq), :])) * v["inv"]
            oa_s[pl.ds(r8, seq), :] = jnp.where(lo8, o[0:seq], o[2 * seq:3 * seq])
            ob_s[pl.ds(r8, seq), :] = jnp.where(lo8, o[seq:2 * seq], o[3 * seq:4 * seq])

        def window():
            ko_ref[0, bl] = jnp.where(keep_old, pltpu.roll(kc_ref[0, bl], WINDOW - seq, axis=1),
                                      pltpu.roll(kn_t, WINDOW - seq - r8, axis=1))
            vo_ref[0, bl] = jnp.where(keep_old, pltpu.roll(vc_ref[0, bl], WINDOW - seq, axis=1),
                                      pltpu.roll(vn_t, WINDOW - seq - r8, axis=1))

        return [logits, softmax, values, window]

    all_stages = [swa_stages(bl) for bl in range(SEQ_PER_STEP)]
    for k in range(len(all_stages[0])):
        for stages in all_stages:
            stages[k]()
    for t in range(seq):
        src = pl.ds(t, SEQ_PER_STEP, stride=seq)
        oswa_ref[t] = jnp.concatenate([oa_s[src, :], ob_s[src, :]], axis=1)


def _sample_back_kernel(yt_ref, ot_ref, oswa_ref, xs_ref, gates_ref, h_ref, p_ref, dsk_ref, snw_ref, gnw_ref,
                        wout_ref, wpe_ref, wpg_ref, y_ref, mix_s):
    n_seq = BLK
    for t in range(yt_ref.shape[0]):
        rows = slice(t * n_seq, (t + 1) * n_seq)
        y = (yt_ref[t].T + dsk_ref[...] * xs_ref[rows, :]) * gates_ref[rows, 0:SSD_WIDTH]
        mix_s[rows, 0:SSD_WIDTH] = _group_rmsnorm(y, snw_ref[...]).astype(BF16)
        o = ot_ref[t].T
        y_gla = o * _head_rms_scale(o) * gnw_ref[...] * gates_ref[rows, SSD_WIDTH:SSD_WIDTH + GLA_WIDTH]
        mix_s[rows, SSD_WIDTH:SSD_WIDTH + GLA_WIDTH] = y_gla.astype(BF16)
        mix_s[rows, SSD_WIDTH + GLA_WIDTH:] = (oswa_ref[t] * gates_ref[rows, SSD_WIDTH + GLA_WIDTH:]).astype(BF16)
    y_ref[...] = _epilogue(h_ref[...], mix_s[...], p_ref[0], wout_ref, wpg_ref, wpe_ref)


def _const_spec(shape):
    nd = len(shape)
    return pl.BlockSpec(shape, lambda *_: (0,) * nd)


def _smem_spec():
    return pl.BlockSpec(memory_space=pltpu.SMEM)


def _layer_spec(arr, layer):
    return pl.BlockSpec((None,) + arr.shape[1:], lambda *_: (layer, 0, 0), pipeline_mode=pl.Buffered(1))


def _layer_weights(layer, w):
    ops = [w[name] for name in ("norm_w", "w_in", "conv_w", "conv_b", "dt_bias", "a_log", "d_skip", "ssd_norm_w",
                                "gla_w_gk", "gla_b_gk", "gla_norm_w", "q_norm_w", "k_norm_w", "w_out", "w_pe",
                                "w_pg")]
    return ops, [_layer_spec(o, layer) for o in ops]


def _prompt_layer(layer, depth, bsz, h, p_all, prev_states, bucket, rel, sinks, wops, wspecs):
    rows_total, _ = h.shape
    seq_len = rows_total // bsz
    chunk = PROMPT_CHUNK_ROWS
    pair = 2 * chunk
    chunks_per_seq = seq_len // chunk
    assert seq_len % pair == 0 and chunks_per_seq & (chunks_per_seq - 1) == 0
    n_pairs = rows_total // pair
    kern = functools.partial(_prompt_kernel, chunks_per_seq, len(prev_states))
    proj_rows = pl.BlockSpec((pair, D_MODEL), lambda k: (jnp.minimum(k, n_pairs - 1), 0))
    out_rows = pl.BlockSpec((pair, D_MODEL), lambda k: (jnp.maximum(k - 1, 0), 0))
    p_spec = pl.BlockSpec((1, pair, PLE_DIM), lambda k: (layer, jnp.maximum(k - 1, 0), 0))
    per_seq = lambda s: pl.BlockSpec(
        (1, 1) + s, lambda k: (layer, jnp.maximum(2 * k - 1, 0) // chunks_per_seq) + (0,) * len(s))
    state_shapes = ((SSD_WIDTH, SSD_STATE), (SSD_CONV - 1, SSD_CONV_DIM), (GLA_HEADS * GLA_DK, GLA_DV),
                    (LANES, WINDOW), (LANES, WINDOW))
    out_shape = (jax.ShapeDtypeStruct((rows_total, D_MODEL), F32),) + tuple(
        jax.ShapeDtypeStruct((depth, bsz) + s, F32) for s in state_shapes)
    return pl.pallas_call(
        kern,
        grid=(n_pairs + 1,),
        in_specs=[proj_rows, out_rows, p_spec, _const_spec(bucket.shape), _smem_spec(), _smem_spec()]
        + wspecs + [pl.BlockSpec(memory_space=pl.ANY)] * len(prev_states),
        out_specs=(out_rows,) + tuple(per_seq(s) for s in state_shapes),
        out_shape=out_shape,
        input_output_aliases={N_PROMPT_INPUTS + k: 1 + k for k in range(len(prev_states))},
        scratch_shapes=[
            pltpu.VMEM((chunk, PROJ_W), F32), pltpu.VMEM((chunk, PROJ_W), F32),
            pltpu.VMEM((chunk, XBC_W), F32), pltpu.VMEM((chunk, XBC_W), F32),
            pltpu.VMEM((chunk, D_MODEL), BF16), pltpu.VMEM((chunk, D_MODEL), BF16),
            pltpu.VMEM((chunk, D_MODEL), BF16),
            pltpu.VMEM((chunk, D_MODEL), F32),
            pltpu.VMEM((chunk, D_MODEL), BF16),
            pltpu.VMEM((SUBLANES, XBC_W), F32),
            pltpu.VMEM((BLK, SSD_WIDTH), F32),
            pltpu.VMEM((LANES, GLA_WIDTH), F32),
            pltpu.VMEM((2 * BLK, LANES), F32),
            pltpu.VMEM((2 * BLK, LANES), F32),
            pltpu.VMEM((2 * SWA_HEADS, BLK, 2 * BLK), F32),
            pltpu.VMEM((SUBLANES, XBC_W), F32), pltpu.VMEM((BLK, SSD_WIDTH), F32),
            pltpu.VMEM((LANES, GLA_WIDTH), F32), pltpu.VMEM((BLK, LANES), F32),
            pltpu.VMEM((BLK, LANES), F32),
        ],
        compiler_params=pltpu.CompilerParams(
            dimension_semantics=("arbitrary",), vmem_limit_bytes=VMEM_LIMIT_BYTES),
        name="prompt_layer",
    )(h, h, p_all, bucket, rel, sinks, *wops, *prev_states)


def _whole(shape, layer=None):
    if layer is None:
        return pl.BlockSpec(shape, lambda *_: (0,) * len(shape), pipeline_mode=pl.Buffered(1))
    return pl.BlockSpec((1,) + shape[1:], lambda *_: (layer,) + (0,) * (len(shape) - 1),
                        pipeline_mode=pl.Buffered(1))


def _sample_layer_native(layer, depth, seq, h, p_all, conv_in, ssm_in, gla_in, kc_in, vc_in, prev_states,
                         buckets, rel, sinks, wops):
    (nw, win, cw, cb, dtb, alog, dsk, snw, wgk, bgk, gnw, qnw, knw, wout, wpe, wpg) = wops
    rows = h.shape[0]
    n_seq = rows // seq
    assert n_seq == LANES and n_seq % SEQ_PER_STEP == 0 and SSD_HEADS * SEQ_PER_STEP == n_seq
    prev_conv, prev_rest = (prev_states[:1], prev_states[1:]) if prev_states else ((), ())
    f32 = lambda *s: jax.ShapeDtypeStruct(s, F32)
    cparams = lambda sem: pltpu.CompilerParams(dimension_semantics=sem, vmem_limit_bytes=VMEM_LIMIT_BYTES)

    per_step = SAMPLE_POSITIONS_PER_STEP
    step_rows = per_step * n_seq
    assert seq % per_step == 0
    row_blk = lambda w: pl.BlockSpec((step_rows, w), lambda s: (s, 0))
    pos_blk = lambda n: pl.BlockSpec((per_step, n, n_seq), lambda s: (s, 0, 0))
    front_in = [h, conv_in, nw, win, cw, cb, dtb, alog, wgk, bgk, qnw, knw]
    front_specs = [row_blk(D_MODEL), _whole(conv_in.shape, layer)] + [_layer_spec(a, layer) for a in front_in[2:]]
    front_out = (f32(rows, SSD_WIDTH), f32(rows, D_MODEL),
                 f32(seq, SSD_WIDTH, n_seq), f32(seq, LANES, n_seq), f32(seq, LANES, n_seq),
                 f32(seq, SSD_HEADS, n_seq), f32(seq, LANES, n_seq), f32(seq, LANES, n_seq), f32(seq, LANES, n_seq),
                 f32(seq, GLA_WIDTH, n_seq), f32(rows, SWA_WIDTH), f32(rows, LANES), f32(rows, LANES),
                 f32(*conv_in.shape))
    front_out_specs = (row_blk(SSD_WIDTH), row_blk(D_MODEL), pos_blk(SSD_WIDTH), pos_blk(LANES), pos_blk(LANES),
                       pos_blk(SSD_HEADS), pos_blk(LANES), pos_blk(LANES), pos_blk(LANES), pos_blk(GLA_WIDTH),
                       row_blk(SWA_WIDTH), row_blk(LANES), row_blk(LANES), _whole(conv_in.shape, layer))
    (xs, gates, xt, bt, ct, at, qt, kt, egt, vt, qn, kn, vn, conv_o) = pl.pallas_call(
        functools.partial(_sample_front_kernel, seq, len(prev_conv)),
        grid=(seq // per_step,),
        in_specs=front_specs + [pl.BlockSpec(memory_space=pl.ANY)] * len(prev_conv),
        out_specs=front_out_specs,
        out_shape=front_out,
        input_output_aliases={len(front_in) + k: len(front_out) - 1 + k for k in range(len(prev_conv))},
        scratch_shapes=[pltpu.VMEM((step_rows, D_MODEL), BF16), pltpu.VMEM((rows, XBC_W), F32),
                        pltpu.VMEM((step_rows, PROJ_W), F32)],
        compiler_params=cparams(("arbitrary",)),
        name="sample_front",
    )(*front_in, *prev_conv)

    n_steps = SSD_HEADS
    per_group = SSD_HEADS // SSD_GROUPS
    gla_head = lambda j: jnp.minimum(j, GLA_HEADS - 1)
    blk3 = lambda n, f: pl.BlockSpec((seq, n, n_seq), lambda j: (0, f(j), 0))
    state_in = [xt, bt, ct, at, qt, kt, egt, vt, qn, kn, vn, ssm_in, gla_in, kc_in, vc_in, buckets[0], buckets[1],
                rel, sinks]
    ssm_spec = pl.BlockSpec((1, 1) + ssm_in.shape[2:], lambda j: (layer, j, 0, 0))
    gla_spec = pl.BlockSpec((1, 1) + gla_in.shape[2:], lambda j: (layer, gla_head(j), 0, 0))
    kv_spec = pl.BlockSpec((1, SEQ_PER_STEP) + kc_in.shape[2:], lambda j: (layer, j, 0, 0))
    state_specs = [blk3(SSD_HEAD_DIM, lambda j: j), blk3(SSD_STATE, lambda j: j // per_group),
                   blk3(SSD_STATE, lambda j: j // per_group), _const_spec(at.shape),
                   blk3(GLA_DK, gla_head), blk3(GLA_DK, gla_head), blk3(GLA_DK, gla_head), blk3(GLA_DV, gla_head),
                   _const_spec(qn.shape), _const_spec(kn.shape), _const_spec(vn.shape),
                   ssm_spec, gla_spec, kv_spec, kv_spec,
                   _const_spec(buckets[0].shape), _const_spec(buckets[1].shape), _smem_spec(), _smem_spec()]
    state_out = (f32(*ssm_in.shape), f32(*gla_in.shape), f32(*kc_in.shape), f32(*vc_in.shape),
                 f32(seq, SSD_WIDTH, n_seq), f32(seq, GLA_WIDTH, n_seq), f32(seq, n_seq, SWA_WIDTH))
    ssm_o, gla_o, ko, vo, yt, ot, oswa = pl.pallas_call(
        functools.partial(_sample_state_kernel, seq, len(prev_rest)),
        grid=(n_steps,),
        in_specs=state_specs + [pl.BlockSpec(memory_space=pl.ANY)] * len(prev_rest),
        out_specs=(ssm_spec, gla_spec, kv_spec, kv_spec, blk3(SSD_HEAD_DIM, lambda j: j), blk3(GLA_DV, gla_head),
                   pl.BlockSpec((seq, SEQ_PER_STEP, SWA_WIDTH), lambda j: (0, j, 0))),
        out_shape=state_out,
        input_output_aliases={len(state_in) + k: k for k in range(len(prev_rest))},
        scratch_shapes=[pltpu.VMEM((SEQ_PER_STEP * seq, LANES), F32)] * 6
        + [pltpu.VMEM((SWA_HEADS * seq, LANES), F32)] * 2,
        compiler_params=cparams(("arbitrary",)),
        name="sample_state",
    )(*state_in, *prev_rest)

    back_in = [yt, ot, oswa, xs, gates, h, p_all, dsk, snw, gnw, wout, wpe, wpg]
    back_specs = [pos_blk(SSD_WIDTH), pos_blk(GLA_WIDTH),
                  pl.BlockSpec((per_step, n_seq, SWA_WIDTH), lambda s: (s, 0, 0)),
                  row_blk(SSD_WIDTH), row_blk(D_MODEL), row_blk(D_MODEL),
                  pl.BlockSpec((1, step_rows, PLE_DIM), lambda s: (layer, s, 0))] + [
        _layer_spec(a, layer) for a in back_in[7:]]
    y = pl.pallas_call(
        _sample_back_kernel,
        grid=(seq // per_step,),
        in_specs=back_specs,
        out_specs=row_blk(D_MODEL),
        out_shape=f32(rows, D_MODEL),
        scratch_shapes=[pltpu.VMEM((step_rows, D_MODEL), BF16)],
        compiler_params=cparams(("arbitrary",)),
        name="sample_back",
    )(*back_in)
    return y, (conv_o, ssm_o, gla_o, ko, vo)


SWA_HEAD_ORDER = (0, 2, 1, 3)


def _win_tile_runs():
    sizes = (SSD_WIDTH, SSD_CONV_DIM, SSD_HEADS, GLA_HEADS * GLA_DK, GLA_HEADS * GLA_DK, GLA_WIDTH, GLA_WIDTH,
             GLA_RANK, SWA_WIDTH, SWA_KV_HEADS * SWA_HEAD_DIM, SWA_KV_HEADS * SWA_HEAD_DIM, SWA_WIDTH)
    offs = np.concatenate([[0], np.cumsum(sizes)])
    seg = lambda k: np.arange(offs[k], offs[k + 1])
    z, xbc, dt, gq, gk, gv, gg, glr, sq, sk, sv, sg = [seg(k) for k in range(len(sizes))]
    heads = lambda a: np.concatenate([a[h * SWA_HEAD_DIM:(h + 1) * SWA_HEAD_DIM] for h in SWA_HEAD_ORDER])
    pad = np.full(LANES - SSD_HEADS - GLA_RANK, -1)
    src = np.concatenate([xbc, z, gq, gk, gv, gg, heads(sq), sk, sv, heads(sg), dt, glr, pad])
    assert src.size == XBC_W + PROJ_W
    tiles = []
    for j in range(src.size // LANES):
        idx = src[j * LANES:(j + 1) * LANES]
        cuts = [0] + [k for k in range(1, LANES) if (idx[k] != idx[k - 1] + 1 and not (idx[k] == -1 == idx[k - 1]))]
        runs = [(int(idx[a]), b - a) for a, b in zip(cuts, cuts[1:] + [LANES])]
        assert all(n % SUBLANES == 0 and (s < 0 or s % SUBLANES == 0) for s, n in runs)
        tiles.append(runs)
    return tiles


def _win_prep_kernel(tile_runs, wt_ref, out_ref):
    for j, runs in enumerate(tile_runs):
        parts = [jnp.zeros((n, D_MODEL), F32) if s < 0 else wt_ref[0, s:s + n, :] for s, n in runs]
        tile = parts[0] if len(parts) == 1 else jnp.concatenate(parts, axis=0)
        out_ref[0, :, j * LANES:(j + 1) * LANES] = tile.T.astype(BF16)


def _prepare_w_in(w_in):
    depth, d_model, d_in = w_in.shape
    w_t = jnp.swapaxes(w_in, 1, 2)
    return pl.pallas_call(
        functools.partial(_win_prep_kernel, _win_tile_runs()),
        grid=(depth,),
        in_specs=[pl.BlockSpec((1, d_in, d_model), lambda l: (l, 0, 0))],
        out_specs=pl.BlockSpec((1, d_model, XBC_W + PROJ_W), lambda l: (l, 0, 0)),
        out_shape=jax.ShapeDtypeStruct((depth, d_model, XBC_W + PROJ_W), BF16),
        compiler_params=pltpu.CompilerParams(
            dimension_semantics=("arbitrary",), vmem_limit_bytes=VMEM_LIMIT_BYTES),
        name="w_in_prep",
    )(w_t)


def _cast_prep_kernel(wout_ref, wpe_ref, wpg_ref, wout_o, wpe_o, wpg_o):
    band = pl.program_id(1)

    @pl.when(band < pl.num_programs(1) - 1)
    def _():
        wout_o[0] = wout_ref[0].astype(BF16)

    @pl.when(band == pl.num_programs(1) - 1)
    def _():
        for slot, head in enumerate(SWA_HEAD_ORDER):
            src = slice(head * SWA_HEAD_DIM, (head + 1) * SWA_HEAD_DIM)
            dst = slice(slot * SWA_HEAD_DIM, (slot + 1) * SWA_HEAD_DIM)
            wout_o[0, dst, :] = wout_ref[0, src, :].astype(BF16)

    wpe_o[0] = wpe_ref[0].astype(BF16)
    wpg_o[0] = wpg_ref[0].astype(BF16)


def _prepare_out_weights(w_out, w_pe, w_pg):
    depth = w_out.shape[0]
    ops = (w_out, w_pe, w_pg)
    assert (w_out.shape[1] - SWA_WIDTH) % SWA_WIDTH == 0
    n_bands = w_out.shape[1] // SWA_WIDTH
    assert all(a.shape[1] % (n_bands * SUBLANES) == 0 for a in ops)
    spec = lambda a: pl.BlockSpec((1, a.shape[1] // n_bands, a.shape[2]), lambda l, r: (l, r, 0))
    return pl.pallas_call(
        _cast_prep_kernel,
        grid=(depth, n_bands),
        in_specs=[spec(a) for a in ops],
        out_specs=tuple(spec(a) for a in ops),
        out_shape=tuple(jax.ShapeDtypeStruct(a.shape, BF16) for a in ops),
        compiler_params=pltpu.CompilerParams(
            dimension_semantics=("arbitrary", "arbitrary"), vmem_limit_bytes=VMEM_LIMIT_BYTES),
        name="w_out_prep",
    )(*ops)


def _prepare_weights(norm_w, w_in, conv_w, conv_b, dt_bias, a_log, d_skip, ssd_norm_w, gla_w_gk, gla_b_gk,
                     gla_norm_w, q_norm_w, k_norm_w, w_out, w_pe, w_pg):
    w_out_p, w_pe_p, w_pg_p = _prepare_out_weights(w_out, w_pe, w_pg)
    lane_pad = lambda x: jnp.pad(x, ((0, 0), (0, LANES - x.shape[-1])))[:, None, :]
    wgk_p = jnp.pad(gla_w_gk, ((0, 0), (LR_LANE0, LANES - LR_LANE0 - GLA_RANK), (0, 0))).astype(BF16)
    return dict(
        norm_w=norm_w[:, None, :], w_in=_prepare_w_in(w_in), conv_w=conv_w, conv_b=conv_b[:, None, :],
        dt_bias=lane_pad(dt_bias), a_log=lane_pad(a_log),
        d_skip=jnp.repeat(d_skip, SSD_HEAD_DIM, axis=-1)[:, None, :], ssd_norm_w=ssd_norm_w[:, None, :],
        gla_w_gk=wgk_p, gla_b_gk=gla_b_gk[:, None, :],
        gla_norm_w=jnp.tile(gla_norm_w, (1, GLA_HEADS))[:, None, :],
        q_norm_w=jnp.tile(q_norm_w, (1, SWA_HEADS))[:, None, :],
        k_norm_w=jnp.tile(k_norm_w, (1, SWA_KV_HEADS))[:, None, :],
        w_out=w_out_p, w_pe=w_pe_p, w_pg=w_pg_p)


def kernel(x_prompt, x_sample, state_ssm, state_conv, state_gla, cache_swa_k, cache_swa_v, p_prompt, p_sample, rel_bias, norm_w, w_in, conv_w, conv_b, dt_bias, a_log, d_skip, ssd_norm_w, gla_w_gk, gla_b_gk, gla_norm_w, q_norm_w, k_norm_w, attn_sinks, w_out, w_pe, w_pg):
    depth = w_in.shape[0]
    bp, seq_p, _ = x_prompt.shape
    bs, seq_s, _ = x_sample.shape
    assert seq_s == SUBLANES and BLK % seq_s == 0 and (bs * seq_s) % BLK == 0
    assert cache_swa_k.shape[2] == WINDOW
    w = _prepare_weights(norm_w, w_in, conv_w, conv_b, dt_bias, a_log, d_skip, ssd_norm_w, gla_w_gk, gla_b_gk,
                         gla_norm_w, q_norm_w, k_norm_w, w_out, w_pe, w_pg)
    rel_flat = rel_bias.reshape(-1)
    dist_p = WINDOW + np.arange(BLK)[:, None] - np.arange(2 * BLK)[None, :]
    bucket_p = jnp.asarray(_bucket_table(dist_p))
    t_of_row = np.tile(np.arange(seq_s), SWA_HEADS)[:, None]
    bucket_c = jnp.asarray(_bucket_table(WINDOW + t_of_row - np.arange(WINDOW)[None, :]))
    dist_n = np.where(np.arange(LANES)[None, :] < seq_s, t_of_row - np.arange(LANES)[None, :], -1)
    bucket_n = jnp.asarray(_bucket_table(dist_n))

    ssm_in = jnp.transpose(state_ssm, (0, 2, 3, 4, 1)).reshape(depth, SSD_HEADS, SSD_HEAD_DIM * SSD_STATE, bs)
    gla_in = jnp.transpose(state_gla, (0, 2, 3, 4, 1)).reshape(depth, GLA_HEADS, GLA_DK * GLA_DV, bs)
    kv_in = lambda a: jnp.transpose(a, (0, 1, 3, 4, 2)).reshape(depth, bs, SWA_KV_HEADS * SWA_HEAD_DIM, WINDOW)
    kc_in, vc_in = kv_in(cache_swa_k), kv_in(cache_swa_v)
    conv_in = jnp.transpose(state_conv, (0, 2, 1, 3))

    hp = x_prompt.reshape(bp * seq_p, D_MODEL)
    p_prompt_rows = p_prompt.reshape(depth, bp * seq_p, PLE_DIM)
    hs = jnp.transpose(x_sample, (1, 0, 2)).reshape(seq_s * bs, D_MODEL)
    p_sample_rows = jnp.transpose(p_sample, (0, 2, 1, 3)).reshape(depth, seq_s * bs, PLE_DIM)
    states_p = ()
    states_s = ()
    for i in range(depth):
        wops, wspecs = _layer_weights(i, w)
        hp, *states_p = _prompt_layer(i, depth, bp, hp, p_prompt_rows, tuple(states_p), bucket_p, rel_flat,
                                      attn_sinks[i], wops, wspecs)
        hs, states_s = _sample_layer_native(i, depth, seq_s, hs, p_sample_rows, conv_in, ssm_in, gla_in, kc_in,
                                            vc_in, states_s, (bucket_c, bucket_n), rel_flat, attn_sinks[i], wops)
    ssm_p, conv_p, gla_p, kt_p, vt_p = states_p
    conv_s, ssm_s, gla_s, kt_s, vt_s = states_s
    unpack_kv = lambda a: jnp.transpose(
        a.reshape(a.shape[:2] + (SWA_KV_HEADS, SWA_HEAD_DIM, WINDOW)), (0, 1, 4, 2, 3))
    outs_p = (ssm_p.reshape(depth, bp, SSD_HEADS, SSD_HEAD_DIM, SSD_STATE), conv_p,
              gla_p.reshape(depth, bp, GLA_HEADS, GLA_DK, GLA_DV), unpack_kv(kt_p), unpack_kv(vt_p))
    seq_last = lambda a, dims: jnp.transpose(a.reshape(a.shape[:2] + dims + (bs,)), (0, 4, 1, 2, 3))
    outs_s = (seq_last(ssm_s, (SSD_HEAD_DIM, SSD_STATE)), jnp.transpose(conv_s, (0, 2, 1, 3)),
              seq_last(gla_s, (GLA_DK, GLA_DV)), unpack_kv(kt_s), unpack_kv(vt_s))
    y_sample = jnp.transpose(hs.reshape(seq_s, bs, D_MODEL), (1, 0, 2))
    return (hp.reshape(bp, seq_p, D_MODEL), y_sample) + outs_p + outs_s
```

```python
import functools
import math

import numpy as np
import jax
import jax.numpy as jnp
from jax import lax
from jax.experimental import pallas as pl
from jax.experimental.pallas import tpu as pltpu

D_MODEL = 1024
DEPTH = 2
SSD_HEADS = 8
SSD_HEAD_DIM = 64
SSD_WIDTH = SSD_HEADS * SSD_HEAD_DIM
SSD_GROUPS = 2
SSD_STATE = 64
SSD_CONV = 4
SSD_CONV_DIM = SSD_WIDTH + 2 * SSD_GROUPS * SSD_STATE
SSD_CHUNK = 128
GLA_HEADS = 4
GLA_DK = 32
GLA_DV = 64
GLA_WIDTH = GLA_HEADS * GLA_DV
GLA_RANK = 16
GLA_GATE_NORM = 16.0
GLA_CHUNK = 64
SWA_HEADS = 4
SWA_KV_HEADS = 2
SWA_HEAD_DIM = 64
SWA_WIDTH = SWA_HEADS * SWA_HEAD_DIM
WINDOW = 128
REL_BUCKETS = 32
REL_MAX_DIST = 128
PLE_DIM = 256
EPS = 1e-6

LANES = 128
SUBLANES = 8
HALF = LANES // 2
BLK = 128
VMEM_LIMIT_BYTES = 56 * 1024 * 1024

XBC_W = SSD_CONV_DIM
P_Z = 0
P_GQ = P_Z + SSD_WIDTH
P_GK = P_GQ + LANES
P_GV = P_GK + LANES
P_GG = P_GV + GLA_WIDTH
P_SQ = P_GG + GLA_WIDTH
P_SK = P_SQ + SWA_WIDTH
P_SV = P_SK + LANES
P_SG = P_SV + LANES
P_DTLR = P_SG + SWA_WIDTH
PROJ_W = P_DTLR + LANES
LR_LANE0 = SSD_HEADS

F32 = jnp.float32
BF16 = jnp.bfloat16
NEG_INF = float("-inf")
LOG2E = math.log2(math.e)
N_PROMPT_INPUTS = 22
PROMPT_CHUNK_ROWS = 2 * BLK
TICKS_PER_ITEM = 2
GATED_AFTER_PROJ_ITEMS = 9
NT_DIMS = (((1,), (1,)), ((), ()))


def _iota(shape, dim):
    return lax.broadcasted_iota(jnp.int32, shape, dim)


def _div(x, d):
    return x >> (d.bit_length() - 1)


def _mod(x, d):
    return x & (d - 1)


def _softplus(x):
    e = jnp.exp(-jnp.abs(x))
    u = 1.0 + e
    d = u - 1.0
    log1p_e = jnp.where(d == 0.0, e, jnp.log(u) * (e / jnp.where(d == 0.0, 1.0, d)))
    return jnp.maximum(x, 0.0) + log1p_e


def _log_sigmoid(x):
    return jnp.minimum(x, 0.0) - jnp.log(1.0 + jnp.exp(-jnp.abs(x)))


def _silu(x):
    return x * jax.nn.sigmoid(x)


def _dot(a, b):
    return jnp.dot(a.astype(BF16), b.astype(BF16), preferred_element_type=F32)


def _dot_nt(a, b):
    return lax.dot_general(a.astype(BF16), b.astype(BF16), NT_DIMS, preferred_element_type=F32)


def _dot_exact(sel, x):
    x1 = x.astype(BF16)
    r1 = x - x1.astype(F32)
    x2 = r1.astype(BF16)
    x3 = (r1 - x2.astype(F32)).astype(BF16)
    dot = functools.partial(jnp.dot, sel, preferred_element_type=F32)
    return dot(x1) + dot(x2) + dot(x3)


def _expand_heads(x, n_heads):
    rows = x.shape[0]
    lo = _iota((rows, LANES), 1) < HALF
    tiles = []
    for j in range(n_heads // 2):
        a = jnp.broadcast_to(x[:, 2 * j:2 * j + 1], (rows, LANES))
        b = jnp.broadcast_to(x[:, 2 * j + 1:2 * j + 2], (rows, LANES))
        tiles.append(jnp.where(lo, a, b))
    return jnp.concatenate(tiles, axis=1)


def _head_rms_scale(x):
    rows, width = x.shape
    lo = _iota((rows, LANES), 1) < HALF
    outs = []
    for j in range(width // LANES):
        t = x[:, j * LANES:(j + 1) * LANES]
        sq = t * t
        s_lo = jnp.sum(jnp.where(lo, sq, 0.0), axis=-1, keepdims=True)
        s_hi = jnp.sum(jnp.where(lo, 0.0, sq), axis=-1, keepdims=True)
        outs.append(lax.rsqrt(jnp.where(lo, s_lo, s_hi) * (1.0 / HALF) + EPS))
    return outs[0] if len(outs) == 1 else jnp.concatenate(outs, axis=1)


def _group_rmsnorm(y, w):
    gw = SSD_WIDTH // SSD_GROUPS
    outs = []
    for g in range(SSD_GROUPS):
        t = y[:, g * gw:(g + 1) * gw]
        ms = jnp.sum(t * t, axis=-1, keepdims=True) * (1.0 / gw)
        outs.append(t * lax.rsqrt(ms + EPS))
    return jnp.concatenate(outs, axis=1) * w


def _rel_bucket_np(dist):
    n = np.maximum(dist, 0)
    exact = REL_BUCKETS // 2
    nf = np.maximum(n, 1).astype(np.float64)
    large = exact + (np.log(nf / exact) / math.log(REL_MAX_DIST / exact) * (REL_BUCKETS - exact)).astype(np.int32)
    large = np.minimum(large, REL_BUCKETS - 1)
    return np.where(n < exact, n, large).astype(np.int32)


def _bucket_table(dist):
    return np.where((dist >= 0) & (dist < WINDOW), _rel_bucket_np(dist), -1).astype(np.int32)


def _no_tick():
    pass


def _ssd_intra(xbc_c, dtv, acum, pair_mask, tick=_no_tick):
    xs = xbc_c[:, :SSD_WIDTH]
    bm = xbc_c[:, SSD_WIDTH:SSD_WIDTH + LANES]
    cm = xbc_c[:, SSD_WIDTH + LANES:]
    lane = _iota((BLK, LANES), 1)
    lo = lane < HALF
    acum_t = acum.T
    eacum = jnp.exp2(acum)
    tail = jnp.exp2(acum[BLK - 1:BLK, :] - acum)
    dtv_e = _expand_heads(dtv, SSD_HEADS)
    eacum_e = _expand_heads(eacum, SSD_HEADS)
    tail_e = _expand_heads(tail, SSD_HEADS)
    tick()
    xdt = xs * dtv_e
    xw = xdt * tail_e
    cb = [_dot_nt(jnp.where(lo, cm, 0.0), bm), _dot_nt(jnp.where(lo, 0.0, cm), bm)]
    y_pairs = []
    for j in range(SSD_HEADS // 2):
        tick()
        g = (2 * j) // (SSD_HEADS // SSD_GROUPS)
        ms = []
        for k in range(2):
            h = 2 * j + k
            seg = acum[:, h:h + 1] - acum_t[h:h + 1, :]
            dec = jnp.where(pair_mask, jnp.exp2(seg), 0.0)
            ms.append((cb[g] * dec).astype(BF16))
        xp = xdt[:, j * LANES:(j + 1) * LANES]
        rhs = jnp.concatenate([jnp.where(lo, xp, 0.0), jnp.where(lo, 0.0, xp)], axis=0)
        y_pairs.append(_dot(jnp.concatenate(ms, axis=1), rhs))
    y_intra = jnp.concatenate(y_pairs, axis=1)
    return y_intra, xs, bm, cm, xw, eacum, eacum_e


def _gla_intra(gq, gk, gv, bcs, att_mask, tick=_no_tick):
    eb = jnp.exp2(bcs)
    qe = gq * (GLA_DK ** -0.5) * eb
    ke = gk * jnp.exp2(-bcs)
    btot = jnp.concatenate(
        [jnp.broadcast_to(bcs[(c2 + 1) * GLA_CHUNK - 1:(c2 + 1) * GLA_CHUNK, :], (GLA_CHUNK, LANES))
         for c2 in range(BLK // GLA_CHUNK)], axis=0)
    kd = gk * jnp.exp2(btot - bcs)
    lane_k = _iota((GLA_CHUNK, LANES), 1)
    lane_v = _iota((GLA_CHUNK, GLA_WIDTH), 1)
    outs = []
    for c2 in range(BLK // GLA_CHUNK):
        tick()
        rs = slice(c2 * GLA_CHUNK, (c2 + 1) * GLA_CHUNK)
        ke_c = ke[rs]
        v_c = gv[rs]
        kbd = jnp.concatenate(
            [jnp.where(_div(lane_k, GLA_DK) == h, ke_c, 0.0) for h in range(GLA_HEADS)], axis=0)
        att = _dot_nt(qe[rs], kbd)
        att = jnp.where(att_mask, att, 0.0)
        vbd = jnp.concatenate(
            [jnp.where(_div(lane_v, GLA_DV) == h, v_c, 0.0) for h in range(GLA_HEADS)], axis=0)
        outs.append(_dot(att, vbd))
    return jnp.concatenate(outs, axis=0), qe, kd, jnp.exp2(btot)


def _build_bias(bucket, rel_ref):
    accs = [jnp.full(bucket.shape, NEG_INF, F32) for _ in range(SWA_HEADS)]
    for b in range(REL_BUCKETS):
        hit = bucket == b
        for h in range(SWA_HEADS):
            accs[h] = jnp.where(hit, rel_ref[b * SWA_HEADS + h], accs[h])
    return accs


def _epilogue(h, mix, p, wout_ref, wpg_ref, wpe_ref):
    h1 = h + jnp.dot(mix, wout_ref[...], preferred_element_type=F32)
    gate = jax.nn.sigmoid(jnp.dot(h1.astype(BF16), wpg_ref[...], preferred_element_type=F32))
    pe = jnp.dot(p.astype(BF16), wpe_ref[...], preferred_element_type=F32)
    return h1 + gate * pe


def _prompt_kernel(chunks_per_seq, n_aliased, *refs):
    (ha_ref, hc_ref, p_ref, bucket_ref, rel_ref, sink_ref, nw_ref, win_ref, cw_ref, cb_ref, dtb_ref,
     alog_ref, dsk_ref, snw_ref, wgk_ref, bgk_ref, gnw_ref, qnw_ref, knw_ref, wout_ref,
     wpe_ref, wpg_ref) = refs[:N_PROMPT_INPUTS]
    (y_ref, ssm_ref, conv_ref, gla_ref, ko_ref, vo_ref,
     proj_e, proj_o, xbc_e, xbc_o, mix_e, mix_o, u_s, h1_s, h1b_s, hist_s, st_s, s2_s, kext_s, vext_s,
     bias_s, hist_snap, st_snap, s2_snap, k_snap, v_snap) = refs[N_PROMPT_INPUTS + n_aliased:]
    k_idx = pl.program_id(0)

    @pl.when(k_idx == 0)
    def _():
        accs = [a * LOG2E for a in _build_bias(bucket_ref[...], rel_ref)]
        own_block = _iota((BLK, 2 * BLK), 1) >= BLK
        for hh in range(SWA_HEADS):
            bias_s[hh] = accs[hh]
            bias_s[SWA_HEADS + hh] = jnp.where(own_block, accs[hh], NEG_INF)
        for ref in (proj_o, xbc_o, mix_e, mix_o, hist_s, st_s, s2_s, kext_s, vext_s):
            ref[...] = jnp.zeros(ref.shape, ref.dtype)

    row = _iota((BLK, BLK), 0)
    col = _iota((BLK, BLK), 1)
    causal = row >= col
    tri = jnp.where(causal, 1.0, 0.0).astype(BF16)
    lo = col < HALF
    lane_row = _iota((1, LANES), 1)
    a_row = jnp.where(lane_row < SSD_HEADS, -jnp.exp(alog_ref[...]) * LOG2E, 0.0)
    bd_mask = _div(_iota((LANES, GLA_WIDTH), 0), GLA_DK) == _div(_iota((LANES, GLA_WIDTH), 1), GLA_DV)
    att_t = _iota((GLA_CHUNK, GLA_WIDTH), 0)
    att_s = _mod(_iota((GLA_CHUNK, GLA_WIDTH), 1), GLA_CHUNK)
    att_mask = att_s <= att_t
    lo2 = _iota((2 * BLK, LANES), 1) < HALF

    group_w = SSD_WIDTH // SSD_GROUPS

    def block(blk, proj_s, xbc_s, mix_s, starts_sequence, tick):
        rows = slice(blk * BLK, (blk + 1) * BLK)
        cw = cw_ref[...]
        tick()
        if blk == 0:
            xwin = jnp.concatenate([hist_s[...], xbc_s[0:BLK, :]], axis=0)
        else:
            xwin = xbc_s[blk * BLK - SUBLANES:(blk + 1) * BLK, :]
        acc = xwin[SUBLANES - 3:SUBLANES - 3 + BLK, :] * cw[0:1, :]
        for k in range(1, SSD_CONV):
            acc = acc + xwin[SUBLANES - 3 + k:SUBLANES - 3 + k + BLK, :] * cw[k:k + 1, :]
        xbc_c = _silu(acc + cb_ref[...])
        tick()
        dtlr = proj_s[rows, P_DTLR:P_DTLR + LANES]
        dtv_t = _softplus((dtlr + dtb_ref[...]).T[0:SSD_HEADS, :])
        dtv = jnp.concatenate([dtv_t, jnp.zeros((LANES - SSD_HEADS, BLK), F32)], axis=0).T
        glog = _log_sigmoid(_dot(dtlr, wgk_ref[...]) + bgk_ref[...]) * (LOG2E / GLA_GATE_NORM)
        sums = _dot_exact(tri, jnp.concatenate([dtv * a_row, glog], axis=1))
        acum = sums[:, :LANES]
        gsum = sums[:, LANES:]
        bcs = gsum
        for c2 in range(1, BLK // GLA_CHUNK):
            before = gsum[c2 * GLA_CHUNK - 1:c2 * GLA_CHUNK, :]
            bcs = jnp.where(_div(row, GLA_CHUNK) == c2, gsum - before, bcs)
        tick()
        y_intra, xs, bm, cm, xw, eacum, eacum_e = _ssd_intra(xbc_c, dtv, acum, causal, tick)
        tick()
        st = st_s[...]
        y = y_intra + _dot(cm, st) * eacum_e + dsk_ref[...] * xs
        bm_t = bm.T
        for g in range(SSD_GROUPS):
            gr = slice(g * SSD_STATE, (g + 1) * SSD_STATE)
            gc = slice(g * group_w, (g + 1) * group_w)
            st_s[gr, gc] = st[gr, gc] * eacum_e[BLK - 1:BLK, gc] + _dot(bm_t[gr, :], xw[:, gc])
        tick()
        y = y * _silu(proj_s[rows, P_Z:P_Z + SSD_WIDTH])
        mix_s[rows, 0:SSD_WIDTH] = _group_rmsnorm(y, snw_ref[...]).astype(BF16)
        tick()
        gk = proj_s[rows, P_GK:P_GK + LANES]
        gv = proj_s[rows, P_GV:P_GV + GLA_WIDTH]
        o_intra, qe, kd, ebt = _gla_intra(proj_s[rows, P_GQ:P_GQ + LANES], gk, gv, bcs, att_mask, tick)
        kd_t = kd.T
        ebt_t = ebt.T
        s2 = s2_s[...]
        o_parts = []
        for c2 in range(BLK // GLA_CHUNK):
            tick()
            rs = slice(c2 * GLA_CHUNK, (c2 + 1) * GLA_CHUNK)
            o_parts.append(o_intra[rs] + _dot(qe[rs], s2))
            u2 = _dot(jnp.where(_div(col, GLA_CHUNK) == c2, kd_t, 0.0), gv)
            last = (c2 + 1) * GLA_CHUNK - 1
            s2 = s2 * ebt_t[:, last:last + 1] + jnp.where(bd_mask, u2, 0.0)
        s2_s[...] = s2
        o = jnp.concatenate(o_parts, axis=0)
        y_gla = o * _head_rms_scale(o) * gnw_ref[...] * _silu(proj_s[rows, P_GG:P_GG + GLA_WIDTH])
        mix_s[rows, SSD_WIDTH:SSD_WIDTH + GLA_WIDTH] = y_gla.astype(BF16)
        tick()
        sq = proj_s[rows, P_SQ:P_SQ + SWA_WIDTH]
        qn = sq * _head_rms_scale(sq) * qnw_ref[...] * (SWA_HEAD_DIM ** -0.5 * LOG2E)
        sk = proj_s[rows, P_SK:P_SK + LANES]
        kn = sk * _head_rms_scale(sk) * knw_ref[...]
        vn = proj_s[rows, P_SV:P_SV + LANES]
        kext_s[BLK:2 * BLK, :] = kn
        vext_s[BLK:2 * BLK, :] = vn
        kext = kext_s[...]
        vext = vext_s[...]
        qa = qn[:, :LANES]
        qb = qn[:, LANES:]
        qs = jnp.concatenate([jnp.where(lo, qa, 0.0), jnp.where(lo, qb, 0.0),
                              jnp.where(lo, 0.0, qa), jnp.where(lo, 0.0, qb)], axis=0)
        logits = _dot_nt(qs, kext)
        tick()
        if blk == 0 and starts_sequence is not False:
            bias_row0 = jnp.where(starts_sequence, SWA_HEADS, 0)
        else:
            bias_row0 = 0
        es = []
        invs = []
        for hh in range(SWA_HEADS):
            tick()
            sink = sink_ref[hh] * LOG2E
            l = logits[hh * BLK:(hh + 1) * BLK] + bias_s[bias_row0 + hh]
            m = jnp.maximum(jnp.max(l, axis=-1, keepdims=True), sink)
            e = jnp.exp2(l - m)
            den = jnp.sum(e, axis=-1, keepdims=True) + jnp.exp2(sink - m)
            es.append(e.astype(BF16))
            invs.append(1.0 / den)
        v_stack = jnp.concatenate([jnp.where(lo2, vext, 0.0), jnp.where(lo2, 0.0, vext)], axis=0)
        tile_a = _dot(jnp.concatenate([es[0], es[2]], axis=1), v_stack) * jnp.where(lo, invs[0], invs[2])
        tile_b = _dot(jnp.concatenate([es[1], es[3]], axis=1), v_stack) * jnp.where(lo, invs[1], invs[3])
        oa = jnp.concatenate([tile_a, tile_b], axis=1)
        y_swa = oa * _silu(proj_s[rows, P_SG:P_SG + SWA_WIDTH])
        mix_s[rows, SSD_WIDTH + GLA_WIDTH:] = y_swa.astype(BF16)
        kext_s[0:BLK, :] = kn
        vext_s[0:BLK, :] = vn

    chunk = proj_e.shape[0]

    def project_items(rows, proj_s, xbc_s):
        def norm():
            h = ha_ref[rows, :]
            ms = jnp.mean(h * h, axis=-1, keepdims=True)
            u_s[...] = (h * lax.rsqrt(ms + EPS) * nw_ref[...]).astype(BF16)

        def cols(dst, lo_c, hi_c, w_off):
            def item():
                dst[:, lo_c:hi_c] = jnp.dot(u_s[...], win_ref[:, w_off + lo_c:w_off + hi_c],
                                            preferred_element_type=F32)
            return item

        step = 2 * LANES
        items = [norm]
        items += [cols(xbc_s, c, min(c + step, XBC_W), 0) for c in range(0, XBC_W, step)]
        items += [cols(proj_s, c, min(c + step, PROJ_W), XBC_W) for c in range(0, PROJ_W, step)]
        return items

    def epilogue_items(rows, mix_s):
        half_w = 2 * LANES

        def residual(c):
            def item():
                h1 = hc_ref[rows, c:c + half_w] + jnp.dot(
                    mix_s[...], wout_ref[:, c:c + half_w], preferred_element_type=F32)
                h1_s[:, c:c + half_w] = h1
                h1b_s[:, c:c + half_w] = h1.astype(BF16)
            return item

        def gated(c):
            def item():
                gate = jax.nn.sigmoid(jnp.dot(h1b_s[...], wpg_ref[:, c:c + half_w], preferred_element_type=F32))
                pe = jnp.dot(p_ref[0, rows, :].astype(BF16), wpe_ref[:, c:c + half_w],
                             preferred_element_type=F32)
                y_ref[rows, c:c + half_w] = h1_s[:, c:c + half_w] + gate * pe
            return item

        col0 = range(0, D_MODEL, half_w)
        return [residual(c) for c in col0], [gated(c) for c in col0]

    def mixer(proj_s, xbc_s, mix_s, starts_sequence, items):
        if starts_sequence is not False:
            keep = jnp.where(starts_sequence, 0.0, 1.0)
            for ref in (hist_s, st_s, s2_s):
                ref[...] = ref[...] * keep
            kext_s[0:BLK, :] = kext_s[0:BLK, :] * keep
            vext_s[0:BLK, :] = vext_s[0:BLK, :] * keep
        queue = list(items)
        calls = [0]

        def tick():
            calls[0] += 1
            if queue and calls[0] % TICKS_PER_ITEM == 0:
                queue.pop(0)()

        for blk in range(chunk // BLK):
            block(blk, proj_s, xbc_s, mix_s, starts_sequence, tick)
        while queue:
            queue.pop(0)()
        hist_s[...] = xbc_s[chunk - SUBLANES:chunk, :]

    def snapshot_states():
        st_snap[...] = st_s[...]
        s2_snap[...] = s2_s[...]
        hist_snap[...] = hist_s[...]
        k_snap[...] = kext_s[0:BLK, :]
        v_snap[...] = vext_s[0:BLK, :]

    def write_states():
        st = st_snap[...]
        stc = st[:SSD_STATE] + st[SSD_STATE:]
        ssm_ref[0, 0] = jnp.concatenate([stc, stc], axis=0).T[:, :SSD_STATE]
        conv_ref[0, 0] = hist_snap[SUBLANES - (SSD_CONV - 1):SUBLANES, :]
        s2 = s2_snap[...]
        w = s2[:, :LANES] + s2[:, LANES:]
        gla_ref[0, 0] = w[:, :GLA_DV] + w[:, GLA_DV:]
        ko_ref[0, 0] = k_snap[...].T
        vo_ref[0, 0] = v_snap[...].T

    even = slice(0, chunk)
    odd = slice(chunk, 2 * chunk)
    def stage_items(rows, proj_w, xbc_w, mix_r):
        residuals, gateds = epilogue_items(rows, mix_r)
        proj = project_items(rows, proj_w, xbc_w)
        return residuals + proj[:GATED_AFTER_PROJ_ITEMS] + gateds + proj[GATED_AFTER_PROJ_ITEMS:]

    mixer(proj_o, xbc_o, mix_o, False, stage_items(even, proj_e, xbc_e, mix_e))
    snapshot_states()
    mixer(proj_e, xbc_e, mix_e, _mod(2 * k_idx, chunks_per_seq) == 0, stage_items(odd, proj_o, xbc_o, mix_o))

    @pl.when((k_idx >= 1) & (_mod(2 * k_idx - 1, chunks_per_seq) == chunks_per_seq - 1))
    def _():
        write_states()


N_FRONT_INPUTS = 12
SAMPLE_POSITIONS_PER_STEP = 4


def _sample_front_kernel(seq, n_aliased, *refs):
    (h_ref, cst_ref, nw_ref, win_ref, cw_ref, cb_ref, dtb_ref, alog_ref, wgk_ref, bgk_ref,
     qnw_ref, knw_ref) = refs[:N_FRONT_INPUTS]
    (xs_ref, gates_ref, xt_ref, bt_ref, ct_ref, at_ref, qt_ref, kt_ref, egt_ref, vt_ref,
     qn_ref, kn_ref, vn_ref, conv_ref, u_s, xbc_s, proj_s) = refs[N_FRONT_INPUTS + n_aliased:]
    n_seq = BLK
    per_step = h_ref.shape[0] // n_seq
    step = pl.program_id(0)
    ht = h_ref[...]
    ms = jnp.mean(ht * ht, axis=-1, keepdims=True)
    u_s[...] = (ht * lax.rsqrt(ms + EPS) * nw_ref[...]).astype(BF16)
    xbc_s[pl.ds(pl.multiple_of(step * per_step * n_seq, n_seq), per_step * n_seq), :] = jnp.dot(
        u_s[...], win_ref[:, :XBC_W], preferred_element_type=F32)
    proj_s[...] = jnp.dot(u_s[...], win_ref[:, XBC_W:], preferred_element_type=F32)
    cw = cw_ref[...]
    a_row = jnp.where(_iota((1, LANES), 1) < SSD_HEADS, -jnp.exp(alog_ref[...]), 0.0)
    for i in range(per_step):
        t = step * per_step + i
        rows = slice(i * n_seq, (i + 1) * n_seq)

        def raw_xbc(back):
            cur = xbc_s[pl.ds(pl.multiple_of(jnp.maximum(t - back, 0) * n_seq, n_seq), n_seq), :]
            if back == 0:
                return cur
            old = cst_ref[0, jnp.clip(SSD_CONV - 1 + t - back, 0, SSD_CONV - 2)]
            return jnp.where(t >= back, cur, old)

        acc = raw_xbc(SSD_CONV - 1) * cw[0:1, :]
        for k in range(1, SSD_CONV):
            acc = acc + raw_xbc(SSD_CONV - 1 - k) * cw[k:k + 1, :]
        xbc_c = _silu(acc + cb_ref[...])
        xs = xbc_c[:, :SSD_WIDTH]
        dtlr = proj_s[rows, P_DTLR:P_DTLR + LANES]
        dtv = _softplus(dtlr + dtb_ref[...])
        xs_ref[rows, :] = xs
        xt_ref[i] = (xs * _expand_heads(dtv, SSD_HEADS)).T
        bt_ref[i] = xbc_c[:, SSD_WIDTH:SSD_WIDTH + LANES].T
        ct_ref[i] = xbc_c[:, SSD_WIDTH + LANES:].T
        at_ref[i] = jnp.exp(dtv * a_row).T[:SSD_HEADS, :]
        glog = _log_sigmoid(_dot(dtlr, wgk_ref[...]) + bgk_ref[...]) * (1.0 / GLA_GATE_NORM)
        qt_ref[i] = (proj_s[rows, P_GQ:P_GQ + LANES] * (GLA_DK ** -0.5)).T
        kt_ref[i] = proj_s[rows, P_GK:P_GK + LANES].T
        egt_ref[i] = jnp.exp(glog).T
        vt_ref[i] = proj_s[rows, P_GV:P_GV + GLA_WIDTH].T
        sq = proj_s[rows, P_SQ:P_SQ + SWA_WIDTH]
        qn_ref[rows, :] = sq * _head_rms_scale(sq) * qnw_ref[...] * (SWA_HEAD_DIM ** -0.5)
        sk = proj_s[rows, P_SK:P_SK + LANES]
        kn_ref[rows, :] = sk * _head_rms_scale(sk) * knw_ref[...]
        vn_ref[rows, :] = proj_s[rows, P_SV:P_SV + LANES]
        gates_ref[rows, :] = jnp.concatenate(
            [_silu(proj_s[rows, P_Z:P_Z + SSD_WIDTH]), _silu(proj_s[rows, P_GG:P_GG + GLA_WIDTH]),
             _silu(proj_s[rows, P_SG:P_SG + SWA_WIDTH])], axis=1)
        first_kept = seq - (SSD_CONV - 1)

        @pl.when(t >= first_kept)
        def _():
            conv_ref[0, jnp.maximum(t - first_kept, 0)] = raw_xbc(0)


N_STATE_INPUTS = 19
SEQ_PER_STEP = 16


def _sample_state_kernel(seq, n_aliased, *refs):
    (xt_ref, bt_ref, ct_ref, at_ref, qt_ref, kt_ref, egt_ref, vt_ref, qn_ref, kn_ref, vn_ref,
     ssm_ref, gla_ref, kc_ref, vc_ref, bucket_c_ref, bucket_n_ref, rel_ref, sink_ref) = refs[:N_STATE_INPUTS]
    (ssm_o, gla_o, ko_ref, vo_ref, yt_ref, ot_ref, oswa_ref,
     qa_s, qb_s, krow_s, vrow_s, oa_s, ob_s, biasc_s, biasn_s) = refs[N_STATE_INPUTS + n_aliased:]
    j = pl.program_id(0)
    n_seq = LANES
    head_of_row = _div(_iota((SWA_HEADS * seq, LANES), 0), seq)

    def by_head(values):
        out = values[SWA_HEADS - 1]
        for hh in range(SWA_HEADS - 2, -1, -1):
            out = jnp.where(head_of_row == hh, values[hh], out)
        return out

    @pl.when(j == 0)
    def _():
        biasc_s[...] = by_head(_build_bias(bucket_c_ref[...], rel_ref))
        biasn_s[...] = by_head(_build_bias(bucket_n_ref[...], rel_ref))

    sub = _iota((SUBLANES, LANES), 0)
    a_rows = [jnp.sum(jnp.where(sub == j, at_ref[t], 0.0), axis=0, keepdims=True) for t in range(seq)]

    def ssd_body(p8, carry):
        r8 = pl.multiple_of(p8 * SUBLANES, SUBLANES)
        x_tiles = [xt_ref[t, pl.ds(r8, SUBLANES), :] for t in range(seq)]
        y_rows = [[] for _ in range(seq)]
        for pp in range(SUBLANES):
            r64 = pl.multiple_of((p8 * SUBLANES + pp) * SSD_STATE, SSD_STATE)
            slab = ssm_ref[0, 0, pl.ds(r64, SSD_STATE), :]
            for t in range(seq):
                slab = slab * a_rows[t] + x_tiles[t][pp:pp + 1, :] * bt_ref[t]
                y_rows[t].append(jnp.sum(ct_ref[t] * slab, axis=0, keepdims=True))
            ssm_o[0, 0, pl.ds(r64, SSD_STATE), :] = slab
        for t in range(seq):
            yt_ref[t, pl.ds(r8, SUBLANES), :] = jnp.concatenate(y_rows[t], axis=0)
        return carry

    lax.fori_loop(0, SSD_HEAD_DIM // SUBLANES, ssd_body, 0)

    @pl.when(j < GLA_HEADS)
    def _():
        for t in range(seq):
            ot_ref[t] = jnp.zeros((GLA_DV, LANES), F32)

        def gla_body(d8, carry):
            r8 = pl.multiple_of(d8 * SUBLANES, SUBLANES)
            q_tiles = [qt_ref[t, pl.ds(r8, SUBLANES), :] for t in range(seq)]
            k_tiles = [kt_ref[t, pl.ds(r8, SUBLANES), :] for t in range(seq)]
            g_tiles = [egt_ref[t, pl.ds(r8, SUBLANES), :] for t in range(seq)]
            for dd in range(SUBLANES):
                r64 = pl.multiple_of((d8 * SUBLANES + dd) * GLA_DV, GLA_DV)
                slab = gla_ref[0, 0, pl.ds(r64, GLA_DV), :]
                for t in range(seq):
                    slab = slab * g_tiles[t][dd:dd + 1, :] + k_tiles[t][dd:dd + 1, :] * vt_ref[t]
                    ot_ref[t] = ot_ref[t] + q_tiles[t][dd:dd + 1, :] * slab
                gla_o[0, 0, pl.ds(r64, GLA_DV), :] = slab
            return carry

        lax.fori_loop(0, GLA_DK // SUBLANES, gla_body, 0)

    base = pl.multiple_of(j * SEQ_PER_STEP, SEQ_PER_STEP)
    for t in range(seq):
        src = pl.ds(t * n_seq + base, SEQ_PER_STEP)
        dst = pl.ds(t, SEQ_PER_STEP, stride=seq)
        qa_s[dst, :] = qn_ref[src, 0:LANES]
        qb_s[dst, :] = qn_ref[src, LANES:2 * LANES]
        krow_s[dst, :] = kn_ref[src, :]
        vrow_s[dst, :] = vn_ref[src, :]
    kn_t = krow_s[...].T
    vn_t = vrow_s[...].T
    keep_old = _iota((LANES, WINDOW), 1) < WINDOW - seq
    lo8 = _iota((seq, LANES), 1) < HALF
    sink_col = by_head([jnp.full((SWA_HEADS * seq, LANES), sink_ref[hh], F32) for hh in range(SWA_HEADS)])[:, 0:1]

    def swa_stages(bl):
        r8 = pl.multiple_of(bl * seq, seq)
        v = {}

        def logits():
            qa = qa_s[pl.ds(r8, seq), :]
            qb = qb_s[pl.ds(r8, seq), :]
            qs = jnp.concatenate([jnp.where(lo8, qa, 0.0), jnp.where(lo8, qb, 0.0),
                                  jnp.where(lo8, 0.0, qa), jnp.where(lo8, 0.0, qb)], axis=0)
            v["lc"] = _dot(qs, kc_ref[0, bl]) + biasc_s[...]
            v["ln"] = _dot_nt(qs, krow_s[pl.ds(r8, seq), :]) + biasn_s[:, 0:seq]

        def softmax():
            lc, ln = v["lc"], v["ln"]
            m = jnp.maximum(jnp.maximum(jnp.max(lc, axis=-1, keepdims=True), jnp.max(ln, axis=-1, keepdims=True)),
                            sink_col)
            v["ec"] = jnp.exp(lc - m)
            v["en"] = jnp.exp(ln - m)
            v["inv"] = 1.0 / (jnp.sum(v["ec"], axis=-1, keepdims=True) + jnp.sum(v["en"], axis=-1, keepdims=True)
                              + jnp.exp(sink_col - m))

        def values():
            o = (_dot_nt(v["ec"], vc_ref[0, bl]) + _dot(v["en"], vrow_s[pl.ds(r8, seq), :])) * v["inv"]
            oa_s[pl.ds(r8, seq), :] = jnp.where(lo8, o[0:seq], o[2 * seq:3 * seq])
            ob_s[pl.ds(r8, seq), :] = jnp.where(lo8, o[seq:2 * seq], o[3 * seq:4 * seq])

        def window():
            ko_ref[0, bl] = jnp.where(keep_old, pltpu.roll(kc_ref[0, bl], WINDOW - seq, axis=1),
                                      pltpu.roll(kn_t, WINDOW - seq - r8, axis=1))
            vo_ref[0, bl] = jnp.where(keep_old, pltpu.roll(vc_ref[0, bl], WINDOW - seq, axis=1),
                                      pltpu.roll(vn_t, WINDOW - seq - r8, axis=1))

        return [logits, softmax, values, window]

    all_stages = [swa_stages(bl) for bl in range(SEQ_PER_STEP)]
    for k in range(len(all_stages[0])):
        for stages in all_stages:
            stages[k]()
    for t in range(seq):
        src = pl.ds(t, SEQ_PER_STEP, stride=seq)
        oswa_ref[t] = jnp.concatenate([oa_s[src, :], ob_s[src, :]], axis=1)


def _sample_back_kernel(yt_ref, ot_ref, oswa_ref, xs_ref, gates_ref, h_ref, p_ref, dsk_ref, snw_ref, gnw_ref,
                        wout_ref, wpe_ref, wpg_ref, y_ref, mix_s):
    n_seq = BLK
    for t in range(yt_ref.shape[0]):
        rows = slice(t * n_seq, (t + 1) * n_seq)
        y = (yt_ref[t].T + dsk_ref[...] * xs_ref[rows, :]) * gates_ref[rows, 0:SSD_WIDTH]
        mix_s[rows, 0:SSD_WIDTH] = _group_rmsnorm(y, snw_ref[...]).astype(BF16)
        o = ot_ref[t].T
        y_gla = o * _head_rms_scale(o) * gnw_ref[...] * gates_ref[rows, SSD_WIDTH:SSD_WIDTH + GLA_WIDTH]
        mix_s[rows, SSD_WIDTH:SSD_WIDTH + GLA_WIDTH] = y_gla.astype(BF16)
        mix_s[rows, SSD_WIDTH + GLA_WIDTH:] = (oswa_ref[t] * gates_ref[rows, SSD_WIDTH + GLA_WIDTH:]).astype(BF16)
    y_ref[...] = _epilogue(h_ref[...], mix_s[...], p_ref[0], wout_ref, wpg_ref, wpe_ref)


def _const_spec(shape):
    nd = len(shape)
    return pl.BlockSpec(shape, lambda *_: (0,) * nd)


def _smem_spec():
    return pl.BlockSpec(memory_space=pltpu.SMEM)


def _layer_spec(arr, layer):
    return pl.BlockSpec((None,) + arr.shape[1:], lambda *_: (layer, 0, 0), pipeline_mode=pl.Buffered(1))


def _layer_weights(layer, w):
    ops = [w[name] for name in ("norm_w", "w_in", "conv_w", "conv_b", "dt_bias", "a_log", "d_skip", "ssd_norm_w",
                                "gla_w_gk", "gla_b_gk", "gla_norm_w", "q_norm_w", "k_norm_w", "w_out", "w_pe",
                                "w_pg")]
    return ops, [_layer_spec(o, layer) for o in ops]


def _prompt_layer(layer, depth, bsz, h, p_all, prev_states, bucket, rel, sinks, wops, wspecs):
    rows_total, _ = h.shape
    seq_len = rows_total // bsz
    chunk = PROMPT_CHUNK_ROWS
    pair = 2 * chunk
    chunks_per_seq = seq_len // chunk
    assert seq_len % pair == 0 and chunks_per_seq & (chunks_per_seq - 1) == 0
    n_pairs = rows_total // pair
    kern = functools.partial(_prompt_kernel, chunks_per_seq, len(prev_states))
    proj_rows = pl.BlockSpec((pair, D_MODEL), lambda k: (jnp.minimum(k, n_pairs - 1), 0))
    out_rows = pl.BlockSpec((pair, D_MODEL), lambda k: (jnp.maximum(k - 1, 0), 0))
    p_spec = pl.BlockSpec((1, pair, PLE_DIM), lambda k: (layer, jnp.maximum(k - 1, 0), 0))
    per_seq = lambda s: pl.BlockSpec(
        (1, 1) + s, lambda k: (layer, jnp.maximum(2 * k - 1, 0) // chunks_per_seq) + (0,) * len(s))
    state_shapes = ((SSD_WIDTH, SSD_STATE), (SSD_CONV - 1, SSD_CONV_DIM), (GLA_HEADS * GLA_DK, GLA_DV),
                    (LANES, WINDOW), (LANES, WINDOW))
    out_shape = (jax.ShapeDtypeStruct((rows_total, D_MODEL), F32),) + tuple(
        jax.ShapeDtypeStruct((depth, bsz) + s, F32) for s in state_shapes)
    return pl.pallas_call(
        kern,
        grid=(n_pairs + 1,),
        in_specs=[proj_rows, out_rows, p_spec, _const_spec(bucket.shape), _smem_spec(), _smem_spec()]
        + wspecs + [pl.BlockSpec(memory_space=pl.ANY)] * len(prev_states),
        out_specs=(out_rows,) + tuple(per_seq(s) for s in state_shapes),
        out_shape=out_shape,
        input_output_aliases={N_PROMPT_INPUTS + k: 1 + k for k in range(len(prev_states))},
        scratch_shapes=[
            pltpu.VMEM((chunk, PROJ_W), F32), pltpu.VMEM((chunk, PROJ_W), F32),
            pltpu.VMEM((chunk, XBC_W), F32), pltpu.VMEM((chunk, XBC_W), F32),
            pltpu.VMEM((chunk, D_MODEL), BF16), pltpu.VMEM((chunk, D_MODEL), BF16),
            pltpu.VMEM((chunk, D_MODEL), BF16),
            pltpu.VMEM((chunk, D_MODEL), F32),
            pltpu.VMEM((chunk, D_MODEL), BF16),
            pltpu.VMEM((SUBLANES, XBC_W), F32),
            pltpu.VMEM((BLK, SSD_WIDTH), F32),
            pltpu.VMEM((LANES, GLA_WIDTH), F32),
            pltpu.VMEM((2 * BLK, LANES), F32),
            pltpu.VMEM((2 * BLK, LANES), F32),
            pltpu.VMEM((2 * SWA_HEADS, BLK, 2 * BLK), F32),
            pltpu.VMEM((SUBLANES, XBC_W), F32), pltpu.VMEM((BLK, SSD_WIDTH), F32),
            pltpu.VMEM((LANES, GLA_WIDTH), F32), pltpu.VMEM((BLK, LANES), F32),
            pltpu.VMEM((BLK, LANES), F32),
        ],
        compiler_params=pltpu.CompilerParams(
            dimension_semantics=("arbitrary",), vmem_limit_bytes=VMEM_LIMIT_BYTES),
        name="prompt_layer",
    )(h, h, p_all, bucket, rel, sinks, *wops, *prev_states)


def _whole(shape, layer=None):
    if layer is None:
        return pl.BlockSpec(shape, lambda *_: (0,) * len(shape), pipeline_mode=pl.Buffered(1))
    return pl.BlockSpec((1,) + shape[1:], lambda *_: (layer,) + (0,) * (len(shape) - 1),
                        pipeline_mode=pl.Buffered(1))


def _sample_layer_native(layer, depth, seq, h, p_all, conv_in, ssm_in, gla_in, kc_in, vc_in, prev_states,
                         buckets, rel, sinks, wops):
    (nw, win, cw, cb, dtb, alog, dsk, snw, wgk, bgk, gnw, qnw, knw, wout, wpe, wpg) = wops
    rows = h.shape[0]
    n_seq = rows // seq
    assert n_seq == LANES and n_seq % SEQ_PER_STEP == 0 and SSD_HEADS * SEQ_PER_STEP == n_seq
    prev_conv, prev_rest = (prev_states[:1], prev_states[1:]) if prev_states else ((), ())
    f32 = lambda *s: jax.ShapeDtypeStruct(s, F32)
    cparams = lambda sem: pltpu.CompilerParams(dimension_semantics=sem, vmem_limit_bytes=VMEM_LIMIT_BYTES)

    per_step = SAMPLE_POSITIONS_PER_STEP
    step_rows = per_step * n_seq
    assert seq % per_step == 0
    row_blk = lambda w: pl.BlockSpec((step_rows, w), lambda s: (s, 0))
    pos_blk = lambda n: pl.BlockSpec((per_step, n, n_seq), lambda s: (s, 0, 0))
    front_in = [h, conv_in, nw, win, cw, cb, dtb, alog, wgk, bgk, qnw, knw]
    front_specs = [row_blk(D_MODEL), _whole(conv_in.shape, layer)] + [_layer_spec(a, layer) for a in front_in[2:]]
    front_out = (f32(rows, SSD_WIDTH), f32(rows, D_MODEL),
                 f32(seq, SSD_WIDTH, n_seq), f32(seq, LANES, n_seq), f32(seq, LANES, n_seq),
                 f32(seq, SSD_HEADS, n_seq), f32(seq, LANES, n_seq), f32(seq, LANES, n_seq), f32(seq, LANES, n_seq),
                 f32(seq, GLA_WIDTH, n_seq), f32(rows, SWA_WIDTH), f32(rows, LANES), f32(rows, LANES),
                 f32(*conv_in.shape))
    front_out_specs = (row_blk(SSD_WIDTH), row_blk(D_MODEL), pos_blk(SSD_WIDTH), pos_blk(LANES), pos_blk(LANES),
                       pos_blk(SSD_HEADS), pos_blk(LANES), pos_blk(LANES), pos_blk(LANES), pos_blk(GLA_WIDTH),
                       row_blk(SWA_WIDTH), row_blk(LANES), row_blk(LANES), _whole(conv_in.shape, layer))
    (xs, gates, xt, bt, ct, at, qt, kt, egt, vt, qn, kn, vn, conv_o) = pl.pallas_call(
        functools.partial(_sample_front_kernel, seq, len(prev_conv)),
        grid=(seq // per_step,),
        in_specs=front_specs + [pl.BlockSpec(memory_space=pl.ANY)] * len(prev_conv),
        out_specs=front_out_specs,
        out_shape=front_out,
        input_output_aliases={len(front_in) + k: len(front_out) - 1 + k for k in range(len(prev_conv))},
        scratch_shapes=[pltpu.VMEM((step_rows, D_MODEL), BF16), pltpu.VMEM((rows, XBC_W), F32),
                        pltpu.VMEM((step_rows, PROJ_W), F32)],
        compiler_params=cparams(("arbitrary",)),
        name="sample_front",
    )(*front_in, *prev_conv)

    n_steps = SSD_HEADS
    per_group = SSD_HEADS // SSD_GROUPS
    gla_head = lambda j: jnp.minimum(j, GLA_HEADS - 1)
    blk3 = lambda n, f: pl.BlockSpec((seq, n, n_seq), lambda j: (0, f(j), 0))
    state_in = [xt, bt, ct, at, qt, kt, egt, vt, qn, kn, vn, ssm_in, gla_in, kc_in, vc_in, buckets[0], buckets[1],
                rel, sinks]
    ssm_spec = pl.BlockSpec((1, 1) + ssm_in.shape[2:], lambda j: (layer, j, 0, 0))
    gla_spec = pl.BlockSpec((1, 1) + gla_in.shape[2:], lambda j: (layer, gla_head(j), 0, 0))
    kv_spec = pl.BlockSpec((1, SEQ_PER_STEP) + kc_in.shape[2:], lambda j: (layer, j, 0, 0))
    state_specs = [blk3(SSD_HEAD_DIM, lambda j: j), blk3(SSD_STATE, lambda j: j // per_group),
                   blk3(SSD_STATE, lambda j: j // per_group), _const_spec(at.shape),
                   blk3(GLA_DK, gla_head), blk3(GLA_DK, gla_head), blk3(GLA_DK, gla_head), blk3(GLA_DV, gla_head),
                   _const_spec(qn.shape), _const_spec(kn.shape), _const_spec(vn.shape),
                   ssm_spec, gla_spec, kv_spec, kv_spec,
                   _const_spec(buckets[0].shape), _const_spec(buckets[1].shape), _smem_spec(), _smem_spec()]
    state_out = (f32(*ssm_in.shape), f32(*gla_in.shape), f32(*kc_in.shape), f32(*vc_in.shape),
                 f32(seq, SSD_WIDTH, n_seq), f32(seq, GLA_WIDTH, n_seq), f32(seq, n_seq, SWA_WIDTH))
    ssm_o, gla_o, ko, vo, yt, ot, oswa = pl.pallas_call(
        functools.partial(_sample_state_kernel, seq, len(prev_rest)),
        grid=(n_steps,),
        in_specs=state_specs + [pl.BlockSpec(memory_space=pl.ANY)] * len(prev_rest),
        out_specs=(ssm_spec, gla_spec, kv_spec, kv_spec, blk3(SSD_HEAD_DIM, lambda j: j), blk3(GLA_DV, gla_head),
                   pl.BlockSpec((seq, SEQ_PER_STEP, SWA_WIDTH), lambda j: (0, j, 0))),
        out_shape=state_out,
        input_output_aliases={len(state_in) + k: k for k in range(len(prev_rest))},
        scratch_shapes=[pltpu.VMEM((SEQ_PER_STEP * seq, LANES), F32)] * 6
        + [pltpu.VMEM((SWA_HEADS * seq, LANES), F32)] * 2,
        compiler_params=cparams(("arbitrary",)),
        name="sample_state",
    )(*state_in, *prev_rest)

    back_in = [yt, ot, oswa, xs, gates, h, p_all, dsk, snw, gnw, wout, wpe, wpg]
    back_specs = [pos_blk(SSD_WIDTH), pos_blk(GLA_WIDTH),
                  pl.BlockSpec((per_step, n_seq, SWA_WIDTH), lambda s: (s, 0, 0)),
                  row_blk(SSD_WIDTH), row_blk(D_MODEL), row_blk(D_MODEL),
                  pl.BlockSpec((1, step_rows, PLE_DIM), lambda s: (layer, s, 0))] + [
        _layer_spec(a, layer) for a in back_in[7:]]
    y = pl.pallas_call(
        _sample_back_kernel,
        grid=(seq // per_step,),
        in_specs=back_specs,
        out_specs=row_blk(D_MODEL),
        out_shape=f32(rows, D_MODEL),
        scratch_shapes=[pltpu.VMEM((step_rows, D_MODEL), BF16)],
        compiler_params=cparams(("arbitrary",)),
        name="sample_back",
    )(*back_in)
    return y, (conv_o, ssm_o, gla_o, ko, vo)


SWA_HEAD_ORDER = (0, 2, 1, 3)


def _win_tile_runs():
    sizes = (SSD_WIDTH, SSD_CONV_DIM, SSD_HEADS, GLA_HEADS * GLA_DK, GLA_HEADS * GLA_DK, GLA_WIDTH, GLA_WIDTH,
             GLA_RANK, SWA_WIDTH, SWA_KV_HEADS * SWA_HEAD_DIM, SWA_KV_HEADS * SWA_HEAD_DIM, SWA_WIDTH)
    offs = np.concatenate([[0], np.cumsum(sizes)])
    seg = lambda k: np.arange(offs[k], offs[k + 1])
    z, xbc, dt, gq, gk, gv, gg, glr, sq, sk, sv, sg = [seg(k) for k in range(len(sizes))]
    heads = lambda a: np.concatenate([a[h * SWA_HEAD_DIM:(h + 1) * SWA_HEAD_DIM] for h in SWA_HEAD_ORDER])
    pad = np.full(LANES - SSD_HEADS - GLA_RANK, -1)
    src = np.concatenate([xbc, z, gq, gk, gv, gg, heads(sq), sk, sv, heads(sg), dt, glr, pad])
    assert src.size == XBC_W + PROJ_W
    tiles = []
    for j in range(src.size // LANES):
        idx = src[j * LANES:(j + 1) * LANES]
        cuts = [0] + [k for k in range(1, LANES) if (idx[k] != idx[k - 1] + 1 and not (idx[k] == -1 == idx[k - 1]))]
        runs = [(int(idx[a]), b - a) for a, b in zip(cuts, cuts[1:] + [LANES])]
        assert all(n % SUBLANES == 0 and (s < 0 or s % SUBLANES == 0) for s, n in runs)
        tiles.append(runs)
    return tiles


def _win_prep_kernel(tile_runs, wt_ref, out_ref):
    for j, runs in enumerate(tile_runs):
        parts = [jnp.zeros((n, D_MODEL), F32) if s < 0 else wt_ref[0, s:s + n, :] for s, n in runs]
        tile = parts[0] if len(parts) == 1 else jnp.concatenate(parts, axis=0)
        out_ref[0, :, j * LANES:(j + 1) * LANES] = tile.T.astype(BF16)


def _prepare_w_in(w_in):
    depth, d_model, d_in = w_in.shape
    w_t = jnp.swapaxes(w_in, 1, 2)
    return pl.pallas_call(
        functools.partial(_win_prep_kernel, _win_tile_runs()),
        grid=(depth,),
        in_specs=[pl.BlockSpec((1, d_in, d_model), lambda l: (l, 0, 0))],
        out_specs=pl.BlockSpec((1, d_model, XBC_W + PROJ_W), lambda l: (l, 0, 0)),
        out_shape=jax.ShapeDtypeStruct((depth, d_model, XBC_W + PROJ_W), BF16),
        compiler_params=pltpu.CompilerParams(
            dimension_semantics=("arbitrary",), vmem_limit_bytes=VMEM_LIMIT_BYTES),
        name="w_in_prep",
    )(w_t)


def _cast_prep_kernel(wout_ref, wpe_ref, wpg_ref, wout_o, wpe_o, wpg_o):
    mix_w = SSD_WIDTH + GLA_WIDTH
    wout_o[0, 0:mix_w, :] = wout_ref[0, 0:mix_w, :].astype(BF16)
    for slot, head in enumerate(SWA_HEAD_ORDER):
        src = slice(mix_w + head * SWA_HEAD_DIM, mix_w + (head + 1) * SWA_HEAD_DIM)
        dst = slice(mix_w + slot * SWA_HEAD_DIM, mix_w + (slot + 1) * SWA_HEAD_DIM)
        wout_o[0, dst, :] = wout_ref[0, src, :].astype(BF16)
    wpe_o[0] = wpe_ref[0].astype(BF16)
    wpg_o[0] = wpg_ref[0].astype(BF16)


def _prepare_out_weights(w_out, w_pe, w_pg):
    depth = w_out.shape[0]
    ops = (w_out, w_pe, w_pg)
    spec = lambda a: pl.BlockSpec((1,) + a.shape[1:], lambda l: (l, 0, 0))
    return pl.pallas_call(
        _cast_prep_kernel,
        grid=(depth,),
        in_specs=[spec(a) for a in ops],
        out_specs=tuple(spec(a) for a in ops),
        out_shape=tuple(jax.ShapeDtypeStruct(a.shape, BF16) for a in ops),
        compiler_params=pltpu.CompilerParams(
            dimension_semantics=("arbitrary",), vmem_limit_bytes=VMEM_LIMIT_BYTES),
        name="w_out_prep",
    )(*ops)


def _prepare_weights(norm_w, w_in, conv_w, conv_b, dt_bias, a_log, d_skip, ssd_norm_w, gla_w_gk, gla_b_gk,
                     gla_norm_w, q_norm_w, k_norm_w, w_out, w_pe, w_pg):
    w_out_p, w_pe_p, w_pg_p = _prepare_out_weights(w_out, w_pe, w_pg)
    lane_pad = lambda x: jnp.pad(x, ((0, 0), (0, LANES - x.shape[-1])))[:, None, :]
    wgk_p = jnp.pad(gla_w_gk, ((0, 0), (LR_LANE0, LANES - LR_LANE0 - GLA_RANK), (0, 0))).astype(BF16)
    return dict(
        norm_w=norm_w[:, None, :], w_in=_prepare_w_in(w_in), conv_w=conv_w, conv_b=conv_b[:, None, :],
        dt_bias=lane_pad(dt_bias), a_log=lane_pad(a_log),
        d_skip=jnp.repeat(d_skip, SSD_HEAD_DIM, axis=-1)[:, None, :], ssd_norm_w=ssd_norm_w[:, None, :],
        gla_w_gk=wgk_p, gla_b_gk=gla_b_gk[:, None, :],
        gla_norm_w=jnp.tile(gla_norm_w, (1, GLA_HEADS))[:, None, :],
        q_norm_w=jnp.tile(q_norm_w, (1, SWA_HEADS))[:, None, :],
        k_norm_w=jnp.tile(k_norm_w, (1, SWA_KV_HEADS))[:, None, :],
        w_out=w_out_p, w_pe=w_pe_p, w_pg=w_pg_p)


def kernel(x_prompt, x_sample, state_ssm, state_conv, state_gla, cache_swa_k, cache_swa_v, p_prompt, p_sample, rel_bias, norm_w, w_in, conv_w, conv_b, dt_bias, a_log, d_skip, ssd_norm_w, gla_w_gk, gla_b_gk, gla_norm_w, q_norm_w, k_norm_w, attn_sinks, w_out, w_pe, w_pg):
    depth = w_in.shape[0]
    bp, seq_p, _ = x_prompt.shape
    bs, seq_s, _ = x_sample.shape
    assert seq_s == SUBLANES and BLK % seq_s == 0 and (bs * seq_s) % BLK == 0
    assert cache_swa_k.shape[2] == WINDOW
    w = _prepare_weights(norm_w, w_in, conv_w, conv_b, dt_bias, a_log, d_skip, ssd_norm_w, gla_w_gk, gla_b_gk,
                         gla_norm_w, q_norm_w, k_norm_w, w_out, w_pe, w_pg)
    rel_flat = rel_bias.reshape(-1)
    dist_p = WINDOW + np.arange(BLK)[:, None] - np.arange(2 * BLK)[None, :]
    bucket_p = jnp.asarray(_bucket_table(dist_p))
    t_of_row = np.tile(np.arange(seq_s), SWA_HEADS)[:, None]
    bucket_c = jnp.asarray(_bucket_table(WINDOW + t_of_row - np.arange(WINDOW)[None, :]))
    dist_n = np.where(np.arange(LANES)[None, :] < seq_s, t_of_row - np.arange(LANES)[None, :], -1)
    bucket_n = jnp.asarray(_bucket_table(dist_n))

    ssm_in = jnp.transpose(state_ssm, (0, 2, 3, 4, 1)).reshape(depth, SSD_HEADS, SSD_HEAD_DIM * SSD_STATE, bs)
    gla_in = jnp.transpose(state_gla, (0, 2, 3, 4, 1)).reshape(depth, GLA_HEADS, GLA_DK * GLA_DV, bs)
    kv_in = lambda a: jnp.transpose(a, (0, 1, 3, 4, 2)).reshape(depth, bs, SWA_KV_HEADS * SWA_HEAD_DIM, WINDOW)
    kc_in, vc_in = kv_in(cache_swa_k), kv_in(cache_swa_v)
    conv_in = jnp.transpose(state_conv, (0, 2, 1, 3))

    hp = x_prompt.reshape(bp * seq_p, D_MODEL)
    p_prompt_rows = p_prompt.reshape(depth, bp * seq_p, PLE_DIM)
    hs = jnp.transpose(x_sample, (1, 0, 2)).reshape(seq_s * bs, D_MODEL)
    p_sample_rows = jnp.transpose(p_sample, (0, 2, 1, 3)).reshape(depth, seq_s * bs, PLE_DIM)
    states_p = ()
    states_s = ()
    for i in range(depth):
        wops, wspecs = _layer_weights(i, w)
        hp, *states_p = _prompt_layer(i, depth, bp, hp, p_prompt_rows, tuple(states_p), bucket_p, rel_flat,
                                      attn_sinks[i], wops, wspecs)
        hs, states_s = _sample_layer_native(i, depth, seq_s, hs, p_sample_rows, conv_in, ssm_in, gla_in, kc_in,
                                            vc_in, states_s, (bucket_c, bucket_n), rel_flat, attn_sinks[i], wops)
    ssm_p, conv_p, gla_p, kt_p, vt_p = states_p
    conv_s, ssm_s, gla_s, kt_s, vt_s = states_s
    unpack_kv = lambda a: jnp.transpose(
        a.reshape(a.shape[:2] + (SWA_KV_HEADS, SWA_HEAD_DIM, WINDOW)), (0, 1, 4, 2, 3))
    outs_p = (ssm_p.reshape(depth, bp, SSD_HEADS, SSD_HEAD_DIM, SSD_STATE), conv_p,
              gla_p.reshape(depth, bp, GLA_HEADS, GLA_DK, GLA_DV), unpack_kv(kt_p), unpack_kv(vt_p))
    seq_last = lambda a, dims: jnp.transpose(a.reshape(a.shape[:2] + dims + (bs,)), (0, 4, 1, 2, 3))
    outs_s = (seq_last(ssm_s, (SSD_HEAD_DIM, SSD_STATE)), jnp.transpose(conv_s, (0, 2, 1, 3)),
              seq_last(gla_s, (GLA_DK, GLA_DV)), unpack_kv(kt_s), unpack_kv(vt_s))
    y_sample = jnp.transpose(hs.reshape(seq_s, bs, D_MODEL), (1, 0, 2))
    return (hp.reshape(bp, seq_p, D_MODEL), y_sample) + outs_p + outs_s
```
